```python
import math
import jax, jax.numpy as jnp
from jax import lax
import numpy as np

D_MODEL = 2048
BATCH = 2
SEQ = 16384
DEPTH = 2

N_MIXERS = 4
GROUP_W = D_MODEL // N_MIXERS
MIX_W = N_MIXERS * GROUP_W
CHUNK = 64
Q_BLOCK = 128
HG_DK = 128
HG_HEADS = GROUP_W // HG_DK
HG_DV = GROUP_W // HG_HEADS
NSA_HEAD_DIM = 64
NSA_HEADS = GROUP_W // NSA_HEAD_DIM
NSA_KV = 2
NSA_GROUP = NSA_HEADS // NSA_KV
NSA_CMP_BLOCK = 32
NSA_CMP_STRIDE = 16
NSA_SLC_BLOCK = 64
NSA_TOP_N = 16
NSA_WINDOW = 512
NSA_KVW = NSA_KV * NSA_HEAD_DIM
NSA_CMP_HIDDEN = 256
SSM_HEAD_DIM = 64
SSM_HEADS = GROUP_W // SSM_HEAD_DIM
SSM_GROUPS = 2
SSM_STATE = 128
SSM_CONV = 4
SSM_CONV_DIM = GROUP_W + 2 * SSM_GROUPS * SSM_STATE
RET_HEADS = 4
RET_DK = GROUP_W // RET_HEADS
REL_BUCKETS = 32
REL_EXACT = REL_BUCKETS // 2
REL_MAX_DIST = 2048
D_FF = 5632
DEEPNORM_ALPHA = (2 * DEPTH) ** 0.25
DEEPNORM_BETA = (8 * DEPTH) ** -0.25

IN_SIZES = ((GROUP_W,) * 4
            + (GROUP_W,) + (NSA_KVW,) * 6 + (3 * NSA_HEADS,)
            + (GROUP_W, SSM_CONV_DIM, SSM_HEADS)
            + (GROUP_W,) * 4)
D_IN = sum(IN_SIZES)
IN_SPLITS = tuple(int(v) for v in np.cumsum(IN_SIZES)[:-1])

kernel_name = 'hybrid_parallel_groups_hgrn2_nsa_ssd_retention'


def layer_norm(x, g, b, eps=1e-5):
    xf = x.astype(jnp.float32)
    mu = jnp.mean(xf, axis=-1, keepdims=True)
    var = jnp.mean(jnp.square(xf - mu), axis=-1, keepdims=True)
    return ((xf - mu) * lax.rsqrt(var + eps) * g + b).astype(x.dtype)


def rms_normalize(x, eps=1e-6):
    xf = x.astype(jnp.float32)
    return xf * lax.rsqrt(jnp.mean(jnp.square(xf), axis=-1, keepdims=True) + eps)


def swiglu(h, w1, w3, w2):
    return (jax.nn.silu(h @ w1) * (h @ w3)) @ w2


def t5_bucket(dist):
    n = jnp.maximum(dist, 0)
    nf = jnp.maximum(n, 1).astype(jnp.float32)
    large = REL_EXACT + (jnp.log(nf / REL_EXACT) / math.log(REL_MAX_DIST / REL_EXACT)
                         * (REL_BUCKETS - REL_EXACT)).astype(jnp.int32)
    return jnp.where(n < REL_EXACT, n, jnp.minimum(large, REL_BUCKETS - 1))


def masked_softmax(s, mask):
    p = jax.nn.softmax(jnp.where(mask, s.astype(jnp.float32), -1e30), axis=-1)
    return jnp.where(mask, p, 0.0)


def causal_depthwise_conv(x, w, b):
    out = lax.conv_general_dilated(x, w[:, None, :], window_strides=(1,),
                                   padding=[(SSM_CONV - 1, 0)],
                                   dimension_numbers=('NWC', 'WIO', 'NWC'),
                                   feature_group_count=x.shape[-1])
    return out + b


def rotary(t, pos):
    half = t.shape[-1] // 2
    theta = 1.0 / (10000.0 ** jnp.linspace(0.0, 1.0, half, dtype=jnp.float32))
    ang = pos[:, None] * theta[None, :]
    cos, sin = jnp.cos(ang)[None, :, None, :], jnp.sin(ang)[None, :, None, :]
    t1, t2 = t[..., 0::2], t[..., 1::2]
    return jnp.stack([t1 * cos - t2 * sin, t1 * sin + t2 * cos], axis=-1).reshape(t.shape)


def chunked_scalar_decay(q, k, v, log_a):
    B_, S_, H, dk = q.shape
    dv = v.shape[-1]
    nc = S_ // CHUNK
    qc = q.reshape(B_, nc, CHUNK, H, dk)
    kc = k.reshape(B_, nc, CHUNK, H, dk)
    vc = v.reshape(B_, nc, CHUNK, H, dv)
    b = jnp.cumsum(log_a.reshape(B_, nc, CHUNK, H), axis=2)
    bh = b.transpose(0, 1, 3, 2)
    causal = jnp.tril(jnp.ones((CHUNK, CHUNK), bool))
    decay = jnp.exp(jnp.where(causal, bh[..., :, None] - bh[..., None, :], -jnp.inf))
    scores = jnp.einsum('bcihk,bcjhk->bchij', qc, kc) * decay
    y_intra = jnp.einsum('bchij,bcjhv->bcihv', scores, vc)
    b_last = b[:, :, -1]
    chunk_state = jnp.einsum('bcjhk,bcjh,bcjhv->bchkv', kc, jnp.exp(b_last[:, :, None, :] - b), vc)

    def step(state, inp):
        a, r = inp
        return a[..., None, None] * state + r, state

    _, s_prev = lax.scan(step, jnp.zeros((B_, H, dk, dv), jnp.float32),
                         (jnp.exp(b_last).transpose(1, 0, 2), chunk_state.transpose(1, 0, 2, 3, 4)))
    s_prev = s_prev.transpose(1, 0, 2, 3, 4)
    y_inter = jnp.einsum('bcihk,bchkv->bcihv', qc * jnp.exp(b)[..., None], s_prev)
    return (y_intra + y_inter).reshape(B_, S_, H, dv)


def hgrn2_chunked(q, k, v, log_f):
    B_, S_, H, dk = q.shape
    dv = v.shape[-1]
    nc = S_ // CHUNK

    def chunks(t):
        return t.reshape(B_, nc, CHUNK, H, t.shape[-1]).transpose(1, 0, 3, 2, 4)

    qc, kc, vc = chunks(q), chunks(k), chunks(v)
    bc = jnp.cumsum(chunks(log_f), axis=3)
    causal = jnp.tril(jnp.ones((CHUNK, CHUNK), bool))[:, :, None]

    def step(state, inp):
        qi, ki, vi, bi = inp
        dec = jnp.exp(jnp.where(causal, bi[:, :, :, None, :] - bi[:, :, None, :, :], -jnp.inf))
        a = jnp.einsum('bhid,bhjd,bhijd->bhij', qi, ki, dec)
        o = (jnp.einsum('bhij,bhjv->bhiv', a, vi)
             + jnp.einsum('bhid,bhdv->bhiv', qi * jnp.exp(bi), state))
        b_last = bi[:, :, -1:, :]
        state = (jnp.exp(b_last[:, :, 0, :])[..., None] * state
                 + jnp.einsum('bhjd,bhjv->bhdv', ki * jnp.exp(b_last - bi), vi))
        return state, o

    _, o = lax.scan(step, jnp.zeros((B_, H, dk, dv), jnp.float32), (qc, kc, vc, bc))
    return o.transpose(1, 0, 3, 2, 4).reshape(B_, S_, H, dv)


def hgrn2_mixer(q, f_logit, i_in, g, lb, norm_g):
    B_, S_, _ = q.shape
    f32 = jnp.float32
    q = jax.nn.silu(q.astype(f32)).reshape(B_, S_, HG_HEADS, HG_DK)
    z = f_logit.astype(f32).reshape(B_, S_, HG_HEADS, HG_DK)
    lb = lb.astype(f32).reshape(HG_HEADS, HG_DK)
    log_f = jnp.logaddexp(jnp.log(lb), jnp.log1p(-lb) + jax.nn.log_sigmoid(z))
    k = (1.0 - lb) * jax.nn.sigmoid(-z)
    v = i_in.astype(f32).reshape(B_, S_, HG_HEADS, HG_DV)
    o = rms_normalize(hgrn2_chunked(q, k, v, log_f)).reshape(B_, S_, GROUP_W) * norm_g
    return o * jax.nn.silu(g.astype(f32))


def compress_kv(t, pe, w1, w2):
    B_, S_ = t.shape[:2]
    blk = t.reshape(B_, S_ // NSA_CMP_STRIDE, NSA_CMP_STRIDE, NSA_KV, NSA_HEAD_DIM)
    blk = jnp.concatenate([blk[:, :-1], blk[:, 1:]], axis=2) + pe[None, None, :, None, :]
    flat = blk.transpose(0, 1, 3, 2, 4).reshape(B_, blk.shape[1], NSA_KV, NSA_CMP_BLOCK * NSA_HEAD_DIM)
    return jax.nn.silu(flat @ w1) @ w2


def cmp_to_slc(p, n_slc):
    r = NSA_SLC_BLOCK // NSA_CMP_STRIDE
    pad_r = r * n_slc + r - (p.shape[-1] + 1)
    pp = jnp.pad(p, [(0, 0)] * (p.ndim - 1) + [(1, pad_r)])
    lead = p.shape[:-1]
    a = pp[..., : r * n_slc].reshape(*lead, n_slc, r)
    e = pp[..., r: r * n_slc + r].reshape(*lead, n_slc, r)[..., 0]
    return 0.5 * a[..., 0] + a[..., 1:].sum(-1) + 0.5 * e


def nsa_mixer(q, k_c, v_c, k_s, v_s, k_w, v_w, gate, pe_k, w1_k, w2_k, pe_v, w1_v, w2_v,
              rel_bias, norm_g):
    B_, S_, _ = q.shape
    f32 = jnp.float32

    def kvh(t):
        return t.astype(f32).reshape(B_, S_, NSA_KV, NSA_HEAD_DIM)

    k_c, v_c, k_s, v_s, k_w, v_w = kvh(k_c), kvh(v_c), kvh(k_s), kvh(v_s), kvh(k_w), kvh(v_w)
    k_cmp = compress_kv(k_c, pe_k, w1_k, w2_k)
    v_cmp = compress_kv(v_c, pe_v, w1_v, w2_v)
    n_cmp = k_cmp.shape[1]
    n_slc = S_ // NSA_SLC_BLOCK
    n_top = min(NSA_TOP_N, n_slc)
    k_blk = k_s.reshape(B_, n_slc, NSA_SLC_BLOCK, NSA_KV, NSA_HEAD_DIM).transpose(0, 3, 1, 2, 4)
    v_blk = v_s.reshape(B_, n_slc, NSA_SLC_BLOCK, NSA_KV, NSA_HEAD_DIM).transpose(0, 3, 1, 2, 4)
    pad = ((0, 0), (NSA_WINDOW, 0), (0, 0), (0, 0))
    k_wp, v_wp = jnp.pad(k_w, pad), jnp.pad(v_w, pad)
    table = rel_bias.astype(f32)
    table_kv = table.reshape(REL_BUCKETS, NSA_KV, NSA_GROUP).transpose(1, 0, 2)
    cmp_end = jnp.arange(n_cmp) * NSA_CMP_STRIDE + NSA_CMP_BLOCK - 1
    n_qb = S_ // Q_BLOCK
    qs = (q.astype(f32) * NSA_HEAD_DIM ** -0.5).reshape(
        B_, n_qb, Q_BLOCK, NSA_KV, NSA_GROUP, NSA_HEAD_DIM).transpose(1, 0, 2, 3, 4, 5)
    take = jax.vmap(jax.vmap(lambda blk, ix: blk[ix]))

    def head_bias(dist):
        return table[t5_bucket(dist)].transpose(2, 0, 1).reshape(NSA_KV, NSA_GROUP, *dist.shape)

    def block(args):
        bi, qb = args
        q0 = bi * Q_BLOCK
        qpos = q0 + jnp.arange(Q_BLOCK)
        dist_c = qpos[:, None] - cmp_end[None, :]
        s_c = jnp.einsum('bqkgd,bmkd->bkgqm', qb, k_cmp) + head_bias(dist_c)
        p_c = masked_softmax(s_c, dist_c >= 0)
        o_c = jnp.einsum('bkgqm,bmkd->bqkgd', p_c, v_cmp)
        imp = cmp_to_slc(p_c.sum(axis=2), n_slc)
        j = jnp.arange(n_slc)[None, :]
        cur = (qpos // NSA_SLC_BLOCK)[:, None]
        forced = (j == 0) | (j == cur) | (j == cur - 1)
        score = jnp.where(j > cur, -1.0, jnp.where(forced, NSA_GROUP + 1.0, imp))
        _, idx = lax.top_k(score, n_top)
        kg = take(k_blk, idx).reshape(B_, NSA_KV, Q_BLOCK, n_top * NSA_SLC_BLOCK, NSA_HEAD_DIM)
        vg = take(v_blk, idx).reshape(B_, NSA_KV, Q_BLOCK, n_top * NSA_SLC_BLOCK, NSA_HEAD_DIM)
        pos = (idx[..., None] * NSA_SLC_BLOCK + jnp.arange(NSA_SLC_BLOCK)).reshape(
            B_, NSA_KV, Q_BLOCK, n_top * NSA_SLC_BLOCK)
        dist_s = qpos[None, None, :, None] - pos
        bias_s = jax.vmap(lambda t, bk: t[bk], in_axes=(0, 1), out_axes=1)(table_kv, t5_bucket(dist_s))
        s_s = jnp.einsum('bqkgd,bkqnd->bkgqn', qb, kg) + bias_s.transpose(0, 1, 4, 2, 3)
        p_s = masked_softmax(s_s, (dist_s >= 0)[:, :, None])
        o_s = jnp.einsum('bkgqn,bkqnd->bqkgd', p_s, vg)
        kw = lax.dynamic_slice_in_dim(k_wp, q0, Q_BLOCK + NSA_WINDOW, axis=1)
        vw = lax.dynamic_slice_in_dim(v_wp, q0, Q_BLOCK + NSA_WINDOW, axis=1)
        kpos = q0 - NSA_WINDOW + jnp.arange(Q_BLOCK + NSA_WINDOW)
        dist_w = qpos[:, None] - kpos[None, :]
        mask_w = (dist_w >= 0) & (dist_w < NSA_WINDOW) & (kpos[None, :] >= 0)
        s_w = jnp.einsum('bqkgd,bwkd->bkgqw', qb, kw) + head_bias(dist_w)
        p_w = masked_softmax(s_w, mask_w)
        o_w = jnp.einsum('bkgqw,bwkd->bqkgd', p_w, vw)
        return o_c, o_s, o_w

    o_c, o_s, o_w = lax.map(block, (jnp.arange(n_qb), qs))

    def unblock(t):
        return t.transpose(1, 0, 2, 3, 4, 5).reshape(B_, S_, NSA_HEADS, NSA_HEAD_DIM)

    gates = jax.nn.sigmoid(gate.astype(f32)).reshape(B_, S_, NSA_HEADS, 3)
    o = (gates[..., 0:1] * unblock(o_c) + gates[..., 1:2] * unblock(o_s)
         + gates[..., 2:3] * unblock(o_w))
    return rms_normalize(o.reshape(B_, S_, GROUP_W)) * norm_g


def ssd_mixer(z, xbc, dt, conv_w, conv_b, dt_bias, a_log, d_skip, norm_g):
    B_, S_, _ = z.shape
    f32 = jnp.float32
    xbc = jax.nn.silu(causal_depthwise_conv(xbc.astype(f32), conv_w.astype(f32), conv_b.astype(f32)))
    xs, bm, cm = jnp.split(xbc, [GROUP_W, GROUP_W + SSM_GROUPS * SSM_STATE], axis=-1)
    xs = xs.reshape(B_, S_, SSM_HEADS, SSM_HEAD_DIM)
    rep = SSM_HEADS // SSM_GROUPS
    bm = jnp.repeat(bm.reshape(B_, S_, SSM_GROUPS, SSM_STATE), rep, axis=2)
    cm = jnp.repeat(cm.reshape(B_, S_, SSM_GROUPS, SSM_STATE), rep, axis=2)
    dt = jax.nn.softplus(dt.astype(f32) + dt_bias)
    log_a = dt * (-jnp.exp(a_log.astype(f32)))
    y = chunked_scalar_decay(cm, bm, xs * dt[..., None], log_a) + d_skip[:, None] * xs
    y = y.reshape(B_, S_, GROUP_W) * jax.nn.silu(z.astype(f32))
    y = rms_normalize(y.reshape(B_, S_, SSM_GROUPS, GROUP_W // SSM_GROUPS)).reshape(B_, S_, GROUP_W)
    return y * norm_g


def retention_mixer(q, k, v, g):
    B_, S_, _ = q.shape
    f32 = jnp.float32
    pos = jnp.arange(S_, dtype=f32)
    q = rotary(q.astype(f32).reshape(B_, S_, RET_HEADS, RET_DK), pos)
    k = rotary(k.astype(f32).reshape(B_, S_, RET_HEADS, RET_DK), pos) * RET_DK ** -0.5
    v = v.astype(f32).reshape(B_, S_, RET_HEADS, RET_DK)
    log_gamma = jnp.log(1.0 - 2.0 ** (-5.0 - jnp.arange(RET_HEADS, dtype=f32)))
    y = chunked_scalar_decay(q, k, v, jnp.broadcast_to(log_gamma, (B_, S_, RET_HEADS)))
    mu = jnp.mean(y, axis=-1, keepdims=True)
    y = (y - mu) * lax.rsqrt(jnp.mean(jnp.square(y - mu), axis=-1, keepdims=True) + 1e-5)
    return jax.nn.silu(g.astype(f32)) * y.reshape(B_, S_, GROUP_W)


def hybrid_mixer(h, w_in, w_out, lb, hg_norm_g, pe_k, w1_k, w2_k, pe_v, w1_v, w2_v, nsa_norm_g,
                 rel_bias, conv_w, conv_b, dt_bias, a_log, d_skip, ssm_norm_g):
    (hq, hf, hi, hg, nq, nkc, nvc, nks, nvs, nkw, nvw, ngate,
     sz, sxbc, sdt, rq, rk, rv, rg) = jnp.split(h @ w_in, IN_SPLITS, axis=-1)
    o_a = hgrn2_mixer(hq, hf, hi, hg, lb, hg_norm_g)
    o_b = nsa_mixer(nq, nkc, nvc, nks, nvs, nkw, nvw, ngate, pe_k, w1_k, w2_k, pe_v, w1_v, w2_v,
                    rel_bias, nsa_norm_g)
    o_c = ssd_mixer(sz, sxbc, sdt, conv_w, conv_b, dt_bias, a_log, d_skip, ssm_norm_g)
    o_d = retention_mixer(rq, rk, rv, rg)
    mixed = jnp.concatenate([o_a, o_b, o_c, o_d], axis=-1)
    return (mixed.astype(w_out.dtype) @ w_out).astype(h.dtype)


def setup_inputs(seed: int = 0) -> dict:
    key = jax.random.key(seed)
    keys = iter(jax.random.split(key, 40))
    L = DEPTH

    def normal(shape, scale):
        return jax.random.normal(next(keys), shape, jnp.float32) * scale

    def gain(shape):
        return 1.0 + normal(shape, 0.02)

    w_in_s = D_MODEL ** -0.5
    ffn_out_s = D_FF ** -0.5 * DEEPNORM_BETA
    return {
        'x': normal((BATCH, SEQ, D_MODEL), 1.0),
        'ln1_g': gain((L, D_MODEL)),
        'ln1_b': normal((L, D_MODEL), 0.02),
        'ffn1_w1': normal((L, D_MODEL, D_FF), w_in_s),
        'ffn1_w3': normal((L, D_MODEL, D_FF), w_in_s),
        'ffn1_w2': normal((L, D_FF, D_MODEL), ffn_out_s),
        'ln2_g': gain((L, D_MODEL)),
        'ln2_b': normal((L, D_MODEL), 0.02),
        'w_in': normal((L, D_MODEL, D_IN), w_in_s),
        'w_out': normal((L, MIX_W, D_MODEL), MIX_W ** -0.5 * DEEPNORM_BETA),
        'hgrn_lb_logits': normal((L, HG_HEADS * HG_DK), 0.5),
        'hgrn_norm_g': gain((L, GROUP_W)),
        'nsa_pe_k': normal((L, NSA_CMP_BLOCK, NSA_HEAD_DIM), 0.1),
        'nsa_w1_k': normal((L, NSA_CMP_BLOCK * NSA_HEAD_DIM, NSA_CMP_HIDDEN), (NSA_CMP_BLOCK * NSA_HEAD_DIM) ** -0.5),
        'nsa_w2_k': normal((L, NSA_CMP_HIDDEN, NSA_HEAD_DIM), NSA_CMP_HIDDEN ** -0.5),
        'nsa_pe_v': normal((L, NSA_CMP_BLOCK, NSA_HEAD_DIM), 0.1),
        'nsa_w1_v': normal((L, NSA_CMP_BLOCK * NSA_HEAD_DIM, NSA_CMP_HIDDEN), (NSA_CMP_BLOCK * NSA_HEAD_DIM) ** -0.5),
        'nsa_w2_v': normal((L, NSA_CMP_HIDDEN, NSA_HEAD_DIM), NSA_CMP_HIDDEN ** -0.5),
        'nsa_norm_g': gain((L, GROUP_W)),
        'rel_bias': normal((REL_BUCKETS, NSA_HEADS), 0.2),
        'ssm_conv_w': normal((L, SSM_CONV, SSM_CONV_DIM), 0.5),
        'ssm_conv_b': normal((L, SSM_CONV_DIM), 0.02),
        'ssm_dt_bias': (lambda dt: dt + jnp.log(-jnp.expm1(-dt)))(jnp.exp(jax.random.uniform(
            next(keys), (L, SSM_HEADS), jnp.float32, math.log(1e-3), math.log(1e-1)))),
        'ssm_a_log': jnp.log(jax.random.uniform(next(keys), (L, SSM_HEADS), jnp.float32, 1.0, 16.0)),
        'ssm_d': gain((L, SSM_HEADS)),
        'ssm_norm_g': gain((L, GROUP_W)),
        'ln3_g': gain((L, D_MODEL)),
        'ln3_b': normal((L, D_MODEL), 0.02),
        'ffn2_w1': normal((L, D_MODEL, D_FF), w_in_s),
        'ffn2_w3': normal((L, D_MODEL, D_FF), w_in_s),
        'ffn2_w2': normal((L, D_FF, D_MODEL), ffn_out_s),
    }


def reference(x, ln1_g, ln1_b, ffn1_w1, ffn1_w3, ffn1_w2, ln2_g, ln2_b, w_in, w_out,
              hgrn_lb_logits, hgrn_norm_g, nsa_pe_k, nsa_w1_k, nsa_w2_k, nsa_pe_v, nsa_w1_v,
              nsa_w2_v, nsa_norm_g, rel_bias, ssm_conv_w, ssm_conv_b, ssm_dt_bias, ssm_a_log,
              ssm_d, ssm_norm_g, ln3_g, ln3_b, ffn2_w1, ffn2_w3, ffn2_w2):
    cum = jnp.cumsum(jax.nn.softmax(hgrn_lb_logits.astype(jnp.float32), axis=0), axis=0)
    lower_bounds = cum - cum[:1]
    for l in range(DEPTH):
        x = layer_norm(DEEPNORM_ALPHA * x + 0.5 * swiglu(x, ffn1_w1[l], ffn1_w3[l], ffn1_w2[l]),
                       ln1_g[l], ln1_b[l])
        mix = hybrid_mixer(x, w_in[l], w_out[l], lower_bounds[l], hgrn_norm_g[l],
                           nsa_pe_k[l], nsa_w1_k[l], nsa_w2_k[l], nsa_pe_v[l], nsa_w1_v[l],
                           nsa_w2_v[l], nsa_norm_g[l], rel_bias, ssm_conv_w[l], ssm_conv_b[l],
                           ssm_dt_bias[l], ssm_a_log[l], ssm_d[l], ssm_norm_g[l])
        x = layer_norm(DEEPNORM_ALPHA * x + mix, ln2_g[l], ln2_b[l])
        x = layer_norm(DEEPNORM_ALPHA * x + 0.5 * swiglu(x, ffn2_w1[l], ffn2_w3[l], ffn2_w2[l]),
                       ln3_g[l], ln3_b[l])
    return x
```

```python
import functools
import math

import numpy as np
import jax
import jax.numpy as jnp
from jax import lax
from jax.experimental import pallas as pl
from jax.experimental.pallas import tpu as pltpu

F32 = jnp.float32
BF16 = jnp.bfloat16
HIGHEST = lax.Precision.HIGHEST

D_MODEL = 2048
DEPTH = 2
GROUP_W = 512
ALPHA = (2 * DEPTH) ** 0.25
HG_HEADS = 4
NSA_HEADS = 8
NSA_KV = 2
NSA_GROUP = 4
NSA_HEAD_DIM = 64
NSA_CMP_STRIDE = 16
NSA_CMP_BLOCK = 32
NSA_SLC_BLOCK = 64
NSA_TOP_N = 16
NSA_WINDOW = 512
NSA_CMP_HIDDEN = 256
SSM_HEADS = 8
SSM_HEAD_DIM = 64
SSM_GROUPS = 2
SSM_STATE = 128
SSM_CONV = 4
RET_HEADS = 4
RET_DK = 128
REL_BUCKETS = 32
REL_EXACT = 16
REL_MAX_DIST = 2048
IN_SIZES = ((GROUP_W,) * 4 + (GROUP_W,) + (128,) * 6 + (24,)
            + (GROUP_W, 1024, SSM_HEADS) + (GROUP_W,) * 4)
IN_SPLITS = tuple(int(v) for v in np.cumsum(IN_SIZES)[:-1])

LANES = 128
VMEM_LIMIT = 56 * 1024 * 1024

COL = dict(hq=0, hf=4, hi=8, hg=12, nqw=16, sxbc=24, sz=32, rq=36, rk=40, rv=44, rg=48,
           nkc=52, nvc=53, nks=54, nvs=55, nkw=56, nvw=57, small=58)
NCOL = 60
GATE_LANE0 = 0
DT_LANE0 = 24

CHUNK = 128
TQ = 64
TK = 1024
WIN_SPAN = NSA_WINDOW + 2 * TQ
NEG = -1e30


def _params(sem):
    return pltpu.CompilerParams(dimension_semantics=sem, vmem_limit_bytes=VMEM_LIMIT)


def _dot(a, b):
    return jnp.dot(a, b, preferred_element_type=F32)


def _dot_nt(a, b):
    return lax.dot_general(a, b, (((1,), (1,)), ((), ())), preferred_element_type=F32)


def _dot_exact(a, b):
    return jnp.dot(a, b, precision=HIGHEST, preferred_element_type=F32)


def _sigmoid(x):
    return 1.0 / (1.0 + jnp.exp(-x))


def _silu(x):
    return x * _sigmoid(x)


def _softplus(x):
    return jnp.maximum(x, 0.0) + jnp.log1p(jnp.exp(-jnp.abs(x)))


def _layer_norm(r, g, b):
    mu = jnp.mean(r, axis=-1, keepdims=True)
    d = r - mu
    var = jnp.mean(d * d, axis=-1, keepdims=True)
    return d * lax.rsqrt(var + 1e-5) * g + b


def _ffn_kernel(x_ref, w1_ref, w3_ref, w2_ref, g_ref, b_ref, o_ref, acc_ref, xb_ref):
    j = pl.program_id(1)

    @pl.when(j == 0)
    def _():
        xb_ref[...] = x_ref[...].astype(BF16)
        acc_ref[...] = jnp.zeros_like(acc_ref)

    xb = xb_ref[...]
    h1 = _dot(xb, w1_ref[...])
    h3 = _dot(xb, w3_ref[...])
    a = (_silu(h1) * h3).astype(BF16)
    acc_ref[...] += _dot(a, w2_ref[...])

    @pl.when(j == pl.num_programs(1) - 1)
    def _():
        r = ALPHA * x_ref[...] + 0.5 * acc_ref[...]
        o_ref[...] = _layer_norm(r, g_ref[...], b_ref[...])


def _ffn(x, w1, w3, w2, g, b, tm, tf):
    T, D = x.shape
    F = w1.shape[1]
    return pl.pallas_call(
        _ffn_kernel,
        grid=(T // tm, F // tf),
        in_specs=[
            pl.BlockSpec((tm, D), lambda i, j: (i, 0)),
            pl.BlockSpec((D, tf), lambda i, j: (0, j)),
            pl.BlockSpec((D, tf), lambda i, j: (0, j)),
            pl.BlockSpec((tf, D), lambda i, j: (j, 0)),
            pl.BlockSpec((1, D), lambda i, j: (0, 0)),
            pl.BlockSpec((1, D), lambda i, j: (0, 0)),
        ],
        out_specs=pl.BlockSpec((tm, D), lambda i, j: (i, 0)),
        out_shape=jax.ShapeDtypeStruct((T, D), F32),
        scratch_shapes=[pltpu.VMEM((tm, D), F32), pltpu.VMEM((tm, D), BF16)],
        compiler_params=_params(("parallel", "arbitrary")),
        name="ffn_ln",
    )(x, w1, w3, w2, g, b)


def _proj_kernel(x_ref, w_ref, o_ref, xb_ref):
    @pl.when(pl.program_id(1) == 0)
    def _():
        xb_ref[...] = x_ref[...].astype(BF16)

    o_ref[...] = _dot(xb_ref[...], w_ref[...])


def _proj(x, w, tm, tn):
    T, D = x.shape
    N = w.shape[1]
    return pl.pallas_call(
        _proj_kernel,
        grid=(T // tm, N // tn),
        in_specs=[pl.BlockSpec((tm, D), lambda i, j: (i, 0)),
                  pl.BlockSpec((D, tn), lambda i, j: (0, j))],
        out_specs=pl.BlockSpec((tm, tn), lambda i, j: (i, j)),
        out_shape=jax.ShapeDtypeStruct((T, N), F32),
        scratch_shapes=[pltpu.VMEM((tm, D), BF16)],
        compiler_params=_params(("parallel", "arbitrary")),
        name="in_proj",
    )(x, w)


def _outproj_kernel(x_ref, oa_ref, ob_ref, oc_ref, od_ref, wa_ref, wb_ref, wc_ref, wd_ref,
                    g_ref, b_ref, o_ref):
    mix = (_dot(oa_ref[...], wa_ref[...]) + _dot(ob_ref[...], wb_ref[...])
           + _dot(oc_ref[...], wc_ref[...]) + _dot(od_ref[...], wd_ref[...]))
    o_ref[...] = _layer_norm(ALPHA * x_ref[...] + mix, g_ref[...], b_ref[...])


def _outproj(x, oa, ob, oc, od, wa, wb, wc, wd, g, b, tm):
    T, D = x.shape
    row = lambda a: pl.BlockSpec((tm, a.shape[1]), lambda i: (i, 0))
    full = lambda a: pl.BlockSpec(a.shape, lambda i: (0, 0))
    return pl.pallas_call(
        _outproj_kernel,
        grid=(T // tm,),
        in_specs=[row(x), row(oa), row(ob), row(oc), row(od),
                  full(wa), full(wb), full(wc), full(wd), full(g), full(b)],
        out_specs=row(x),
        out_shape=jax.ShapeDtypeStruct((T, D), F32),
        compiler_params=_params(("parallel",)),
        name="out_proj_ln",
    )(x, oa, ob, oc, od, wa, wb, wc, wd, g, b)


def _hgrn_tables(C):
    i = np.arange(C)[:, None]
    ip = np.arange(C)[None, :]
    mq = [(ip <= i)]
    mk = [(ip > i)]
    masks = [np.eye(C, dtype=bool)]
    s = C // 2
    while s >= 1:
        blk = i // s
        mq.append((ip > blk * s) & (ip <= i))
        mk.append((ip > i) & (ip <= np.minimum((blk + 1) * s, C - 1)))
        blk_j = (np.arange(C)[None, :]) // s
        masks.append((blk % 2 == 1) & (blk_j == blk - 1))
        s //= 2
    f = lambda xs: np.concatenate([x.astype(np.float32) for x in xs], axis=0)
    return f(mq), f(mk), np.stack([m.astype(np.float32) for m in masks])


def _hgrn_kernel(q_ref, f_ref, i_ref, g_ref, llb_ref, l1m_ref, oml_ref, ng_ref,
                 mq_ref, mk_ref, msk_ref, o_ref, st_ref):
    @pl.when(pl.program_id(1) == 0)
    def _():
        st_ref[...] = jnp.zeros_like(st_ref)

    C = q_ref.shape[0]
    nlev = msk_ref.shape[0] - 1
    q = _silu(q_ref[...])
    z = f_ref[...]
    log_sig = jnp.minimum(z, 0.0) - jnp.log1p(jnp.exp(-jnp.abs(z)))
    cc = l1m_ref[...] + log_sig
    llb = llb_ref[...]
    logf = jnp.maximum(llb, cc) + jnp.log1p(jnp.exp(-jnp.abs(llb - cc)))
    k = oml_ref[...] * _sigmoid(-z)
    v = i_ref[...]
    eq = _dot_exact(mq_ref[...], logf)
    ek = _dot_exact(mk_ref[...], logf)
    outs = []
    for h in range(HG_HEADS):
        sl = slice(h * LANES, (h + 1) * LANES)
        qh, kh, vh = q[:, sl], k[:, sl], v[:, sl]
        a = msk_ref[0] * _dot_nt(qh.astype(BF16), kh.astype(BF16))
        for l in range(nlev):
            rows = slice((1 + l) * C, (2 + l) * C)
            qs = (qh * jnp.exp(eq[rows, sl])).astype(BF16)
            ks = (kh * jnp.exp(ek[rows, sl])).astype(BF16)
            a = a + msk_ref[1 + l] * _dot_nt(qs, ks)
        b = eq[0:C, sl]
        st = st_ref[h]
        o = _dot(a.astype(BF16), vh.astype(BF16))
        o = o + _dot_nt((qh * jnp.exp(b)).astype(BF16), st.astype(BF16))
        kd = (kh * jnp.exp(ek[0:C, sl])).astype(BF16)
        st_ref[h] = st * jnp.exp(b[C - 1:C, :]) + _dot(vh.T.astype(BF16), kd)
        outs.append(o * lax.rsqrt(jnp.mean(o * o, axis=-1, keepdims=True) + 1e-6))
    o = jnp.concatenate(outs, axis=1)
    o_ref[...] = (o * ng_ref[...] * _silu(g_ref[...])).astype(o_ref.dtype)


def _hgrn(proj, B, S, llb, l1m, oml, ng):
    C = CHUNK
    nc = S // C
    mq, mk, msk = (jnp.asarray(t) for t in _hgrn_tables(C))
    col = lambda name: pl.BlockSpec((C, GROUP_W), lambda b, c, n=COL[name] // 4: (b * nc + c, n))
    vec = pl.BlockSpec((1, GROUP_W), lambda b, c: (0, 0))
    full2 = lambda a: pl.BlockSpec(a.shape, lambda b, c: (0, 0))
    return pl.pallas_call(
        _hgrn_kernel,
        grid=(B, nc),
        in_specs=[col("hq"), col("hf"), col("hi"), col("hg"), vec, vec, vec, vec,
                  full2(mq), full2(mk), pl.BlockSpec(msk.shape, lambda b, c: (0, 0, 0))],
        out_specs=pl.BlockSpec((C, GROUP_W), lambda b, c: (b * nc + c, 0)),
        out_shape=jax.ShapeDtypeStruct((B * S, GROUP_W), BF16),
        scratch_shapes=[pltpu.VMEM((HG_HEADS, LANES, LANES), F32)],
        compiler_params=_params(("parallel", "arbitrary")),
        name="hgrn2",
    )(proj, proj, proj, proj, llb, l1m, oml, ng, mq, mk, msk)


def _ssd_kernel(z_ref, xbc_ref, sm_ref, cw_ref, cb_ref, dtb_ref, aneg_ref, dsk_ref, ng_ref,
                ex_ref, o_ref, tail_ref, st_ref):
    @pl.when(pl.program_id(1) == 0)
    def _():
        tail_ref[...] = jnp.zeros_like(tail_ref)
        st_ref[...] = jnp.zeros_like(st_ref)

    L = xbc_ref.shape[0]
    x = xbc_ref[...]
    xe = jnp.concatenate([tail_ref[...], x], axis=0)
    cw = cw_ref[...]
    conv = cb_ref[...]
    for kk in range(SSM_CONV):
        conv = conv + cw[kk:kk + 1, :] * xe[5 + kk:5 + kk + L, :]
    tail_ref[...] = x[L - 8:L, :]
    conv = _silu(conv)
    xs = conv[:, 0:GROUP_W]
    bm = conv[:, GROUP_W:GROUP_W + 256]
    cm = conv[:, GROUP_W + 256:GROUP_W + 512]

    dtf = _softplus(sm_ref[...] + dtb_ref[...])
    la = dtf * aneg_ref[...]
    ri = lax.broadcasted_iota(jnp.int32, (L, L), 0)
    ci = lax.broadcasted_iota(jnp.int32, (L, L), 1)
    tri = ri >= ci
    bfull = _dot_exact(tri.astype(F32), la)
    ex = ex_ref[...]
    bexp = _dot_exact(bfull, ex)
    dtexp = _dot_exact(dtf, ex)
    b_t = bfull.T
    xdt = xs * dtexp
    lane = lax.broadcasted_iota(jnp.int32, (L, LANES), 1)

    scores = []
    for g in range(SSM_GROUPS):
        cg = cm[:, g * SSM_STATE:(g + 1) * SSM_STATE].astype(BF16)
        bg = bm[:, g * SSM_STATE:(g + 1) * SSM_STATE].astype(BF16)
        cb = _dot_nt(cg, bg)
        for hh in range(SSM_HEADS // SSM_GROUPS):
            h = g * (SSM_HEADS // SSM_GROUPS) + hh
            bcol = bfull[:, DT_LANE0 + h:DT_LANE0 + h + 1]
            brow = b_t[DT_LANE0 + h:DT_LANE0 + h + 1, :]
            dec = jnp.exp(jnp.where(tri, bcol - brow, NEG))
            scores.append((cb * dec).astype(BF16))
    y_pairs = []
    for u in range(SSM_HEADS // 2):
        slab = xdt[:, u * LANES:(u + 1) * LANES]
        lo = jnp.where(lane < SSM_HEAD_DIM, slab, 0.0).astype(BF16)
        hi = jnp.where(lane >= SSM_HEAD_DIM, slab, 0.0).astype(BF16)
        y_pairs.append(_dot(scores[2 * u], lo) + _dot(scores[2 * u + 1], hi))
    y_intra = jnp.concatenate(y_pairs, axis=1)

    blast = bexp[L - 1:L, :]
    w = (xdt * jnp.exp(blast - bexp)).astype(BF16)
    y_inter = []
    for g in range(SSM_GROUPS):
        gs = slice(g * 256, (g + 1) * 256)
        cg = cm[:, g * SSM_STATE:(g + 1) * SSM_STATE].astype(BF16)
        st = st_ref[g]
        y_inter.append(_dot(cg, st.astype(BF16)))
        bg_t = bm[:, g * SSM_STATE:(g + 1) * SSM_STATE].T.astype(BF16)
        st_ref[g] = st * jnp.exp(blast[:, gs]) + _dot(bg_t, w[:, gs])
    y = y_intra + jnp.concatenate(y_inter, axis=1) * jnp.exp(bexp) + dsk_ref[...] * xs
    y = y * _silu(z_ref[...])
    halves = []
    for g in range(SSM_GROUPS):
        seg = y[:, g * 256:(g + 1) * 256]
        halves.append(seg * lax.rsqrt(jnp.mean(seg * seg, axis=-1, keepdims=True) + 1e-6))
    o_ref[...] = (jnp.concatenate(halves, axis=1) * ng_ref[...]).astype(o_ref.dtype)


def _ssd(proj, B, S, cw, cb, dtb, aneg, dsk, ng):
    L = CHUNK
    nc = S // L
    ex = np.zeros((LANES, GROUP_W), np.float32)
    for h in range(SSM_HEADS):
        ex[DT_LANE0 + h, h * SSM_HEAD_DIM:(h + 1) * SSM_HEAD_DIM] = 1.0
    ex = jnp.asarray(ex)
    full2 = lambda a: pl.BlockSpec(a.shape, lambda b, c: (0, 0))
    return pl.pallas_call(
        _ssd_kernel,
        grid=(B, nc),
        in_specs=[
            pl.BlockSpec((L, GROUP_W), lambda b, c: (b * nc + c, COL["sz"] // 4)),
            pl.BlockSpec((L, 1024), lambda b, c: (b * nc + c, COL["sxbc"] // 8)),
            pl.BlockSpec((L, LANES), lambda b, c: (b * nc + c, COL["small"])),
            full2(cw), full2(cb), full2(dtb), full2(aneg), full2(dsk), full2(ng), full2(ex)],
        out_specs=pl.BlockSpec((L, GROUP_W), lambda b, c: (b * nc + c, 0)),
        out_shape=jax.ShapeDtypeStruct((B * S, GROUP_W), BF16),
        scratch_shapes=[pltpu.VMEM((8, 1024), F32), pltpu.VMEM((SSM_GROUPS, SSM_STATE, 256), F32)],
        compiler_params=_params(("parallel", "arbitrary")),
        name="ssd",
    )(proj, proj, proj, cw, cb, dtb, aneg, dsk, ng, ex)


def _ret_kernel(q_ref, k_ref, v_ref, g_ref, cos_ref, sin_ref, dec_ref, qs_ref, ks_ref, sd_ref,
                o_ref, st_ref):
    @pl.when(pl.program_id(1) == 0)
    def _():
        st_ref[...] = jnp.zeros_like(st_ref)

    cos = cos_ref[...]
    sin = sin_ref[...]
    outs = []
    for h in range(RET_HEADS):
        sl = slice(h * LANES, (h + 1) * LANES)
        qh = q_ref[:, sl]
        kh = k_ref[:, sl]
        qh = qh * cos + pltpu.roll(qh, RET_DK // 2, axis=1) * sin
        kh = (kh * cos + pltpu.roll(kh, RET_DK // 2, axis=1) * sin) * (RET_DK ** -0.5)
        vh = v_ref[:, sl].astype(BF16)
        sc = (_dot_nt(qh.astype(BF16), kh.astype(BF16)) * dec_ref[h]).astype(BF16)
        st = st_ref[h]
        y = _dot(sc, vh) + _dot((qh * qs_ref[:, sl]).astype(BF16), st.astype(BF16))
        kd_t = (kh * ks_ref[:, sl]).T.astype(BF16)
        st_ref[h] = st * sd_ref[h] + _dot(kd_t, vh)
        mu = jnp.mean(y, axis=-1, keepdims=True)
        d = y - mu
        outs.append(d * lax.rsqrt(jnp.mean(d * d, axis=-1, keepdims=True) + 1e-5))
    o_ref[...] = (_silu(g_ref[...]) * jnp.concatenate(outs, axis=1)).astype(o_ref.dtype)


def _retention(proj, B, S, cos_t, sin_t):
    L = CHUNK
    nc = S // L
    lg = jnp.log(1.0 - 2.0 ** (-5.0 - jnp.arange(RET_HEADS, dtype=F32)))
    i = jnp.arange(L, dtype=F32)
    diff = i[:, None] - i[None, :]
    dec = jnp.where(diff >= 0, jnp.exp(lg[:, None, None] * jnp.maximum(diff, 0.0)), 0.0)
    rep = lambda t: jnp.repeat(t, LANES, axis=1)
    qs = rep(jnp.exp((i[:, None] + 1.0) * lg[None, :]))
    ks = rep(jnp.exp((L - 1.0 - i[:, None]) * lg[None, :]))
    sd = jnp.broadcast_to(jnp.exp(L * lg)[:, None, None], (RET_HEADS, LANES, LANES))
    col = lambda name: pl.BlockSpec((L, GROUP_W), lambda b, c, n=COL[name] // 4: (b * nc + c, n))
    return pl.pallas_call(
        _ret_kernel,
        grid=(B, nc),
        in_specs=[col("rq"), col("rk"), col("rv"), col("rg"),
                  pl.BlockSpec((L, LANES), lambda b, c: (c, 0)),
                  pl.BlockSpec((L, LANES), lambda b, c: (c, 0)),
                  pl.BlockSpec((RET_HEADS, L, L), lambda b, c: (0, 0, 0)),
                  pl.BlockSpec((L, GROUP_W), lambda b, c: (0, 0)),
                  pl.BlockSpec((L, GROUP_W), lambda b, c: (0, 0)),
                  pl.BlockSpec((RET_HEADS, LANES, LANES), lambda b, c: (0, 0, 0))],
        out_specs=pl.BlockSpec((L, GROUP_W), lambda b, c: (b * nc + c, 0)),
        out_shape=jax.ShapeDtypeStruct((B * S, GROUP_W), BF16),
        scratch_shapes=[pltpu.VMEM((RET_HEADS, RET_DK, RET_DK), F32)],
        compiler_params=_params(("parallel", "arbitrary")),
        name="retention",
    )(proj, proj, proj, proj, cos_t, sin_t, dec, qs, ks, sd)


def _t5_bucket(dist):
    n = jnp.maximum(dist, 0)
    nf = jnp.maximum(n, 1).astype(F32)
    large = REL_EXACT + (jnp.log(nf / REL_EXACT) / math.log(REL_MAX_DIST / REL_EXACT)
                         * (REL_BUCKETS - REL_EXACT)).astype(jnp.int32)
    return jnp.where(n < REL_EXACT, n, jnp.minimum(large, REL_BUCKETS - 1))


def _head_bias(bucket, rel_ref):
    rows, cols = bucket.shape
    per_head = []
    for h in range(NSA_HEADS):
        tbl = jnp.broadcast_to(rel_ref[h:h + 1, :], (rows, LANES))
        chunks = [jnp.take_along_axis(tbl, bucket[:, c:c + LANES], axis=1)
                  for c in range(0, cols, LANES)]
        per_head.append(chunks[0] if len(chunks) == 1 else jnp.concatenate(chunks, axis=1))
    return jnp.stack(per_head, axis=0)


def _stack_heads(qw):
    return jnp.concatenate([qw[:, h * LANES:(h + 1) * LANES] for h in range(NSA_HEADS)],
                           axis=0).astype(BF16)


def _cmp_kernel(g_ref, pe_ref, w1a_ref, w1b_ref, w2_ref, o_ref):
    nb = g_ref.shape[0]
    gw = g_ref.shape[1] // 4
    pe = pe_ref[...]
    slabs = [g_ref[:, s * gw:(s + 1) * gw] for s in range(4)]
    nxt0 = pltpu.roll(slabs[0], nb - 1, axis=0)
    for s in range(4):
        a = (slabs[s] + pe[0:1, :]).astype(BF16)
        bn = ((slabs[s + 1] if s < 3 else nxt0) + pe[1:2, :]).astype(BF16)
        hid = _silu(_dot(a, w1a_ref[...]) + _dot(bn, w1b_ref[...]))
        o_ref[s * nb:(s + 1) * nb, :] = _dot(hid.astype(BF16), w2_ref[...]).astype(o_ref.dtype)


def _compress(g, pe2, w1a, w1b, w2bd):
    B, nb, gw4 = g.shape
    full2 = lambda a: pl.BlockSpec(a.shape, lambda b: (0, 0))
    return pl.pallas_call(
        _cmp_kernel,
        grid=(B,),
        in_specs=[pl.BlockSpec((None, nb, gw4), lambda b: (b, 0, 0)),
                  full2(pe2), full2(w1a), full2(w1b), full2(w2bd)],
        out_specs=pl.BlockSpec((None, 4 * nb, LANES), lambda b: (b, 0, 0)),
        out_shape=jax.ShapeDtypeStruct((B, 4 * nb, LANES), BF16),
        compiler_params=_params(("parallel",)),
        name="nsa_compress",
    )(g, pe2, w1a, w1b, w2bd)


def _cmpattn_kernel(q_ref, kc_ref, vc_ref, rel_ref, oc_ref, selb_ref):
    tq = q_ref.shape[0]
    ncmp = kc_ref.shape[0]
    nb = ncmp // 4
    R = NSA_HEADS * tq
    q0 = pl.program_id(1) * tq
    Q = _stack_heads(q_ref[...])
    s = _dot_nt(Q, kc_ref[...])
    row = lax.broadcasted_iota(jnp.int32, (tq, ncmp), 0)
    col = lax.broadcasted_iota(jnp.int32, (tq, ncmp), 1)
    m = (col % nb) * 4 + col // nb
    dist = (q0 + row) - (m * NSA_CMP_STRIDE + NSA_CMP_BLOCK - 1)
    allowed = (dist >= 0)[None]
    s3 = s.reshape(NSA_HEADS, tq, ncmp) + _head_bias(_t5_bucket(dist), rel_ref)
    s3 = jnp.where(allowed, s3, NEG)
    mx = jnp.max(s3, axis=-1, keepdims=True)
    e = jnp.exp(s3 - mx)
    p = jnp.where(allowed, e / jnp.sum(e, axis=-1, keepdims=True), 0.0)
    oc_ref[...] = _dot(p.reshape(R, ncmp).astype(BF16), vc_ref[...])

    ps = p.reshape(NSA_KV, NSA_GROUP, tq, ncmp).sum(axis=1).reshape(NSA_KV * tq, ncmp)
    p0, p1, p2, p3 = (ps[:, i * nb:(i + 1) * nb] for i in range(4))
    j = lax.broadcasted_iota(jnp.int32, (NSA_KV * tq, nb), 1)
    prev3 = jnp.where(j == 0, 0.0, pltpu.roll(p3, 1, axis=1))
    imp = p0 + p1 + p2 + 0.5 * (p3 + prev3)
    t = q0 + (lax.broadcasted_iota(jnp.int32, (NSA_KV * tq, nb), 0) % tq)
    cur = t // NSA_SLC_BLOCK
    forced = (j == 0) | (j == cur) | (j == cur - 1)
    score = jnp.where(j > cur, -1.0, jnp.where(forced, NSA_GROUP + 1.0, imp))
    jf = j.astype(F32)
    sel = jnp.zeros(score.shape, jnp.bool_)
    for _ in range(min(NSA_TOP_N, nb)):
        best = jnp.max(score, axis=-1, keepdims=True)
        first = jnp.min(jnp.where(score == best, jf, float(nb)), axis=-1, keepdims=True)
        hit = jf == first
        sel = sel | hit
        score = jnp.where(hit, -jnp.inf, score)
    selb = jnp.where(sel, 0.0, NEG).astype(selb_ref.dtype)
    selb_ref[:, 0:nb] = selb[0:tq]
    selb_ref[:, nb:2 * nb] = selb[tq:2 * tq]


def _cmpattn(proj, kcmp, vcmp, rel_t, B, S):
    tq = TQ
    nqt = S // tq
    ncmp = kcmp.shape[1]
    nb = ncmp // 4
    return pl.pallas_call(
        _cmpattn_kernel,
        grid=(B, nqt),
        in_specs=[pl.BlockSpec((tq, 1024), lambda b, i: (b * nqt + i, COL["nqw"] // 8)),
                  pl.BlockSpec((None, ncmp, LANES), lambda b, i: (b, 0, 0)),
                  pl.BlockSpec((None, ncmp, LANES), lambda b, i: (b, 0, 0)),
                  pl.BlockSpec(rel_t.shape, lambda b, i: (0, 0))],
        out_specs=[pl.BlockSpec((NSA_HEADS * tq, LANES), lambda b, i: (b * nqt + i, 0)),
                   pl.BlockSpec((tq, 2 * nb), lambda b, i: (b * nqt + i, 0))],
        out_shape=[jax.ShapeDtypeStruct((B * S * NSA_HEADS, LANES), F32),
                   jax.ShapeDtypeStruct((B * S, 2 * nb), BF16)],
        compiler_params=_params(("parallel", "parallel")),
        name="nsa_cmp_attn_topk",
    )(proj, kcmp, vcmp, rel_t)


def _selattn_kernel(q_ref, selb_ref, ks_ref, vs_ref, rel_ref, eb_ref, os_ref, m_ref, l_ref, acc_ref):
    tq = q_ref.shape[0]
    R = NSA_HEADS * tq
    nb = selb_ref.shape[1] // 2
    tk = eb_ref.shape[1]
    bpt = tk // NSA_SLC_BLOCK
    q0 = pl.program_id(1) * tq
    Q = _stack_heads(q_ref[...])
    selb = [selb_ref[:, 0:nb], selb_ref[:, nb:2 * nb]]
    m_ref[...] = jnp.full(m_ref.shape, NEG, F32)
    l_ref[...] = jnp.zeros_like(l_ref)
    acc_ref[...] = jnp.zeros_like(acc_ref)
    far_bias = jnp.stack([jnp.broadcast_to(rel_ref[h:h + 1, REL_BUCKETS - 1:REL_BUCKETS], (tq, 1))
                          for h in range(NSA_HEADS)], axis=0)

    def tile(c, near):
        k0 = pl.multiple_of(c * tk, tk)
        kt = ks_ref[pl.ds(k0, tk), :]
        vt = vs_ref[pl.ds(k0, tk), :]
        e_c = eb_ref[pl.ds(pl.multiple_of(nb - bpt * c, bpt), nb), :]
        s3 = _dot_nt(Q, kt).reshape(NSA_HEADS, tq, tk)
        mb = jnp.stack([_dot(selb[kv], e_c) for kv in range(NSA_KV)], axis=0)
        if near:
            row = lax.broadcasted_iota(jnp.int32, (tq, tk), 0)
            col = lax.broadcasted_iota(jnp.int32, (tq, tk), 1)
            dist = (q0 + row) - (k0 + col)
            s3 = s3 + _head_bias(_t5_bucket(dist), rel_ref)
            mb = jnp.where((dist >= 0)[None], mb, NEG)
        else:
            s3 = s3 + far_bias
        s2 = (s3.reshape(NSA_KV, NSA_GROUP, tq, tk) + mb[:, None]).reshape(R, tk)
        m_old = m_ref[...]
        m_new = jnp.maximum(m_old, jnp.max(s2, axis=-1, keepdims=True))
        alpha = jnp.exp(m_old - m_new)
        p = jnp.exp(s2 - m_new)
        l_ref[...] = alpha * l_ref[...] + jnp.sum(p, axis=-1, keepdims=True)
        acc_ref[...] = alpha * acc_ref[...] + _dot(p.astype(BF16), vt)
        m_ref[...] = m_new

    n_tiles = (q0 + tq - 1) // tk + 1
    n_far = jnp.maximum(q0 - (REL_MAX_DIST - 1), 0) // tk

    def far_body(c, carry):
        tile(c, False)
        return carry

    def near_body(c, carry):
        tile(c, True)
        return carry

    lax.fori_loop(0, n_far, far_body, 0)
    lax.fori_loop(n_far, n_tiles, near_body, 0)
    os_ref[...] = acc_ref[...] / l_ref[...]


def _selattn(proj, selb, ks, vs, rel_t, B, S):
    tq = TQ
    tk = min(TK, S)
    nqt = S // tq
    nb = S // NSA_SLC_BLOCK
    R = NSA_HEADS * tq
    r = np.arange(2 * nb)[:, None] - nb
    eb = jnp.asarray((r == (np.arange(tk)[None, :] // NSA_SLC_BLOCK)).astype(np.float32), BF16)
    return pl.pallas_call(
        _selattn_kernel,
        grid=(B, nqt),
        in_specs=[pl.BlockSpec((tq, 1024), lambda b, i: (b * nqt + i, COL["nqw"] // 8)),
                  pl.BlockSpec((tq, 2 * nb), lambda b, i: (b * nqt + i, 0)),
                  pl.BlockSpec((None, S, LANES), lambda b, i: (b, 0, 0)),
                  pl.BlockSpec((None, S, LANES), lambda b, i: (b, 0, 0)),
                  pl.BlockSpec(rel_t.shape, lambda b, i: (0, 0)),
                  pl.BlockSpec(eb.shape, lambda b, i: (0, 0))],
        out_specs=pl.BlockSpec((R, LANES), lambda b, i: (b * nqt + i, 0)),
        out_shape=jax.ShapeDtypeStruct((B * S * NSA_HEADS, LANES), F32),
        scratch_shapes=[pltpu.VMEM((R, 1), F32), pltpu.VMEM((R, 1), F32), pltpu.VMEM((R, LANES), F32)],
        compiler_params=_params(("parallel", "parallel")),
        name="nsa_sel_attn",
    )(proj, selb, ks, vs, rel_t, eb)


def _winattn_kernel(q_ref, sm_ref, oc_ref, os_ref, kw_ref, vw_ref, rel_ref, ng_ref, o_ref):
    tq = q_ref.shape[0]
    R = NSA_HEADS * tq
    S = kw_ref.shape[0]
    span = min(WIN_SPAN, S)
    q0 = pl.program_id(1) * tq
    start = pl.multiple_of(jnp.clip(q0 + tq - span, 0, S - span), tq)
    Q = _stack_heads(q_ref[...])
    kt = kw_ref[pl.ds(start, span), :]
    vt = vw_ref[pl.ds(start, span), :]
    row = lax.broadcasted_iota(jnp.int32, (tq, span), 0)
    col = lax.broadcasted_iota(jnp.int32, (tq, span), 1)
    dist = (q0 + row) - (start + col)
    allowed = ((dist >= 0) & (dist < NSA_WINDOW))[None]
    s3 = _dot_nt(Q, kt).reshape(NSA_HEADS, tq, span) + _head_bias(_t5_bucket(dist), rel_ref)
    s3 = jnp.where(allowed, s3, NEG)
    mx = jnp.max(s3, axis=-1, keepdims=True)
    e = jnp.exp(s3 - mx)
    p = jnp.where(allowed, e / jnp.sum(e, axis=-1, keepdims=True), 0.0)
    ow = _dot(p.reshape(R, span).astype(BF16), vt)

    gates = _sigmoid(sm_ref[...])
    lane = lax.broadcasted_iota(jnp.int32, (tq, LANES), 1)
    heads = []
    ssq = jnp.zeros((tq, 1), F32)
    for h in range(NSA_HEADS):
        rs = slice(h * tq, (h + 1) * tq)
        g = [gates[:, GATE_LANE0 + 3 * h + br:GATE_LANE0 + 3 * h + br + 1] for br in range(3)]
        oh = g[0] * oc_ref[rs, :] + g[1] * os_ref[rs, :] + g[2] * ow[rs, :]
        kv = h // NSA_GROUP
        valid = (lane >= kv * NSA_HEAD_DIM) & (lane < (kv + 1) * NSA_HEAD_DIM)
        oh = jnp.where(valid, oh, 0.0)
        ssq = ssq + jnp.sum(oh * oh, axis=-1, keepdims=True)
        heads.append(oh)
    rinv = lax.rsqrt(ssq / GROUP_W + 1e-6)
    o_ref[...] = (jnp.concatenate(heads, axis=1) * rinv * ng_ref[...]).astype(o_ref.dtype)


def _winattn(proj, oc, os_, kw, vw, rel_t, ngw, B, S):
    tq = TQ
    nqt = S // tq
    R = NSA_HEADS * tq
    return pl.pallas_call(
        _winattn_kernel,
        grid=(B, nqt),
        in_specs=[pl.BlockSpec((tq, 1024), lambda b, i: (b * nqt + i, COL["nqw"] // 8)),
                  pl.BlockSpec((tq, LANES), lambda b, i: (b * nqt + i, COL["small"])),
                  pl.BlockSpec((R, LANES), lambda b, i: (b * nqt + i, 0)),
                  pl.BlockSpec((R, LANES), lambda b, i: (b * nqt + i, 0)),
                  pl.BlockSpec((None, S, LANES), lambda b, i: (b, 0, 0)),
                  pl.BlockSpec((None, S, LANES), lambda b, i: (b, 0, 0)),
                  pl.BlockSpec(rel_t.shape, lambda b, i: (0, 0)),
                  pl.BlockSpec(ngw.shape, lambda b, i: (0, 0))],
        out_specs=pl.BlockSpec((tq, NSA_HEADS * LANES), lambda b, i: (b * nqt + i, 0)),
        out_shape=jax.ShapeDtypeStruct((B * S, NSA_HEADS * LANES), BF16),
        compiler_params=_params(("parallel", "parallel")),
        name="nsa_win_attn_merge",
    )(proj, proj, oc, os_, kw, vw, rel_t, ngw)


def _widen_heads(x, axis):
    x = jnp.moveaxis(x, axis, -1)
    lead = x.shape[:-1]
    x = x.reshape(*lead, NSA_KV, NSA_GROUP, 1, NSA_HEAD_DIM)
    sel = jnp.eye(NSA_KV, dtype=x.dtype).reshape(NSA_KV, 1, NSA_KV, 1)
    x = (x * sel).reshape(*lead, NSA_HEADS * LANES)
    return jnp.moveaxis(x, -1, axis)


def _build_w_in(w):
    (hq, hf, hi, hg, nq, nkc, nvc, nks, nvs, nkw, nvw, ngate,
     sz, sxbc, sdt, rq, rk, rv, rg) = jnp.split(w, IN_SPLITS, axis=1)
    D = w.shape[0]
    nqw = _widen_heads(nq * NSA_HEAD_DIM ** -0.5, 1)
    perm = np.concatenate([np.arange(0, RET_DK, 2), np.arange(1, RET_DK, 2)])
    deint = lambda t: t.reshape(D, RET_HEADS, RET_DK)[:, :, perm].reshape(D, GROUP_W)
    small = jnp.concatenate([ngate, sdt, jnp.zeros((D, LANES - 32), w.dtype)], axis=1)
    pad = jnp.zeros((D, LANES), w.dtype)
    cols = [hq, hf, hi, hg, nqw, sxbc, sz, deint(rq), deint(rk), rv, rg,
            nkc, nvc, nks, nvs, nkw, nvw, small, pad]
    return jnp.concatenate(cols, axis=1).astype(BF16)


def _build_cmp_weights(pe, w1, w2):
    w1r = w1.reshape(2, NSA_CMP_STRIDE, NSA_HEAD_DIM, NSA_CMP_HIDDEN)
    eye = jnp.eye(NSA_KV, dtype=w1.dtype)
    big = jnp.einsum("ardc,kj->arkdjc", w1r, eye).reshape(
        2, NSA_CMP_STRIDE * NSA_KV * NSA_HEAD_DIM, NSA_KV * NSA_CMP_HIDDEN)
    w2bd = jnp.einsum("cd,kj->kcjd", w2, eye).reshape(NSA_KV * NSA_CMP_HIDDEN, NSA_KV * NSA_HEAD_DIM)
    per = pe.reshape(2, NSA_CMP_STRIDE, 1, NSA_HEAD_DIM)
    pe2 = jnp.broadcast_to(per, (2, NSA_CMP_STRIDE, NSA_KV, NSA_HEAD_DIM)).reshape(2, -1)
    return pe2, big[0].astype(BF16), big[1].astype(BF16), w2bd.astype(BF16)


def _rotary_tables(S):
    half = RET_DK // 2
    theta = 1.0 / (10000.0 ** jnp.linspace(0.0, 1.0, half, dtype=F32))
    ang = jnp.arange(S, dtype=F32)[:, None] * theta[None, :]
    cos, sin = jnp.cos(ang), jnp.sin(ang)
    return jnp.concatenate([cos, cos], axis=1), jnp.concatenate([-sin, sin], axis=1)


def _mixer(x2, B, S, l, p, lower_bounds, rel_t, cos_t, sin_t):
    T = B * S
    proj = _proj(x2, _build_w_in(p["w_in"][l]), tm=min(512, T), tn=1536)
    row = lambda v: v.reshape(1, -1).astype(F32)

    lb = lower_bounds[l].astype(F32)
    o_a = _hgrn(proj, B, S, row(jnp.log(lb)), row(jnp.log1p(-lb)), row(1.0 - lb),
                row(p["hgrn_norm_g"][l]))

    col = lambda name: proj[:, COL[name] * LANES:(COL[name] + 1) * LANES]
    nb = S // NSA_SLC_BLOCK
    grp = lambda name: col(name).reshape(B, nb, 4 * NSA_CMP_STRIDE * LANES)
    kcmp = _compress(grp("nkc"), *_build_cmp_weights(p["nsa_pe_k"][l], p["nsa_w1_k"][l], p["nsa_w2_k"][l]))
    vcmp = _compress(grp("nvc"), *_build_cmp_weights(p["nsa_pe_v"][l], p["nsa_w1_v"][l], p["nsa_w2_v"][l]))
    o_cmp, selb = _cmpattn(proj, kcmp, vcmp, rel_t, B, S)
    seq = lambda name: col(name).astype(BF16).reshape(B, S, LANES)
    o_sel = _selattn(proj, selb, seq("nks"), seq("nvs"), rel_t, B, S)
    ngw = _widen_heads(p["nsa_norm_g"][l].astype(F32), 0).reshape(1, -1)
    o_b = _winattn(proj, o_cmp, o_sel, seq("nkw"), seq("nvw"), rel_t, ngw, B, S)

    lane_vec = lambda v: jnp.zeros((1, LANES), F32).at[0, DT_LANE0:DT_LANE0 + SSM_HEADS].set(v.astype(F32))
    o_c = _ssd(proj, B, S, p["ssm_conv_w"][l].astype(F32), row(p["ssm_conv_b"][l]),
               lane_vec(p["ssm_dt_bias"][l]), lane_vec(-jnp.exp(p["ssm_a_log"][l].astype(F32))),
               row(jnp.repeat(p["ssm_d"][l].astype(F32), SSM_HEAD_DIM)), row(p["ssm_norm_g"][l]))

    o_d = _retention(proj, B, S, cos_t, sin_t)

    w_out = p["w_out"][l]
    wa, wb, wc, wd = (w_out[i * GROUP_W:(i + 1) * GROUP_W] for i in range(4))
    return o_a, o_b, o_c, o_d, wa.astype(BF16), _widen_heads(wb, 0).astype(BF16), wc.astype(BF16), wd.astype(BF16)


def kernel(x, ln1_g, ln1_b, ffn1_w1, ffn1_w3, ffn1_w2, ln2_g, ln2_b, w_in, w_out, hgrn_lb_logits, hgrn_norm_g, nsa_pe_k, nsa_w1_k, nsa_w2_k, nsa_pe_v, nsa_w1_v, nsa_w2_v, nsa_norm_g, rel_bias, ssm_conv_w, ssm_conv_b, ssm_dt_bias, ssm_a_log, ssm_d, ssm_norm_g, ln3_g, ln3_b, ffn2_w1, ffn2_w3, ffn2_w2):
    B, S, D = x.shape
    T = B * S
    depth = w_in.shape[0]
    p = dict(w_in=w_in, w_out=w_out, hgrn_norm_g=hgrn_norm_g, nsa_pe_k=nsa_pe_k, nsa_w1_k=nsa_w1_k,
             nsa_w2_k=nsa_w2_k, nsa_pe_v=nsa_pe_v, nsa_w1_v=nsa_w1_v, nsa_w2_v=nsa_w2_v,
             nsa_norm_g=nsa_norm_g, ssm_conv_w=ssm_conv_w, ssm_conv_b=ssm_conv_b,
             ssm_dt_bias=ssm_dt_bias, ssm_a_log=ssm_a_log, ssm_d=ssm_d, ssm_norm_g=ssm_norm_g)
    cum = jnp.cumsum(jax.nn.softmax(hgrn_lb_logits.astype(F32), axis=0), axis=0)
    lower_bounds = cum - cum[:1]
    rel_t = jnp.zeros((NSA_HEADS, LANES), F32).at[:, :REL_BUCKETS].set(rel_bias.astype(F32).T)
    cos_t, sin_t = _rotary_tables(S)
    row = lambda v: v.reshape(1, -1).astype(F32)
    tm = min(512, T)
    tf = 512 if ffn1_w1.shape[2] % 512 == 0 else ffn1_w1.shape[2]
    x2 = x.reshape(T, D).astype(F32)
    for l in range(depth):
        x2 = _ffn(x2, ffn1_w1[l].astype(BF16), ffn1_w3[l].astype(BF16), ffn1_w2[l].astype(BF16),
                  row(ln1_g[l]), row(ln1_b[l]), tm, tf)
        o_a, o_b, o_c, o_d, wa, wb, wc, wd = _mixer(x2, B, S, l, p, lower_bounds, rel_t, cos_t, sin_t)
        x2 = _outproj(x2, o_a, o_b, o_c, o_d, wa, wb, wc, wd, row(ln2_g[l]), row(ln2_b[l]), min(256, T))
        x2 = _ffn(x2, ffn2_w1[l].astype(BF16), ffn2_w3[l].astype(BF16), ffn2_w2[l].astype(BF16),
                  row(ln3_g[l]), row(ln3_b[l]), tm, tf)
    return x2.reshape(B, S, D).astype(x.dtype)
```

```python
import functools
import math

import numpy as np
import jax
import jax.numpy as jnp
from jax import lax
from jax.experimental import pallas as pl
from jax.experimental.pallas import tpu as pltpu

F32 = jnp.float32
BF16 = jnp.bfloat16
HIGHEST = lax.Precision.HIGHEST

D_MODEL = 2048
DEPTH = 2
GROUP_W = 512
ALPHA = (2 * DEPTH) ** 0.25
HG_HEADS = 4
NSA_HEADS = 8
NSA_KV = 2
NSA_GROUP = 4
NSA_HEAD_DIM = 64
NSA_CMP_STRIDE = 16
NSA_CMP_BLOCK = 32
NSA_SLC_BLOCK = 64
NSA_TOP_N = 16
NSA_WINDOW = 512
NSA_CMP_HIDDEN = 256
SSM_HEADS = 8
SSM_HEAD_DIM = 64
SSM_GROUPS = 2
SSM_STATE = 128
SSM_CONV = 4
RET_HEADS = 4
RET_DK = 128
REL_BUCKETS = 32
REL_EXACT = 16
REL_MAX_DIST = 2048
IN_SIZES = ((GROUP_W,) * 4 + (GROUP_W,) + (128,) * 6 + (24,)
            + (GROUP_W, 1024, SSM_HEADS) + (GROUP_W,) * 4)
IN_SPLITS = tuple(int(v) for v in np.cumsum(IN_SIZES)[:-1])

LANES = 128
VMEM_LIMIT = 56 * 1024 * 1024

COL = dict(hq=0, hf=4, hi=8, hg=12, nqw=16, sxbc=24, sz=32, rq=36, rk=40, rv=44, rg=48,
           nkc=52, nvc=53, nks=54, nvs=55, nkw=56, nvw=57, small=58)
NCOL = 60
GATE_LANE0 = 0
DT_LANE0 = 24

CHUNK = 128
TQ = 64
TK = 1024
BAND_TK = 256
BAND_OFF = 2
WIN_SPAN = NSA_WINDOW + 2 * TQ
NEG = -1e30


def _params(sem):
    return pltpu.CompilerParams(dimension_semantics=sem, vmem_limit_bytes=VMEM_LIMIT)


def _dot(a, b):
    return jnp.dot(a, b, preferred_element_type=F32)


def _dot_nt(a, b):
    return lax.dot_general(a, b, (((1,), (1,)), ((), ())), preferred_element_type=F32)


def _dot_exact(a, b):
    return jnp.dot(a, b, precision=HIGHEST, preferred_element_type=F32)


def _sigmoid(x):
    return 1.0 / (1.0 + jnp.exp(-x))


def _silu(x):
    return x * _sigmoid(x)


def _softplus(x):
    return jnp.maximum(x, 0.0) + jnp.log1p(jnp.exp(-jnp.abs(x)))


def _layer_norm(r, g, b):
    mu = jnp.mean(r, axis=-1, keepdims=True)
    d = r - mu
    var = jnp.mean(d * d, axis=-1, keepdims=True)
    return d * lax.rsqrt(var + 1e-5) * g + b


def _ffn_kernel(x_ref, w1_ref, w3_ref, w2_ref, g_ref, b_ref, o_ref, acc_ref, xb_ref):
    j = pl.program_id(1)

    @pl.when(j == 0)
    def _():
        xb_ref[...] = x_ref[...].astype(BF16)
        acc_ref[...] = jnp.zeros_like(acc_ref)

    xb = xb_ref[...]
    h1 = _dot(xb, w1_ref[...])
    h3 = _dot(xb, w3_ref[...])
    a = (_silu(h1) * h3).astype(BF16)
    acc_ref[...] += _dot(a, w2_ref[...])

    @pl.when(j == pl.num_programs(1) - 1)
    def _():
        r = ALPHA * x_ref[...] + 0.5 * acc_ref[...]
        o_ref[...] = _layer_norm(r, g_ref[...], b_ref[...])


def _ffn(x, w1, w3, w2, g, b, tm, tf):
    T, D = x.shape
    F = w1.shape[1]
    return pl.pallas_call(
        _ffn_kernel,
        grid=(T // tm, F // tf),
        in_specs=[
            pl.BlockSpec((tm, D), lambda i, j: (i, 0)),
            pl.BlockSpec((D, tf), lambda i, j: (0, j)),
            pl.BlockSpec((D, tf), lambda i, j: (0, j)),
            pl.BlockSpec((tf, D), lambda i, j: (j, 0)),
            pl.BlockSpec((1, D), lambda i, j: (0, 0)),
            pl.BlockSpec((1, D), lambda i, j: (0, 0)),
        ],
        out_specs=pl.BlockSpec((tm, D), lambda i, j: (i, 0)),
        out_shape=jax.ShapeDtypeStruct((T, D), F32),
        scratch_shapes=[pltpu.VMEM((tm, D), F32), pltpu.VMEM((tm, D), BF16)],
        compiler_params=_params(("parallel", "arbitrary")),
        name="ffn_ln",
    )(x, w1, w3, w2, g, b)


def _proj_kernel(x_ref, w_ref, o_ref, xb_ref):
    @pl.when(pl.program_id(1) == 0)
    def _():
        xb_ref[...] = x_ref[...].astype(BF16)

    o_ref[...] = _dot(xb_ref[...], w_ref[...])


def _proj(x, w, tm, tn):
    T, D = x.shape
    N = w.shape[1]
    return pl.pallas_call(
        _proj_kernel,
        grid=(T // tm, N // tn),
        in_specs=[pl.BlockSpec((tm, D), lambda i, j: (i, 0)),
                  pl.BlockSpec((D, tn), lambda i, j: (0, j))],
        out_specs=pl.BlockSpec((tm, tn), lambda i, j: (i, j)),
        out_shape=jax.ShapeDtypeStruct((T, N), F32),
        scratch_shapes=[pltpu.VMEM((tm, D), BF16)],
        compiler_params=_params(("parallel", "arbitrary")),
        name="in_proj",
    )(x, w)


def _outproj_kernel(x_ref, oa_ref, ob_ref, oc_ref, od_ref, wa_ref, wb_ref, wc_ref, wd_ref,
                    g_ref, b_ref, o_ref):
    mix = (_dot(oa_ref[...], wa_ref[...]) + _dot(ob_ref[...], wb_ref[...])
           + _dot(oc_ref[...], wc_ref[...]) + _dot(od_ref[...], wd_ref[...]))
    o_ref[...] = _layer_norm(ALPHA * x_ref[...] + mix, g_ref[...], b_ref[...])


def _outproj(x, oa, ob, oc, od, wa, wb, wc, wd, g, b, tm):
    T, D = x.shape
    row = lambda a: pl.BlockSpec((tm, a.shape[1]), lambda i: (i, 0))
    full = lambda a: pl.BlockSpec(a.shape, lambda i: (0, 0))
    return pl.pallas_call(
        _outproj_kernel,
        grid=(T // tm,),
        in_specs=[row(x), row(oa), row(ob), row(oc), row(od),
                  full(wa), full(wb), full(wc), full(wd), full(g), full(b)],
        out_specs=row(x),
        out_shape=jax.ShapeDtypeStruct((T, D), F32),
        compiler_params=_params(("parallel",)),
        name="out_proj_ln",
    )(x, oa, ob, oc, od, wa, wb, wc, wd, g, b)


def _hgrn_tables(C):
    i = np.arange(C)[:, None]
    ip = np.arange(C)[None, :]
    mq = [(ip <= i)]
    mk = [(ip > i)]
    masks = [np.eye(C, dtype=bool)]
    s = C // 2
    while s >= 1:
        blk = i // s
        mq.append((ip > blk * s) & (ip <= i))
        mk.append((ip > i) & (ip <= np.minimum((blk + 1) * s, C - 1)))
        blk_j = (np.arange(C)[None, :]) // s
        masks.append((blk % 2 == 1) & (blk_j == blk - 1))
        s //= 2
    f = lambda xs: np.concatenate([x.astype(np.float32) for x in xs], axis=0)
    return f(mq), f(mk), np.stack([m.astype(np.float32) for m in masks])


def _hgrn_kernel(q_ref, f_ref, i_ref, g_ref, llb_ref, l1m_ref, oml_ref, ng_ref,
                 mq_ref, mk_ref, msk_ref, o_ref, st_ref):
    @pl.when(pl.program_id(1) == 0)
    def _():
        st_ref[...] = jnp.zeros_like(st_ref)

    C = q_ref.shape[0]
    nlev = msk_ref.shape[0] - 1
    q = _silu(q_ref[...])
    z = f_ref[...]
    log_sig = jnp.minimum(z, 0.0) - jnp.log1p(jnp.exp(-jnp.abs(z)))
    cc = l1m_ref[...] + log_sig
    llb = llb_ref[...]
    logf = jnp.maximum(llb, cc) + jnp.log1p(jnp.exp(-jnp.abs(llb - cc)))
    k = oml_ref[...] * _sigmoid(-z)
    v = i_ref[...]
    eq = _dot_exact(mq_ref[...], logf)
    ek = _dot_exact(mk_ref[...], logf)
    outs = []
    for h in range(HG_HEADS):
        sl = slice(h * LANES, (h + 1) * LANES)
        qh, kh, vh = q[:, sl], k[:, sl], v[:, sl]
        a = msk_ref[0] * _dot_nt(qh.astype(BF16), kh.astype(BF16))
        for l in range(nlev):
            rows = slice((1 + l) * C, (2 + l) * C)
            qs = (qh * jnp.exp(eq[rows, sl])).astype(BF16)
            ks = (kh * jnp.exp(ek[rows, sl])).astype(BF16)
            a = a + msk_ref[1 + l] * _dot_nt(qs, ks)
        b = eq[0:C, sl]
        st = st_ref[h]
        o = _dot(a.astype(BF16), vh.astype(BF16))
        o = o + _dot_nt((qh * jnp.exp(b)).astype(BF16), st.astype(BF16))
        kd = (kh * jnp.exp(ek[0:C, sl])).astype(BF16)
        st_ref[h] = st * jnp.exp(b[C - 1:C, :]) + _dot(vh.T.astype(BF16), kd)
        outs.append(o * lax.rsqrt(jnp.mean(o * o, axis=-1, keepdims=True) + 1e-6))
    o = jnp.concatenate(outs, axis=1)
    o_ref[...] = (o * ng_ref[...] * _silu(g_ref[...])).astype(o_ref.dtype)


def _hgrn(proj, B, S, llb, l1m, oml, ng):
    C = CHUNK
    nc = S // C
    mq, mk, msk = (jnp.asarray(t) for t in _hgrn_tables(C))
    col = lambda name: pl.BlockSpec((C, GROUP_W), lambda b, c, n=COL[name] // 4: (b * nc + c, n))
    vec = pl.BlockSpec((1, GROUP_W), lambda b, c: (0, 0))
    full2 = lambda a: pl.BlockSpec(a.shape, lambda b, c: (0, 0))
    return pl.pallas_call(
        _hgrn_kernel,
        grid=(B, nc),
        in_specs=[col("hq"), col("hf"), col("hi"), col("hg"), vec, vec, vec, vec,
                  full2(mq), full2(mk), pl.BlockSpec(msk.shape, lambda b, c: (0, 0, 0))],
        out_specs=pl.BlockSpec((C, GROUP_W), lambda b, c: (b * nc + c, 0)),
        out_shape=jax.ShapeDtypeStruct((B * S, GROUP_W), BF16),
        scratch_shapes=[pltpu.VMEM((HG_HEADS, LANES, LANES), F32)],
        compiler_params=_params(("parallel", "arbitrary")),
        name="hgrn2",
    )(proj, proj, proj, proj, llb, l1m, oml, ng, mq, mk, msk)


def _ssd_kernel(z_ref, xbc_ref, sm_ref, cw_ref, cb_ref, dtb_ref, aneg_ref, dsk_ref, ng_ref,
                ex_ref, o_ref, tail_ref, st_ref):
    @pl.when(pl.program_id(1) == 0)
    def _():
        tail_ref[...] = jnp.zeros_like(tail_ref)
        st_ref[...] = jnp.zeros_like(st_ref)

    L = xbc_ref.shape[0]
    x = xbc_ref[...]
    xe = jnp.concatenate([tail_ref[...], x], axis=0)
    cw = cw_ref[...]
    conv = cb_ref[...]
    for kk in range(SSM_CONV):
        conv = conv + cw[kk:kk + 1, :] * xe[5 + kk:5 + kk + L, :]
    tail_ref[...] = x[L - 8:L, :]
    conv = _silu(conv)
    xs = conv[:, 0:GROUP_W]
    bm = conv[:, GROUP_W:GROUP_W + 256]
    cm = conv[:, GROUP_W + 256:GROUP_W + 512]

    dtf = _softplus(sm_ref[...] + dtb_ref[...])
    la = dtf * aneg_ref[...]
    ri = lax.broadcasted_iota(jnp.int32, (L, L), 0)
    ci = lax.broadcasted_iota(jnp.int32, (L, L), 1)
    tri = ri >= ci
    bfull = _dot_exact(tri.astype(F32), la)
    ex = ex_ref[...]
    bexp = _dot_exact(bfull, ex)
    dtexp = _dot_exact(dtf, ex)
    b_t = bfull.T
    xdt = xs * dtexp
    lane = lax.broadcasted_iota(jnp.int32, (L, LANES), 1)

    scores = []
    for g in range(SSM_GROUPS):
        cg = cm[:, g * SSM_STATE:(g + 1) * SSM_STATE].astype(BF16)
        bg = bm[:, g * SSM_STATE:(g + 1) * SSM_STATE].astype(BF16)
        cb = _dot_nt(cg, bg)
        for hh in range(SSM_HEADS // SSM_GROUPS):
            h = g * (SSM_HEADS // SSM_GROUPS) + hh
            bcol = bfull[:, DT_LANE0 + h:DT_LANE0 + h + 1]
            brow = b_t[DT_LANE0 + h:DT_LANE0 + h + 1, :]
            dec = jnp.exp(jnp.where(tri, bcol - brow, NEG))
            scores.append((cb * dec).astype(BF16))
    y_pairs = []
    for u in range(SSM_HEADS // 2):
        slab = xdt[:, u * LANES:(u + 1) * LANES]
        lo = jnp.where(lane < SSM_HEAD_DIM, slab, 0.0).astype(BF16)
        hi = jnp.where(lane >= SSM_HEAD_DIM, slab, 0.0).astype(BF16)
        y_pairs.append(_dot(scores[2 * u], lo) + _dot(scores[2 * u + 1], hi))
    y_intra = jnp.concatenate(y_pairs, axis=1)

    blast = bexp[L - 1:L, :]
    w = (xdt * jnp.exp(blast - bexp)).astype(BF16)
    y_inter = []
    for g in range(SSM_GROUPS):
        gs = slice(g * 256, (g + 1) * 256)
        cg = cm[:, g * SSM_STATE:(g + 1) * SSM_STATE].astype(BF16)
        st = st_ref[g]
        y_inter.append(_dot(cg, st.astype(BF16)))
        bg_t = bm[:, g * SSM_STATE:(g + 1) * SSM_STATE].T.astype(BF16)
        st_ref[g] = st * jnp.exp(blast[:, gs]) + _dot(bg_t, w[:, gs])
    y = y_intra + jnp.concatenate(y_inter, axis=1) * jnp.exp(bexp) + dsk_ref[...] * xs
    y = y * _silu(z_ref[...])
    halves = []
    for g in range(SSM_GROUPS):
        seg = y[:, g * 256:(g + 1) * 256]
        halves.append(seg * lax.rsqrt(jnp.mean(seg * seg, axis=-1, keepdims=True) + 1e-6))
    o_ref[...] = (jnp.concatenate(halves, axis=1) * ng_ref[...]).astype(o_ref.dtype)


def _ssd(proj, B, S, cw, cb, dtb, aneg, dsk, ng):
    L = CHUNK
    nc = S // L
    ex = np.zeros((LANES, GROUP_W), np.float32)
    for h in range(SSM_HEADS):
        ex[DT_LANE0 + h, h * SSM_HEAD_DIM:(h + 1) * SSM_HEAD_DIM] = 1.0
    ex = jnp.asarray(ex)
    full2 = lambda a: pl.BlockSpec(a.shape, lambda b, c: (0, 0))
    return pl.pallas_call(
        _ssd_kernel,
        grid=(B, nc),
        in_specs=[
            pl.BlockSpec((L, GROUP_W), lambda b, c: (b * nc + c, COL["sz"] // 4)),
            pl.BlockSpec((L, 1024), lambda b, c: (b * nc + c, COL["sxbc"] // 8)),
            pl.BlockSpec((L, LANES), lambda b, c: (b * nc + c, COL["small"])),
            full2(cw), full2(cb), full2(dtb), full2(aneg), full2(dsk), full2(ng), full2(ex)],
        out_specs=pl.BlockSpec((L, GROUP_W), lambda b, c: (b * nc + c, 0)),
        out_shape=jax.ShapeDtypeStruct((B * S, GROUP_W), BF16),
        scratch_shapes=[pltpu.VMEM((8, 1024), F32), pltpu.VMEM((SSM_GROUPS, SSM_STATE, 256), F32)],
        compiler_params=_params(("parallel", "arbitrary")),
        name="ssd",
    )(proj, proj, proj, cw, cb, dtb, aneg, dsk, ng, ex)


def _ret_kernel(q_ref, k_ref, v_ref, g_ref, cos_ref, sin_ref, dec_ref, qs_ref, ks_ref, sd_ref,
                o_ref, st_ref):
    @pl.when(pl.program_id(1) == 0)
    def _():
        st_ref[...] = jnp.zeros_like(st_ref)

    cos = cos_ref[...]
    sin = sin_ref[...]
    outs = []
    for h in range(RET_HEADS):
        sl = slice(h * LANES, (h + 1) * LANES)
        qh = q_ref[:, sl]
        kh = k_ref[:, sl]
        qh = qh * cos + pltpu.roll(qh, RET_DK // 2, axis=1) * sin
        kh = (kh * cos + pltpu.roll(kh, RET_DK // 2, axis=1) * sin) * (RET_DK ** -0.5)
        vh = v_ref[:, sl].astype(BF16)
        sc = (_dot_nt(qh.astype(BF16), kh.astype(BF16)) * dec_ref[h]).astype(BF16)
        st = st_ref[h]
        y = _dot(sc, vh) + _dot((qh * qs_ref[:, sl]).astype(BF16), st.astype(BF16))
        kd_t = (kh * ks_ref[:, sl]).T.astype(BF16)
        st_ref[h] = st * sd_ref[h] + _dot(kd_t, vh)
        mu = jnp.mean(y, axis=-1, keepdims=True)
        d = y - mu
        outs.append(d * lax.rsqrt(jnp.mean(d * d, axis=-1, keepdims=True) + 1e-5))
    o_ref[...] = (_silu(g_ref[...]) * jnp.concatenate(outs, axis=1)).astype(o_ref.dtype)


def _retention(proj, B, S, cos_t, sin_t):
    L = CHUNK
    nc = S // L
    lg = jnp.log(1.0 - 2.0 ** (-5.0 - jnp.arange(RET_HEADS, dtype=F32)))
    i = jnp.arange(L, dtype=F32)
    diff = i[:, None] - i[None, :]
    dec = jnp.where(diff >= 0, jnp.exp(lg[:, None, None] * jnp.maximum(diff, 0.0)), 0.0)
    rep = lambda t: jnp.repeat(t, LANES, axis=1)
    qs = rep(jnp.exp((i[:, None] + 1.0) * lg[None, :]))
    ks = rep(jnp.exp((L - 1.0 - i[:, None]) * lg[None, :]))
    sd = jnp.broadcast_to(jnp.exp(L * lg)[:, None, None], (RET_HEADS, LANES, LANES))
    col = lambda name: pl.BlockSpec((L, GROUP_W), lambda b, c, n=COL[name] // 4: (b * nc + c, n))
    return pl.pallas_call(
        _ret_kernel,
        grid=(B, nc),
        in_specs=[col("rq"), col("rk"), col("rv"), col("rg"),
                  pl.BlockSpec((L, LANES), lambda b, c: (c, 0)),
                  pl.BlockSpec((L, LANES), lambda b, c: (c, 0)),
                  pl.BlockSpec((RET_HEADS, L, L), lambda b, c: (0, 0, 0)),
                  pl.BlockSpec((L, GROUP_W), lambda b, c: (0, 0)),
                  pl.BlockSpec((L, GROUP_W), lambda b, c: (0, 0)),
                  pl.BlockSpec((RET_HEADS, LANES, LANES), lambda b, c: (0, 0, 0))],
        out_specs=pl.BlockSpec((L, GROUP_W), lambda b, c: (b * nc + c, 0)),
        out_shape=jax.ShapeDtypeStruct((B * S, GROUP_W), BF16),
        scratch_shapes=[pltpu.VMEM((RET_HEADS, RET_DK, RET_DK), F32)],
        compiler_params=_params(("parallel", "arbitrary")),
        name="retention",
    )(proj, proj, proj, proj, cos_t, sin_t, dec, qs, ks, sd)


def _t5_bucket(dist):
    n = jnp.maximum(dist, 0)
    nf = jnp.maximum(n, 1).astype(F32)
    large = REL_EXACT + (jnp.log(nf / REL_EXACT) / math.log(REL_MAX_DIST / REL_EXACT)
                         * (REL_BUCKETS - REL_EXACT)).astype(jnp.int32)
    return jnp.where(n < REL_EXACT, n, jnp.minimum(large, REL_BUCKETS - 1))


def _head_bias(bucket, rel_ref):
    rows, cols = bucket.shape
    per_head = []
    for h in range(NSA_HEADS):
        tbl = jnp.broadcast_to(rel_ref[h:h + 1, :], (rows, LANES))
        chunks = [jnp.take_along_axis(tbl, bucket[:, c:c + LANES], axis=1)
                  for c in range(0, cols, LANES)]
        per_head.append(chunks[0] if len(chunks) == 1 else jnp.concatenate(chunks, axis=1))
    return jnp.stack(per_head, axis=0)


def _stack_heads(qw):
    return jnp.concatenate([qw[:, h * LANES:(h + 1) * LANES] for h in range(NSA_HEADS)],
                           axis=0).astype(BF16)


def _cmp_kernel(g_ref, pe_ref, w1a_ref, w1b_ref, w2_ref, o_ref):
    nb = g_ref.shape[0]
    gw = g_ref.shape[1] // 4
    pe = pe_ref[...]
    slabs = [g_ref[:, s * gw:(s + 1) * gw] for s in range(4)]
    nxt0 = pltpu.roll(slabs[0], nb - 1, axis=0)
    for s in range(4):
        a = (slabs[s] + pe[0:1, :]).astype(BF16)
        bn = ((slabs[s + 1] if s < 3 else nxt0) + pe[1:2, :]).astype(BF16)
        hid = _silu(_dot(a, w1a_ref[...]) + _dot(bn, w1b_ref[...]))
        o_ref[s * nb:(s + 1) * nb, :] = _dot(hid.astype(BF16), w2_ref[...]).astype(o_ref.dtype)


def _compress(g, pe2, w1a, w1b, w2bd):
    B, nb, gw4 = g.shape
    full2 = lambda a: pl.BlockSpec(a.shape, lambda b: (0, 0))
    return pl.pallas_call(
        _cmp_kernel,
        grid=(B,),
        in_specs=[pl.BlockSpec((None, nb, gw4), lambda b: (b, 0, 0)),
                  full2(pe2), full2(w1a), full2(w1b), full2(w2bd)],
        out_specs=pl.BlockSpec((None, 4 * nb, LANES), lambda b: (b, 0, 0)),
        out_shape=jax.ShapeDtypeStruct((B, 4 * nb, LANES), BF16),
        compiler_params=_params(("parallel",)),
        name="nsa_compress",
    )(g, pe2, w1a, w1b, w2bd)


def _cmpattn_kernel(q_ref, kc_ref, vc_ref, rel_ref, oc_ref, selb_ref):
    tq = q_ref.shape[0]
    ncmp = kc_ref.shape[0]
    nb = ncmp // 4
    R = NSA_HEADS * tq
    q0 = pl.program_id(1) * tq
    Q = _stack_heads(q_ref[...])
    s = _dot_nt(Q, kc_ref[...])
    row = lax.broadcasted_iota(jnp.int32, (tq, ncmp), 0)
    col = lax.broadcasted_iota(jnp.int32, (tq, ncmp), 1)
    m = (col % nb) * 4 + col // nb
    dist = (q0 + row) - (m * NSA_CMP_STRIDE + NSA_CMP_BLOCK - 1)
    allowed = (dist >= 0)[None]
    s3 = s.reshape(NSA_HEADS, tq, ncmp) + _head_bias(_t5_bucket(dist), rel_ref)
    s3 = jnp.where(allowed, s3, NEG)
    mx = jnp.max(s3, axis=-1, keepdims=True)
    e = jnp.exp(s3 - mx)
    p = jnp.where(allowed, e / jnp.sum(e, axis=-1, keepdims=True), 0.0)
    oc_ref[...] = _dot(p.reshape(R, ncmp).astype(BF16), vc_ref[...])

    ps = p.reshape(NSA_KV, NSA_GROUP, tq, ncmp).sum(axis=1).reshape(NSA_KV * tq, ncmp)
    p0, p1, p2, p3 = (ps[:, i * nb:(i + 1) * nb] for i in range(4))
    j = lax.broadcasted_iota(jnp.int32, (NSA_KV * tq, nb), 1)
    prev3 = jnp.where(j == 0, 0.0, pltpu.roll(p3, 1, axis=1))
    imp = p0 + p1 + p2 + 0.5 * (p3 + prev3)
    t = q0 + (lax.broadcasted_iota(jnp.int32, (NSA_KV * tq, nb), 0) % tq)
    cur = t // NSA_SLC_BLOCK
    forced = (j == 0) | (j == cur) | (j == cur - 1)
    score = jnp.where(j > cur, -1.0, jnp.where(forced, NSA_GROUP + 1.0, imp))
    score = score.T
    jf = lax.broadcasted_iota(jnp.int32, score.shape, 0).astype(F32)
    sel = jnp.zeros(score.shape, jnp.bool_)
    for _ in range(min(NSA_TOP_N, nb)):
        best = jnp.max(score, axis=0, keepdims=True)
        first = jnp.min(jnp.where(score == best, jf, float(nb)), axis=0, keepdims=True)
        hit = jf == first
        sel = sel | hit
        score = jnp.where(hit, -jnp.inf, score)
    selb = jnp.where(sel, 0.0, NEG).T.astype(selb_ref.dtype)
    selb_ref[:, 0:nb] = selb[0:tq]
    selb_ref[:, nb:2 * nb] = selb[tq:2 * tq]


def _cmpattn(proj, kcmp, vcmp, rel_t, B, S):
    tq = TQ
    nqt = S // tq
    ncmp = kcmp.shape[1]
    nb = ncmp // 4
    return pl.pallas_call(
        _cmpattn_kernel,
        grid=(B, nqt),
        in_specs=[pl.BlockSpec((tq, 1024), lambda b, i: (b * nqt + i, COL["nqw"] // 8)),
                  pl.BlockSpec((None, ncmp, LANES), lambda b, i: (b, 0, 0)),
                  pl.BlockSpec((None, ncmp, LANES), lambda b, i: (b, 0, 0)),
                  pl.BlockSpec(rel_t.shape, lambda b, i: (0, 0))],
        out_specs=[pl.BlockSpec((NSA_HEADS * tq, LANES), lambda b, i: (b * nqt + i, 0)),
                   pl.BlockSpec((tq, 2 * nb), lambda b, i: (b * nqt + i, 0))],
        out_shape=[jax.ShapeDtypeStruct((B * S * NSA_HEADS, LANES), F32),
                   jax.ShapeDtypeStruct((B * S, 2 * nb), BF16)],
        compiler_params=_params(("parallel", "parallel")),
        name="nsa_cmp_attn_topk",
    )(proj, kcmp, vcmp, rel_t)


def _band_kernel(rel_ref, o_ref):
    tq = o_ref.shape[1]
    delta = (pl.program_id(0) - BAND_OFF) * tq
    row = lax.broadcasted_iota(jnp.int32, (tq, LANES), 0)
    col = lax.broadcasted_iota(jnp.int32, (tq, LANES), 1)
    dist = delta + row - col
    bias = _head_bias(_t5_bucket(dist), rel_ref)
    o_ref[...] = jnp.where((dist >= 0)[None], bias, NEG)


def _band_table(rel_t, tq):
    nd = -(-(REL_MAX_DIST + LANES - 1) // tq) + BAND_OFF + 1
    return pl.pallas_call(
        _band_kernel,
        grid=(nd,),
        in_specs=[pl.BlockSpec(rel_t.shape, lambda d: (0, 0))],
        out_specs=pl.BlockSpec((None, NSA_HEADS, tq, LANES), lambda d: (d, 0, 0, 0)),
        out_shape=jax.ShapeDtypeStruct((nd, NSA_HEADS, tq, LANES), F32),
        compiler_params=_params(("parallel",)),
        name="nsa_bias_band",
    )(rel_t)


def _selattn_kernel(q_ref, selb_ref, ks_ref, vs_ref, band_ref, sp_ref, os_ref,
                    qaug_ref, msel_ref, s_ref, m_ref, acc_ref):
    tq = q_ref.shape[0]
    R = NSA_HEADS * tq
    nb = selb_ref.shape[1] // 2
    n_kt = msel_ref.shape[0] - 1
    tk = ks_ref.shape[0] // (n_kt + 1)
    nd = band_ref.shape[0]
    q0 = pl.program_id(1) * tq
    qaug_ref[:, 0:LANES] = _stack_heads(q_ref[...])
    selb2 = jnp.concatenate([selb_ref[:, 0:nb], selb_ref[:, nb:2 * nb]], axis=0)
    spread = _dot(selb2, sp_ref[...])
    for c in range(n_kt):
        msel_ref[c] = spread[:, c * LANES:(c + 1) * LANES].astype(BF16)
    msel_ref[n_kt] = jnp.zeros(msel_ref.shape[1:], BF16)
    m_ref[...] = jnp.full(m_ref.shape, NEG, F32)
    acc_ref[...] = jnp.zeros_like(acc_ref)

    def scores(c, slot):
        c = jnp.minimum(c, n_kt)
        m2 = msel_ref[c]
        qaug_ref[:, LANES:2 * LANES] = jnp.concatenate(
            [m2[0:tq]] * NSA_GROUP + [m2[tq:2 * tq]] * NSA_GROUP, axis=0)
        s_ref[slot] = _dot_nt(qaug_ref[...], ks_ref[pl.ds(pl.multiple_of(c * tk, tk), tk), :])

    def accumulate(c, slot):
        k0 = pl.multiple_of(c * tk, tk)
        s3 = s_ref[slot].reshape(NSA_HEADS, tq, tk)
        parts = []
        for ch in range(tk // LANES):
            d = jnp.clip((q0 - k0) // tq - ch * (LANES // tq) + BAND_OFF, 0, nd - 1)
            parts.append(s3[:, :, ch * LANES:(ch + 1) * LANES] + band_ref[d])
        s = jnp.concatenate(parts, axis=2).reshape(R, tk)
        m_old = m_ref[...]
        m_new = jnp.maximum(m_old, jnp.max(s, axis=-1, keepdims=True))
        alpha = jnp.exp(m_old - m_new)
        p = jnp.exp(s - jnp.tile(m_new, (1, tk // LANES)))
        acc_ref[...] = jnp.tile(alpha, (1, 2)) * acc_ref[...] + _dot(p.astype(BF16), vs_ref[pl.ds(k0, tk), :])
        m_ref[...] = m_new

    def pair(i, carry):
        c = 2 * i
        scores(c + 1, 1)
        accumulate(c, 0)
        scores(c + 2, 0)
        accumulate(c + 1, 1)
        return carry

    n_tiles = (q0 + tq - 1) // tk + 1
    scores(0, 0)
    lax.fori_loop(0, (n_tiles + 1) // 2, pair, 0)
    acc = acc_ref[...]
    os_ref[...] = acc[:, 0:LANES] / acc[:, LANES:2 * LANES]


def _selattn(proj, selb, ks, vs, band, B, S):
    tq = TQ
    tk = min(TK, S)
    n_kt = S // tk
    bpt = tk // NSA_SLC_BLOCK
    nqt = S // tq
    nb = S // NSA_SLC_BLOCK
    R = NSA_HEADS * tq
    blk = (jnp.arange(S) // NSA_SLC_BLOCK) % bpt
    ext = jnp.concatenate([jax.nn.one_hot(blk, bpt, dtype=BF16), jnp.zeros((S, LANES - bpt), BF16)], axis=1)
    ks_aug = jnp.concatenate([ks, jnp.broadcast_to(ext[None], (B, S, LANES))], axis=2)
    vs_aug = jnp.concatenate([vs, jnp.ones((B, S, LANES), BF16)], axis=2)
    pad_tile = lambda t: jnp.pad(t, ((0, 0), (0, tk), (0, 0)))
    ks_aug, vs_aug = pad_tile(ks_aug), pad_tile(vs_aug)
    sp = np.zeros((nb, n_kt * LANES), np.float32)
    sp[np.arange(nb), (np.arange(nb) // bpt) * LANES + np.arange(nb) % bpt] = 1.0
    once = pl.Buffered(1)
    return pl.pallas_call(
        _selattn_kernel,
        grid=(B, nqt),
        in_specs=[pl.BlockSpec((tq, 1024), lambda b, i: (b * nqt + i, COL["nqw"] // 8)),
                  pl.BlockSpec((tq, 2 * nb), lambda b, i: (b * nqt + i, 0)),
                  pl.BlockSpec((None, S + tk, 2 * LANES), lambda b, i: (b, 0, 0), pipeline_mode=once),
                  pl.BlockSpec((None, S + tk, 2 * LANES), lambda b, i: (b, 0, 0), pipeline_mode=once),
                  pl.BlockSpec(band.shape, lambda b, i: (0, 0, 0, 0), pipeline_mode=once),
                  pl.BlockSpec(sp.shape, lambda b, i: (0, 0))],
        out_specs=pl.BlockSpec((R, LANES), lambda b, i: (b * nqt + i, 0)),
        out_shape=jax.ShapeDtypeStruct((B * S * NSA_HEADS, LANES), F32),
        scratch_shapes=[pltpu.VMEM((R, 2 * LANES), BF16), pltpu.VMEM((n_kt + 1, 2 * tq, LANES), BF16),
                        pltpu.VMEM((2, R, tk), F32), pltpu.VMEM((R, LANES), F32),
                        pltpu.VMEM((R, 2 * LANES), F32)],
        compiler_params=_params(("parallel", "arbitrary")),
        name="nsa_sel_attn",
    )(proj, selb, ks_aug, vs_aug, band, jnp.asarray(sp, BF16))


def _winattn_kernel(q_ref, sm_ref, oc_ref, os_ref, kw_ref, vw_ref, rel_ref, ng_ref, o_ref):
    tq = q_ref.shape[0]
    R = NSA_HEADS * tq
    S = kw_ref.shape[0]
    span = min(WIN_SPAN, S)
    q0 = pl.program_id(1) * tq
    start = pl.multiple_of(jnp.clip(q0 + tq - span, 0, S - span), tq)
    Q = _stack_heads(q_ref[...])
    kt = kw_ref[pl.ds(start, span), :]
    vt = vw_ref[pl.ds(start, span), :]
    row = lax.broadcasted_iota(jnp.int32, (tq, span), 0)
    col = lax.broadcasted_iota(jnp.int32, (tq, span), 1)
    dist = (q0 + row) - (start + col)
    allowed = ((dist >= 0) & (dist < NSA_WINDOW))[None]
    s3 = _dot_nt(Q, kt).reshape(NSA_HEADS, tq, span) + _head_bias(_t5_bucket(dist), rel_ref)
    s3 = jnp.where(allowed, s3, NEG)
    mx = jnp.max(s3, axis=-1, keepdims=True)
    e = jnp.exp(s3 - mx)
    p = jnp.where(allowed, e / jnp.sum(e, axis=-1, keepdims=True), 0.0)
    ow = _dot(p.reshape(R, span).astype(BF16), vt)

    gates = _sigmoid(sm_ref[...])
    lane = lax.broadcasted_iota(jnp.int32, (tq, LANES), 1)
    heads = []
    ssq = jnp.zeros((tq, 1), F32)
    for h in range(NSA_HEADS):
        rs = slice(h * tq, (h + 1) * tq)
        g = [gates[:, GATE_LANE0 + 3 * h + br:GATE_LANE0 + 3 * h + br + 1] for br in range(3)]
        oh = g[0] * oc_ref[rs, :] + g[1] * os_ref[rs, :] + g[2] * ow[rs, :]
        kv = h // NSA_GROUP
        valid = (lane >= kv * NSA_HEAD_DIM) & (lane < (kv + 1) * NSA_HEAD_DIM)
        oh = jnp.where(valid, oh, 0.0)
        ssq = ssq + jnp.sum(oh * oh, axis=-1, keepdims=True)
        heads.append(oh)
    rinv = lax.rsqrt(ssq / GROUP_W + 1e-6)
    o_ref[...] = (jnp.concatenate(heads, axis=1) * rinv * ng_ref[...]).astype(o_ref.dtype)


def _winattn(proj, oc, os_, kw, vw, rel_t, ngw, B, S):
    tq = TQ
    nqt = S // tq
    R = NSA_HEADS * tq
    return pl.pallas_call(
        _winattn_kernel,
        grid=(B, nqt),
        in_specs=[pl.BlockSpec((tq, 1024), lambda b, i: (b * nqt + i, COL["nqw"] // 8)),
                  pl.BlockSpec((tq, LANES), lambda b, i: (b * nqt + i, COL["small"])),
                  pl.BlockSpec((R, LANES), lambda b, i: (b * nqt + i, 0)),
                  pl.BlockSpec((R, LANES), lambda b, i: (b * nqt + i, 0)),
                  pl.BlockSpec((None, S, LANES), lambda b, i: (b, 0, 0)),
                  pl.BlockSpec((None, S, LANES), lambda b, i: (b, 0, 0)),
                  pl.BlockSpec(rel_t.shape, lambda b, i: (0, 0)),
                  pl.BlockSpec(ngw.shape, lambda b, i: (0, 0))],
        out_specs=pl.BlockSpec((tq, NSA_HEADS * LANES), lambda b, i: (b * nqt + i, 0)),
        out_shape=jax.ShapeDtypeStruct((B * S, NSA_HEADS * LANES), BF16),
        compiler_params=_params(("parallel", "parallel")),
        name="nsa_win_attn_merge",
    )(proj, proj, oc, os_, kw, vw, rel_t, ngw)


def _widen_heads(x, axis):
    x = jnp.moveaxis(x, axis, -1)
    lead = x.shape[:-1]
    x = x.reshape(*lead, NSA_KV, NSA_GROUP, 1, NSA_HEAD_DIM)
    sel = jnp.eye(NSA_KV, dtype=x.dtype).reshape(NSA_KV, 1, NSA_KV, 1)
    x = (x * sel).reshape(*lead, NSA_HEADS * LANES)
    return jnp.moveaxis(x, -1, axis)


def _build_w_in(w):
    (hq, hf, hi, hg, nq, nkc, nvc, nks, nvs, nkw, nvw, ngate,
     sz, sxbc, sdt, rq, rk, rv, rg) = jnp.split(w, IN_SPLITS, axis=1)
    D = w.shape[0]
    nqw = _widen_heads(nq * NSA_HEAD_DIM ** -0.5, 1)
    perm = np.concatenate([np.arange(0, RET_DK, 2), np.arange(1, RET_DK, 2)])
    deint = lambda t: t.reshape(D, RET_HEADS, RET_DK)[:, :, perm].reshape(D, GROUP_W)
    small = jnp.concatenate([ngate, sdt, jnp.zeros((D, LANES - 32), w.dtype)], axis=1)
    pad = jnp.zeros((D, LANES), w.dtype)
    cols = [hq, hf, hi, hg, nqw, sxbc, sz, deint(rq), deint(rk), rv, rg,
            nkc, nvc, nks, nvs, nkw, nvw, small, pad]
    return jnp.concatenate(cols, axis=1).astype(BF16)


def _build_cmp_weights(pe, w1, w2):
    w1r = w1.reshape(2, NSA_CMP_STRIDE, NSA_HEAD_DIM, NSA_CMP_HIDDEN)
    eye = jnp.eye(NSA_KV, dtype=w1.dtype)
    big = jnp.einsum("ardc,kj->arkdjc", w1r, eye).reshape(
        2, NSA_CMP_STRIDE * NSA_KV * NSA_HEAD_DIM, NSA_KV * NSA_CMP_HIDDEN)
    w2bd = jnp.einsum("cd,kj->kcjd", w2, eye).reshape(NSA_KV * NSA_CMP_HIDDEN, NSA_KV * NSA_HEAD_DIM)
    per = pe.reshape(2, NSA_CMP_STRIDE, 1, NSA_HEAD_DIM)
    pe2 = jnp.broadcast_to(per, (2, NSA_CMP_STRIDE, NSA_KV, NSA_HEAD_DIM)).reshape(2, -1)
    return pe2, big[0].astype(BF16), big[1].astype(BF16), w2bd.astype(BF16)


def _rotary_tables(S):
    half = RET_DK // 2
    theta = 1.0 / (10000.0 ** jnp.linspace(0.0, 1.0, half, dtype=F32))
    ang = jnp.arange(S, dtype=F32)[:, None] * theta[None, :]
    cos, sin = jnp.cos(ang), jnp.sin(ang)
    return jnp.concatenate([cos, cos], axis=1), jnp.concatenate([-sin, sin], axis=1)


def _mixer(x2, B, S, l, p, lower_bounds, rel_t, band, cos_t, sin_t):
    T = B * S
    proj = _proj(x2, _build_w_in(p["w_in"][l]), tm=min(512, T), tn=1536)
    row = lambda v: v.reshape(1, -1).astype(F32)

    lb = lower_bounds[l].astype(F32)
    o_a = _hgrn(proj, B, S, row(jnp.log(lb)), row(jnp.log1p(-lb)), row(1.0 - lb),
                row(p["hgrn_norm_g"][l]))

    col = lambda name: proj[:, COL[name] * LANES:(COL[name] + 1) * LANES]
    nb = S // NSA_SLC_BLOCK
    grp = lambda name: col(name).reshape(B, nb, 4 * NSA_CMP_STRIDE * LANES)
    kcmp = _compress(grp("nkc"), *_build_cmp_weights(p["nsa_pe_k"][l], p["nsa_w1_k"][l], p["nsa_w2_k"][l]))
    vcmp = _compress(grp("nvc"), *_build_cmp_weights(p["nsa_pe_v"][l], p["nsa_w1_v"][l], p["nsa_w2_v"][l]))
    o_cmp, selb = _cmpattn(proj, kcmp, vcmp, rel_t, B, S)
    seq = lambda name: col(name).astype(BF16).reshape(B, S, LANES)
    o_sel = _selattn(proj, selb, seq("nks"), seq("nvs"), band, B, S)
    ngw = _widen_heads(p["nsa_norm_g"][l].astype(F32), 0).reshape(1, -1)
    o_b = _winattn(proj, o_cmp, o_sel, seq("nkw"), seq("nvw"), rel_t, ngw, B, S)

    lane_vec = lambda v: jnp.zeros((1, LANES), F32).at[0, DT_LANE0:DT_LANE0 + SSM_HEADS].set(v.astype(F32))
    o_c = _ssd(proj, B, S, p["ssm_conv_w"][l].astype(F32), row(p["ssm_conv_b"][l]),
               lane_vec(p["ssm_dt_bias"][l]), lane_vec(-jnp.exp(p["ssm_a_log"][l].astype(F32))),
               row(jnp.repeat(p["ssm_d"][l].astype(F32), SSM_HEAD_DIM)), row(p["ssm_norm_g"][l]))

    o_d = _retention(proj, B, S, cos_t, sin_t)

    w_out = p["w_out"][l]
    wa, wb, wc, wd = (w_out[i * GROUP_W:(i + 1) * GROUP_W] for i in range(4))
    return o_a, o_b, o_c, o_d, wa.astype(BF16), _widen_heads(wb, 0).astype(BF16), wc.astype(BF16), wd.astype(BF16)


def kernel(x, ln1_g, ln1_b, ffn1_w1, ffn1_w3, ffn1_w2, ln2_g, ln2_b, w_in, w_out, hgrn_lb_logits, hgrn_norm_g, nsa_pe_k, nsa_w1_k, nsa_w2_k, nsa_pe_v, nsa_w1_v, nsa_w2_v, nsa_norm_g, rel_bias, ssm_conv_w, ssm_conv_b, ssm_dt_bias, ssm_a_log, ssm_d, ssm_norm_g, ln3_g, ln3_b, ffn2_w1, ffn2_w3, ffn2_w2):
    B, S, D = x.shape
    T = B * S
    depth = w_in.shape[0]
    p = dict(w_in=w_in, w_out=w_out, hgrn_norm_g=hgrn_norm_g, nsa_pe_k=nsa_pe_k, nsa_w1_k=nsa_w1_k,
             nsa_w2_k=nsa_w2_k, nsa_pe_v=nsa_pe_v, nsa_w1_v=nsa_w1_v, nsa_w2_v=nsa_w2_v,
             nsa_norm_g=nsa_norm_g, ssm_conv_w=ssm_conv_w, ssm_conv_b=ssm_conv_b,
             ssm_dt_bias=ssm_dt_bias, ssm_a_log=ssm_a_log, ssm_d=ssm_d, ssm_norm_g=ssm_norm_g)
    cum = jnp.cumsum(jax.nn.softmax(hgrn_lb_logits.astype(F32), axis=0), axis=0)
    lower_bounds = cum - cum[:1]
    rel_t = jnp.zeros((NSA_HEADS, LANES), F32).at[:, :REL_BUCKETS].set(rel_bias.astype(F32).T)
    band = _band_table(rel_t, TQ)
    cos_t, sin_t = _rotary_tables(S)
    row = lambda v: v.reshape(1, -1).astype(F32)
    tm = min(512, T)
    tf = 512 if ffn1_w1.shape[2] % 512 == 0 else ffn1_w1.shape[2]
    x2 = x.reshape(T, D).astype(F32)
    for l in range(depth):
        x2 = _ffn(x2, ffn1_w1[l].astype(BF16), ffn1_w3[l].astype(BF16), ffn1_w2[l].astype(BF16),
                  row(ln1_g[l]), row(ln1_b[l]), tm, tf)
        o_a, o_b, o_c, o_d, wa, wb, wc, wd = _mixer(x2, B, S, l, p, lower_bounds, rel_t, band, cos_t, sin_t)
        x2 = _outproj(x2, o_a, o_b, o_c, o_d, wa, wb, wc, wd, row(ln2_g[l]), row(ln2_b[l]), min(256, T))
        x2 = _ffn(x2, ffn2_w1[l].astype(BF16), ffn2_w3[l].astype(BF16), ffn2_w2[l].astype(BF16),
                  row(ln3_g[l]), row(ln3_b[l]), tm, tf)
    return x2.reshape(B, S, D).astype(x.dtype)
```

```python
import functools
import math

import numpy as np
import jax
import jax.numpy as jnp
from jax import lax
from jax.experimental import pallas as pl
from jax.experimental.pallas import tpu as pltpu

F32 = jnp.float32
BF16 = jnp.bfloat16
HIGHEST = lax.Precision.HIGHEST

D_MODEL = 2048
DEPTH = 2
GROUP_W = 512
ALPHA = (2 * DEPTH) ** 0.25
HG_HEADS = 4
NSA_HEADS = 8
NSA_KV = 2
NSA_GROUP = 4
NSA_HEAD_DIM = 64
NSA_CMP_STRIDE = 16
NSA_CMP_BLOCK = 32
NSA_SLC_BLOCK = 64
NSA_TOP_N = 16
NSA_WINDOW = 512
NSA_CMP_HIDDEN = 256
SSM_HEADS = 8
SSM_HEAD_DIM = 64
SSM_GROUPS = 2
SSM_STATE = 128
SSM_CONV = 4
RET_HEADS = 4
RET_DK = 128
REL_BUCKETS = 32
REL_EXACT = 16
REL_MAX_DIST = 2048
IN_SIZES = ((GROUP_W,) * 4 + (GROUP_W,) + (128,) * 6 + (24,)
            + (GROUP_W, 1024, SSM_HEADS) + (GROUP_W,) * 4)
IN_SPLITS = tuple(int(v) for v in np.cumsum(IN_SIZES)[:-1])

LANES = 128
VMEM_LIMIT = 56 * 1024 * 1024

COL = dict(hq=0, hf=4, hi=8, hg=12, nqw=16, sxbc=24, sz=32, rq=36, rk=40, rv=44, rg=48,
           nkc=52, nvc=53, nks=54, nvs=55, nkw=56, nvw=57, small=58)
NCOL = 60
GATE_LANE0 = 0
DT_LANE0 = 24

CHUNK = 128
TQ = 64
TK = 1024
BAND_TK = 256
BAND_OFF = 2
WIN_SPAN = NSA_WINDOW + 2 * TQ
NEG = -1e30


def _params(sem):
    return pltpu.CompilerParams(dimension_semantics=sem, vmem_limit_bytes=VMEM_LIMIT)


def _dot(a, b):
    return jnp.dot(a, b, preferred_element_type=F32)


def _dot_nt(a, b):
    return lax.dot_general(a, b, (((1,), (1,)), ((), ())), preferred_element_type=F32)


def _dot_exact(a, b):
    return jnp.dot(a, b, precision=HIGHEST, preferred_element_type=F32)


def _sigmoid(x):
    return 1.0 / (1.0 + jnp.exp(-x))


def _silu(x):
    return x * _sigmoid(x)


def _softplus(x):
    return jnp.maximum(x, 0.0) + jnp.log1p(jnp.exp(-jnp.abs(x)))


def _layer_norm(r, g, b):
    mu = jnp.mean(r, axis=-1, keepdims=True)
    d = r - mu
    var = jnp.mean(d * d, axis=-1, keepdims=True)
    return d * lax.rsqrt(var + 1e-5) * g + b


def _ffn_kernel(x_ref, w1_ref, w3_ref, w2_ref, g_ref, b_ref, o_ref, acc_ref, xb_ref):
    j = pl.program_id(1)

    @pl.when(j == 0)
    def _():
        xb_ref[...] = x_ref[...].astype(BF16)
        acc_ref[...] = jnp.zeros_like(acc_ref)

    xb = xb_ref[...]
    h1 = _dot(xb, w1_ref[...])
    h3 = _dot(xb, w3_ref[...])
    a = (_silu(h1) * h3).astype(BF16)
    acc_ref[...] += _dot(a, w2_ref[...])

    @pl.when(j == pl.num_programs(1) - 1)
    def _():
        r = ALPHA * x_ref[...] + 0.5 * acc_ref[...]
        o_ref[...] = _layer_norm(r, g_ref[...], b_ref[...])


def _ffn(x, w1, w3, w2, g, b, tm, tf):
    T, D = x.shape
    F = w1.shape[1]
    return pl.pallas_call(
        _ffn_kernel,
        grid=(T // tm, F // tf),
        in_specs=[
            pl.BlockSpec((tm, D), lambda i, j: (i, 0)),
            pl.BlockSpec((D, tf), lambda i, j: (0, j)),
            pl.BlockSpec((D, tf), lambda i, j: (0, j)),
            pl.BlockSpec((tf, D), lambda i, j: (j, 0)),
            pl.BlockSpec((1, D), lambda i, j: (0, 0)),
            pl.BlockSpec((1, D), lambda i, j: (0, 0)),
        ],
        out_specs=pl.BlockSpec((tm, D), lambda i, j: (i, 0)),
        out_shape=jax.ShapeDtypeStruct((T, D), F32),
        scratch_shapes=[pltpu.VMEM((tm, D), F32), pltpu.VMEM((tm, D), BF16)],
        compiler_params=_params(("parallel", "arbitrary")),
        name="ffn_ln",
    )(x, w1, w3, w2, g, b)


def _proj_kernel(x_ref, w_ref, o_ref, xb_ref):
    @pl.when(pl.program_id(1) == 0)
    def _():
        xb_ref[...] = x_ref[...].astype(BF16)

    o_ref[...] = _dot(xb_ref[...], w_ref[...])


def _proj(x, w, tm, tn):
    T, D = x.shape
    N = w.shape[1]
    return pl.pallas_call(
        _proj_kernel,
        grid=(T // tm, N // tn),
        in_specs=[pl.BlockSpec((tm, D), lambda i, j: (i, 0)),
                  pl.BlockSpec((D, tn), lambda i, j: (0, j))],
        out_specs=pl.BlockSpec((tm, tn), lambda i, j: (i, j)),
        out_shape=jax.ShapeDtypeStruct((T, N), F32),
        scratch_shapes=[pltpu.VMEM((tm, D), BF16)],
        compiler_params=_params(("parallel", "arbitrary")),
        name="in_proj",
    )(x, w)


def _outproj_kernel(x_ref, oa_ref, ob_ref, oc_ref, od_ref, wa_ref, wb_ref, wc_ref, wd_ref,
                    g_ref, b_ref, o_ref):
    mix = (_dot(oa_ref[...], wa_ref[...]) + _dot(ob_ref[...], wb_ref[...])
           + _dot(oc_ref[...], wc_ref[...]) + _dot(od_ref[...], wd_ref[...]))
    o_ref[...] = _layer_norm(ALPHA * x_ref[...] + mix, g_ref[...], b_ref[...])


def _outproj(x, oa, ob, oc, od, wa, wb, wc, wd, g, b, tm):
    T, D = x.shape
    row = lambda a: pl.BlockSpec((tm, a.shape[1]), lambda i: (i, 0))
    full = lambda a: pl.BlockSpec(a.shape, lambda i: (0, 0))
    return pl.pallas_call(
        _outproj_kernel,
        grid=(T // tm,),
        in_specs=[row(x), row(oa), row(ob), row(oc), row(od),
                  full(wa), full(wb), full(wc), full(wd), full(g), full(b)],
        out_specs=row(x),
        out_shape=jax.ShapeDtypeStruct((T, D), F32),
        compiler_params=_params(("parallel",)),
        name="out_proj_ln",
    )(x, oa, ob, oc, od, wa, wb, wc, wd, g, b)


def _hgrn_tables(C):
    i = np.arange(C)[:, None]
    ip = np.arange(C)[None, :]
    seg = [(ip <= i),
           (ip > i)]
    masks = [np.eye(C, dtype=bool)]
    s = C // 2
    while s >= 1:
        blk = i // s
        seg.append(np.where(blk % 2 == 1, (ip > blk * s) & (ip <= i), (ip > i) & (ip <= (blk + 1) * s)))
        masks.append((blk % 2 == 1) & (ip // s == blk - 1))
        s //= 2
    seg = np.concatenate([x.astype(np.float32) for x in seg], axis=0)
    return seg, np.stack([m.astype(np.float32) for m in masks])


def _segment_sums(onehot, x):
    hi = x.astype(BF16)
    r1 = x - hi.astype(F32)
    mid = r1.astype(BF16)
    lo = (r1 - mid.astype(F32)).astype(BF16)
    n = x.shape[1]
    y = _dot(onehot, jnp.concatenate([hi, mid, lo], axis=1))
    return y[:, 0:n] + y[:, n:2 * n] + y[:, 2 * n:3 * n]


def _hgrn_kernel(q_ref, f_ref, i_ref, g_ref, llb_ref, l1m_ref, oml_ref, ng_ref,
                 seg_ref, msk_ref, o_ref, st_ref):
    @pl.when(pl.program_id(1) == 0)
    def _():
        st_ref[...] = jnp.zeros_like(st_ref)

    C = q_ref.shape[0]
    nlev = msk_ref.shape[0] - 1
    q = _silu(q_ref[...])
    z = f_ref[...]
    log_sig = jnp.minimum(z, 0.0) - jnp.log1p(jnp.exp(-jnp.abs(z)))
    cc = l1m_ref[...] + log_sig
    llb = llb_ref[...]
    logf = jnp.maximum(llb, cc) + jnp.log1p(jnp.exp(-jnp.abs(llb - cc)))
    k = oml_ref[...] * _sigmoid(-z)
    v = i_ref[...]
    seg = _segment_sums(seg_ref[...], logf)
    outs = []
    for h in range(HG_HEADS):
        sl = slice(h * LANES, (h + 1) * LANES)
        qh, kh, vh = q[:, sl], k[:, sl], v[:, sl]
        a = msk_ref[0] * _dot_nt(qh.astype(BF16), kh.astype(BF16))
        for l in range(nlev):
            dec = jnp.exp(seg[(2 + l) * C:(3 + l) * C, sl])
            a = a + msk_ref[1 + l] * _dot_nt((qh * dec).astype(BF16), (kh * dec).astype(BF16))
        b = seg[0:C, sl]
        st = st_ref[h]
        o = _dot(a.astype(BF16), vh.astype(BF16))
        o = o + _dot_nt((qh * jnp.exp(b)).astype(BF16), st.astype(BF16))
        kd = (kh * jnp.exp(seg[C:2 * C, sl])).astype(BF16)
        st_ref[h] = st * jnp.exp(b[C - 1:C, :]) + _dot(vh.T.astype(BF16), kd)
        outs.append(o * lax.rsqrt(jnp.mean(o * o, axis=-1, keepdims=True) + 1e-6))
    o = jnp.concatenate(outs, axis=1)
    o_ref[...] = (o * ng_ref[...] * _silu(g_ref[...])).astype(o_ref.dtype)


def _hgrn(proj, B, S, llb, l1m, oml, ng):
    C = CHUNK
    nc = S // C
    seg, msk = _hgrn_tables(C)
    seg, msk = jnp.asarray(seg, BF16), jnp.asarray(msk)
    col = lambda name: pl.BlockSpec((C, GROUP_W), lambda b, c, n=COL[name] // 4: (b * nc + c, n))
    vec = pl.BlockSpec((1, GROUP_W), lambda b, c: (0, 0))
    full2 = lambda a: pl.BlockSpec(a.shape, lambda b, c: (0, 0))
    return pl.pallas_call(
        _hgrn_kernel,
        grid=(B, nc),
        in_specs=[col("hq"), col("hf"), col("hi"), col("hg"), vec, vec, vec, vec,
                  full2(seg), pl.BlockSpec(msk.shape, lambda b, c: (0, 0, 0))],
        out_specs=pl.BlockSpec((C, GROUP_W), lambda b, c: (b * nc + c, 0)),
        out_shape=jax.ShapeDtypeStruct((B * S, GROUP_W), BF16),
        scratch_shapes=[pltpu.VMEM((HG_HEADS, LANES, LANES), F32)],
        compiler_params=_params(("parallel", "arbitrary")),
        name="hgrn2",
    )(proj, proj, proj, proj, llb, l1m, oml, ng, seg, msk)


def _ssd_kernel(z_ref, xbc_ref, sm_ref, cw_ref, cb_ref, dtb_ref, aneg_ref, dsk_ref, ng_ref,
                ex_ref, o_ref, tail_ref, st_ref):
    @pl.when(pl.program_id(1) == 0)
    def _():
        tail_ref[...] = jnp.zeros_like(tail_ref)
        st_ref[...] = jnp.zeros_like(st_ref)

    L = xbc_ref.shape[0]
    x = xbc_ref[...]
    xe = jnp.concatenate([tail_ref[...], x], axis=0)
    cw = cw_ref[...]
    conv = cb_ref[...]
    for kk in range(SSM_CONV):
        conv = conv + cw[kk:kk + 1, :] * xe[5 + kk:5 + kk + L, :]
    tail_ref[...] = x[L - 8:L, :]
    conv = _silu(conv)
    xs = conv[:, 0:GROUP_W]
    bm = conv[:, GROUP_W:GROUP_W + 256]
    cm = conv[:, GROUP_W + 256:GROUP_W + 512]

    dtf = _softplus(sm_ref[...] + dtb_ref[...])
    la = dtf * aneg_ref[...]
    ri = lax.broadcasted_iota(jnp.int32, (L, L), 0)
    ci = lax.broadcasted_iota(jnp.int32, (L, L), 1)
    tri = ri >= ci
    bfull = _dot_exact(tri.astype(F32), la)
    ex = ex_ref[...]
    bexp = _dot_exact(bfull, ex)
    dtexp = _dot_exact(dtf, ex)
    b_t = bfull.T
    xdt = xs * dtexp
    lane = lax.broadcasted_iota(jnp.int32, (L, LANES), 1)

    scores = []
    for g in range(SSM_GROUPS):
        cg = cm[:, g * SSM_STATE:(g + 1) * SSM_STATE].astype(BF16)
        bg = bm[:, g * SSM_STATE:(g + 1) * SSM_STATE].astype(BF16)
        cb = _dot_nt(cg, bg)
        for hh in range(SSM_HEADS // SSM_GROUPS):
            h = g * (SSM_HEADS // SSM_GROUPS) + hh
            bcol = bfull[:, DT_LANE0 + h:DT_LANE0 + h + 1]
            brow = b_t[DT_LANE0 + h:DT_LANE0 + h + 1, :]
            dec = jnp.exp(jnp.where(tri, bcol - brow, NEG))
            scores.append((cb * dec).astype(BF16))
    y_pairs = []
    for u in range(SSM_HEADS // 2):
        slab = xdt[:, u * LANES:(u + 1) * LANES]
        lo = jnp.where(lane < SSM_HEAD_DIM, slab, 0.0).astype(BF16)
        hi = jnp.where(lane >= SSM_HEAD_DIM, slab, 0.0).astype(BF16)
        y_pairs.append(_dot(scores[2 * u], lo) + _dot(scores[2 * u + 1], hi))
    y_intra = jnp.concatenate(y_pairs, axis=1)

    blast = bexp[L - 1:L, :]
    w = (xdt * jnp.exp(blast - bexp)).astype(BF16)
    y_inter = []
    for g in range(SSM_GROUPS):
        gs = slice(g * 256, (g + 1) * 256)
        cg = cm[:, g * SSM_STATE:(g + 1) * SSM_STATE].astype(BF16)
        st = st_ref[g]
        y_inter.append(_dot(cg, st.astype(BF16)))
        bg_t = bm[:, g * SSM_STATE:(g + 1) * SSM_STATE].T.astype(BF16)
        st_ref[g] = st * jnp.exp(blast[:, gs]) + _dot(bg_t, w[:, gs])
    y = y_intra + jnp.concatenate(y_inter, axis=1) * jnp.exp(bexp) + dsk_ref[...] * xs
    y = y * _silu(z_ref[...])
    halves = []
    for g in range(SSM_GROUPS):
        seg = y[:, g * 256:(g + 1) * 256]
        halves.append(seg * lax.rsqrt(jnp.mean(seg * seg, axis=-1, keepdims=True) + 1e-6))
    o_ref[...] = (jnp.concatenate(halves, axis=1) * ng_ref[...]).astype(o_ref.dtype)


def _ssd(proj, B, S, cw, cb, dtb, aneg, dsk, ng):
    L = CHUNK
    nc = S // L
    ex = np.zeros((LANES, GROUP_W), np.float32)
    for h in range(SSM_HEADS):
        ex[DT_LANE0 + h, h * SSM_HEAD_DIM:(h + 1) * SSM_HEAD_DIM] = 1.0
    ex = jnp.asarray(ex)
    full2 = lambda a: pl.BlockSpec(a.shape, lambda b, c: (0, 0))
    return pl.pallas_call(
        _ssd_kernel,
        grid=(B, nc),
        in_specs=[
            pl.BlockSpec((L, GROUP_W), lambda b, c: (b * nc + c, COL["sz"] // 4)),
            pl.BlockSpec((L, 1024), lambda b, c: (b * nc + c, COL["sxbc"] // 8)),
            pl.BlockSpec((L, LANES), lambda b, c: (b * nc + c, COL["small"])),
            full2(cw), full2(cb), full2(dtb), full2(aneg), full2(dsk), full2(ng), full2(ex)],
        out_specs=pl.BlockSpec((L, GROUP_W), lambda b, c: (b * nc + c, 0)),
        out_shape=jax.ShapeDtypeStruct((B * S, GROUP_W), BF16),
        scratch_shapes=[pltpu.VMEM((8, 1024), F32), pltpu.VMEM((SSM_GROUPS, SSM_STATE, 256), F32)],
        compiler_params=_params(("parallel", "arbitrary")),
        name="ssd",
    )(proj, proj, proj, cw, cb, dtb, aneg, dsk, ng, ex)


def _ret_kernel(q_ref, k_ref, v_ref, g_ref, cos_ref, sin_ref, dec_ref, qs_ref, ks_ref, sd_ref,
                o_ref, st_ref):
    @pl.when(pl.program_id(1) == 0)
    def _():
        st_ref[...] = jnp.zeros_like(st_ref)

    cos = cos_ref[...]
    sin = sin_ref[...]
    outs = []
    for h in range(RET_HEADS):
        sl = slice(h * LANES, (h + 1) * LANES)
        qh = q_ref[:, sl]
        kh = k_ref[:, sl]
        qh = qh * cos + pltpu.roll(qh, RET_DK // 2, axis=1) * sin
        kh = (kh * cos + pltpu.roll(kh, RET_DK // 2, axis=1) * sin) * (RET_DK ** -0.5)
        vh = v_ref[:, sl].astype(BF16)
        sc = (_dot_nt(qh.astype(BF16), kh.astype(BF16)) * dec_ref[h]).astype(BF16)
        st = st_ref[h]
        y = _dot(sc, vh) + _dot((qh * qs_ref[:, sl]).astype(BF16), st.astype(BF16))
        kd_t = (kh * ks_ref[:, sl]).T.astype(BF16)
        st_ref[h] = st * sd_ref[h] + _dot(kd_t, vh)
        mu = jnp.mean(y, axis=-1, keepdims=True)
        d = y - mu
        outs.append(d * lax.rsqrt(jnp.mean(d * d, axis=-1, keepdims=True) + 1e-5))
    o_ref[...] = (_silu(g_ref[...]) * jnp.concatenate(outs, axis=1)).astype(o_ref.dtype)


def _retention(proj, B, S, cos_t, sin_t):
    L = CHUNK
    nc = S // L
    lg = jnp.log(1.0 - 2.0 ** (-5.0 - jnp.arange(RET_HEADS, dtype=F32)))
    i = jnp.arange(L, dtype=F32)
    diff = i[:, None] - i[None, :]
    dec = jnp.where(diff >= 0, jnp.exp(lg[:, None, None] * jnp.maximum(diff, 0.0)), 0.0)
    rep = lambda t: jnp.repeat(t, LANES, axis=1)
    qs = rep(jnp.exp((i[:, None] + 1.0) * lg[None, :]))
    ks = rep(jnp.exp((L - 1.0 - i[:, None]) * lg[None, :]))
    sd = jnp.broadcast_to(jnp.exp(L * lg)[:, None, None], (RET_HEADS, LANES, LANES))
    col = lambda name: pl.BlockSpec((L, GROUP_W), lambda b, c, n=COL[name] // 4: (b * nc + c, n))
    return pl.pallas_call(
        _ret_kernel,
        grid=(B, nc),
        in_specs=[col("rq"), col("rk"), col("rv"), col("rg"),
                  pl.BlockSpec((L, LANES), lambda b, c: (c, 0)),
                  pl.BlockSpec((L, LANES), lambda b, c: (c, 0)),
                  pl.BlockSpec((RET_HEADS, L, L), lambda b, c: (0, 0, 0)),
                  pl.BlockSpec((L, GROUP_W), lambda b, c: (0, 0)),
                  pl.BlockSpec((L, GROUP_W), lambda b, c: (0, 0)),
                  pl.BlockSpec((RET_HEADS, LANES, LANES), lambda b, c: (0, 0, 0))],
        out_specs=pl.BlockSpec((L, GROUP_W), lambda b, c: (b * nc + c, 0)),
        out_shape=jax.ShapeDtypeStruct((B * S, GROUP_W), BF16),
        scratch_shapes=[pltpu.VMEM((RET_HEADS, RET_DK, RET_DK), F32)],
        compiler_params=_params(("parallel", "arbitrary")),
        name="retention",
    )(proj, proj, proj, proj, cos_t, sin_t, dec, qs, ks, sd)


def _t5_bucket(dist):
    n = jnp.maximum(dist, 0)
    nf = jnp.maximum(n, 1).astype(F32)
    large = REL_EXACT + (jnp.log(nf / REL_EXACT) / math.log(REL_MAX_DIST / REL_EXACT)
                         * (REL_BUCKETS - REL_EXACT)).astype(jnp.int32)
    return jnp.where(n < REL_EXACT, n, jnp.minimum(large, REL_BUCKETS - 1))


def _head_bias(bucket, rel_ref):
    rows, cols = bucket.shape
    per_head = []
    for h in range(NSA_HEADS):
        tbl = jnp.broadcast_to(rel_ref[h:h + 1, :], (rows, LANES))
        chunks = [jnp.take_along_axis(tbl, bucket[:, c:c + LANES], axis=1)
                  for c in range(0, cols, LANES)]
        per_head.append(chunks[0] if len(chunks) == 1 else jnp.concatenate(chunks, axis=1))
    return jnp.stack(per_head, axis=0)


def _stack_heads(qw):
    return jnp.concatenate([qw[:, h * LANES:(h + 1) * LANES] for h in range(NSA_HEADS)],
                           axis=0).astype(BF16)


def _cmp_kernel(g_ref, pe_ref, w1a_ref, w1b_ref, w2_ref, o_ref):
    nb = g_ref.shape[0]
    gw = g_ref.shape[1] // 4
    pe = pe_ref[...]
    slabs = [g_ref[:, s * gw:(s + 1) * gw] for s in range(4)]
    nxt0 = pltpu.roll(slabs[0], nb - 1, axis=0)
    for s in range(4):
        a = (slabs[s] + pe[0:1, :]).astype(BF16)
        bn = ((slabs[s + 1] if s < 3 else nxt0) + pe[1:2, :]).astype(BF16)
        hid = _silu(_dot(a, w1a_ref[...]) + _dot(bn, w1b_ref[...]))
        o_ref[s * nb:(s + 1) * nb, :] = _dot(hid.astype(BF16), w2_ref[...]).astype(o_ref.dtype)


def _compress(g, pe2, w1a, w1b, w2bd):
    B, nb, gw4 = g.shape
    full2 = lambda a: pl.BlockSpec(a.shape, lambda b: (0, 0))
    return pl.pallas_call(
        _cmp_kernel,
        grid=(B,),
        in_specs=[pl.BlockSpec((None, nb, gw4), lambda b: (b, 0, 0)),
                  full2(pe2), full2(w1a), full2(w1b), full2(w2bd)],
        out_specs=pl.BlockSpec((None, 4 * nb, LANES), lambda b: (b, 0, 0)),
        out_shape=jax.ShapeDtypeStruct((B, 4 * nb, LANES), BF16),
        compiler_params=_params(("parallel",)),
        name="nsa_compress",
    )(g, pe2, w1a, w1b, w2bd)


def _cmpattn_kernel(q_ref, kc_ref, vc_ref, rel_ref, oc_ref, selb_ref):
    tq = q_ref.shape[0]
    ncmp = kc_ref.shape[0]
    nb = ncmp // 4
    R = NSA_HEADS * tq
    q0 = pl.program_id(1) * tq
    Q = _stack_heads(q_ref[...])
    s = _dot_nt(Q, kc_ref[...])
    row = lax.broadcasted_iota(jnp.int32, (tq, ncmp), 0)
    col = lax.broadcasted_iota(jnp.int32, (tq, ncmp), 1)
    m = (col % nb) * 4 + col // nb
    dist = (q0 + row) - (m * NSA_CMP_STRIDE + NSA_CMP_BLOCK - 1)
    allowed = (dist >= 0)[None]
    s3 = s.reshape(NSA_HEADS, tq, ncmp) + _head_bias(_t5_bucket(dist), rel_ref)
    s3 = jnp.where(allowed, s3, NEG)
    mx = jnp.max(s3, axis=-1, keepdims=True)
    e = jnp.exp(s3 - mx)
    live = (q0 + lax.broadcasted_iota(jnp.int32, (tq, 1), 0) >= NSA_CMP_BLOCK - 1)[None]
    p = e * jnp.where(live, 1.0 / jnp.sum(e, axis=-1, keepdims=True), 0.0)
    oc_ref[...] = _dot(p.reshape(R, ncmp).astype(BF16), vc_ref[...])

    ps = p.reshape(NSA_KV, NSA_GROUP, tq, ncmp).sum(axis=1).reshape(NSA_KV * tq, ncmp)
    p0, p1, p2, p3 = (ps[:, i * nb:(i + 1) * nb] for i in range(4))
    j = lax.broadcasted_iota(jnp.int32, (NSA_KV * tq, nb), 1)
    prev3 = jnp.where(j == 0, 0.0, pltpu.roll(p3, 1, axis=1))
    imp = p0 + p1 + p2 + 0.5 * (p3 + prev3)
    t = q0 + (lax.broadcasted_iota(jnp.int32, (NSA_KV * tq, nb), 0) % tq)
    cur = t // NSA_SLC_BLOCK
    forced = (j == 0) | (j == cur) | (j == cur - 1)
    score = jnp.where(j > cur, -1.0, jnp.where(forced, NSA_GROUP + 1.0, imp))
    score = score.T
    jf = lax.broadcasted_iota(jnp.int32, score.shape, 0).astype(F32)
    sel = score == NSA_GROUP + 1.0
    score = jnp.where(sel, -jnp.inf, score)
    for _ in range(min(NSA_TOP_N, nb) - 3):
        best = jnp.max(score, axis=0, keepdims=True)
        first = jnp.min(jnp.where(score == best, jf, float(nb)), axis=0, keepdims=True)
        hit = jf == first
        sel = sel | hit
        score = jnp.where(hit, -jnp.inf, score)
    selb = jnp.where(sel, 0.0, NEG).T.astype(selb_ref.dtype)
    selb_ref[:, 0:nb] = selb[0:tq]
    selb_ref[:, nb:2 * nb] = selb[tq:2 * tq]


def _cmpattn(proj, kcmp, vcmp, rel_t, B, S):
    tq = TQ
    nqt = S // tq
    ncmp = kcmp.shape[1]
    nb = ncmp // 4
    return pl.pallas_call(
        _cmpattn_kernel,
        grid=(B, nqt),
        in_specs=[pl.BlockSpec((tq, 1024), lambda b, i: (b * nqt + i, COL["nqw"] // 8)),
                  pl.BlockSpec((None, ncmp, LANES), lambda b, i: (b, 0, 0)),
                  pl.BlockSpec((None, ncmp, LANES), lambda b, i: (b, 0, 0)),
                  pl.BlockSpec(rel_t.shape, lambda b, i: (0, 0))],
        out_specs=[pl.BlockSpec((NSA_HEADS * tq, LANES), lambda b, i: (b * nqt + i, 0)),
                   pl.BlockSpec((tq, 2 * nb), lambda b, i: (b * nqt + i, 0))],
        out_shape=[jax.ShapeDtypeStruct((B * S * NSA_HEADS, LANES), F32),
                   jax.ShapeDtypeStruct((B * S, 2 * nb), BF16)],
        compiler_params=_params(("parallel", "parallel")),
        name="nsa_cmp_attn_topk",
    )(proj, kcmp, vcmp, rel_t)


def _band_kernel(rel_ref, o_ref):
    tq = o_ref.shape[1]
    delta = (pl.program_id(0) - BAND_OFF) * tq
    row = lax.broadcasted_iota(jnp.int32, (tq, LANES), 0)
    col = lax.broadcasted_iota(jnp.int32, (tq, LANES), 1)
    dist = delta + row - col
    bias = _head_bias(_t5_bucket(dist), rel_ref)
    o_ref[...] = jnp.where((dist >= 0)[None], bias, NEG)


def _band_table(rel_t, tq):
    nd = -(-(REL_MAX_DIST + LANES - 1) // tq) + BAND_OFF + 1
    return pl.pallas_call(
        _band_kernel,
        grid=(nd,),
        in_specs=[pl.BlockSpec(rel_t.shape, lambda d: (0, 0))],
        out_specs=pl.BlockSpec((None, NSA_HEADS, tq, LANES), lambda d: (d, 0, 0, 0)),
        out_shape=jax.ShapeDtypeStruct((nd, NSA_HEADS, tq, LANES), F32),
        compiler_params=_params(("parallel",)),
        name="nsa_bias_band",
    )(rel_t)


def _selattn_kernel(q_ref, selb_ref, ks_ref, vs_ref, band_ref, sp_ref, os_ref,
                    qaug_ref, msel_ref, s_ref, m_ref, acc_ref):
    tq = q_ref.shape[0]
    R = NSA_HEADS * tq
    nb = selb_ref.shape[1] // 2
    n_kt = msel_ref.shape[0] - 1
    tk = ks_ref.shape[0] // (n_kt + 1)
    nd = band_ref.shape[0]
    q0 = pl.program_id(1) * tq
    qaug_ref[:, 0:LANES] = _stack_heads(q_ref[...])
    selb2 = jnp.concatenate([selb_ref[:, 0:nb], selb_ref[:, nb:2 * nb]], axis=0)
    spread = _dot(selb2, sp_ref[...])
    for c in range(n_kt):
        msel_ref[c] = spread[:, c * LANES:(c + 1) * LANES].astype(BF16)
    msel_ref[n_kt] = jnp.zeros(msel_ref.shape[1:], BF16)
    m_ref[...] = jnp.full(m_ref.shape, NEG, F32)
    acc_ref[...] = jnp.zeros_like(acc_ref)

    def scores(c, slot):
        c = jnp.minimum(c, n_kt)
        m2 = msel_ref[c]
        qaug_ref[:, LANES:2 * LANES] = jnp.concatenate(
            [m2[0:tq]] * NSA_GROUP + [m2[tq:2 * tq]] * NSA_GROUP, axis=0)
        s_ref[slot] = _dot_nt(qaug_ref[...], ks_ref[pl.ds(pl.multiple_of(c * tk, tk), tk), :])

    def accumulate(c, slot):
        k0 = pl.multiple_of(c * tk, tk)
        s3 = s_ref[slot].reshape(NSA_HEADS, tq, tk)
        parts = []
        for ch in range(tk // LANES):
            d = jnp.clip((q0 - k0) // tq - ch * (LANES // tq) + BAND_OFF, 0, nd - 1)
            parts.append(s3[:, :, ch * LANES:(ch + 1) * LANES] + band_ref[d])
        s = jnp.concatenate(parts, axis=2).reshape(R, tk)
        m_old = m_ref[...]
        m_new = jnp.maximum(m_old, jnp.max(s, axis=-1, keepdims=True))
        alpha = jnp.exp(m_old - m_new)
        p = jnp.exp(s - jnp.tile(m_new, (1, tk // LANES)))
        acc_ref[...] = jnp.tile(alpha, (1, 2)) * acc_ref[...] + _dot(p.astype(BF16), vs_ref[pl.ds(k0, tk), :])
        m_ref[...] = m_new

    def pair(i, carry):
        c = 2 * i
        scores(c + 1, 1)
        accumulate(c, 0)
        scores(c + 2, 0)
        accumulate(c + 1, 1)
        return carry

    n_tiles = (q0 + tq - 1) // tk + 1
    scores(0, 0)
    lax.fori_loop(0, (n_tiles + 1) // 2, pair, 0)
    acc = acc_ref[...]
    os_ref[...] = acc[:, 0:LANES] / acc[:, LANES:2 * LANES]


def _selattn(proj, selb, ks, vs, band, B, S):
    tq = TQ
    tk = min(TK, S)
    n_kt = S // tk
    bpt = tk // NSA_SLC_BLOCK
    nqt = S // tq
    nb = S // NSA_SLC_BLOCK
    R = NSA_HEADS * tq
    blk = (jnp.arange(S) // NSA_SLC_BLOCK) % bpt
    ext = jnp.concatenate([jax.nn.one_hot(blk, bpt, dtype=BF16), jnp.zeros((S, LANES - bpt), BF16)], axis=1)
    ks_aug = jnp.concatenate([ks, jnp.broadcast_to(ext[None], (B, S, LANES))], axis=2)
    vs_aug = jnp.concatenate([vs, jnp.ones((B, S, LANES), BF16)], axis=2)
    pad_tile = lambda t: jnp.pad(t, ((0, 0), (0, tk), (0, 0)))
    ks_aug, vs_aug = pad_tile(ks_aug), pad_tile(vs_aug)
    sp = np.zeros((nb, n_kt * LANES), np.float32)
    sp[np.arange(nb), (np.arange(nb) // bpt) * LANES + np.arange(nb) % bpt] = 1.0
    once = pl.Buffered(1)
    return pl.pallas_call(
        _selattn_kernel,
        grid=(B, nqt),
        in_specs=[pl.BlockSpec((tq, 1024), lambda b, i: (b * nqt + i, COL["nqw"] // 8)),
                  pl.BlockSpec((tq, 2 * nb), lambda b, i: (b * nqt + i, 0)),
                  pl.BlockSpec((None, S + tk, 2 * LANES), lambda b, i: (b, 0, 0), pipeline_mode=once),
                  pl.BlockSpec((None, S + tk, 2 * LANES), lambda b, i: (b, 0, 0), pipeline_mode=once),
                  pl.BlockSpec(band.shape, lambda b, i: (0, 0, 0, 0), pipeline_mode=once),
                  pl.BlockSpec(sp.shape, lambda b, i: (0, 0))],
        out_specs=pl.BlockSpec((R, LANES), lambda b, i: (b * nqt + i, 0)),
        out_shape=jax.ShapeDtypeStruct((B * S * NSA_HEADS, LANES), F32),
        scratch_shapes=[pltpu.VMEM((R, 2 * LANES), BF16), pltpu.VMEM((n_kt + 1, 2 * tq, LANES), BF16),
                        pltpu.VMEM((2, R, tk), F32), pltpu.VMEM((R, LANES), F32),
                        pltpu.VMEM((R, 2 * LANES), F32)],
        compiler_params=_params(("parallel", "arbitrary")),
        name="nsa_sel_attn",
    )(proj, selb, ks_aug, vs_aug, band, jnp.asarray(sp, BF16))


def _winattn_kernel(q_ref, sm_ref, oc_ref, os_ref, kw_ref, vw_ref, band_ref, ng_ref, o_ref):
    tq = q_ref.shape[0]
    R = NSA_HEADS * tq
    S = kw_ref.shape[0]
    span = min(WIN_SPAN, S)
    nd = band_ref.shape[0]
    q0 = pl.program_id(1) * tq
    start = pl.multiple_of(jnp.clip(q0 + tq - span, 0, S - span), tq)
    Q = _stack_heads(q_ref[...])
    kt = kw_ref[pl.ds(start, span), :]
    vt = vw_ref[pl.ds(start, span), :]
    s3 = _dot_nt(Q, kt).reshape(NSA_HEADS, tq, span)
    parts = []
    for ch in range(span // LANES):
        d = jnp.clip((q0 - start) // tq - ch * (LANES // tq) + BAND_OFF, 0, nd - 1)
        parts.append(s3[:, :, ch * LANES:(ch + 1) * LANES] + band_ref[d])
    row = lax.broadcasted_iota(jnp.int32, (tq, span), 0)
    col = lax.broadcasted_iota(jnp.int32, (tq, span), 1)
    in_window = ((q0 + row) - (start + col) < NSA_WINDOW)[None]
    s3 = jnp.where(in_window, jnp.concatenate(parts, axis=2), NEG)
    mx = jnp.max(s3, axis=-1, keepdims=True)
    e = jnp.exp(s3 - mx)
    p = e * (1.0 / jnp.sum(e, axis=-1, keepdims=True))
    ow = _dot(p.reshape(R, span).astype(BF16), vt)

    gates = _sigmoid(sm_ref[...])
    lane = lax.broadcasted_iota(jnp.int32, (tq, LANES), 1)
    heads = []
    ssq = jnp.zeros((tq, 1), F32)
    for h in range(NSA_HEADS):
        rs = slice(h * tq, (h + 1) * tq)
        g = [gates[:, GATE_LANE0 + 3 * h + br:GATE_LANE0 + 3 * h + br + 1] for br in range(3)]
        oh = g[0] * oc_ref[rs, :] + g[1] * os_ref[rs, :] + g[2] * ow[rs, :]
        kv = h // NSA_GROUP
        valid = (lane >= kv * NSA_HEAD_DIM) & (lane < (kv + 1) * NSA_HEAD_DIM)
        oh = jnp.where(valid, oh, 0.0)
        ssq = ssq + jnp.sum(oh * oh, axis=-1, keepdims=True)
        heads.append(oh)
    rinv = lax.rsqrt(ssq / GROUP_W + 1e-6)
    o_ref[...] = (jnp.concatenate(heads, axis=1) * rinv * ng_ref[...]).astype(o_ref.dtype)


def _winattn(proj, oc, os_, kw, vw, band, ngw, B, S):
    tq = TQ
    nqt = S // tq
    R = NSA_HEADS * tq
    once = pl.Buffered(1)
    return pl.pallas_call(
        _winattn_kernel,
        grid=(B, nqt),
        in_specs=[pl.BlockSpec((tq, 1024), lambda b, i: (b * nqt + i, COL["nqw"] // 8)),
                  pl.BlockSpec((tq, LANES), lambda b, i: (b * nqt + i, COL["small"])),
                  pl.BlockSpec((R, LANES), lambda b, i: (b * nqt + i, 0)),
                  pl.BlockSpec((R, LANES), lambda b, i: (b * nqt + i, 0)),
                  pl.BlockSpec((None, S, LANES), lambda b, i: (b, 0, 0), pipeline_mode=once),
                  pl.BlockSpec((None, S, LANES), lambda b, i: (b, 0, 0), pipeline_mode=once),
                  pl.BlockSpec(band.shape, lambda b, i: (0, 0, 0, 0), pipeline_mode=once),
                  pl.BlockSpec(ngw.shape, lambda b, i: (0, 0))],
        out_specs=pl.BlockSpec((tq, NSA_HEADS * LANES), lambda b, i: (b * nqt + i, 0)),
        out_shape=jax.ShapeDtypeStruct((B * S, NSA_HEADS * LANES), BF16),
        compiler_params=_params(("parallel", "arbitrary")),
        name="nsa_win_attn_merge",
    )(proj, proj, oc, os_, kw, vw, band, ngw)


def _widen_heads(x, axis):
    x = jnp.moveaxis(x, axis, -1)
    lead = x.shape[:-1]
    x = x.reshape(*lead, NSA_KV, NSA_GROUP, 1, NSA_HEAD_DIM)
    sel = jnp.eye(NSA_KV, dtype=x.dtype).reshape(NSA_KV, 1, NSA_KV, 1)
    x = (x * sel).reshape(*lead, NSA_HEADS * LANES)
    return jnp.moveaxis(x, -1, axis)


def _build_w_in(w):
    (hq, hf, hi, hg, nq, nkc, nvc, nks, nvs, nkw, nvw, ngate,
     sz, sxbc, sdt, rq, rk, rv, rg) = jnp.split(w, IN_SPLITS, axis=1)
    D = w.shape[0]
    nqw = _widen_heads(nq * NSA_HEAD_DIM ** -0.5, 1)
    deint = lambda t: t.reshape(D, RET_HEADS, RET_DK // 2, 2).transpose(0, 1, 3, 2).reshape(D, GROUP_W)
    small = jnp.concatenate([ngate, sdt, jnp.zeros((D, LANES - 32), w.dtype)], axis=1)
    pad = jnp.zeros((D, LANES), w.dtype)
    cols = [hq, hf, hi, hg, nqw, sxbc, sz, deint(rq), deint(rk), rv, rg,
            nkc, nvc, nks, nvs, nkw, nvw, small, pad]
    return jnp.concatenate(cols, axis=1).astype(BF16)


def _build_cmp_weights(pe, w1, w2):
    w1r = w1.reshape(2, NSA_CMP_STRIDE, NSA_HEAD_DIM, NSA_CMP_HIDDEN)
    eye = jnp.eye(NSA_KV, dtype=w1.dtype)
    big = jnp.einsum("ardc,kj->arkdjc", w1r, eye).reshape(
        2, NSA_CMP_STRIDE * NSA_KV * NSA_HEAD_DIM, NSA_KV * NSA_CMP_HIDDEN)
    w2bd = jnp.einsum("cd,kj->kcjd", w2, eye).reshape(NSA_KV * NSA_CMP_HIDDEN, NSA_KV * NSA_HEAD_DIM)
    per = pe.reshape(2, NSA_CMP_STRIDE, 1, NSA_HEAD_DIM)
    pe2 = jnp.broadcast_to(per, (2, NSA_CMP_STRIDE, NSA_KV, NSA_HEAD_DIM)).reshape(2, -1)
    return pe2, big[0].astype(BF16), big[1].astype(BF16), w2bd.astype(BF16)


def _rotary_tables(S):
    half = RET_DK // 2
    theta = 1.0 / (10000.0 ** jnp.linspace(0.0, 1.0, half, dtype=F32))
    ang = jnp.arange(S, dtype=F32)[:, None] * theta[None, :]
    cos, sin = jnp.cos(ang), jnp.sin(ang)
    return jnp.concatenate([cos, cos], axis=1), jnp.concatenate([-sin, sin], axis=1)


def _mixer(x2, B, S, l, p, lower_bounds, rel_t, band, cos_t, sin_t):
    T = B * S
    proj = _proj(x2, _build_w_in(p["w_in"][l]), tm=min(512, T), tn=1536)
    row = lambda v: v.reshape(1, -1).astype(F32)

    lb = lower_bounds[l].astype(F32)
    o_a = _hgrn(proj, B, S, row(jnp.log(lb)), row(jnp.log1p(-lb)), row(1.0 - lb),
                row(p["hgrn_norm_g"][l]))

    col = lambda name: proj[:, COL[name] * LANES:(COL[name] + 1) * LANES]
    nb = S // NSA_SLC_BLOCK
    grp = lambda name: col(name).reshape(B, nb, 4 * NSA_CMP_STRIDE * LANES)
    kcmp = _compress(grp("nkc"), *_build_cmp_weights(p["nsa_pe_k"][l], p["nsa_w1_k"][l], p["nsa_w2_k"][l]))
    vcmp = _compress(grp("nvc"), *_build_cmp_weights(p["nsa_pe_v"][l], p["nsa_w1_v"][l], p["nsa_w2_v"][l]))
    o_cmp, selb = _cmpattn(proj, kcmp, vcmp, rel_t, B, S)
    seq = lambda name: col(name).astype(BF16).reshape(B, S, LANES)
    o_sel = _selattn(proj, selb, seq("nks"), seq("nvs"), band, B, S)
    ngw = _widen_heads(p["nsa_norm_g"][l].astype(F32), 0).reshape(1, -1)
    o_b = _winattn(proj, o_cmp, o_sel, seq("nkw"), seq("nvw"), band, ngw, B, S)

    lane_vec = lambda v: jnp.zeros((1, LANES), F32).at[0, DT_LANE0:DT_LANE0 + SSM_HEADS].set(v.astype(F32))
    o_c = _ssd(proj, B, S, p["ssm_conv_w"][l].astype(F32), row(p["ssm_conv_b"][l]),
               lane_vec(p["ssm_dt_bias"][l]), lane_vec(-jnp.exp(p["ssm_a_log"][l].astype(F32))),
               row(jnp.repeat(p["ssm_d"][l].astype(F32), SSM_HEAD_DIM)), row(p["ssm_norm_g"][l]))

    o_d = _retention(proj, B, S, cos_t, sin_t)

    w_out = p["w_out"][l]
    wa, wb, wc, wd = (w_out[i * GROUP_W:(i + 1) * GROUP_W] for i in range(4))
    return o_a, o_b, o_c, o_d, wa.astype(BF16), _widen_heads(wb, 0).astype(BF16), wc.astype(BF16), wd.astype(BF16)


def kernel(x, ln1_g, ln1_b, ffn1_w1, ffn1_w3, ffn1_w2, ln2_g, ln2_b, w_in, w_out, hgrn_lb_logits, hgrn_norm_g, nsa_pe_k, nsa_w1_k, nsa_w2_k, nsa_pe_v, nsa_w1_v, nsa_w2_v, nsa_norm_g, rel_bias, ssm_conv_w, ssm_conv_b, ssm_dt_bias, ssm_a_log, ssm_d, ssm_norm_g, ln3_g, ln3_b, ffn2_w1, ffn2_w3, ffn2_w2):
    B, S, D = x.shape
    T = B * S
    depth = w_in.shape[0]
    p = dict(w_in=w_in, w_out=w_out, hgrn_norm_g=hgrn_norm_g, nsa_pe_k=nsa_pe_k, nsa_w1_k=nsa_w1_k,
             nsa_w2_k=nsa_w2_k, nsa_pe_v=nsa_pe_v, nsa_w1_v=nsa_w1_v, nsa_w2_v=nsa_w2_v,
             nsa_norm_g=nsa_norm_g, ssm_conv_w=ssm_conv_w, ssm_conv_b=ssm_conv_b,
             ssm_dt_bias=ssm_dt_bias, ssm_a_log=ssm_a_log, ssm_d=ssm_d, ssm_norm_g=ssm_norm_g)
    cum = jnp.cumsum(jax.nn.softmax(hgrn_lb_logits.astype(F32), axis=0), axis=0)
    lower_bounds = cum - cum[:1]
    rel_t = jnp.zeros((NSA_HEADS, LANES), F32).at[:, :REL_BUCKETS].set(rel_bias.astype(F32).T)
    band = _band_table(rel_t, TQ)
    cos_t, sin_t = _rotary_tables(S)
    row = lambda v: v.reshape(1, -1).astype(F32)
    tm = min(512, T)
    tf = 512 if ffn1_w1.shape[2] % 512 == 0 else ffn1_w1.shape[2]
    x2 = x.reshape(T, D).astype(F32)
    for l in range(depth):
        x2 = _ffn(x2, ffn1_w1[l].astype(BF16), ffn1_w3[l].astype(BF16), ffn1_w2[l].astype(BF16),
                  row(ln1_g[l]), row(ln1_b[l]), tm, tf)
        o_a, o_b, o_c, o_d, wa, wb, wc, wd = _mixer(x2, B, S, l, p, lower_bounds, rel_t, band, cos_t, sin_t)
        x2 = _outproj(x2, o_a, o_b, o_c, o_d, wa, wb, wc, wd, row(ln2_g[l]), row(ln2_b[l]), min(256, T))
        x2 = _ffn(x2, ffn2_w1[l].astype(BF16), ffn2_w3[l].astype(BF16), ffn2_w2[l].astype(BF16),
                  row(ln3_g[l]), row(ln3_b[l]), tm, tf)
    return x2.reshape(B, S, D).astype(x.dtype)
```

```python
import functools
import math

import numpy as np
import jax
import jax.numpy as jnp
from jax import lax
from jax.experimental import pallas as pl
from jax.experimental.pallas import tpu as pltpu

F32 = jnp.float32
BF16 = jnp.bfloat16
HIGHEST = lax.Precision.HIGHEST

D_MODEL = 2048
DEPTH = 2
GROUP_W = 512
ALPHA = (2 * DEPTH) ** 0.25
HG_HEADS = 4
NSA_HEADS = 8
NSA_KV = 2
NSA_GROUP = 4
NSA_HEAD_DIM = 64
NSA_CMP_STRIDE = 16
NSA_CMP_BLOCK = 32
NSA_SLC_BLOCK = 64
NSA_TOP_N = 16
NSA_WINDOW = 512
NSA_CMP_HIDDEN = 256
SSM_HEADS = 8
SSM_HEAD_DIM = 64
SSM_GROUPS = 2
SSM_STATE = 128
SSM_CONV = 4
RET_HEADS = 4
RET_DK = 128
REL_BUCKETS = 32
REL_EXACT = 16
REL_MAX_DIST = 2048
IN_SIZES = ((GROUP_W,) * 4 + (GROUP_W,) + (128,) * 6 + (24,)
            + (GROUP_W, 1024, SSM_HEADS) + (GROUP_W,) * 4)
IN_SPLITS = tuple(int(v) for v in np.cumsum(IN_SIZES)[:-1])

LANES = 128
VMEM_LIMIT = 56 * 1024 * 1024

COL = dict(hq=0, hf=4, hi=8, hg=12, nqw=16, sxbc=24, sz=32, rq=36, rk=40, rv=44, rg=48)
NCOL = 52
KV_COLS = ("nkc", "nvc", "nks", "nvs", "nkw", "nvw", "small")
GATE_LANE0 = 0
DT_LANE0 = 24

CHUNK = 128
TQ = 64
TK = 1024
BAND_OFF = 2
CMP_BAND_OFF = 1
WIN_SPAN = NSA_WINDOW + 2 * TQ
NEG = -1e30


def _params(sem):
    return pltpu.CompilerParams(dimension_semantics=sem, vmem_limit_bytes=VMEM_LIMIT)


def _dot(a, b):
    return jnp.dot(a, b, preferred_element_type=F32)


def _dot_nt(a, b):
    return lax.dot_general(a, b, (((1,), (1,)), ((), ())), preferred_element_type=F32)


def _dot_exact(a, b):
    return jnp.dot(a, b, precision=HIGHEST, preferred_element_type=F32)


def _sigmoid(x):
    return 1.0 / (1.0 + jnp.exp(-x))


def _silu(x):
    return x * _sigmoid(x)


def _softplus(x):
    return jnp.maximum(x, 0.0) + jnp.log1p(jnp.exp(-jnp.abs(x)))


def _layer_norm(r, g, b):
    mu = jnp.mean(r, axis=-1, keepdims=True)
    d = r - mu
    var = jnp.mean(d * d, axis=-1, keepdims=True)
    return d * lax.rsqrt(var + 1e-5) * g + b


def _ffn_kernel(x_ref, w1_ref, w3_ref, w2_ref, g_ref, b_ref, o_ref, acc_ref, xb_ref):
    j = pl.program_id(1)

    @pl.when(j == 0)
    def _():
        xb_ref[...] = x_ref[...].astype(BF16)
        acc_ref[...] = jnp.zeros_like(acc_ref)

    xb = xb_ref[...]
    h1 = _dot(xb, w1_ref[...])
    h3 = _dot(xb, w3_ref[...])
    a = (_silu(h1) * h3).astype(BF16)
    acc_ref[...] += _dot(a, w2_ref[...])

    @pl.when(j == pl.num_programs(1) - 1)
    def _():
        r = ALPHA * x_ref[...] + 0.5 * acc_ref[...]
        o_ref[...] = _layer_norm(r, g_ref[...], b_ref[...])


def _ffn(x, w1, w3, w2, g, b, tm, tf):
    T, D = x.shape
    F = w1.shape[1]
    return pl.pallas_call(
        _ffn_kernel,
        grid=(T // tm, F // tf),
        in_specs=[
            pl.BlockSpec((tm, D), lambda i, j: (i, 0)),
            pl.BlockSpec((D, tf), lambda i, j: (0, j)),
            pl.BlockSpec((D, tf), lambda i, j: (0, j)),
            pl.BlockSpec((tf, D), lambda i, j: (j, 0)),
            pl.BlockSpec((1, D), lambda i, j: (0, 0)),
            pl.BlockSpec((1, D), lambda i, j: (0, 0)),
        ],
        out_specs=pl.BlockSpec((tm, D), lambda i, j: (i, 0)),
        out_shape=jax.ShapeDtypeStruct((T, D), F32),
        scratch_shapes=[pltpu.VMEM((tm, D), F32), pltpu.VMEM((tm, D), BF16)],
        compiler_params=_params(("parallel", "arbitrary")),
        name="ffn_ln",
    )(x, w1, w3, w2, g, b)


def _proj_kernel(x_ref, w_ref, o_ref, xb_ref):
    @pl.when(pl.program_id(1) == 0)
    def _():
        xb_ref[...] = x_ref[...].astype(BF16)

    o_ref[...] = _dot(xb_ref[...], w_ref[...])


def _proj(x, w, tm, tn):
    T, D = x.shape
    N = w.shape[1]
    return pl.pallas_call(
        _proj_kernel,
        grid=(T // tm, N // tn),
        in_specs=[pl.BlockSpec((tm, D), lambda i, j: (i, 0)),
                  pl.BlockSpec((D, tn), lambda i, j: (0, j))],
        out_specs=pl.BlockSpec((tm, tn), lambda i, j: (i, j)),
        out_shape=jax.ShapeDtypeStruct((T, N), F32),
        scratch_shapes=[pltpu.VMEM((tm, D), BF16)],
        compiler_params=_params(("parallel", "arbitrary")),
        name="in_proj",
    )(x, w)


def _kvproj_kernel(x_ref, w_ref, kc_ref, vc_ref, ks_ref, vs_ref, kw_ref, vw_ref, sm_ref):
    tm = x_ref.shape[0]
    y = _dot(x_ref[...].astype(BF16), w_ref[...])
    piece = lambda n: y[:, n * LANES:(n + 1) * LANES]
    kc_ref[...] = piece(0)
    vc_ref[...] = piece(1)
    row = pl.program_id(0) * tm + lax.broadcasted_iota(jnp.int32, (tm, LANES), 0)
    lane = lax.broadcasted_iota(jnp.int32, (tm, LANES), 1)
    onehot = ((row // NSA_SLC_BLOCK) % (TK // NSA_SLC_BLOCK) == lane).astype(BF16)
    ks_ref[...] = jnp.concatenate([piece(2).astype(BF16), onehot], axis=1)
    vs_ref[...] = jnp.concatenate([piece(3).astype(BF16), jnp.ones((tm, LANES), BF16)], axis=1)
    kw_ref[...] = piece(4).astype(BF16)
    vw_ref[...] = piece(5).astype(BF16)
    sm_ref[...] = piece(6)


def _kvproj(x, w, tm):
    T, D = x.shape
    narrow = lambda dt, width=LANES: (pl.BlockSpec((tm, width), lambda i: (i, 0)),
                                      jax.ShapeDtypeStruct((T, width), dt))
    outs = [narrow(F32), narrow(F32), narrow(BF16, 2 * LANES), narrow(BF16, 2 * LANES),
            narrow(BF16), narrow(BF16), narrow(F32)]
    return pl.pallas_call(
        _kvproj_kernel,
        grid=(T // tm,),
        in_specs=[pl.BlockSpec((tm, D), lambda i: (i, 0)), pl.BlockSpec(w.shape, lambda i: (0, 0))],
        out_specs=[o[0] for o in outs],
        out_shape=[o[1] for o in outs],
        compiler_params=_params(("parallel",)),
        name="kv_proj",
    )(x, w)


def _outproj_kernel(x_ref, oa_ref, ob_ref, oc_ref, od_ref, wa_ref, wb_ref, wc_ref, wd_ref,
                    g_ref, b_ref, o_ref):
    mix = (_dot(oa_ref[...], wa_ref[...]) + _dot(ob_ref[...], wb_ref[...])
           + _dot(oc_ref[...], wc_ref[...]) + _dot(od_ref[...], wd_ref[...]))
    o_ref[...] = _layer_norm(ALPHA * x_ref[...] + mix, g_ref[...], b_ref[...])


def _outproj(x, oa, ob, oc, od, wa, wb, wc, wd, g, b, tm):
    T, D = x.shape
    row = lambda a: pl.BlockSpec((tm, a.shape[1]), lambda i: (i, 0))
    full = lambda a: pl.BlockSpec(a.shape, lambda i: (0, 0))
    return pl.pallas_call(
        _outproj_kernel,
        grid=(T // tm,),
        in_specs=[row(x), row(oa), row(ob), row(oc), row(od),
                  full(wa), full(wb), full(wc), full(wd), full(g), full(b)],
        out_specs=row(x),
        out_shape=jax.ShapeDtypeStruct((T, D), F32),
        compiler_params=_params(("parallel",)),
        name="out_proj_ln",
    )(x, oa, ob, oc, od, wa, wb, wc, wd, g, b)


def _hgrn_tables(C):
    i = np.arange(C)[:, None]
    ip = np.arange(C)[None, :]
    seg = [(ip <= i),
           (ip > i)]
    masks = [np.eye(C, dtype=bool)]
    s = C // 2
    while s >= 1:
        blk = i // s
        seg.append(np.where(blk % 2 == 1, (ip > blk * s) & (ip <= i), (ip > i) & (ip <= (blk + 1) * s)))
        masks.append((blk % 2 == 1) & (ip // s == blk - 1))
        s //= 2
    seg = np.concatenate([x.astype(np.float32) for x in seg], axis=0)
    return seg, np.stack([m.astype(np.float32) for m in masks])


def _segment_sums(onehot, x):
    hi = x.astype(BF16)
    r1 = x - hi.astype(F32)
    mid = r1.astype(BF16)
    lo = (r1 - mid.astype(F32)).astype(BF16)
    n = x.shape[1]
    y = _dot(onehot, jnp.concatenate([hi, mid, lo], axis=1))
    return y[:, 0:n] + y[:, n:2 * n] + y[:, 2 * n:3 * n]


def _hgrn_kernel(q_ref, f_ref, i_ref, g_ref, llb_ref, l1m_ref, oml_ref, ng_ref,
                 seg_ref, msk_ref, o_ref, st_ref):
    @pl.when(pl.program_id(1) == 0)
    def _():
        st_ref[...] = jnp.zeros_like(st_ref)

    C = q_ref.shape[0]
    nlev = msk_ref.shape[0] - 1
    q = _silu(q_ref[...])
    z = f_ref[...]
    log_sig = jnp.minimum(z, 0.0) - jnp.log1p(jnp.exp(-jnp.abs(z)))
    cc = l1m_ref[...] + log_sig
    llb = llb_ref[...]
    logf = jnp.maximum(llb, cc) + jnp.log1p(jnp.exp(-jnp.abs(llb - cc)))
    k = oml_ref[...] * _sigmoid(-z)
    v = i_ref[...]
    seg = _segment_sums(seg_ref[...], logf)
    outs = []
    for h in range(HG_HEADS):
        sl = slice(h * LANES, (h + 1) * LANES)
        qh, kh, vh = q[:, sl], k[:, sl], v[:, sl]
        a = msk_ref[0] * _dot_nt(qh.astype(BF16), kh.astype(BF16))
        for l in range(nlev):
            dec = jnp.exp(seg[(2 + l) * C:(3 + l) * C, sl])
            a = a + msk_ref[1 + l] * _dot_nt((qh * dec).astype(BF16), (kh * dec).astype(BF16))
        b = seg[0:C, sl]
        st = st_ref[h]
        o = _dot(a.astype(BF16), vh.astype(BF16))
        o = o + _dot_nt((qh * jnp.exp(b)).astype(BF16), st.astype(BF16))
        kd = (kh * jnp.exp(seg[C:2 * C, sl])).astype(BF16)
        st_ref[h] = st * jnp.exp(b[C - 1:C, :]) + _dot(vh.T.astype(BF16), kd)
        outs.append(o * lax.rsqrt(jnp.mean(o * o, axis=-1, keepdims=True) + 1e-6))
    o = jnp.concatenate(outs, axis=1)
    o_ref[...] = (o * ng_ref[...] * _silu(g_ref[...])).astype(o_ref.dtype)


def _hgrn(proj, B, S, llb, l1m, oml, ng):
    C = CHUNK
    nc = S // C
    seg, msk = _hgrn_tables(C)
    seg, msk = jnp.asarray(seg, BF16), jnp.asarray(msk)
    col = lambda name: pl.BlockSpec((C, GROUP_W), lambda b, c, n=COL[name] // 4: (b * nc + c, n))
    vec = pl.BlockSpec((1, GROUP_W), lambda b, c: (0, 0))
    full2 = lambda a: pl.BlockSpec(a.shape, lambda b, c: (0, 0))
    return pl.pallas_call(
        _hgrn_kernel,
        grid=(B, nc),
        in_specs=[col("hq"), col("hf"), col("hi"), col("hg"), vec, vec, vec, vec,
                  full2(seg), pl.BlockSpec(msk.shape, lambda b, c: (0, 0, 0))],
        out_specs=pl.BlockSpec((C, GROUP_W), lambda b, c: (b * nc + c, 0)),
        out_shape=jax.ShapeDtypeStruct((B * S, GROUP_W), BF16),
        scratch_shapes=[pltpu.VMEM((HG_HEADS, LANES, LANES), F32)],
        compiler_params=_params(("parallel", "arbitrary")),
        name="hgrn2",
    )(proj, proj, proj, proj, llb, l1m, oml, ng, seg, msk)


def _ssd_kernel(z_ref, xbc_ref, sm_ref, cw_ref, cb_ref, dtb_ref, aneg_ref, dsk_ref, ng_ref,
                ex_ref, o_ref, tail_ref, st_ref):
    @pl.when(pl.program_id(1) == 0)
    def _():
        tail_ref[...] = jnp.zeros_like(tail_ref)
        st_ref[...] = jnp.zeros_like(st_ref)

    L = xbc_ref.shape[0]
    x = xbc_ref[...]
    xe = jnp.concatenate([tail_ref[...], x], axis=0)
    cw = cw_ref[...]
    conv = cb_ref[...]
    for kk in range(SSM_CONV):
        conv = conv + cw[kk:kk + 1, :] * xe[5 + kk:5 + kk + L, :]
    tail_ref[...] = x[L - 8:L, :]
    conv = _silu(conv)
    xs = conv[:, 0:GROUP_W]
    bm = conv[:, GROUP_W:GROUP_W + 256]
    cm = conv[:, GROUP_W + 256:GROUP_W + 512]

    dtf = _softplus(sm_ref[...] + dtb_ref[...])
    la = dtf * aneg_ref[...]
    ri = lax.broadcasted_iota(jnp.int32, (L, L), 0)
    ci = lax.broadcasted_iota(jnp.int32, (L, L), 1)
    tri = ri >= ci
    bfull = _dot_exact(tri.astype(F32), la)
    ex = ex_ref[...]
    bexp = _dot_exact(bfull, ex)
    dtexp = _dot_exact(dtf, ex)
    b_t = bfull.T
    xdt = xs * dtexp
    lane = lax.broadcasted_iota(jnp.int32, (L, LANES), 1)

    scores = []
    for g in range(SSM_GROUPS):
        cg = cm[:, g * SSM_STATE:(g + 1) * SSM_STATE].astype(BF16)
        bg = bm[:, g * SSM_STATE:(g + 1) * SSM_STATE].astype(BF16)
        cb = _dot_nt(cg, bg)
        for hh in range(SSM_HEADS // SSM_GROUPS):
            h = g * (SSM_HEADS // SSM_GROUPS) + hh
            bcol = bfull[:, DT_LANE0 + h:DT_LANE0 + h + 1]
            brow = b_t[DT_LANE0 + h:DT_LANE0 + h + 1, :]
            dec = jnp.exp(jnp.where(tri, bcol - brow, NEG))
            scores.append((cb * dec).astype(BF16))
    y_pairs = []
    for u in range(SSM_HEADS // 2):
        slab = xdt[:, u * LANES:(u + 1) * LANES]
        lo = jnp.where(lane < SSM_HEAD_DIM, slab, 0.0).astype(BF16)
        hi = jnp.where(lane >= SSM_HEAD_DIM, slab, 0.0).astype(BF16)
        y_pairs.append(_dot(scores[2 * u], lo) + _dot(scores[2 * u + 1], hi))
    y_intra = jnp.concatenate(y_pairs, axis=1)

    blast = bexp[L - 1:L, :]
    w = (xdt * jnp.exp(blast - bexp)).astype(BF16)
    y_inter = []
    for g in range(SSM_GROUPS):
        gs = slice(g * 256, (g + 1) * 256)
        cg = cm[:, g * SSM_STATE:(g + 1) * SSM_STATE].astype(BF16)
        st = st_ref[g]
        y_inter.append(_dot(cg, st.astype(BF16)))
        bg_t = bm[:, g * SSM_STATE:(g + 1) * SSM_STATE].T.astype(BF16)
        st_ref[g] = st * jnp.exp(blast[:, gs]) + _dot(bg_t, w[:, gs])
    y = y_intra + jnp.concatenate(y_inter, axis=1) * jnp.exp(bexp) + dsk_ref[...] * xs
    y = y * _silu(z_ref[...])
    halves = []
    for g in range(SSM_GROUPS):
        seg = y[:, g * 256:(g + 1) * 256]
        halves.append(seg * lax.rsqrt(jnp.mean(seg * seg, axis=-1, keepdims=True) + 1e-6))
    o_ref[...] = (jnp.concatenate(halves, axis=1) * ng_ref[...]).astype(o_ref.dtype)


def _ssd(proj, small, B, S, cw, cb, dtb, aneg, dsk, ng):
    L = CHUNK
    nc = S // L
    ex = np.zeros((LANES, GROUP_W), np.float32)
    for h in range(SSM_HEADS):
        ex[DT_LANE0 + h, h * SSM_HEAD_DIM:(h + 1) * SSM_HEAD_DIM] = 1.0
    ex = jnp.asarray(ex)
    full2 = lambda a: pl.BlockSpec(a.shape, lambda b, c: (0, 0))
    return pl.pallas_call(
        _ssd_kernel,
        grid=(B, nc),
        in_specs=[
            pl.BlockSpec((L, GROUP_W), lambda b, c: (b * nc + c, COL["sz"] // 4)),
            pl.BlockSpec((L, 1024), lambda b, c: (b * nc + c, COL["sxbc"] // 8)),
            pl.BlockSpec((L, LANES), lambda b, c: (b * nc + c, 0)),
            full2(cw), full2(cb), full2(dtb), full2(aneg), full2(dsk), full2(ng), full2(ex)],
        out_specs=pl.BlockSpec((L, GROUP_W), lambda b, c: (b * nc + c, 0)),
        out_shape=jax.ShapeDtypeStruct((B * S, GROUP_W), BF16),
        scratch_shapes=[pltpu.VMEM((8, 1024), F32), pltpu.VMEM((SSM_GROUPS, SSM_STATE, 256), F32)],
        compiler_params=_params(("parallel", "arbitrary")),
        name="ssd",
    )(proj, proj, small, cw, cb, dtb, aneg, dsk, ng, ex)


def _ret_kernel(q_ref, k_ref, v_ref, g_ref, cos_ref, sin_ref, dec_ref, qs_ref, ks_ref, sd_ref,
                o_ref, st_ref):
    @pl.when(pl.program_id(1) == 0)
    def _():
        st_ref[...] = jnp.zeros_like(st_ref)

    cos = cos_ref[...]
    sin = sin_ref[...]
    outs = []
    for h in range(RET_HEADS):
        sl = slice(h * LANES, (h + 1) * LANES)
        qh = q_ref[:, sl]
        kh = k_ref[:, sl]
        qh = qh * cos + pltpu.roll(qh, RET_DK // 2, axis=1) * sin
        kh = (kh * cos + pltpu.roll(kh, RET_DK // 2, axis=1) * sin) * (RET_DK ** -0.5)
        vh = v_ref[:, sl].astype(BF16)
        sc = (_dot_nt(qh.astype(BF16), kh.astype(BF16)) * dec_ref[h]).astype(BF16)
        st = st_ref[h]
        y = _dot(sc, vh) + _dot((qh * qs_ref[:, sl]).astype(BF16), st.astype(BF16))
        kd_t = (kh * ks_ref[:, sl]).T.astype(BF16)
        st_ref[h] = st * sd_ref[h] + _dot(kd_t, vh)
        mu = jnp.mean(y, axis=-1, keepdims=True)
        d = y - mu
        outs.append(d * lax.rsqrt(jnp.mean(d * d, axis=-1, keepdims=True) + 1e-5))
    o_ref[...] = (_silu(g_ref[...]) * jnp.concatenate(outs, axis=1)).astype(o_ref.dtype)


def _retention(proj, B, S, cos_t, sin_t):
    L = CHUNK
    nc = S // L
    lg = jnp.log(1.0 - 2.0 ** (-5.0 - jnp.arange(RET_HEADS, dtype=F32)))
    i = jnp.arange(L, dtype=F32)
    diff = i[:, None] - i[None, :]
    dec = jnp.where(diff >= 0, jnp.exp(lg[:, None, None] * jnp.maximum(diff, 0.0)), 0.0)
    rep = lambda t: jnp.repeat(t, LANES, axis=1)
    qs = rep(jnp.exp((i[:, None] + 1.0) * lg[None, :]))
    ks = rep(jnp.exp((L - 1.0 - i[:, None]) * lg[None, :]))
    sd = jnp.broadcast_to(jnp.exp(L * lg)[:, None, None], (RET_HEADS, LANES, LANES))
    col = lambda name: pl.BlockSpec((L, GROUP_W), lambda b, c, n=COL[name] // 4: (b * nc + c, n))
    return pl.pallas_call(
        _ret_kernel,
        grid=(B, nc),
        in_specs=[col("rq"), col("rk"), col("rv"), col("rg"),
                  pl.BlockSpec((L, LANES), lambda b, c: (c, 0)),
                  pl.BlockSpec((L, LANES), lambda b, c: (c, 0)),
                  pl.BlockSpec((RET_HEADS, L, L), lambda b, c: (0, 0, 0)),
                  pl.BlockSpec((L, GROUP_W), lambda b, c: (0, 0)),
                  pl.BlockSpec((L, GROUP_W), lambda b, c: (0, 0)),
                  pl.BlockSpec((RET_HEADS, LANES, LANES), lambda b, c: (0, 0, 0))],
        out_specs=pl.BlockSpec((L, GROUP_W), lambda b, c: (b * nc + c, 0)),
        out_shape=jax.ShapeDtypeStruct((B * S, GROUP_W), BF16),
        scratch_shapes=[pltpu.VMEM((RET_HEADS, RET_DK, RET_DK), F32)],
        compiler_params=_params(("parallel", "arbitrary")),
        name="retention",
    )(proj, proj, proj, proj, cos_t, sin_t, dec, qs, ks, sd)


def _t5_bucket(dist):
    n = jnp.maximum(dist, 0)
    nf = jnp.maximum(n, 1).astype(F32)
    large = REL_EXACT + (jnp.log(nf / REL_EXACT) / math.log(REL_MAX_DIST / REL_EXACT)
                         * (REL_BUCKETS - REL_EXACT)).astype(jnp.int32)
    return jnp.where(n < REL_EXACT, n, jnp.minimum(large, REL_BUCKETS - 1))


def _head_bias(bucket, rel_ref):
    rows, cols = bucket.shape
    per_head = []
    for h in range(NSA_HEADS):
        tbl = jnp.broadcast_to(rel_ref[h:h + 1, :], (rows, LANES))
        chunks = [jnp.take_along_axis(tbl, bucket[:, c:c + LANES], axis=1)
                  for c in range(0, cols, LANES)]
        per_head.append(chunks[0] if len(chunks) == 1 else jnp.concatenate(chunks, axis=1))
    return jnp.stack(per_head, axis=0)


def _stack_heads(qw):
    return jnp.concatenate([qw[:, h * LANES:(h + 1) * LANES] for h in range(NSA_HEADS)],
                           axis=0).astype(BF16)


def _cmp_kernel(g_ref, pe_ref, w1a_ref, w1b_ref, w2_ref, o_ref):
    nb = g_ref.shape[0]
    gw = g_ref.shape[1] // 4
    pe = pe_ref[...]
    slabs = [g_ref[:, s * gw:(s + 1) * gw] for s in range(4)]
    nxt0 = pltpu.roll(slabs[0], nb - 1, axis=0)
    for s in range(4):
        a = (slabs[s] + pe[0:1, :]).astype(BF16)
        bn = ((slabs[s + 1] if s < 3 else nxt0) + pe[1:2, :]).astype(BF16)
        hid = _silu(_dot(a, w1a_ref[...]) + _dot(bn, w1b_ref[...]))
        o_ref[s * nb:(s + 1) * nb, :] = _dot(hid.astype(BF16), w2_ref[...]).astype(o_ref.dtype)


def _compress(g, pe2, w1a, w1b, w2bd):
    B, nb, gw4 = g.shape
    full2 = lambda a: pl.BlockSpec(a.shape, lambda b: (0, 0))
    slab_major = pl.pallas_call(
        _cmp_kernel,
        grid=(B,),
        in_specs=[pl.BlockSpec((None, nb, gw4), lambda b: (b, 0, 0)),
                  full2(pe2), full2(w1a), full2(w1b), full2(w2bd)],
        out_specs=pl.BlockSpec((None, 4 * nb, LANES), lambda b: (b, 0, 0)),
        out_shape=jax.ShapeDtypeStruct((B, 4 * nb, LANES), BF16),
        compiler_params=_params(("parallel",)),
        name="nsa_compress",
    )(g, pe2, w1a, w1b, w2bd)
    return slab_major.reshape(B, 4, nb, LANES).transpose(0, 2, 1, 3).reshape(B, 4 * nb, LANES)


def _exact_rows_dot(x, w):
    hi = x.astype(BF16)
    r1 = x - hi.astype(F32)
    mid = r1.astype(BF16)
    lo = (r1 - mid.astype(F32)).astype(BF16)
    n = x.shape[0]
    y = _dot(jnp.concatenate([hi, mid, lo], axis=0), w)
    return y[0:n] + y[n:2 * n] + y[2 * n:3 * n]


def _cmpattn_kernel(q_ref, kc_ref, vc_ref, band_ref, st_ref, oc_ref, selb_ref):
    tq = q_ref.shape[0]
    ncmp = kc_ref.shape[0]
    nb = ncmp // 4
    R = NSA_HEADS * tq
    q0 = pl.program_id(1) * tq
    nd = band_ref.shape[0]
    Q = _stack_heads(q_ref[...])
    s3 = _dot_nt(Q, kc_ref[...]).reshape(NSA_HEADS, tq, ncmp)
    parts = []
    for ch in range(ncmp // LANES):
        d = jnp.clip((q0 - ch * LANES * NSA_CMP_STRIDE) // tq + CMP_BAND_OFF, 0, nd - 1)
        parts.append(s3[:, :, ch * LANES:(ch + 1) * LANES] + band_ref[d])
    s3 = jnp.concatenate(parts, axis=2)
    mx = jnp.max(s3, axis=-1, keepdims=True)
    e = jnp.exp(s3 - mx)
    live = (q0 + lax.broadcasted_iota(jnp.int32, (tq, 1), 0) >= NSA_CMP_BLOCK - 1)[None]
    p = e * jnp.where(live, 1.0 / jnp.sum(e, axis=-1, keepdims=True), 0.0)
    oc_ref[...] = _dot(p.reshape(R, ncmp).astype(BF16), vc_ref[...])

    ps = p.reshape(NSA_KV, NSA_GROUP, tq, ncmp).sum(axis=1).reshape(NSA_KV * tq, ncmp)
    imp = _exact_rows_dot(ps, st_ref[...])
    j = lax.broadcasted_iota(jnp.int32, (NSA_KV * tq, nb), 1)
    t = q0 + (lax.broadcasted_iota(jnp.int32, (NSA_KV * tq, nb), 0) % tq)
    cur = t // NSA_SLC_BLOCK
    forced = (j == 0) | (j == cur) | (j == cur - 1)
    score = jnp.where(j > cur, -1.0, jnp.where(forced, NSA_GROUP + 1.0, imp))
    score = score.T
    jf = lax.broadcasted_iota(jnp.int32, score.shape, 0).astype(F32)
    sel = score == NSA_GROUP + 1.0
    score = jnp.where(sel, -jnp.inf, score)
    for _ in range(min(NSA_TOP_N, nb) - 3):
        best = jnp.max(score, axis=0, keepdims=True)
        first = jnp.min(jnp.where(score == best, jf, float(nb)), axis=0, keepdims=True)
        hit = jf == first
        sel = sel | hit
        score = jnp.where(hit, -jnp.inf, score)
    selb = jnp.where(sel, 0.0, NEG).T.astype(selb_ref.dtype)
    selb_ref[:, 0:nb] = selb[0:tq]
    selb_ref[:, nb:2 * nb] = selb[tq:2 * tq]


def _cmpattn(proj, kcmp, vcmp, cband, B, S):
    tq = TQ
    nqt = S // tq
    ncmp = kcmp.shape[1]
    nb = ncmp // 4
    off = np.arange(ncmp)[:, None] - 4 * np.arange(nb)[None, :]
    stencil = np.where((off >= 0) & (off <= 2), 1.0, np.where((off == -1) | (off == 3), 0.5, 0.0))
    stencil = jnp.asarray(stencil, BF16)
    once = pl.Buffered(1)
    return pl.pallas_call(
        _cmpattn_kernel,
        grid=(B, nqt),
        in_specs=[pl.BlockSpec((tq, 1024), lambda b, i: (b * nqt + i, COL["nqw"] // 8)),
                  pl.BlockSpec((None, ncmp, LANES), lambda b, i: (b, 0, 0)),
                  pl.BlockSpec((None, ncmp, LANES), lambda b, i: (b, 0, 0)),
                  pl.BlockSpec(cband.shape, lambda b, i: (0, 0, 0, 0), pipeline_mode=once),
                  pl.BlockSpec(stencil.shape, lambda b, i: (0, 0))],
        out_specs=[pl.BlockSpec((NSA_HEADS * tq, LANES), lambda b, i: (b * nqt + i, 0)),
                   pl.BlockSpec((tq, 2 * nb), lambda b, i: (b * nqt + i, 0))],
        out_shape=[jax.ShapeDtypeStruct((B * S * NSA_HEADS, LANES), F32),
                   jax.ShapeDtypeStruct((B * S, 2 * nb), BF16)],
        compiler_params=_params(("parallel", "arbitrary")),
        name="nsa_cmp_attn_topk",
    )(proj, kcmp, vcmp, cband, stencil)


def _band_kernel(rel_ref, o_ref, *, entry_off, key_step, key_end):
    tq = o_ref.shape[1]
    delta = (pl.program_id(0) - entry_off) * tq
    row = lax.broadcasted_iota(jnp.int32, (tq, LANES), 0)
    col = lax.broadcasted_iota(jnp.int32, (tq, LANES), 1)
    dist = delta + row - (col * key_step + key_end)
    bias = _head_bias(_t5_bucket(dist), rel_ref)
    o_ref[...] = jnp.where((dist >= 0)[None], bias, NEG)


def _band_table(rel_t, tq, entry_off, key_step=1, key_end=0):
    nd = -(-(REL_MAX_DIST + key_step * (LANES - 1) + key_end) // tq) + entry_off + 1
    return pl.pallas_call(
        functools.partial(_band_kernel, entry_off=entry_off, key_step=key_step, key_end=key_end),
        grid=(nd,),
        in_specs=[pl.BlockSpec(rel_t.shape, lambda d: (0, 0))],
        out_specs=pl.BlockSpec((None, NSA_HEADS, tq, LANES), lambda d: (d, 0, 0, 0)),
        out_shape=jax.ShapeDtypeStruct((nd, NSA_HEADS, tq, LANES), F32),
        compiler_params=_params(("parallel",)),
        name="nsa_bias_band",
    )(rel_t)


def _selattn_kernel(q_ref, selb_ref, ks_ref, vs_ref, band_ref, sp_ref, os_ref,
                    qaug_ref, msel_ref, s_ref, m_ref, acc_ref):
    tq = q_ref.shape[0]
    R = NSA_HEADS * tq
    nb = selb_ref.shape[1] // 2
    n_kt = msel_ref.shape[0]
    tk = ks_ref.shape[0] // n_kt
    nd = band_ref.shape[0]
    q0 = pl.program_id(1) * tq
    qaug_ref[:, 0:LANES] = _stack_heads(q_ref[...])
    selb2 = jnp.concatenate([selb_ref[:, 0:nb], selb_ref[:, nb:2 * nb]], axis=0)
    spread = _dot(selb2, sp_ref[...])
    for c in range(n_kt):
        msel_ref[c] = spread[:, c * LANES:(c + 1) * LANES].astype(BF16)
    m_ref[...] = jnp.full(m_ref.shape, NEG, F32)
    acc_ref[...] = jnp.zeros_like(acc_ref)
    n_tiles = (q0 + tq - 1) // tk + 1

    def scores(c, slot):
        c = jnp.minimum(c, n_kt - 1)
        m2 = msel_ref[c]
        qaug_ref[:, LANES:2 * LANES] = jnp.concatenate(
            [m2[0:tq]] * NSA_GROUP + [m2[tq:2 * tq]] * NSA_GROUP, axis=0)
        s_ref[slot] = _dot_nt(qaug_ref[...], ks_ref[pl.ds(pl.multiple_of(c * tk, tk), tk), :])

    def accumulate(c, slot):
        live = c < n_tiles
        k0 = pl.multiple_of(jnp.minimum(c, n_kt - 1) * tk, tk)
        s3 = s_ref[slot].reshape(NSA_HEADS, tq, tk)
        parts = []
        for ch in range(tk // LANES):
            d = jnp.clip((q0 - k0) // tq - ch * (LANES // tq) + BAND_OFF, 0, nd - 1)
            parts.append(s3[:, :, ch * LANES:(ch + 1) * LANES] + band_ref[jnp.where(live, d, 0)])
        s = jnp.concatenate(parts, axis=2).reshape(R, tk)
        m_old = m_ref[...]
        m_new = jnp.maximum(m_old, jnp.max(s, axis=-1, keepdims=True))
        alpha = jnp.exp(m_old - m_new)
        p = jnp.exp(s - jnp.tile(m_new, (1, tk // LANES)))
        acc_ref[...] = jnp.tile(alpha, (1, 2)) * acc_ref[...] + _dot(p.astype(BF16), vs_ref[pl.ds(k0, tk), :])
        m_ref[...] = m_new

    def pair(i, carry):
        c = 2 * i
        scores(c + 1, 1)
        accumulate(c, 0)
        scores(c + 2, 0)
        accumulate(c + 1, 1)
        return carry

    scores(0, 0)
    lax.fori_loop(0, (n_tiles + 1) // 2, pair, 0)
    acc = acc_ref[...]
    os_ref[...] = acc[:, 0:LANES] / acc[:, LANES:2 * LANES]


def _selattn(proj, selb, ks, vs, band, B, S):
    tq = TQ
    tk = min(TK, S)
    n_kt = S // tk
    bpt = tk // NSA_SLC_BLOCK
    nqt = S // tq
    nb = S // NSA_SLC_BLOCK
    R = NSA_HEADS * tq
    sp = np.zeros((nb, n_kt * LANES), np.float32)
    sp[np.arange(nb), (np.arange(nb) // bpt) * LANES + np.arange(nb) % bpt] = 1.0
    once = pl.Buffered(1)
    return pl.pallas_call(
        _selattn_kernel,
        grid=(B, nqt),
        in_specs=[pl.BlockSpec((tq, 1024), lambda b, i: (b * nqt + i, COL["nqw"] // 8)),
                  pl.BlockSpec((tq, 2 * nb), lambda b, i: (b * nqt + i, 0)),
                  pl.BlockSpec((None, S, 2 * LANES), lambda b, i: (b, 0, 0), pipeline_mode=once),
                  pl.BlockSpec((None, S, 2 * LANES), lambda b, i: (b, 0, 0), pipeline_mode=once),
                  pl.BlockSpec(band.shape, lambda b, i: (0, 0, 0, 0), pipeline_mode=once),
                  pl.BlockSpec(sp.shape, lambda b, i: (0, 0))],
        out_specs=pl.BlockSpec((R, LANES), lambda b, i: (b * nqt + i, 0)),
        out_shape=jax.ShapeDtypeStruct((B * S * NSA_HEADS, LANES), F32),
        scratch_shapes=[pltpu.VMEM((R, 2 * LANES), BF16), pltpu.VMEM((n_kt, 2 * tq, LANES), BF16),
                        pltpu.VMEM((2, R, tk), F32), pltpu.VMEM((R, LANES), F32),
                        pltpu.VMEM((R, 2 * LANES), F32)],
        compiler_params=_params(("parallel", "arbitrary")),
        name="nsa_sel_attn",
    )(proj, selb, ks, vs, band, jnp.asarray(sp, BF16))


def _winattn_kernel(q_ref, sm_ref, oc_ref, os_ref, kw_ref, vw_ref, band_ref, ng_ref, o_ref):
    tq = q_ref.shape[0]
    R = NSA_HEADS * tq
    S = kw_ref.shape[0]
    span = min(WIN_SPAN, S)
    nd = band_ref.shape[0]
    q0 = pl.program_id(1) * tq
    start = pl.multiple_of(jnp.clip(q0 + tq - span, 0, S - span), tq)
    Q = _stack_heads(q_ref[...])
    kt = kw_ref[pl.ds(start, span), :]
    vt = vw_ref[pl.ds(start, span), :]
    s3 = _dot_nt(Q, kt).reshape(NSA_HEADS, tq, span)
    parts = []
    for ch in range(span // LANES):
        d = jnp.clip((q0 - start) // tq - ch * (LANES // tq) + BAND_OFF, 0, nd - 1)
        parts.append(s3[:, :, ch * LANES:(ch + 1) * LANES] + band_ref[d])
    row = lax.broadcasted_iota(jnp.int32, (tq, span), 0)
    col = lax.broadcasted_iota(jnp.int32, (tq, span), 1)
    in_window = ((q0 + row) - (start + col) < NSA_WINDOW)[None]
    s3 = jnp.where(in_window, jnp.concatenate(parts, axis=2), NEG)
    mx = jnp.max(s3, axis=-1, keepdims=True)
    e = jnp.exp(s3 - mx)
    p = e * (1.0 / jnp.sum(e, axis=-1, keepdims=True))
    ow = _dot(p.reshape(R, span).astype(BF16), vt)

    gates = _sigmoid(sm_ref[...])
    lane = lax.broadcasted_iota(jnp.int32, (tq, LANES), 1)
    heads = []
    ssq = jnp.zeros((tq, 1), F32)
    for h in range(NSA_HEADS):
        rs = slice(h * tq, (h + 1) * tq)
        g = [gates[:, GATE_LANE0 + 3 * h + br:GATE_LANE0 + 3 * h + br + 1] for br in range(3)]
        oh = g[0] * oc_ref[rs, :] + g[1] * os_ref[rs, :] + g[2] * ow[rs, :]
        kv = h // NSA_GROUP
        valid = (lane >= kv * NSA_HEAD_DIM) & (lane < (kv + 1) * NSA_HEAD_DIM)
        oh = jnp.where(valid, oh, 0.0)
        ssq = ssq + jnp.sum(oh * oh, axis=-1, keepdims=True)
        heads.append(oh)
    rinv = lax.rsqrt(ssq / GROUP_W + 1e-6)
    o_ref[...] = (jnp.concatenate(heads, axis=1) * rinv * ng_ref[...]).astype(o_ref.dtype)


def _winattn(proj, small, oc, os_, kw, vw, band, ngw, B, S):
    tq = TQ
    nqt = S // tq
    R = NSA_HEADS * tq
    once = pl.Buffered(1)
    return pl.pallas_call(
        _winattn_kernel,
        grid=(B, nqt),
        in_specs=[pl.BlockSpec((tq, 1024), lambda b, i: (b * nqt + i, COL["nqw"] // 8)),
                  pl.BlockSpec((tq, LANES), lambda b, i: (b * nqt + i, 0)),
                  pl.BlockSpec((R, LANES), lambda b, i: (b * nqt + i, 0)),
                  pl.BlockSpec((R, LANES), lambda b, i: (b * nqt + i, 0)),
                  pl.BlockSpec((None, S, LANES), lambda b, i: (b, 0, 0), pipeline_mode=once),
                  pl.BlockSpec((None, S, LANES), lambda b, i: (b, 0, 0), pipeline_mode=once),
                  pl.BlockSpec(band.shape, lambda b, i: (0, 0, 0, 0), pipeline_mode=once),
                  pl.BlockSpec(ngw.shape, lambda b, i: (0, 0))],
        out_specs=pl.BlockSpec((tq, NSA_HEADS * LANES), lambda b, i: (b * nqt + i, 0)),
        out_shape=jax.ShapeDtypeStruct((B * S, NSA_HEADS * LANES), BF16),
        compiler_params=_params(("parallel", "arbitrary")),
        name="nsa_win_attn_merge",
    )(proj, small, oc, os_, kw, vw, band, ngw)


def _widen_heads(x, axis):
    x = jnp.moveaxis(x, axis, -1)
    lead = x.shape[:-1]
    x = x.reshape(*lead, NSA_KV, NSA_GROUP, 1, NSA_HEAD_DIM)
    sel = jnp.eye(NSA_KV, dtype=x.dtype).reshape(NSA_KV, 1, NSA_KV, 1)
    x = (x * sel).reshape(*lead, NSA_HEADS * LANES)
    return jnp.moveaxis(x, -1, axis)


def _build_w_in(w):
    (hq, hf, hi, hg, nq, nkc, nvc, nks, nvs, nkw, nvw, ngate,
     sz, sxbc, sdt, rq, rk, rv, rg) = jnp.split(w, IN_SPLITS, axis=1)
    D = w.shape[0]
    nqw = _widen_heads(nq * NSA_HEAD_DIM ** -0.5, 1)
    deint = lambda t: t.reshape(D, RET_HEADS, RET_DK // 2, 2).transpose(0, 1, 3, 2).reshape(D, GROUP_W)
    small = jnp.concatenate([ngate, sdt, jnp.zeros((D, LANES - 32), w.dtype)], axis=1)
    wide = [hq, hf, hi, hg, nqw, sxbc, sz, deint(rq), deint(rk), rv, rg]
    narrow = [nkc, nvc, nks, nvs, nkw, nvw, small]
    return jnp.concatenate(wide, axis=1).astype(BF16), jnp.concatenate(narrow, axis=1).astype(BF16)


def _build_cmp_weights(pe, w1, w2):
    w1r = w1.reshape(2, NSA_CMP_STRIDE, NSA_HEAD_DIM, NSA_CMP_HIDDEN)
    eye = jnp.eye(NSA_KV, dtype=w1.dtype)
    big = jnp.einsum("ardc,kj->arkdjc", w1r, eye).reshape(
        2, NSA_CMP_STRIDE * NSA_KV * NSA_HEAD_DIM, NSA_KV * NSA_CMP_HIDDEN)
    w2bd = jnp.einsum("cd,kj->kcjd", w2, eye).reshape(NSA_KV * NSA_CMP_HIDDEN, NSA_KV * NSA_HEAD_DIM)
    per = pe.reshape(2, NSA_CMP_STRIDE, 1, NSA_HEAD_DIM)
    pe2 = jnp.broadcast_to(per, (2, NSA_CMP_STRIDE, NSA_KV, NSA_HEAD_DIM)).reshape(2, -1)
    return pe2, big[0].astype(BF16), big[1].astype(BF16), w2bd.astype(BF16)


def _rotary_tables(S):
    half = RET_DK // 2
    theta = 1.0 / (10000.0 ** jnp.linspace(0.0, 1.0, half, dtype=F32))
    ang = jnp.arange(S, dtype=F32)[:, None] * theta[None, :]
    cos, sin = jnp.cos(ang), jnp.sin(ang)
    return jnp.concatenate([cos, cos], axis=1), jnp.concatenate([-sin, sin], axis=1)


def _mixer(x2, B, S, l, p, lower_bounds, band, cband, cos_t, sin_t):
    T = B * S
    w_wide, w_narrow = _build_w_in(p["w_in"][l])
    proj = _proj(x2, w_wide, tm=min(512, T), tn=NCOL * LANES // 4)
    kc, vc, ks, vs, kw, vw, small = _kvproj(x2, w_narrow, tm=min(512, T))
    row = lambda v: v.reshape(1, -1).astype(F32)

    lb = lower_bounds[l].astype(F32)
    o_a = _hgrn(proj, B, S, row(jnp.log(lb)), row(jnp.log1p(-lb)), row(1.0 - lb),
                row(p["hgrn_norm_g"][l]))

    nb = S // NSA_SLC_BLOCK
    grp = lambda t: t.reshape(B, nb, 4 * NSA_CMP_STRIDE * LANES)
    kcmp = _compress(grp(kc), *_build_cmp_weights(p["nsa_pe_k"][l], p["nsa_w1_k"][l], p["nsa_w2_k"][l]))
    vcmp = _compress(grp(vc), *_build_cmp_weights(p["nsa_pe_v"][l], p["nsa_w1_v"][l], p["nsa_w2_v"][l]))
    o_cmp, selb = _cmpattn(proj, kcmp, vcmp, cband, B, S)
    seq = lambda t: t.reshape(B, S, t.shape[1])
    o_sel = _selattn(proj, selb, seq(ks), seq(vs), band, B, S)
    ngw = _widen_heads(p["nsa_norm_g"][l].astype(F32), 0).reshape(1, -1)
    o_b = _winattn(proj, small, o_cmp, o_sel, seq(kw), seq(vw), band, ngw, B, S)

    lane_vec = lambda v: jnp.zeros((1, LANES), F32).at[0, DT_LANE0:DT_LANE0 + SSM_HEADS].set(v.astype(F32))
    o_c = _ssd(proj, small, B, S, p["ssm_conv_w"][l].astype(F32), row(p["ssm_conv_b"][l]),
               lane_vec(p["ssm_dt_bias"][l]), lane_vec(-jnp.exp(p["ssm_a_log"][l].astype(F32))),
               row(jnp.repeat(p["ssm_d"][l].astype(F32), SSM_HEAD_DIM)), row(p["ssm_norm_g"][l]))

    o_d = _retention(proj, B, S, cos_t, sin_t)

    w_out = p["w_out"][l]
    wa, wb, wc, wd = (w_out[i * GROUP_W:(i + 1) * GROUP_W] for i in range(4))
    return o_a, o_b, o_c, o_d, wa.astype(BF16), _widen_heads(wb, 0).astype(BF16), wc.astype(BF16), wd.astype(BF16)


def kernel(x, ln1_g, ln1_b, ffn1_w1, ffn1_w3, ffn1_w2, ln2_g, ln2_b, w_in, w_out, hgrn_lb_logits, hgrn_norm_g, nsa_pe_k, nsa_w1_k, nsa_w2_k, nsa_pe_v, nsa_w1_v, nsa_w2_v, nsa_norm_g, rel_bias, ssm_conv_w, ssm_conv_b, ssm_dt_bias, ssm_a_log, ssm_d, ssm_norm_g, ln3_g, ln3_b, ffn2_w1, ffn2_w3, ffn2_w2):
    B, S, D = x.shape
    T = B * S
    depth = w_in.shape[0]
    p = dict(w_in=w_in, w_out=w_out, hgrn_norm_g=hgrn_norm_g, nsa_pe_k=nsa_pe_k, nsa_w1_k=nsa_w1_k,
             nsa_w2_k=nsa_w2_k, nsa_pe_v=nsa_pe_v, nsa_w1_v=nsa_w1_v, nsa_w2_v=nsa_w2_v,
             nsa_norm_g=nsa_norm_g, ssm_conv_w=ssm_conv_w, ssm_conv_b=ssm_conv_b,
             ssm_dt_bias=ssm_dt_bias, ssm_a_log=ssm_a_log, ssm_d=ssm_d, ssm_norm_g=ssm_norm_g)
    cum = jnp.cumsum(jax.nn.softmax(hgrn_lb_logits.astype(F32), axis=0), axis=0)
    lower_bounds = cum - cum[:1]
    rel_t = jnp.zeros((NSA_HEADS, LANES), F32).at[:, :REL_BUCKETS].set(rel_bias.astype(F32).T)
    band = _band_table(rel_t, TQ, BAND_OFF)
    cband = _band_table(rel_t, TQ, CMP_BAND_OFF, NSA_CMP_STRIDE, NSA_CMP_BLOCK - 1)
    cos_t, sin_t = _rotary_tables(S)
    row = lambda v: v.reshape(1, -1).astype(F32)
    tm = min(512, T)
    tf = 512 if ffn1_w1.shape[2] % 512 == 0 else ffn1_w1.shape[2]
    x2 = x.reshape(T, D).astype(F32)
    for l in range(depth):
        x2 = _ffn(x2, ffn1_w1[l].astype(BF16), ffn1_w3[l].astype(BF16), ffn1_w2[l].astype(BF16),
                  row(ln1_g[l]), row(ln1_b[l]), tm, tf)
        o_a, o_b, o_c, o_d, wa, wb, wc, wd = _mixer(x2, B, S, l, p, lower_bounds, band, cband, cos_t, sin_t)
        x2 = _outproj(x2, o_a, o_b, o_c, o_d, wa, wb, wc, wd, row(ln2_g[l]), row(ln2_b[l]), min(256, T))
        x2 = _ffn(x2, ffn2_w1[l].astype(BF16), ffn2_w3[l].astype(BF16), ffn2_w2[l].astype(BF16),
                  row(ln3_g[l]), row(ln3_b[l]), tm, tf)
    return x2.reshape(B, S, D).astype(x.dtype)
```

```python
import functools
import math

import numpy as np
import jax
import jax.numpy as jnp
from jax import lax
from jax.experimental import pallas as pl
from jax.experimental.pallas import tpu as pltpu

F32 = jnp.float32
BF16 = jnp.bfloat16

D_MODEL = 2048
DEPTH = 2
GROUP_W = 512
ALPHA = (2 * DEPTH) ** 0.25
HG_HEADS = 4
NSA_HEADS = 8
NSA_KV = 2
NSA_GROUP = 4
NSA_HEAD_DIM = 64
NSA_CMP_STRIDE = 16
NSA_CMP_BLOCK = 32
NSA_SLC_BLOCK = 64
NSA_TOP_N = 16
NSA_WINDOW = 512
NSA_CMP_HIDDEN = 256
SSM_HEADS = 8
SSM_HEAD_DIM = 64
SSM_GROUPS = 2
SSM_STATE = 128
SSM_CONV = 4
RET_HEADS = 4
RET_DK = 128
REL_BUCKETS = 32
REL_EXACT = 16
REL_MAX_DIST = 2048
IN_SIZES = ((GROUP_W,) * 4 + (GROUP_W,) + (128,) * 6 + (24,)
            + (GROUP_W, 1024, SSM_HEADS) + (GROUP_W,) * 4)
IN_SPLITS = tuple(int(v) for v in np.cumsum(IN_SIZES)[:-1])

LANES = 128
VMEM_LIMIT = 56 * 1024 * 1024

COL = dict(hq=0, hf=4, hi=8, hg=12, nqw=16, sxbc=24, sz=32, rq=36, rk=40, rv=44, rg=48)
NCOL = 52
KV_COLS = ("nkc", "nvc", "nks", "nvs", "nkw", "nvw", "small")
GATE_LANE0 = 0
DT_LANE0 = 24

CHUNK = 128
TQ = 128
TK = 1024
BAND_OFF = 2
CMP_BAND_OFF = 1
WIN_SPAN = NSA_WINDOW + 2 * TQ
NEG = -1e30


def _params(sem):
    return pltpu.CompilerParams(dimension_semantics=sem, vmem_limit_bytes=VMEM_LIMIT)


def _dot(a, b):
    return jnp.dot(a, b, preferred_element_type=F32)


def _dot_nt(a, b):
    return lax.dot_general(a, b, (((1,), (1,)), ((), ())), preferred_element_type=F32)


def _split3(x):
    hi = x.astype(BF16)
    r1 = x - hi.astype(F32)
    mid = r1.astype(BF16)
    return hi, mid, (r1 - mid.astype(F32)).astype(BF16)


def _exact_left_dot(w, x):
    n = x.shape[1]
    y = _dot(w, jnp.concatenate(_split3(x), axis=1))
    return y[:, 0:n] + y[:, n:2 * n] + y[:, 2 * n:3 * n]


def _exact_right_dot(x, w):
    n = x.shape[0]
    y = _dot(jnp.concatenate(_split3(x), axis=0), w)
    return y[0:n] + y[n:2 * n] + y[2 * n:3 * n]


def _sigmoid(x):
    return 1.0 / (1.0 + jnp.exp(-x))


def _silu(x):
    return x * _sigmoid(x)


def _softplus(x):
    return jnp.maximum(x, 0.0) + jnp.log1p(jnp.exp(-jnp.abs(x)))


def _layer_norm(r, g, b):
    mu = jnp.mean(r, axis=-1, keepdims=True)
    d = r - mu
    var = jnp.mean(d * d, axis=-1, keepdims=True)
    return d * lax.rsqrt(var + 1e-5) * g + b


def _ffn_kernel(x_ref, w1_ref, w3_ref, w2_ref, g_ref, b_ref, o_ref, acc_ref, xb_ref):
    j = pl.program_id(1)

    @pl.when(j == 0)
    def _():
        xb_ref[...] = x_ref[...].astype(BF16)
        acc_ref[...] = jnp.zeros_like(acc_ref)

    xb = xb_ref[...]
    h1 = _dot(xb, w1_ref[...])
    h3 = _dot(xb, w3_ref[...])
    a = (_silu(h1) * h3).astype(BF16)
    acc_ref[...] += _dot(a, w2_ref[...])

    @pl.when(j == pl.num_programs(1) - 1)
    def _():
        r = ALPHA * x_ref[...] + 0.5 * acc_ref[...]
        o_ref[...] = _layer_norm(r, g_ref[...], b_ref[...])


def _ffn(x, w1, w3, w2, g, b, tm, tf):
    T, D = x.shape
    F = w1.shape[1]
    return pl.pallas_call(
        _ffn_kernel,
        grid=(T // tm, F // tf),
        in_specs=[
            pl.BlockSpec((tm, D), lambda i, j: (i, 0)),
            pl.BlockSpec((D, tf), lambda i, j: (0, j)),
            pl.BlockSpec((D, tf), lambda i, j: (0, j)),
            pl.BlockSpec((tf, D), lambda i, j: (j, 0)),
            pl.BlockSpec((1, D), lambda i, j: (0, 0)),
            pl.BlockSpec((1, D), lambda i, j: (0, 0)),
        ],
        out_specs=pl.BlockSpec((tm, D), lambda i, j: (i, 0)),
        out_shape=jax.ShapeDtypeStruct((T, D), F32),
        scratch_shapes=[pltpu.VMEM((tm, D), F32), pltpu.VMEM((tm, D), BF16)],
        compiler_params=_params(("parallel", "arbitrary")),
        name="ffn_ln",
    )(x, w1, w3, w2, g, b)


def _proj_kernel(x_ref, w_ref, o_ref, xb_ref):
    @pl.when(pl.program_id(1) == 0)
    def _():
        xb_ref[...] = x_ref[...].astype(BF16)

    o_ref[...] = _dot(xb_ref[...], w_ref[...])


def _proj(x, w, tm, tn):
    T, D = x.shape
    N = w.shape[1]
    return pl.pallas_call(
        _proj_kernel,
        grid=(T // tm, N // tn),
        in_specs=[pl.BlockSpec((tm, D), lambda i, j: (i, 0)),
                  pl.BlockSpec((D, tn), lambda i, j: (0, j))],
        out_specs=pl.BlockSpec((tm, tn), lambda i, j: (i, j)),
        out_shape=jax.ShapeDtypeStruct((T, N), F32),
        scratch_shapes=[pltpu.VMEM((tm, D), BF16)],
        compiler_params=_params(("parallel", "arbitrary")),
        name="in_proj",
    )(x, w)


def _kvproj_kernel(x_ref, w_ref, kc_ref, vc_ref, ks_ref, vs_ref, kw_ref, vw_ref, sm_ref):
    tm = x_ref.shape[0]
    y = _dot(x_ref[...].astype(BF16), w_ref[...])
    piece = lambda n: y[:, n * LANES:(n + 1) * LANES]
    kc_ref[...] = piece(0)
    vc_ref[...] = piece(1)
    row = pl.program_id(0) * tm + lax.broadcasted_iota(jnp.int32, (tm, LANES), 0)
    lane = lax.broadcasted_iota(jnp.int32, (tm, LANES), 1)
    onehot = ((row // NSA_SLC_BLOCK) % (TK // NSA_SLC_BLOCK) == lane).astype(BF16)
    ks_ref[...] = jnp.concatenate([piece(2).astype(BF16), onehot], axis=1)
    vs_ref[...] = jnp.concatenate([piece(3).astype(BF16), jnp.ones((tm, LANES), BF16)], axis=1)
    kw_ref[...] = piece(4).astype(BF16)
    vw_ref[...] = piece(5).astype(BF16)
    sm_ref[...] = piece(6)


def _kvproj(x, w, tm):
    T, D = x.shape
    narrow = lambda dt, width=LANES: (pl.BlockSpec((tm, width), lambda i: (i, 0)),
                                      jax.ShapeDtypeStruct((T, width), dt))
    outs = [narrow(F32), narrow(F32), narrow(BF16, 2 * LANES), narrow(BF16, 2 * LANES),
            narrow(BF16), narrow(BF16), narrow(F32)]
    return pl.pallas_call(
        _kvproj_kernel,
        grid=(T // tm,),
        in_specs=[pl.BlockSpec((tm, D), lambda i: (i, 0)), pl.BlockSpec(w.shape, lambda i: (0, 0))],
        out_specs=[o[0] for o in outs],
        out_shape=[o[1] for o in outs],
        compiler_params=_params(("parallel",)),
        name="kv_proj",
    )(x, w)


def _outproj_kernel(x_ref, oa_ref, ob_ref, oc_ref, od_ref, wa_ref, wb_ref, wc_ref, wd_ref,
                    g_ref, b_ref, o_ref):
    mix = (_dot(oa_ref[...], wa_ref[...]) + _dot(ob_ref[...], wb_ref[...])
           + _dot(oc_ref[...], wc_ref[...]) + _dot(od_ref[...], wd_ref[...]))
    o_ref[...] = _layer_norm(ALPHA * x_ref[...] + mix, g_ref[...], b_ref[...])


def _outproj(x, oa, ob, oc, od, wa, wb, wc, wd, g, b, tm):
    T, D = x.shape
    row = lambda a: pl.BlockSpec((tm, a.shape[1]), lambda i: (i, 0))
    full = lambda a: pl.BlockSpec(a.shape, lambda i: (0, 0))
    return pl.pallas_call(
        _outproj_kernel,
        grid=(T // tm,),
        in_specs=[row(x), row(oa), row(ob), row(oc), row(od),
                  full(wa), full(wb), full(wc), full(wd), full(g), full(b)],
        out_specs=row(x),
        out_shape=jax.ShapeDtypeStruct((T, D), F32),
        compiler_params=_params(("parallel",)),
        name="out_proj_ln",
    )(x, oa, ob, oc, od, wa, wb, wc, wd, g, b)


def _hgrn_tables(C):
    i = np.arange(C)[:, None]
    ip = np.arange(C)[None, :]
    seg = [(ip <= i),
           (ip > i)]
    masks = [np.eye(C, dtype=bool)]
    s = C // 2
    while s >= 1:
        blk = i // s
        seg.append(np.where(blk % 2 == 1, (ip > blk * s) & (ip <= i), (ip > i) & (ip <= (blk + 1) * s)))
        masks.append((blk % 2 == 1) & (ip // s == blk - 1))
        s //= 2
    seg = np.concatenate([x.astype(np.float32) for x in seg], axis=0)
    return seg, np.stack([m.astype(np.float32) for m in masks])


def _hgrn_kernel(q_ref, f_ref, i_ref, g_ref, llb_ref, l1m_ref, oml_ref, ng_ref,
                 seg_ref, msk_ref, o_ref, st_ref):
    @pl.when(pl.program_id(1) == 0)
    def _():
        st_ref[...] = jnp.zeros_like(st_ref)

    C = q_ref.shape[0]
    nlev = msk_ref.shape[0] - 1
    q = _silu(q_ref[...])
    z = f_ref[...]
    log_sig = jnp.minimum(z, 0.0) - jnp.log1p(jnp.exp(-jnp.abs(z)))
    cc = l1m_ref[...] + log_sig
    llb = llb_ref[...]
    logf = jnp.maximum(llb, cc) + jnp.log1p(jnp.exp(-jnp.abs(llb - cc)))
    k = oml_ref[...] * _sigmoid(-z)
    v = i_ref[...]
    seg = _exact_left_dot(seg_ref[...], logf)
    outs = []
    for h in range(HG_HEADS):
        sl = slice(h * LANES, (h + 1) * LANES)
        qh, kh, vh = q[:, sl], k[:, sl], v[:, sl]
        a = msk_ref[0] * _dot_nt(qh.astype(BF16), kh.astype(BF16))
        for l in range(nlev):
            dec = jnp.exp(seg[(2 + l) * C:(3 + l) * C, sl])
            a = a + msk_ref[1 + l] * _dot_nt((qh * dec).astype(BF16), (kh * dec).astype(BF16))
        b = seg[0:C, sl]
        st = st_ref[h]
        o = _dot(a.astype(BF16), vh.astype(BF16))
        o = o + _dot_nt((qh * jnp.exp(b)).astype(BF16), st.astype(BF16))
        kd = (kh * jnp.exp(seg[C:2 * C, sl])).astype(BF16)
        st_ref[h] = st * jnp.exp(b[C - 1:C, :]) + _dot(vh.T.astype(BF16), kd)
        outs.append(o * lax.rsqrt(jnp.mean(o * o, axis=-1, keepdims=True) + 1e-6))
    o = jnp.concatenate(outs, axis=1)
    o_ref[...] = (o * ng_ref[...] * _silu(g_ref[...])).astype(o_ref.dtype)


def _hgrn(proj, B, S, llb, l1m, oml, ng):
    C = CHUNK
    nc = S // C
    seg, msk = _hgrn_tables(C)
    seg, msk = jnp.asarray(seg, BF16), jnp.asarray(msk)
    col = lambda name: pl.BlockSpec((C, GROUP_W), lambda b, c, n=COL[name] // 4: (b * nc + c, n))
    vec = pl.BlockSpec((1, GROUP_W), lambda b, c: (0, 0))
    full2 = lambda a: pl.BlockSpec(a.shape, lambda b, c: (0, 0))
    return pl.pallas_call(
        _hgrn_kernel,
        grid=(B, nc),
        in_specs=[col("hq"), col("hf"), col("hi"), col("hg"), vec, vec, vec, vec,
                  full2(seg), pl.BlockSpec(msk.shape, lambda b, c: (0, 0, 0))],
        out_specs=pl.BlockSpec((C, GROUP_W), lambda b, c: (b * nc + c, 0)),
        out_shape=jax.ShapeDtypeStruct((B * S, GROUP_W), BF16),
        scratch_shapes=[pltpu.VMEM((HG_HEADS, LANES, LANES), F32)],
        compiler_params=_params(("parallel", "arbitrary")),
        name="hgrn2",
    )(proj, proj, proj, proj, llb, l1m, oml, ng, seg, msk)


def _ssd_kernel(z_ref, xbc_ref, sm_ref, cw_ref, cb_ref, dtb_ref, aneg_ref, dsk_ref, ng_ref,
                ex_ref, o_ref, tail_ref, st_ref):
    @pl.when(pl.program_id(1) == 0)
    def _():
        tail_ref[...] = jnp.zeros_like(tail_ref)
        st_ref[...] = jnp.zeros_like(st_ref)

    L = xbc_ref.shape[0]
    x = xbc_ref[...]
    xe = jnp.concatenate([tail_ref[...], x], axis=0)
    cw = cw_ref[...]
    conv = cb_ref[...]
    for kk in range(SSM_CONV):
        conv = conv + cw[kk:kk + 1, :] * xe[5 + kk:5 + kk + L, :]
    tail_ref[...] = x[L - 8:L, :]
    conv = _silu(conv)
    xs = conv[:, 0:GROUP_W]
    bm = conv[:, GROUP_W:GROUP_W + 256]
    cm = conv[:, GROUP_W + 256:GROUP_W + 512]

    dtf = _softplus(sm_ref[...] + dtb_ref[...])
    la = dtf * aneg_ref[...]
    ri = lax.broadcasted_iota(jnp.int32, (L, L), 0)
    ci = lax.broadcasted_iota(jnp.int32, (L, L), 1)
    tri = ri >= ci
    bfull = _exact_left_dot(tri.astype(BF16), la)
    ex = ex_ref[...]
    bexp = _exact_right_dot(bfull, ex)
    dtexp = _exact_right_dot(dtf, ex)
    b_t = bfull.T
    xdt = xs * dtexp
    lane = lax.broadcasted_iota(jnp.int32, (L, LANES), 1)

    scores = []
    for g in range(SSM_GROUPS):
        cg = cm[:, g * SSM_STATE:(g + 1) * SSM_STATE].astype(BF16)
        bg = bm[:, g * SSM_STATE:(g + 1) * SSM_STATE].astype(BF16)
        cb = _dot_nt(cg, bg)
        for hh in range(SSM_HEADS // SSM_GROUPS):
            h = g * (SSM_HEADS // SSM_GROUPS) + hh
            bcol = bfull[:, DT_LANE0 + h:DT_LANE0 + h + 1]
            brow = b_t[DT_LANE0 + h:DT_LANE0 + h + 1, :]
            dec = jnp.exp(jnp.where(tri, bcol - brow, NEG))
            scores.append((cb * dec).astype(BF16))
    y_pairs = []
    for u in range(SSM_HEADS // 2):
        slab = xdt[:, u * LANES:(u + 1) * LANES]
        lo = jnp.where(lane < SSM_HEAD_DIM, slab, 0.0).astype(BF16)
        hi = jnp.where(lane >= SSM_HEAD_DIM, slab, 0.0).astype(BF16)
        y_pairs.append(_dot(scores[2 * u], lo) + _dot(scores[2 * u + 1], hi))
    y_intra = jnp.concatenate(y_pairs, axis=1)

    blast = bexp[L - 1:L, :]
    w = (xdt * jnp.exp(blast - bexp)).astype(BF16)
    y_inter = []
    for g in range(SSM_GROUPS):
        gs = slice(g * 256, (g + 1) * 256)
        cg = cm[:, g * SSM_STATE:(g + 1) * SSM_STATE].astype(BF16)
        st = st_ref[g]
        y_inter.append(_dot(cg, st.astype(BF16)))
        bg_t = bm[:, g * SSM_STATE:(g + 1) * SSM_STATE].T.astype(BF16)
        st_ref[g] = st * jnp.exp(blast[:, gs]) + _dot(bg_t, w[:, gs])
    y = y_intra + jnp.concatenate(y_inter, axis=1) * jnp.exp(bexp) + dsk_ref[...] * xs
    y = y * _silu(z_ref[...])
    halves = []
    for g in range(SSM_GROUPS):
        seg = y[:, g * 256:(g + 1) * 256]
        halves.append(seg * lax.rsqrt(jnp.mean(seg * seg, axis=-1, keepdims=True) + 1e-6))
    o_ref[...] = (jnp.concatenate(halves, axis=1) * ng_ref[...]).astype(o_ref.dtype)


def _ssd(proj, small, B, S, cw, cb, dtb, aneg, dsk, ng):
    L = CHUNK
    nc = S // L
    ex = np.zeros((LANES, GROUP_W), np.float32)
    for h in range(SSM_HEADS):
        ex[DT_LANE0 + h, h * SSM_HEAD_DIM:(h + 1) * SSM_HEAD_DIM] = 1.0
    ex = jnp.asarray(ex, BF16)
    full2 = lambda a: pl.BlockSpec(a.shape, lambda b, c: (0, 0))
    return pl.pallas_call(
        _ssd_kernel,
        grid=(B, nc),
        in_specs=[
            pl.BlockSpec((L, GROUP_W), lambda b, c: (b * nc + c, COL["sz"] // 4)),
            pl.BlockSpec((L, 1024), lambda b, c: (b * nc + c, COL["sxbc"] // 8)),
            pl.BlockSpec((L, LANES), lambda b, c: (b * nc + c, 0)),
            full2(cw), full2(cb), full2(dtb), full2(aneg), full2(dsk), full2(ng), full2(ex)],
        out_specs=pl.BlockSpec((L, GROUP_W), lambda b, c: (b * nc + c, 0)),
        out_shape=jax.ShapeDtypeStruct((B * S, GROUP_W), BF16),
        scratch_shapes=[pltpu.VMEM((8, 1024), F32), pltpu.VMEM((SSM_GROUPS, SSM_STATE, 256), F32)],
        compiler_params=_params(("parallel", "arbitrary")),
        name="ssd",
    )(proj, proj, small, cw, cb, dtb, aneg, dsk, ng, ex)


def _ret_kernel(q_ref, k_ref, v_ref, g_ref, cos_ref, sin_ref, dec_ref, qs_ref, ks_ref, sd_ref,
                o_ref, st_ref):
    @pl.when(pl.program_id(1) == 0)
    def _():
        st_ref[...] = jnp.zeros_like(st_ref)

    cos = cos_ref[...]
    sin = sin_ref[...]
    outs = []
    for h in range(RET_HEADS):
        sl = slice(h * LANES, (h + 1) * LANES)
        qh = q_ref[:, sl]
        kh = k_ref[:, sl]
        qh = qh * cos + pltpu.roll(qh, RET_DK // 2, axis=1) * sin
        kh = (kh * cos + pltpu.roll(kh, RET_DK // 2, axis=1) * sin) * (RET_DK ** -0.5)
        vh = v_ref[:, sl].astype(BF16)
        sc = (_dot_nt(qh.astype(BF16), kh.astype(BF16)) * dec_ref[h]).astype(BF16)
        st = st_ref[h]
        y = _dot(sc, vh) + _dot((qh * qs_ref[:, sl]).astype(BF16), st.astype(BF16))
        kd_t = (kh * ks_ref[:, sl]).T.astype(BF16)
        st_ref[h] = st * sd_ref[h] + _dot(kd_t, vh)
        mu = jnp.mean(y, axis=-1, keepdims=True)
        d = y - mu
        outs.append(d * lax.rsqrt(jnp.mean(d * d, axis=-1, keepdims=True) + 1e-5))
    o_ref[...] = (_silu(g_ref[...]) * jnp.concatenate(outs, axis=1)).astype(o_ref.dtype)


def _retention(proj, B, S, cos_t, sin_t):
    L = CHUNK
    nc = S // L
    lg = jnp.log(1.0 - 2.0 ** (-5.0 - jnp.arange(RET_HEADS, dtype=F32)))
    i = jnp.arange(L, dtype=F32)
    diff = i[:, None] - i[None, :]
    dec = jnp.where(diff >= 0, jnp.exp(lg[:, None, None] * jnp.maximum(diff, 0.0)), 0.0)
    rep = lambda t: jnp.repeat(t, LANES, axis=1)
    qs = rep(jnp.exp((i[:, None] + 1.0) * lg[None, :]))
    ks = rep(jnp.exp((L - 1.0 - i[:, None]) * lg[None, :]))
    sd = jnp.broadcast_to(jnp.exp(L * lg)[:, None, None], (RET_HEADS, LANES, LANES))
    col = lambda name: pl.BlockSpec((L, GROUP_W), lambda b, c, n=COL[name] // 4: (b * nc + c, n))
    return pl.pallas_call(
        _ret_kernel,
        grid=(B, nc),
        in_specs=[col("rq"), col("rk"), col("rv"), col("rg"),
                  pl.BlockSpec((L, LANES), lambda b, c: (c, 0)),
                  pl.BlockSpec((L, LANES), lambda b, c: (c, 0)),
                  pl.BlockSpec((RET_HEADS, L, L), lambda b, c: (0, 0, 0)),
                  pl.BlockSpec((L, GROUP_W), lambda b, c: (0, 0)),
                  pl.BlockSpec((L, GROUP_W), lambda b, c: (0, 0)),
                  pl.BlockSpec((RET_HEADS, LANES, LANES), lambda b, c: (0, 0, 0))],
        out_specs=pl.BlockSpec((L, GROUP_W), lambda b, c: (b * nc + c, 0)),
        out_shape=jax.ShapeDtypeStruct((B * S, GROUP_W), BF16),
        scratch_shapes=[pltpu.VMEM((RET_HEADS, RET_DK, RET_DK), F32)],
        compiler_params=_params(("parallel", "arbitrary")),
        name="retention",
    )(proj, proj, proj, proj, cos_t, sin_t, dec, qs, ks, sd)


def _t5_bucket(dist):
    n = jnp.maximum(dist, 0)
    nf = jnp.maximum(n, 1).astype(F32)
    large = REL_EXACT + (jnp.log(nf / REL_EXACT) / math.log(REL_MAX_DIST / REL_EXACT)
                         * (REL_BUCKETS - REL_EXACT)).astype(jnp.int32)
    return jnp.where(n < REL_EXACT, n, jnp.minimum(large, REL_BUCKETS - 1))


def _head_bias(bucket, rel_ref):
    rows, cols = bucket.shape
    per_head = []
    for h in range(NSA_HEADS):
        tbl = jnp.broadcast_to(rel_ref[h:h + 1, :], (rows, LANES))
        chunks = [jnp.take_along_axis(tbl, bucket[:, c:c + LANES], axis=1)
                  for c in range(0, cols, LANES)]
        per_head.append(chunks[0] if len(chunks) == 1 else jnp.concatenate(chunks, axis=1))
    return jnp.stack(per_head, axis=0)


def _stack_heads(qw):
    return jnp.concatenate([qw[:, h * LANES:(h + 1) * LANES] for h in range(NSA_HEADS)],
                           axis=0).astype(BF16)


def _cmp_kernel(g_ref, pe_ref, w1a_ref, w1b_ref, w2_ref, o_ref):
    nb = g_ref.shape[0]
    gw = g_ref.shape[1] // 4
    pe = pe_ref[...]
    slabs = [g_ref[:, s * gw:(s + 1) * gw] for s in range(4)]
    nxt0 = pltpu.roll(slabs[0], nb - 1, axis=0)
    for s in range(4):
        a = (slabs[s] + pe[0:1, :]).astype(BF16)
        bn = ((slabs[s + 1] if s < 3 else nxt0) + pe[1:2, :]).astype(BF16)
        hid = _silu(_dot(a, w1a_ref[...]) + _dot(bn, w1b_ref[...]))
        o_ref[s * nb:(s + 1) * nb, :] = _dot(hid.astype(BF16), w2_ref[...]).astype(o_ref.dtype)


def _compress(g, pe2, w1a, w1b, w2bd):
    B, nb, gw4 = g.shape
    full2 = lambda a: pl.BlockSpec(a.shape, lambda b: (0, 0))
    slab_major = pl.pallas_call(
        _cmp_kernel,
        grid=(B,),
        in_specs=[pl.BlockSpec((None, nb, gw4), lambda b: (b, 0, 0)),
                  full2(pe2), full2(w1a), full2(w1b), full2(w2bd)],
        out_specs=pl.BlockSpec((None, 4 * nb, LANES), lambda b: (b, 0, 0)),
        out_shape=jax.ShapeDtypeStruct((B, 4 * nb, LANES), BF16),
        compiler_params=_params(("parallel",)),
        name="nsa_compress",
    )(g, pe2, w1a, w1b, w2bd)
    return slab_major.reshape(B, 4, nb, LANES).transpose(0, 2, 1, 3).reshape(B, 4 * nb, LANES)


def _cmpattn_kernel(q_ref, kc_ref, vc_ref, band_ref, st_ref, oc_ref, selb_ref):
    tq = q_ref.shape[0]
    ncmp = kc_ref.shape[0]
    nb = ncmp // 4
    R = NSA_HEADS * tq
    q0 = pl.program_id(1) * tq
    nd = band_ref.shape[0]
    Q = _stack_heads(q_ref[...])
    s3 = _dot_nt(Q, kc_ref[...]).reshape(NSA_HEADS, tq, ncmp)
    parts = []
    for ch in range(ncmp // LANES):
        d = jnp.clip((q0 - ch * LANES * NSA_CMP_STRIDE) // tq + CMP_BAND_OFF, 0, nd - 1)
        parts.append(s3[:, :, ch * LANES:(ch + 1) * LANES] + band_ref[d])
    s3 = jnp.concatenate(parts, axis=2)
    mx = jnp.max(s3, axis=-1, keepdims=True)
    e = jnp.exp(s3 - mx)
    live = (q0 + lax.broadcasted_iota(jnp.int32, (tq, 1), 0) >= NSA_CMP_BLOCK - 1)[None]
    p = e * jnp.where(live, 1.0 / jnp.sum(e, axis=-1, keepdims=True), 0.0)
    oc_ref[...] = _dot(p.reshape(R, ncmp).astype(BF16), vc_ref[...])

    ps = p.reshape(NSA_KV, NSA_GROUP, tq, ncmp).sum(axis=1).reshape(NSA_KV * tq, ncmp)
    imp = _exact_right_dot(ps, st_ref[...])
    j = lax.broadcasted_iota(jnp.int32, (NSA_KV * tq, nb), 1)
    t = q0 + (lax.broadcasted_iota(jnp.int32, (NSA_KV * tq, nb), 0) % tq)
    cur = t // NSA_SLC_BLOCK
    forced = (j == 0) | (j == cur) | (j == cur - 1)
    score = jnp.where(j > cur, -1.0, jnp.where(forced, NSA_GROUP + 1.0, imp))
    score = score.T
    jf = lax.broadcasted_iota(jnp.int32, score.shape, 0).astype(F32)
    sel = score == NSA_GROUP + 1.0
    score = jnp.where(sel, -jnp.inf, score)
    for _ in range(min(NSA_TOP_N, nb) - 3):
        best = jnp.max(score, axis=0, keepdims=True)
        first = jnp.min(jnp.where(score == best, jf, float(nb)), axis=0, keepdims=True)
        hit = jf == first
        sel = sel | hit
        score = jnp.where(hit, -jnp.inf, score)
    selb = jnp.where(sel, 0.0, NEG).T.astype(selb_ref.dtype)
    selb_ref[:, 0:nb] = selb[0:tq]
    selb_ref[:, nb:2 * nb] = selb[tq:2 * tq]


def _cmpattn(proj, kcmp, vcmp, cband, B, S):
    tq = TQ
    nqt = S // tq
    ncmp = kcmp.shape[1]
    nb = ncmp // 4
    off = np.arange(ncmp)[:, None] - 4 * np.arange(nb)[None, :]
    stencil = np.where((off >= 0) & (off <= 2), 1.0, np.where((off == -1) | (off == 3), 0.5, 0.0))
    stencil = jnp.asarray(stencil, BF16)
    once = pl.Buffered(1)
    return pl.pallas_call(
        _cmpattn_kernel,
        grid=(B, nqt),
        in_specs=[pl.BlockSpec((tq, 1024), lambda b, i: (b * nqt + i, COL["nqw"] // 8)),
                  pl.BlockSpec((None, ncmp, LANES), lambda b, i: (b, 0, 0)),
                  pl.BlockSpec((None, ncmp, LANES), lambda b, i: (b, 0, 0)),
                  pl.BlockSpec(cband.shape, lambda b, i: (0, 0, 0, 0), pipeline_mode=once),
                  pl.BlockSpec(stencil.shape, lambda b, i: (0, 0))],
        out_specs=[pl.BlockSpec((NSA_HEADS * tq, LANES), lambda b, i: (b * nqt + i, 0)),
                   pl.BlockSpec((tq, 2 * nb), lambda b, i: (b * nqt + i, 0))],
        out_shape=[jax.ShapeDtypeStruct((B * S * NSA_HEADS, LANES), F32),
                   jax.ShapeDtypeStruct((B * S, 2 * nb), BF16)],
        compiler_params=_params(("parallel", "arbitrary")),
        name="nsa_cmp_attn_topk",
    )(proj, kcmp, vcmp, cband, stencil)


def _band_kernel(rel_ref, o_ref, *, entry_off, key_step, key_end):
    tq = o_ref.shape[1]
    delta = (pl.program_id(0) - entry_off) * tq
    row = lax.broadcasted_iota(jnp.int32, (tq, LANES), 0)
    col = lax.broadcasted_iota(jnp.int32, (tq, LANES), 1)
    dist = delta + row - (col * key_step + key_end)
    bias = _head_bias(_t5_bucket(dist), rel_ref)
    o_ref[...] = jnp.where((dist >= 0)[None], bias, NEG)


def _band_table(rel_t, tq, entry_off, key_step=1, key_end=0):
    nd = -(-(REL_MAX_DIST + key_step * (LANES - 1) + key_end) // tq) + entry_off + 1
    return pl.pallas_call(
        functools.partial(_band_kernel, entry_off=entry_off, key_step=key_step, key_end=key_end),
        grid=(nd,),
        in_specs=[pl.BlockSpec(rel_t.shape, lambda d: (0, 0))],
        out_specs=pl.BlockSpec((None, NSA_HEADS, tq, LANES), lambda d: (d, 0, 0, 0)),
        out_shape=jax.ShapeDtypeStruct((nd, NSA_HEADS, tq, LANES), F32),
        compiler_params=_params(("parallel",)),
        name="nsa_bias_band",
    )(rel_t)


def _selattn_kernel(q_ref, selb_ref, ks_ref, vs_ref, band_ref, sp_ref, os_ref,
                    qaug_ref, msel_ref, s_ref, m_ref, acc_ref):
    tq = q_ref.shape[0]
    R = NSA_HEADS * tq
    nb = selb_ref.shape[1] // 2
    n_kt = msel_ref.shape[0]
    tk = ks_ref.shape[0] // n_kt
    nd = band_ref.shape[0]
    q0 = pl.program_id(1) * tq
    qaug_ref[:, 0:LANES] = _stack_heads(q_ref[...])
    selb2 = jnp.concatenate([selb_ref[:, 0:nb], selb_ref[:, nb:2 * nb]], axis=0)
    spread = _dot(selb2, sp_ref[...])
    for c in range(n_kt):
        msel_ref[c] = spread[:, c * LANES:(c + 1) * LANES].astype(BF16)
    m_ref[...] = jnp.full(m_ref.shape, NEG, F32)
    acc_ref[...] = jnp.zeros_like(acc_ref)
    n_tiles = (q0 + tq - 1) // tk + 1

    def scores(c, slot):
        c = jnp.minimum(c, n_kt - 1)
        m2 = msel_ref[c]
        qaug_ref[:, LANES:2 * LANES] = jnp.concatenate(
            [m2[0:tq]] * NSA_GROUP + [m2[tq:2 * tq]] * NSA_GROUP, axis=0)
        s_ref[slot] = _dot_nt(qaug_ref[...], ks_ref[pl.ds(pl.multiple_of(c * tk, tk), tk), :])

    def accumulate(c, slot):
        k0 = pl.multiple_of(c * tk, tk)
        s3 = s_ref[slot].reshape(NSA_HEADS, tq, tk)
        parts = []
        for ch in range(tk // LANES):
            d = jnp.clip((q0 - k0) // tq - ch * (LANES // tq) + BAND_OFF, 0, nd - 1)
            parts.append(s3[:, :, ch * LANES:(ch + 1) * LANES] + band_ref[d])
        s = jnp.concatenate(parts, axis=2).reshape(R, tk)
        m_old = m_ref[...]
        m_new = jnp.maximum(m_old, jnp.max(s, axis=-1, keepdims=True))
        alpha = jnp.exp(m_old - m_new)
        p = jnp.exp(s - jnp.tile(m_new, (1, tk // LANES)))
        acc_ref[...] = jnp.tile(alpha, (1, 2)) * acc_ref[...] + _dot(p.astype(BF16), vs_ref[pl.ds(k0, tk), :])
        m_ref[...] = m_new

    def pair(i, carry):
        c = 2 * i
        scores(c + 1, 1)
        accumulate(c, 0)
        scores(c + 2, 0)
        accumulate(c + 1, 1)
        return carry

    scores(0, 0)
    lax.fori_loop(0, n_tiles // 2, pair, 0)

    @pl.when(n_tiles % 2 == 1)
    def _():
        accumulate(n_tiles - 1, 0)

    acc = acc_ref[...]
    os_ref[...] = acc[:, 0:LANES] / acc[:, LANES:2 * LANES]


def _selattn(proj, selb, ks, vs, band, B, S):
    tq = TQ
    tk = min(TK, S)
    n_kt = S // tk
    bpt = tk // NSA_SLC_BLOCK
    nqt = S // tq
    nb = S // NSA_SLC_BLOCK
    R = NSA_HEADS * tq
    sp = np.zeros((nb, n_kt * LANES), np.float32)
    sp[np.arange(nb), (np.arange(nb) // bpt) * LANES + np.arange(nb) % bpt] = 1.0
    once = pl.Buffered(1)
    return pl.pallas_call(
        _selattn_kernel,
        grid=(B, nqt),
        in_specs=[pl.BlockSpec((tq, 1024), lambda b, i: (b * nqt + i, COL["nqw"] // 8)),
                  pl.BlockSpec((tq, 2 * nb), lambda b, i: (b * nqt + i, 0)),
                  pl.BlockSpec((None, S, 2 * LANES), lambda b, i: (b, 0, 0), pipeline_mode=once),
                  pl.BlockSpec((None, S, 2 * LANES), lambda b, i: (b, 0, 0), pipeline_mode=once),
                  pl.BlockSpec(band.shape, lambda b, i: (0, 0, 0, 0), pipeline_mode=once),
                  pl.BlockSpec(sp.shape, lambda b, i: (0, 0))],
        out_specs=pl.BlockSpec((R, LANES), lambda b, i: (b * nqt + i, 0)),
        out_shape=jax.ShapeDtypeStruct((B * S * NSA_HEADS, LANES), F32),
        scratch_shapes=[pltpu.VMEM((R, 2 * LANES), BF16), pltpu.VMEM((n_kt, 2 * tq, LANES), BF16),
                        pltpu.VMEM((2, R, tk), F32), pltpu.VMEM((R, LANES), F32),
                        pltpu.VMEM((R, 2 * LANES), F32)],
        compiler_params=_params(("parallel", "arbitrary")),
        name="nsa_sel_attn",
    )(proj, selb, ks, vs, band, jnp.asarray(sp, BF16))


def _winattn_kernel(q_ref, sm_ref, oc_ref, os_ref, kw_ref, vw_ref, band_ref, ng_ref, o_ref):
    tq = q_ref.shape[0]
    R = NSA_HEADS * tq
    S = kw_ref.shape[0]
    span = min(WIN_SPAN, S)
    nd = band_ref.shape[0]
    q0 = pl.program_id(1) * tq
    start = pl.multiple_of(jnp.clip(q0 + tq - span, 0, S - span), tq)
    Q = _stack_heads(q_ref[...])
    kt = kw_ref[pl.ds(start, span), :]
    vt = vw_ref[pl.ds(start, span), :]
    s3 = _dot_nt(Q, kt).reshape(NSA_HEADS, tq, span)
    parts = []
    for ch in range(span // LANES):
        d = jnp.clip((q0 - start) // tq - ch * (LANES // tq) + BAND_OFF, 0, nd - 1)
        parts.append(s3[:, :, ch * LANES:(ch + 1) * LANES] + band_ref[d])
    row = lax.broadcasted_iota(jnp.int32, (tq, span), 0)
    col = lax.broadcasted_iota(jnp.int32, (tq, span), 1)
    in_window = ((q0 + row) - (start + col) < NSA_WINDOW)[None]
    s3 = jnp.where(in_window, jnp.concatenate(parts, axis=2), NEG)
    mx = jnp.max(s3, axis=-1, keepdims=True)
    e = jnp.exp(s3 - mx)
    p = e * (1.0 / jnp.sum(e, axis=-1, keepdims=True))
    ow = _dot(p.reshape(R, span).astype(BF16), vt)

    gates = _sigmoid(sm_ref[...])
    lane = lax.broadcasted_iota(jnp.int32, (tq, LANES), 1)
    heads = []
    ssq = jnp.zeros((tq, 1), F32)
    for h in range(NSA_HEADS):
        rs = slice(h * tq, (h + 1) * tq)
        g = [gates[:, GATE_LANE0 + 3 * h + br:GATE_LANE0 + 3 * h + br + 1] for br in range(3)]
        oh = g[0] * oc_ref[rs, :] + g[1] * os_ref[rs, :] + g[2] * ow[rs, :]
        kv = h // NSA_GROUP
        valid = (lane >= kv * NSA_HEAD_DIM) & (lane < (kv + 1) * NSA_HEAD_DIM)
        oh = jnp.where(valid, oh, 0.0)
        ssq = ssq + jnp.sum(oh * oh, axis=-1, keepdims=True)
        heads.append(oh)
    rinv = lax.rsqrt(ssq / GROUP_W + 1e-6)
    o_ref[...] = (jnp.concatenate(heads, axis=1) * rinv * ng_ref[...]).astype(o_ref.dtype)


def _winattn(proj, small, oc, os_, kw, vw, band, ngw, B, S):
    tq = TQ
    nqt = S // tq
    R = NSA_HEADS * tq
    once = pl.Buffered(1)
    return pl.pallas_call(
        _winattn_kernel,
        grid=(B, nqt),
        in_specs=[pl.BlockSpec((tq, 1024), lambda b, i: (b * nqt + i, COL["nqw"] // 8)),
                  pl.BlockSpec((tq, LANES), lambda b, i: (b * nqt + i, 0)),
                  pl.BlockSpec((R, LANES), lambda b, i: (b * nqt + i, 0)),
                  pl.BlockSpec((R, LANES), lambda b, i: (b * nqt + i, 0)),
                  pl.BlockSpec((None, S, LANES), lambda b, i: (b, 0, 0), pipeline_mode=once),
                  pl.BlockSpec((None, S, LANES), lambda b, i: (b, 0, 0), pipeline_mode=once),
                  pl.BlockSpec(band.shape, lambda b, i: (0, 0, 0, 0), pipeline_mode=once),
                  pl.BlockSpec(ngw.shape, lambda b, i: (0, 0))],
        out_specs=pl.BlockSpec((tq, NSA_HEADS * LANES), lambda b, i: (b * nqt + i, 0)),
        out_shape=jax.ShapeDtypeStruct((B * S, NSA_HEADS * LANES), BF16),
        compiler_params=_params(("parallel", "arbitrary")),
        name="nsa_win_attn_merge",
    )(proj, small, oc, os_, kw, vw, band, ngw)


def _widen_heads(x, axis):
    x = jnp.moveaxis(x, axis, -1)
    lead = x.shape[:-1]
    x = x.reshape(*lead, NSA_KV, NSA_GROUP, 1, NSA_HEAD_DIM)
    sel = jnp.eye(NSA_KV, dtype=x.dtype).reshape(NSA_KV, 1, NSA_KV, 1)
    x = (x * sel).reshape(*lead, NSA_HEADS * LANES)
    return jnp.moveaxis(x, -1, axis)


def _build_w_in(w):
    (hq, hf, hi, hg, nq, nkc, nvc, nks, nvs, nkw, nvw, ngate,
     sz, sxbc, sdt, rq, rk, rv, rg) = jnp.split(w, IN_SPLITS, axis=1)
    D = w.shape[0]
    nqw = _widen_heads(nq * NSA_HEAD_DIM ** -0.5, 1)
    deint = lambda t: t.reshape(D, RET_HEADS, RET_DK // 2, 2).transpose(0, 1, 3, 2).reshape(D, GROUP_W)
    small = jnp.concatenate([ngate, sdt, jnp.zeros((D, LANES - 32), w.dtype)], axis=1)
    wide = [hq, hf, hi, hg, nqw, sxbc, sz, deint(rq), deint(rk), rv, rg]
    narrow = [nkc, nvc, nks, nvs, nkw, nvw, small]
    return jnp.concatenate(wide, axis=1).astype(BF16), jnp.concatenate(narrow, axis=1).astype(BF16)


def _build_cmp_weights(pe, w1, w2):
    w1r = w1.reshape(2, NSA_CMP_STRIDE, NSA_HEAD_DIM, NSA_CMP_HIDDEN)
    eye = jnp.eye(NSA_KV, dtype=w1.dtype)
    big = jnp.einsum("ardc,kj->arkdjc", w1r, eye).reshape(
        2, NSA_CMP_STRIDE * NSA_KV * NSA_HEAD_DIM, NSA_KV * NSA_CMP_HIDDEN)
    w2bd = jnp.einsum("cd,kj->kcjd", w2, eye).reshape(NSA_KV * NSA_CMP_HIDDEN, NSA_KV * NSA_HEAD_DIM)
    per = pe.reshape(2, NSA_CMP_STRIDE, 1, NSA_HEAD_DIM)
    pe2 = jnp.broadcast_to(per, (2, NSA_CMP_STRIDE, NSA_KV, NSA_HEAD_DIM)).reshape(2, -1)
    return pe2, big[0].astype(BF16), big[1].astype(BF16), w2bd.astype(BF16)


def _rotary_tables(S):
    half = RET_DK // 2
    theta = 1.0 / (10000.0 ** jnp.linspace(0.0, 1.0, half, dtype=F32))
    ang = jnp.arange(S, dtype=F32)[:, None] * theta[None, :]
    cos, sin = jnp.cos(ang), jnp.sin(ang)
    return jnp.concatenate([cos, cos], axis=1), jnp.concatenate([-sin, sin], axis=1)


def _mixer(x2, B, S, l, p, lower_bounds, band, cband, cos_t, sin_t):
    T = B * S
    w_wide, w_narrow = _build_w_in(p["w_in"][l])
    proj = _proj(x2, w_wide, tm=min(512, T), tn=NCOL * LANES // 4)
    kc, vc, ks, vs, kw, vw, small = _kvproj(x2, w_narrow, tm=min(512, T))
    row = lambda v: v.reshape(1, -1).astype(F32)

    lb = lower_bounds[l].astype(F32)
    o_a = _hgrn(proj, B, S, row(jnp.log(lb)), row(jnp.log1p(-lb)), row(1.0 - lb),
                row(p["hgrn_norm_g"][l]))

    nb = S // NSA_SLC_BLOCK
    grp = lambda t: t.reshape(B, nb, 4 * NSA_CMP_STRIDE * LANES)
    kcmp = _compress(grp(kc), *_build_cmp_weights(p["nsa_pe_k"][l], p["nsa_w1_k"][l], p["nsa_w2_k"][l]))
    vcmp = _compress(grp(vc), *_build_cmp_weights(p["nsa_pe_v"][l], p["nsa_w1_v"][l], p["nsa_w2_v"][l]))
    o_cmp, selb = _cmpattn(proj, kcmp, vcmp, cband, B, S)
    seq = lambda t: t.reshape(B, S, t.shape[1])
    o_sel = _selattn(proj, selb, seq(ks), seq(vs), band, B, S)
    ngw = _widen_heads(p["nsa_norm_g"][l].astype(F32), 0).reshape(1, -1)
    o_b = _winattn(proj, small, o_cmp, o_sel, seq(kw), seq(vw), band, ngw, B, S)

    lane_vec = lambda v: jnp.zeros((1, LANES), F32).at[0, DT_LANE0:DT_LANE0 + SSM_HEADS].set(v.astype(F32))
    o_c = _ssd(proj, small, B, S, p["ssm_conv_w"][l].astype(F32), row(p["ssm_conv_b"][l]),
               lane_vec(p["ssm_dt_bias"][l]), lane_vec(-jnp.exp(p["ssm_a_log"][l].astype(F32))),
               row(jnp.repeat(p["ssm_d"][l].astype(F32), SSM_HEAD_DIM)), row(p["ssm_norm_g"][l]))

    o_d = _retention(proj, B, S, cos_t, sin_t)

    w_out = p["w_out"][l]
    wa, wb, wc, wd = (w_out[i * GROUP_W:(i + 1) * GROUP_W] for i in range(4))
    return o_a, o_b, o_c, o_d, wa.astype(BF16), _widen_heads(wb, 0).astype(BF16), wc.astype(BF16), wd.astype(BF16)


def kernel(x, ln1_g, ln1_b, ffn1_w1, ffn1_w3, ffn1_w2, ln2_g, ln2_b, w_in, w_out, hgrn_lb_logits, hgrn_norm_g, nsa_pe_k, nsa_w1_k, nsa_w2_k, nsa_pe_v, nsa_w1_v, nsa_w2_v, nsa_norm_g, rel_bias, ssm_conv_w, ssm_conv_b, ssm_dt_bias, ssm_a_log, ssm_d, ssm_norm_g, ln3_g, ln3_b, ffn2_w1, ffn2_w3, ffn2_w2):
    B, S, D = x.shape
    T = B * S
    depth = w_in.shape[0]
    p = dict(w_in=w_in, w_out=w_out, hgrn_norm_g=hgrn_norm_g, nsa_pe_k=nsa_pe_k, nsa_w1_k=nsa_w1_k,
             nsa_w2_k=nsa_w2_k, nsa_pe_v=nsa_pe_v, nsa_w1_v=nsa_w1_v, nsa_w2_v=nsa_w2_v,
             nsa_norm_g=nsa_norm_g, ssm_conv_w=ssm_conv_w, ssm_conv_b=ssm_conv_b,
             ssm_dt_bias=ssm_dt_bias, ssm_a_log=ssm_a_log, ssm_d=ssm_d, ssm_norm_g=ssm_norm_g)
    cum = jnp.cumsum(jax.nn.softmax(hgrn_lb_logits.astype(F32), axis=0), axis=0)
    lower_bounds = cum - cum[:1]
    rel_t = jnp.zeros((NSA_HEADS, LANES), F32).at[:, :REL_BUCKETS].set(rel_bias.astype(F32).T)
    band = _band_table(rel_t, TQ, BAND_OFF)
    cband = _band_table(rel_t, TQ, CMP_BAND_OFF, NSA_CMP_STRIDE, NSA_CMP_BLOCK - 1)
    cos_t, sin_t = _rotary_tables(S)
    row = lambda v: v.reshape(1, -1).astype(F32)
    tm = min(512, T)
    tf = 512 if ffn1_w1.shape[2] % 512 == 0 else ffn1_w1.shape[2]
    x2 = x.reshape(T, D).astype(F32)
    for l in range(depth):
        x2 = _ffn(x2, ffn1_w1[l].astype(BF16), ffn1_w3[l].astype(BF16), ffn1_w2[l].astype(BF16),
                  row(ln1_g[l]), row(ln1_b[l]), tm, tf)
        o_a, o_b, o_c, o_d, wa, wb, wc, wd = _mixer(x2, B, S, l, p, lower_bounds, band, cband, cos_t, sin_t)
        x2 = _outproj(x2, o_a, o_b, o_c, o_d, wa, wb, wc, wd, row(ln2_g[l]), row(ln2_b[l]), min(256, T))
        x2 = _ffn(x2, ffn2_w1[l].astype(BF16), ffn2_w3[l].astype(BF16), ffn2_w2[l].astype(BF16),
                  row(ln3_g[l]), row(ln3_b[l]), tm, tf)
    return x2.reshape(B, S, D).astype(x.dtype)
```

```python
import functools
import math

import numpy as np
import jax
import jax.numpy as jnp
from jax import lax
from jax.experimental import pallas as pl
from jax.experimental.pallas import tpu as pltpu

F32 = jnp.float32
BF16 = jnp.bfloat16

D_MODEL = 2048
DEPTH = 2
GROUP_W = 512
ALPHA = (2 * DEPTH) ** 0.25
HG_HEADS = 4
NSA_HEADS = 8
NSA_KV = 2
NSA_GROUP = 4
NSA_HEAD_DIM = 64
NSA_CMP_STRIDE = 16
NSA_CMP_BLOCK = 32
NSA_SLC_BLOCK = 64
NSA_TOP_N = 16
NSA_WINDOW = 512
NSA_CMP_HIDDEN = 256
SSM_HEADS = 8
SSM_HEAD_DIM = 64
SSM_GROUPS = 2
SSM_STATE = 128
SSM_CONV = 4
RET_HEADS = 4
RET_DK = 128
REL_BUCKETS = 32
REL_EXACT = 16
REL_MAX_DIST = 2048
IN_SIZES = ((GROUP_W,) * 4 + (GROUP_W,) + (128,) * 6 + (24,)
            + (GROUP_W, 1024, SSM_HEADS) + (GROUP_W,) * 4)
IN_SPLITS = tuple(int(v) for v in np.cumsum(IN_SIZES)[:-1])

LANES = 128
VMEM_LIMIT = 56 * 1024 * 1024

COL = dict(hq=0, hf=4, hi=8, hg=12, nqw=16, sxbc=24, sz=32, rq=36, rk=40, rv=44, rg=48)
NCOL = 52
KV_COLS = ("nkc", "nvc", "nks", "nvs", "nkw", "nvw", "small")
GATE_LANE0 = 0
DT_LANE0 = 24

CHUNK = 128
TQ = 128
TK = 1024
BAND_OFF = 2
CMP_BAND_OFF = 1
WIN_SPAN = NSA_WINDOW + 2 * TQ
NEG = -1e30


def _params(sem):
    return pltpu.CompilerParams(dimension_semantics=sem, vmem_limit_bytes=VMEM_LIMIT)


def _dot(a, b):
    return jnp.dot(a, b, preferred_element_type=F32)


def _dot_nt(a, b):
    return lax.dot_general(a, b, (((1,), (1,)), ((), ())), preferred_element_type=F32)


def _split3(x):
    hi = x.astype(BF16)
    r1 = x - hi.astype(F32)
    mid = r1.astype(BF16)
    return hi, mid, (r1 - mid.astype(F32)).astype(BF16)


def _exact_left_dot(w, x):
    n = x.shape[1]
    y = _dot(w, jnp.concatenate(_split3(x), axis=1))
    return y[:, 0:n] + y[:, n:2 * n] + y[:, 2 * n:3 * n]


def _exact_right_dot(x, w):
    n = x.shape[0]
    y = _dot(jnp.concatenate(_split3(x), axis=0), w)
    return y[0:n] + y[n:2 * n] + y[2 * n:3 * n]


def _sigmoid(x):
    return 1.0 / (1.0 + jnp.exp(-x))


def _silu(x):
    return x * _sigmoid(x)


def _softplus(x):
    return jnp.maximum(x, 0.0) + jnp.log1p(jnp.exp(-jnp.abs(x)))


def _layer_norm(r, g, b):
    mu = jnp.mean(r, axis=-1, keepdims=True)
    d = r - mu
    var = jnp.mean(d * d, axis=-1, keepdims=True)
    return d * lax.rsqrt(var + 1e-5) * g + b


def _ffn_kernel(x_ref, w1_ref, w3_ref, w2_ref, g_ref, b_ref, o_ref, acc_ref, xb_ref):
    j = pl.program_id(1)

    @pl.when(j == 0)
    def _():
        xb_ref[...] = x_ref[...].astype(BF16)
        acc_ref[...] = jnp.zeros_like(acc_ref)

    xb = xb_ref[...]
    h1 = _dot(xb, w1_ref[...])
    h3 = _dot(xb, w3_ref[...])
    a = (_silu(h1) * h3).astype(BF16)
    acc_ref[...] += _dot(a, w2_ref[...])

    @pl.when(j == pl.num_programs(1) - 1)
    def _():
        r = ALPHA * x_ref[...] + 0.5 * acc_ref[...]
        o_ref[...] = _layer_norm(r, g_ref[...], b_ref[...])


def _ffn(x, w1, w3, w2, g, b, tm, tf):
    T, D = x.shape
    F = w1.shape[1]
    return pl.pallas_call(
        _ffn_kernel,
        grid=(T // tm, F // tf),
        in_specs=[
            pl.BlockSpec((tm, D), lambda i, j: (i, 0)),
            pl.BlockSpec((D, tf), lambda i, j: (0, j)),
            pl.BlockSpec((D, tf), lambda i, j: (0, j)),
            pl.BlockSpec((tf, D), lambda i, j: (j, 0)),
            pl.BlockSpec((1, D), lambda i, j: (0, 0)),
            pl.BlockSpec((1, D), lambda i, j: (0, 0)),
        ],
        out_specs=pl.BlockSpec((tm, D), lambda i, j: (i, 0)),
        out_shape=jax.ShapeDtypeStruct((T, D), F32),
        scratch_shapes=[pltpu.VMEM((tm, D), F32), pltpu.VMEM((tm, D), BF16)],
        compiler_params=_params(("parallel", "arbitrary")),
        name="ffn_ln",
    )(x, w1, w3, w2, g, b)


def _proj_kernel(x_ref, w_ref, o_ref, xb_ref):
    @pl.when(pl.program_id(1) == 0)
    def _():
        xb_ref[...] = x_ref[...].astype(BF16)

    o_ref[...] = _dot(xb_ref[...], w_ref[...])


def _proj(x, w, tm, tn):
    T, D = x.shape
    N = w.shape[1]
    return pl.pallas_call(
        _proj_kernel,
        grid=(T // tm, N // tn),
        in_specs=[pl.BlockSpec((tm, D), lambda i, j: (i, 0)),
                  pl.BlockSpec((D, tn), lambda i, j: (0, j))],
        out_specs=pl.BlockSpec((tm, tn), lambda i, j: (i, j)),
        out_shape=jax.ShapeDtypeStruct((T, N), F32),
        scratch_shapes=[pltpu.VMEM((tm, D), BF16)],
        compiler_params=_params(("parallel", "arbitrary")),
        name="in_proj",
    )(x, w)


def _kvproj_kernel(x_ref, w_ref, kc_ref, vc_ref, ks_ref, vs_ref, kw_ref, vw_ref, sm_ref):
    tm = x_ref.shape[0]
    y = _dot(x_ref[...].astype(BF16), w_ref[...])
    piece = lambda n: y[:, n * LANES:(n + 1) * LANES]
    kc_ref[...] = piece(0)
    vc_ref[...] = piece(1)
    row = pl.program_id(0) * tm + lax.broadcasted_iota(jnp.int32, (tm, LANES), 0)
    lane = lax.broadcasted_iota(jnp.int32, (tm, LANES), 1)
    onehot = ((row // NSA_SLC_BLOCK) % (TK // NSA_SLC_BLOCK) == lane).astype(BF16)
    ks_ref[...] = jnp.concatenate([piece(2).astype(BF16), onehot], axis=1)
    vs_ref[...] = jnp.concatenate([piece(3).astype(BF16), jnp.ones((tm, LANES), BF16)], axis=1)
    kw_ref[...] = piece(4).astype(BF16)
    vw_ref[...] = piece(5).astype(BF16)
    sm_ref[...] = piece(6)


def _kvproj(x, w, tm):
    T, D = x.shape
    narrow = lambda dt, width=LANES: (pl.BlockSpec((tm, width), lambda i: (i, 0)),
                                      jax.ShapeDtypeStruct((T, width), dt))
    outs = [narrow(F32), narrow(F32), narrow(BF16, 2 * LANES), narrow(BF16, 2 * LANES),
            narrow(BF16), narrow(BF16), narrow(F32)]
    return pl.pallas_call(
        _kvproj_kernel,
        grid=(T // tm,),
        in_specs=[pl.BlockSpec((tm, D), lambda i: (i, 0)), pl.BlockSpec(w.shape, lambda i: (0, 0))],
        out_specs=[o[0] for o in outs],
        out_shape=[o[1] for o in outs],
        compiler_params=_params(("parallel",)),
        name="kv_proj",
    )(x, w)


def _outproj_kernel(x_ref, oa_ref, ob_ref, oc_ref, od_ref, wa_ref, wb_ref, wc_ref, wd_ref,
                    g_ref, b_ref, o_ref):
    mix = (_dot(oa_ref[...], wa_ref[...]) + _dot(ob_ref[...], wb_ref[...])
           + _dot(oc_ref[...], wc_ref[...]) + _dot(od_ref[...], wd_ref[...]))
    o_ref[...] = _layer_norm(ALPHA * x_ref[...] + mix, g_ref[...], b_ref[...])


def _outproj(x, oa, ob, oc, od, wa, wb, wc, wd, g, b, tm):
    T, D = x.shape
    row = lambda a: pl.BlockSpec((tm, a.shape[1]), lambda i: (i, 0))
    full = lambda a: pl.BlockSpec(a.shape, lambda i: (0, 0))
    return pl.pallas_call(
        _outproj_kernel,
        grid=(T // tm,),
        in_specs=[row(x), row(oa), row(ob), row(oc), row(od),
                  full(wa), full(wb), full(wc), full(wd), full(g), full(b)],
        out_specs=row(x),
        out_shape=jax.ShapeDtypeStruct((T, D), F32),
        compiler_params=_params(("parallel",)),
        name="out_proj_ln",
    )(x, oa, ob, oc, od, wa, wb, wc, wd, g, b)


def _hgrn_tables(C):
    i = np.arange(C)[:, None]
    ip = np.arange(C)[None, :]
    seg = [(ip <= i),
           (ip > i)]
    masks = [np.eye(C, dtype=bool)]
    s = C // 2
    while s >= 1:
        blk = i // s
        seg.append(np.where(blk % 2 == 1, (ip > blk * s) & (ip <= i), (ip > i) & (ip <= (blk + 1) * s)))
        masks.append((blk % 2 == 1) & (ip // s == blk - 1))
        s //= 2
    seg = np.concatenate([x.astype(np.float32) for x in seg], axis=0)
    return seg, np.stack([m.astype(np.float32) for m in masks])


def _hgrn_kernel(q_ref, f_ref, i_ref, g_ref, llb_ref, l1m_ref, oml_ref, ng_ref,
                 seg_ref, msk_ref, o_ref, st_ref):
    @pl.when(pl.program_id(0) == 0)
    def _():
        st_ref[...] = jnp.zeros_like(st_ref)

    C = q_ref.shape[1]
    nlev = msk_ref.shape[0] - 1
    for bi in range(q_ref.shape[0]):
        q = _silu(q_ref[bi])
        z = f_ref[bi]
        log_sig = jnp.minimum(z, 0.0) - jnp.log1p(jnp.exp(-jnp.abs(z)))
        cc = l1m_ref[...] + log_sig
        llb = llb_ref[...]
        logf = jnp.maximum(llb, cc) + jnp.log1p(jnp.exp(-jnp.abs(llb - cc)))
        k = oml_ref[...] * _sigmoid(-z)
        v = i_ref[bi]
        seg = _exact_left_dot(seg_ref[...], logf)
        outs = []
        for h in range(HG_HEADS):
            sl = slice(h * LANES, (h + 1) * LANES)
            qh, kh, vh = q[:, sl], k[:, sl], v[:, sl]
            a = msk_ref[0] * _dot_nt(qh.astype(BF16), kh.astype(BF16))
            for l in range(nlev):
                dec = jnp.exp(seg[(2 + l) * C:(3 + l) * C, sl])
                a = a + msk_ref[1 + l] * _dot_nt((qh * dec).astype(BF16), (kh * dec).astype(BF16))
            b = seg[0:C, sl]
            st = st_ref[bi, h]
            o = _dot(a.astype(BF16), vh.astype(BF16))
            o = o + _dot_nt((qh * jnp.exp(b)).astype(BF16), st.astype(BF16))
            kd = (kh * jnp.exp(seg[C:2 * C, sl])).astype(BF16)
            st_ref[bi, h] = st * jnp.exp(b[C - 1:C, :]) + _dot(vh.T.astype(BF16), kd)
            outs.append(o * lax.rsqrt(jnp.mean(o * o, axis=-1, keepdims=True) + 1e-6))
        o = jnp.concatenate(outs, axis=1)
        o_ref[bi] = (o * ng_ref[...] * _silu(g_ref[bi])).astype(o_ref.dtype)


def _hgrn(proj, B, S, llb, l1m, oml, ng):
    C = CHUNK
    nc = S // C
    seg, msk = _hgrn_tables(C)
    seg, msk = jnp.asarray(seg, BF16), jnp.asarray(msk)
    proj3 = proj.reshape(B, S, proj.shape[1])
    col = lambda name: pl.BlockSpec((B, C, GROUP_W), lambda c, n=COL[name] // 4: (0, c, n))
    vec = pl.BlockSpec((1, GROUP_W), lambda c: (0, 0))
    full2 = lambda a: pl.BlockSpec(a.shape, lambda c: (0, 0))
    return pl.pallas_call(
        _hgrn_kernel,
        grid=(nc,),
        in_specs=[col("hq"), col("hf"), col("hi"), col("hg"), vec, vec, vec, vec,
                  full2(seg), pl.BlockSpec(msk.shape, lambda c: (0, 0, 0))],
        out_specs=pl.BlockSpec((B, C, GROUP_W), lambda c: (0, c, 0)),
        out_shape=jax.ShapeDtypeStruct((B, S, GROUP_W), BF16),
        scratch_shapes=[pltpu.VMEM((B, HG_HEADS, LANES, LANES), F32)],
        compiler_params=_params(("arbitrary",)),
        name="hgrn2",
    )(proj3, proj3, proj3, proj3, llb, l1m, oml, ng, seg, msk).reshape(B * S, GROUP_W)


def _ssd_kernel(z_ref, xbc_ref, sm_ref, cw_ref, cb_ref, dtb_ref, aneg_ref, dsk_ref, ng_ref,
                ex_ref, o_ref, tail_ref, st_ref):
    @pl.when(pl.program_id(1) == 0)
    def _():
        tail_ref[...] = jnp.zeros_like(tail_ref)
        st_ref[...] = jnp.zeros_like(st_ref)

    L = xbc_ref.shape[0]
    x = xbc_ref[...]
    xe = jnp.concatenate([tail_ref[...], x], axis=0)
    cw = cw_ref[...]
    conv = cb_ref[...]
    for kk in range(SSM_CONV):
        conv = conv + cw[kk:kk + 1, :] * xe[5 + kk:5 + kk + L, :]
    tail_ref[...] = x[L - 8:L, :]
    conv = _silu(conv)
    xs = conv[:, 0:GROUP_W]
    bm = conv[:, GROUP_W:GROUP_W + 256]
    cm = conv[:, GROUP_W + 256:GROUP_W + 512]

    dtf = _softplus(sm_ref[...] + dtb_ref[...])
    la = dtf * aneg_ref[...]
    ri = lax.broadcasted_iota(jnp.int32, (L, L), 0)
    ci = lax.broadcasted_iota(jnp.int32, (L, L), 1)
    tri = ri >= ci
    bfull = _exact_left_dot(tri.astype(BF16), la)
    ex = ex_ref[...]
    bexp = _exact_right_dot(bfull, ex)
    dtexp = _exact_right_dot(dtf, ex)
    b_t = bfull.T
    xdt = xs * dtexp
    lane = lax.broadcasted_iota(jnp.int32, (L, LANES), 1)

    scores = []
    for g in range(SSM_GROUPS):
        cg = cm[:, g * SSM_STATE:(g + 1) * SSM_STATE].astype(BF16)
        bg = bm[:, g * SSM_STATE:(g + 1) * SSM_STATE].astype(BF16)
        cb = _dot_nt(cg, bg)
        for hh in range(SSM_HEADS // SSM_GROUPS):
            h = g * (SSM_HEADS // SSM_GROUPS) + hh
            bcol = bfull[:, DT_LANE0 + h:DT_LANE0 + h + 1]
            brow = b_t[DT_LANE0 + h:DT_LANE0 + h + 1, :]
            dec = jnp.exp(jnp.where(tri, bcol - brow, NEG))
            scores.append((cb * dec).astype(BF16))
    y_pairs = []
    for u in range(SSM_HEADS // 2):
        slab = xdt[:, u * LANES:(u + 1) * LANES]
        lo = jnp.where(lane < SSM_HEAD_DIM, slab, 0.0).astype(BF16)
        hi = jnp.where(lane >= SSM_HEAD_DIM, slab, 0.0).astype(BF16)
        y_pairs.append(_dot(scores[2 * u], lo) + _dot(scores[2 * u + 1], hi))
    y_intra = jnp.concatenate(y_pairs, axis=1)

    blast = bexp[L - 1:L, :]
    w = (xdt * jnp.exp(blast - bexp)).astype(BF16)
    y_inter = []
    for g in range(SSM_GROUPS):
        gs = slice(g * 256, (g + 1) * 256)
        cg = cm[:, g * SSM_STATE:(g + 1) * SSM_STATE].astype(BF16)
        st = st_ref[g]
        y_inter.append(_dot(cg, st.astype(BF16)))
        bg_t = bm[:, g * SSM_STATE:(g + 1) * SSM_STATE].T.astype(BF16)
        st_ref[g] = st * jnp.exp(blast[:, gs]) + _dot(bg_t, w[:, gs])
    y = y_intra + jnp.concatenate(y_inter, axis=1) * jnp.exp(bexp) + dsk_ref[...] * xs
    y = y * _silu(z_ref[...])
    halves = []
    for g in range(SSM_GROUPS):
        seg = y[:, g * 256:(g + 1) * 256]
        halves.append(seg * lax.rsqrt(jnp.mean(seg * seg, axis=-1, keepdims=True) + 1e-6))
    o_ref[...] = (jnp.concatenate(halves, axis=1) * ng_ref[...]).astype(o_ref.dtype)


def _ssd(proj, small, B, S, cw, cb, dtb, aneg, dsk, ng):
    L = CHUNK
    nc = S // L
    ex = np.zeros((LANES, GROUP_W), np.float32)
    for h in range(SSM_HEADS):
        ex[DT_LANE0 + h, h * SSM_HEAD_DIM:(h + 1) * SSM_HEAD_DIM] = 1.0
    ex = jnp.asarray(ex, BF16)
    full2 = lambda a: pl.BlockSpec(a.shape, lambda b, c: (0, 0))
    return pl.pallas_call(
        _ssd_kernel,
        grid=(B, nc),
        in_specs=[
            pl.BlockSpec((L, GROUP_W), lambda b, c: (b * nc + c, COL["sz"] // 4)),
            pl.BlockSpec((L, 1024), lambda b, c: (b * nc + c, COL["sxbc"] // 8)),
            pl.BlockSpec((L, LANES), lambda b, c: (b * nc + c, 0)),
            full2(cw), full2(cb), full2(dtb), full2(aneg), full2(dsk), full2(ng), full2(ex)],
        out_specs=pl.BlockSpec((L, GROUP_W), lambda b, c: (b * nc + c, 0)),
        out_shape=jax.ShapeDtypeStruct((B * S, GROUP_W), BF16),
        scratch_shapes=[pltpu.VMEM((8, 1024), F32), pltpu.VMEM((SSM_GROUPS, SSM_STATE, 256), F32)],
        compiler_params=_params(("parallel", "arbitrary")),
        name="ssd",
    )(proj, proj, small, cw, cb, dtb, aneg, dsk, ng, ex)


def _ret_kernel(q_ref, k_ref, v_ref, g_ref, cos_ref, sin_ref, dec_ref, qs_ref, ks_ref, sd_ref,
                o_ref, st_ref):
    @pl.when(pl.program_id(0) == 0)
    def _():
        st_ref[...] = jnp.zeros_like(st_ref)

    cos = cos_ref[...]
    sin = sin_ref[...]
    for b in range(q_ref.shape[0]):
        outs = []
        for h in range(RET_HEADS):
            sl = slice(h * LANES, (h + 1) * LANES)
            qh = q_ref[b, :, sl]
            kh = k_ref[b, :, sl]
            qh = qh * cos + pltpu.roll(qh, RET_DK // 2, axis=1) * sin
            kh = (kh * cos + pltpu.roll(kh, RET_DK // 2, axis=1) * sin) * (RET_DK ** -0.5)
            vh = v_ref[b, :, sl].astype(BF16)
            sc = (_dot_nt(qh.astype(BF16), kh.astype(BF16)) * dec_ref[h]).astype(BF16)
            st = st_ref[b, h]
            y = _dot(sc, vh) + _dot((qh * qs_ref[:, sl]).astype(BF16), st.astype(BF16))
            kd_t = (kh * ks_ref[:, sl]).T.astype(BF16)
            st_ref[b, h] = st * sd_ref[h] + _dot(kd_t, vh)
            mu = jnp.mean(y, axis=-1, keepdims=True)
            d = y - mu
            outs.append(d * lax.rsqrt(jnp.mean(d * d, axis=-1, keepdims=True) + 1e-5))
        o_ref[b] = (_silu(g_ref[b]) * jnp.concatenate(outs, axis=1)).astype(o_ref.dtype)


def _retention(proj, B, S, cos_t, sin_t):
    L = CHUNK
    nc = S // L
    lg = jnp.log(1.0 - 2.0 ** (-5.0 - jnp.arange(RET_HEADS, dtype=F32)))
    i = jnp.arange(L, dtype=F32)
    diff = i[:, None] - i[None, :]
    dec = jnp.where(diff >= 0, jnp.exp(lg[:, None, None] * jnp.maximum(diff, 0.0)), 0.0)
    rep = lambda t: jnp.repeat(t, LANES, axis=1)
    qs = rep(jnp.exp((i[:, None] + 1.0) * lg[None, :]))
    ks = rep(jnp.exp((L - 1.0 - i[:, None]) * lg[None, :]))
    sd = jnp.broadcast_to(jnp.exp(L * lg)[:, None, None], (RET_HEADS, LANES, LANES))
    proj3 = proj.reshape(B, S, proj.shape[1])
    col = lambda name: pl.BlockSpec((B, L, GROUP_W), lambda c, n=COL[name] // 4: (0, c, n))
    return pl.pallas_call(
        _ret_kernel,
        grid=(nc,),
        in_specs=[col("rq"), col("rk"), col("rv"), col("rg"),
                  pl.BlockSpec((L, LANES), lambda c: (c, 0)),
                  pl.BlockSpec((L, LANES), lambda c: (c, 0)),
                  pl.BlockSpec((RET_HEADS, L, L), lambda c: (0, 0, 0)),
                  pl.BlockSpec((L, GROUP_W), lambda c: (0, 0)),
                  pl.BlockSpec((L, GROUP_W), lambda c: (0, 0)),
                  pl.BlockSpec((RET_HEADS, LANES, LANES), lambda c: (0, 0, 0))],
        out_specs=pl.BlockSpec((B, L, GROUP_W), lambda c: (0, c, 0)),
        out_shape=jax.ShapeDtypeStruct((B, S, GROUP_W), BF16),
        scratch_shapes=[pltpu.VMEM((B, RET_HEADS, RET_DK, RET_DK), F32)],
        compiler_params=_params(("arbitrary",)),
        name="retention",
    )(proj3, proj3, proj3, proj3, cos_t, sin_t, dec, qs, ks, sd).reshape(B * S, GROUP_W)


def _t5_bucket(dist):
    n = jnp.maximum(dist, 0)
    nf = jnp.maximum(n, 1).astype(F32)
    large = REL_EXACT + (jnp.log(nf / REL_EXACT) / math.log(REL_MAX_DIST / REL_EXACT)
                         * (REL_BUCKETS - REL_EXACT)).astype(jnp.int32)
    return jnp.where(n < REL_EXACT, n, jnp.minimum(large, REL_BUCKETS - 1))


def _head_bias(bucket, rel_ref):
    rows, cols = bucket.shape
    per_head = []
    for h in range(NSA_HEADS):
        tbl = jnp.broadcast_to(rel_ref[h:h + 1, :], (rows, LANES))
        chunks = [jnp.take_along_axis(tbl, bucket[:, c:c + LANES], axis=1)
                  for c in range(0, cols, LANES)]
        per_head.append(chunks[0] if len(chunks) == 1 else jnp.concatenate(chunks, axis=1))
    return jnp.stack(per_head, axis=0)


def _stack_heads(qw):
    return jnp.concatenate([qw[:, h * LANES:(h + 1) * LANES] for h in range(NSA_HEADS)],
                           axis=0).astype(BF16)


def _cmp_kernel(g_ref, pe_ref, w1a_ref, w1b_ref, w2_ref, o_ref):
    nb = g_ref.shape[0]
    gw = g_ref.shape[1] // 4
    pe = pe_ref[...]
    slabs = [g_ref[:, s * gw:(s + 1) * gw] for s in range(4)]
    nxt0 = pltpu.roll(slabs[0], nb - 1, axis=0)
    for s in range(4):
        a = (slabs[s] + pe[0:1, :]).astype(BF16)
        bn = ((slabs[s + 1] if s < 3 else nxt0) + pe[1:2, :]).astype(BF16)
        hid = _silu(_dot(a, w1a_ref[...]) + _dot(bn, w1b_ref[...]))
        o_ref[s * nb:(s + 1) * nb, :] = _dot(hid.astype(BF16), w2_ref[...]).astype(o_ref.dtype)


def _compress(g, pe2, w1a, w1b, w2bd):
    B, nb, gw4 = g.shape
    full2 = lambda a: pl.BlockSpec(a.shape, lambda b: (0, 0))
    slab_major = pl.pallas_call(
        _cmp_kernel,
        grid=(B,),
        in_specs=[pl.BlockSpec((None, nb, gw4), lambda b: (b, 0, 0)),
                  full2(pe2), full2(w1a), full2(w1b), full2(w2bd)],
        out_specs=pl.BlockSpec((None, 4 * nb, LANES), lambda b: (b, 0, 0)),
        out_shape=jax.ShapeDtypeStruct((B, 4 * nb, LANES), BF16),
        compiler_params=_params(("parallel",)),
        name="nsa_compress",
    )(g, pe2, w1a, w1b, w2bd)
    return slab_major.reshape(B, 4, nb, LANES).transpose(0, 2, 1, 3).reshape(B, 4 * nb, LANES)


def _cmpattn_kernel(q_ref, kc_ref, vc_ref, band_ref, st_ref, oc_ref, selb_ref):
    tq = q_ref.shape[0]
    ncmp = kc_ref.shape[0]
    nb = ncmp // 4
    R = NSA_HEADS * tq
    q0 = pl.program_id(1) * tq
    nd = band_ref.shape[0]
    Q = _stack_heads(q_ref[...])
    s3 = _dot_nt(Q, kc_ref[...]).reshape(NSA_HEADS, tq, ncmp)
    parts = []
    for ch in range(ncmp // LANES):
        d = jnp.clip((q0 - ch * LANES * NSA_CMP_STRIDE) // tq + CMP_BAND_OFF, 0, nd - 1)
        parts.append(s3[:, :, ch * LANES:(ch + 1) * LANES] + band_ref[d])
    s3 = jnp.concatenate(parts, axis=2)
    mx = jnp.max(s3, axis=-1, keepdims=True)
    e = jnp.exp(s3 - mx)
    live = (q0 + lax.broadcasted_iota(jnp.int32, (tq, 1), 0) >= NSA_CMP_BLOCK - 1)[None]
    p = e * jnp.where(live, 1.0 / jnp.sum(e, axis=-1, keepdims=True), 0.0)
    oc_ref[...] = _dot(p.reshape(R, ncmp).astype(BF16), vc_ref[...])

    ps = p.reshape(NSA_KV, NSA_GROUP, tq, ncmp).sum(axis=1).reshape(NSA_KV * tq, ncmp)
    imp = _exact_right_dot(ps, st_ref[...])
    j = lax.broadcasted_iota(jnp.int32, (NSA_KV * tq, nb), 1)
    t = q0 + (lax.broadcasted_iota(jnp.int32, (NSA_KV * tq, nb), 0) % tq)
    cur = t // NSA_SLC_BLOCK
    forced = (j == 0) | (j == cur) | (j == cur - 1)
    score = jnp.where(j > cur, -1.0, jnp.where(forced, NSA_GROUP + 1.0, imp))
    score = score.T
    jf = lax.broadcasted_iota(jnp.int32, score.shape, 0).astype(F32)
    sel = score == NSA_GROUP + 1.0
    score = jnp.where(sel, -jnp.inf, score)
    for _ in range(min(NSA_TOP_N, nb) - 3):
        best = jnp.max(score, axis=0, keepdims=True)
        first = jnp.min(jnp.where(score == best, jf, float(nb)), axis=0, keepdims=True)
        hit = jf == first
        sel = sel | hit
        score = jnp.where(hit, -jnp.inf, score)
    selb = jnp.where(sel, 0.0, NEG).T.astype(selb_ref.dtype)
    selb_ref[:, 0:nb] = selb[0:tq]
    selb_ref[:, nb:2 * nb] = selb[tq:2 * tq]


def _cmpattn(proj, kcmp, vcmp, cband, B, S):
    tq = TQ
    nqt = S // tq
    ncmp = kcmp.shape[1]
    nb = ncmp // 4
    off = np.arange(ncmp)[:, None] - 4 * np.arange(nb)[None, :]
    stencil = np.where((off >= 0) & (off <= 2), 1.0, np.where((off == -1) | (off == 3), 0.5, 0.0))
    stencil = jnp.asarray(stencil, BF16)
    once = pl.Buffered(1)
    return pl.pallas_call(
        _cmpattn_kernel,
        grid=(B, nqt),
        in_specs=[pl.BlockSpec((tq, 1024), lambda b, i: (b * nqt + i, COL["nqw"] // 8)),
                  pl.BlockSpec((None, ncmp, LANES), lambda b, i: (b, 0, 0)),
                  pl.BlockSpec((None, ncmp, LANES), lambda b, i: (b, 0, 0)),
                  pl.BlockSpec(cband.shape, lambda b, i: (0, 0, 0, 0), pipeline_mode=once),
                  pl.BlockSpec(stencil.shape, lambda b, i: (0, 0))],
        out_specs=[pl.BlockSpec((NSA_HEADS * tq, LANES), lambda b, i: (b * nqt + i, 0)),
                   pl.BlockSpec((tq, 2 * nb), lambda b, i: (b * nqt + i, 0))],
        out_shape=[jax.ShapeDtypeStruct((B * S * NSA_HEADS, LANES), F32),
                   jax.ShapeDtypeStruct((B * S, 2 * nb), BF16)],
        compiler_params=_params(("parallel", "arbitrary")),
        name="nsa_cmp_attn_topk",
    )(proj, kcmp, vcmp, cband, stencil)


def _band_kernel(rel_ref, o_ref, *, entry_off, key_step, key_end):
    tq = o_ref.shape[1]
    delta = (pl.program_id(0) - entry_off) * tq
    row = lax.broadcasted_iota(jnp.int32, (tq, LANES), 0)
    col = lax.broadcasted_iota(jnp.int32, (tq, LANES), 1)
    dist = delta + row - (col * key_step + key_end)
    bias = _head_bias(_t5_bucket(dist), rel_ref)
    o_ref[...] = jnp.where((dist >= 0)[None], bias, NEG)


def _band_table(rel_t, tq, entry_off, key_step=1, key_end=0):
    nd = -(-(REL_MAX_DIST + key_step * (LANES - 1) + key_end) // tq) + entry_off + 1
    return pl.pallas_call(
        functools.partial(_band_kernel, entry_off=entry_off, key_step=key_step, key_end=key_end),
        grid=(nd,),
        in_specs=[pl.BlockSpec(rel_t.shape, lambda d: (0, 0))],
        out_specs=pl.BlockSpec((None, NSA_HEADS, tq, LANES), lambda d: (d, 0, 0, 0)),
        out_shape=jax.ShapeDtypeStruct((nd, NSA_HEADS, tq, LANES), F32),
        compiler_params=_params(("parallel",)),
        name="nsa_bias_band",
    )(rel_t)


def _selattn_kernel(q_ref, selb_ref, ks_ref, vs_ref, band_ref, sp_ref, os_ref,
                    qaug_ref, msel_ref, s_ref, m_ref, acc_ref):
    tq = q_ref.shape[0]
    R = NSA_HEADS * tq
    nb = selb_ref.shape[1] // 2
    n_kt = msel_ref.shape[0]
    tk = ks_ref.shape[0] // n_kt
    nd = band_ref.shape[0]
    q0 = pl.program_id(1) * tq
    qaug_ref[:, 0:LANES] = _stack_heads(q_ref[...])
    selb2 = jnp.concatenate([selb_ref[:, 0:nb], selb_ref[:, nb:2 * nb]], axis=0)
    spread = _dot(selb2, sp_ref[...])
    for c in range(n_kt):
        msel_ref[c] = spread[:, c * LANES:(c + 1) * LANES].astype(BF16)
    m_ref[...] = jnp.full(m_ref.shape, NEG, F32)
    acc_ref[...] = jnp.zeros_like(acc_ref)
    n_tiles = (q0 + tq - 1) // tk + 1

    def scores(c, slot):
        c = jnp.minimum(c, n_kt - 1)
        m2 = msel_ref[c]
        qaug_ref[:, LANES:2 * LANES] = jnp.concatenate(
            [m2[0:tq]] * NSA_GROUP + [m2[tq:2 * tq]] * NSA_GROUP, axis=0)
        s_ref[slot] = _dot_nt(qaug_ref[...], ks_ref[pl.ds(pl.multiple_of(c * tk, tk), tk), :])

    def accumulate(c, slot):
        k0 = pl.multiple_of(c * tk, tk)
        s3 = s_ref[slot].reshape(NSA_HEADS, tq, tk)
        parts = []
        for ch in range(tk // LANES):
            d = jnp.clip((q0 - k0) // tq - ch * (LANES // tq) + BAND_OFF, 0, nd - 1)
            parts.append(s3[:, :, ch * LANES:(ch + 1) * LANES] + band_ref[d])
        s = jnp.concatenate(parts, axis=2).reshape(R, tk)
        m_old = m_ref[...]
        m_new = jnp.maximum(m_old, jnp.max(s, axis=-1, keepdims=True))
        alpha = jnp.exp(m_old - m_new)
        p = jnp.exp(s - jnp.tile(m_new, (1, tk // LANES)))
        acc_ref[...] = jnp.tile(alpha, (1, 2)) * acc_ref[...] + _dot(p.astype(BF16), vs_ref[pl.ds(k0, tk), :])
        m_ref[...] = m_new

    def pair(i, carry):
        c = 2 * i
        scores(c + 1, 1)
        accumulate(c, 0)
        scores(c + 2, 0)
        accumulate(c + 1, 1)
        return carry

    scores(0, 0)
    lax.fori_loop(0, n_tiles // 2, pair, 0)

    @pl.when(n_tiles % 2 == 1)
    def _():
        accumulate(n_tiles - 1, 0)

    acc = acc_ref[...]
    os_ref[...] = acc[:, 0:LANES] / acc[:, LANES:2 * LANES]


def _selattn(proj, selb, ks, vs, band, B, S):
    tq = TQ
    tk = min(TK, S)
    n_kt = S // tk
    bpt = tk // NSA_SLC_BLOCK
    nqt = S // tq
    nb = S // NSA_SLC_BLOCK
    R = NSA_HEADS * tq
    sp = np.zeros((nb, n_kt * LANES), np.float32)
    sp[np.arange(nb), (np.arange(nb) // bpt) * LANES + np.arange(nb) % bpt] = 1.0
    once = pl.Buffered(1)
    return pl.pallas_call(
        _selattn_kernel,
        grid=(B, nqt),
        in_specs=[pl.BlockSpec((tq, 1024), lambda b, i: (b * nqt + i, COL["nqw"] // 8)),
                  pl.BlockSpec((tq, 2 * nb), lambda b, i: (b * nqt + i, 0)),
                  pl.BlockSpec((None, S, 2 * LANES), lambda b, i: (b, 0, 0), pipeline_mode=once),
                  pl.BlockSpec((None, S, 2 * LANES), lambda b, i: (b, 0, 0), pipeline_mode=once),
                  pl.BlockSpec(band.shape, lambda b, i: (0, 0, 0, 0), pipeline_mode=once),
                  pl.BlockSpec(sp.shape, lambda b, i: (0, 0))],
        out_specs=pl.BlockSpec((R, LANES), lambda b, i: (b * nqt + i, 0)),
        out_shape=jax.ShapeDtypeStruct((B * S * NSA_HEADS, LANES), F32),
        scratch_shapes=[pltpu.VMEM((R, 2 * LANES), BF16), pltpu.VMEM((n_kt, 2 * tq, LANES), BF16),
                        pltpu.VMEM((2, R, tk), F32), pltpu.VMEM((R, LANES), F32),
                        pltpu.VMEM((R, 2 * LANES), F32)],
        compiler_params=_params(("parallel", "arbitrary")),
        name="nsa_sel_attn",
    )(proj, selb, ks, vs, band, jnp.asarray(sp, BF16))


def _winattn_kernel(q_ref, sm_ref, oc_ref, os_ref, kw_ref, vw_ref, band_ref, ng_ref, o_ref):
    tq = q_ref.shape[0]
    R = NSA_HEADS * tq
    S = kw_ref.shape[0]
    span = min(WIN_SPAN, S)
    nd = band_ref.shape[0]
    q0 = pl.program_id(1) * tq
    start = pl.multiple_of(jnp.clip(q0 + tq - span, 0, S - span), tq)
    Q = _stack_heads(q_ref[...])
    kt = kw_ref[pl.ds(start, span), :]
    vt = vw_ref[pl.ds(start, span), :]
    s3 = _dot_nt(Q, kt).reshape(NSA_HEADS, tq, span)
    parts = []
    for ch in range(span // LANES):
        d = jnp.clip((q0 - start) // tq - ch * (LANES // tq) + BAND_OFF, 0, nd - 1)
        parts.append(s3[:, :, ch * LANES:(ch + 1) * LANES] + band_ref[d])
    row = lax.broadcasted_iota(jnp.int32, (tq, span), 0)
    col = lax.broadcasted_iota(jnp.int32, (tq, span), 1)
    in_window = ((q0 + row) - (start + col) < NSA_WINDOW)[None]
    s3 = jnp.where(in_window, jnp.concatenate(parts, axis=2), NEG)
    mx = jnp.max(s3, axis=-1, keepdims=True)
    e = jnp.exp(s3 - mx)
    p = e * (1.0 / jnp.sum(e, axis=-1, keepdims=True))
    ow = _dot(p.reshape(R, span).astype(BF16), vt)

    gates = _sigmoid(sm_ref[...])
    lane = lax.broadcasted_iota(jnp.int32, (tq, LANES), 1)
    heads = []
    ssq = jnp.zeros((tq, 1), F32)
    for h in range(NSA_HEADS):
        rs = slice(h * tq, (h + 1) * tq)
        g = [gates[:, GATE_LANE0 + 3 * h + br:GATE_LANE0 + 3 * h + br + 1] for br in range(3)]
        oh = g[0] * oc_ref[rs, :] + g[1] * os_ref[rs, :] + g[2] * ow[rs, :]
        kv = h // NSA_GROUP
        valid = (lane >= kv * NSA_HEAD_DIM) & (lane < (kv + 1) * NSA_HEAD_DIM)
        oh = jnp.where(valid, oh, 0.0)
        ssq = ssq + jnp.sum(oh * oh, axis=-1, keepdims=True)
        heads.append(oh)
    rinv = lax.rsqrt(ssq / GROUP_W + 1e-6)
    o_ref[...] = (jnp.concatenate(heads, axis=1) * rinv * ng_ref[...]).astype(o_ref.dtype)


def _winattn(proj, small, oc, os_, kw, vw, band, ngw, B, S):
    tq = TQ
    nqt = S // tq
    R = NSA_HEADS * tq
    once = pl.Buffered(1)
    return pl.pallas_call(
        _winattn_kernel,
        grid=(B, nqt),
        in_specs=[pl.BlockSpec((tq, 1024), lambda b, i: (b * nqt + i, COL["nqw"] // 8)),
                  pl.BlockSpec((tq, LANES), lambda b, i: (b * nqt + i, 0)),
                  pl.BlockSpec((R, LANES), lambda b, i: (b * nqt + i, 0)),
                  pl.BlockSpec((R, LANES), lambda b, i: (b * nqt + i, 0)),
                  pl.BlockSpec((None, S, LANES), lambda b, i: (b, 0, 0), pipeline_mode=once),
                  pl.BlockSpec((None, S, LANES), lambda b, i: (b, 0, 0), pipeline_mode=once),
                  pl.BlockSpec(band.shape, lambda b, i: (0, 0, 0, 0), pipeline_mode=once),
                  pl.BlockSpec(ngw.shape, lambda b, i: (0, 0))],
        out_specs=pl.BlockSpec((tq, NSA_HEADS * LANES), lambda b, i: (b * nqt + i, 0)),
        out_shape=jax.ShapeDtypeStruct((B * S, NSA_HEADS * LANES), BF16),
        compiler_params=_params(("parallel", "arbitrary")),
        name="nsa_win_attn_merge",
    )(proj, small, oc, os_, kw, vw, band, ngw)


def _widen_heads(x, axis):
    x = jnp.moveaxis(x, axis, -1)
    lead = x.shape[:-1]
    x = x.reshape(*lead, NSA_KV, NSA_GROUP, 1, NSA_HEAD_DIM)
    sel = jnp.eye(NSA_KV, dtype=x.dtype).reshape(NSA_KV, 1, NSA_KV, 1)
    x = (x * sel).reshape(*lead, NSA_HEADS * LANES)
    return jnp.moveaxis(x, -1, axis)


def _build_w_in(w):
    (hq, hf, hi, hg, nq, nkc, nvc, nks, nvs, nkw, nvw, ngate,
     sz, sxbc, sdt, rq, rk, rv, rg) = jnp.split(w, IN_SPLITS, axis=1)
    D = w.shape[0]
    nqw = _widen_heads(nq * NSA_HEAD_DIM ** -0.5, 1)
    deint = lambda t: t.reshape(D, RET_HEADS, RET_DK // 2, 2).transpose(0, 1, 3, 2).reshape(D, GROUP_W)
    small = jnp.concatenate([ngate, sdt, jnp.zeros((D, LANES - 32), w.dtype)], axis=1)
    wide = [hq, hf, hi, hg, nqw, sxbc, sz, deint(rq), deint(rk), rv, rg]
    narrow = [nkc, nvc, nks, nvs, nkw, nvw, small]
    return jnp.concatenate(wide, axis=1).astype(BF16), jnp.concatenate(narrow, axis=1).astype(BF16)


def _build_cmp_weights(pe, w1, w2):
    w1r = w1.reshape(2, NSA_CMP_STRIDE, NSA_HEAD_DIM, NSA_CMP_HIDDEN)
    eye = jnp.eye(NSA_KV, dtype=w1.dtype)
    big = jnp.einsum("ardc,kj->arkdjc", w1r, eye).reshape(
        2, NSA_CMP_STRIDE * NSA_KV * NSA_HEAD_DIM, NSA_KV * NSA_CMP_HIDDEN)
    w2bd = jnp.einsum("cd,kj->kcjd", w2, eye).reshape(NSA_KV * NSA_CMP_HIDDEN, NSA_KV * NSA_HEAD_DIM)
    per = pe.reshape(2, NSA_CMP_STRIDE, 1, NSA_HEAD_DIM)
    pe2 = jnp.broadcast_to(per, (2, NSA_CMP_STRIDE, NSA_KV, NSA_HEAD_DIM)).reshape(2, -1)
    return pe2, big[0].astype(BF16), big[1].astype(BF16), w2bd.astype(BF16)


def _rotary_tables(S):
    half = RET_DK // 2
    theta = 1.0 / (10000.0 ** jnp.linspace(0.0, 1.0, half, dtype=F32))
    ang = jnp.arange(S, dtype=F32)[:, None] * theta[None, :]
    cos, sin = jnp.cos(ang), jnp.sin(ang)
    return jnp.concatenate([cos, cos], axis=1), jnp.concatenate([-sin, sin], axis=1)


def _mixer(x2, B, S, l, p, lower_bounds, band, cband, cos_t, sin_t):
    T = B * S
    w_wide, w_narrow = _build_w_in(p["w_in"][l])
    proj = _proj(x2, w_wide, tm=min(1024, T), tn=NCOL * LANES // 4)
    kc, vc, ks, vs, kw, vw, small = _kvproj(x2, w_narrow, tm=min(512, T))
    row = lambda v: v.reshape(1, -1).astype(F32)

    lb = lower_bounds[l].astype(F32)
    o_a = _hgrn(proj, B, S, row(jnp.log(lb)), row(jnp.log1p(-lb)), row(1.0 - lb),
                row(p["hgrn_norm_g"][l]))

    nb = S // NSA_SLC_BLOCK
    grp = lambda t: t.reshape(B, nb, 4 * NSA_CMP_STRIDE * LANES)
    kcmp = _compress(grp(kc), *_build_cmp_weights(p["nsa_pe_k"][l], p["nsa_w1_k"][l], p["nsa_w2_k"][l]))
    vcmp = _compress(grp(vc), *_build_cmp_weights(p["nsa_pe_v"][l], p["nsa_w1_v"][l], p["nsa_w2_v"][l]))
    o_cmp, selb = _cmpattn(proj, kcmp, vcmp, cband, B, S)
    seq = lambda t: t.reshape(B, S, t.shape[1])
    o_sel = _selattn(proj, selb, seq(ks), seq(vs), band, B, S)
    ngw = _widen_heads(p["nsa_norm_g"][l].astype(F32), 0).reshape(1, -1)
    o_b = _winattn(proj, small, o_cmp, o_sel, seq(kw), seq(vw), band, ngw, B, S)

    lane_vec = lambda v: jnp.zeros((1, LANES), F32).at[0, DT_LANE0:DT_LANE0 + SSM_HEADS].set(v.astype(F32))
    o_c = _ssd(proj, small, B, S, p["ssm_conv_w"][l].astype(F32), row(p["ssm_conv_b"][l]),
               lane_vec(p["ssm_dt_bias"][l]), lane_vec(-jnp.exp(p["ssm_a_log"][l].astype(F32))),
               row(jnp.repeat(p["ssm_d"][l].astype(F32), SSM_HEAD_DIM)), row(p["ssm_norm_g"][l]))

    o_d = _retention(proj, B, S, cos_t, sin_t)

    w_out = p["w_out"][l]
    wa, wb, wc, wd = (w_out[i * GROUP_W:(i + 1) * GROUP_W] for i in range(4))
    return o_a, o_b, o_c, o_d, wa.astype(BF16), _widen_heads(wb, 0).astype(BF16), wc.astype(BF16), wd.astype(BF16)


def kernel(x, ln1_g, ln1_b, ffn1_w1, ffn1_w3, ffn1_w2, ln2_g, ln2_b, w_in, w_out, hgrn_lb_logits, hgrn_norm_g, nsa_pe_k, nsa_w1_k, nsa_w2_k, nsa_pe_v, nsa_w1_v, nsa_w2_v, nsa_norm_g, rel_bias, ssm_conv_w, ssm_conv_b, ssm_dt_bias, ssm_a_log, ssm_d, ssm_norm_g, ln3_g, ln3_b, ffn2_w1, ffn2_w3, ffn2_w2):
    B, S, D = x.shape
    T = B * S
    depth = w_in.shape[0]
    p = dict(w_in=w_in, w_out=w_out, hgrn_norm_g=hgrn_norm_g, nsa_pe_k=nsa_pe_k, nsa_w1_k=nsa_w1_k,
             nsa_w2_k=nsa_w2_k, nsa_pe_v=nsa_pe_v, nsa_w1_v=nsa_w1_v, nsa_w2_v=nsa_w2_v,
             nsa_norm_g=nsa_norm_g, ssm_conv_w=ssm_conv_w, ssm_conv_b=ssm_conv_b,
             ssm_dt_bias=ssm_dt_bias, ssm_a_log=ssm_a_log, ssm_d=ssm_d, ssm_norm_g=ssm_norm_g)
    cum = jnp.cumsum(jax.nn.softmax(hgrn_lb_logits.astype(F32), axis=0), axis=0)
    lower_bounds = cum - cum[:1]
    rel_t = jnp.zeros((NSA_HEADS, LANES), F32).at[:, :REL_BUCKETS].set(rel_bias.astype(F32).T)
    band = _band_table(rel_t, TQ, BAND_OFF)
    cband = _band_table(rel_t, TQ, CMP_BAND_OFF, NSA_CMP_STRIDE, NSA_CMP_BLOCK - 1)
    cos_t, sin_t = _rotary_tables(S)
    row = lambda v: v.reshape(1, -1).astype(F32)
    tm = min(512, T)
    tf = 512 if ffn1_w1.shape[2] % 512 == 0 else ffn1_w1.shape[2]
    x2 = x.reshape(T, D).astype(F32)
    for l in range(depth):
        x2 = _ffn(x2, ffn1_w1[l].astype(BF16), ffn1_w3[l].astype(BF16), ffn1_w2[l].astype(BF16),
                  row(ln1_g[l]), row(ln1_b[l]), tm, tf)
        o_a, o_b, o_c, o_d, wa, wb, wc, wd = _mixer(x2, B, S, l, p, lower_bounds, band, cband, cos_t, sin_t)
        x2 = _outproj(x2, o_a, o_b, o_c, o_d, wa, wb, wc, wd, row(ln2_g[l]), row(ln2_b[l]), min(512, T))
        x2 = _ffn(x2, ffn2_w1[l].astype(BF16), ffn2_w3[l].astype(BF16), ffn2_w2[l].astype(BF16),
                  row(ln3_g[l]), row(ln3_b[l]), tm, tf)
    return x2.reshape(B, S, D).astype(x.dtype)
```

```python
import functools
import math

import numpy as np
import jax
import jax.numpy as jnp
from jax import lax
from jax.experimental import pallas as pl
from jax.experimental.pallas import tpu as pltpu

F32 = jnp.float32
BF16 = jnp.bfloat16

D_MODEL = 2048
DEPTH = 2
GROUP_W = 512
ALPHA = (2 * DEPTH) ** 0.25
HG_HEADS = 4
NSA_HEADS = 8
NSA_KV = 2
NSA_GROUP = 4
NSA_HEAD_DIM = 64
NSA_CMP_STRIDE = 16
NSA_CMP_BLOCK = 32
NSA_SLC_BLOCK = 64
NSA_TOP_N = 16
NSA_WINDOW = 512
NSA_CMP_HIDDEN = 256
SSM_HEADS = 8
SSM_HEAD_DIM = 64
SSM_GROUPS = 2
SSM_STATE = 128
SSM_CONV = 4
RET_HEADS = 4
RET_DK = 128
REL_BUCKETS = 32
REL_EXACT = 16
REL_MAX_DIST = 2048
IN_SIZES = ((GROUP_W,) * 4 + (GROUP_W,) + (128,) * 6 + (24,)
            + (GROUP_W, 1024, SSM_HEADS) + (GROUP_W,) * 4)
IN_SPLITS = tuple(int(v) for v in np.cumsum(IN_SIZES)[:-1])

LANES = 128
VMEM_LIMIT = 56 * 1024 * 1024

COL = dict(hq=0, hf=4, hi=8, hg=12, nqw=16, sxbc=24, sz=32, rq=36, rk=40, rv=44, rg=48)
NCOL = 52
KV_COLS = ("nkc", "nvc", "nks", "nvs", "nkw", "nvw", "small")
GATE_LANE0 = 0
DT_LANE0 = 24

CHUNK = 128
TQ = 128
TK = 1024
BAND_OFF = 2
CMP_BAND_OFF = 1
WIN_SPAN = NSA_WINDOW + 2 * TQ
NEG = -1e30


def _params(sem):
    return pltpu.CompilerParams(dimension_semantics=sem, vmem_limit_bytes=VMEM_LIMIT)


def _dot(a, b):
    return jnp.dot(a, b, preferred_element_type=F32)


def _dot_nt(a, b):
    return lax.dot_general(a, b, (((1,), (1,)), ((), ())), preferred_element_type=F32)


def _split3(x):
    hi = x.astype(BF16)
    r1 = x - hi.astype(F32)
    mid = r1.astype(BF16)
    return hi, mid, (r1 - mid.astype(F32)).astype(BF16)


def _exact_left_dot(w, x):
    n = x.shape[1]
    y = _dot(w, jnp.concatenate(_split3(x), axis=1))
    return y[:, 0:n] + y[:, n:2 * n] + y[:, 2 * n:3 * n]


def _exact_right_dot(x, w):
    n = x.shape[0]
    y = _dot(jnp.concatenate(_split3(x), axis=0), w)
    return y[0:n] + y[n:2 * n] + y[2 * n:3 * n]


def _sigmoid(x):
    return 1.0 / (1.0 + jnp.exp(-x))


def _silu(x):
    return x * _sigmoid(x)


def _softplus(x):
    return jnp.maximum(x, 0.0) + jnp.log1p(jnp.exp(-jnp.abs(x)))


def _layer_norm(r, g, b):
    mu = jnp.mean(r, axis=-1, keepdims=True)
    d = r - mu
    var = jnp.mean(d * d, axis=-1, keepdims=True)
    return d * lax.rsqrt(var + 1e-5) * g + b


def _ffn_kernel(x_ref, w1_ref, w3_ref, w2_ref, g_ref, b_ref, o_ref, acc_ref, xb_ref):
    j = pl.program_id(1)

    @pl.when(j == 0)
    def _():
        xb_ref[...] = x_ref[...].astype(BF16)
        acc_ref[...] = jnp.zeros_like(acc_ref)

    xb = xb_ref[...]
    h1 = _dot(xb, w1_ref[...])
    h3 = _dot(xb, w3_ref[...])
    a = (_silu(h1) * h3).astype(BF16)
    acc_ref[...] += _dot(a, w2_ref[...])

    @pl.when(j == pl.num_programs(1) - 1)
    def _():
        r = ALPHA * x_ref[...] + 0.5 * acc_ref[...]
        o_ref[...] = _layer_norm(r, g_ref[...], b_ref[...])


def _ffn(x, w1, w3, w2, g, b, tm, tf):
    T, D = x.shape
    F = w1.shape[1]
    return pl.pallas_call(
        _ffn_kernel,
        grid=(T // tm, F // tf),
        in_specs=[
            pl.BlockSpec((tm, D), lambda i, j: (i, 0)),
            pl.BlockSpec((D, tf), lambda i, j: (0, j)),
            pl.BlockSpec((D, tf), lambda i, j: (0, j)),
            pl.BlockSpec((tf, D), lambda i, j: (j, 0)),
            pl.BlockSpec((1, D), lambda i, j: (0, 0)),
            pl.BlockSpec((1, D), lambda i, j: (0, 0)),
        ],
        out_specs=pl.BlockSpec((tm, D), lambda i, j: (i, 0)),
        out_shape=jax.ShapeDtypeStruct((T, D), F32),
        scratch_shapes=[pltpu.VMEM((tm, D), F32), pltpu.VMEM((tm, D), BF16)],
        compiler_params=_params(("parallel", "arbitrary")),
        name="ffn_ln",
    )(x, w1, w3, w2, g, b)


def _proj_kernel(x_ref, w_ref, o_ref, xb_ref):
    @pl.when(pl.program_id(1) == 0)
    def _():
        xb_ref[...] = x_ref[...].astype(BF16)

    o_ref[...] = _dot(xb_ref[...], w_ref[...])


def _proj(x, w, tm, tn):
    T, D = x.shape
    N = w.shape[1]
    return pl.pallas_call(
        _proj_kernel,
        grid=(T // tm, N // tn),
        in_specs=[pl.BlockSpec((tm, D), lambda i, j: (i, 0)),
                  pl.BlockSpec((D, tn), lambda i, j: (0, j))],
        out_specs=pl.BlockSpec((tm, tn), lambda i, j: (i, j)),
        out_shape=jax.ShapeDtypeStruct((T, N), F32),
        scratch_shapes=[pltpu.VMEM((tm, D), BF16)],
        compiler_params=_params(("parallel", "arbitrary")),
        name="in_proj",
    )(x, w)


def _kvproj_kernel(x_ref, w_ref, kc_ref, vc_ref, ks_ref, vs_ref, kw_ref, vw_ref, sm_ref):
    tm = x_ref.shape[0]
    y = _dot(x_ref[...].astype(BF16), w_ref[...])
    piece = lambda n: y[:, n * LANES:(n + 1) * LANES]
    kc_ref[...] = piece(0)
    vc_ref[...] = piece(1)
    row = pl.program_id(0) * tm + lax.broadcasted_iota(jnp.int32, (tm, LANES), 0)
    lane = lax.broadcasted_iota(jnp.int32, (tm, LANES), 1)
    bpt = TK // NSA_SLC_BLOCK
    onehot = (((row // NSA_SLC_BLOCK) % bpt == lane) | ((lane >= bpt) & (lane < bpt + 3))).astype(BF16)
    ks_ref[...] = jnp.concatenate([piece(2).astype(BF16), onehot], axis=1)
    vs_ref[...] = jnp.concatenate([piece(3).astype(BF16), jnp.ones((tm, LANES), BF16)], axis=1)
    kw_ref[...] = piece(4).astype(BF16)
    vw_ref[...] = piece(5).astype(BF16)
    sm_ref[...] = piece(6)


def _kvproj(x, w, tm):
    T, D = x.shape
    narrow = lambda dt, width=LANES: (pl.BlockSpec((tm, width), lambda i: (i, 0)),
                                      jax.ShapeDtypeStruct((T, width), dt))
    outs = [narrow(F32), narrow(F32), narrow(BF16, 2 * LANES), narrow(BF16, 2 * LANES),
            narrow(BF16), narrow(BF16), narrow(F32)]
    return pl.pallas_call(
        _kvproj_kernel,
        grid=(T // tm,),
        in_specs=[pl.BlockSpec((tm, D), lambda i: (i, 0)), pl.BlockSpec(w.shape, lambda i: (0, 0))],
        out_specs=[o[0] for o in outs],
        out_shape=[o[1] for o in outs],
        compiler_params=_params(("parallel",)),
        name="kv_proj",
    )(x, w)


def _outproj_kernel(x_ref, oa_ref, ob_ref, oc_ref, od_ref, wa_ref, wb_ref, wc_ref, wd_ref,
                    g_ref, b_ref, o_ref):
    mix = (_dot(oa_ref[...], wa_ref[...]) + _dot(ob_ref[...], wb_ref[...])
           + _dot(oc_ref[...], wc_ref[...]) + _dot(od_ref[...], wd_ref[...]))
    o_ref[...] = _layer_norm(ALPHA * x_ref[...] + mix, g_ref[...], b_ref[...])


def _outproj(x, oa, ob, oc, od, wa, wb, wc, wd, g, b, tm):
    T, D = x.shape
    row = lambda a: pl.BlockSpec((tm, a.shape[1]), lambda i: (i, 0))
    full = lambda a: pl.BlockSpec(a.shape, lambda i: (0, 0))
    return pl.pallas_call(
        _outproj_kernel,
        grid=(T // tm,),
        in_specs=[row(x), row(oa), row(ob), row(oc), row(od),
                  full(wa), full(wb), full(wc), full(wd), full(g), full(b)],
        out_specs=row(x),
        out_shape=jax.ShapeDtypeStruct((T, D), F32),
        compiler_params=_params(("parallel",)),
        name="out_proj_ln",
    )(x, oa, ob, oc, od, wa, wb, wc, wd, g, b)


def _hgrn_tables(C):
    i = np.arange(C)[:, None]
    ip = np.arange(C)[None, :]
    seg = [(ip <= i),
           (ip > i)]
    masks = [np.eye(C, dtype=bool)]
    s = C // 2
    while s >= 1:
        blk = i // s
        seg.append(np.where(blk % 2 == 1, (ip > blk * s) & (ip <= i), (ip > i) & (ip <= (blk + 1) * s)))
        masks.append((blk % 2 == 1) & (ip // s == blk - 1))
        s //= 2
    seg = np.concatenate([x.astype(np.float32) for x in seg], axis=0)
    return seg, np.stack([m.astype(np.float32) for m in masks])


def _hgrn_kernel(q_ref, f_ref, i_ref, g_ref, llb_ref, l1m_ref, oml_ref, ng_ref,
                 seg_ref, msk_ref, o_ref, st_ref):
    @pl.when(pl.program_id(0) == 0)
    def _():
        st_ref[...] = jnp.zeros_like(st_ref)

    C = q_ref.shape[1]
    nlev = msk_ref.shape[0] - 1
    for bi in range(q_ref.shape[0]):
        q = _silu(q_ref[bi])
        z = f_ref[bi]
        log_sig = jnp.minimum(z, 0.0) - jnp.log1p(jnp.exp(-jnp.abs(z)))
        cc = l1m_ref[...] + log_sig
        llb = llb_ref[...]
        logf = jnp.maximum(llb, cc) + jnp.log1p(jnp.exp(-jnp.abs(llb - cc)))
        k = oml_ref[...] * _sigmoid(-z)
        v = i_ref[bi]
        seg = _exact_left_dot(seg_ref[...], logf)
        outs = []
        for h in range(HG_HEADS):
            sl = slice(h * LANES, (h + 1) * LANES)
            qh, kh, vh = q[:, sl], k[:, sl], v[:, sl]
            a = msk_ref[0] * _dot_nt(qh.astype(BF16), kh.astype(BF16))
            for l in range(nlev):
                dec = jnp.exp(seg[(2 + l) * C:(3 + l) * C, sl])
                a = a + msk_ref[1 + l] * _dot_nt((qh * dec).astype(BF16), (kh * dec).astype(BF16))
            b = seg[0:C, sl]
            st = st_ref[bi, h]
            o = _dot(a.astype(BF16), vh.astype(BF16))
            o = o + _dot_nt((qh * jnp.exp(b)).astype(BF16), st.astype(BF16))
            kd = (kh * jnp.exp(seg[C:2 * C, sl])).astype(BF16)
            st_ref[bi, h] = st * jnp.exp(b[C - 1:C, :]) + _dot(vh.T.astype(BF16), kd)
            outs.append(o * lax.rsqrt(jnp.mean(o * o, axis=-1, keepdims=True) + 1e-6))
        o = jnp.concatenate(outs, axis=1)
        o_ref[bi] = (o * ng_ref[...] * _silu(g_ref[bi])).astype(o_ref.dtype)


def _hgrn(proj, B, S, llb, l1m, oml, ng):
    C = CHUNK
    nc = S // C
    seg, msk = _hgrn_tables(C)
    seg, msk = jnp.asarray(seg, BF16), jnp.asarray(msk)
    proj3 = proj.reshape(B, S, proj.shape[1])
    col = lambda name: pl.BlockSpec((B, C, GROUP_W), lambda c, n=COL[name] // 4: (0, c, n))
    vec = pl.BlockSpec((1, GROUP_W), lambda c: (0, 0))
    full2 = lambda a: pl.BlockSpec(a.shape, lambda c: (0, 0))
    return pl.pallas_call(
        _hgrn_kernel,
        grid=(nc,),
        in_specs=[col("hq"), col("hf"), col("hi"), col("hg"), vec, vec, vec, vec,
                  full2(seg), pl.BlockSpec(msk.shape, lambda c: (0, 0, 0))],
        out_specs=pl.BlockSpec((B, C, GROUP_W), lambda c: (0, c, 0)),
        out_shape=jax.ShapeDtypeStruct((B, S, GROUP_W), BF16),
        scratch_shapes=[pltpu.VMEM((B, HG_HEADS, LANES, LANES), F32)],
        compiler_params=_params(("arbitrary",)),
        name="hgrn2",
    )(proj3, proj3, proj3, proj3, llb, l1m, oml, ng, seg, msk).reshape(B * S, GROUP_W)


def _ssd_kernel(z_ref, xbc_ref, sm_ref, cw_ref, cb_ref, dtb_ref, aneg_ref, dsk_ref, ng_ref,
                ex_ref, o_ref, tail_ref, st_ref):
    @pl.when(pl.program_id(1) == 0)
    def _():
        tail_ref[...] = jnp.zeros_like(tail_ref)
        st_ref[...] = jnp.zeros_like(st_ref)

    L = xbc_ref.shape[0]
    x = xbc_ref[...]
    xe = jnp.concatenate([tail_ref[...], x], axis=0)
    cw = cw_ref[...]
    conv = cb_ref[...]
    for kk in range(SSM_CONV):
        conv = conv + cw[kk:kk + 1, :] * xe[5 + kk:5 + kk + L, :]
    tail_ref[...] = x[L - 8:L, :]
    conv = _silu(conv)
    xs = conv[:, 0:GROUP_W]
    bm = conv[:, GROUP_W:GROUP_W + 256]
    cm = conv[:, GROUP_W + 256:GROUP_W + 512]

    dtf = _softplus(sm_ref[...] + dtb_ref[...])
    la = dtf * aneg_ref[...]
    ri = lax.broadcasted_iota(jnp.int32, (L, L), 0)
    ci = lax.broadcasted_iota(jnp.int32, (L, L), 1)
    tri = ri >= ci
    bfull = _exact_left_dot(tri.astype(BF16), la)
    ex = ex_ref[...]
    bexp = _exact_right_dot(bfull, ex)
    dtexp = _exact_right_dot(dtf, ex)
    b_t = bfull.T
    xdt = xs * dtexp
    lane = lax.broadcasted_iota(jnp.int32, (L, LANES), 1)

    scores = []
    for g in range(SSM_GROUPS):
        cg = cm[:, g * SSM_STATE:(g + 1) * SSM_STATE].astype(BF16)
        bg = bm[:, g * SSM_STATE:(g + 1) * SSM_STATE].astype(BF16)
        cb = _dot_nt(cg, bg)
        for hh in range(SSM_HEADS // SSM_GROUPS):
            h = g * (SSM_HEADS // SSM_GROUPS) + hh
            bcol = bfull[:, DT_LANE0 + h:DT_LANE0 + h + 1]
            brow = b_t[DT_LANE0 + h:DT_LANE0 + h + 1, :]
            dec = jnp.exp(jnp.where(tri, bcol - brow, NEG))
            scores.append((cb * dec).astype(BF16))
    y_pairs = []
    for u in range(SSM_HEADS // 2):
        slab = xdt[:, u * LANES:(u + 1) * LANES]
        lo = jnp.where(lane < SSM_HEAD_DIM, slab, 0.0).astype(BF16)
        hi = jnp.where(lane >= SSM_HEAD_DIM, slab, 0.0).astype(BF16)
        y_pairs.append(_dot(scores[2 * u], lo) + _dot(scores[2 * u + 1], hi))
    y_intra = jnp.concatenate(y_pairs, axis=1)

    blast = bexp[L - 1:L, :]
    w = (xdt * jnp.exp(blast - bexp)).astype(BF16)
    y_inter = []
    for g in range(SSM_GROUPS):
        gs = slice(g * 256, (g + 1) * 256)
        cg = cm[:, g * SSM_STATE:(g + 1) * SSM_STATE].astype(BF16)
        st = st_ref[g]
        y_inter.append(_dot(cg, st.astype(BF16)))
        bg_t = bm[:, g * SSM_STATE:(g + 1) * SSM_STATE].T.astype(BF16)
        st_ref[g] = st * jnp.exp(blast[:, gs]) + _dot(bg_t, w[:, gs])
    y = y_intra + jnp.concatenate(y_inter, axis=1) * jnp.exp(bexp) + dsk_ref[...] * xs
    y = y * _silu(z_ref[...])
    halves = []
    for g in range(SSM_GROUPS):
        seg = y[:, g * 256:(g + 1) * 256]
        halves.append(seg * lax.rsqrt(jnp.mean(seg * seg, axis=-1, keepdims=True) + 1e-6))
    o_ref[...] = (jnp.concatenate(halves, axis=1) * ng_ref[...]).astype(o_ref.dtype)


def _ssd(proj, small, B, S, cw, cb, dtb, aneg, dsk, ng):
    L = CHUNK
    nc = S // L
    ex = np.zeros((LANES, GROUP_W), np.float32)
    for h in range(SSM_HEADS):
        ex[DT_LANE0 + h, h * SSM_HEAD_DIM:(h + 1) * SSM_HEAD_DIM] = 1.0
    ex = jnp.asarray(ex, BF16)
    full2 = lambda a: pl.BlockSpec(a.shape, lambda b, c: (0, 0))
    return pl.pallas_call(
        _ssd_kernel,
        grid=(B, nc),
        in_specs=[
            pl.BlockSpec((L, GROUP_W), lambda b, c: (b * nc + c, COL["sz"] // 4)),
            pl.BlockSpec((L, 1024), lambda b, c: (b * nc + c, COL["sxbc"] // 8)),
            pl.BlockSpec((L, LANES), lambda b, c: (b * nc + c, 0)),
            full2(cw), full2(cb), full2(dtb), full2(aneg), full2(dsk), full2(ng), full2(ex)],
        out_specs=pl.BlockSpec((L, GROUP_W), lambda b, c: (b * nc + c, 0)),
        out_shape=jax.ShapeDtypeStruct((B * S, GROUP_W), BF16),
        scratch_shapes=[pltpu.VMEM((8, 1024), F32), pltpu.VMEM((SSM_GROUPS, SSM_STATE, 256), F32)],
        compiler_params=_params(("parallel", "arbitrary")),
        name="ssd",
    )(proj, proj, small, cw, cb, dtb, aneg, dsk, ng, ex)


def _ret_kernel(q_ref, k_ref, v_ref, g_ref, cos_ref, sin_ref, dec_ref, qs_ref, ks_ref, sd_ref,
                o_ref, st_ref):
    @pl.when(pl.program_id(0) == 0)
    def _():
        st_ref[...] = jnp.zeros_like(st_ref)

    cos = cos_ref[...]
    sin = sin_ref[...]
    for b in range(q_ref.shape[0]):
        outs = []
        for h in range(RET_HEADS):
            sl = slice(h * LANES, (h + 1) * LANES)
            qh = q_ref[b, :, sl]
            kh = k_ref[b, :, sl]
            qh = qh * cos + pltpu.roll(qh, RET_DK // 2, axis=1) * sin
            kh = (kh * cos + pltpu.roll(kh, RET_DK // 2, axis=1) * sin) * (RET_DK ** -0.5)
            vh = v_ref[b, :, sl].astype(BF16)
            sc = (_dot_nt(qh.astype(BF16), kh.astype(BF16)) * dec_ref[h]).astype(BF16)
            st = st_ref[b, h]
            y = _dot(sc, vh) + _dot((qh * qs_ref[:, sl]).astype(BF16), st.astype(BF16))
            kd_t = (kh * ks_ref[:, sl]).T.astype(BF16)
            st_ref[b, h] = st * sd_ref[h] + _dot(kd_t, vh)
            mu = jnp.mean(y, axis=-1, keepdims=True)
            d = y - mu
            outs.append(d * lax.rsqrt(jnp.mean(d * d, axis=-1, keepdims=True) + 1e-5))
        o_ref[b] = (_silu(g_ref[b]) * jnp.concatenate(outs, axis=1)).astype(o_ref.dtype)


def _retention(proj, B, S, cos_t, sin_t):
    L = CHUNK
    nc = S // L
    lg = jnp.log(1.0 - 2.0 ** (-5.0 - jnp.arange(RET_HEADS, dtype=F32)))
    i = jnp.arange(L, dtype=F32)
    diff = i[:, None] - i[None, :]
    dec = jnp.where(diff >= 0, jnp.exp(lg[:, None, None] * jnp.maximum(diff, 0.0)), 0.0)
    rep = lambda t: jnp.repeat(t, LANES, axis=1)
    qs = rep(jnp.exp((i[:, None] + 1.0) * lg[None, :]))
    ks = rep(jnp.exp((L - 1.0 - i[:, None]) * lg[None, :]))
    sd = jnp.broadcast_to(jnp.exp(L * lg)[:, None, None], (RET_HEADS, LANES, LANES))
    proj3 = proj.reshape(B, S, proj.shape[1])
    col = lambda name: pl.BlockSpec((B, L, GROUP_W), lambda c, n=COL[name] // 4: (0, c, n))
    return pl.pallas_call(
        _ret_kernel,
        grid=(nc,),
        in_specs=[col("rq"), col("rk"), col("rv"), col("rg"),
                  pl.BlockSpec((L, LANES), lambda c: (c, 0)),
                  pl.BlockSpec((L, LANES), lambda c: (c, 0)),
                  pl.BlockSpec((RET_HEADS, L, L), lambda c: (0, 0, 0)),
                  pl.BlockSpec((L, GROUP_W), lambda c: (0, 0)),
                  pl.BlockSpec((L, GROUP_W), lambda c: (0, 0)),
                  pl.BlockSpec((RET_HEADS, LANES, LANES), lambda c: (0, 0, 0))],
        out_specs=pl.BlockSpec((B, L, GROUP_W), lambda c: (0, c, 0)),
        out_shape=jax.ShapeDtypeStruct((B, S, GROUP_W), BF16),
        scratch_shapes=[pltpu.VMEM((B, RET_HEADS, RET_DK, RET_DK), F32)],
        compiler_params=_params(("arbitrary",)),
        name="retention",
    )(proj3, proj3, proj3, proj3, cos_t, sin_t, dec, qs, ks, sd).reshape(B * S, GROUP_W)


def _t5_bucket(dist):
    n = jnp.maximum(dist, 0)
    nf = jnp.maximum(n, 1).astype(F32)
    large = REL_EXACT + (jnp.log(nf / REL_EXACT) / math.log(REL_MAX_DIST / REL_EXACT)
                         * (REL_BUCKETS - REL_EXACT)).astype(jnp.int32)
    return jnp.where(n < REL_EXACT, n, jnp.minimum(large, REL_BUCKETS - 1))


def _head_bias(bucket, rel_ref):
    rows, cols = bucket.shape
    per_head = []
    for h in range(NSA_HEADS):
        tbl = jnp.broadcast_to(rel_ref[h:h + 1, :], (rows, LANES))
        chunks = [jnp.take_along_axis(tbl, bucket[:, c:c + LANES], axis=1)
                  for c in range(0, cols, LANES)]
        per_head.append(chunks[0] if len(chunks) == 1 else jnp.concatenate(chunks, axis=1))
    return jnp.stack(per_head, axis=0)


def _stack_heads(qw):
    return jnp.concatenate([qw[:, h * LANES:(h + 1) * LANES] for h in range(NSA_HEADS)],
                           axis=0).astype(BF16)


def _cmp_kernel(g_ref, pe_ref, w1a_ref, w1b_ref, w2_ref, o_ref):
    nb = g_ref.shape[0]
    gw = g_ref.shape[1] // 4
    pe = pe_ref[...]
    slabs = [g_ref[:, s * gw:(s + 1) * gw] for s in range(4)]
    nxt0 = pltpu.roll(slabs[0], nb - 1, axis=0)
    for s in range(4):
        a = (slabs[s] + pe[0:1, :]).astype(BF16)
        bn = ((slabs[s + 1] if s < 3 else nxt0) + pe[1:2, :]).astype(BF16)
        hid = _silu(_dot(a, w1a_ref[...]) + _dot(bn, w1b_ref[...]))
        o_ref[s * nb:(s + 1) * nb, :] = _dot(hid.astype(BF16), w2_ref[...]).astype(o_ref.dtype)


def _compress(g, pe2, w1a, w1b, w2bd):
    B, nb, gw4 = g.shape
    full2 = lambda a: pl.BlockSpec(a.shape, lambda b: (0, 0))
    slab_major = pl.pallas_call(
        _cmp_kernel,
        grid=(B,),
        in_specs=[pl.BlockSpec((None, nb, gw4), lambda b: (b, 0, 0)),
                  full2(pe2), full2(w1a), full2(w1b), full2(w2bd)],
        out_specs=pl.BlockSpec((None, 4 * nb, LANES), lambda b: (b, 0, 0)),
        out_shape=jax.ShapeDtypeStruct((B, 4 * nb, LANES), BF16),
        compiler_params=_params(("parallel",)),
        name="nsa_compress",
    )(g, pe2, w1a, w1b, w2bd)
    return slab_major.reshape(B, 4, nb, LANES).transpose(0, 2, 1, 3).reshape(B, 4 * nb, LANES)


def _cmpattn_kernel(q_ref, kc_ref, vc_ref, band_ref, st_ref, oc_ref, selb_ref):
    tq = q_ref.shape[0]
    ncmp = kc_ref.shape[0]
    nb = ncmp // 4
    R = NSA_HEADS * tq
    q0 = pl.program_id(1) * tq
    nd = band_ref.shape[0]
    Q = _stack_heads(q_ref[...])
    s3 = _dot_nt(Q, kc_ref[...]).reshape(NSA_HEADS, tq, ncmp)
    parts = []
    for ch in range(ncmp // LANES):
        d = jnp.clip((q0 - ch * LANES * NSA_CMP_STRIDE) // tq + CMP_BAND_OFF, 0, nd - 1)
        parts.append(s3[:, :, ch * LANES:(ch + 1) * LANES] + band_ref[d])
    s3 = jnp.concatenate(parts, axis=2)
    mx = jnp.max(s3, axis=-1, keepdims=True)
    e = jnp.exp(s3 - mx)
    live = (q0 + lax.broadcasted_iota(jnp.int32, (tq, 1), 0) >= NSA_CMP_BLOCK - 1)[None]
    p = e * jnp.where(live, 1.0 / jnp.sum(e, axis=-1, keepdims=True), 0.0)
    oc_ref[...] = _dot(p.reshape(R, ncmp).astype(BF16), vc_ref[...])

    ps = p.reshape(NSA_KV, NSA_GROUP, tq, ncmp).sum(axis=1).reshape(NSA_KV * tq, ncmp)
    imp = _exact_right_dot(ps, st_ref[...])
    j = lax.broadcasted_iota(jnp.int32, (NSA_KV * tq, nb), 1)
    t = q0 + (lax.broadcasted_iota(jnp.int32, (NSA_KV * tq, nb), 0) % tq)
    cur = t // NSA_SLC_BLOCK
    forced = (j == 0) | (j == cur) | (j == cur - 1)
    score = jnp.where(j > cur, -1.0, jnp.where(forced, NSA_GROUP + 1.0, imp))
    score = score.T
    jf = lax.broadcasted_iota(jnp.int32, score.shape, 0).astype(F32)
    sel = score == NSA_GROUP + 1.0
    score = jnp.where(sel, -jnp.inf, score)
    for _ in range(min(NSA_TOP_N, nb) - 3):
        best = jnp.max(score, axis=0, keepdims=True)
        first = jnp.min(jnp.where(score == best, jf, float(nb)), axis=0, keepdims=True)
        hit = jf == first
        sel = sel | hit
        score = jnp.where(hit, -jnp.inf, score)
    selb = jnp.where(sel, 0.0, NEG).T.astype(selb_ref.dtype)
    selb_ref[:, 0:nb] = selb[0:tq]
    selb_ref[:, nb:2 * nb] = selb[tq:2 * tq]


def _cmpattn(proj, kcmp, vcmp, cband, B, S):
    tq = TQ
    nqt = S // tq
    ncmp = kcmp.shape[1]
    nb = ncmp // 4
    off = np.arange(ncmp)[:, None] - 4 * np.arange(nb)[None, :]
    stencil = np.where((off >= 0) & (off <= 2), 1.0, np.where((off == -1) | (off == 3), 0.5, 0.0))
    stencil = jnp.asarray(stencil, BF16)
    once = pl.Buffered(1)
    return pl.pallas_call(
        _cmpattn_kernel,
        grid=(B, nqt),
        in_specs=[pl.BlockSpec((tq, 1024), lambda b, i: (b * nqt + i, COL["nqw"] // 8)),
                  pl.BlockSpec((None, ncmp, LANES), lambda b, i: (b, 0, 0)),
                  pl.BlockSpec((None, ncmp, LANES), lambda b, i: (b, 0, 0)),
                  pl.BlockSpec(cband.shape, lambda b, i: (0, 0, 0, 0), pipeline_mode=once),
                  pl.BlockSpec(stencil.shape, lambda b, i: (0, 0))],
        out_specs=[pl.BlockSpec((NSA_HEADS * tq, LANES), lambda b, i: (b * nqt + i, 0)),
                   pl.BlockSpec((tq, 2 * nb), lambda b, i: (b * nqt + i, 0))],
        out_shape=[jax.ShapeDtypeStruct((B * S * NSA_HEADS, LANES), F32),
                   jax.ShapeDtypeStruct((B * S, 2 * nb), BF16)],
        compiler_params=_params(("parallel", "arbitrary")),
        name="nsa_cmp_attn_topk",
    )(proj, kcmp, vcmp, cband, stencil)


def _band_kernel(rel_ref, o_ref, *, entry_off, key_step, key_end):
    tq = o_ref.shape[1]
    delta = (pl.program_id(0) - entry_off) * tq
    row = lax.broadcasted_iota(jnp.int32, (tq, LANES), 0)
    col = lax.broadcasted_iota(jnp.int32, (tq, LANES), 1)
    dist = delta + row - (col * key_step + key_end)
    bias = _head_bias(_t5_bucket(dist), rel_ref)
    o_ref[...] = jnp.where((dist >= 0)[None], bias, NEG)


def _band_table(rel_t, tq, entry_off, key_step=1, key_end=0):
    nd = -(-(REL_MAX_DIST + key_step * (LANES - 1) + key_end) // tq) + entry_off + 1
    return pl.pallas_call(
        functools.partial(_band_kernel, entry_off=entry_off, key_step=key_step, key_end=key_end),
        grid=(nd,),
        in_specs=[pl.BlockSpec(rel_t.shape, lambda d: (0, 0))],
        out_specs=pl.BlockSpec((None, NSA_HEADS, tq, LANES), lambda d: (d, 0, 0, 0)),
        out_shape=jax.ShapeDtypeStruct((nd, NSA_HEADS, tq, LANES), F32),
        compiler_params=_params(("parallel",)),
        name="nsa_bias_band",
    )(rel_t)


def _selattn_kernel(q_ref, selb_ref, ks_ref, vs_ref, band_ref, sp_ref, farq_ref, os_ref,
                    qaug_ref, msel_ref, s_ref, m_ref, acc_ref):
    tq = q_ref.shape[0]
    R = NSA_HEADS * tq
    nb = selb_ref.shape[1] // 2
    n_kt = msel_ref.shape[0]
    tk = ks_ref.shape[0] // n_kt
    nd = band_ref.shape[0]
    q0 = pl.program_id(1) * tq
    qaug_ref[:, 0:LANES] = _stack_heads(q_ref[...])
    selb2 = jnp.concatenate([selb_ref[:, 0:nb], selb_ref[:, nb:2 * nb]], axis=0)
    spread = _dot(selb2, sp_ref[...])
    for c in range(n_kt):
        msel_ref[c] = spread[:, c * LANES:(c + 1) * LANES].astype(BF16)
    m_ref[...] = jnp.full(m_ref.shape, NEG, F32)
    acc_ref[...] = jnp.zeros_like(acc_ref)
    n_tiles = (q0 + tq - 1) // tk + 1

    far_pairs = (jnp.maximum(q0 - (REL_MAX_DIST - 1), 0) // tk) // 2

    def scores(c, slot):
        c = jnp.minimum(c, n_kt - 1)
        m2 = msel_ref[c]
        flags = jnp.concatenate([m2[0:tq]] * NSA_GROUP + [m2[tq:2 * tq]] * NSA_GROUP, axis=0)
        qaug_ref[:, LANES:2 * LANES] = jnp.where(c < 2 * far_pairs, flags + farq_ref[...], flags)
        s_ref[slot] = _dot_nt(qaug_ref[...], ks_ref[pl.ds(pl.multiple_of(c * tk, tk), tk), :])

    def accumulate(c, slot, far):
        k0 = pl.multiple_of(c * tk, tk)
        s = s_ref[slot]
        if not far:
            s3 = s.reshape(NSA_HEADS, tq, tk)
            parts = []
            for ch in range(tk // LANES):
                d = jnp.clip((q0 - k0) // tq - ch * (LANES // tq) + BAND_OFF, 0, nd - 1)
                parts.append(s3[:, :, ch * LANES:(ch + 1) * LANES] + band_ref[d])
            s = jnp.concatenate(parts, axis=2).reshape(R, tk)
        m_old = m_ref[...]
        m_new = jnp.maximum(m_old, jnp.max(s, axis=-1, keepdims=True))
        alpha = jnp.exp(m_old - m_new)
        p = jnp.exp(s - jnp.tile(m_new, (1, tk // LANES)))
        acc_ref[...] = jnp.tile(alpha, (1, 2)) * acc_ref[...] + _dot(p.astype(BF16), vs_ref[pl.ds(k0, tk), :])
        m_ref[...] = m_new

    def pair(i, far):
        c = 2 * i
        scores(c + 1, 1)
        accumulate(c, 0, far)
        scores(c + 2, 0)
        accumulate(c + 1, 1, far)

    scores(0, 0)
    lax.fori_loop(0, far_pairs, lambda i, carry: pair(i, True), None)
    lax.fori_loop(far_pairs, n_tiles // 2, lambda i, carry: pair(i, False), None)

    @pl.when(n_tiles % 2 == 1)
    def _():
        accumulate(n_tiles - 1, 0, False)

    acc = acc_ref[...]
    os_ref[...] = acc[:, 0:LANES] / acc[:, LANES:2 * LANES]


def _selattn(proj, selb, ks, vs, band, farq, B, S):
    tq = TQ
    tk = min(TK, S)
    n_kt = S // tk
    bpt = tk // NSA_SLC_BLOCK
    nqt = S // tq
    nb = S // NSA_SLC_BLOCK
    R = NSA_HEADS * tq
    sp = np.zeros((nb, n_kt * LANES), np.float32)
    sp[np.arange(nb), (np.arange(nb) // bpt) * LANES + np.arange(nb) % bpt] = 1.0
    once = pl.Buffered(1)
    return pl.pallas_call(
        _selattn_kernel,
        grid=(B, nqt),
        in_specs=[pl.BlockSpec((tq, 1024), lambda b, i: (b * nqt + i, COL["nqw"] // 8)),
                  pl.BlockSpec((tq, 2 * nb), lambda b, i: (b * nqt + i, 0)),
                  pl.BlockSpec((None, S, 2 * LANES), lambda b, i: (b, 0, 0), pipeline_mode=once),
                  pl.BlockSpec((None, S, 2 * LANES), lambda b, i: (b, 0, 0), pipeline_mode=once),
                  pl.BlockSpec(band.shape, lambda b, i: (0, 0, 0, 0), pipeline_mode=once),
                  pl.BlockSpec(sp.shape, lambda b, i: (0, 0)),
                  pl.BlockSpec(farq.shape, lambda b, i: (0, 0))],
        out_specs=pl.BlockSpec((R, LANES), lambda b, i: (b * nqt + i, 0)),
        out_shape=jax.ShapeDtypeStruct((B * S * NSA_HEADS, LANES), F32),
        scratch_shapes=[pltpu.VMEM((R, 2 * LANES), BF16), pltpu.VMEM((n_kt, 2 * tq, LANES), BF16),
                        pltpu.VMEM((2, R, tk), F32), pltpu.VMEM((R, LANES), F32),
                        pltpu.VMEM((R, 2 * LANES), F32)],
        compiler_params=_params(("parallel", "arbitrary")),
        name="nsa_sel_attn",
    )(proj, selb, ks, vs, band, jnp.asarray(sp, BF16), farq)


def _winattn_kernel(q_ref, sm_ref, oc_ref, os_ref, kw_ref, vw_ref, band_ref, ng_ref, o_ref):
    tq = q_ref.shape[1]
    R = NSA_HEADS * tq
    S = kw_ref.shape[1]
    span = min(WIN_SPAN, S)
    nd = band_ref.shape[0]
    q0 = pl.program_id(0) * tq
    start = pl.multiple_of(jnp.clip(q0 + tq - span, 0, S - span), tq)
    row = lax.broadcasted_iota(jnp.int32, (tq, span), 0)
    col = lax.broadcasted_iota(jnp.int32, (tq, span), 1)
    in_window = ((q0 + row) - (start + col) < NSA_WINDOW)[None]
    lane = lax.broadcasted_iota(jnp.int32, (tq, LANES), 1)
    for b in range(q_ref.shape[0]):
        Q = _stack_heads(q_ref[b])
        kt = kw_ref[b, pl.ds(start, span), :]
        vt = vw_ref[b, pl.ds(start, span), :]
        s3 = _dot_nt(Q, kt).reshape(NSA_HEADS, tq, span)
        parts = []
        for ch in range(span // LANES):
            d = jnp.clip((q0 - start) // tq - ch * (LANES // tq) + BAND_OFF, 0, nd - 1)
            parts.append(s3[:, :, ch * LANES:(ch + 1) * LANES] + band_ref[d])
        s3 = jnp.where(in_window, jnp.concatenate(parts, axis=2), NEG)
        mx = jnp.max(s3, axis=-1, keepdims=True)
        e = jnp.exp(s3 - mx)
        p = e * (1.0 / jnp.sum(e, axis=-1, keepdims=True))
        ow = _dot(p.reshape(R, span).astype(BF16), vt)

        gates = _sigmoid(sm_ref[b])
        heads = []
        ssq = jnp.zeros((tq, 1), F32)
        for h in range(NSA_HEADS):
            rs = slice(h * tq, (h + 1) * tq)
            g = [gates[:, GATE_LANE0 + 3 * h + br:GATE_LANE0 + 3 * h + br + 1] for br in range(3)]
            oh = g[0] * oc_ref[b, rs, :] + g[1] * os_ref[b, rs, :] + g[2] * ow[rs, :]
            kv = h // NSA_GROUP
            valid = (lane >= kv * NSA_HEAD_DIM) & (lane < (kv + 1) * NSA_HEAD_DIM)
            oh = jnp.where(valid, oh, 0.0)
            ssq = ssq + jnp.sum(oh * oh, axis=-1, keepdims=True)
            heads.append(oh)
        rinv = lax.rsqrt(ssq / GROUP_W + 1e-6)
        o_ref[b] = (jnp.concatenate(heads, axis=1) * rinv * ng_ref[...]).astype(o_ref.dtype)


def _winattn(proj, small, oc, os_, kw, vw, band, ngw, B, S):
    tq = TQ
    R = NSA_HEADS * tq
    once = pl.Buffered(1)
    per_b = lambda t: t.reshape(B, t.shape[0] // B, t.shape[1])
    return pl.pallas_call(
        _winattn_kernel,
        grid=(S // tq,),
        in_specs=[pl.BlockSpec((B, tq, 1024), lambda i: (0, i, COL["nqw"] // 8)),
                  pl.BlockSpec((B, tq, LANES), lambda i: (0, i, 0)),
                  pl.BlockSpec((B, R, LANES), lambda i: (0, i, 0)),
                  pl.BlockSpec((B, R, LANES), lambda i: (0, i, 0)),
                  pl.BlockSpec((B, S, LANES), lambda i: (0, 0, 0), pipeline_mode=once),
                  pl.BlockSpec((B, S, LANES), lambda i: (0, 0, 0), pipeline_mode=once),
                  pl.BlockSpec(band.shape, lambda i: (0, 0, 0, 0), pipeline_mode=once),
                  pl.BlockSpec(ngw.shape, lambda i: (0, 0))],
        out_specs=pl.BlockSpec((B, tq, NSA_HEADS * LANES), lambda i: (0, i, 0)),
        out_shape=jax.ShapeDtypeStruct((B, S, NSA_HEADS * LANES), BF16),
        compiler_params=_params(("arbitrary",)),
        name="nsa_win_attn_merge",
    )(per_b(proj), per_b(small), per_b(oc), per_b(os_), kw, vw, band, ngw).reshape(B * S, NSA_HEADS * LANES)


def _widen_heads(x, axis):
    x = jnp.moveaxis(x, axis, -1)
    lead = x.shape[:-1]
    x = x.reshape(*lead, NSA_KV, NSA_GROUP, 1, NSA_HEAD_DIM)
    sel = jnp.eye(NSA_KV, dtype=x.dtype).reshape(NSA_KV, 1, NSA_KV, 1)
    x = (x * sel).reshape(*lead, NSA_HEADS * LANES)
    return jnp.moveaxis(x, -1, axis)


def _build_w_in(w):
    (hq, hf, hi, hg, nq, nkc, nvc, nks, nvs, nkw, nvw, ngate,
     sz, sxbc, sdt, rq, rk, rv, rg) = jnp.split(w, IN_SPLITS, axis=1)
    D = w.shape[0]
    nqw = _widen_heads(nq * NSA_HEAD_DIM ** -0.5, 1)
    deint = lambda t: t.reshape(D, RET_HEADS, RET_DK // 2, 2).transpose(0, 1, 3, 2).reshape(D, GROUP_W)
    small = jnp.concatenate([ngate, sdt, jnp.zeros((D, LANES - 32), w.dtype)], axis=1)
    wide = [hq, hf, hi, hg, nqw, sxbc, sz, deint(rq), deint(rk), rv, rg]
    narrow = [nkc, nvc, nks, nvs, nkw, nvw, small]
    return jnp.concatenate(wide, axis=1).astype(BF16), jnp.concatenate(narrow, axis=1).astype(BF16)


def _build_cmp_weights(pe, w1, w2):
    w1r = w1.reshape(2, NSA_CMP_STRIDE, NSA_HEAD_DIM, NSA_CMP_HIDDEN)
    eye = jnp.eye(NSA_KV, dtype=w1.dtype)
    big = jnp.einsum("ardc,kj->arkdjc", w1r, eye).reshape(
        2, NSA_CMP_STRIDE * NSA_KV * NSA_HEAD_DIM, NSA_KV * NSA_CMP_HIDDEN)
    w2bd = jnp.einsum("cd,kj->kcjd", w2, eye).reshape(NSA_KV * NSA_CMP_HIDDEN, NSA_KV * NSA_HEAD_DIM)
    per = pe.reshape(2, NSA_CMP_STRIDE, 1, NSA_HEAD_DIM)
    pe2 = jnp.broadcast_to(per, (2, NSA_CMP_STRIDE, NSA_KV, NSA_HEAD_DIM)).reshape(2, -1)
    return pe2, big[0].astype(BF16), big[1].astype(BF16), w2bd.astype(BF16)


def _rotary_tables(S):
    half = RET_DK // 2
    theta = 1.0 / (10000.0 ** jnp.linspace(0.0, 1.0, half, dtype=F32))
    ang = jnp.arange(S, dtype=F32)[:, None] * theta[None, :]
    cos, sin = jnp.cos(ang), jnp.sin(ang)
    return jnp.concatenate([cos, cos], axis=1), jnp.concatenate([-sin, sin], axis=1)


def _mixer(x2, B, S, l, p, lower_bounds, band, cband, farq, cos_t, sin_t):
    T = B * S
    w_wide, w_narrow = _build_w_in(p["w_in"][l])
    proj = _proj(x2, w_wide, tm=min(1024, T), tn=NCOL * LANES // 4)
    kc, vc, ks, vs, kw, vw, small = _kvproj(x2, w_narrow, tm=min(512, T))
    row = lambda v: v.reshape(1, -1).astype(F32)

    lb = lower_bounds[l].astype(F32)
    o_a = _hgrn(proj, B, S, row(jnp.log(lb)), row(jnp.log1p(-lb)), row(1.0 - lb),
                row(p["hgrn_norm_g"][l]))

    nb = S // NSA_SLC_BLOCK
    grp = lambda t: t.reshape(B, nb, 4 * NSA_CMP_STRIDE * LANES)
    kcmp = _compress(grp(kc), *_build_cmp_weights(p["nsa_pe_k"][l], p["nsa_w1_k"][l], p["nsa_w2_k"][l]))
    vcmp = _compress(grp(vc), *_build_cmp_weights(p["nsa_pe_v"][l], p["nsa_w1_v"][l], p["nsa_w2_v"][l]))
    o_cmp, selb = _cmpattn(proj, kcmp, vcmp, cband, B, S)
    seq = lambda t: t.reshape(B, S, t.shape[1])
    o_sel = _selattn(proj, selb, seq(ks), seq(vs), band, farq, B, S)
    ngw = _widen_heads(p["nsa_norm_g"][l].astype(F32), 0).reshape(1, -1)
    o_b = _winattn(proj, small, o_cmp, o_sel, seq(kw), seq(vw), band, ngw, B, S)

    lane_vec = lambda v: jnp.zeros((1, LANES), F32).at[0, DT_LANE0:DT_LANE0 + SSM_HEADS].set(v.astype(F32))
    o_c = _ssd(proj, small, B, S, p["ssm_conv_w"][l].astype(F32), row(p["ssm_conv_b"][l]),
               lane_vec(p["ssm_dt_bias"][l]), lane_vec(-jnp.exp(p["ssm_a_log"][l].astype(F32))),
               row(jnp.repeat(p["ssm_d"][l].astype(F32), SSM_HEAD_DIM)), row(p["ssm_norm_g"][l]))

    o_d = _retention(proj, B, S, cos_t, sin_t)

    w_out = p["w_out"][l]
    wa, wb, wc, wd = (w_out[i * GROUP_W:(i + 1) * GROUP_W] for i in range(4))
    return o_a, o_b, o_c, o_d, wa.astype(BF16), _widen_heads(wb, 0).astype(BF16), wc.astype(BF16), wd.astype(BF16)


def kernel(x, ln1_g, ln1_b, ffn1_w1, ffn1_w3, ffn1_w2, ln2_g, ln2_b, w_in, w_out, hgrn_lb_logits, hgrn_norm_g, nsa_pe_k, nsa_w1_k, nsa_w2_k, nsa_pe_v, nsa_w1_v, nsa_w2_v, nsa_norm_g, rel_bias, ssm_conv_w, ssm_conv_b, ssm_dt_bias, ssm_a_log, ssm_d, ssm_norm_g, ln3_g, ln3_b, ffn2_w1, ffn2_w3, ffn2_w2):
    B, S, D = x.shape
    T = B * S
    depth = w_in.shape[0]
    p = dict(w_in=w_in, w_out=w_out, hgrn_norm_g=hgrn_norm_g, nsa_pe_k=nsa_pe_k, nsa_w1_k=nsa_w1_k,
             nsa_w2_k=nsa_w2_k, nsa_pe_v=nsa_pe_v, nsa_w1_v=nsa_w1_v, nsa_w2_v=nsa_w2_v,
             nsa_norm_g=nsa_norm_g, ssm_conv_w=ssm_conv_w, ssm_conv_b=ssm_conv_b,
             ssm_dt_bias=ssm_dt_bias, ssm_a_log=ssm_a_log, ssm_d=ssm_d, ssm_norm_g=ssm_norm_g)
    cum = jnp.cumsum(jax.nn.softmax(hgrn_lb_logits.astype(F32), axis=0), axis=0)
    lower_bounds = cum - cum[:1]
    rel_t = jnp.zeros((NSA_HEADS, LANES), F32).at[:, :REL_BUCKETS].set(rel_bias.astype(F32).T)
    band = _band_table(rel_t, TQ, BAND_OFF)
    cband = _band_table(rel_t, TQ, CMP_BAND_OFF, NSA_CMP_STRIDE, NSA_CMP_BLOCK - 1)
    bpt = TK // NSA_SLC_BLOCK
    farq = jnp.zeros((NSA_HEADS, LANES), BF16).at[:, bpt:bpt + 3].set(
        jnp.stack(_split3(rel_bias.astype(F32)[REL_BUCKETS - 1]), axis=1))
    farq = jnp.repeat(farq, TQ, axis=0)
    cos_t, sin_t = _rotary_tables(S)
    row = lambda v: v.reshape(1, -1).astype(F32)
    tm = min(512, T)
    tf = 512 if ffn1_w1.shape[2] % 512 == 0 else ffn1_w1.shape[2]
    x2 = x.reshape(T, D).astype(F32)
    for l in range(depth):
        x2 = _ffn(x2, ffn1_w1[l].astype(BF16), ffn1_w3[l].astype(BF16), ffn1_w2[l].astype(BF16),
                  row(ln1_g[l]), row(ln1_b[l]), tm, tf)
        o_a, o_b, o_c, o_d, wa, wb, wc, wd = _mixer(x2, B, S, l, p, lower_bounds, band, cband, farq, cos_t, sin_t)
        x2 = _outproj(x2, o_a, o_b, o_c, o_d, wa, wb, wc, wd, row(ln2_g[l]), row(ln2_b[l]), min(512, T))
        x2 = _ffn(x2, ffn2_w1[l].astype(BF16), ffn2_w3[l].astype(BF16), ffn2_w2[l].astype(BF16),
                  row(ln3_g[l]), row(ln3_b[l]), tm, tf)
    return x2.reshape(B, S, D).astype(x.dtype)
```

```python
import functools
import math

import numpy as np
import jax
import jax.numpy as jnp
from jax import lax
from jax.experimental import pallas as pl
from jax.experimental.pallas import tpu as pltpu

F32 = jnp.float32
BF16 = jnp.bfloat16

D_MODEL = 2048
DEPTH = 2
GROUP_W = 512
ALPHA = (2 * DEPTH) ** 0.25
HG_HEADS = 4
NSA_HEADS = 8
NSA_KV = 2
NSA_GROUP = 4
NSA_HEAD_DIM = 64
NSA_CMP_STRIDE = 16
NSA_CMP_BLOCK = 32
NSA_SLC_BLOCK = 64
NSA_TOP_N = 16
NSA_WINDOW = 512
NSA_CMP_HIDDEN = 256
SSM_HEADS = 8
SSM_HEAD_DIM = 64
SSM_GROUPS = 2
SSM_STATE = 128
SSM_CONV = 4
RET_HEADS = 4
RET_DK = 128
REL_BUCKETS = 32
REL_EXACT = 16
REL_MAX_DIST = 2048
IN_SIZES = ((GROUP_W,) * 4 + (GROUP_W,) + (128,) * 6 + (24,)
            + (GROUP_W, 1024, SSM_HEADS) + (GROUP_W,) * 4)
IN_SPLITS = tuple(int(v) for v in np.cumsum(IN_SIZES)[:-1])

LANES = 128
VMEM_LIMIT = 56 * 1024 * 1024
CAST_BLOCK_BYTES = 4 * 1024 * 1024

COL = dict(hq=0, hf=4, hi=8, hg=12, nqw=16, sxbc=24, sz=32, rq=36, rk=40, rv=44, rg=48)
NCOL = 52
KV_COLS = ("nkc", "nvc", "nks", "nvs", "nkw", "nvw", "small")
GATE_LANE0 = 0
DT_LANE0 = 24

CHUNK = 128
TQ = 128
TK = 1024
BAND_OFF = 2
CMP_BAND_OFF = 1
WIN_SPAN = NSA_WINDOW + 2 * TQ
NEG = -1e30


def _params(sem):
    return pltpu.CompilerParams(dimension_semantics=sem, vmem_limit_bytes=VMEM_LIMIT)


def _dot(a, b):
    return jnp.dot(a, b, preferred_element_type=F32)


def _dot_nt(a, b):
    return lax.dot_general(a, b, (((1,), (1,)), ((), ())), preferred_element_type=F32)


def _split3(x):
    hi = x.astype(BF16)
    r1 = x - hi.astype(F32)
    mid = r1.astype(BF16)
    return hi, mid, (r1 - mid.astype(F32)).astype(BF16)


def _exact_left_dot(w, x):
    n = x.shape[1]
    y = _dot(w, jnp.concatenate(_split3(x), axis=1))
    return y[:, 0:n] + y[:, n:2 * n] + y[:, 2 * n:3 * n]


def _exact_right_dot(x, w):
    n = x.shape[0]
    y = _dot(jnp.concatenate(_split3(x), axis=0), w)
    return y[0:n] + y[n:2 * n] + y[2 * n:3 * n]


def _sigmoid(x):
    return 1.0 / (1.0 + jnp.exp(-x))


def _silu(x):
    return x * _sigmoid(x)


def _softplus(x):
    return jnp.maximum(x, 0.0) + jnp.log1p(jnp.exp(-jnp.abs(x)))


def _layer_norm(r, g, b):
    mu = jnp.mean(r, axis=-1, keepdims=True)
    d = r - mu
    var = jnp.mean(d * d, axis=-1, keepdims=True)
    return d * lax.rsqrt(var + 1e-5) * g + b


def _cast_kernel(x_ref, o_ref):
    o_ref[...] = x_ref[...].astype(o_ref.dtype)


def _to_bf16(w, l):
    _, r, c = w.shape
    tr = min(r, max(16, CAST_BLOCK_BYTES // (4 * c) // 16 * 16))
    while r % tr:
        tr -= 16
    return pl.pallas_call(
        _cast_kernel,
        grid=(r // tr,),
        in_specs=[pl.BlockSpec((None, tr, c), lambda i: (l, i, 0))],
        out_specs=pl.BlockSpec((tr, c), lambda i: (i, 0)),
        out_shape=jax.ShapeDtypeStruct((r, c), BF16),
        compiler_params=_params(("parallel",)),
        name="cast_bf16",
    )(w)


def _to_bf16_col_blocks(w, l, tc):
    _, r, c = w.shape
    return pl.pallas_call(
        _cast_kernel,
        grid=(c // tc,),
        in_specs=[pl.BlockSpec((None, r, tc), lambda j: (l, 0, j))],
        out_specs=pl.BlockSpec((None, r, tc), lambda j: (j, 0, 0)),
        out_shape=jax.ShapeDtypeStruct((c // tc, r, tc), BF16),
        compiler_params=_params(("parallel",)),
        name="cast_bf16_blocked",
    )(w)


def _ffn_kernel(x_ref, w1_ref, w3_ref, w2_ref, g_ref, b_ref, o_ref, acc_ref, xb_ref):
    j = pl.program_id(1)

    @pl.when(j == 0)
    def _():
        xb_ref[...] = x_ref[...].astype(BF16)
        acc_ref[...] = jnp.zeros_like(acc_ref)

    xb = xb_ref[...]
    h1 = _dot(xb, w1_ref[...])
    h3 = _dot(xb, w3_ref[...])
    a = (_silu(h1) * h3).astype(BF16)
    acc_ref[...] += _dot(a, w2_ref[...])

    @pl.when(j == pl.num_programs(1) - 1)
    def _():
        r = ALPHA * x_ref[...] + 0.5 * acc_ref[...]
        o_ref[...] = _layer_norm(r, g_ref[...], b_ref[...])


def _ffn(x, w1, w3, w2, g, b, tm):
    T, D = x.shape
    nf, _, tf = w1.shape
    return pl.pallas_call(
        _ffn_kernel,
        grid=(T // tm, nf),
        in_specs=[
            pl.BlockSpec((tm, D), lambda i, j: (i, 0)),
            pl.BlockSpec((None, D, tf), lambda i, j: (j, 0, 0)),
            pl.BlockSpec((None, D, tf), lambda i, j: (j, 0, 0)),
            pl.BlockSpec((tf, D), lambda i, j: (j, 0)),
            pl.BlockSpec((1, D), lambda i, j: (0, 0)),
            pl.BlockSpec((1, D), lambda i, j: (0, 0)),
        ],
        out_specs=pl.BlockSpec((tm, D), lambda i, j: (i, 0)),
        out_shape=jax.ShapeDtypeStruct((T, D), F32),
        scratch_shapes=[pltpu.VMEM((tm, D), F32), pltpu.VMEM((tm, D), BF16)],
        compiler_params=_params(("parallel", "arbitrary")),
        name="ffn_ln",
    )(x, w1, w3, w2, g, b)


def _proj_kernel(x_ref, w_ref, o_ref, xb_ref):
    @pl.when(pl.program_id(1) == 0)
    def _():
        xb_ref[...] = x_ref[...].astype(BF16)

    o_ref[...] = _dot(xb_ref[...], w_ref[...])


def _proj(x, w, tm):
    T, D = x.shape
    nn, _, tn = w.shape
    return pl.pallas_call(
        _proj_kernel,
        grid=(T // tm, nn),
        in_specs=[pl.BlockSpec((tm, D), lambda i, j: (i, 0)),
                  pl.BlockSpec((None, D, tn), lambda i, j: (j, 0, 0))],
        out_specs=pl.BlockSpec((tm, tn), lambda i, j: (i, j)),
        out_shape=jax.ShapeDtypeStruct((T, nn * tn), F32),
        scratch_shapes=[pltpu.VMEM((tm, D), BF16)],
        compiler_params=_params(("parallel", "arbitrary")),
        name="in_proj",
    )(x, w)


def _kvproj_kernel(x_ref, w_ref, kc_ref, vc_ref, ks_ref, vs_ref, kw_ref, vw_ref, sm_ref):
    tm = x_ref.shape[0]
    y = _dot(x_ref[...].astype(BF16), w_ref[...])
    piece = lambda n: y[:, n * LANES:(n + 1) * LANES]
    kc_ref[...] = piece(0)
    vc_ref[...] = piece(1)
    row = pl.program_id(0) * tm + lax.broadcasted_iota(jnp.int32, (tm, LANES), 0)
    lane = lax.broadcasted_iota(jnp.int32, (tm, LANES), 1)
    bpt = TK // NSA_SLC_BLOCK
    onehot = (((row // NSA_SLC_BLOCK) % bpt == lane) | ((lane >= bpt) & (lane < bpt + 3))).astype(BF16)
    ks_ref[...] = jnp.concatenate([piece(2).astype(BF16), onehot], axis=1)
    vs_ref[...] = jnp.concatenate([piece(3).astype(BF16), jnp.ones((tm, LANES), BF16)], axis=1)
    kw_ref[...] = piece(4).astype(BF16)
    vw_ref[...] = piece(5).astype(BF16)
    sm_ref[...] = piece(6)


def _kvproj(x, w, tm):
    T, D = x.shape
    narrow = lambda dt, width=LANES: (pl.BlockSpec((tm, width), lambda i: (i, 0)),
                                      jax.ShapeDtypeStruct((T, width), dt))
    outs = [narrow(F32), narrow(F32), narrow(BF16, 2 * LANES), narrow(BF16, 2 * LANES),
            narrow(BF16), narrow(BF16), narrow(F32)]
    return pl.pallas_call(
        _kvproj_kernel,
        grid=(T // tm,),
        in_specs=[pl.BlockSpec((tm, D), lambda i: (i, 0)), pl.BlockSpec(w.shape, lambda i: (0, 0))],
        out_specs=[o[0] for o in outs],
        out_shape=[o[1] for o in outs],
        compiler_params=_params(("parallel",)),
        name="kv_proj",
    )(x, w)


def _outproj_kernel(x_ref, oa_ref, ob_ref, oc_ref, od_ref, wa_ref, wb_ref, wc_ref, wd_ref,
                    g_ref, b_ref, o_ref):
    mix = (_dot(oa_ref[...], wa_ref[...]) + _dot(ob_ref[...], wb_ref[...])
           + _dot(oc_ref[...], wc_ref[...]) + _dot(od_ref[...], wd_ref[...]))
    o_ref[...] = _layer_norm(ALPHA * x_ref[...] + mix, g_ref[...], b_ref[...])


def _outproj(x, oa, ob, oc, od, wa, wb, wc, wd, g, b, tm):
    T, D = x.shape
    row = lambda a: pl.BlockSpec((tm, a.shape[1]), lambda i: (i, 0))
    full = lambda a: pl.BlockSpec(a.shape, lambda i: (0, 0))
    return pl.pallas_call(
        _outproj_kernel,
        grid=(T // tm,),
        in_specs=[row(x), row(oa), row(ob), row(oc), row(od),
                  full(wa), full(wb), full(wc), full(wd), full(g), full(b)],
        out_specs=row(x),
        out_shape=jax.ShapeDtypeStruct((T, D), F32),
        compiler_params=_params(("parallel",)),
        name="out_proj_ln",
    )(x, oa, ob, oc, od, wa, wb, wc, wd, g, b)


def _hgrn_tables(C):
    i = np.arange(C)[:, None]
    ip = np.arange(C)[None, :]
    seg = [(ip <= i),
           (ip > i)]
    masks = [np.eye(C, dtype=bool)]
    s = C // 2
    while s >= 1:
        blk = i // s
        seg.append(np.where(blk % 2 == 1, (ip > blk * s) & (ip <= i), (ip > i) & (ip <= (blk + 1) * s)))
        masks.append((blk % 2 == 1) & (ip // s == blk - 1))
        s //= 2
    seg = np.concatenate([x.astype(np.float32) for x in seg], axis=0)
    return seg, np.stack([m.astype(np.float32) for m in masks])


def _hgrn_kernel(q_ref, f_ref, i_ref, g_ref, llb_ref, l1m_ref, oml_ref, ng_ref,
                 seg_ref, msk_ref, o_ref, st_ref):
    @pl.when(pl.program_id(0) == 0)
    def _():
        st_ref[...] = jnp.zeros_like(st_ref)

    C = q_ref.shape[1]
    nlev = msk_ref.shape[0] - 1
    for bi in range(q_ref.shape[0]):
        q = _silu(q_ref[bi])
        z = f_ref[bi]
        log_sig = jnp.minimum(z, 0.0) - jnp.log1p(jnp.exp(-jnp.abs(z)))
        cc = l1m_ref[...] + log_sig
        llb = llb_ref[...]
        logf = jnp.maximum(llb, cc) + jnp.log1p(jnp.exp(-jnp.abs(llb - cc)))
        k = oml_ref[...] * _sigmoid(-z)
        v = i_ref[bi]
        seg = _exact_left_dot(seg_ref[...], logf)
        outs = []
        for h in range(HG_HEADS):
            sl = slice(h * LANES, (h + 1) * LANES)
            qh, kh, vh = q[:, sl], k[:, sl], v[:, sl]
            a = msk_ref[0] * _dot_nt(qh.astype(BF16), kh.astype(BF16))
            for l in range(nlev):
                dec = jnp.exp(seg[(2 + l) * C:(3 + l) * C, sl])
                a = a + msk_ref[1 + l] * _dot_nt((qh * dec).astype(BF16), (kh * dec).astype(BF16))
            b = seg[0:C, sl]
            st = st_ref[bi, h]
            o = _dot(a.astype(BF16), vh.astype(BF16))
            o = o + _dot_nt((qh * jnp.exp(b)).astype(BF16), st.astype(BF16))
            kd = (kh * jnp.exp(seg[C:2 * C, sl])).astype(BF16)
            st_ref[bi, h] = st * jnp.exp(b[C - 1:C, :]) + _dot(vh.T.astype(BF16), kd)
            outs.append(o * lax.rsqrt(jnp.mean(o * o, axis=-1, keepdims=True) + 1e-6))
        o = jnp.concatenate(outs, axis=1)
        o_ref[bi] = (o * ng_ref[...] * _silu(g_ref[bi])).astype(o_ref.dtype)


def _hgrn(proj, B, S, llb, l1m, oml, ng):
    C = CHUNK
    nc = S // C
    seg, msk = _hgrn_tables(C)
    seg, msk = jnp.asarray(seg, BF16), jnp.asarray(msk)
    proj3 = proj.reshape(B, S, proj.shape[1])
    col = lambda name: pl.BlockSpec((B, C, GROUP_W), lambda c, n=COL[name] // 4: (0, c, n))
    vec = pl.BlockSpec((1, GROUP_W), lambda c: (0, 0))
    full2 = lambda a: pl.BlockSpec(a.shape, lambda c: (0, 0))
    return pl.pallas_call(
        _hgrn_kernel,
        grid=(nc,),
        in_specs=[col("hq"), col("hf"), col("hi"), col("hg"), vec, vec, vec, vec,
                  full2(seg), pl.BlockSpec(msk.shape, lambda c: (0, 0, 0))],
        out_specs=pl.BlockSpec((B, C, GROUP_W), lambda c: (0, c, 0)),
        out_shape=jax.ShapeDtypeStruct((B, S, GROUP_W), BF16),
        scratch_shapes=[pltpu.VMEM((B, HG_HEADS, LANES, LANES), F32)],
        compiler_params=_params(("arbitrary",)),
        name="hgrn2",
    )(proj3, proj3, proj3, proj3, llb, l1m, oml, ng, seg, msk).reshape(B * S, GROUP_W)


def _ssd_kernel(z_ref, xbc_ref, sm_ref, cw_ref, cb_ref, dtb_ref, aneg_ref, dsk_ref, ng_ref,
                ex_ref, o_ref, tail_ref, st_ref):
    @pl.when(pl.program_id(1) == 0)
    def _():
        tail_ref[...] = jnp.zeros_like(tail_ref)
        st_ref[...] = jnp.zeros_like(st_ref)

    L = xbc_ref.shape[0]
    x = xbc_ref[...]
    xe = jnp.concatenate([tail_ref[...], x], axis=0)
    cw = cw_ref[...]
    conv = cb_ref[...]
    for kk in range(SSM_CONV):
        conv = conv + cw[kk:kk + 1, :] * xe[5 + kk:5 + kk + L, :]
    tail_ref[...] = x[L - 8:L, :]
    conv = _silu(conv)
    xs = conv[:, 0:GROUP_W]
    bm = conv[:, GROUP_W:GROUP_W + 256]
    cm = conv[:, GROUP_W + 256:GROUP_W + 512]

    dtf = _softplus(sm_ref[...] + dtb_ref[...])
    la = dtf * aneg_ref[...]
    ri = lax.broadcasted_iota(jnp.int32, (L, L), 0)
    ci = lax.broadcasted_iota(jnp.int32, (L, L), 1)
    tri = ri >= ci
    bfull = _exact_left_dot(tri.astype(BF16), la)
    ex = ex_ref[...]
    bexp = _exact_right_dot(bfull, ex)
    dtexp = _exact_right_dot(dtf, ex)
    b_t = bfull.T
    xdt = xs * dtexp
    lane = lax.broadcasted_iota(jnp.int32, (L, LANES), 1)

    scores = []
    for g in range(SSM_GROUPS):
        cg = cm[:, g * SSM_STATE:(g + 1) * SSM_STATE].astype(BF16)
        bg = bm[:, g * SSM_STATE:(g + 1) * SSM_STATE].astype(BF16)
        cb = _dot_nt(cg, bg)
        for hh in range(SSM_HEADS // SSM_GROUPS):
            h = g * (SSM_HEADS // SSM_GROUPS) + hh
            bcol = bfull[:, DT_LANE0 + h:DT_LANE0 + h + 1]
            brow = b_t[DT_LANE0 + h:DT_LANE0 + h + 1, :]
            dec = jnp.exp(jnp.where(tri, bcol - brow, NEG))
            scores.append((cb * dec).astype(BF16))
    y_pairs = []
    for u in range(SSM_HEADS // 2):
        slab = xdt[:, u * LANES:(u + 1) * LANES]
        lo = jnp.where(lane < SSM_HEAD_DIM, slab, 0.0).astype(BF16)
        hi = jnp.where(lane >= SSM_HEAD_DIM, slab, 0.0).astype(BF16)
        y_pairs.append(_dot(scores[2 * u], lo) + _dot(scores[2 * u + 1], hi))
    y_intra = jnp.concatenate(y_pairs, axis=1)

    blast = bexp[L - 1:L, :]
    w = (xdt * jnp.exp(blast - bexp)).astype(BF16)
    y_inter = []
    for g in range(SSM_GROUPS):
        gs = slice(g * 256, (g + 1) * 256)
        cg = cm[:, g * SSM_STATE:(g + 1) * SSM_STATE].astype(BF16)
        st = st_ref[g]
        y_inter.append(_dot(cg, st.astype(BF16)))
        bg_t = bm[:, g * SSM_STATE:(g + 1) * SSM_STATE].T.astype(BF16)
        st_ref[g] = st * jnp.exp(blast[:, gs]) + _dot(bg_t, w[:, gs])
    y = y_intra + jnp.concatenate(y_inter, axis=1) * jnp.exp(bexp) + dsk_ref[...] * xs
    y = y * _silu(z_ref[...])
    halves = []
    for g in range(SSM_GROUPS):
        seg = y[:, g * 256:(g + 1) * 256]
        halves.append(seg * lax.rsqrt(jnp.mean(seg * seg, axis=-1, keepdims=True) + 1e-6))
    o_ref[...] = (jnp.concatenate(halves, axis=1) * ng_ref[...]).astype(o_ref.dtype)


def _ssd(proj, small, B, S, cw, cb, dtb, aneg, dsk, ng):
    L = CHUNK
    nc = S // L
    ex = np.zeros((LANES, GROUP_W), np.float32)
    for h in range(SSM_HEADS):
        ex[DT_LANE0 + h, h * SSM_HEAD_DIM:(h + 1) * SSM_HEAD_DIM] = 1.0
    ex = jnp.asarray(ex, BF16)
    full2 = lambda a: pl.BlockSpec(a.shape, lambda b, c: (0, 0))
    return pl.pallas_call(
        _ssd_kernel,
        grid=(B, nc),
        in_specs=[
            pl.BlockSpec((L, GROUP_W), lambda b, c: (b * nc + c, COL["sz"] // 4)),
            pl.BlockSpec((L, 1024), lambda b, c: (b * nc + c, COL["sxbc"] // 8)),
            pl.BlockSpec((L, LANES), lambda b, c: (b * nc + c, 0)),
            full2(cw), full2(cb), full2(dtb), full2(aneg), full2(dsk), full2(ng), full2(ex)],
        out_specs=pl.BlockSpec((L, GROUP_W), lambda b, c: (b * nc + c, 0)),
        out_shape=jax.ShapeDtypeStruct((B * S, GROUP_W), BF16),
        scratch_shapes=[pltpu.VMEM((8, 1024), F32), pltpu.VMEM((SSM_GROUPS, SSM_STATE, 256), F32)],
        compiler_params=_params(("parallel", "arbitrary")),
        name="ssd",
    )(proj, proj, small, cw, cb, dtb, aneg, dsk, ng, ex)


def _ret_kernel(q_ref, k_ref, v_ref, g_ref, cos_ref, sin_ref, dec_ref, qs_ref, ks_ref, sd_ref,
                o_ref, st_ref):
    @pl.when(pl.program_id(0) == 0)
    def _():
        st_ref[...] = jnp.zeros_like(st_ref)

    cos = cos_ref[...]
    sin = sin_ref[...]
    for b in range(q_ref.shape[0]):
        outs = []
        for h in range(RET_HEADS):
            sl = slice(h * LANES, (h + 1) * LANES)
            qh = q_ref[b, :, sl]
            kh = k_ref[b, :, sl]
            qh = qh * cos + pltpu.roll(qh, RET_DK // 2, axis=1) * sin
            kh = (kh * cos + pltpu.roll(kh, RET_DK // 2, axis=1) * sin) * (RET_DK ** -0.5)
            vh = v_ref[b, :, sl].astype(BF16)
            sc = (_dot_nt(qh.astype(BF16), kh.astype(BF16)) * dec_ref[h]).astype(BF16)
            st = st_ref[b, h]
            y = _dot(sc, vh) + _dot((qh * qs_ref[:, sl]).astype(BF16), st.astype(BF16))
            kd_t = (kh * ks_ref[:, sl]).T.astype(BF16)
            st_ref[b, h] = st * sd_ref[h] + _dot(kd_t, vh)
            mu = jnp.mean(y, axis=-1, keepdims=True)
            d = y - mu
            outs.append(d * lax.rsqrt(jnp.mean(d * d, axis=-1, keepdims=True) + 1e-5))
        o_ref[b] = (_silu(g_ref[b]) * jnp.concatenate(outs, axis=1)).astype(o_ref.dtype)


def _retention(proj, B, S, cos_t, sin_t):
    L = CHUNK
    nc = S // L
    lg = jnp.log(1.0 - 2.0 ** (-5.0 - jnp.arange(RET_HEADS, dtype=F32)))
    i = jnp.arange(L, dtype=F32)
    diff = i[:, None] - i[None, :]
    dec = jnp.where(diff >= 0, jnp.exp(lg[:, None, None] * jnp.maximum(diff, 0.0)), 0.0)
    rep = lambda t: jnp.repeat(t, LANES, axis=1)
    qs = rep(jnp.exp((i[:, None] + 1.0) * lg[None, :]))
    ks = rep(jnp.exp((L - 1.0 - i[:, None]) * lg[None, :]))
    sd = jnp.broadcast_to(jnp.exp(L * lg)[:, None, None], (RET_HEADS, LANES, LANES))
    proj3 = proj.reshape(B, S, proj.shape[1])
    col = lambda name: pl.BlockSpec((B, L, GROUP_W), lambda c, n=COL[name] // 4: (0, c, n))
    return pl.pallas_call(
        _ret_kernel,
        grid=(nc,),
        in_specs=[col("rq"), col("rk"), col("rv"), col("rg"),
                  pl.BlockSpec((L, LANES), lambda c: (c, 0)),
                  pl.BlockSpec((L, LANES), lambda c: (c, 0)),
                  pl.BlockSpec((RET_HEADS, L, L), lambda c: (0, 0, 0)),
                  pl.BlockSpec((L, GROUP_W), lambda c: (0, 0)),
                  pl.BlockSpec((L, GROUP_W), lambda c: (0, 0)),
                  pl.BlockSpec((RET_HEADS, LANES, LANES), lambda c: (0, 0, 0))],
        out_specs=pl.BlockSpec((B, L, GROUP_W), lambda c: (0, c, 0)),
        out_shape=jax.ShapeDtypeStruct((B, S, GROUP_W), BF16),
        scratch_shapes=[pltpu.VMEM((B, RET_HEADS, RET_DK, RET_DK), F32)],
        compiler_params=_params(("arbitrary",)),
        name="retention",
    )(proj3, proj3, proj3, proj3, cos_t, sin_t, dec, qs, ks, sd).reshape(B * S, GROUP_W)


def _t5_bucket(dist):
    n = jnp.maximum(dist, 0)
    nf = jnp.maximum(n, 1).astype(F32)
    large = REL_EXACT + (jnp.log(nf / REL_EXACT) / math.log(REL_MAX_DIST / REL_EXACT)
                         * (REL_BUCKETS - REL_EXACT)).astype(jnp.int32)
    return jnp.where(n < REL_EXACT, n, jnp.minimum(large, REL_BUCKETS - 1))


def _head_bias(bucket, rel_ref):
    rows, cols = bucket.shape
    per_head = []
    for h in range(NSA_HEADS):
        tbl = jnp.broadcast_to(rel_ref[h:h + 1, :], (rows, LANES))
        chunks = [jnp.take_along_axis(tbl, bucket[:, c:c + LANES], axis=1)
                  for c in range(0, cols, LANES)]
        per_head.append(chunks[0] if len(chunks) == 1 else jnp.concatenate(chunks, axis=1))
    return jnp.stack(per_head, axis=0)


def _stack_heads(qw):
    return jnp.concatenate([qw[:, h * LANES:(h + 1) * LANES] for h in range(NSA_HEADS)],
                           axis=0).astype(BF16)


def _cmp_kernel(g_ref, pe_ref, w1a_ref, w1b_ref, w2_ref, o_ref):
    nb = g_ref.shape[0]
    gw = g_ref.shape[1] // 4
    pe = pe_ref[...]
    slabs = [g_ref[:, s * gw:(s + 1) * gw] for s in range(4)]
    nxt0 = pltpu.roll(slabs[0], nb - 1, axis=0)
    for s in range(4):
        a = (slabs[s] + pe[0:1, :]).astype(BF16)
        bn = ((slabs[s + 1] if s < 3 else nxt0) + pe[1:2, :]).astype(BF16)
        hid = _silu(_dot(a, w1a_ref[...]) + _dot(bn, w1b_ref[...]))
        o_ref[s * nb:(s + 1) * nb, :] = _dot(hid.astype(BF16), w2_ref[...]).astype(o_ref.dtype)


def _compress(g, pe2, w1a, w1b, w2bd):
    B, nb, gw4 = g.shape
    full2 = lambda a: pl.BlockSpec(a.shape, lambda b: (0, 0))
    slab_major = pl.pallas_call(
        _cmp_kernel,
        grid=(B,),
        in_specs=[pl.BlockSpec((None, nb, gw4), lambda b: (b, 0, 0)),
                  full2(pe2), full2(w1a), full2(w1b), full2(w2bd)],
        out_specs=pl.BlockSpec((None, 4 * nb, LANES), lambda b: (b, 0, 0)),
        out_shape=jax.ShapeDtypeStruct((B, 4 * nb, LANES), BF16),
        compiler_params=_params(("parallel",)),
        name="nsa_compress",
    )(g, pe2, w1a, w1b, w2bd)
    return slab_major.reshape(B, 4, nb, LANES).transpose(0, 2, 1, 3).reshape(B, 4 * nb, LANES)


def _cmpattn_kernel(q_ref, kc_ref, vc_ref, band_ref, st_ref, oc_ref, selb_ref):
    tq = q_ref.shape[0]
    ncmp = kc_ref.shape[0]
    nb = ncmp // 4
    R = NSA_HEADS * tq
    q0 = pl.program_id(1) * tq
    nd = band_ref.shape[0]
    Q = _stack_heads(q_ref[...])
    s3 = _dot_nt(Q, kc_ref[...]).reshape(NSA_HEADS, tq, ncmp)
    parts = []
    for ch in range(ncmp // LANES):
        d = jnp.clip((q0 - ch * LANES * NSA_CMP_STRIDE) // tq + CMP_BAND_OFF, 0, nd - 1)
        parts.append(s3[:, :, ch * LANES:(ch + 1) * LANES] + band_ref[d])
    s3 = jnp.concatenate(parts, axis=2)
    mx = jnp.max(s3, axis=-1, keepdims=True)
    e = jnp.exp(s3 - mx)
    live = (q0 + lax.broadcasted_iota(jnp.int32, (tq, 1), 0) >= NSA_CMP_BLOCK - 1)[None]
    p = e * jnp.where(live, 1.0 / jnp.sum(e, axis=-1, keepdims=True), 0.0)
    oc_ref[...] = _dot(p.reshape(R, ncmp).astype(BF16), vc_ref[...])

    ps = p.reshape(NSA_KV, NSA_GROUP, tq, ncmp).sum(axis=1).reshape(NSA_KV * tq, ncmp)
    imp = _exact_right_dot(ps, st_ref[...])
    j = lax.broadcasted_iota(jnp.int32, (NSA_KV * tq, nb), 1)
    t = q0 + (lax.broadcasted_iota(jnp.int32, (NSA_KV * tq, nb), 0) % tq)
    cur = t // NSA_SLC_BLOCK
    forced = (j == 0) | (j == cur) | (j == cur - 1)
    score = jnp.where(j > cur, -1.0, jnp.where(forced, NSA_GROUP + 1.0, imp))
    score = score.T
    jf = lax.broadcasted_iota(jnp.int32, score.shape, 0).astype(F32)
    sel = score == NSA_GROUP + 1.0
    score = jnp.where(sel, -jnp.inf, score)
    for _ in range(min(NSA_TOP_N, nb) - 3):
        best = jnp.max(score, axis=0, keepdims=True)
        first = jnp.min(jnp.where(score == best, jf, float(nb)), axis=0, keepdims=True)
        hit = jf == first
        sel = sel | hit
        score = jnp.where(hit, -jnp.inf, score)
    selb = jnp.where(sel, 0.0, NEG).T.astype(selb_ref.dtype)
    selb_ref[:, 0:nb] = selb[0:tq]
    selb_ref[:, nb:2 * nb] = selb[tq:2 * tq]


def _cmpattn(proj, kcmp, vcmp, cband, B, S):
    tq = TQ
    nqt = S // tq
    ncmp = kcmp.shape[1]
    nb = ncmp // 4
    off = np.arange(ncmp)[:, None] - 4 * np.arange(nb)[None, :]
    stencil = np.where((off >= 0) & (off <= 2), 1.0, np.where((off == -1) | (off == 3), 0.5, 0.0))
    stencil = jnp.asarray(stencil, BF16)
    once = pl.Buffered(1)
    return pl.pallas_call(
        _cmpattn_kernel,
        grid=(B, nqt),
        in_specs=[pl.BlockSpec((tq, 1024), lambda b, i: (b * nqt + i, COL["nqw"] // 8)),
                  pl.BlockSpec((None, ncmp, LANES), lambda b, i: (b, 0, 0)),
                  pl.BlockSpec((None, ncmp, LANES), lambda b, i: (b, 0, 0)),
                  pl.BlockSpec(cband.shape, lambda b, i: (0, 0, 0, 0), pipeline_mode=once),
                  pl.BlockSpec(stencil.shape, lambda b, i: (0, 0))],
        out_specs=[pl.BlockSpec((NSA_HEADS * tq, LANES), lambda b, i: (b * nqt + i, 0)),
                   pl.BlockSpec((tq, 2 * nb), lambda b, i: (b * nqt + i, 0))],
        out_shape=[jax.ShapeDtypeStruct((B * S * NSA_HEADS, LANES), F32),
                   jax.ShapeDtypeStruct((B * S, 2 * nb), BF16)],
        compiler_params=_params(("parallel", "arbitrary")),
        name="nsa_cmp_attn_topk",
    )(proj, kcmp, vcmp, cband, stencil)


def _band_kernel(rel_ref, o_ref, *, entry_off, key_step, key_end):
    tq = o_ref.shape[1]
    delta = (pl.program_id(0) - entry_off) * tq
    row = lax.broadcasted_iota(jnp.int32, (tq, LANES), 0)
    col = lax.broadcasted_iota(jnp.int32, (tq, LANES), 1)
    dist = delta + row - (col * key_step + key_end)
    bias = _head_bias(_t5_bucket(dist), rel_ref)
    o_ref[...] = jnp.where((dist >= 0)[None], bias, NEG)


def _band_table(rel_t, tq, entry_off, key_step=1, key_end=0):
    nd = -(-(REL_MAX_DIST + key_step * (LANES - 1) + key_end) // tq) + entry_off + 1
    return pl.pallas_call(
        functools.partial(_band_kernel, entry_off=entry_off, key_step=key_step, key_end=key_end),
        grid=(nd,),
        in_specs=[pl.BlockSpec(rel_t.shape, lambda d: (0, 0))],
        out_specs=pl.BlockSpec((None, NSA_HEADS, tq, LANES), lambda d: (d, 0, 0, 0)),
        out_shape=jax.ShapeDtypeStruct((nd, NSA_HEADS, tq, LANES), F32),
        compiler_params=_params(("parallel",)),
        name="nsa_bias_band",
    )(rel_t)


def _selattn_kernel(q_ref, selb_ref, ks_ref, vs_ref, band_ref, sp_ref, farq_ref, os_ref,
                    qaug_ref, msel_ref, s_ref, m_ref, acc_ref):
    tq = q_ref.shape[0]
    R = NSA_HEADS * tq
    nb = selb_ref.shape[1] // 2
    n_kt = msel_ref.shape[0]
    tk = ks_ref.shape[0] // n_kt
    nd = band_ref.shape[0]
    q0 = pl.program_id(1) * tq
    qaug_ref[:, 0:LANES] = _stack_heads(q_ref[...])
    selb2 = jnp.concatenate([selb_ref[:, 0:nb], selb_ref[:, nb:2 * nb]], axis=0)
    spread = _dot(selb2, sp_ref[...])
    for c in range(n_kt):
        msel_ref[c] = spread[:, c * LANES:(c + 1) * LANES].astype(BF16)
    m_ref[...] = jnp.full(m_ref.shape, NEG, F32)
    acc_ref[...] = jnp.zeros_like(acc_ref)
    n_tiles = (q0 + tq - 1) // tk + 1

    far_pairs = (jnp.maximum(q0 - (REL_MAX_DIST - 1), 0) // tk) // 2

    def scores(c, slot):
        c = jnp.minimum(c, n_kt - 1)
        m2 = msel_ref[c]
        flags = jnp.concatenate([m2[0:tq]] * NSA_GROUP + [m2[tq:2 * tq]] * NSA_GROUP, axis=0)
        qaug_ref[:, LANES:2 * LANES] = jnp.where(c < 2 * far_pairs, flags + farq_ref[...], flags)
        s_ref[slot] = _dot_nt(qaug_ref[...], ks_ref[pl.ds(pl.multiple_of(c * tk, tk), tk), :])

    def accumulate(c, slot, far):
        k0 = pl.multiple_of(c * tk, tk)
        s = s_ref[slot]
        if not far:
            s3 = s.reshape(NSA_HEADS, tq, tk)
            parts = []
            for ch in range(tk // LANES):
                d = jnp.clip((q0 - k0) // tq - ch * (LANES // tq) + BAND_OFF, 0, nd - 1)
                parts.append(s3[:, :, ch * LANES:(ch + 1) * LANES] + band_ref[d])
            s = jnp.concatenate(parts, axis=2).reshape(R, tk)
        m_old = m_ref[...]
        m_new = jnp.maximum(m_old, jnp.max(s, axis=-1, keepdims=True))
        alpha = jnp.exp(m_old - m_new)
        p = jnp.exp(s - jnp.tile(m_new, (1, tk // LANES)))
        acc_ref[...] = jnp.tile(alpha, (1, 2)) * acc_ref[...] + _dot(p.astype(BF16), vs_ref[pl.ds(k0, tk), :])
        m_ref[...] = m_new

    def pair(i, far):
        c = 2 * i
        scores(c + 1, 1)
        accumulate(c, 0, far)
        scores(c + 2, 0)
        accumulate(c + 1, 1, far)

    scores(0, 0)
    lax.fori_loop(0, far_pairs, lambda i, carry: pair(i, True), None)
    lax.fori_loop(far_pairs, n_tiles // 2, lambda i, carry: pair(i, False), None)

    @pl.when(n_tiles % 2 == 1)
    def _():
        accumulate(n_tiles - 1, 0, False)

    acc = acc_ref[...]
    os_ref[...] = acc[:, 0:LANES] / acc[:, LANES:2 * LANES]


def _selattn(proj, selb, ks, vs, band, farq, B, S):
    tq = TQ
    tk = min(TK, S)
    n_kt = S // tk
    bpt = tk // NSA_SLC_BLOCK
    nqt = S // tq
    nb = S // NSA_SLC_BLOCK
    R = NSA_HEADS * tq
    sp = np.zeros((nb, n_kt * LANES), np.float32)
    sp[np.arange(nb), (np.arange(nb) // bpt) * LANES + np.arange(nb) % bpt] = 1.0
    once = pl.Buffered(1)
    return pl.pallas_call(
        _selattn_kernel,
        grid=(B, nqt),
        in_specs=[pl.BlockSpec((tq, 1024), lambda b, i: (b * nqt + i, COL["nqw"] // 8)),
                  pl.BlockSpec((tq, 2 * nb), lambda b, i: (b * nqt + i, 0)),
                  pl.BlockSpec((None, S, 2 * LANES), lambda b, i: (b, 0, 0), pipeline_mode=once),
                  pl.BlockSpec((None, S, 2 * LANES), lambda b, i: (b, 0, 0), pipeline_mode=once),
                  pl.BlockSpec(band.shape, lambda b, i: (0, 0, 0, 0), pipeline_mode=once),
                  pl.BlockSpec(sp.shape, lambda b, i: (0, 0)),
                  pl.BlockSpec(farq.shape, lambda b, i: (0, 0))],
        out_specs=pl.BlockSpec((R, LANES), lambda b, i: (b * nqt + i, 0)),
        out_shape=jax.ShapeDtypeStruct((B * S * NSA_HEADS, LANES), F32),
        scratch_shapes=[pltpu.VMEM((R, 2 * LANES), BF16), pltpu.VMEM((n_kt, 2 * tq, LANES), BF16),
                        pltpu.VMEM((2, R, tk), F32), pltpu.VMEM((R, LANES), F32),
                        pltpu.VMEM((R, 2 * LANES), F32)],
        compiler_params=_params(("parallel", "arbitrary")),
        name="nsa_sel_attn",
    )(proj, selb, ks, vs, band, jnp.asarray(sp, BF16), farq)


def _winattn_kernel(q_ref, sm_ref, oc_ref, os_ref, kw_ref, vw_ref, band_ref, ng_ref, o_ref):
    tq = q_ref.shape[1]
    R = NSA_HEADS * tq
    S = kw_ref.shape[1]
    span = min(WIN_SPAN, S)
    nd = band_ref.shape[0]
    q0 = pl.program_id(0) * tq
    start = pl.multiple_of(jnp.clip(q0 + tq - span, 0, S - span), tq)
    row = lax.broadcasted_iota(jnp.int32, (tq, span), 0)
    col = lax.broadcasted_iota(jnp.int32, (tq, span), 1)
    in_window = ((q0 + row) - (start + col) < NSA_WINDOW)[None]
    lane = lax.broadcasted_iota(jnp.int32, (tq, LANES), 1)
    for b in range(q_ref.shape[0]):
        Q = _stack_heads(q_ref[b])
        kt = kw_ref[b, pl.ds(start, span), :]
        vt = vw_ref[b, pl.ds(start, span), :]
        s3 = _dot_nt(Q, kt).reshape(NSA_HEADS, tq, span)
        parts = []
        for ch in range(span // LANES):
            d = jnp.clip((q0 - start) // tq - ch * (LANES // tq) + BAND_OFF, 0, nd - 1)
            parts.append(s3[:, :, ch * LANES:(ch + 1) * LANES] + band_ref[d])
        s3 = jnp.where(in_window, jnp.concatenate(parts, axis=2), NEG)
        mx = jnp.max(s3, axis=-1, keepdims=True)
        e = jnp.exp(s3 - mx)
        p = e * (1.0 / jnp.sum(e, axis=-1, keepdims=True))
        ow = _dot(p.reshape(R, span).astype(BF16), vt)

        gates = _sigmoid(sm_ref[b])
        heads = []
        ssq = jnp.zeros((tq, 1), F32)
        for h in range(NSA_HEADS):
            rs = slice(h * tq, (h + 1) * tq)
            g = [gates[:, GATE_LANE0 + 3 * h + br:GATE_LANE0 + 3 * h + br + 1] for br in range(3)]
            oh = g[0] * oc_ref[b, rs, :] + g[1] * os_ref[b, rs, :] + g[2] * ow[rs, :]
            kv = h // NSA_GROUP
            valid = (lane >= kv * NSA_HEAD_DIM) & (lane < (kv + 1) * NSA_HEAD_DIM)
            oh = jnp.where(valid, oh, 0.0)
            ssq = ssq + jnp.sum(oh * oh, axis=-1, keepdims=True)
            heads.append(oh)
        rinv = lax.rsqrt(ssq / GROUP_W + 1e-6)
        o_ref[b] = (jnp.concatenate(heads, axis=1) * rinv * ng_ref[...]).astype(o_ref.dtype)


def _winattn(proj, small, oc, os_, kw, vw, band, ngw, B, S):
    tq = TQ
    R = NSA_HEADS * tq
    once = pl.Buffered(1)
    per_b = lambda t: t.reshape(B, t.shape[0] // B, t.shape[1])
    return pl.pallas_call(
        _winattn_kernel,
        grid=(S // tq,),
        in_specs=[pl.BlockSpec((B, tq, 1024), lambda i: (0, i, COL["nqw"] // 8)),
                  pl.BlockSpec((B, tq, LANES), lambda i: (0, i, 0)),
                  pl.BlockSpec((B, R, LANES), lambda i: (0, i, 0)),
                  pl.BlockSpec((B, R, LANES), lambda i: (0, i, 0)),
                  pl.BlockSpec((B, S, LANES), lambda i: (0, 0, 0), pipeline_mode=once),
                  pl.BlockSpec((B, S, LANES), lambda i: (0, 0, 0), pipeline_mode=once),
                  pl.BlockSpec(band.shape, lambda i: (0, 0, 0, 0), pipeline_mode=once),
                  pl.BlockSpec(ngw.shape, lambda i: (0, 0))],
        out_specs=pl.BlockSpec((B, tq, NSA_HEADS * LANES), lambda i: (0, i, 0)),
        out_shape=jax.ShapeDtypeStruct((B, S, NSA_HEADS * LANES), BF16),
        compiler_params=_params(("arbitrary",)),
        name="nsa_win_attn_merge",
    )(per_b(proj), per_b(small), per_b(oc), per_b(os_), kw, vw, band, ngw).reshape(B * S, NSA_HEADS * LANES)


def _widen_heads(x, axis):
    x = jnp.moveaxis(x, axis, -1)
    lead = x.shape[:-1]
    x = x.reshape(*lead, NSA_KV, NSA_GROUP, 1, NSA_HEAD_DIM)
    sel = jnp.eye(NSA_KV, dtype=x.dtype).reshape(NSA_KV, 1, NSA_KV, 1)
    x = (x * sel).reshape(*lead, NSA_HEADS * LANES)
    return jnp.moveaxis(x, -1, axis)


def _build_w_in(w):
    (hq, hf, hi, hg, nq, nkc, nvc, nks, nvs, nkw, nvw, ngate,
     sz, sxbc, sdt, rq, rk, rv, rg) = jnp.split(w, IN_SPLITS, axis=1)
    D = w.shape[0]
    nqw = _widen_heads(nq * NSA_HEAD_DIM ** -0.5, 1)
    deint = lambda t: t.reshape(D, RET_HEADS, RET_DK // 2, 2).transpose(0, 1, 3, 2).reshape(D, GROUP_W)
    small = jnp.concatenate([ngate, sdt, jnp.zeros((D, LANES - 32), w.dtype)], axis=1)
    wide = [hq, hf, hi, hg, nqw, sxbc, sz, deint(rq), deint(rk), rv, rg]
    narrow = [nkc, nvc, nks, nvs, nkw, nvw, small]
    return jnp.concatenate(wide, axis=1).astype(BF16), jnp.concatenate(narrow, axis=1).astype(BF16)


def _build_cmp_weights(pe, w1, w2):
    w1r = w1.reshape(2, NSA_CMP_STRIDE, NSA_HEAD_DIM, NSA_CMP_HIDDEN)
    eye = jnp.eye(NSA_KV, dtype=w1.dtype)
    big = jnp.einsum("ardc,kj->arkdjc", w1r, eye).reshape(
        2, NSA_CMP_STRIDE * NSA_KV * NSA_HEAD_DIM, NSA_KV * NSA_CMP_HIDDEN)
    w2bd = jnp.einsum("cd,kj->kcjd", w2, eye).reshape(NSA_KV * NSA_CMP_HIDDEN, NSA_KV * NSA_HEAD_DIM)
    per = pe.reshape(2, NSA_CMP_STRIDE, 1, NSA_HEAD_DIM)
    pe2 = jnp.broadcast_to(per, (2, NSA_CMP_STRIDE, NSA_KV, NSA_HEAD_DIM)).reshape(2, -1)
    return pe2, big[0].astype(BF16), big[1].astype(BF16), w2bd.astype(BF16)


def _rotary_tables(S):
    half = RET_DK // 2
    theta = 1.0 / (10000.0 ** jnp.linspace(0.0, 1.0, half, dtype=F32))
    ang = jnp.arange(S, dtype=F32)[:, None] * theta[None, :]
    cos, sin = jnp.cos(ang), jnp.sin(ang)
    return jnp.concatenate([cos, cos], axis=1), jnp.concatenate([-sin, sin], axis=1)


def _mixer(x2, B, S, l, p, lower_bounds, band, cband, farq, cos_t, sin_t):
    T = B * S
    w_wide, w_narrow = _build_w_in(p["w_in"][l])
    tn = NCOL * LANES // 4
    proj = _proj(x2, w_wide.reshape(-1, 4, tn).transpose(1, 0, 2), tm=min(1024, T))
    kc, vc, ks, vs, kw, vw, small = _kvproj(x2, w_narrow, tm=min(512, T))
    row = lambda v: v.reshape(1, -1).astype(F32)

    lb = lower_bounds[l].astype(F32)
    o_a = _hgrn(proj, B, S, row(jnp.log(lb)), row(jnp.log1p(-lb)), row(1.0 - lb),
                row(p["hgrn_norm_g"][l]))

    nb = S // NSA_SLC_BLOCK
    grp = lambda t: t.reshape(B, nb, 4 * NSA_CMP_STRIDE * LANES)
    kcmp = _compress(grp(kc), *_build_cmp_weights(p["nsa_pe_k"][l], p["nsa_w1_k"][l], p["nsa_w2_k"][l]))
    vcmp = _compress(grp(vc), *_build_cmp_weights(p["nsa_pe_v"][l], p["nsa_w1_v"][l], p["nsa_w2_v"][l]))
    o_cmp, selb = _cmpattn(proj, kcmp, vcmp, cband, B, S)
    seq = lambda t: t.reshape(B, S, t.shape[1])
    o_sel = _selattn(proj, selb, seq(ks), seq(vs), band, farq, B, S)
    ngw = _widen_heads(p["nsa_norm_g"][l].astype(F32), 0).reshape(1, -1)
    o_b = _winattn(proj, small, o_cmp, o_sel, seq(kw), seq(vw), band, ngw, B, S)

    lane_vec = lambda v: jnp.zeros((1, LANES), F32).at[0, DT_LANE0:DT_LANE0 + SSM_HEADS].set(v.astype(F32))
    o_c = _ssd(proj, small, B, S, p["ssm_conv_w"][l].astype(F32), row(p["ssm_conv_b"][l]),
               lane_vec(p["ssm_dt_bias"][l]), lane_vec(-jnp.exp(p["ssm_a_log"][l].astype(F32))),
               row(jnp.repeat(p["ssm_d"][l].astype(F32), SSM_HEAD_DIM)), row(p["ssm_norm_g"][l]))

    o_d = _retention(proj, B, S, cos_t, sin_t)

    w_out = p["w_out"][l]
    wa, wb, wc, wd = (w_out[i * GROUP_W:(i + 1) * GROUP_W] for i in range(4))
    return o_a, o_b, o_c, o_d, wa.astype(BF16), _widen_heads(wb, 0).astype(BF16), wc.astype(BF16), wd.astype(BF16)


def kernel(x, ln1_g, ln1_b, ffn1_w1, ffn1_w3, ffn1_w2, ln2_g, ln2_b, w_in, w_out, hgrn_lb_logits, hgrn_norm_g, nsa_pe_k, nsa_w1_k, nsa_w2_k, nsa_pe_v, nsa_w1_v, nsa_w2_v, nsa_norm_g, rel_bias, ssm_conv_w, ssm_conv_b, ssm_dt_bias, ssm_a_log, ssm_d, ssm_norm_g, ln3_g, ln3_b, ffn2_w1, ffn2_w3, ffn2_w2):
    B, S, D = x.shape
    T = B * S
    depth = w_in.shape[0]
    p = dict(w_in=w_in, w_out=w_out, hgrn_norm_g=hgrn_norm_g, nsa_pe_k=nsa_pe_k, nsa_w1_k=nsa_w1_k,
             nsa_w2_k=nsa_w2_k, nsa_pe_v=nsa_pe_v, nsa_w1_v=nsa_w1_v, nsa_w2_v=nsa_w2_v,
             nsa_norm_g=nsa_norm_g, ssm_conv_w=ssm_conv_w, ssm_conv_b=ssm_conv_b,
             ssm_dt_bias=ssm_dt_bias, ssm_a_log=ssm_a_log, ssm_d=ssm_d, ssm_norm_g=ssm_norm_g)
    cum = jnp.cumsum(jax.nn.softmax(hgrn_lb_logits.astype(F32), axis=0), axis=0)
    lower_bounds = cum - cum[:1]
    rel_t = jnp.zeros((NSA_HEADS, LANES), F32).at[:, :REL_BUCKETS].set(rel_bias.astype(F32).T)
    band = _band_table(rel_t, TQ, BAND_OFF)
    cband = _band_table(rel_t, TQ, CMP_BAND_OFF, NSA_CMP_STRIDE, NSA_CMP_BLOCK - 1)
    bpt = TK // NSA_SLC_BLOCK
    farq = jnp.zeros((NSA_HEADS, LANES), BF16).at[:, bpt:bpt + 3].set(
        jnp.stack(_split3(rel_bias.astype(F32)[REL_BUCKETS - 1]), axis=1))
    farq = jnp.repeat(farq, TQ, axis=0)
    cos_t, sin_t = _rotary_tables(S)
    row = lambda v: v.reshape(1, -1).astype(F32)
    tm = min(512, T)
    tf = 512 if ffn1_w1.shape[2] % 512 == 0 else ffn1_w1.shape[2]
    x2 = x.reshape(T, D).astype(F32)
    for l in range(depth):
        x2 = _ffn(x2, _to_bf16_col_blocks(ffn1_w1, l, tf), _to_bf16_col_blocks(ffn1_w3, l, tf),
                  _to_bf16(ffn1_w2, l), row(ln1_g[l]), row(ln1_b[l]), tm)
        o_a, o_b, o_c, o_d, wa, wb, wc, wd = _mixer(x2, B, S, l, p, lower_bounds, band, cband, farq, cos_t, sin_t)
        x2 = _outproj(x2, o_a, o_b, o_c, o_d, wa, wb, wc, wd, row(ln2_g[l]), row(ln2_b[l]), min(512, T))
        x2 = _ffn(x2, _to_bf16_col_blocks(ffn2_w1, l, tf), _to_bf16_col_blocks(ffn2_w3, l, tf),
                  _to_bf16(ffn2_w2, l), row(ln3_g[l]), row(ln3_b[l]), tm)
    return x2.reshape(B, S, D).astype(x.dtype)
```

```python
import functools
import math

import numpy as np
import jax
import jax.numpy as jnp
from jax import lax
from jax.experimental import pallas as pl
from jax.experimental.pallas import tpu as pltpu

F32 = jnp.float32
BF16 = jnp.bfloat16

D_MODEL = 2048
DEPTH = 2
GROUP_W = 512
ALPHA = (2 * DEPTH) ** 0.25
HG_HEADS = 4
NSA_HEADS = 8
NSA_KV = 2
NSA_GROUP = 4
NSA_HEAD_DIM = 64
NSA_CMP_STRIDE = 16
NSA_CMP_BLOCK = 32
NSA_SLC_BLOCK = 64
NSA_TOP_N = 16
NSA_WINDOW = 512
NSA_CMP_HIDDEN = 256
SSM_HEADS = 8
SSM_HEAD_DIM = 64
SSM_GROUPS = 2
SSM_STATE = 128
SSM_CONV = 4
RET_HEADS = 4
RET_DK = 128
REL_BUCKETS = 32
REL_EXACT = 16
REL_MAX_DIST = 2048
IN_SIZES = ((GROUP_W,) * 4 + (GROUP_W,) + (128,) * 6 + (24,)
            + (GROUP_W, 1024, SSM_HEADS) + (GROUP_W,) * 4)
IN_SPLITS = tuple(int(v) for v in np.cumsum(IN_SIZES)[:-1])

LANES = 128
VMEM_LIMIT = 56 * 1024 * 1024
CAST_BLOCK_BYTES = 4 * 1024 * 1024

COL = dict(hq=0, hf=4, hi=8, hg=12, sxbc=16, nq=24, sz=28, rq=32, rk=36, rv=40, rg=44)
NCOL = 48
KV_COLS = ("nkc", "nvc", "nks", "nvs", "nkw", "nvw", "small")
GATE_LANE0 = 0
DT_LANE0 = 24

CHUNK = 128
TQ = 128
TK = 1024
BAND_OFF = 2
CMP_BAND_OFF = 1
WIN_SPAN = NSA_WINDOW + 2 * TQ
NEG = -1e30


def _params(sem):
    return pltpu.CompilerParams(dimension_semantics=sem, vmem_limit_bytes=VMEM_LIMIT)


def _dot(a, b):
    return jnp.dot(a, b, preferred_element_type=F32)


def _dot_nt(a, b):
    return lax.dot_general(a, b, (((1,), (1,)), ((), ())), preferred_element_type=F32)


def _split3(x):
    hi = x.astype(BF16)
    r1 = x - hi.astype(F32)
    mid = r1.astype(BF16)
    return hi, mid, (r1 - mid.astype(F32)).astype(BF16)


def _exact_left_dot(w, x):
    n = x.shape[1]
    y = _dot(w, jnp.concatenate(_split3(x), axis=1))
    return y[:, 0:n] + y[:, n:2 * n] + y[:, 2 * n:3 * n]


def _exact_right_dot(x, w):
    n = x.shape[0]
    y = _dot(jnp.concatenate(_split3(x), axis=0), w)
    return y[0:n] + y[n:2 * n] + y[2 * n:3 * n]


def _sigmoid(x):
    return 1.0 / (1.0 + jnp.exp(-x))


def _silu(x):
    return x * _sigmoid(x)


def _softplus(x):
    return jnp.maximum(x, 0.0) + jnp.log1p(jnp.exp(-jnp.abs(x)))


def _layer_norm(r, g, b):
    mu = jnp.mean(r, axis=-1, keepdims=True)
    d = r - mu
    var = jnp.mean(d * d, axis=-1, keepdims=True)
    return d * lax.rsqrt(var + 1e-5) * g + b


def _cast_kernel(x_ref, o_ref):
    o_ref[...] = x_ref[...].astype(o_ref.dtype)


def _to_bf16(w, l):
    _, r, c = w.shape
    tr = min(r, max(16, CAST_BLOCK_BYTES // (4 * c) // 16 * 16))
    while r % tr:
        tr -= 16
    return pl.pallas_call(
        _cast_kernel,
        grid=(r // tr,),
        in_specs=[pl.BlockSpec((None, tr, c), lambda i: (l, i, 0))],
        out_specs=pl.BlockSpec((tr, c), lambda i: (i, 0)),
        out_shape=jax.ShapeDtypeStruct((r, c), BF16),
        compiler_params=_params(("parallel",)),
        name="cast_bf16",
    )(w)


def _to_bf16_col_blocks(w, l, tc):
    _, r, c = w.shape
    return pl.pallas_call(
        _cast_kernel,
        grid=(c // tc,),
        in_specs=[pl.BlockSpec((None, r, tc), lambda j: (l, 0, j))],
        out_specs=pl.BlockSpec((None, r, tc), lambda j: (j, 0, 0)),
        out_shape=jax.ShapeDtypeStruct((c // tc, r, tc), BF16),
        compiler_params=_params(("parallel",)),
        name="cast_bf16_blocked",
    )(w)


def _ffn_kernel(x_ref, w1_ref, w3_ref, w2_ref, g_ref, b_ref, o_ref, acc_ref, xb_ref):
    j = pl.program_id(1)

    @pl.when(j == 0)
    def _():
        xb_ref[...] = x_ref[...].astype(BF16)
        acc_ref[...] = jnp.zeros_like(acc_ref)

    xb = xb_ref[...]
    h1 = _dot(xb, w1_ref[...])
    h3 = _dot(xb, w3_ref[...])
    a = (_silu(h1) * h3).astype(BF16)
    acc_ref[...] += _dot(a, w2_ref[...])

    @pl.when(j == pl.num_programs(1) - 1)
    def _():
        r = ALPHA * x_ref[...] + 0.5 * acc_ref[...]
        o_ref[...] = _layer_norm(r, g_ref[...], b_ref[...])


def _ffn(x, w1, w3, w2, g, b, tm):
    T, D = x.shape
    nf, _, tf = w1.shape
    return pl.pallas_call(
        _ffn_kernel,
        grid=(T // tm, nf),
        in_specs=[
            pl.BlockSpec((tm, D), lambda i, j: (i, 0)),
            pl.BlockSpec((None, D, tf), lambda i, j: (j, 0, 0)),
            pl.BlockSpec((None, D, tf), lambda i, j: (j, 0, 0)),
            pl.BlockSpec((tf, D), lambda i, j: (j, 0)),
            pl.BlockSpec((1, D), lambda i, j: (0, 0)),
            pl.BlockSpec((1, D), lambda i, j: (0, 0)),
        ],
        out_specs=pl.BlockSpec((tm, D), lambda i, j: (i, 0)),
        out_shape=jax.ShapeDtypeStruct((T, D), F32),
        scratch_shapes=[pltpu.VMEM((tm, D), F32), pltpu.VMEM((tm, D), BF16)],
        compiler_params=_params(("parallel", "arbitrary")),
        name="ffn_ln",
    )(x, w1, w3, w2, g, b)


def _proj_kernel(x_ref, w_ref, o_ref, xb_ref):
    @pl.when(pl.program_id(1) == 0)
    def _():
        xb_ref[...] = x_ref[...].astype(BF16)

    o_ref[...] = _dot(xb_ref[...], w_ref[...])


def _proj(x, w, tm):
    T, D = x.shape
    nn, _, tn = w.shape
    return pl.pallas_call(
        _proj_kernel,
        grid=(T // tm, nn),
        in_specs=[pl.BlockSpec((tm, D), lambda i, j: (i, 0)),
                  pl.BlockSpec((None, D, tn), lambda i, j: (j, 0, 0))],
        out_specs=pl.BlockSpec((tm, tn), lambda i, j: (i, j)),
        out_shape=jax.ShapeDtypeStruct((T, nn * tn), F32),
        scratch_shapes=[pltpu.VMEM((tm, D), BF16)],
        compiler_params=_params(("parallel", "arbitrary")),
        name="in_proj",
    )(x, w)


def _kvproj_kernel(x_ref, w_ref, kc_ref, vc_ref, ks_ref, vs_ref, kw_ref, vw_ref, sm_ref):
    tm = x_ref.shape[0]
    y = _dot(x_ref[...].astype(BF16), w_ref[...])
    piece = lambda n: y[:, n * LANES:(n + 1) * LANES]
    kc_ref[...] = piece(0)
    vc_ref[...] = piece(1)
    row = pl.program_id(0) * tm + lax.broadcasted_iota(jnp.int32, (tm, LANES), 0)
    lane = lax.broadcasted_iota(jnp.int32, (tm, LANES), 1)
    bpt = TK // NSA_SLC_BLOCK
    onehot = (((row // NSA_SLC_BLOCK) % bpt == lane) | ((lane >= bpt) & (lane < bpt + 3))).astype(BF16)
    ks_ref[...] = jnp.concatenate([piece(2).astype(BF16), onehot], axis=1)
    vs_ref[...] = jnp.concatenate([piece(3).astype(BF16), jnp.ones((tm, LANES), BF16)], axis=1)
    kw_ref[...] = piece(4).astype(BF16)
    vw_ref[...] = piece(5).astype(BF16)
    sm_ref[...] = piece(6)


def _kvproj(x, w, tm):
    T, D = x.shape
    narrow = lambda dt, width=LANES: (pl.BlockSpec((tm, width), lambda i: (i, 0)),
                                      jax.ShapeDtypeStruct((T, width), dt))
    outs = [narrow(F32), narrow(F32), narrow(BF16, 2 * LANES), narrow(BF16, 2 * LANES),
            narrow(BF16), narrow(BF16), narrow(F32)]
    return pl.pallas_call(
        _kvproj_kernel,
        grid=(T // tm,),
        in_specs=[pl.BlockSpec((tm, D), lambda i: (i, 0)), pl.BlockSpec(w.shape, lambda i: (0, 0))],
        out_specs=[o[0] for o in outs],
        out_shape=[o[1] for o in outs],
        compiler_params=_params(("parallel",)),
        name="kv_proj",
    )(x, w)


def _outproj_kernel(x_ref, oa_ref, ob_ref, oc_ref, od_ref, wa_ref, wb_ref, wc_ref, wd_ref,
                    g_ref, b_ref, o_ref):
    mix = (_dot(oa_ref[...], wa_ref[...]) + _dot(ob_ref[...], wb_ref[...])
           + _dot(oc_ref[...], wc_ref[...]) + _dot(od_ref[...], wd_ref[...]))
    o_ref[...] = _layer_norm(ALPHA * x_ref[...] + mix, g_ref[...], b_ref[...])


def _outproj(x, oa, ob, oc, od, wa, wb, wc, wd, g, b, tm):
    T, D = x.shape
    row = lambda a: pl.BlockSpec((tm, a.shape[1]), lambda i: (i, 0))
    full = lambda a: pl.BlockSpec(a.shape, lambda i: (0, 0))
    return pl.pallas_call(
        _outproj_kernel,
        grid=(T // tm,),
        in_specs=[row(x), row(oa), row(ob), row(oc), row(od),
                  full(wa), full(wb), full(wc), full(wd), full(g), full(b)],
        out_specs=row(x),
        out_shape=jax.ShapeDtypeStruct((T, D), F32),
        compiler_params=_params(("parallel",)),
        name="out_proj_ln",
    )(x, oa, ob, oc, od, wa, wb, wc, wd, g, b)


def _hgrn_tables(C):
    i = np.arange(C)[:, None]
    ip = np.arange(C)[None, :]
    seg = [(ip <= i),
           (ip > i)]
    masks = [np.eye(C, dtype=bool)]
    s = C // 2
    while s >= 1:
        blk = i // s
        seg.append(np.where(blk % 2 == 1, (ip > blk * s) & (ip <= i), (ip > i) & (ip <= (blk + 1) * s)))
        masks.append((blk % 2 == 1) & (ip // s == blk - 1))
        s //= 2
    seg = np.concatenate([x.astype(np.float32) for x in seg], axis=0)
    return seg, np.stack([m.astype(np.float32) for m in masks])


def _hgrn_kernel(q_ref, f_ref, i_ref, g_ref, llb_ref, l1m_ref, oml_ref, ng_ref,
                 seg_ref, msk_ref, o_ref, st_ref):
    @pl.when(pl.program_id(0) == 0)
    def _():
        st_ref[...] = jnp.zeros_like(st_ref)

    C = q_ref.shape[1]
    nlev = msk_ref.shape[0] - 1
    for bi in range(q_ref.shape[0]):
        q = _silu(q_ref[bi])
        z = f_ref[bi]
        log_sig = jnp.minimum(z, 0.0) - jnp.log1p(jnp.exp(-jnp.abs(z)))
        cc = l1m_ref[...] + log_sig
        llb = llb_ref[...]
        logf = jnp.maximum(llb, cc) + jnp.log1p(jnp.exp(-jnp.abs(llb - cc)))
        k = oml_ref[...] * _sigmoid(-z)
        v = i_ref[bi]
        seg = _exact_left_dot(seg_ref[...], logf)
        outs = []
        for h in range(HG_HEADS):
            sl = slice(h * LANES, (h + 1) * LANES)
            qh, kh, vh = q[:, sl], k[:, sl], v[:, sl]
            a = msk_ref[0] * _dot_nt(qh.astype(BF16), kh.astype(BF16))
            for l in range(nlev):
                dec = jnp.exp(seg[(2 + l) * C:(3 + l) * C, sl])
                a = a + msk_ref[1 + l] * _dot_nt((qh * dec).astype(BF16), (kh * dec).astype(BF16))
            b = seg[0:C, sl]
            st = st_ref[bi, h]
            o = _dot(a.astype(BF16), vh.astype(BF16))
            o = o + _dot_nt((qh * jnp.exp(b)).astype(BF16), st.astype(BF16))
            kd = (kh * jnp.exp(seg[C:2 * C, sl])).astype(BF16)
            st_ref[bi, h] = st * jnp.exp(b[C - 1:C, :]) + _dot(vh.T.astype(BF16), kd)
            outs.append(o * lax.rsqrt(jnp.mean(o * o, axis=-1, keepdims=True) + 1e-6))
        o = jnp.concatenate(outs, axis=1)
        o_ref[bi] = (o * ng_ref[...] * _silu(g_ref[bi])).astype(o_ref.dtype)


def _hgrn(proj, B, S, llb, l1m, oml, ng):
    C = CHUNK
    nc = S // C
    seg, msk = _hgrn_tables(C)
    seg, msk = jnp.asarray(seg, BF16), jnp.asarray(msk)
    proj3 = proj.reshape(B, S, proj.shape[1])
    col = lambda name: pl.BlockSpec((B, C, GROUP_W), lambda c, n=COL[name] // 4: (0, c, n))
    vec = pl.BlockSpec((1, GROUP_W), lambda c: (0, 0))
    full2 = lambda a: pl.BlockSpec(a.shape, lambda c: (0, 0))
    return pl.pallas_call(
        _hgrn_kernel,
        grid=(nc,),
        in_specs=[col("hq"), col("hf"), col("hi"), col("hg"), vec, vec, vec, vec,
                  full2(seg), pl.BlockSpec(msk.shape, lambda c: (0, 0, 0))],
        out_specs=pl.BlockSpec((B, C, GROUP_W), lambda c: (0, c, 0)),
        out_shape=jax.ShapeDtypeStruct((B, S, GROUP_W), BF16),
        scratch_shapes=[pltpu.VMEM((B, HG_HEADS, LANES, LANES), F32)],
        compiler_params=_params(("arbitrary",)),
        name="hgrn2",
    )(proj3, proj3, proj3, proj3, llb, l1m, oml, ng, seg, msk).reshape(B * S, GROUP_W)


def _ssd_kernel(z_ref, xbc_ref, sm_ref, cw_ref, cb_ref, dtb_ref, aneg_ref, dsk_ref, ng_ref,
                ex_ref, o_ref, tail_ref, st_ref):
    @pl.when(pl.program_id(1) == 0)
    def _():
        tail_ref[...] = jnp.zeros_like(tail_ref)
        st_ref[...] = jnp.zeros_like(st_ref)

    L = xbc_ref.shape[0]
    x = xbc_ref[...]
    xe = jnp.concatenate([tail_ref[...], x], axis=0)
    cw = cw_ref[...]
    conv = cb_ref[...]
    for kk in range(SSM_CONV):
        conv = conv + cw[kk:kk + 1, :] * xe[5 + kk:5 + kk + L, :]
    tail_ref[...] = x[L - 8:L, :]
    conv = _silu(conv)
    xs = conv[:, 0:GROUP_W]
    bm = conv[:, GROUP_W:GROUP_W + 256]
    cm = conv[:, GROUP_W + 256:GROUP_W + 512]

    dtf = _softplus(sm_ref[...] + dtb_ref[...])
    la = dtf * aneg_ref[...]
    ri = lax.broadcasted_iota(jnp.int32, (L, L), 0)
    ci = lax.broadcasted_iota(jnp.int32, (L, L), 1)
    tri = ri >= ci
    bfull = _exact_left_dot(tri.astype(BF16), la)
    ex = ex_ref[...]
    bexp = _exact_right_dot(bfull, ex)
    dtexp = _exact_right_dot(dtf, ex)
    b_t = bfull.T
    xdt = xs * dtexp
    lane = lax.broadcasted_iota(jnp.int32, (L, LANES), 1)

    scores = []
    for g in range(SSM_GROUPS):
        cg = cm[:, g * SSM_STATE:(g + 1) * SSM_STATE].astype(BF16)
        bg = bm[:, g * SSM_STATE:(g + 1) * SSM_STATE].astype(BF16)
        cb = _dot_nt(cg, bg)
        for hh in range(SSM_HEADS // SSM_GROUPS):
            h = g * (SSM_HEADS // SSM_GROUPS) + hh
            bcol = bfull[:, DT_LANE0 + h:DT_LANE0 + h + 1]
            brow = b_t[DT_LANE0 + h:DT_LANE0 + h + 1, :]
            dec = jnp.exp(jnp.where(tri, bcol - brow, NEG))
            scores.append((cb * dec).astype(BF16))
    y_pairs = []
    for u in range(SSM_HEADS // 2):
        slab = xdt[:, u * LANES:(u + 1) * LANES]
        lo = jnp.where(lane < SSM_HEAD_DIM, slab, 0.0).astype(BF16)
        hi = jnp.where(lane >= SSM_HEAD_DIM, slab, 0.0).astype(BF16)
        y_pairs.append(_dot(scores[2 * u], lo) + _dot(scores[2 * u + 1], hi))
    y_intra = jnp.concatenate(y_pairs, axis=1)

    blast = bexp[L - 1:L, :]
    w = (xdt * jnp.exp(blast - bexp)).astype(BF16)
    y_inter = []
    for g in range(SSM_GROUPS):
        gs = slice(g * 256, (g + 1) * 256)
        cg = cm[:, g * SSM_STATE:(g + 1) * SSM_STATE].astype(BF16)
        st = st_ref[g]
        y_inter.append(_dot(cg, st.astype(BF16)))
        bg_t = bm[:, g * SSM_STATE:(g + 1) * SSM_STATE].T.astype(BF16)
        st_ref[g] = st * jnp.exp(blast[:, gs]) + _dot(bg_t, w[:, gs])
    y = y_intra + jnp.concatenate(y_inter, axis=1) * jnp.exp(bexp) + dsk_ref[...] * xs
    y = y * _silu(z_ref[...])
    halves = []
    for g in range(SSM_GROUPS):
        seg = y[:, g * 256:(g + 1) * 256]
        halves.append(seg * lax.rsqrt(jnp.mean(seg * seg, axis=-1, keepdims=True) + 1e-6))
    o_ref[...] = (jnp.concatenate(halves, axis=1) * ng_ref[...]).astype(o_ref.dtype)


def _ssd(proj, small, B, S, cw, cb, dtb, aneg, dsk, ng):
    L = CHUNK
    nc = S // L
    ex = np.zeros((LANES, GROUP_W), np.float32)
    for h in range(SSM_HEADS):
        ex[DT_LANE0 + h, h * SSM_HEAD_DIM:(h + 1) * SSM_HEAD_DIM] = 1.0
    ex = jnp.asarray(ex, BF16)
    full2 = lambda a: pl.BlockSpec(a.shape, lambda b, c: (0, 0))
    return pl.pallas_call(
        _ssd_kernel,
        grid=(B, nc),
        in_specs=[
            pl.BlockSpec((L, GROUP_W), lambda b, c: (b * nc + c, COL["sz"] // 4)),
            pl.BlockSpec((L, 1024), lambda b, c: (b * nc + c, COL["sxbc"] // 8)),
            pl.BlockSpec((L, LANES), lambda b, c: (b * nc + c, 0)),
            full2(cw), full2(cb), full2(dtb), full2(aneg), full2(dsk), full2(ng), full2(ex)],
        out_specs=pl.BlockSpec((L, GROUP_W), lambda b, c: (b * nc + c, 0)),
        out_shape=jax.ShapeDtypeStruct((B * S, GROUP_W), BF16),
        scratch_shapes=[pltpu.VMEM((8, 1024), F32), pltpu.VMEM((SSM_GROUPS, SSM_STATE, 256), F32)],
        compiler_params=_params(("parallel", "arbitrary")),
        name="ssd",
    )(proj, proj, small, cw, cb, dtb, aneg, dsk, ng, ex)


def _ret_kernel(q_ref, k_ref, v_ref, g_ref, cos_ref, sin_ref, dec_ref, qs_ref, ks_ref, sd_ref,
                o_ref, st_ref):
    @pl.when(pl.program_id(0) == 0)
    def _():
        st_ref[...] = jnp.zeros_like(st_ref)

    cos = cos_ref[...]
    sin = sin_ref[...]
    for b in range(q_ref.shape[0]):
        outs = []
        for h in range(RET_HEADS):
            sl = slice(h * LANES, (h + 1) * LANES)
            qh = q_ref[b, :, sl]
            kh = k_ref[b, :, sl]
            qh = qh * cos + pltpu.roll(qh, RET_DK // 2, axis=1) * sin
            kh = (kh * cos + pltpu.roll(kh, RET_DK // 2, axis=1) * sin) * (RET_DK ** -0.5)
            vh = v_ref[b, :, sl].astype(BF16)
            sc = (_dot_nt(qh.astype(BF16), kh.astype(BF16)) * dec_ref[h]).astype(BF16)
            st = st_ref[b, h]
            y = _dot(sc, vh) + _dot((qh * qs_ref[:, sl]).astype(BF16), st.astype(BF16))
            kd_t = (kh * ks_ref[:, sl]).T.astype(BF16)
            st_ref[b, h] = st * sd_ref[h] + _dot(kd_t, vh)
            mu = jnp.mean(y, axis=-1, keepdims=True)
            d = y - mu
            outs.append(d * lax.rsqrt(jnp.mean(d * d, axis=-1, keepdims=True) + 1e-5))
        o_ref[b] = (_silu(g_ref[b]) * jnp.concatenate(outs, axis=1)).astype(o_ref.dtype)


def _retention(proj, B, S, cos_t, sin_t):
    L = CHUNK
    nc = S // L
    lg = jnp.log(1.0 - 2.0 ** (-5.0 - jnp.arange(RET_HEADS, dtype=F32)))
    i = jnp.arange(L, dtype=F32)
    diff = i[:, None] - i[None, :]
    dec = jnp.where(diff >= 0, jnp.exp(lg[:, None, None] * jnp.maximum(diff, 0.0)), 0.0)
    rep = lambda t: jnp.repeat(t, LANES, axis=1)
    qs = rep(jnp.exp((i[:, None] + 1.0) * lg[None, :]))
    ks = rep(jnp.exp((L - 1.0 - i[:, None]) * lg[None, :]))
    sd = jnp.broadcast_to(jnp.exp(L * lg)[:, None, None], (RET_HEADS, LANES, LANES))
    proj3 = proj.reshape(B, S, proj.shape[1])
    col = lambda name: pl.BlockSpec((B, L, GROUP_W), lambda c, n=COL[name] // 4: (0, c, n))
    return pl.pallas_call(
        _ret_kernel,
        grid=(nc,),
        in_specs=[col("rq"), col("rk"), col("rv"), col("rg"),
                  pl.BlockSpec((L, LANES), lambda c: (c, 0)),
                  pl.BlockSpec((L, LANES), lambda c: (c, 0)),
                  pl.BlockSpec((RET_HEADS, L, L), lambda c: (0, 0, 0)),
                  pl.BlockSpec((L, GROUP_W), lambda c: (0, 0)),
                  pl.BlockSpec((L, GROUP_W), lambda c: (0, 0)),
                  pl.BlockSpec((RET_HEADS, LANES, LANES), lambda c: (0, 0, 0))],
        out_specs=pl.BlockSpec((B, L, GROUP_W), lambda c: (0, c, 0)),
        out_shape=jax.ShapeDtypeStruct((B, S, GROUP_W), BF16),
        scratch_shapes=[pltpu.VMEM((B, RET_HEADS, RET_DK, RET_DK), F32)],
        compiler_params=_params(("arbitrary",)),
        name="retention",
    )(proj3, proj3, proj3, proj3, cos_t, sin_t, dec, qs, ks, sd).reshape(B * S, GROUP_W)


def _t5_bucket(dist):
    n = jnp.maximum(dist, 0)
    nf = jnp.maximum(n, 1).astype(F32)
    large = REL_EXACT + (jnp.log(nf / REL_EXACT) / math.log(REL_MAX_DIST / REL_EXACT)
                         * (REL_BUCKETS - REL_EXACT)).astype(jnp.int32)
    return jnp.where(n < REL_EXACT, n, jnp.minimum(large, REL_BUCKETS - 1))


def _head_bias(bucket, rel_ref):
    rows, cols = bucket.shape
    per_head = []
    for h in range(NSA_HEADS):
        tbl = jnp.broadcast_to(rel_ref[h:h + 1, :], (rows, LANES))
        chunks = [jnp.take_along_axis(tbl, bucket[:, c:c + LANES], axis=1)
                  for c in range(0, cols, LANES)]
        per_head.append(chunks[0] if len(chunks) == 1 else jnp.concatenate(chunks, axis=1))
    return jnp.stack(per_head, axis=0)


def _stack_heads(q):
    lane = lax.broadcasted_iota(jnp.int32, (q.shape[0], LANES), 1)
    rows = []
    for h in range(NSA_HEADS):
        slab = q[:, (h // 2) * LANES:(h // 2 + 1) * LANES]
        src_half, dst_half = h % 2, h // NSA_GROUP
        if src_half != dst_half:
            slab = pltpu.roll(slab, NSA_HEAD_DIM, axis=1)
        rows.append(jnp.where(lane // NSA_HEAD_DIM == dst_half, slab, 0.0))
    return jnp.concatenate(rows, axis=0).astype(BF16)


def _cmp_kernel(g_ref, pe_ref, w1a_ref, w1b_ref, w2_ref, o_ref):
    nb = g_ref.shape[0]
    gw = g_ref.shape[1] // 4
    pe = pe_ref[...]
    slabs = [g_ref[:, s * gw:(s + 1) * gw] for s in range(4)]
    nxt0 = pltpu.roll(slabs[0], nb - 1, axis=0)
    for s in range(4):
        a = (slabs[s] + pe[0:1, :]).astype(BF16)
        bn = ((slabs[s + 1] if s < 3 else nxt0) + pe[1:2, :]).astype(BF16)
        hid = _silu(_dot(a, w1a_ref[...]) + _dot(bn, w1b_ref[...]))
        o_ref[s * nb:(s + 1) * nb, :] = _dot(hid.astype(BF16), w2_ref[...]).astype(o_ref.dtype)


def _compress(g, pe2, w1a, w1b, w2bd):
    B, nb, gw4 = g.shape
    full2 = lambda a: pl.BlockSpec(a.shape, lambda b: (0, 0))
    slab_major = pl.pallas_call(
        _cmp_kernel,
        grid=(B,),
        in_specs=[pl.BlockSpec((None, nb, gw4), lambda b: (b, 0, 0)),
                  full2(pe2), full2(w1a), full2(w1b), full2(w2bd)],
        out_specs=pl.BlockSpec((None, 4 * nb, LANES), lambda b: (b, 0, 0)),
        out_shape=jax.ShapeDtypeStruct((B, 4 * nb, LANES), BF16),
        compiler_params=_params(("parallel",)),
        name="nsa_compress",
    )(g, pe2, w1a, w1b, w2bd)
    return slab_major.reshape(B, 4, nb, LANES).transpose(0, 2, 1, 3).reshape(B, 4 * nb, LANES)


def _cmpattn_kernel(q_ref, kc_ref, vc_ref, band_ref, st_ref, oc_ref, selb_ref):
    tq = q_ref.shape[0]
    ncmp = kc_ref.shape[0]
    nb = ncmp // 4
    R = NSA_HEADS * tq
    q0 = pl.program_id(1) * tq
    nd = band_ref.shape[0]
    Q = _stack_heads(q_ref[...])
    s3 = _dot_nt(Q, kc_ref[...]).reshape(NSA_HEADS, tq, ncmp)
    parts = []
    for ch in range(ncmp // LANES):
        d = jnp.clip((q0 - ch * LANES * NSA_CMP_STRIDE) // tq + CMP_BAND_OFF, 0, nd - 1)
        parts.append(s3[:, :, ch * LANES:(ch + 1) * LANES] + band_ref[d])
    s3 = jnp.concatenate(parts, axis=2)
    mx = jnp.max(s3, axis=-1, keepdims=True)
    e = jnp.exp(s3 - mx)
    live = (q0 + lax.broadcasted_iota(jnp.int32, (tq, 1), 0) >= NSA_CMP_BLOCK - 1)[None]
    p = e * jnp.where(live, 1.0 / jnp.sum(e, axis=-1, keepdims=True), 0.0)
    oc_ref[...] = _dot(p.reshape(R, ncmp).astype(BF16), vc_ref[...])

    ps = p.reshape(NSA_KV, NSA_GROUP, tq, ncmp).sum(axis=1).reshape(NSA_KV * tq, ncmp)
    imp = _exact_right_dot(ps, st_ref[...])
    j = lax.broadcasted_iota(jnp.int32, (NSA_KV * tq, nb), 1)
    t = q0 + (lax.broadcasted_iota(jnp.int32, (NSA_KV * tq, nb), 0) % tq)
    cur = t // NSA_SLC_BLOCK
    forced = (j == 0) | (j == cur) | (j == cur - 1)
    score = jnp.where(j > cur, -1.0, jnp.where(forced, NSA_GROUP + 1.0, imp))
    score = score.T
    jf = lax.broadcasted_iota(jnp.int32, score.shape, 0).astype(F32)
    sel = score == NSA_GROUP + 1.0
    score = jnp.where(sel, -jnp.inf, score)
    for _ in range(min(NSA_TOP_N, nb) - 3):
        best = jnp.max(score, axis=0, keepdims=True)
        first = jnp.min(jnp.where(score == best, jf, float(nb)), axis=0, keepdims=True)
        hit = jf == first
        sel = sel | hit
        score = jnp.where(hit, -jnp.inf, score)
    selb = jnp.where(sel, 0.0, NEG).T.astype(selb_ref.dtype)
    selb_ref[:, 0:nb] = selb[0:tq]
    selb_ref[:, nb:2 * nb] = selb[tq:2 * tq]


def _cmpattn(proj, kcmp, vcmp, cband, B, S):
    tq = TQ
    nqt = S // tq
    ncmp = kcmp.shape[1]
    nb = ncmp // 4
    off = np.arange(ncmp)[:, None] - 4 * np.arange(nb)[None, :]
    stencil = np.where((off >= 0) & (off <= 2), 1.0, np.where((off == -1) | (off == 3), 0.5, 0.0))
    stencil = jnp.asarray(stencil, BF16)
    once = pl.Buffered(1)
    return pl.pallas_call(
        _cmpattn_kernel,
        grid=(B, nqt),
        in_specs=[pl.BlockSpec((tq, GROUP_W), lambda b, i: (b * nqt + i, COL["nq"] // 4)),
                  pl.BlockSpec((None, ncmp, LANES), lambda b, i: (b, 0, 0)),
                  pl.BlockSpec((None, ncmp, LANES), lambda b, i: (b, 0, 0)),
                  pl.BlockSpec(cband.shape, lambda b, i: (0, 0, 0, 0), pipeline_mode=once),
                  pl.BlockSpec(stencil.shape, lambda b, i: (0, 0))],
        out_specs=[pl.BlockSpec((NSA_HEADS * tq, LANES), lambda b, i: (b * nqt + i, 0)),
                   pl.BlockSpec((tq, 2 * nb), lambda b, i: (b * nqt + i, 0))],
        out_shape=[jax.ShapeDtypeStruct((B * S * NSA_HEADS, LANES), F32),
                   jax.ShapeDtypeStruct((B * S, 2 * nb), BF16)],
        compiler_params=_params(("parallel", "arbitrary")),
        name="nsa_cmp_attn_topk",
    )(proj, kcmp, vcmp, cband, stencil)


def _band_kernel(rel_ref, o_ref, *, entry_off, key_step, key_end):
    tq = o_ref.shape[1]
    delta = (pl.program_id(0) - entry_off) * tq
    row = lax.broadcasted_iota(jnp.int32, (tq, LANES), 0)
    col = lax.broadcasted_iota(jnp.int32, (tq, LANES), 1)
    dist = delta + row - (col * key_step + key_end)
    bias = _head_bias(_t5_bucket(dist), rel_ref)
    o_ref[...] = jnp.where((dist >= 0)[None], bias, NEG)


def _band_table(rel_t, tq, entry_off, key_step=1, key_end=0):
    nd = -(-(REL_MAX_DIST + key_step * (LANES - 1) + key_end) // tq) + entry_off + 1
    return pl.pallas_call(
        functools.partial(_band_kernel, entry_off=entry_off, key_step=key_step, key_end=key_end),
        grid=(nd,),
        in_specs=[pl.BlockSpec(rel_t.shape, lambda d: (0, 0))],
        out_specs=pl.BlockSpec((None, NSA_HEADS, tq, LANES), lambda d: (d, 0, 0, 0)),
        out_shape=jax.ShapeDtypeStruct((nd, NSA_HEADS, tq, LANES), F32),
        compiler_params=_params(("parallel",)),
        name="nsa_bias_band",
    )(rel_t)


def _selattn_kernel(q_ref, selb_ref, ks_ref, vs_ref, band_ref, sp_ref, farq_ref, os_ref,
                    qaug_ref, msel_ref, s_ref, m_ref, acc_ref):
    tq = q_ref.shape[0]
    R = NSA_HEADS * tq
    nb = selb_ref.shape[1] // 2
    n_kt = msel_ref.shape[0]
    tk = ks_ref.shape[0] // n_kt
    nd = band_ref.shape[0]
    q0 = pl.program_id(1) * tq
    qaug_ref[:, 0:LANES] = _stack_heads(q_ref[...])
    selb2 = jnp.concatenate([selb_ref[:, 0:nb], selb_ref[:, nb:2 * nb]], axis=0)
    spread = _dot(selb2, sp_ref[...])
    for c in range(n_kt):
        msel_ref[c] = spread[:, c * LANES:(c + 1) * LANES].astype(BF16)
    m_ref[...] = jnp.full(m_ref.shape, NEG, F32)
    acc_ref[...] = jnp.zeros_like(acc_ref)
    n_tiles = (q0 + tq - 1) // tk + 1

    far_pairs = (jnp.maximum(q0 - (REL_MAX_DIST - 1), 0) // tk) // 2

    def scores(c, slot):
        c = jnp.minimum(c, n_kt - 1)
        m2 = msel_ref[c]
        flags = jnp.concatenate([m2[0:tq]] * NSA_GROUP + [m2[tq:2 * tq]] * NSA_GROUP, axis=0)
        qaug_ref[:, LANES:2 * LANES] = jnp.where(c < 2 * far_pairs, flags + farq_ref[...], flags)
        s_ref[slot] = _dot_nt(qaug_ref[...], ks_ref[pl.ds(pl.multiple_of(c * tk, tk), tk), :])

    def accumulate(c, slot, far):
        k0 = pl.multiple_of(c * tk, tk)
        s = s_ref[slot]
        if not far:
            s3 = s.reshape(NSA_HEADS, tq, tk)
            parts = []
            for ch in range(tk // LANES):
                d = jnp.clip((q0 - k0) // tq - ch * (LANES // tq) + BAND_OFF, 0, nd - 1)
                parts.append(s3[:, :, ch * LANES:(ch + 1) * LANES] + band_ref[d])
            s = jnp.concatenate(parts, axis=2).reshape(R, tk)
        m_old = m_ref[...]
        m_new = jnp.maximum(m_old, jnp.max(s, axis=-1, keepdims=True))
        alpha = jnp.exp(m_old - m_new)
        p = jnp.exp(s - jnp.tile(m_new, (1, tk // LANES)))
        acc_ref[...] = jnp.tile(alpha, (1, 2)) * acc_ref[...] + _dot(p.astype(BF16), vs_ref[pl.ds(k0, tk), :])
        m_ref[...] = m_new

    def pair(i, far):
        c = 2 * i
        scores(c + 1, 1)
        accumulate(c, 0, far)
        scores(c + 2, 0)
        accumulate(c + 1, 1, far)

    scores(0, 0)
    lax.fori_loop(0, far_pairs, lambda i, carry: pair(i, True), None)
    lax.fori_loop(far_pairs, n_tiles // 2, lambda i, carry: pair(i, False), None)

    @pl.when(n_tiles % 2 == 1)
    def _():
        accumulate(n_tiles - 1, 0, False)

    acc = acc_ref[...]
    os_ref[...] = acc[:, 0:LANES] / acc[:, LANES:2 * LANES]


def _selattn(proj, selb, ks, vs, band, farq, B, S):
    tq = TQ
    tk = min(TK, S)
    n_kt = S // tk
    bpt = tk // NSA_SLC_BLOCK
    nqt = S // tq
    nb = S // NSA_SLC_BLOCK
    R = NSA_HEADS * tq
    sp = np.zeros((nb, n_kt * LANES), np.float32)
    sp[np.arange(nb), (np.arange(nb) // bpt) * LANES + np.arange(nb) % bpt] = 1.0
    once = pl.Buffered(1)
    return pl.pallas_call(
        _selattn_kernel,
        grid=(B, nqt),
        in_specs=[pl.BlockSpec((tq, GROUP_W), lambda b, i: (b * nqt + i, COL["nq"] // 4)),
                  pl.BlockSpec((tq, 2 * nb), lambda b, i: (b * nqt + i, 0)),
                  pl.BlockSpec((None, S, 2 * LANES), lambda b, i: (b, 0, 0), pipeline_mode=once),
                  pl.BlockSpec((None, S, 2 * LANES), lambda b, i: (b, 0, 0), pipeline_mode=once),
                  pl.BlockSpec(band.shape, lambda b, i: (0, 0, 0, 0), pipeline_mode=once),
                  pl.BlockSpec(sp.shape, lambda b, i: (0, 0)),
                  pl.BlockSpec(farq.shape, lambda b, i: (0, 0))],
        out_specs=pl.BlockSpec((R, LANES), lambda b, i: (b * nqt + i, 0)),
        out_shape=jax.ShapeDtypeStruct((B * S * NSA_HEADS, LANES), F32),
        scratch_shapes=[pltpu.VMEM((R, 2 * LANES), BF16), pltpu.VMEM((n_kt, 2 * tq, LANES), BF16),
                        pltpu.VMEM((2, R, tk), F32), pltpu.VMEM((R, LANES), F32),
                        pltpu.VMEM((R, 2 * LANES), F32)],
        compiler_params=_params(("parallel", "arbitrary")),
        name="nsa_sel_attn",
    )(proj, selb, ks, vs, band, jnp.asarray(sp, BF16), farq)


def _winattn_kernel(q_ref, sm_ref, oc_ref, os_ref, kw_ref, vw_ref, band_ref, ng_ref, o_ref):
    tq = q_ref.shape[1]
    R = NSA_HEADS * tq
    S = kw_ref.shape[1]
    span = min(WIN_SPAN, S)
    nd = band_ref.shape[0]
    q0 = pl.program_id(0) * tq
    start = pl.multiple_of(jnp.clip(q0 + tq - span, 0, S - span), tq)
    row = lax.broadcasted_iota(jnp.int32, (tq, span), 0)
    col = lax.broadcasted_iota(jnp.int32, (tq, span), 1)
    in_window = ((q0 + row) - (start + col) < NSA_WINDOW)[None]
    lane = lax.broadcasted_iota(jnp.int32, (tq, LANES), 1)
    for b in range(q_ref.shape[0]):
        Q = _stack_heads(q_ref[b])
        kt = kw_ref[b, pl.ds(start, span), :]
        vt = vw_ref[b, pl.ds(start, span), :]
        s3 = _dot_nt(Q, kt).reshape(NSA_HEADS, tq, span)
        parts = []
        for ch in range(span // LANES):
            d = jnp.clip((q0 - start) // tq - ch * (LANES // tq) + BAND_OFF, 0, nd - 1)
            parts.append(s3[:, :, ch * LANES:(ch + 1) * LANES] + band_ref[d])
        s3 = jnp.where(in_window, jnp.concatenate(parts, axis=2), NEG)
        mx = jnp.max(s3, axis=-1, keepdims=True)
        e = jnp.exp(s3 - mx)
        p = e * (1.0 / jnp.sum(e, axis=-1, keepdims=True))
        ow = _dot(p.reshape(R, span).astype(BF16), vt)

        gates = _sigmoid(sm_ref[b])
        heads = []
        ssq = jnp.zeros((tq, 1), F32)
        for h in range(NSA_HEADS):
            rs = slice(h * tq, (h + 1) * tq)
            g = [gates[:, GATE_LANE0 + 3 * h + br:GATE_LANE0 + 3 * h + br + 1] for br in range(3)]
            oh = g[0] * oc_ref[b, rs, :] + g[1] * os_ref[b, rs, :] + g[2] * ow[rs, :]
            kv = h // NSA_GROUP
            valid = (lane >= kv * NSA_HEAD_DIM) & (lane < (kv + 1) * NSA_HEAD_DIM)
            oh = jnp.where(valid, oh, 0.0)
            ssq = ssq + jnp.sum(oh * oh, axis=-1, keepdims=True)
            heads.append(oh)
        rinv = lax.rsqrt(ssq / GROUP_W + 1e-6)
        o_ref[b] = (jnp.concatenate(heads, axis=1) * rinv * ng_ref[...]).astype(o_ref.dtype)


def _winattn(proj, small, oc, os_, kw, vw, band, ngw, B, S):
    tq = TQ
    R = NSA_HEADS * tq
    once = pl.Buffered(1)
    per_b = lambda t: t.reshape(B, t.shape[0] // B, t.shape[1])
    return pl.pallas_call(
        _winattn_kernel,
        grid=(S // tq,),
        in_specs=[pl.BlockSpec((B, tq, GROUP_W), lambda i: (0, i, COL["nq"] // 4)),
                  pl.BlockSpec((B, tq, LANES), lambda i: (0, i, 0)),
                  pl.BlockSpec((B, R, LANES), lambda i: (0, i, 0)),
                  pl.BlockSpec((B, R, LANES), lambda i: (0, i, 0)),
                  pl.BlockSpec((B, S, LANES), lambda i: (0, 0, 0), pipeline_mode=once),
                  pl.BlockSpec((B, S, LANES), lambda i: (0, 0, 0), pipeline_mode=once),
                  pl.BlockSpec(band.shape, lambda i: (0, 0, 0, 0), pipeline_mode=once),
                  pl.BlockSpec(ngw.shape, lambda i: (0, 0))],
        out_specs=pl.BlockSpec((B, tq, NSA_HEADS * LANES), lambda i: (0, i, 0)),
        out_shape=jax.ShapeDtypeStruct((B, S, NSA_HEADS * LANES), BF16),
        compiler_params=_params(("arbitrary",)),
        name="nsa_win_attn_merge",
    )(per_b(proj), per_b(small), per_b(oc), per_b(os_), kw, vw, band, ngw).reshape(B * S, NSA_HEADS * LANES)


def _widen_heads(x, axis):
    x = jnp.moveaxis(x, axis, -1)
    lead = x.shape[:-1]
    x = x.reshape(*lead, NSA_KV, NSA_GROUP, 1, NSA_HEAD_DIM)
    sel = jnp.eye(NSA_KV, dtype=x.dtype).reshape(NSA_KV, 1, NSA_KV, 1)
    x = (x * sel).reshape(*lead, NSA_HEADS * LANES)
    return jnp.moveaxis(x, -1, axis)


def _build_w_in(w):
    (hq, hf, hi, hg, nq, nkc, nvc, nks, nvs, nkw, nvw, ngate,
     sz, sxbc, sdt, rq, rk, rv, rg) = jnp.split(w, IN_SPLITS, axis=1)
    D = w.shape[0]
    nq = nq * NSA_HEAD_DIM ** -0.5
    deint = lambda t: t.reshape(D, RET_HEADS, RET_DK // 2, 2).transpose(0, 1, 3, 2).reshape(D, GROUP_W)
    small = jnp.concatenate([ngate, sdt, jnp.zeros((D, LANES - 32), w.dtype)], axis=1)
    wide = [hq, hf, hi, hg, sxbc, nq, sz, deint(rq), deint(rk), rv, rg]
    narrow = [nkc, nvc, nks, nvs, nkw, nvw, small]
    return jnp.concatenate(wide, axis=1).astype(BF16), jnp.concatenate(narrow, axis=1).astype(BF16)


def _build_cmp_weights(pe, w1, w2):
    w1r = w1.reshape(2, NSA_CMP_STRIDE, NSA_HEAD_DIM, NSA_CMP_HIDDEN)
    eye = jnp.eye(NSA_KV, dtype=w1.dtype)
    big = jnp.einsum("ardc,kj->arkdjc", w1r, eye).reshape(
        2, NSA_CMP_STRIDE * NSA_KV * NSA_HEAD_DIM, NSA_KV * NSA_CMP_HIDDEN)
    w2bd = jnp.einsum("cd,kj->kcjd", w2, eye).reshape(NSA_KV * NSA_CMP_HIDDEN, NSA_KV * NSA_HEAD_DIM)
    per = pe.reshape(2, NSA_CMP_STRIDE, 1, NSA_HEAD_DIM)
    pe2 = jnp.broadcast_to(per, (2, NSA_CMP_STRIDE, NSA_KV, NSA_HEAD_DIM)).reshape(2, -1)
    return pe2, big[0].astype(BF16), big[1].astype(BF16), w2bd.astype(BF16)


def _rotary_tables(S):
    half = RET_DK // 2
    theta = 1.0 / (10000.0 ** jnp.linspace(0.0, 1.0, half, dtype=F32))
    ang = jnp.arange(S, dtype=F32)[:, None] * theta[None, :]
    cos, sin = jnp.cos(ang), jnp.sin(ang)
    return jnp.concatenate([cos, cos], axis=1), jnp.concatenate([-sin, sin], axis=1)


def _mixer(x2, B, S, l, p, lower_bounds, band, cband, farq, cos_t, sin_t):
    T = B * S
    w_wide, w_narrow = _build_w_in(p["w_in"][l])
    tn = NCOL * LANES // 4
    proj = _proj(x2, w_wide.reshape(-1, 4, tn).transpose(1, 0, 2), tm=min(1024, T))
    kc, vc, ks, vs, kw, vw, small = _kvproj(x2, w_narrow, tm=min(512, T))
    row = lambda v: v.reshape(1, -1).astype(F32)

    lb = lower_bounds[l].astype(F32)
    o_a = _hgrn(proj, B, S, row(jnp.log(lb)), row(jnp.log1p(-lb)), row(1.0 - lb),
                row(p["hgrn_norm_g"][l]))

    nb = S // NSA_SLC_BLOCK
    grp = lambda t: t.reshape(B, nb, 4 * NSA_CMP_STRIDE * LANES)
    kcmp = _compress(grp(kc), *_build_cmp_weights(p["nsa_pe_k"][l], p["nsa_w1_k"][l], p["nsa_w2_k"][l]))
    vcmp = _compress(grp(vc), *_build_cmp_weights(p["nsa_pe_v"][l], p["nsa_w1_v"][l], p["nsa_w2_v"][l]))
    o_cmp, selb = _cmpattn(proj, kcmp, vcmp, cband, B, S)
    seq = lambda t: t.reshape(B, S, t.shape[1])
    o_sel = _selattn(proj, selb, seq(ks), seq(vs), band, farq, B, S)
    ngw = _widen_heads(p["nsa_norm_g"][l].astype(F32), 0).reshape(1, -1)
    o_b = _winattn(proj, small, o_cmp, o_sel, seq(kw), seq(vw), band, ngw, B, S)

    lane_vec = lambda v: jnp.zeros((1, LANES), F32).at[0, DT_LANE0:DT_LANE0 + SSM_HEADS].set(v.astype(F32))
    o_c = _ssd(proj, small, B, S, p["ssm_conv_w"][l].astype(F32), row(p["ssm_conv_b"][l]),
               lane_vec(p["ssm_dt_bias"][l]), lane_vec(-jnp.exp(p["ssm_a_log"][l].astype(F32))),
               row(jnp.repeat(p["ssm_d"][l].astype(F32), SSM_HEAD_DIM)), row(p["ssm_norm_g"][l]))

    o_d = _retention(proj, B, S, cos_t, sin_t)

    w_out = p["w_out"][l]
    wa, wb, wc, wd = (w_out[i * GROUP_W:(i + 1) * GROUP_W] for i in range(4))
    return o_a, o_b, o_c, o_d, wa.astype(BF16), _widen_heads(wb, 0).astype(BF16), wc.astype(BF16), wd.astype(BF16)


def kernel(x, ln1_g, ln1_b, ffn1_w1, ffn1_w3, ffn1_w2, ln2_g, ln2_b, w_in, w_out, hgrn_lb_logits, hgrn_norm_g, nsa_pe_k, nsa_w1_k, nsa_w2_k, nsa_pe_v, nsa_w1_v, nsa_w2_v, nsa_norm_g, rel_bias, ssm_conv_w, ssm_conv_b, ssm_dt_bias, ssm_a_log, ssm_d, ssm_norm_g, ln3_g, ln3_b, ffn2_w1, ffn2_w3, ffn2_w2):
    B, S, D = x.shape
    T = B * S
    depth = w_in.shape[0]
    p = dict(w_in=w_in, w_out=w_out, hgrn_norm_g=hgrn_norm_g, nsa_pe_k=nsa_pe_k, nsa_w1_k=nsa_w1_k,
             nsa_w2_k=nsa_w2_k, nsa_pe_v=nsa_pe_v, nsa_w1_v=nsa_w1_v, nsa_w2_v=nsa_w2_v,
             nsa_norm_g=nsa_norm_g, ssm_conv_w=ssm_conv_w, ssm_conv_b=ssm_conv_b,
             ssm_dt_bias=ssm_dt_bias, ssm_a_log=ssm_a_log, ssm_d=ssm_d, ssm_norm_g=ssm_norm_g)
    cum = jnp.cumsum(jax.nn.softmax(hgrn_lb_logits.astype(F32), axis=0), axis=0)
    lower_bounds = cum - cum[:1]
    rel_t = jnp.zeros((NSA_HEADS, LANES), F32).at[:, :REL_BUCKETS].set(rel_bias.astype(F32).T)
    band = _band_table(rel_t, TQ, BAND_OFF)
    cband = _band_table(rel_t, TQ, CMP_BAND_OFF, NSA_CMP_STRIDE, NSA_CMP_BLOCK - 1)
    bpt = TK // NSA_SLC_BLOCK
    farq = jnp.zeros((NSA_HEADS, LANES), BF16).at[:, bpt:bpt + 3].set(
        jnp.stack(_split3(rel_bias.astype(F32)[REL_BUCKETS - 1]), axis=1))
    farq = jnp.repeat(farq, TQ, axis=0)
    cos_t, sin_t = _rotary_tables(S)
    row = lambda v: v.reshape(1, -1).astype(F32)
    tm = min(512, T)
    tf = 512 if ffn1_w1.shape[2] % 512 == 0 else ffn1_w1.shape[2]
    x2 = x.reshape(T, D).astype(F32)
    for l in range(depth):
        x2 = _ffn(x2, _to_bf16_col_blocks(ffn1_w1, l, tf), _to_bf16_col_blocks(ffn1_w3, l, tf),
                  _to_bf16(ffn1_w2, l), row(ln1_g[l]), row(ln1_b[l]), tm)
        o_a, o_b, o_c, o_d, wa, wb, wc, wd = _mixer(x2, B, S, l, p, lower_bounds, band, cband, farq, cos_t, sin_t)
        x2 = _outproj(x2, o_a, o_b, o_c, o_d, wa, wb, wc, wd, row(ln2_g[l]), row(ln2_b[l]), min(512, T))
        x2 = _ffn(x2, _to_bf16_col_blocks(ffn2_w1, l, tf), _to_bf16_col_blocks(ffn2_w3, l, tf),
                  _to_bf16(ffn2_w2, l), row(ln3_g[l]), row(ln3_b[l]), tm)
    return x2.reshape(B, S, D).astype(x.dtype)
```

```python
import functools
import math

import numpy as np
import jax
import jax.numpy as jnp
from jax import lax
from jax.experimental import pallas as pl
from jax.experimental.pallas import tpu as pltpu

F32 = jnp.float32
BF16 = jnp.bfloat16

D_MODEL = 2048
DEPTH = 2
GROUP_W = 512
ALPHA = (2 * DEPTH) ** 0.25
HG_HEADS = 4
NSA_HEADS = 8
NSA_KV = 2
NSA_GROUP = 4
NSA_HEAD_DIM = 64
NSA_CMP_STRIDE = 16
NSA_CMP_BLOCK = 32
NSA_SLC_BLOCK = 64
NSA_TOP_N = 16
NSA_WINDOW = 512
NSA_CMP_HIDDEN = 256
SSM_HEADS = 8
SSM_HEAD_DIM = 64
SSM_GROUPS = 2
SSM_STATE = 128
SSM_CONV = 4
RET_HEADS = 4
RET_DK = 128
REL_BUCKETS = 32
REL_EXACT = 16
REL_MAX_DIST = 2048
IN_SIZES = ((GROUP_W,) * 4 + (GROUP_W,) + (128,) * 6 + (24,)
            + (GROUP_W, 1024, SSM_HEADS) + (GROUP_W,) * 4)
IN_SPLITS = tuple(int(v) for v in np.cumsum(IN_SIZES)[:-1])

LANES = 128
VMEM_LIMIT = 56 * 1024 * 1024
CAST_BLOCK_BYTES = 4 * 1024 * 1024

COL = dict(hq=0, hf=4, hi=8, hg=12, sxbc=16, nq=24, sz=28, rq=32, rk=36, rv=40, rg=44)
NCOL = 48
KV_COLS = ("nkc", "nvc", "nks", "nvs", "nkw", "nvw", "small")
GATE_LANE0 = 0
DT_LANE0 = 24

CHUNK = 128
TQ = 128
TK = 1024
BAND_OFF = 2
CMP_BAND_OFF = 1
CMP_STEP = 256
WIN_SPAN = NSA_WINDOW + 2 * TQ
NEG = -1e30


def _params(sem):
    return pltpu.CompilerParams(dimension_semantics=sem, vmem_limit_bytes=VMEM_LIMIT)


def _dot(a, b):
    return jnp.dot(a, b, preferred_element_type=F32)


def _dot_nt(a, b):
    return lax.dot_general(a, b, (((1,), (1,)), ((), ())), preferred_element_type=F32)


def _split3(x):
    hi = x.astype(BF16)
    r1 = x - hi.astype(F32)
    mid = r1.astype(BF16)
    return hi, mid, (r1 - mid.astype(F32)).astype(BF16)


def _exact_left_dot(w, x):
    n = x.shape[1]
    y = _dot(w, jnp.concatenate(_split3(x), axis=1))
    return y[:, 0:n] + y[:, n:2 * n] + y[:, 2 * n:3 * n]


def _exact_right_dot(x, w):
    n = x.shape[0]
    y = _dot(jnp.concatenate(_split3(x), axis=0), w)
    return y[0:n] + y[n:2 * n] + y[2 * n:3 * n]


def _sigmoid(x):
    return 1.0 / (1.0 + jnp.exp(-x))


def _silu(x):
    return x * _sigmoid(x)


def _softplus(x):
    return jnp.maximum(x, 0.0) + jnp.log1p(jnp.exp(-jnp.abs(x)))


def _layer_norm(r, g, b):
    mu = jnp.mean(r, axis=-1, keepdims=True)
    d = r - mu
    var = jnp.mean(d * d, axis=-1, keepdims=True)
    return d * lax.rsqrt(var + 1e-5) * g + b


def _cast_kernel(x_ref, o_ref):
    o_ref[...] = x_ref[...].astype(o_ref.dtype)


def _to_bf16(w, l):
    _, r, c = w.shape
    tr = min(r, max(16, CAST_BLOCK_BYTES // (4 * c) // 16 * 16))
    while r % tr:
        tr -= 16
    return pl.pallas_call(
        _cast_kernel,
        grid=(r // tr,),
        in_specs=[pl.BlockSpec((None, tr, c), lambda i: (l, i, 0))],
        out_specs=pl.BlockSpec((tr, c), lambda i: (i, 0)),
        out_shape=jax.ShapeDtypeStruct((r, c), BF16),
        compiler_params=_params(("parallel",)),
        name="cast_bf16",
    )(w)


def _to_bf16_col_blocks(w, l, tc):
    _, r, c = w.shape
    return pl.pallas_call(
        _cast_kernel,
        grid=(c // tc,),
        in_specs=[pl.BlockSpec((None, r, tc), lambda j: (l, 0, j))],
        out_specs=pl.BlockSpec((None, r, tc), lambda j: (j, 0, 0)),
        out_shape=jax.ShapeDtypeStruct((c // tc, r, tc), BF16),
        compiler_params=_params(("parallel",)),
        name="cast_bf16_blocked",
    )(w)


def _ffn_kernel(x_ref, w1_ref, w3_ref, w2_ref, g_ref, b_ref, o_ref, acc_ref, xb_ref):
    j = pl.program_id(1)

    @pl.when(j == 0)
    def _():
        xb_ref[...] = x_ref[...].astype(BF16)
        acc_ref[...] = jnp.zeros_like(acc_ref)

    xb = xb_ref[...]
    h1 = _dot(xb, w1_ref[...])
    h3 = _dot(xb, w3_ref[...])
    a = (_silu(h1) * h3).astype(BF16)
    acc_ref[...] += _dot(a, w2_ref[...])

    @pl.when(j == pl.num_programs(1) - 1)
    def _():
        r = ALPHA * x_ref[...] + 0.5 * acc_ref[...]
        o_ref[...] = _layer_norm(r, g_ref[...], b_ref[...])


def _ffn(x, w1, w3, w2, g, b, tm):
    T, D = x.shape
    nf, _, tf = w1.shape
    return pl.pallas_call(
        _ffn_kernel,
        grid=(T // tm, nf),
        in_specs=[
            pl.BlockSpec((tm, D), lambda i, j: (i, 0)),
            pl.BlockSpec((None, D, tf), lambda i, j: (j, 0, 0)),
            pl.BlockSpec((None, D, tf), lambda i, j: (j, 0, 0)),
            pl.BlockSpec((tf, D), lambda i, j: (j, 0)),
            pl.BlockSpec((1, D), lambda i, j: (0, 0)),
            pl.BlockSpec((1, D), lambda i, j: (0, 0)),
        ],
        out_specs=pl.BlockSpec((tm, D), lambda i, j: (i, 0)),
        out_shape=jax.ShapeDtypeStruct((T, D), F32),
        scratch_shapes=[pltpu.VMEM((tm, D), F32), pltpu.VMEM((tm, D), BF16)],
        compiler_params=_params(("parallel", "arbitrary")),
        name="ffn_ln",
    )(x, w1, w3, w2, g, b)


def _proj_kernel(x_ref, w_ref, o_ref, xb_ref):
    @pl.when(pl.program_id(1) == 0)
    def _():
        xb_ref[...] = x_ref[...].astype(BF16)

    o_ref[...] = _dot(xb_ref[...], w_ref[...])


def _proj(x, w, tm):
    T, D = x.shape
    nn, _, tn = w.shape
    return pl.pallas_call(
        _proj_kernel,
        grid=(T // tm, nn),
        in_specs=[pl.BlockSpec((tm, D), lambda i, j: (i, 0)),
                  pl.BlockSpec((None, D, tn), lambda i, j: (j, 0, 0))],
        out_specs=pl.BlockSpec((tm, tn), lambda i, j: (i, j)),
        out_shape=jax.ShapeDtypeStruct((T, nn * tn), F32),
        scratch_shapes=[pltpu.VMEM((tm, D), BF16)],
        compiler_params=_params(("parallel", "arbitrary")),
        name="in_proj",
    )(x, w)


def _kvproj_kernel(x_ref, w_ref, kc_ref, vc_ref, ks_ref, vs_ref, kw_ref, vw_ref, sm_ref):
    tm = x_ref.shape[0]
    y = _dot(x_ref[...].astype(BF16), w_ref[...])
    piece = lambda n: y[:, n * LANES:(n + 1) * LANES]
    kc_ref[...] = piece(0)
    vc_ref[...] = piece(1)
    row = pl.program_id(0) * tm + lax.broadcasted_iota(jnp.int32, (tm, LANES), 0)
    lane = lax.broadcasted_iota(jnp.int32, (tm, LANES), 1)
    bpt = TK // NSA_SLC_BLOCK
    onehot = (((row // NSA_SLC_BLOCK) % bpt == lane) | ((lane >= bpt) & (lane < bpt + 3))).astype(BF16)
    ks_ref[...] = jnp.concatenate([piece(2).astype(BF16), onehot], axis=1)
    vs_ref[...] = jnp.concatenate([piece(3).astype(BF16), jnp.ones((tm, LANES), BF16)], axis=1)
    kw_ref[...] = piece(4).astype(BF16)
    vw_ref[...] = piece(5).astype(BF16)
    sm_ref[...] = piece(6)


def _kvproj(x, w, tm):
    T, D = x.shape
    narrow = lambda dt, width=LANES: (pl.BlockSpec((tm, width), lambda i: (i, 0)),
                                      jax.ShapeDtypeStruct((T, width), dt))
    outs = [narrow(F32), narrow(F32), narrow(BF16, 2 * LANES), narrow(BF16, 2 * LANES),
            narrow(BF16), narrow(BF16), narrow(F32)]
    return pl.pallas_call(
        _kvproj_kernel,
        grid=(T // tm,),
        in_specs=[pl.BlockSpec((tm, D), lambda i: (i, 0)), pl.BlockSpec(w.shape, lambda i: (0, 0))],
        out_specs=[o[0] for o in outs],
        out_shape=[o[1] for o in outs],
        compiler_params=_params(("parallel",)),
        name="kv_proj",
    )(x, w)


def _outproj_kernel(x_ref, oa_ref, ob_ref, oc_ref, od_ref, wa_ref, wb_ref, wc_ref, wd_ref,
                    g_ref, b_ref, o_ref):
    mix = (_dot(oa_ref[...], wa_ref[...]) + _dot(ob_ref[...], wb_ref[...])
           + _dot(oc_ref[...], wc_ref[...]) + _dot(od_ref[...], wd_ref[...]))
    o_ref[...] = _layer_norm(ALPHA * x_ref[...] + mix, g_ref[...], b_ref[...])


def _outproj(x, oa, ob, oc, od, wa, wb, wc, wd, g, b, tm):
    T, D = x.shape
    row = lambda a: pl.BlockSpec((tm, a.shape[1]), lambda i: (i, 0))
    full = lambda a: pl.BlockSpec(a.shape, lambda i: (0, 0))
    return pl.pallas_call(
        _outproj_kernel,
        grid=(T // tm,),
        in_specs=[row(x), row(oa), row(ob), row(oc), row(od),
                  full(wa), full(wb), full(wc), full(wd), full(g), full(b)],
        out_specs=row(x),
        out_shape=jax.ShapeDtypeStruct((T, D), F32),
        compiler_params=_params(("parallel",)),
        name="out_proj_ln",
    )(x, oa, ob, oc, od, wa, wb, wc, wd, g, b)


def _hgrn_tables(C):
    i = np.arange(C)[:, None]
    ip = np.arange(C)[None, :]
    seg = [(ip <= i),
           (ip > i)]
    masks = [np.eye(C, dtype=bool)]
    s = C // 2
    while s >= 1:
        blk = i // s
        seg.append(np.where(blk % 2 == 1, (ip > blk * s) & (ip <= i), (ip > i) & (ip <= (blk + 1) * s)))
        masks.append((blk % 2 == 1) & (ip // s == blk - 1))
        s //= 2
    seg = np.concatenate([x.astype(np.float32) for x in seg], axis=0)
    return seg, np.stack([m.astype(np.float32) for m in masks])


def _hgrn_kernel(q_ref, f_ref, i_ref, g_ref, llb_ref, l1m_ref, oml_ref, ng_ref,
                 seg_ref, msk_ref, o_ref, st_ref):
    @pl.when(pl.program_id(0) == 0)
    def _():
        st_ref[...] = jnp.zeros_like(st_ref)

    C = q_ref.shape[1]
    nlev = msk_ref.shape[0] - 1
    for bi in range(q_ref.shape[0]):
        q = _silu(q_ref[bi])
        z = f_ref[bi]
        log_sig = jnp.minimum(z, 0.0) - jnp.log1p(jnp.exp(-jnp.abs(z)))
        cc = l1m_ref[...] + log_sig
        llb = llb_ref[...]
        logf = jnp.maximum(llb, cc) + jnp.log1p(jnp.exp(-jnp.abs(llb - cc)))
        k = oml_ref[...] * _sigmoid(-z)
        v = i_ref[bi]
        seg = _exact_left_dot(seg_ref[...], logf)
        outs = []
        for h in range(HG_HEADS):
            sl = slice(h * LANES, (h + 1) * LANES)
            qh, kh, vh = q[:, sl], k[:, sl], v[:, sl]
            a = msk_ref[0] * _dot_nt(qh.astype(BF16), kh.astype(BF16))
            for l in range(nlev):
                dec = jnp.exp(seg[(2 + l) * C:(3 + l) * C, sl])
                a = a + msk_ref[1 + l] * _dot_nt((qh * dec).astype(BF16), (kh * dec).astype(BF16))
            b = seg[0:C, sl]
            st = st_ref[bi, h]
            o = _dot(a.astype(BF16), vh.astype(BF16))
            o = o + _dot_nt((qh * jnp.exp(b)).astype(BF16), st.astype(BF16))
            kd = (kh * jnp.exp(seg[C:2 * C, sl])).astype(BF16)
            st_ref[bi, h] = st * jnp.exp(b[C - 1:C, :]) + _dot(vh.T.astype(BF16), kd)
            outs.append(o * lax.rsqrt(jnp.mean(o * o, axis=-1, keepdims=True) + 1e-6))
        o = jnp.concatenate(outs, axis=1)
        o_ref[bi] = (o * ng_ref[...] * _silu(g_ref[bi])).astype(o_ref.dtype)


def _hgrn(proj, B, S, llb, l1m, oml, ng):
    C = CHUNK
    nc = S // C
    seg, msk = _hgrn_tables(C)
    seg, msk = jnp.asarray(seg, BF16), jnp.asarray(msk)
    proj3 = proj.reshape(B, S, proj.shape[1])
    col = lambda name: pl.BlockSpec((B, C, GROUP_W), lambda c, n=COL[name] // 4: (0, c, n))
    vec = pl.BlockSpec((1, GROUP_W), lambda c: (0, 0))
    full2 = lambda a: pl.BlockSpec(a.shape, lambda c: (0, 0))
    return pl.pallas_call(
        _hgrn_kernel,
        grid=(nc,),
        in_specs=[col("hq"), col("hf"), col("hi"), col("hg"), vec, vec, vec, vec,
                  full2(seg), pl.BlockSpec(msk.shape, lambda c: (0, 0, 0))],
        out_specs=pl.BlockSpec((B, C, GROUP_W), lambda c: (0, c, 0)),
        out_shape=jax.ShapeDtypeStruct((B, S, GROUP_W), BF16),
        scratch_shapes=[pltpu.VMEM((B, HG_HEADS, LANES, LANES), F32)],
        compiler_params=_params(("arbitrary",)),
        name="hgrn2",
    )(proj3, proj3, proj3, proj3, llb, l1m, oml, ng, seg, msk).reshape(B * S, GROUP_W)


def _ssd_kernel(z_ref, xbc_ref, sm_ref, cw_ref, cb_ref, dtb_ref, aneg_ref, dsk_ref, ng_ref,
                ex_ref, o_ref, tail_ref, st_ref):
    @pl.when(pl.program_id(1) == 0)
    def _():
        tail_ref[...] = jnp.zeros_like(tail_ref)
        st_ref[...] = jnp.zeros_like(st_ref)

    L = xbc_ref.shape[0]
    x = xbc_ref[...]
    xe = jnp.concatenate([tail_ref[...], x], axis=0)
    cw = cw_ref[...]
    conv = cb_ref[...]
    for kk in range(SSM_CONV):
        conv = conv + cw[kk:kk + 1, :] * xe[5 + kk:5 + kk + L, :]
    tail_ref[...] = x[L - 8:L, :]
    conv = _silu(conv)
    xs = conv[:, 0:GROUP_W]
    bm = conv[:, GROUP_W:GROUP_W + 256]
    cm = conv[:, GROUP_W + 256:GROUP_W + 512]

    dtf = _softplus(sm_ref[...] + dtb_ref[...])
    la = dtf * aneg_ref[...]
    ri = lax.broadcasted_iota(jnp.int32, (L, L), 0)
    ci = lax.broadcasted_iota(jnp.int32, (L, L), 1)
    tri = ri >= ci
    bfull = _exact_left_dot(tri.astype(BF16), la)
    ex = ex_ref[...]
    bexp = _exact_right_dot(bfull, ex)
    dtexp = _exact_right_dot(dtf, ex)
    b_t = bfull.T
    xdt = xs * dtexp
    lane = lax.broadcasted_iota(jnp.int32, (L, LANES), 1)

    scores = []
    for g in range(SSM_GROUPS):
        cg = cm[:, g * SSM_STATE:(g + 1) * SSM_STATE].astype(BF16)
        bg = bm[:, g * SSM_STATE:(g + 1) * SSM_STATE].astype(BF16)
        cb = _dot_nt(cg, bg)
        for hh in range(SSM_HEADS // SSM_GROUPS):
            h = g * (SSM_HEADS // SSM_GROUPS) + hh
            bcol = bfull[:, DT_LANE0 + h:DT_LANE0 + h + 1]
            brow = b_t[DT_LANE0 + h:DT_LANE0 + h + 1, :]
            dec = jnp.exp(jnp.where(tri, bcol - brow, NEG))
            scores.append((cb * dec).astype(BF16))
    y_pairs = []
    for u in range(SSM_HEADS // 2):
        slab = xdt[:, u * LANES:(u + 1) * LANES]
        lo = jnp.where(lane < SSM_HEAD_DIM, slab, 0.0).astype(BF16)
        hi = jnp.where(lane >= SSM_HEAD_DIM, slab, 0.0).astype(BF16)
        y_pairs.append(_dot(scores[2 * u], lo) + _dot(scores[2 * u + 1], hi))
    y_intra = jnp.concatenate(y_pairs, axis=1)

    blast = bexp[L - 1:L, :]
    w = (xdt * jnp.exp(blast - bexp)).astype(BF16)
    y_inter = []
    for g in range(SSM_GROUPS):
        gs = slice(g * 256, (g + 1) * 256)
        cg = cm[:, g * SSM_STATE:(g + 1) * SSM_STATE].astype(BF16)
        st = st_ref[g]
        y_inter.append(_dot(cg, st.astype(BF16)))
        bg_t = bm[:, g * SSM_STATE:(g + 1) * SSM_STATE].T.astype(BF16)
        st_ref[g] = st * jnp.exp(blast[:, gs]) + _dot(bg_t, w[:, gs])
    y = y_intra + jnp.concatenate(y_inter, axis=1) * jnp.exp(bexp) + dsk_ref[...] * xs
    y = y * _silu(z_ref[...])
    halves = []
    for g in range(SSM_GROUPS):
        seg = y[:, g * 256:(g + 1) * 256]
        halves.append(seg * lax.rsqrt(jnp.mean(seg * seg, axis=-1, keepdims=True) + 1e-6))
    o_ref[...] = (jnp.concatenate(halves, axis=1) * ng_ref[...]).astype(o_ref.dtype)


def _ssd(proj, small, B, S, cw, cb, dtb, aneg, dsk, ng):
    L = CHUNK
    nc = S // L
    ex = np.zeros((LANES, GROUP_W), np.float32)
    for h in range(SSM_HEADS):
        ex[DT_LANE0 + h, h * SSM_HEAD_DIM:(h + 1) * SSM_HEAD_DIM] = 1.0
    ex = jnp.asarray(ex, BF16)
    full2 = lambda a: pl.BlockSpec(a.shape, lambda b, c: (0, 0))
    return pl.pallas_call(
        _ssd_kernel,
        grid=(B, nc),
        in_specs=[
            pl.BlockSpec((L, GROUP_W), lambda b, c: (b * nc + c, COL["sz"] // 4)),
            pl.BlockSpec((L, 1024), lambda b, c: (b * nc + c, COL["sxbc"] // 8)),
            pl.BlockSpec((L, LANES), lambda b, c: (b * nc + c, 0)),
            full2(cw), full2(cb), full2(dtb), full2(aneg), full2(dsk), full2(ng), full2(ex)],
        out_specs=pl.BlockSpec((L, GROUP_W), lambda b, c: (b * nc + c, 0)),
        out_shape=jax.ShapeDtypeStruct((B * S, GROUP_W), BF16),
        scratch_shapes=[pltpu.VMEM((8, 1024), F32), pltpu.VMEM((SSM_GROUPS, SSM_STATE, 256), F32)],
        compiler_params=_params(("parallel", "arbitrary")),
        name="ssd",
    )(proj, proj, small, cw, cb, dtb, aneg, dsk, ng, ex)


def _ret_kernel(q_ref, k_ref, v_ref, g_ref, cos_ref, sin_ref, dec_ref, qs_ref, ks_ref, sd_ref,
                o_ref, st_ref):
    @pl.when(pl.program_id(0) == 0)
    def _():
        st_ref[...] = jnp.zeros_like(st_ref)

    cos = cos_ref[...]
    sin = sin_ref[...]
    for b in range(q_ref.shape[0]):
        outs = []
        for h in range(RET_HEADS):
            sl = slice(h * LANES, (h + 1) * LANES)
            qh = q_ref[b, :, sl]
            kh = k_ref[b, :, sl]
            qh = qh * cos + pltpu.roll(qh, RET_DK // 2, axis=1) * sin
            kh = (kh * cos + pltpu.roll(kh, RET_DK // 2, axis=1) * sin) * (RET_DK ** -0.5)
            vh = v_ref[b, :, sl].astype(BF16)
            sc = (_dot_nt(qh.astype(BF16), kh.astype(BF16)) * dec_ref[h]).astype(BF16)
            st = st_ref[b, h]
            y = _dot(sc, vh) + _dot((qh * qs_ref[:, sl]).astype(BF16), st.astype(BF16))
            kd_t = (kh * ks_ref[:, sl]).T.astype(BF16)
            st_ref[b, h] = st * sd_ref[h] + _dot(kd_t, vh)
            mu = jnp.mean(y, axis=-1, keepdims=True)
            d = y - mu
            outs.append(d * lax.rsqrt(jnp.mean(d * d, axis=-1, keepdims=True) + 1e-5))
        o_ref[b] = (_silu(g_ref[b]) * jnp.concatenate(outs, axis=1)).astype(o_ref.dtype)


def _retention(proj, B, S, cos_t, sin_t):
    L = CHUNK
    nc = S // L
    lg = jnp.log(1.0 - 2.0 ** (-5.0 - jnp.arange(RET_HEADS, dtype=F32)))
    i = jnp.arange(L, dtype=F32)
    diff = i[:, None] - i[None, :]
    dec = jnp.where(diff >= 0, jnp.exp(lg[:, None, None] * jnp.maximum(diff, 0.0)), 0.0)
    rep = lambda t: jnp.repeat(t, LANES, axis=1)
    qs = rep(jnp.exp((i[:, None] + 1.0) * lg[None, :]))
    ks = rep(jnp.exp((L - 1.0 - i[:, None]) * lg[None, :]))
    sd = jnp.broadcast_to(jnp.exp(L * lg)[:, None, None], (RET_HEADS, LANES, LANES))
    proj3 = proj.reshape(B, S, proj.shape[1])
    col = lambda name: pl.BlockSpec((B, L, GROUP_W), lambda c, n=COL[name] // 4: (0, c, n))
    return pl.pallas_call(
        _ret_kernel,
        grid=(nc,),
        in_specs=[col("rq"), col("rk"), col("rv"), col("rg"),
                  pl.BlockSpec((L, LANES), lambda c: (c, 0)),
                  pl.BlockSpec((L, LANES), lambda c: (c, 0)),
                  pl.BlockSpec((RET_HEADS, L, L), lambda c: (0, 0, 0)),
                  pl.BlockSpec((L, GROUP_W), lambda c: (0, 0)),
                  pl.BlockSpec((L, GROUP_W), lambda c: (0, 0)),
                  pl.BlockSpec((RET_HEADS, LANES, LANES), lambda c: (0, 0, 0))],
        out_specs=pl.BlockSpec((B, L, GROUP_W), lambda c: (0, c, 0)),
        out_shape=jax.ShapeDtypeStruct((B, S, GROUP_W), BF16),
        scratch_shapes=[pltpu.VMEM((B, RET_HEADS, RET_DK, RET_DK), F32)],
        compiler_params=_params(("arbitrary",)),
        name="retention",
    )(proj3, proj3, proj3, proj3, cos_t, sin_t, dec, qs, ks, sd).reshape(B * S, GROUP_W)


def _t5_bucket(dist):
    n = jnp.maximum(dist, 0)
    nf = jnp.maximum(n, 1).astype(F32)
    large = REL_EXACT + (jnp.log(nf / REL_EXACT) / math.log(REL_MAX_DIST / REL_EXACT)
                         * (REL_BUCKETS - REL_EXACT)).astype(jnp.int32)
    return jnp.where(n < REL_EXACT, n, jnp.minimum(large, REL_BUCKETS - 1))


def _head_bias(bucket, rel_ref):
    rows, cols = bucket.shape
    per_head = []
    for h in range(NSA_HEADS):
        tbl = jnp.broadcast_to(rel_ref[h:h + 1, :], (rows, LANES))
        chunks = [jnp.take_along_axis(tbl, bucket[:, c:c + LANES], axis=1)
                  for c in range(0, cols, LANES)]
        per_head.append(chunks[0] if len(chunks) == 1 else jnp.concatenate(chunks, axis=1))
    return jnp.stack(per_head, axis=0)


def _stack_heads(q):
    lane = lax.broadcasted_iota(jnp.int32, (q.shape[0], LANES), 1)
    rows = []
    for h in range(NSA_HEADS):
        slab = q[:, (h // 2) * LANES:(h // 2 + 1) * LANES]
        src_half, dst_half = h % 2, h // NSA_GROUP
        if src_half != dst_half:
            slab = pltpu.roll(slab, NSA_HEAD_DIM, axis=1)
        rows.append(jnp.where(lane // NSA_HEAD_DIM == dst_half, slab, 0.0))
    return jnp.concatenate(rows, axis=0).astype(BF16)


def _cmp_kernel(g_ref, pe_ref, w1a_ref, w1b_ref, w2_ref, o_ref):
    nb = g_ref.shape[0]
    gw = g_ref.shape[1] // 4
    pe = pe_ref[...]
    slabs = [g_ref[:, s * gw:(s + 1) * gw] for s in range(4)]
    nxt0 = pltpu.roll(slabs[0], nb - 1, axis=0)
    for s in range(4):
        a = (slabs[s] + pe[0:1, :]).astype(BF16)
        bn = ((slabs[s + 1] if s < 3 else nxt0) + pe[1:2, :]).astype(BF16)
        hid = _silu(_dot(a, w1a_ref[...]) + _dot(bn, w1b_ref[...]))
        o_ref[s * nb:(s + 1) * nb, :] = _dot(hid.astype(BF16), w2_ref[...]).astype(o_ref.dtype)


def _compress(g, pe2, w1a, w1b, w2bd):
    B, nb, gw4 = g.shape
    full2 = lambda a: pl.BlockSpec(a.shape, lambda b: (0, 0))
    slab_major = pl.pallas_call(
        _cmp_kernel,
        grid=(B,),
        in_specs=[pl.BlockSpec((None, nb, gw4), lambda b: (b, 0, 0)),
                  full2(pe2), full2(w1a), full2(w1b), full2(w2bd)],
        out_specs=pl.BlockSpec((None, 4 * nb, LANES), lambda b: (b, 0, 0)),
        out_shape=jax.ShapeDtypeStruct((B, 4 * nb, LANES), BF16),
        compiler_params=_params(("parallel",)),
        name="nsa_compress",
    )(g, pe2, w1a, w1b, w2bd)
    return slab_major.reshape(B, 4, nb, LANES).transpose(0, 2, 1, 3).reshape(B, 4 * nb, LANES)


def _cmpattn_kernel(q_ref, kc_ref, vc_ref, band_ref, st_ref, oc_ref, selb_ref):
    tq = q_ref.shape[0]
    ncmp = kc_ref.shape[0]
    nb = ncmp // 4
    R = NSA_HEADS * tq
    q0 = pl.program_id(1) * tq
    nd = band_ref.shape[0]

    def attend(ncv, nbv):
        Q = _stack_heads(q_ref[...])
        s3 = _dot_nt(Q, kc_ref[0:ncv, :]).reshape(NSA_HEADS, tq, ncv)
        parts = []
        for ch in range(ncv // LANES):
            d = jnp.clip((q0 - ch * LANES * NSA_CMP_STRIDE) // tq + CMP_BAND_OFF, 0, nd - 1)
            parts.append(s3[:, :, ch * LANES:(ch + 1) * LANES] + band_ref[d])
        s3 = jnp.concatenate(parts, axis=2)
        mx = jnp.max(s3, axis=-1, keepdims=True)
        e = jnp.exp(s3 - mx)
        live = (q0 + lax.broadcasted_iota(jnp.int32, (tq, 1), 0) >= NSA_CMP_BLOCK - 1)[None]
        p = e * jnp.where(live, 1.0 / jnp.sum(e, axis=-1, keepdims=True), 0.0)
        oc_ref[...] = _dot(p.reshape(R, ncv).astype(BF16), vc_ref[0:ncv, :])

        ps = p.reshape(NSA_KV, NSA_GROUP, tq, ncv).sum(axis=1).reshape(NSA_KV * tq, ncv)
        imp = _exact_right_dot(ps, st_ref[0:ncv, 0:nbv])
        j = lax.broadcasted_iota(jnp.int32, (NSA_KV * tq, nbv), 1)
        t = q0 + (lax.broadcasted_iota(jnp.int32, (NSA_KV * tq, nbv), 0) % tq)
        cur = t // NSA_SLC_BLOCK
        forced = (j == 0) | (j == cur) | (j == cur - 1)
        score = jnp.where(j > cur, -1.0, jnp.where(forced, NSA_GROUP + 1.0, imp))
        score = score.T
        jf = lax.broadcasted_iota(jnp.int32, score.shape, 0).astype(F32)
        sel = score == NSA_GROUP + 1.0
        score = jnp.where(sel, -jnp.inf, score)
        for _ in range(min(NSA_TOP_N, nb) - 3):
            best = jnp.max(score, axis=0, keepdims=True)
            first = jnp.min(jnp.where(score == best, jf, float(nbv)), axis=0, keepdims=True)
            hit = jf == first
            sel = sel | hit
            score = jnp.where(hit, -jnp.inf, score)
        selb = jnp.where(sel, 0.0, NEG).T.astype(selb_ref.dtype)
        for kv in range(NSA_KV):
            selb_ref[:, kv * nb:kv * nb + nbv] = selb[kv * tq:(kv + 1) * tq]
            if nbv < nb:
                selb_ref[:, kv * nb + nbv:(kv + 1) * nb] = jnp.full((tq, nb - nbv), NEG, selb_ref.dtype)

    if ncmp % CMP_STEP:
        attend(ncmp, nb)
    else:
        nvar = ncmp // CMP_STEP
        sizes = [(CMP_STEP * v, min(nb, LANES * ((v + 1) // 2))) for v in range(1, nvar + 1)]
        last_key = (q0 + tq - 1) // NSA_CMP_STRIDE
        lax.switch(jnp.minimum(last_key // CMP_STEP, nvar - 1),
                   [functools.partial(attend, ncv, nbv) for ncv, nbv in sizes])


def _cmpattn(proj, kcmp, vcmp, cband, B, S):
    tq = TQ
    nqt = S // tq
    ncmp = kcmp.shape[1]
    nb = ncmp // 4
    off = np.arange(ncmp)[:, None] - 4 * np.arange(nb)[None, :]
    stencil = np.where((off >= 0) & (off <= 2), 1.0, np.where((off == -1) | (off == 3), 0.5, 0.0))
    stencil = jnp.asarray(stencil, BF16)
    once = pl.Buffered(1)
    return pl.pallas_call(
        _cmpattn_kernel,
        grid=(B, nqt),
        in_specs=[pl.BlockSpec((tq, GROUP_W), lambda b, i: (b * nqt + i, COL["nq"] // 4)),
                  pl.BlockSpec((None, ncmp, LANES), lambda b, i: (b, 0, 0)),
                  pl.BlockSpec((None, ncmp, LANES), lambda b, i: (b, 0, 0)),
                  pl.BlockSpec(cband.shape, lambda b, i: (0, 0, 0, 0), pipeline_mode=once),
                  pl.BlockSpec(stencil.shape, lambda b, i: (0, 0))],
        out_specs=[pl.BlockSpec((NSA_HEADS * tq, LANES), lambda b, i: (b * nqt + i, 0)),
                   pl.BlockSpec((tq, 2 * nb), lambda b, i: (b * nqt + i, 0))],
        out_shape=[jax.ShapeDtypeStruct((B * S * NSA_HEADS, LANES), F32),
                   jax.ShapeDtypeStruct((B * S, 2 * nb), BF16)],
        compiler_params=_params(("parallel", "arbitrary")),
        name="nsa_cmp_attn_topk",
    )(proj, kcmp, vcmp, cband, stencil)


def _band_kernel(rel_ref, o_ref, *, entry_off, key_step, key_end):
    tq = o_ref.shape[1]
    delta = (pl.program_id(0) - entry_off) * tq
    row = lax.broadcasted_iota(jnp.int32, (tq, LANES), 0)
    col = lax.broadcasted_iota(jnp.int32, (tq, LANES), 1)
    dist = delta + row - (col * key_step + key_end)
    bias = _head_bias(_t5_bucket(dist), rel_ref)
    o_ref[...] = jnp.where((dist >= 0)[None], bias, NEG)


def _band_table(rel_t, tq, entry_off, key_step=1, key_end=0):
    nd = -(-(REL_MAX_DIST + key_step * (LANES - 1) + key_end) // tq) + entry_off + 1
    return pl.pallas_call(
        functools.partial(_band_kernel, entry_off=entry_off, key_step=key_step, key_end=key_end),
        grid=(nd,),
        in_specs=[pl.BlockSpec(rel_t.shape, lambda d: (0, 0))],
        out_specs=pl.BlockSpec((None, NSA_HEADS, tq, LANES), lambda d: (d, 0, 0, 0)),
        out_shape=jax.ShapeDtypeStruct((nd, NSA_HEADS, tq, LANES), F32),
        compiler_params=_params(("parallel",)),
        name="nsa_bias_band",
    )(rel_t)


def _selattn_kernel(q_ref, selb_ref, ks_ref, vs_ref, band_ref, sp_ref, farq_ref, os_ref,
                    qaug_ref, msel_ref, s_ref, m_ref, acc_ref):
    tq = q_ref.shape[0]
    R = NSA_HEADS * tq
    nb = selb_ref.shape[1] // 2
    n_kt = msel_ref.shape[0]
    tk = ks_ref.shape[0] // n_kt
    nd = band_ref.shape[0]
    q0 = pl.program_id(1) * tq
    qaug_ref[:, 0:LANES] = _stack_heads(q_ref[...])
    selb2 = jnp.concatenate([selb_ref[:, 0:nb], selb_ref[:, nb:2 * nb]], axis=0)
    spread = _dot(selb2, sp_ref[...])
    for c in range(n_kt):
        msel_ref[c] = spread[:, c * LANES:(c + 1) * LANES].astype(BF16)
    m_ref[...] = jnp.full(m_ref.shape, NEG, F32)
    acc_ref[...] = jnp.zeros_like(acc_ref)
    n_tiles = (q0 + tq - 1) // tk + 1

    far_pairs = (jnp.maximum(q0 - (REL_MAX_DIST - 1), 0) // tk) // 2

    def scores(c, slot):
        c = jnp.minimum(c, n_kt - 1)
        m2 = msel_ref[c]
        flags = jnp.concatenate([m2[0:tq]] * NSA_GROUP + [m2[tq:2 * tq]] * NSA_GROUP, axis=0)
        qaug_ref[:, LANES:2 * LANES] = jnp.where(c < 2 * far_pairs, flags + farq_ref[...], flags)
        s_ref[slot] = _dot_nt(qaug_ref[...], ks_ref[pl.ds(pl.multiple_of(c * tk, tk), tk), :])

    def accumulate(c, slot, far):
        k0 = pl.multiple_of(c * tk, tk)
        s = s_ref[slot]
        if not far:
            s3 = s.reshape(NSA_HEADS, tq, tk)
            parts = []
            for ch in range(tk // LANES):
                d = jnp.clip((q0 - k0) // tq - ch * (LANES // tq) + BAND_OFF, 0, nd - 1)
                parts.append(s3[:, :, ch * LANES:(ch + 1) * LANES] + band_ref[d])
            s = jnp.concatenate(parts, axis=2).reshape(R, tk)
        m_old = m_ref[...]
        m_new = jnp.maximum(m_old, jnp.max(s, axis=-1, keepdims=True))
        alpha = jnp.exp(m_old - m_new)
        p = jnp.exp(s - jnp.tile(m_new, (1, tk // LANES)))
        acc_ref[...] = jnp.tile(alpha, (1, 2)) * acc_ref[...] + _dot(p.astype(BF16), vs_ref[pl.ds(k0, tk), :])
        m_ref[...] = m_new

    def pair(i, far):
        c = 2 * i
        scores(c + 1, 1)
        accumulate(c, 0, far)
        scores(c + 2, 0)
        accumulate(c + 1, 1, far)

    scores(0, 0)
    lax.fori_loop(0, far_pairs, lambda i, carry: pair(i, True), None)
    lax.fori_loop(far_pairs, n_tiles // 2, lambda i, carry: pair(i, False), None)

    @pl.when(n_tiles % 2 == 1)
    def _():
        accumulate(n_tiles - 1, 0, False)

    acc = acc_ref[...]
    os_ref[...] = acc[:, 0:LANES] / acc[:, LANES:2 * LANES]


def _selattn(proj, selb, ks, vs, band, farq, B, S):
    tq = TQ
    tk = min(TK, S)
    n_kt = S // tk
    bpt = tk // NSA_SLC_BLOCK
    nqt = S // tq
    nb = S // NSA_SLC_BLOCK
    R = NSA_HEADS * tq
    sp = np.zeros((nb, n_kt * LANES), np.float32)
    sp[np.arange(nb), (np.arange(nb) // bpt) * LANES + np.arange(nb) % bpt] = 1.0
    once = pl.Buffered(1)
    return pl.pallas_call(
        _selattn_kernel,
        grid=(B, nqt),
        in_specs=[pl.BlockSpec((tq, GROUP_W), lambda b, i: (b * nqt + i, COL["nq"] // 4)),
                  pl.BlockSpec((tq, 2 * nb), lambda b, i: (b * nqt + i, 0)),
                  pl.BlockSpec((None, S, 2 * LANES), lambda b, i: (b, 0, 0), pipeline_mode=once),
                  pl.BlockSpec((None, S, 2 * LANES), lambda b, i: (b, 0, 0), pipeline_mode=once),
                  pl.BlockSpec(band.shape, lambda b, i: (0, 0, 0, 0), pipeline_mode=once),
                  pl.BlockSpec(sp.shape, lambda b, i: (0, 0)),
                  pl.BlockSpec(farq.shape, lambda b, i: (0, 0))],
        out_specs=pl.BlockSpec((R, LANES), lambda b, i: (b * nqt + i, 0)),
        out_shape=jax.ShapeDtypeStruct((B * S * NSA_HEADS, LANES), F32),
        scratch_shapes=[pltpu.VMEM((R, 2 * LANES), BF16), pltpu.VMEM((n_kt, 2 * tq, LANES), BF16),
                        pltpu.VMEM((2, R, tk), F32), pltpu.VMEM((R, LANES), F32),
                        pltpu.VMEM((R, 2 * LANES), F32)],
        compiler_params=_params(("parallel", "arbitrary")),
        name="nsa_sel_attn",
    )(proj, selb, ks, vs, band, jnp.asarray(sp, BF16), farq)


def _winattn_kernel(q_ref, sm_ref, oc_ref, os_ref, kw_ref, vw_ref, band_ref, ng_ref, o_ref):
    tq = q_ref.shape[1]
    R = NSA_HEADS * tq
    S = kw_ref.shape[1]
    span = min(WIN_SPAN, S)
    nd = band_ref.shape[0]
    q0 = pl.program_id(0) * tq
    start = pl.multiple_of(jnp.clip(q0 + tq - span, 0, S - span), tq)
    row = lax.broadcasted_iota(jnp.int32, (tq, span), 0)
    col = lax.broadcasted_iota(jnp.int32, (tq, span), 1)
    in_window = ((q0 + row) - (start + col) < NSA_WINDOW)[None]
    lane = lax.broadcasted_iota(jnp.int32, (tq, LANES), 1)
    for b in range(q_ref.shape[0]):
        Q = _stack_heads(q_ref[b])
        kt = kw_ref[b, pl.ds(start, span), :]
        vt = vw_ref[b, pl.ds(start, span), :]
        s3 = _dot_nt(Q, kt).reshape(NSA_HEADS, tq, span)
        parts = []
        for ch in range(span // LANES):
            d = jnp.clip((q0 - start) // tq - ch * (LANES // tq) + BAND_OFF, 0, nd - 1)
            parts.append(s3[:, :, ch * LANES:(ch + 1) * LANES] + band_ref[d])
        s3 = jnp.where(in_window, jnp.concatenate(parts, axis=2), NEG)
        mx = jnp.max(s3, axis=-1, keepdims=True)
        e = jnp.exp(s3 - mx)
        p = e * (1.0 / jnp.sum(e, axis=-1, keepdims=True))
        ow = _dot(p.reshape(R, span).astype(BF16), vt)

        gates = _sigmoid(sm_ref[b])
        heads = []
        ssq = jnp.zeros((tq, 1), F32)
        for h in range(NSA_HEADS):
            rs = slice(h * tq, (h + 1) * tq)
            g = [gates[:, GATE_LANE0 + 3 * h + br:GATE_LANE0 + 3 * h + br + 1] for br in range(3)]
            oh = g[0] * oc_ref[b, rs, :] + g[1] * os_ref[b, rs, :] + g[2] * ow[rs, :]
            kv = h // NSA_GROUP
            valid = (lane >= kv * NSA_HEAD_DIM) & (lane < (kv + 1) * NSA_HEAD_DIM)
            oh = jnp.where(valid, oh, 0.0)
            ssq = ssq + jnp.sum(oh * oh, axis=-1, keepdims=True)
            heads.append(oh)
        rinv = lax.rsqrt(ssq / GROUP_W + 1e-6)
        o_ref[b] = (jnp.concatenate(heads, axis=1) * rinv * ng_ref[...]).astype(o_ref.dtype)


def _winattn(proj, small, oc, os_, kw, vw, band, ngw, B, S):
    tq = TQ
    R = NSA_HEADS * tq
    once = pl.Buffered(1)
    per_b = lambda t: t.reshape(B, t.shape[0] // B, t.shape[1])
    return pl.pallas_call(
        _winattn_kernel,
        grid=(S // tq,),
        in_specs=[pl.BlockSpec((B, tq, GROUP_W), lambda i: (0, i, COL["nq"] // 4)),
                  pl.BlockSpec((B, tq, LANES), lambda i: (0, i, 0)),
                  pl.BlockSpec((B, R, LANES), lambda i: (0, i, 0)),
                  pl.BlockSpec((B, R, LANES), lambda i: (0, i, 0)),
                  pl.BlockSpec((B, S, LANES), lambda i: (0, 0, 0), pipeline_mode=once),
                  pl.BlockSpec((B, S, LANES), lambda i: (0, 0, 0), pipeline_mode=once),
                  pl.BlockSpec(band.shape, lambda i: (0, 0, 0, 0), pipeline_mode=once),
                  pl.BlockSpec(ngw.shape, lambda i: (0, 0))],
        out_specs=pl.BlockSpec((B, tq, NSA_HEADS * LANES), lambda i: (0, i, 0)),
        out_shape=jax.ShapeDtypeStruct((B, S, NSA_HEADS * LANES), BF16),
        compiler_params=_params(("arbitrary",)),
        name="nsa_win_attn_merge",
    )(per_b(proj), per_b(small), per_b(oc), per_b(os_), kw, vw, band, ngw).reshape(B * S, NSA_HEADS * LANES)


def _widen_heads(x, axis):
    x = jnp.moveaxis(x, axis, -1)
    lead = x.shape[:-1]
    x = x.reshape(*lead, NSA_KV, NSA_GROUP, 1, NSA_HEAD_DIM)
    sel = jnp.eye(NSA_KV, dtype=x.dtype).reshape(NSA_KV, 1, NSA_KV, 1)
    x = (x * sel).reshape(*lead, NSA_HEADS * LANES)
    return jnp.moveaxis(x, -1, axis)


def _build_w_in(w):
    (hq, hf, hi, hg, nq, nkc, nvc, nks, nvs, nkw, nvw, ngate,
     sz, sxbc, sdt, rq, rk, rv, rg) = jnp.split(w, IN_SPLITS, axis=1)
    D = w.shape[0]
    nq = nq * NSA_HEAD_DIM ** -0.5
    deint = lambda t: t.reshape(D, RET_HEADS, RET_DK // 2, 2).transpose(0, 1, 3, 2).reshape(D, GROUP_W)
    small = jnp.concatenate([ngate, sdt, jnp.zeros((D, LANES - 32), w.dtype)], axis=1)
    wide = [hq, hf, hi, hg, sxbc, nq, sz, deint(rq), deint(rk), rv, rg]
    narrow = [nkc, nvc, nks, nvs, nkw, nvw, small]
    return jnp.concatenate(wide, axis=1).astype(BF16), jnp.concatenate(narrow, axis=1).astype(BF16)


def _build_cmp_weights(pe, w1, w2):
    w1r = w1.reshape(2, NSA_CMP_STRIDE, NSA_HEAD_DIM, NSA_CMP_HIDDEN)
    eye = jnp.eye(NSA_KV, dtype=w1.dtype)
    big = jnp.einsum("ardc,kj->arkdjc", w1r, eye).reshape(
        2, NSA_CMP_STRIDE * NSA_KV * NSA_HEAD_DIM, NSA_KV * NSA_CMP_HIDDEN)
    w2bd = jnp.einsum("cd,kj->kcjd", w2, eye).reshape(NSA_KV * NSA_CMP_HIDDEN, NSA_KV * NSA_HEAD_DIM)
    per = pe.reshape(2, NSA_CMP_STRIDE, 1, NSA_HEAD_DIM)
    pe2 = jnp.broadcast_to(per, (2, NSA_CMP_STRIDE, NSA_KV, NSA_HEAD_DIM)).reshape(2, -1)
    return pe2, big[0].astype(BF16), big[1].astype(BF16), w2bd.astype(BF16)


def _rotary_tables(S):
    half = RET_DK // 2
    theta = 1.0 / (10000.0 ** jnp.linspace(0.0, 1.0, half, dtype=F32))
    ang = jnp.arange(S, dtype=F32)[:, None] * theta[None, :]
    cos, sin = jnp.cos(ang), jnp.sin(ang)
    return jnp.concatenate([cos, cos], axis=1), jnp.concatenate([-sin, sin], axis=1)


def _mixer(x2, B, S, l, p, lower_bounds, band, cband, farq, cos_t, sin_t):
    T = B * S
    w_wide, w_narrow = _build_w_in(p["w_in"][l])
    tn = NCOL * LANES // 4
    proj = _proj(x2, w_wide.reshape(-1, 4, tn).transpose(1, 0, 2), tm=min(1024, T))
    kc, vc, ks, vs, kw, vw, small = _kvproj(x2, w_narrow, tm=min(512, T))
    row = lambda v: v.reshape(1, -1).astype(F32)

    lb = lower_bounds[l].astype(F32)
    o_a = _hgrn(proj, B, S, row(jnp.log(lb)), row(jnp.log1p(-lb)), row(1.0 - lb),
                row(p["hgrn_norm_g"][l]))

    nb = S // NSA_SLC_BLOCK
    grp = lambda t: t.reshape(B, nb, 4 * NSA_CMP_STRIDE * LANES)
    kcmp = _compress(grp(kc), *_build_cmp_weights(p["nsa_pe_k"][l], p["nsa_w1_k"][l], p["nsa_w2_k"][l]))
    vcmp = _compress(grp(vc), *_build_cmp_weights(p["nsa_pe_v"][l], p["nsa_w1_v"][l], p["nsa_w2_v"][l]))
    o_cmp, selb = _cmpattn(proj, kcmp, vcmp, cband, B, S)
    seq = lambda t: t.reshape(B, S, t.shape[1])
    o_sel = _selattn(proj, selb, seq(ks), seq(vs), band, farq, B, S)
    ngw = _widen_heads(p["nsa_norm_g"][l].astype(F32), 0).reshape(1, -1)
    o_b = _winattn(proj, small, o_cmp, o_sel, seq(kw), seq(vw), band, ngw, B, S)

    lane_vec = lambda v: jnp.zeros((1, LANES), F32).at[0, DT_LANE0:DT_LANE0 + SSM_HEADS].set(v.astype(F32))
    o_c = _ssd(proj, small, B, S, p["ssm_conv_w"][l].astype(F32), row(p["ssm_conv_b"][l]),
               lane_vec(p["ssm_dt_bias"][l]), lane_vec(-jnp.exp(p["ssm_a_log"][l].astype(F32))),
               row(jnp.repeat(p["ssm_d"][l].astype(F32), SSM_HEAD_DIM)), row(p["ssm_norm_g"][l]))

    o_d = _retention(proj, B, S, cos_t, sin_t)

    w_out = p["w_out"][l]
    wa, wb, wc, wd = (w_out[i * GROUP_W:(i + 1) * GROUP_W] for i in range(4))
    return o_a, o_b, o_c, o_d, wa.astype(BF16), _widen_heads(wb, 0).astype(BF16), wc.astype(BF16), wd.astype(BF16)


def kernel(x, ln1_g, ln1_b, ffn1_w1, ffn1_w3, ffn1_w2, ln2_g, ln2_b, w_in, w_out, hgrn_lb_logits, hgrn_norm_g, nsa_pe_k, nsa_w1_k, nsa_w2_k, nsa_pe_v, nsa_w1_v, nsa_w2_v, nsa_norm_g, rel_bias, ssm_conv_w, ssm_conv_b, ssm_dt_bias, ssm_a_log, ssm_d, ssm_norm_g, ln3_g, ln3_b, ffn2_w1, ffn2_w3, ffn2_w2):
    B, S, D = x.shape
    T = B * S
    depth = w_in.shape[0]
    p = dict(w_in=w_in, w_out=w_out, hgrn_norm_g=hgrn_norm_g, nsa_pe_k=nsa_pe_k, nsa_w1_k=nsa_w1_k,
             nsa_w2_k=nsa_w2_k, nsa_pe_v=nsa_pe_v, nsa_w1_v=nsa_w1_v, nsa_w2_v=nsa_w2_v,
             nsa_norm_g=nsa_norm_g, ssm_conv_w=ssm_conv_w, ssm_conv_b=ssm_conv_b,
             ssm_dt_bias=ssm_dt_bias, ssm_a_log=ssm_a_log, ssm_d=ssm_d, ssm_norm_g=ssm_norm_g)
    cum = jnp.cumsum(jax.nn.softmax(hgrn_lb_logits.astype(F32), axis=0), axis=0)
    lower_bounds = cum - cum[:1]
    rel_t = jnp.zeros((NSA_HEADS, LANES), F32).at[:, :REL_BUCKETS].set(rel_bias.astype(F32).T)
    band = _band_table(rel_t, TQ, BAND_OFF)
    cband = _band_table(rel_t, TQ, CMP_BAND_OFF, NSA_CMP_STRIDE, NSA_CMP_BLOCK - 1)
    bpt = TK // NSA_SLC_BLOCK
    farq = jnp.zeros((NSA_HEADS, LANES), BF16).at[:, bpt:bpt + 3].set(
        jnp.stack(_split3(rel_bias.astype(F32)[REL_BUCKETS - 1]), axis=1))
    farq = jnp.repeat(farq, TQ, axis=0)
    cos_t, sin_t = _rotary_tables(S)
    row = lambda v: v.reshape(1, -1).astype(F32)
    tm = min(512, T)
    tf = 512 if ffn1_w1.shape[2] % 512 == 0 else ffn1_w1.shape[2]
    x2 = x.reshape(T, D).astype(F32)
    for l in range(depth):
        x2 = _ffn(x2, _to_bf16_col_blocks(ffn1_w1, l, tf), _to_bf16_col_blocks(ffn1_w3, l, tf),
                  _to_bf16(ffn1_w2, l), row(ln1_g[l]), row(ln1_b[l]), tm)
        o_a, o_b, o_c, o_d, wa, wb, wc, wd = _mixer(x2, B, S, l, p, lower_bounds, band, cband, farq, cos_t, sin_t)
        x2 = _outproj(x2, o_a, o_b, o_c, o_d, wa, wb, wc, wd, row(ln2_g[l]), row(ln2_b[l]), min(512, T))
        x2 = _ffn(x2, _to_bf16_col_blocks(ffn2_w1, l, tf), _to_bf16_col_blocks(ffn2_w3, l, tf),
                  _to_bf16(ffn2_w2, l), row(ln3_g[l]), row(ln3_b[l]), tm)
    return x2.reshape(B, S, D).astype(x.dtype)
```

```python
import functools
import math

import numpy as np
import jax
import jax.numpy as jnp
from jax import lax
from jax.experimental import pallas as pl
from jax.experimental.pallas import tpu as pltpu

F32 = jnp.float32
BF16 = jnp.bfloat16

D_MODEL = 2048
DEPTH = 2
GROUP_W = 512
ALPHA = (2 * DEPTH) ** 0.25
HG_HEADS = 4
NSA_HEADS = 8
NSA_KV = 2
NSA_GROUP = 4
NSA_HEAD_DIM = 64
NSA_CMP_STRIDE = 16
NSA_CMP_BLOCK = 32
NSA_SLC_BLOCK = 64
NSA_TOP_N = 16
NSA_WINDOW = 512
NSA_CMP_HIDDEN = 256
SSM_HEADS = 8
SSM_HEAD_DIM = 64
SSM_GROUPS = 2
SSM_STATE = 128
SSM_CONV = 4
RET_HEADS = 4
RET_DK = 128
REL_BUCKETS = 32
REL_EXACT = 16
REL_MAX_DIST = 2048
IN_SIZES = ((GROUP_W,) * 4 + (GROUP_W,) + (128,) * 6 + (24,)
            + (GROUP_W, 1024, SSM_HEADS) + (GROUP_W,) * 4)
IN_SPLITS = tuple(int(v) for v in np.cumsum(IN_SIZES)[:-1])

LANES = 128
VMEM_LIMIT = 56 * 1024 * 1024
CAST_BLOCK_BYTES = 4 * 1024 * 1024

COL = dict(hq=0, hf=4, hi=8, hg=12, sxbc=16, nq=24, sz=28, rq=32, rk=36, rv=40, rg=44)
NCOL = 48
KV_COLS = ("nkc", "nvc", "nks", "nvs", "nkw", "nvw", "small")
GATE_LANE0 = 0
DT_LANE0 = 24

CHUNK = 128
TQ = 128
TK = 1024
BAND_OFF = 2
CMP_BAND_OFF = 1
CMP_STEP = 256
WIN_SPAN = NSA_WINDOW + 2 * TQ
NEG = -1e30


def _params(sem):
    return pltpu.CompilerParams(dimension_semantics=sem, vmem_limit_bytes=VMEM_LIMIT)


def _dot(a, b):
    return jnp.dot(a, b, preferred_element_type=F32)


def _dot_nt(a, b):
    return lax.dot_general(a, b, (((1,), (1,)), ((), ())), preferred_element_type=F32)


def _split3(x):
    hi = x.astype(BF16)
    r1 = x - hi.astype(F32)
    mid = r1.astype(BF16)
    return hi, mid, (r1 - mid.astype(F32)).astype(BF16)


def _exact_left_dot(w, x):
    n = x.shape[1]
    y = _dot(w, jnp.concatenate(_split3(x), axis=1))
    return y[:, 0:n] + y[:, n:2 * n] + y[:, 2 * n:3 * n]


def _exact_right_dot(x, w):
    n = x.shape[0]
    y = _dot(jnp.concatenate(_split3(x), axis=0), w)
    return y[0:n] + y[n:2 * n] + y[2 * n:3 * n]


def _sigmoid(x):
    return 1.0 / (1.0 + jnp.exp(-x))


def _silu(x):
    return x * _sigmoid(x)


def _softplus(x):
    return jnp.maximum(x, 0.0) + jnp.log1p(jnp.exp(-jnp.abs(x)))


def _layer_norm(r, g, b):
    mu = jnp.mean(r, axis=-1, keepdims=True)
    d = r - mu
    var = jnp.mean(d * d, axis=-1, keepdims=True)
    return d * lax.rsqrt(var + 1e-5) * g + b


def _cast_kernel(x_ref, o_ref):
    o_ref[...] = x_ref[...].astype(o_ref.dtype)


def _to_bf16(w, l):
    _, r, c = w.shape
    tr = min(r, max(16, CAST_BLOCK_BYTES // (4 * c) // 16 * 16))
    while r % tr:
        tr -= 16
    return pl.pallas_call(
        _cast_kernel,
        grid=(r // tr,),
        in_specs=[pl.BlockSpec((None, tr, c), lambda i: (l, i, 0))],
        out_specs=pl.BlockSpec((tr, c), lambda i: (i, 0)),
        out_shape=jax.ShapeDtypeStruct((r, c), BF16),
        compiler_params=_params(("parallel",)),
        name="cast_bf16",
    )(w)


def _to_bf16_col_blocks(w, l, tc):
    _, r, c = w.shape
    return pl.pallas_call(
        _cast_kernel,
        grid=(c // tc,),
        in_specs=[pl.BlockSpec((None, r, tc), lambda j: (l, 0, j))],
        out_specs=pl.BlockSpec((None, r, tc), lambda j: (j, 0, 0)),
        out_shape=jax.ShapeDtypeStruct((c // tc, r, tc), BF16),
        compiler_params=_params(("parallel",)),
        name="cast_bf16_blocked",
    )(w)


def _ffn_kernel(x_ref, w1_ref, w3_ref, w2_ref, g_ref, b_ref, o_ref, acc_ref, xb_ref):
    j = pl.program_id(1)

    @pl.when(j == 0)
    def _():
        xb_ref[...] = x_ref[...].astype(BF16)
        acc_ref[...] = jnp.zeros_like(acc_ref)

    xb = xb_ref[...]
    h1 = _dot(xb, w1_ref[...])
    h3 = _dot(xb, w3_ref[...])
    a = (_silu(h1) * h3).astype(BF16)
    acc_ref[...] += _dot(a, w2_ref[...])

    @pl.when(j == pl.num_programs(1) - 1)
    def _():
        r = ALPHA * x_ref[...] + 0.5 * acc_ref[...]
        o_ref[...] = _layer_norm(r, g_ref[...], b_ref[...])


def _ffn(x, w1, w3, w2, g, b, tm):
    T, D = x.shape
    nf, _, tf = w1.shape
    return pl.pallas_call(
        _ffn_kernel,
        grid=(T // tm, nf),
        in_specs=[
            pl.BlockSpec((tm, D), lambda i, j: (i, 0)),
            pl.BlockSpec((None, D, tf), lambda i, j: (j, 0, 0)),
            pl.BlockSpec((None, D, tf), lambda i, j: (j, 0, 0)),
            pl.BlockSpec((tf, D), lambda i, j: (j, 0)),
            pl.BlockSpec((1, D), lambda i, j: (0, 0)),
            pl.BlockSpec((1, D), lambda i, j: (0, 0)),
        ],
        out_specs=pl.BlockSpec((tm, D), lambda i, j: (i, 0)),
        out_shape=jax.ShapeDtypeStruct((T, D), F32),
        scratch_shapes=[pltpu.VMEM((tm, D), F32), pltpu.VMEM((tm, D), BF16)],
        compiler_params=_params(("parallel", "arbitrary")),
        name="ffn_ln",
    )(x, w1, w3, w2, g, b)


def _proj_kernel(x_ref, w_ref, o_ref, xb_ref):
    @pl.when(pl.program_id(1) == 0)
    def _():
        xb_ref[...] = x_ref[...].astype(BF16)

    o_ref[...] = _dot(xb_ref[...], w_ref[...])


def _proj(x, w, tm):
    T, D = x.shape
    nn, _, tn = w.shape
    return pl.pallas_call(
        _proj_kernel,
        grid=(T // tm, nn),
        in_specs=[pl.BlockSpec((tm, D), lambda i, j: (i, 0)),
                  pl.BlockSpec((None, D, tn), lambda i, j: (j, 0, 0))],
        out_specs=pl.BlockSpec((tm, tn), lambda i, j: (i, j)),
        out_shape=jax.ShapeDtypeStruct((T, nn * tn), F32),
        scratch_shapes=[pltpu.VMEM((tm, D), BF16)],
        compiler_params=_params(("parallel", "arbitrary")),
        name="in_proj",
    )(x, w)


def _kvproj_kernel(x_ref, w_ref, kc_ref, vc_ref, ks_ref, vs_ref, kw_ref, vw_ref, sm_ref):
    tm = x_ref.shape[0]
    y = _dot(x_ref[...].astype(BF16), w_ref[...])
    piece = lambda n: y[:, n * LANES:(n + 1) * LANES]
    kc_ref[...] = piece(0)
    vc_ref[...] = piece(1)
    row = pl.program_id(0) * tm + lax.broadcasted_iota(jnp.int32, (tm, LANES), 0)
    lane = lax.broadcasted_iota(jnp.int32, (tm, LANES), 1)
    bpt = TK // NSA_SLC_BLOCK
    onehot = (((row // NSA_SLC_BLOCK) % bpt == lane) | ((lane >= bpt) & (lane < bpt + 3))).astype(BF16)
    ks_ref[...] = jnp.concatenate([piece(2).astype(BF16), onehot], axis=1)
    vs_ref[...] = jnp.concatenate([piece(3).astype(BF16), jnp.ones((tm, LANES), BF16)], axis=1)
    kw_ref[...] = piece(4).astype(BF16)
    vw_ref[...] = piece(5).astype(BF16)
    sm_ref[...] = piece(6)


def _kvproj(x, w, tm):
    T, D = x.shape
    narrow = lambda dt, width=LANES: (pl.BlockSpec((tm, width), lambda i: (i, 0)),
                                      jax.ShapeDtypeStruct((T, width), dt))
    outs = [narrow(F32), narrow(F32), narrow(BF16, 2 * LANES), narrow(BF16, 2 * LANES),
            narrow(BF16), narrow(BF16), narrow(F32)]
    return pl.pallas_call(
        _kvproj_kernel,
        grid=(T // tm,),
        in_specs=[pl.BlockSpec((tm, D), lambda i: (i, 0)), pl.BlockSpec(w.shape, lambda i: (0, 0))],
        out_specs=[o[0] for o in outs],
        out_shape=[o[1] for o in outs],
        compiler_params=_params(("parallel",)),
        name="kv_proj",
    )(x, w)


def _outproj_kernel(x_ref, oa_ref, ob_ref, oc_ref, od_ref, wa_ref, wb_ref, wc_ref, wd_ref,
                    g_ref, b_ref, o_ref):
    mix = (_dot(oa_ref[...], wa_ref[...]) + _dot(ob_ref[...], wb_ref[...])
           + _dot(oc_ref[...], wc_ref[...]) + _dot(od_ref[...], wd_ref[...]))
    o_ref[...] = _layer_norm(ALPHA * x_ref[...] + mix, g_ref[...], b_ref[...])


def _outproj(x, oa, ob, oc, od, wa, wb, wc, wd, g, b, tm):
    T, D = x.shape
    row = lambda a: pl.BlockSpec((tm, a.shape[1]), lambda i: (i, 0))
    full = lambda a: pl.BlockSpec(a.shape, lambda i: (0, 0))
    return pl.pallas_call(
        _outproj_kernel,
        grid=(T // tm,),
        in_specs=[row(x), row(oa), row(ob), row(oc), row(od),
                  full(wa), full(wb), full(wc), full(wd), full(g), full(b)],
        out_specs=row(x),
        out_shape=jax.ShapeDtypeStruct((T, D), F32),
        compiler_params=_params(("parallel",)),
        name="out_proj_ln",
    )(x, oa, ob, oc, od, wa, wb, wc, wd, g, b)


def _hgrn_tables(C):
    i = np.arange(C)[:, None]
    ip = np.arange(C)[None, :]
    seg = [(ip <= i),
           (ip > i)]
    masks = [np.eye(C, dtype=bool)]
    s = C // 2
    while s >= 1:
        blk = i // s
        seg.append(np.where(blk % 2 == 1, (ip > blk * s) & (ip <= i), (ip > i) & (ip <= (blk + 1) * s)))
        masks.append((blk % 2 == 1) & (ip // s == blk - 1))
        s //= 2
    seg = np.concatenate([x.astype(np.float32) for x in seg], axis=0)
    return seg, np.stack([m.astype(np.float32) for m in masks])


def _hgrn_kernel(q_ref, f_ref, i_ref, g_ref, llb_ref, l1m_ref, oml_ref, ng_ref,
                 seg_ref, msk_ref, o_ref, st_ref):
    @pl.when(pl.program_id(0) == 0)
    def _():
        st_ref[...] = jnp.zeros_like(st_ref)

    C = q_ref.shape[1]
    nlev = msk_ref.shape[0] - 1
    for bi in range(q_ref.shape[0]):
        q = _silu(q_ref[bi])
        z = f_ref[bi]
        log_sig = jnp.minimum(z, 0.0) - jnp.log1p(jnp.exp(-jnp.abs(z)))
        cc = l1m_ref[...] + log_sig
        llb = llb_ref[...]
        logf = jnp.maximum(llb, cc) + jnp.log1p(jnp.exp(-jnp.abs(llb - cc)))
        k = oml_ref[...] * _sigmoid(-z)
        v = i_ref[bi]
        seg = _exact_left_dot(seg_ref[...], logf)
        outs = []
        for h in range(HG_HEADS):
            sl = slice(h * LANES, (h + 1) * LANES)
            qh, kh, vh = q[:, sl], k[:, sl], v[:, sl]
            a = msk_ref[0] * _dot_nt(qh.astype(BF16), kh.astype(BF16))
            for l in range(nlev):
                dec = jnp.exp(seg[(2 + l) * C:(3 + l) * C, sl])
                a = a + msk_ref[1 + l] * _dot_nt((qh * dec).astype(BF16), (kh * dec).astype(BF16))
            b = seg[0:C, sl]
            st = st_ref[bi, h]
            o = _dot(a.astype(BF16), vh.astype(BF16))
            o = o + _dot_nt((qh * jnp.exp(b)).astype(BF16), st.astype(BF16))
            kd = (kh * jnp.exp(seg[C:2 * C, sl])).astype(BF16)
            st_ref[bi, h] = st * jnp.exp(b[C - 1:C, :]) + _dot(vh.T.astype(BF16), kd)
            outs.append(o * lax.rsqrt(jnp.mean(o * o, axis=-1, keepdims=True) + 1e-6))
        o = jnp.concatenate(outs, axis=1)
        o_ref[bi] = (o * ng_ref[...] * _silu(g_ref[bi])).astype(o_ref.dtype)


def _hgrn(proj, B, S, llb, l1m, oml, ng):
    C = CHUNK
    nc = S // C
    seg, msk = _hgrn_tables(C)
    seg, msk = jnp.asarray(seg, BF16), jnp.asarray(msk)
    proj3 = proj.reshape(B, S, proj.shape[1])
    col = lambda name: pl.BlockSpec((B, C, GROUP_W), lambda c, n=COL[name] // 4: (0, c, n))
    vec = pl.BlockSpec((1, GROUP_W), lambda c: (0, 0))
    full2 = lambda a: pl.BlockSpec(a.shape, lambda c: (0, 0))
    return pl.pallas_call(
        _hgrn_kernel,
        grid=(nc,),
        in_specs=[col("hq"), col("hf"), col("hi"), col("hg"), vec, vec, vec, vec,
                  full2(seg), pl.BlockSpec(msk.shape, lambda c: (0, 0, 0))],
        out_specs=pl.BlockSpec((B, C, GROUP_W), lambda c: (0, c, 0)),
        out_shape=jax.ShapeDtypeStruct((B, S, GROUP_W), BF16),
        scratch_shapes=[pltpu.VMEM((B, HG_HEADS, LANES, LANES), F32)],
        compiler_params=_params(("arbitrary",)),
        name="hgrn2",
    )(proj3, proj3, proj3, proj3, llb, l1m, oml, ng, seg, msk).reshape(B * S, GROUP_W)


def _ssd_kernel(z_ref, xbc_ref, sm_ref, cw_ref, cb_ref, dtb_ref, aneg_ref, dsk_ref, ng_ref,
                ex_ref, o_ref, tail_ref, st_ref):
    @pl.when(pl.program_id(1) == 0)
    def _():
        tail_ref[...] = jnp.zeros_like(tail_ref)
        st_ref[...] = jnp.zeros_like(st_ref)

    L = xbc_ref.shape[0]
    x = xbc_ref[...]
    xe = jnp.concatenate([tail_ref[...], x], axis=0)
    cw = cw_ref[...]
    conv = cb_ref[...]
    for kk in range(SSM_CONV):
        conv = conv + cw[kk:kk + 1, :] * xe[5 + kk:5 + kk + L, :]
    tail_ref[...] = x[L - 8:L, :]
    conv = _silu(conv)
    xs = conv[:, 0:GROUP_W]
    bm = conv[:, GROUP_W:GROUP_W + 256]
    cm = conv[:, GROUP_W + 256:GROUP_W + 512]

    dtf = _softplus(sm_ref[...] + dtb_ref[...])
    la = dtf * aneg_ref[...]
    ri = lax.broadcasted_iota(jnp.int32, (L, L), 0)
    ci = lax.broadcasted_iota(jnp.int32, (L, L), 1)
    tri = ri >= ci
    bfull = _exact_left_dot(tri.astype(BF16), la)
    ex = ex_ref[...]
    bexp = _exact_right_dot(bfull, ex)
    dtexp = _exact_right_dot(dtf, ex)
    b_t = bfull.T
    xdt = xs * dtexp
    lane = lax.broadcasted_iota(jnp.int32, (L, LANES), 1)

    scores = []
    for g in range(SSM_GROUPS):
        cg = cm[:, g * SSM_STATE:(g + 1) * SSM_STATE].astype(BF16)
        bg = bm[:, g * SSM_STATE:(g + 1) * SSM_STATE].astype(BF16)
        cb = _dot_nt(cg, bg)
        for hh in range(SSM_HEADS // SSM_GROUPS):
            h = g * (SSM_HEADS // SSM_GROUPS) + hh
            bcol = bfull[:, DT_LANE0 + h:DT_LANE0 + h + 1]
            brow = b_t[DT_LANE0 + h:DT_LANE0 + h + 1, :]
            dec = jnp.exp(jnp.where(tri, bcol - brow, NEG))
            scores.append((cb * dec).astype(BF16))
    y_pairs = []
    for u in range(SSM_HEADS // 2):
        slab = xdt[:, u * LANES:(u + 1) * LANES]
        lo = jnp.where(lane < SSM_HEAD_DIM, slab, 0.0).astype(BF16)
        hi = jnp.where(lane >= SSM_HEAD_DIM, slab, 0.0).astype(BF16)
        y_pairs.append(_dot(scores[2 * u], lo) + _dot(scores[2 * u + 1], hi))
    y_intra = jnp.concatenate(y_pairs, axis=1)

    blast = bexp[L - 1:L, :]
    w = (xdt * jnp.exp(blast - bexp)).astype(BF16)
    y_inter = []
    for g in range(SSM_GROUPS):
        gs = slice(g * 256, (g + 1) * 256)
        cg = cm[:, g * SSM_STATE:(g + 1) * SSM_STATE].astype(BF16)
        st = st_ref[g]
        y_inter.append(_dot(cg, st.astype(BF16)))
        bg_t = bm[:, g * SSM_STATE:(g + 1) * SSM_STATE].T.astype(BF16)
        st_ref[g] = st * jnp.exp(blast[:, gs]) + _dot(bg_t, w[:, gs])
    y = y_intra + jnp.concatenate(y_inter, axis=1) * jnp.exp(bexp) + dsk_ref[...] * xs
    y = y * _silu(z_ref[...])
    halves = []
    for g in range(SSM_GROUPS):
        seg = y[:, g * 256:(g + 1) * 256]
        halves.append(seg * lax.rsqrt(jnp.mean(seg * seg, axis=-1, keepdims=True) + 1e-6))
    o_ref[...] = (jnp.concatenate(halves, axis=1) * ng_ref[...]).astype(o_ref.dtype)


def _ssd(proj, small, B, S, cw, cb, dtb, aneg, dsk, ng):
    L = CHUNK
    nc = S // L
    ex = np.zeros((LANES, GROUP_W), np.float32)
    for h in range(SSM_HEADS):
        ex[DT_LANE0 + h, h * SSM_HEAD_DIM:(h + 1) * SSM_HEAD_DIM] = 1.0
    ex = jnp.asarray(ex, BF16)
    full2 = lambda a: pl.BlockSpec(a.shape, lambda b, c: (0, 0))
    return pl.pallas_call(
        _ssd_kernel,
        grid=(B, nc),
        in_specs=[
            pl.BlockSpec((L, GROUP_W), lambda b, c: (b * nc + c, COL["sz"] // 4)),
            pl.BlockSpec((L, 1024), lambda b, c: (b * nc + c, COL["sxbc"] // 8)),
            pl.BlockSpec((L, LANES), lambda b, c: (b * nc + c, 0)),
            full2(cw), full2(cb), full2(dtb), full2(aneg), full2(dsk), full2(ng), full2(ex)],
        out_specs=pl.BlockSpec((L, GROUP_W), lambda b, c: (b * nc + c, 0)),
        out_shape=jax.ShapeDtypeStruct((B * S, GROUP_W), BF16),
        scratch_shapes=[pltpu.VMEM((8, 1024), F32), pltpu.VMEM((SSM_GROUPS, SSM_STATE, 256), F32)],
        compiler_params=_params(("parallel", "arbitrary")),
        name="ssd",
    )(proj, proj, small, cw, cb, dtb, aneg, dsk, ng, ex)


def _ret_kernel(q_ref, k_ref, v_ref, g_ref, cos_ref, sin_ref, dec_ref, qs_ref, ks_ref, sd_ref,
                o_ref, st_ref):
    @pl.when(pl.program_id(0) == 0)
    def _():
        st_ref[...] = jnp.zeros_like(st_ref)

    cos = cos_ref[...]
    sin = sin_ref[...]
    for b in range(q_ref.shape[0]):
        outs = []
        for h in range(RET_HEADS):
            sl = slice(h * LANES, (h + 1) * LANES)
            qh = q_ref[b, :, sl]
            kh = k_ref[b, :, sl]
            qh = qh * cos + pltpu.roll(qh, RET_DK // 2, axis=1) * sin
            kh = (kh * cos + pltpu.roll(kh, RET_DK // 2, axis=1) * sin) * (RET_DK ** -0.5)
            vh = v_ref[b, :, sl].astype(BF16)
            sc = (_dot_nt(qh.astype(BF16), kh.astype(BF16)) * dec_ref[h]).astype(BF16)
            st = st_ref[b, h]
            y = _dot(sc, vh) + _dot((qh * qs_ref[:, sl]).astype(BF16), st.astype(BF16))
            kd_t = (kh * ks_ref[:, sl]).T.astype(BF16)
            st_ref[b, h] = st * sd_ref[h] + _dot(kd_t, vh)
            mu = jnp.mean(y, axis=-1, keepdims=True)
            d = y - mu
            outs.append(d * lax.rsqrt(jnp.mean(d * d, axis=-1, keepdims=True) + 1e-5))
        o_ref[b] = (_silu(g_ref[b]) * jnp.concatenate(outs, axis=1)).astype(o_ref.dtype)


def _retention(proj, B, S, cos_t, sin_t):
    L = CHUNK
    nc = S // L
    lg = jnp.log(1.0 - 2.0 ** (-5.0 - jnp.arange(RET_HEADS, dtype=F32)))
    i = jnp.arange(L, dtype=F32)
    diff = i[:, None] - i[None, :]
    dec = jnp.where(diff >= 0, jnp.exp(lg[:, None, None] * jnp.maximum(diff, 0.0)), 0.0)
    rep = lambda t: jnp.repeat(t, LANES, axis=1)
    qs = rep(jnp.exp((i[:, None] + 1.0) * lg[None, :]))
    ks = rep(jnp.exp((L - 1.0 - i[:, None]) * lg[None, :]))
    sd = jnp.broadcast_to(jnp.exp(L * lg)[:, None, None], (RET_HEADS, LANES, LANES))
    proj3 = proj.reshape(B, S, proj.shape[1])
    col = lambda name: pl.BlockSpec((B, L, GROUP_W), lambda c, n=COL[name] // 4: (0, c, n))
    return pl.pallas_call(
        _ret_kernel,
        grid=(nc,),
        in_specs=[col("rq"), col("rk"), col("rv"), col("rg"),
                  pl.BlockSpec((L, LANES), lambda c: (c, 0)),
                  pl.BlockSpec((L, LANES), lambda c: (c, 0)),
                  pl.BlockSpec((RET_HEADS, L, L), lambda c: (0, 0, 0)),
                  pl.BlockSpec((L, GROUP_W), lambda c: (0, 0)),
                  pl.BlockSpec((L, GROUP_W), lambda c: (0, 0)),
                  pl.BlockSpec((RET_HEADS, LANES, LANES), lambda c: (0, 0, 0))],
        out_specs=pl.BlockSpec((B, L, GROUP_W), lambda c: (0, c, 0)),
        out_shape=jax.ShapeDtypeStruct((B, S, GROUP_W), BF16),
        scratch_shapes=[pltpu.VMEM((B, RET_HEADS, RET_DK, RET_DK), F32)],
        compiler_params=_params(("arbitrary",)),
        name="retention",
    )(proj3, proj3, proj3, proj3, cos_t, sin_t, dec, qs, ks, sd).reshape(B * S, GROUP_W)


def _t5_bucket(dist):
    n = jnp.maximum(dist, 0)
    nf = jnp.maximum(n, 1).astype(F32)
    large = REL_EXACT + (jnp.log(nf / REL_EXACT) / math.log(REL_MAX_DIST / REL_EXACT)
                         * (REL_BUCKETS - REL_EXACT)).astype(jnp.int32)
    return jnp.where(n < REL_EXACT, n, jnp.minimum(large, REL_BUCKETS - 1))


def _head_bias(bucket, rel_ref):
    rows, cols = bucket.shape
    per_head = []
    for h in range(NSA_HEADS):
        tbl = jnp.broadcast_to(rel_ref[h:h + 1, :], (rows, LANES))
        chunks = [jnp.take_along_axis(tbl, bucket[:, c:c + LANES], axis=1)
                  for c in range(0, cols, LANES)]
        per_head.append(chunks[0] if len(chunks) == 1 else jnp.concatenate(chunks, axis=1))
    return jnp.stack(per_head, axis=0)


def _stack_heads(q):
    lane = lax.broadcasted_iota(jnp.int32, (q.shape[0], LANES), 1)
    rows = []
    for h in range(NSA_HEADS):
        slab = q[:, (h // 2) * LANES:(h // 2 + 1) * LANES]
        src_half, dst_half = h % 2, h // NSA_GROUP
        if src_half != dst_half:
            slab = pltpu.roll(slab, NSA_HEAD_DIM, axis=1)
        rows.append(jnp.where(lane // NSA_HEAD_DIM == dst_half, slab, 0.0))
    return jnp.concatenate(rows, axis=0).astype(BF16)


def _cmp_kernel(g_ref, pe_ref, w1a_ref, w1b_ref, w2_ref, o_ref):
    nb = g_ref.shape[0]
    gw = g_ref.shape[1] // 4
    pe = pe_ref[...]
    slabs = [g_ref[:, s * gw:(s + 1) * gw] for s in range(4)]
    nxt0 = pltpu.roll(slabs[0], nb - 1, axis=0)
    for s in range(4):
        a = (slabs[s] + pe[0:1, :]).astype(BF16)
        bn = ((slabs[s + 1] if s < 3 else nxt0) + pe[1:2, :]).astype(BF16)
        hid = _silu(_dot(a, w1a_ref[...]) + _dot(bn, w1b_ref[...]))
        o_ref[s * nb:(s + 1) * nb, :] = _dot(hid.astype(BF16), w2_ref[...]).astype(o_ref.dtype)


def _compress(g, pe2, w1a, w1b, w2bd):
    B, nb, gw4 = g.shape
    full2 = lambda a: pl.BlockSpec(a.shape, lambda b: (0, 0))
    slab_major = pl.pallas_call(
        _cmp_kernel,
        grid=(B,),
        in_specs=[pl.BlockSpec((None, nb, gw4), lambda b: (b, 0, 0)),
                  full2(pe2), full2(w1a), full2(w1b), full2(w2bd)],
        out_specs=pl.BlockSpec((None, 4 * nb, LANES), lambda b: (b, 0, 0)),
        out_shape=jax.ShapeDtypeStruct((B, 4 * nb, LANES), BF16),
        compiler_params=_params(("parallel",)),
        name="nsa_compress",
    )(g, pe2, w1a, w1b, w2bd)
    return slab_major.reshape(B, 4, nb, LANES).transpose(0, 2, 1, 3).reshape(B, 4 * nb, LANES)


def _cmpattn_kernel(q_ref, kc_ref, vc_ref, band_ref, st_ref, oc_ref, selb_ref):
    tq = q_ref.shape[0]
    ncmp = kc_ref.shape[0]
    nb = ncmp // 4
    R = NSA_HEADS * tq
    q0 = pl.program_id(1) * tq
    nd = band_ref.shape[0]

    def attend(ncv, nbv):
        Q = _stack_heads(q_ref[...])
        s3 = _dot_nt(Q, kc_ref[0:ncv, :]).reshape(NSA_HEADS, tq, ncv)
        parts = []
        for ch in range(ncv // LANES):
            d = jnp.clip((q0 - ch * LANES * NSA_CMP_STRIDE) // tq + CMP_BAND_OFF, 0, nd - 1)
            parts.append(s3[:, :, ch * LANES:(ch + 1) * LANES] + band_ref[d])
        s3 = jnp.concatenate(parts, axis=2)
        mx = jnp.max(s3, axis=-1, keepdims=True)
        e = jnp.exp(s3 - mx)
        live = (q0 + lax.broadcasted_iota(jnp.int32, (tq, 1), 0) >= NSA_CMP_BLOCK - 1)[None]
        p = e * jnp.where(live, 1.0 / jnp.sum(e, axis=-1, keepdims=True), 0.0)
        oc_ref[...] = _dot(p.reshape(R, ncv).astype(BF16), vc_ref[0:ncv, :])

        ps = p.reshape(NSA_KV, NSA_GROUP, tq, ncv).sum(axis=1).reshape(NSA_KV * tq, ncv)
        imp = _exact_right_dot(ps, st_ref[0:ncv, 0:nbv])
        j = lax.broadcasted_iota(jnp.int32, (NSA_KV * tq, nbv), 1)
        t = q0 + (lax.broadcasted_iota(jnp.int32, (NSA_KV * tq, nbv), 0) % tq)
        cur = t // NSA_SLC_BLOCK
        forced = (j == 0) | (j == cur) | (j == cur - 1)
        score = jnp.where(j > cur, -1.0, jnp.where(forced, NSA_GROUP + 1.0, imp))
        score = score.T
        jf = lax.broadcasted_iota(jnp.int32, score.shape, 0).astype(F32)
        sel = score == NSA_GROUP + 1.0
        score = jnp.where(sel, -jnp.inf, score)
        for _ in range(min(NSA_TOP_N, nb) - 3):
            best = jnp.max(score, axis=0, keepdims=True)
            first = jnp.min(jnp.where(score == best, jf, float(nbv)), axis=0, keepdims=True)
            hit = jf == first
            sel = sel | hit
            score = jnp.where(hit, -jnp.inf, score)
        selb = jnp.where(sel, 0.0, NEG).T.astype(selb_ref.dtype)
        for kv in range(NSA_KV):
            selb_ref[:, kv * nb:kv * nb + nbv] = selb[kv * tq:(kv + 1) * tq]
            if nbv < nb:
                selb_ref[:, kv * nb + nbv:(kv + 1) * nb] = jnp.full((tq, nb - nbv), NEG, selb_ref.dtype)

    if ncmp % CMP_STEP:
        attend(ncmp, nb)
    else:
        nvar = ncmp // CMP_STEP
        sizes = [(CMP_STEP * v, min(nb, LANES * ((v + 1) // 2))) for v in range(1, nvar + 1)]
        last_key = (q0 + tq - 1) // NSA_CMP_STRIDE
        lax.switch(jnp.minimum(last_key // CMP_STEP, nvar - 1),
                   [functools.partial(attend, ncv, nbv) for ncv, nbv in sizes])


def _cmpattn(proj, kcmp, vcmp, cband, B, S):
    tq = TQ
    nqt = S // tq
    ncmp = kcmp.shape[1]
    nb = ncmp // 4
    off = np.arange(ncmp)[:, None] - 4 * np.arange(nb)[None, :]
    stencil = np.where((off >= 0) & (off <= 2), 1.0, np.where((off == -1) | (off == 3), 0.5, 0.0))
    stencil = jnp.asarray(stencil, BF16)
    once = pl.Buffered(1)
    return pl.pallas_call(
        _cmpattn_kernel,
        grid=(B, nqt),
        in_specs=[pl.BlockSpec((tq, GROUP_W), lambda b, i: (b * nqt + i, COL["nq"] // 4)),
                  pl.BlockSpec((None, ncmp, LANES), lambda b, i: (b, 0, 0)),
                  pl.BlockSpec((None, ncmp, LANES), lambda b, i: (b, 0, 0)),
                  pl.BlockSpec(cband.shape, lambda b, i: (0, 0, 0, 0), pipeline_mode=once),
                  pl.BlockSpec(stencil.shape, lambda b, i: (0, 0))],
        out_specs=[pl.BlockSpec((NSA_HEADS * tq, LANES), lambda b, i: (b * nqt + i, 0)),
                   pl.BlockSpec((tq, 2 * nb), lambda b, i: (b * nqt + i, 0))],
        out_shape=[jax.ShapeDtypeStruct((B * S * NSA_HEADS, LANES), F32),
                   jax.ShapeDtypeStruct((B * S, 2 * nb), BF16)],
        compiler_params=_params(("parallel", "arbitrary")),
        name="nsa_cmp_attn_topk",
    )(proj, kcmp, vcmp, cband, stencil)


def _band_kernel(rel_ref, o_ref, *, entry_off, key_step, key_end):
    tq = o_ref.shape[1]
    delta = (pl.program_id(0) - entry_off) * tq
    row = lax.broadcasted_iota(jnp.int32, (tq, LANES), 0)
    col = lax.broadcasted_iota(jnp.int32, (tq, LANES), 1)
    dist = delta + row - (col * key_step + key_end)
    bias = _head_bias(_t5_bucket(dist), rel_ref)
    o_ref[...] = jnp.where((dist >= 0)[None], bias, NEG)


def _band_table(rel_t, tq, entry_off, key_step=1, key_end=0):
    nd = -(-(REL_MAX_DIST + key_step * (LANES - 1) + key_end) // tq) + entry_off + 1
    return pl.pallas_call(
        functools.partial(_band_kernel, entry_off=entry_off, key_step=key_step, key_end=key_end),
        grid=(nd,),
        in_specs=[pl.BlockSpec(rel_t.shape, lambda d: (0, 0))],
        out_specs=pl.BlockSpec((None, NSA_HEADS, tq, LANES), lambda d: (d, 0, 0, 0)),
        out_shape=jax.ShapeDtypeStruct((nd, NSA_HEADS, tq, LANES), F32),
        compiler_params=_params(("parallel",)),
        name="nsa_bias_band",
    )(rel_t)


def _selattn_kernel(q_ref, selb_ref, ks_ref, vs_ref, band_ref, sp_ref, farq_ref, os_ref,
                    qaug_ref, msel_ref, s_ref, mt_ref, m_ref, acc_ref):
    tq = q_ref.shape[0]
    R = NSA_HEADS * tq
    nb = selb_ref.shape[1] // 2
    n_kt = msel_ref.shape[0]
    tk = ks_ref.shape[0] // n_kt
    nd = band_ref.shape[0]
    q0 = pl.program_id(1) * tq
    qaug_ref[:, 0:LANES] = _stack_heads(q_ref[...])
    selb2 = jnp.concatenate([selb_ref[:, 0:nb], selb_ref[:, nb:2 * nb]], axis=0)
    spread = _dot(selb2, sp_ref[...])
    for c in range(n_kt):
        msel_ref[c] = spread[:, c * LANES:(c + 1) * LANES].astype(BF16)
    m_ref[...] = jnp.full(m_ref.shape, NEG, F32)
    acc_ref[...] = jnp.zeros_like(acc_ref)
    n_tiles = (q0 + tq - 1) // tk + 1

    far_pairs = (jnp.maximum(q0 - (REL_MAX_DIST - 1), 0) // tk) // 2

    def scores(c, slot, far):
        c = jnp.minimum(c, n_kt - 1)
        k0 = pl.multiple_of(c * tk, tk)
        m2 = msel_ref[c]
        flags = jnp.concatenate([m2[0:tq]] * NSA_GROUP + [m2[tq:2 * tq]] * NSA_GROUP, axis=0)
        qaug_ref[:, LANES:2 * LANES] = flags + farq_ref[...] if far else flags
        s = _dot_nt(qaug_ref[...], ks_ref[pl.ds(k0, tk), :])
        if not far:
            s3 = s.reshape(NSA_HEADS, tq, tk)
            parts = []
            for ch in range(tk // LANES):
                d = jnp.clip((q0 - k0) // tq - ch * (LANES // tq) + BAND_OFF, 0, nd - 1)
                parts.append(s3[:, :, ch * LANES:(ch + 1) * LANES] + band_ref[d])
            s = jnp.concatenate(parts, axis=2).reshape(R, tk)
        s_ref[slot] = s
        mt_ref[slot] = jnp.broadcast_to(jnp.max(s, axis=-1, keepdims=True), (R, LANES))

    def accumulate(c, slot):
        m_old = m_ref[...]
        m_new = jnp.maximum(m_old, mt_ref[slot])
        alpha = jnp.exp(m_old - m_new)
        p = jnp.exp(s_ref[slot] - jnp.tile(m_new, (1, tk // LANES)))
        vt = vs_ref[pl.ds(pl.multiple_of(c * tk, tk), tk), :]
        acc_ref[...] = jnp.tile(alpha, (1, 2)) * acc_ref[...] + _dot(p.astype(BF16), vt)
        m_ref[...] = m_new

    def pair(i, far_odd, far_even):
        c = 2 * i
        scores(c + 1, 1, far_odd)
        accumulate(c, 0)
        scores(c + 2, 0, far_even)
        accumulate(c + 1, 1)

    scores(0, 0, False)
    last_far = jnp.maximum(far_pairs - 1, 0)
    lax.fori_loop(0, last_far, lambda i, carry: pair(i, True, True), None)
    lax.fori_loop(last_far, far_pairs, lambda i, carry: pair(i, True, False), None)
    lax.fori_loop(far_pairs, n_tiles // 2, lambda i, carry: pair(i, False, False), None)

    @pl.when(n_tiles % 2 == 1)
    def _():
        accumulate(n_tiles - 1, 0)

    acc = acc_ref[...]
    os_ref[...] = acc[:, 0:LANES] / acc[:, LANES:2 * LANES]


def _selattn(proj, selb, ks, vs, band, farq, B, S):
    tq = TQ
    tk = min(TK, S)
    n_kt = S // tk
    bpt = tk // NSA_SLC_BLOCK
    nqt = S // tq
    nb = S // NSA_SLC_BLOCK
    R = NSA_HEADS * tq
    sp = np.zeros((nb, n_kt * LANES), np.float32)
    sp[np.arange(nb), (np.arange(nb) // bpt) * LANES + np.arange(nb) % bpt] = 1.0
    once = pl.Buffered(1)
    return pl.pallas_call(
        _selattn_kernel,
        grid=(B, nqt),
        in_specs=[pl.BlockSpec((tq, GROUP_W), lambda b, i: (b * nqt + i, COL["nq"] // 4)),
                  pl.BlockSpec((tq, 2 * nb), lambda b, i: (b * nqt + i, 0)),
                  pl.BlockSpec((None, S, 2 * LANES), lambda b, i: (b, 0, 0), pipeline_mode=once),
                  pl.BlockSpec((None, S, 2 * LANES), lambda b, i: (b, 0, 0), pipeline_mode=once),
                  pl.BlockSpec(band.shape, lambda b, i: (0, 0, 0, 0), pipeline_mode=once),
                  pl.BlockSpec(sp.shape, lambda b, i: (0, 0)),
                  pl.BlockSpec(farq.shape, lambda b, i: (0, 0))],
        out_specs=pl.BlockSpec((R, LANES), lambda b, i: (b * nqt + i, 0)),
        out_shape=jax.ShapeDtypeStruct((B * S * NSA_HEADS, LANES), F32),
        scratch_shapes=[pltpu.VMEM((R, 2 * LANES), BF16), pltpu.VMEM((n_kt, 2 * tq, LANES), BF16),
                        pltpu.VMEM((2, R, tk), F32), pltpu.VMEM((2, R, LANES), F32),
                        pltpu.VMEM((R, LANES), F32), pltpu.VMEM((R, 2 * LANES), F32)],
        compiler_params=_params(("parallel", "arbitrary")),
        name="nsa_sel_attn",
    )(proj, selb, ks, vs, band, jnp.asarray(sp, BF16), farq)


def _winattn_kernel(q_ref, sm_ref, oc_ref, os_ref, kw_ref, vw_ref, band_ref, ng_ref, o_ref):
    tq = q_ref.shape[1]
    R = NSA_HEADS * tq
    S = kw_ref.shape[1]
    span = min(WIN_SPAN, S)
    nd = band_ref.shape[0]
    q0 = pl.program_id(0) * tq
    start = pl.multiple_of(jnp.clip(q0 + tq - span, 0, S - span), tq)
    row = lax.broadcasted_iota(jnp.int32, (tq, span), 0)
    col = lax.broadcasted_iota(jnp.int32, (tq, span), 1)
    in_window = ((q0 + row) - (start + col) < NSA_WINDOW)[None]
    lane = lax.broadcasted_iota(jnp.int32, (tq, LANES), 1)
    for b in range(q_ref.shape[0]):
        Q = _stack_heads(q_ref[b])
        kt = kw_ref[b, pl.ds(start, span), :]
        vt = vw_ref[b, pl.ds(start, span), :]
        s3 = _dot_nt(Q, kt).reshape(NSA_HEADS, tq, span)
        parts = []
        for ch in range(span // LANES):
            d = jnp.clip((q0 - start) // tq - ch * (LANES // tq) + BAND_OFF, 0, nd - 1)
            parts.append(s3[:, :, ch * LANES:(ch + 1) * LANES] + band_ref[d])
        s3 = jnp.where(in_window, jnp.concatenate(parts, axis=2), NEG)
        mx = jnp.max(s3, axis=-1, keepdims=True)
        e = jnp.exp(s3 - mx)
        p = e * (1.0 / jnp.sum(e, axis=-1, keepdims=True))
        ow = _dot(p.reshape(R, span).astype(BF16), vt)

        gates = _sigmoid(sm_ref[b])
        heads = []
        ssq = jnp.zeros((tq, 1), F32)
        for h in range(NSA_HEADS):
            rs = slice(h * tq, (h + 1) * tq)
            g = [gates[:, GATE_LANE0 + 3 * h + br:GATE_LANE0 + 3 * h + br + 1] for br in range(3)]
            oh = g[0] * oc_ref[b, rs, :] + g[1] * os_ref[b, rs, :] + g[2] * ow[rs, :]
            kv = h // NSA_GROUP
            valid = (lane >= kv * NSA_HEAD_DIM) & (lane < (kv + 1) * NSA_HEAD_DIM)
            oh = jnp.where(valid, oh, 0.0)
            ssq = ssq + jnp.sum(oh * oh, axis=-1, keepdims=True)
            heads.append(oh)
        rinv = lax.rsqrt(ssq / GROUP_W + 1e-6)
        o_ref[b] = (jnp.concatenate(heads, axis=1) * rinv * ng_ref[...]).astype(o_ref.dtype)


def _winattn(proj, small, oc, os_, kw, vw, band, ngw, B, S):
    tq = TQ
    R = NSA_HEADS * tq
    once = pl.Buffered(1)
    per_b = lambda t: t.reshape(B, t.shape[0] // B, t.shape[1])
    return pl.pallas_call(
        _winattn_kernel,
        grid=(S // tq,),
        in_specs=[pl.BlockSpec((B, tq, GROUP_W), lambda i: (0, i, COL["nq"] // 4)),
                  pl.BlockSpec((B, tq, LANES), lambda i: (0, i, 0)),
                  pl.BlockSpec((B, R, LANES), lambda i: (0, i, 0)),
                  pl.BlockSpec((B, R, LANES), lambda i: (0, i, 0)),
                  pl.BlockSpec((B, S, LANES), lambda i: (0, 0, 0), pipeline_mode=once),
                  pl.BlockSpec((B, S, LANES), lambda i: (0, 0, 0), pipeline_mode=once),
                  pl.BlockSpec(band.shape, lambda i: (0, 0, 0, 0), pipeline_mode=once),
                  pl.BlockSpec(ngw.shape, lambda i: (0, 0))],
        out_specs=pl.BlockSpec((B, tq, NSA_HEADS * LANES), lambda i: (0, i, 0)),
        out_shape=jax.ShapeDtypeStruct((B, S, NSA_HEADS * LANES), BF16),
        compiler_params=_params(("arbitrary",)),
        name="nsa_win_attn_merge",
    )(per_b(proj), per_b(small), per_b(oc), per_b(os_), kw, vw, band, ngw).reshape(B * S, NSA_HEADS * LANES)


def _widen_heads(x, axis):
    x = jnp.moveaxis(x, axis, -1)
    lead = x.shape[:-1]
    x = x.reshape(*lead, NSA_KV, NSA_GROUP, 1, NSA_HEAD_DIM)
    sel = jnp.eye(NSA_KV, dtype=x.dtype).reshape(NSA_KV, 1, NSA_KV, 1)
    x = (x * sel).reshape(*lead, NSA_HEADS * LANES)
    return jnp.moveaxis(x, -1, axis)


def _build_w_in(w):
    (hq, hf, hi, hg, nq, nkc, nvc, nks, nvs, nkw, nvw, ngate,
     sz, sxbc, sdt, rq, rk, rv, rg) = jnp.split(w, IN_SPLITS, axis=1)
    D = w.shape[0]
    nq = nq * NSA_HEAD_DIM ** -0.5
    deint = lambda t: t.reshape(D, RET_HEADS, RET_DK // 2, 2).transpose(0, 1, 3, 2).reshape(D, GROUP_W)
    small = jnp.concatenate([ngate, sdt, jnp.zeros((D, LANES - 32), w.dtype)], axis=1)
    wide = [hq, hf, hi, hg, sxbc, nq, sz, deint(rq), deint(rk), rv, rg]
    narrow = [nkc, nvc, nks, nvs, nkw, nvw, small]
    return jnp.concatenate(wide, axis=1).astype(BF16), jnp.concatenate(narrow, axis=1).astype(BF16)


def _build_cmp_weights(pe, w1, w2):
    w1r = w1.reshape(2, NSA_CMP_STRIDE, NSA_HEAD_DIM, NSA_CMP_HIDDEN)
    eye = jnp.eye(NSA_KV, dtype=w1.dtype)
    big = jnp.einsum("ardc,kj->arkdjc", w1r, eye).reshape(
        2, NSA_CMP_STRIDE * NSA_KV * NSA_HEAD_DIM, NSA_KV * NSA_CMP_HIDDEN)
    w2bd = jnp.einsum("cd,kj->kcjd", w2, eye).reshape(NSA_KV * NSA_CMP_HIDDEN, NSA_KV * NSA_HEAD_DIM)
    per = pe.reshape(2, NSA_CMP_STRIDE, 1, NSA_HEAD_DIM)
    pe2 = jnp.broadcast_to(per, (2, NSA_CMP_STRIDE, NSA_KV, NSA_HEAD_DIM)).reshape(2, -1)
    return pe2, big[0].astype(BF16), big[1].astype(BF16), w2bd.astype(BF16)


def _rotary_tables(S):
    half = RET_DK // 2
    theta = 1.0 / (10000.0 ** jnp.linspace(0.0, 1.0, half, dtype=F32))
    ang = jnp.arange(S, dtype=F32)[:, None] * theta[None, :]
    cos, sin = jnp.cos(ang), jnp.sin(ang)
    return jnp.concatenate([cos, cos], axis=1), jnp.concatenate([-sin, sin], axis=1)


def _mixer(x2, B, S, l, p, lower_bounds, band, cband, farq, cos_t, sin_t):
    T = B * S
    w_wide, w_narrow = _build_w_in(p["w_in"][l])
    tn = NCOL * LANES // 4
    proj = _proj(x2, w_wide.reshape(-1, 4, tn).transpose(1, 0, 2), tm=min(1024, T))
    kc, vc, ks, vs, kw, vw, small = _kvproj(x2, w_narrow, tm=min(512, T))
    row = lambda v: v.reshape(1, -1).astype(F32)

    lb = lower_bounds[l].astype(F32)
    o_a = _hgrn(proj, B, S, row(jnp.log(lb)), row(jnp.log1p(-lb)), row(1.0 - lb),
                row(p["hgrn_norm_g"][l]))

    nb = S // NSA_SLC_BLOCK
    grp = lambda t: t.reshape(B, nb, 4 * NSA_CMP_STRIDE * LANES)
    kcmp = _compress(grp(kc), *_build_cmp_weights(p["nsa_pe_k"][l], p["nsa_w1_k"][l], p["nsa_w2_k"][l]))
    vcmp = _compress(grp(vc), *_build_cmp_weights(p["nsa_pe_v"][l], p["nsa_w1_v"][l], p["nsa_w2_v"][l]))
    o_cmp, selb = _cmpattn(proj, kcmp, vcmp, cband, B, S)
    seq = lambda t: t.reshape(B, S, t.shape[1])
    o_sel = _selattn(proj, selb, seq(ks), seq(vs), band, farq, B, S)
    ngw = _widen_heads(p["nsa_norm_g"][l].astype(F32), 0).reshape(1, -1)
    o_b = _winattn(proj, small, o_cmp, o_sel, seq(kw), seq(vw), band, ngw, B, S)

    lane_vec = lambda v: jnp.zeros((1, LANES), F32).at[0, DT_LANE0:DT_LANE0 + SSM_HEADS].set(v.astype(F32))
    o_c = _ssd(proj, small, B, S, p["ssm_conv_w"][l].astype(F32), row(p["ssm_conv_b"][l]),
               lane_vec(p["ssm_dt_bias"][l]), lane_vec(-jnp.exp(p["ssm_a_log"][l].astype(F32))),
               row(jnp.repeat(p["ssm_d"][l].astype(F32), SSM_HEAD_DIM)), row(p["ssm_norm_g"][l]))

    o_d = _retention(proj, B, S, cos_t, sin_t)

    w_out = p["w_out"][l]
    wa, wb, wc, wd = (w_out[i * GROUP_W:(i + 1) * GROUP_W] for i in range(4))
    return o_a, o_b, o_c, o_d, wa.astype(BF16), _widen_heads(wb, 0).astype(BF16), wc.astype(BF16), wd.astype(BF16)


def kernel(x, ln1_g, ln1_b, ffn1_w1, ffn1_w3, ffn1_w2, ln2_g, ln2_b, w_in, w_out, hgrn_lb_logits, hgrn_norm_g, nsa_pe_k, nsa_w1_k, nsa_w2_k, nsa_pe_v, nsa_w1_v, nsa_w2_v, nsa_norm_g, rel_bias, ssm_conv_w, ssm_conv_b, ssm_dt_bias, ssm_a_log, ssm_d, ssm_norm_g, ln3_g, ln3_b, ffn2_w1, ffn2_w3, ffn2_w2):
    B, S, D = x.shape
    T = B * S
    depth = w_in.shape[0]
    p = dict(w_in=w_in, w_out=w_out, hgrn_norm_g=hgrn_norm_g, nsa_pe_k=nsa_pe_k, nsa_w1_k=nsa_w1_k,
             nsa_w2_k=nsa_w2_k, nsa_pe_v=nsa_pe_v, nsa_w1_v=nsa_w1_v, nsa_w2_v=nsa_w2_v,
             nsa_norm_g=nsa_norm_g, ssm_conv_w=ssm_conv_w, ssm_conv_b=ssm_conv_b,
             ssm_dt_bias=ssm_dt_bias, ssm_a_log=ssm_a_log, ssm_d=ssm_d, ssm_norm_g=ssm_norm_g)
    cum = jnp.cumsum(jax.nn.softmax(hgrn_lb_logits.astype(F32), axis=0), axis=0)
    lower_bounds = cum - cum[:1]
    rel_t = jnp.zeros((NSA_HEADS, LANES), F32).at[:, :REL_BUCKETS].set(rel_bias.astype(F32).T)
    band = _band_table(rel_t, TQ, BAND_OFF)
    cband = _band_table(rel_t, TQ, CMP_BAND_OFF, NSA_CMP_STRIDE, NSA_CMP_BLOCK - 1)
    bpt = TK // NSA_SLC_BLOCK
    farq = jnp.zeros((NSA_HEADS, LANES), BF16).at[:, bpt:bpt + 3].set(
        jnp.stack(_split3(rel_bias.astype(F32)[REL_BUCKETS - 1]), axis=1))
    farq = jnp.repeat(farq, TQ, axis=0)
    cos_t, sin_t = _rotary_tables(S)
    row = lambda v: v.reshape(1, -1).astype(F32)
    tm = min(512, T)
    tf = 512 if ffn1_w1.shape[2] % 512 == 0 else ffn1_w1.shape[2]
    x2 = x.reshape(T, D).astype(F32)
    for l in range(depth):
        x2 = _ffn(x2, _to_bf16_col_blocks(ffn1_w1, l, tf), _to_bf16_col_blocks(ffn1_w3, l, tf),
                  _to_bf16(ffn1_w2, l), row(ln1_g[l]), row(ln1_b[l]), tm)
        o_a, o_b, o_c, o_d, wa, wb, wc, wd = _mixer(x2, B, S, l, p, lower_bounds, band, cband, farq, cos_t, sin_t)
        x2 = _outproj(x2, o_a, o_b, o_c, o_d, wa, wb, wc, wd, row(ln2_g[l]), row(ln2_b[l]), min(512, T))
        x2 = _ffn(x2, _to_bf16_col_blocks(ffn2_w1, l, tf), _to_bf16_col_blocks(ffn2_w3, l, tf),
                  _to_bf16(ffn2_w2, l), row(ln3_g[l]), row(ln3_b[l]), tm)
    return x2.reshape(B, S, D).astype(x.dtype)
```

```python
import functools
import math

import numpy as np
import jax
import jax.numpy as jnp
from jax import lax
from jax.experimental import pallas as pl
from jax.experimental.pallas import tpu as pltpu

F32 = jnp.float32
BF16 = jnp.bfloat16

D_MODEL = 2048
DEPTH = 2
GROUP_W = 512
ALPHA = (2 * DEPTH) ** 0.25
HG_HEADS = 4
NSA_HEADS = 8
NSA_KV = 2
NSA_GROUP = 4
NSA_HEAD_DIM = 64
NSA_CMP_STRIDE = 16
NSA_CMP_BLOCK = 32
NSA_SLC_BLOCK = 64
NSA_TOP_N = 16
NSA_WINDOW = 512
NSA_CMP_HIDDEN = 256
SSM_HEADS = 8
SSM_HEAD_DIM = 64
SSM_GROUPS = 2
SSM_STATE = 128
SSM_CONV = 4
RET_HEADS = 4
RET_DK = 128
REL_BUCKETS = 32
REL_EXACT = 16
REL_MAX_DIST = 2048
IN_SIZES = ((GROUP_W,) * 4 + (GROUP_W,) + (128,) * 6 + (24,)
            + (GROUP_W, 1024, SSM_HEADS) + (GROUP_W,) * 4)
IN_SPLITS = tuple(int(v) for v in np.cumsum(IN_SIZES)[:-1])

LANES = 128
SUBLANES = 8
VMEM_LIMIT = 56 * 1024 * 1024
CAST_BLOCK_BYTES = 4 * 1024 * 1024

COL = dict(hq=0, hf=4, hi=8, hg=12, sxbc=16, nq=24, sz=28, rq=32, rk=36, rv=40, rg=44)
NCOL = 48
KV_COLS = ("nkc", "nvc", "nks", "nvs", "nkw", "nvw", "small")
GATE_LANE0 = 0
DT_LANE0 = 24

CHUNK = 128
TQ = 128
TK = 1024
BAND_OFF = 2
CMP_BAND_OFF = 1
CMP_STEP = 256
WIN_SPAN = NSA_WINDOW + 2 * TQ
NEG = -1e30


def _params(sem):
    return pltpu.CompilerParams(dimension_semantics=sem, vmem_limit_bytes=VMEM_LIMIT)


def _dot(a, b):
    return jnp.dot(a, b, preferred_element_type=F32)


def _dot_nt(a, b):
    return lax.dot_general(a, b, (((1,), (1,)), ((), ())), preferred_element_type=F32)


def _split3(x):
    hi = x.astype(BF16)
    r1 = x - hi.astype(F32)
    mid = r1.astype(BF16)
    return hi, mid, (r1 - mid.astype(F32)).astype(BF16)


def _exact_left_dot(w, x):
    n = x.shape[1]
    y = _dot(w, jnp.concatenate(_split3(x), axis=1))
    return y[:, 0:n] + y[:, n:2 * n] + y[:, 2 * n:3 * n]


def _exact_right_dot(x, w):
    n = x.shape[0]
    y = _dot(jnp.concatenate(_split3(x), axis=0), w)
    return y[0:n] + y[n:2 * n] + y[2 * n:3 * n]


def _sigmoid(x):
    return 1.0 / (1.0 + jnp.exp(-x))


def _silu(x):
    return x * _sigmoid(x)


def _softplus(x):
    return jnp.maximum(x, 0.0) + jnp.log1p(jnp.exp(-jnp.abs(x)))


def _layer_norm(r, g, b):
    mu = jnp.mean(r, axis=-1, keepdims=True)
    d = r - mu
    var = jnp.mean(d * d, axis=-1, keepdims=True)
    return d * lax.rsqrt(var + 1e-5) * g + b


def _cast_kernel(x_ref, o_ref):
    o_ref[...] = x_ref[...].astype(o_ref.dtype)


def _to_bf16(w, l):
    _, r, c = w.shape
    tr = min(r, max(16, CAST_BLOCK_BYTES // (4 * c) // 16 * 16))
    while r % tr:
        tr -= 16
    return pl.pallas_call(
        _cast_kernel,
        grid=(r // tr,),
        in_specs=[pl.BlockSpec((None, tr, c), lambda i: (l, i, 0))],
        out_specs=pl.BlockSpec((tr, c), lambda i: (i, 0)),
        out_shape=jax.ShapeDtypeStruct((r, c), BF16),
        compiler_params=_params(("parallel",)),
        name="cast_bf16",
    )(w)


def _to_bf16_col_blocks(w, l, tc):
    _, r, c = w.shape
    return pl.pallas_call(
        _cast_kernel,
        grid=(c // tc,),
        in_specs=[pl.BlockSpec((None, r, tc), lambda j: (l, 0, j))],
        out_specs=pl.BlockSpec((None, r, tc), lambda j: (j, 0, 0)),
        out_shape=jax.ShapeDtypeStruct((c // tc, r, tc), BF16),
        compiler_params=_params(("parallel",)),
        name="cast_bf16_blocked",
    )(w)


def _ffn_kernel(x_ref, w1_ref, w3_ref, w2_ref, g_ref, b_ref, o_ref, acc_ref, xb_ref):
    j = pl.program_id(1)

    @pl.when(j == 0)
    def _():
        xb_ref[...] = x_ref[...].astype(BF16)
        acc_ref[...] = jnp.zeros_like(acc_ref)

    xb = xb_ref[...]
    h1 = _dot(xb, w1_ref[...])
    h3 = _dot(xb, w3_ref[...])
    a = (_silu(h1) * h3).astype(BF16)
    acc_ref[...] += _dot(a, w2_ref[...])

    @pl.when(j == pl.num_programs(1) - 1)
    def _():
        r = ALPHA * x_ref[...] + 0.5 * acc_ref[...]
        o_ref[...] = _layer_norm(r, g_ref[...], b_ref[...])


def _ffn(x, w1, w3, w2, g, b, tm):
    T, D = x.shape
    nf, _, tf = w1.shape
    return pl.pallas_call(
        _ffn_kernel,
        grid=(T // tm, nf),
        in_specs=[
            pl.BlockSpec((tm, D), lambda i, j: (i, 0)),
            pl.BlockSpec((None, D, tf), lambda i, j: (j, 0, 0)),
            pl.BlockSpec((None, D, tf), lambda i, j: (j, 0, 0)),
            pl.BlockSpec((tf, D), lambda i, j: (j, 0)),
            pl.BlockSpec((1, D), lambda i, j: (0, 0)),
            pl.BlockSpec((1, D), lambda i, j: (0, 0)),
        ],
        out_specs=pl.BlockSpec((tm, D), lambda i, j: (i, 0)),
        out_shape=jax.ShapeDtypeStruct((T, D), F32),
        scratch_shapes=[pltpu.VMEM((tm, D), F32), pltpu.VMEM((tm, D), BF16)],
        compiler_params=_params(("parallel", "arbitrary")),
        name="ffn_ln",
    )(x, w1, w3, w2, g, b)


def _proj_kernel(x_ref, w_ref, o_ref, xb_ref):
    @pl.when(pl.program_id(1) == 0)
    def _():
        xb_ref[...] = x_ref[...].astype(BF16)

    o_ref[...] = _dot(xb_ref[...], w_ref[...])


def _proj(x, w, tm):
    T, D = x.shape
    nn, _, tn = w.shape
    return pl.pallas_call(
        _proj_kernel,
        grid=(T // tm, nn),
        in_specs=[pl.BlockSpec((tm, D), lambda i, j: (i, 0)),
                  pl.BlockSpec((None, D, tn), lambda i, j: (j, 0, 0))],
        out_specs=pl.BlockSpec((tm, tn), lambda i, j: (i, j)),
        out_shape=jax.ShapeDtypeStruct((T, nn * tn), F32),
        scratch_shapes=[pltpu.VMEM((tm, D), BF16)],
        compiler_params=_params(("parallel", "arbitrary")),
        name="in_proj",
    )(x, w)


def _kvproj_kernel(x_ref, w_ref, kc_ref, vc_ref, ks_ref, vs_ref, kw_ref, vw_ref, sm_ref):
    tm = x_ref.shape[0]
    y = _dot(x_ref[...].astype(BF16), w_ref[...])
    piece = lambda n: y[:, n * LANES:(n + 1) * LANES]
    kc_ref[...] = piece(0)
    vc_ref[...] = piece(1)
    row = pl.program_id(0) * tm + lax.broadcasted_iota(jnp.int32, (tm, LANES), 0)
    lane = lax.broadcasted_iota(jnp.int32, (tm, LANES), 1)
    bpt = TK // NSA_SLC_BLOCK
    onehot = (((row // NSA_SLC_BLOCK) % bpt == lane) | ((lane >= bpt) & (lane < bpt + 3))).astype(BF16)
    ks_ref[...] = jnp.concatenate([piece(2).astype(BF16), onehot], axis=1)
    vs_ref[...] = jnp.concatenate([piece(3).astype(BF16), jnp.ones((tm, LANES), BF16)], axis=1)
    kw_ref[...] = piece(4).astype(BF16)
    vw_ref[...] = piece(5).astype(BF16)
    sm_ref[...] = piece(6)


def _kvproj(x, w, tm):
    T, D = x.shape
    narrow = lambda dt, width=LANES: (pl.BlockSpec((tm, width), lambda i: (i, 0)),
                                      jax.ShapeDtypeStruct((T, width), dt))
    outs = [narrow(F32), narrow(F32), narrow(BF16, 2 * LANES), narrow(BF16, 2 * LANES),
            narrow(BF16), narrow(BF16), narrow(F32)]
    return pl.pallas_call(
        _kvproj_kernel,
        grid=(T // tm,),
        in_specs=[pl.BlockSpec((tm, D), lambda i: (i, 0)), pl.BlockSpec(w.shape, lambda i: (0, 0))],
        out_specs=[o[0] for o in outs],
        out_shape=[o[1] for o in outs],
        compiler_params=_params(("parallel",)),
        name="kv_proj",
    )(x, w)


def _outproj_kernel(x_ref, oa_ref, ob_ref, oc_ref, od_ref, wa_ref, wb_ref, wc_ref, wd_ref,
                    g_ref, b_ref, o_ref):
    mix = (_dot(oa_ref[...], wa_ref[...]) + _dot(ob_ref[...], wb_ref[...])
           + _dot(oc_ref[...], wc_ref[...]) + _dot(od_ref[...], wd_ref[...]))
    o_ref[...] = _layer_norm(ALPHA * x_ref[...] + mix, g_ref[...], b_ref[...])


def _outproj(x, oa, ob, oc, od, wa, wb, wc, wd, g, b, tm):
    T, D = x.shape
    row = lambda a: pl.BlockSpec((tm, a.shape[1]), lambda i: (i, 0))
    full = lambda a: pl.BlockSpec(a.shape, lambda i: (0, 0))
    return pl.pallas_call(
        _outproj_kernel,
        grid=(T // tm,),
        in_specs=[row(x), row(oa), row(ob), row(oc), row(od),
                  full(wa), full(wb), full(wc), full(wd), full(g), full(b)],
        out_specs=row(x),
        out_shape=jax.ShapeDtypeStruct((T, D), F32),
        compiler_params=_params(("parallel",)),
        name="out_proj_ln",
    )(x, oa, ob, oc, od, wa, wb, wc, wd, g, b)


def _hgrn_tables(C):
    i = np.arange(C)[:, None]
    ip = np.arange(C)[None, :]
    seg = [(ip <= i),
           (ip > i)]
    masks = [np.eye(C, dtype=bool)]
    s = C // 2
    while s >= 1:
        blk = i // s
        if s < SUBLANES:
            seg.append(np.where(blk % 2 == 1, (ip > blk * s) & (ip <= i), (ip > i) & (ip <= (blk + 1) * s)))
        masks.append((blk % 2 == 1) & (ip // s == blk - 1))
        s //= 2
    seg = np.concatenate([x.astype(np.float32) for x in seg], axis=0)
    return seg, np.stack([m.astype(np.float32) for m in masks])


def _hgrn_kernel(q_ref, f_ref, i_ref, g_ref, llb_ref, l1m_ref, oml_ref, ng_ref,
                 seg_ref, msk_ref, o_ref, st_ref):
    @pl.when(pl.program_id(0) == 0)
    def _():
        st_ref[...] = jnp.zeros_like(st_ref)

    C = q_ref.shape[1]
    nlev = msk_ref.shape[0] - 1
    for bi in range(q_ref.shape[0]):
        q = _silu(q_ref[bi])
        z = f_ref[bi]
        log_sig = jnp.minimum(z, 0.0) - jnp.log1p(jnp.exp(-jnp.abs(z)))
        cc = l1m_ref[...] + log_sig
        llb = llb_ref[...]
        logf = jnp.maximum(llb, cc) + jnp.log1p(jnp.exp(-jnp.abs(llb - cc)))
        k = oml_ref[...] * _sigmoid(-z)
        v = i_ref[bi]
        seg = _exact_left_dot(seg_ref[...], logf)
        b_all = seg[0:C]
        lev = []
        s = C // 2
        while s >= SUBLANES:
            b3 = b_all.reshape(C // s, s, GROUP_W)
            start = b3[:, 0:1, :]
            nxt = jnp.concatenate([start[1:], start[-1:]], axis=0)
            odd = lax.broadcasted_iota(jnp.int32, b3.shape, 0) % 2 == 1
            lev.append(jnp.where(odd, b3 - start, nxt - b3).reshape(C, GROUP_W))
            s //= 2
        lev += [seg[r * C:(r + 1) * C] for r in range(2, seg.shape[0] // C)]
        outs = []
        for h in range(HG_HEADS):
            sl = slice(h * LANES, (h + 1) * LANES)
            qh, kh, vh = q[:, sl], k[:, sl], v[:, sl]
            a = msk_ref[0] * _dot_nt(qh.astype(BF16), kh.astype(BF16))
            for l in range(nlev):
                dec = jnp.exp(lev[l][:, sl])
                a = a + msk_ref[1 + l] * _dot_nt((qh * dec).astype(BF16), (kh * dec).astype(BF16))
            b = seg[0:C, sl]
            st = st_ref[bi, h]
            o = _dot(a.astype(BF16), vh.astype(BF16))
            o = o + _dot_nt((qh * jnp.exp(b)).astype(BF16), st.astype(BF16))
            kd = (kh * jnp.exp(seg[C:2 * C, sl])).astype(BF16)
            st_ref[bi, h] = st * jnp.exp(b[C - 1:C, :]) + _dot(vh.T.astype(BF16), kd)
            outs.append(o * lax.rsqrt(jnp.mean(o * o, axis=-1, keepdims=True) + 1e-6))
        o = jnp.concatenate(outs, axis=1)
        o_ref[bi] = (o * ng_ref[...] * _silu(g_ref[bi])).astype(o_ref.dtype)


def _hgrn(proj, B, S, llb, l1m, oml, ng):
    C = CHUNK
    nc = S // C
    seg, msk = _hgrn_tables(C)
    seg, msk = jnp.asarray(seg, BF16), jnp.asarray(msk)
    proj3 = proj.reshape(B, S, proj.shape[1])
    col = lambda name: pl.BlockSpec((B, C, GROUP_W), lambda c, n=COL[name] // 4: (0, c, n))
    vec = pl.BlockSpec((1, GROUP_W), lambda c: (0, 0))
    full2 = lambda a: pl.BlockSpec(a.shape, lambda c: (0, 0))
    return pl.pallas_call(
        _hgrn_kernel,
        grid=(nc,),
        in_specs=[col("hq"), col("hf"), col("hi"), col("hg"), vec, vec, vec, vec,
                  full2(seg), pl.BlockSpec(msk.shape, lambda c: (0, 0, 0))],
        out_specs=pl.BlockSpec((B, C, GROUP_W), lambda c: (0, c, 0)),
        out_shape=jax.ShapeDtypeStruct((B, S, GROUP_W), BF16),
        scratch_shapes=[pltpu.VMEM((B, HG_HEADS, LANES, LANES), F32)],
        compiler_params=_params(("arbitrary",)),
        name="hgrn2",
    )(proj3, proj3, proj3, proj3, llb, l1m, oml, ng, seg, msk).reshape(B * S, GROUP_W)


def _ssd_kernel(z_ref, xbc_ref, sm_ref, cw_ref, cb_ref, dtb_ref, aneg_ref, dsk_ref, ng_ref,
                ex_ref, o_ref, tail_ref, st_ref):
    @pl.when(pl.program_id(0) == 0)
    def _():
        tail_ref[...] = jnp.zeros_like(tail_ref)
        st_ref[...] = jnp.zeros_like(st_ref)

    for b in range(xbc_ref.shape[0]):
        _ssd_chunk(z_ref.at[b], xbc_ref.at[b], sm_ref.at[b], cw_ref, cb_ref, dtb_ref, aneg_ref, dsk_ref, ng_ref,
                   ex_ref, o_ref.at[b], tail_ref.at[b], st_ref.at[b])


def _ssd_chunk(z_ref, xbc_ref, sm_ref, cw_ref, cb_ref, dtb_ref, aneg_ref, dsk_ref, ng_ref,
               ex_ref, o_ref, tail_ref, st_ref):
    L = xbc_ref.shape[0]
    x = xbc_ref[...]
    xe = jnp.concatenate([tail_ref[...], x], axis=0)
    cw = cw_ref[...]
    conv = cb_ref[...]
    for kk in range(SSM_CONV):
        conv = conv + cw[kk:kk + 1, :] * xe[5 + kk:5 + kk + L, :]
    tail_ref[...] = x[L - 8:L, :]
    conv = _silu(conv)
    xs = conv[:, 0:GROUP_W]
    bm = conv[:, GROUP_W:GROUP_W + 256]
    cm = conv[:, GROUP_W + 256:GROUP_W + 512]

    dtf = _softplus(sm_ref[...] + dtb_ref[...])
    la = dtf * aneg_ref[...]
    ri = lax.broadcasted_iota(jnp.int32, (L, L), 0)
    ci = lax.broadcasted_iota(jnp.int32, (L, L), 1)
    tri = ri >= ci
    bfull = _exact_left_dot(tri.astype(BF16), la)
    ex = ex_ref[...]
    bexp = _exact_right_dot(bfull, ex)
    dtexp = _exact_right_dot(dtf, ex)
    b_t = bfull.T
    xdt = xs * dtexp
    lane = lax.broadcasted_iota(jnp.int32, (L, LANES), 1)

    scores = []
    for g in range(SSM_GROUPS):
        cg = cm[:, g * SSM_STATE:(g + 1) * SSM_STATE].astype(BF16)
        bg = bm[:, g * SSM_STATE:(g + 1) * SSM_STATE].astype(BF16)
        cb = _dot_nt(cg, bg)
        for hh in range(SSM_HEADS // SSM_GROUPS):
            h = g * (SSM_HEADS // SSM_GROUPS) + hh
            bcol = bfull[:, DT_LANE0 + h:DT_LANE0 + h + 1]
            brow = b_t[DT_LANE0 + h:DT_LANE0 + h + 1, :]
            dec = jnp.exp(jnp.where(tri, bcol - brow, NEG))
            scores.append((cb * dec).astype(BF16))
    y_pairs = []
    for u in range(SSM_HEADS // 2):
        slab = xdt[:, u * LANES:(u + 1) * LANES]
        lo = jnp.where(lane < SSM_HEAD_DIM, slab, 0.0).astype(BF16)
        hi = jnp.where(lane >= SSM_HEAD_DIM, slab, 0.0).astype(BF16)
        y_pairs.append(_dot(scores[2 * u], lo) + _dot(scores[2 * u + 1], hi))
    y_intra = jnp.concatenate(y_pairs, axis=1)

    blast = bexp[L - 1:L, :]
    w = (xdt * jnp.exp(blast - bexp)).astype(BF16)
    y_inter = []
    for g in range(SSM_GROUPS):
        gs = slice(g * 256, (g + 1) * 256)
        cg = cm[:, g * SSM_STATE:(g + 1) * SSM_STATE].astype(BF16)
        st = st_ref[g]
        y_inter.append(_dot(cg, st.astype(BF16)))
        bg_t = bm[:, g * SSM_STATE:(g + 1) * SSM_STATE].T.astype(BF16)
        st_ref[g] = st * jnp.exp(blast[:, gs]) + _dot(bg_t, w[:, gs])
    y = y_intra + jnp.concatenate(y_inter, axis=1) * jnp.exp(bexp) + dsk_ref[...] * xs
    y = y * _silu(z_ref[...])
    halves = []
    for g in range(SSM_GROUPS):
        seg = y[:, g * 256:(g + 1) * 256]
        halves.append(seg * lax.rsqrt(jnp.mean(seg * seg, axis=-1, keepdims=True) + 1e-6))
    o_ref[...] = (jnp.concatenate(halves, axis=1) * ng_ref[...]).astype(o_ref.dtype)


def _ssd(proj, small, B, S, cw, cb, dtb, aneg, dsk, ng):
    L = CHUNK
    nc = S // L
    ex = np.zeros((LANES, GROUP_W), np.float32)
    for h in range(SSM_HEADS):
        ex[DT_LANE0 + h, h * SSM_HEAD_DIM:(h + 1) * SSM_HEAD_DIM] = 1.0
    ex = jnp.asarray(ex, BF16)
    full2 = lambda a: pl.BlockSpec(a.shape, lambda c: (0, 0))
    proj3 = proj.reshape(B, S, proj.shape[1])
    return pl.pallas_call(
        _ssd_kernel,
        grid=(nc,),
        in_specs=[
            pl.BlockSpec((B, L, GROUP_W), lambda c: (0, c, COL["sz"] // 4)),
            pl.BlockSpec((B, L, 1024), lambda c: (0, c, COL["sxbc"] // 8)),
            pl.BlockSpec((B, L, LANES), lambda c: (0, c, 0)),
            full2(cw), full2(cb), full2(dtb), full2(aneg), full2(dsk), full2(ng), full2(ex)],
        out_specs=pl.BlockSpec((B, L, GROUP_W), lambda c: (0, c, 0)),
        out_shape=jax.ShapeDtypeStruct((B, S, GROUP_W), BF16),
        scratch_shapes=[pltpu.VMEM((B, 8, 1024), F32), pltpu.VMEM((B, SSM_GROUPS, SSM_STATE, 256), F32)],
        compiler_params=_params(("arbitrary",)),
        name="ssd",
    )(proj3, proj3, small.reshape(B, S, LANES), cw, cb, dtb, aneg, dsk, ng, ex).reshape(B * S, GROUP_W)


def _ret_kernel(q_ref, k_ref, v_ref, g_ref, cos_ref, sin_ref, dec_ref, qs_ref, ks_ref, sd_ref,
                o_ref, st_ref):
    @pl.when(pl.program_id(0) == 0)
    def _():
        st_ref[...] = jnp.zeros_like(st_ref)

    cos = cos_ref[...]
    sin = sin_ref[...]
    for b in range(q_ref.shape[0]):
        outs = []
        for h in range(RET_HEADS):
            sl = slice(h * LANES, (h + 1) * LANES)
            qh = q_ref[b, :, sl]
            kh = k_ref[b, :, sl]
            qh = qh * cos + pltpu.roll(qh, RET_DK // 2, axis=1) * sin
            kh = (kh * cos + pltpu.roll(kh, RET_DK // 2, axis=1) * sin) * (RET_DK ** -0.5)
            vh = v_ref[b, :, sl].astype(BF16)
            sc = (_dot_nt(qh.astype(BF16), kh.astype(BF16)) * dec_ref[h]).astype(BF16)
            st = st_ref[b, h]
            y = _dot(sc, vh) + _dot((qh * qs_ref[:, sl]).astype(BF16), st.astype(BF16))
            kd_t = (kh * ks_ref[:, sl]).T.astype(BF16)
            st_ref[b, h] = st * sd_ref[h] + _dot(kd_t, vh)
            mu = jnp.mean(y, axis=-1, keepdims=True)
            d = y - mu
            outs.append(d * lax.rsqrt(jnp.mean(d * d, axis=-1, keepdims=True) + 1e-5))
        o_ref[b] = (_silu(g_ref[b]) * jnp.concatenate(outs, axis=1)).astype(o_ref.dtype)


def _retention(proj, B, S, cos_t, sin_t):
    L = CHUNK
    nc = S // L
    lg = jnp.log(1.0 - 2.0 ** (-5.0 - jnp.arange(RET_HEADS, dtype=F32)))
    i = jnp.arange(L, dtype=F32)
    diff = i[:, None] - i[None, :]
    dec = jnp.where(diff >= 0, jnp.exp(lg[:, None, None] * jnp.maximum(diff, 0.0)), 0.0)
    rep = lambda t: jnp.repeat(t, LANES, axis=1)
    qs = rep(jnp.exp((i[:, None] + 1.0) * lg[None, :]))
    ks = rep(jnp.exp((L - 1.0 - i[:, None]) * lg[None, :]))
    sd = jnp.broadcast_to(jnp.exp(L * lg)[:, None, None], (RET_HEADS, LANES, LANES))
    proj3 = proj.reshape(B, S, proj.shape[1])
    col = lambda name: pl.BlockSpec((B, L, GROUP_W), lambda c, n=COL[name] // 4: (0, c, n))
    return pl.pallas_call(
        _ret_kernel,
        grid=(nc,),
        in_specs=[col("rq"), col("rk"), col("rv"), col("rg"),
                  pl.BlockSpec((L, LANES), lambda c: (c, 0)),
                  pl.BlockSpec((L, LANES), lambda c: (c, 0)),
                  pl.BlockSpec((RET_HEADS, L, L), lambda c: (0, 0, 0)),
                  pl.BlockSpec((L, GROUP_W), lambda c: (0, 0)),
                  pl.BlockSpec((L, GROUP_W), lambda c: (0, 0)),
                  pl.BlockSpec((RET_HEADS, LANES, LANES), lambda c: (0, 0, 0))],
        out_specs=pl.BlockSpec((B, L, GROUP_W), lambda c: (0, c, 0)),
        out_shape=jax.ShapeDtypeStruct((B, S, GROUP_W), BF16),
        scratch_shapes=[pltpu.VMEM((B, RET_HEADS, RET_DK, RET_DK), F32)],
        compiler_params=_params(("arbitrary",)),
        name="retention",
    )(proj3, proj3, proj3, proj3, cos_t, sin_t, dec, qs, ks, sd).reshape(B * S, GROUP_W)


def _t5_bucket(dist):
    n = jnp.maximum(dist, 0)
    nf = jnp.maximum(n, 1).astype(F32)
    large = REL_EXACT + (jnp.log(nf / REL_EXACT) / math.log(REL_MAX_DIST / REL_EXACT)
                         * (REL_BUCKETS - REL_EXACT)).astype(jnp.int32)
    return jnp.where(n < REL_EXACT, n, jnp.minimum(large, REL_BUCKETS - 1))


def _head_bias(bucket, rel_ref):
    rows, cols = bucket.shape
    per_head = []
    for h in range(NSA_HEADS):
        tbl = jnp.broadcast_to(rel_ref[h:h + 1, :], (rows, LANES))
        chunks = [jnp.take_along_axis(tbl, bucket[:, c:c + LANES], axis=1)
                  for c in range(0, cols, LANES)]
        per_head.append(chunks[0] if len(chunks) == 1 else jnp.concatenate(chunks, axis=1))
    return jnp.stack(per_head, axis=0)


def _stack_heads(q):
    lane = lax.broadcasted_iota(jnp.int32, (q.shape[0], LANES), 1)
    rows = []
    for h in range(NSA_HEADS):
        slab = q[:, (h // 2) * LANES:(h // 2 + 1) * LANES]
        src_half, dst_half = h % 2, h // NSA_GROUP
        if src_half != dst_half:
            slab = pltpu.roll(slab, NSA_HEAD_DIM, axis=1)
        rows.append(jnp.where(lane // NSA_HEAD_DIM == dst_half, slab, 0.0))
    return jnp.concatenate(rows, axis=0).astype(BF16)


def _cmp_kernel(g_ref, pe_ref, w1a_ref, w1b_ref, w2_ref, o_ref):
    nb = g_ref.shape[0]
    gw = g_ref.shape[1] // 4
    pe = pe_ref[...]
    slabs = [g_ref[:, s * gw:(s + 1) * gw] for s in range(4)]
    nxt0 = pltpu.roll(slabs[0], nb - 1, axis=0)
    for s in range(4):
        a = (slabs[s] + pe[0:1, :]).astype(BF16)
        bn = ((slabs[s + 1] if s < 3 else nxt0) + pe[1:2, :]).astype(BF16)
        hid = _silu(_dot(a, w1a_ref[...]) + _dot(bn, w1b_ref[...]))
        o_ref[s * nb:(s + 1) * nb, :] = _dot(hid.astype(BF16), w2_ref[...]).astype(o_ref.dtype)


def _compress(g, pe2, w1a, w1b, w2bd):
    B, nb, gw4 = g.shape
    full2 = lambda a: pl.BlockSpec(a.shape, lambda b: (0, 0))
    slab_major = pl.pallas_call(
        _cmp_kernel,
        grid=(B,),
        in_specs=[pl.BlockSpec((None, nb, gw4), lambda b: (b, 0, 0)),
                  full2(pe2), full2(w1a), full2(w1b), full2(w2bd)],
        out_specs=pl.BlockSpec((None, 4 * nb, LANES), lambda b: (b, 0, 0)),
        out_shape=jax.ShapeDtypeStruct((B, 4 * nb, LANES), BF16),
        compiler_params=_params(("parallel",)),
        name="nsa_compress",
    )(g, pe2, w1a, w1b, w2bd)
    return slab_major.reshape(B, 4, nb, LANES).transpose(0, 2, 1, 3).reshape(B, 4 * nb, LANES)


def _cmpattn_kernel(q_ref, kc_ref, vc_ref, band_ref, st_ref, oc_ref, selb_ref):
    tq = q_ref.shape[0]
    ncmp = kc_ref.shape[0]
    nb = ncmp // 4
    R = NSA_HEADS * tq
    q0 = pl.program_id(1) * tq
    nd = band_ref.shape[0]

    def attend(ncv, nbv):
        Q = _stack_heads(q_ref[...])
        s3 = _dot_nt(Q, kc_ref[0:ncv, :]).reshape(NSA_HEADS, tq, ncv)
        parts = []
        for ch in range(ncv // LANES):
            d = jnp.clip((q0 - ch * LANES * NSA_CMP_STRIDE) // tq + CMP_BAND_OFF, 0, nd - 1)
            parts.append(s3[:, :, ch * LANES:(ch + 1) * LANES] + band_ref[d])
        s3 = jnp.concatenate(parts, axis=2)
        mx = jnp.max(s3, axis=-1, keepdims=True)
        e = jnp.exp(s3 - mx)
        live = (q0 + lax.broadcasted_iota(jnp.int32, (tq, 1), 0) >= NSA_CMP_BLOCK - 1)[None]
        p = e * jnp.where(live, 1.0 / jnp.sum(e, axis=-1, keepdims=True), 0.0)
        oc_ref[...] = _dot(p.reshape(R, ncv).astype(BF16), vc_ref[0:ncv, :])

        ps = p.reshape(NSA_KV, NSA_GROUP, tq, ncv).sum(axis=1).reshape(NSA_KV * tq, ncv)
        imp = _exact_right_dot(ps, st_ref[0:ncv, 0:nbv])
        j = lax.broadcasted_iota(jnp.int32, (NSA_KV * tq, nbv), 1)
        t = q0 + (lax.broadcasted_iota(jnp.int32, (NSA_KV * tq, nbv), 0) % tq)
        cur = t // NSA_SLC_BLOCK
        forced = (j == 0) | (j == cur) | (j == cur - 1)
        score = jnp.where(j > cur, -1.0, jnp.where(forced, NSA_GROUP + 1.0, imp))
        score = score.T
        jf = lax.broadcasted_iota(jnp.int32, score.shape, 0).astype(F32)
        sel = score == NSA_GROUP + 1.0
        score = jnp.where(sel, -jnp.inf, score)
        for _ in range(min(NSA_TOP_N, nb) - 3):
            best = jnp.max(score, axis=0, keepdims=True)
            first = jnp.min(jnp.where(score == best, jf, float(nbv)), axis=0, keepdims=True)
            hit = jf == first
            sel = sel | hit
            score = jnp.where(hit, -jnp.inf, score)
        selb = jnp.where(sel, 0.0, NEG).T.astype(selb_ref.dtype)
        for kv in range(NSA_KV):
            selb_ref[:, kv * nb:kv * nb + nbv] = selb[kv * tq:(kv + 1) * tq]
            if nbv < nb:
                selb_ref[:, kv * nb + nbv:(kv + 1) * nb] = jnp.full((tq, nb - nbv), NEG, selb_ref.dtype)

    if ncmp % CMP_STEP:
        attend(ncmp, nb)
    else:
        nvar = ncmp // CMP_STEP
        sizes = [(CMP_STEP * v, min(nb, LANES * ((v + 1) // 2))) for v in range(1, nvar + 1)]
        last_key = (q0 + tq - 1) // NSA_CMP_STRIDE
        lax.switch(jnp.minimum(last_key // CMP_STEP, nvar - 1),
                   [functools.partial(attend, ncv, nbv) for ncv, nbv in sizes])


def _cmpattn(proj, kcmp, vcmp, cband, B, S):
    tq = TQ
    nqt = S // tq
    ncmp = kcmp.shape[1]
    nb = ncmp // 4
    off = np.arange(ncmp)[:, None] - 4 * np.arange(nb)[None, :]
    stencil = np.where((off >= 0) & (off <= 2), 1.0, np.where((off == -1) | (off == 3), 0.5, 0.0))
    stencil = jnp.asarray(stencil, BF16)
    once = pl.Buffered(1)
    return pl.pallas_call(
        _cmpattn_kernel,
        grid=(B, nqt),
        in_specs=[pl.BlockSpec((tq, GROUP_W), lambda b, i: (b * nqt + i, COL["nq"] // 4)),
                  pl.BlockSpec((None, ncmp, LANES), lambda b, i: (b, 0, 0)),
                  pl.BlockSpec((None, ncmp, LANES), lambda b, i: (b, 0, 0)),
                  pl.BlockSpec(cband.shape, lambda b, i: (0, 0, 0, 0), pipeline_mode=once),
                  pl.BlockSpec(stencil.shape, lambda b, i: (0, 0))],
        out_specs=[pl.BlockSpec((NSA_HEADS * tq, LANES), lambda b, i: (b * nqt + i, 0)),
                   pl.BlockSpec((tq, 2 * nb), lambda b, i: (b * nqt + i, 0))],
        out_shape=[jax.ShapeDtypeStruct((B * S * NSA_HEADS, LANES), F32),
                   jax.ShapeDtypeStruct((B * S, 2 * nb), BF16)],
        compiler_params=_params(("parallel", "arbitrary")),
        name="nsa_cmp_attn_topk",
    )(proj, kcmp, vcmp, cband, stencil)


def _band_kernel(rel_ref, o_ref, *, entry_off, key_step, key_end):
    tq = o_ref.shape[1]
    delta = (pl.program_id(0) - entry_off) * tq
    row = lax.broadcasted_iota(jnp.int32, (tq, LANES), 0)
    col = lax.broadcasted_iota(jnp.int32, (tq, LANES), 1)
    dist = delta + row - (col * key_step + key_end)
    bias = _head_bias(_t5_bucket(dist), rel_ref)
    o_ref[...] = jnp.where((dist >= 0)[None], bias, NEG)


def _band_table(rel_t, tq, entry_off, key_step=1, key_end=0):
    nd = -(-(REL_MAX_DIST + key_step * (LANES - 1) + key_end) // tq) + entry_off + 1
    return pl.pallas_call(
        functools.partial(_band_kernel, entry_off=entry_off, key_step=key_step, key_end=key_end),
        grid=(nd,),
        in_specs=[pl.BlockSpec(rel_t.shape, lambda d: (0, 0))],
        out_specs=pl.BlockSpec((None, NSA_HEADS, tq, LANES), lambda d: (d, 0, 0, 0)),
        out_shape=jax.ShapeDtypeStruct((nd, NSA_HEADS, tq, LANES), F32),
        compiler_params=_params(("parallel",)),
        name="nsa_bias_band",
    )(rel_t)


def _selattn_kernel(q_ref, selb_ref, ks_ref, vs_ref, band_ref, sp_ref, farq_ref, os_ref,
                    qaug_ref, msel_ref, s_ref, mt_ref, m_ref, acc_ref):
    tq = q_ref.shape[0]
    R = NSA_HEADS * tq
    nb = selb_ref.shape[1] // 2
    n_kt = msel_ref.shape[0]
    tk = ks_ref.shape[0] // n_kt
    nd = band_ref.shape[0]
    q0 = pl.program_id(1) * tq
    qaug_ref[:, 0:LANES] = _stack_heads(q_ref[...])
    selb2 = jnp.concatenate([selb_ref[:, 0:nb], selb_ref[:, nb:2 * nb]], axis=0)
    spread = _dot(selb2, sp_ref[...])
    for c in range(n_kt):
        msel_ref[c] = spread[:, c * LANES:(c + 1) * LANES].astype(BF16)
    m_ref[...] = jnp.full(m_ref.shape, NEG, F32)
    acc_ref[...] = jnp.zeros_like(acc_ref)
    n_tiles = (q0 + tq - 1) // tk + 1

    far_pairs = (jnp.maximum(q0 - (REL_MAX_DIST - 1), 0) // tk) // 2

    def scores(c, slot, far):
        c = jnp.minimum(c, n_kt - 1)
        k0 = pl.multiple_of(c * tk, tk)
        m2 = msel_ref[c]
        flags = jnp.concatenate([m2[0:tq]] * NSA_GROUP + [m2[tq:2 * tq]] * NSA_GROUP, axis=0)
        qaug_ref[:, LANES:2 * LANES] = flags + farq_ref[...] if far else flags
        s = _dot_nt(qaug_ref[...], ks_ref[pl.ds(k0, tk), :])
        if not far:
            s3 = s.reshape(NSA_HEADS, tq, tk)
            parts = []
            for ch in range(tk // LANES):
                d = jnp.clip((q0 - k0) // tq - ch * (LANES // tq) + BAND_OFF, 0, nd - 1)
                parts.append(s3[:, :, ch * LANES:(ch + 1) * LANES] + band_ref[d])
            s = jnp.concatenate(parts, axis=2).reshape(R, tk)
        s_ref[slot] = s
        mt_ref[slot] = jnp.broadcast_to(jnp.max(s, axis=-1, keepdims=True), (R, LANES))

    def accumulate(c, slot):
        m_old = m_ref[...]
        m_new = jnp.maximum(m_old, mt_ref[slot])
        alpha = jnp.exp(m_old - m_new)
        p = jnp.exp(s_ref[slot] - jnp.tile(m_new, (1, tk // LANES)))
        vt = vs_ref[pl.ds(pl.multiple_of(c * tk, tk), tk), :]
        acc_ref[...] = jnp.tile(alpha, (1, 2)) * acc_ref[...] + _dot(p.astype(BF16), vt)
        m_ref[...] = m_new

    def pair(i, far_odd, far_even):
        c = 2 * i
        scores(c + 1, 1, far_odd)
        accumulate(c, 0)
        scores(c + 2, 0, far_even)
        accumulate(c + 1, 1)

    scores(0, 0, False)
    last_far = jnp.maximum(far_pairs - 1, 0)
    lax.fori_loop(0, last_far, lambda i, carry: pair(i, True, True), None)
    lax.fori_loop(last_far, far_pairs, lambda i, carry: pair(i, True, False), None)
    lax.fori_loop(far_pairs, n_tiles // 2, lambda i, carry: pair(i, False, False), None)

    @pl.when(n_tiles % 2 == 1)
    def _():
        accumulate(n_tiles - 1, 0)

    acc = acc_ref[...]
    os_ref[...] = acc[:, 0:LANES] / acc[:, LANES:2 * LANES]


def _selattn(proj, selb, ks, vs, band, farq, B, S):
    tq = TQ
    tk = min(TK, S)
    n_kt = S // tk
    bpt = tk // NSA_SLC_BLOCK
    nqt = S // tq
    nb = S // NSA_SLC_BLOCK
    R = NSA_HEADS * tq
    sp = np.zeros((nb, n_kt * LANES), np.float32)
    sp[np.arange(nb), (np.arange(nb) // bpt) * LANES + np.arange(nb) % bpt] = 1.0
    once = pl.Buffered(1)
    return pl.pallas_call(
        _selattn_kernel,
        grid=(B, nqt),
        in_specs=[pl.BlockSpec((tq, GROUP_W), lambda b, i: (b * nqt + i, COL["nq"] // 4)),
                  pl.BlockSpec((tq, 2 * nb), lambda b, i: (b * nqt + i, 0)),
                  pl.BlockSpec((None, S, 2 * LANES), lambda b, i: (b, 0, 0), pipeline_mode=once),
                  pl.BlockSpec((None, S, 2 * LANES), lambda b, i: (b, 0, 0), pipeline_mode=once),
                  pl.BlockSpec(band.shape, lambda b, i: (0, 0, 0, 0), pipeline_mode=once),
                  pl.BlockSpec(sp.shape, lambda b, i: (0, 0)),
                  pl.BlockSpec(farq.shape, lambda b, i: (0, 0))],
        out_specs=pl.BlockSpec((R, LANES), lambda b, i: (b * nqt + i, 0)),
        out_shape=jax.ShapeDtypeStruct((B * S * NSA_HEADS, LANES), F32),
        scratch_shapes=[pltpu.VMEM((R, 2 * LANES), BF16), pltpu.VMEM((n_kt, 2 * tq, LANES), BF16),
                        pltpu.VMEM((2, R, tk), F32), pltpu.VMEM((2, R, LANES), F32),
                        pltpu.VMEM((R, LANES), F32), pltpu.VMEM((R, 2 * LANES), F32)],
        compiler_params=_params(("parallel", "arbitrary")),
        name="nsa_sel_attn",
    )(proj, selb, ks, vs, band, jnp.asarray(sp, BF16), farq)


def _winattn_kernel(q_ref, sm_ref, oc_ref, os_ref, kw_ref, vw_ref, band_ref, ng_ref, o_ref):
    tq = q_ref.shape[1]
    R = NSA_HEADS * tq
    S = kw_ref.shape[1]
    span = min(WIN_SPAN, S)
    nd = band_ref.shape[0]
    q0 = pl.program_id(0) * tq
    start = pl.multiple_of(jnp.clip(q0 + tq - span, 0, S - span), tq)
    row = lax.broadcasted_iota(jnp.int32, (tq, span), 0)
    col = lax.broadcasted_iota(jnp.int32, (tq, span), 1)
    in_window = ((q0 + row) - (start + col) < NSA_WINDOW)[None]
    lane = lax.broadcasted_iota(jnp.int32, (tq, LANES), 1)
    for b in range(q_ref.shape[0]):
        Q = _stack_heads(q_ref[b])
        kt = kw_ref[b, pl.ds(start, span), :]
        vt = vw_ref[b, pl.ds(start, span), :]
        s3 = _dot_nt(Q, kt).reshape(NSA_HEADS, tq, span)
        parts = []
        for ch in range(span // LANES):
            d = jnp.clip((q0 - start) // tq - ch * (LANES // tq) + BAND_OFF, 0, nd - 1)
            parts.append(s3[:, :, ch * LANES:(ch + 1) * LANES] + band_ref[d])
        s3 = jnp.where(in_window, jnp.concatenate(parts, axis=2), NEG)
        mx = jnp.max(s3, axis=-1, keepdims=True)
        e = jnp.exp(s3 - mx)
        p = e * (1.0 / jnp.sum(e, axis=-1, keepdims=True))
        ow = _dot(p.reshape(R, span).astype(BF16), vt)

        gates = _sigmoid(sm_ref[b])
        heads = []
        ssq = jnp.zeros((tq, 1), F32)
        for h in range(NSA_HEADS):
            rs = slice(h * tq, (h + 1) * tq)
            g = [gates[:, GATE_LANE0 + 3 * h + br:GATE_LANE0 + 3 * h + br + 1] for br in range(3)]
            oh = g[0] * oc_ref[b, rs, :] + g[1] * os_ref[b, rs, :] + g[2] * ow[rs, :]
            kv = h // NSA_GROUP
            valid = (lane >= kv * NSA_HEAD_DIM) & (lane < (kv + 1) * NSA_HEAD_DIM)
            oh = jnp.where(valid, oh, 0.0)
            ssq = ssq + jnp.sum(oh * oh, axis=-1, keepdims=True)
            heads.append(oh)
        rinv = lax.rsqrt(ssq / GROUP_W + 1e-6)
        o_ref[b] = (jnp.concatenate(heads, axis=1) * rinv * ng_ref[...]).astype(o_ref.dtype)


def _winattn(proj, small, oc, os_, kw, vw, band, ngw, B, S):
    tq = TQ
    R = NSA_HEADS * tq
    once = pl.Buffered(1)
    per_b = lambda t: t.reshape(B, t.shape[0] // B, t.shape[1])
    return pl.pallas_call(
        _winattn_kernel,
        grid=(S // tq,),
        in_specs=[pl.BlockSpec((B, tq, GROUP_W), lambda i: (0, i, COL["nq"] // 4)),
                  pl.BlockSpec((B, tq, LANES), lambda i: (0, i, 0)),
                  pl.BlockSpec((B, R, LANES), lambda i: (0, i, 0)),
                  pl.BlockSpec((B, R, LANES), lambda i: (0, i, 0)),
                  pl.BlockSpec((B, S, LANES), lambda i: (0, 0, 0), pipeline_mode=once),
                  pl.BlockSpec((B, S, LANES), lambda i: (0, 0, 0), pipeline_mode=once),
                  pl.BlockSpec(band.shape, lambda i: (0, 0, 0, 0), pipeline_mode=once),
                  pl.BlockSpec(ngw.shape, lambda i: (0, 0))],
        out_specs=pl.BlockSpec((B, tq, NSA_HEADS * LANES), lambda i: (0, i, 0)),
        out_shape=jax.ShapeDtypeStruct((B, S, NSA_HEADS * LANES), BF16),
        compiler_params=_params(("arbitrary",)),
        name="nsa_win_attn_merge",
    )(per_b(proj), per_b(small), per_b(oc), per_b(os_), kw, vw, band, ngw).reshape(B * S, NSA_HEADS * LANES)


def _widen_heads(x, axis):
    x = jnp.moveaxis(x, axis, -1)
    lead = x.shape[:-1]
    x = x.reshape(*lead, NSA_KV, NSA_GROUP, 1, NSA_HEAD_DIM)
    sel = jnp.eye(NSA_KV, dtype=x.dtype).reshape(NSA_KV, 1, NSA_KV, 1)
    x = (x * sel).reshape(*lead, NSA_HEADS * LANES)
    return jnp.moveaxis(x, -1, axis)


def _build_w_in(w):
    (hq, hf, hi, hg, nq, nkc, nvc, nks, nvs, nkw, nvw, ngate,
     sz, sxbc, sdt, rq, rk, rv, rg) = jnp.split(w, IN_SPLITS, axis=1)
    D = w.shape[0]
    nq = nq * NSA_HEAD_DIM ** -0.5
    deint = lambda t: t.reshape(D, RET_HEADS, RET_DK // 2, 2).transpose(0, 1, 3, 2).reshape(D, GROUP_W)
    small = jnp.concatenate([ngate, sdt, jnp.zeros((D, LANES - 32), w.dtype)], axis=1)
    wide = [hq, hf, hi, hg, sxbc, nq, sz, deint(rq), deint(rk), rv, rg]
    narrow = [nkc, nvc, nks, nvs, nkw, nvw, small]
    return jnp.concatenate(wide, axis=1).astype(BF16), jnp.concatenate(narrow, axis=1).astype(BF16)


def _build_cmp_weights(pe, w1, w2):
    w1r = w1.reshape(2, NSA_CMP_STRIDE, NSA_HEAD_DIM, NSA_CMP_HIDDEN)
    eye = jnp.eye(NSA_KV, dtype=w1.dtype)
    big = jnp.einsum("ardc,kj->arkdjc", w1r, eye).reshape(
        2, NSA_CMP_STRIDE * NSA_KV * NSA_HEAD_DIM, NSA_KV * NSA_CMP_HIDDEN)
    w2bd = jnp.einsum("cd,kj->kcjd", w2, eye).reshape(NSA_KV * NSA_CMP_HIDDEN, NSA_KV * NSA_HEAD_DIM)
    per = pe.reshape(2, NSA_CMP_STRIDE, 1, NSA_HEAD_DIM)
    pe2 = jnp.broadcast_to(per, (2, NSA_CMP_STRIDE, NSA_KV, NSA_HEAD_DIM)).reshape(2, -1)
    return pe2, big[0].astype(BF16), big[1].astype(BF16), w2bd.astype(BF16)


def _rotary_tables(S):
    half = RET_DK // 2
    theta = 1.0 / (10000.0 ** jnp.linspace(0.0, 1.0, half, dtype=F32))
    ang = jnp.arange(S, dtype=F32)[:, None] * theta[None, :]
    cos, sin = jnp.cos(ang), jnp.sin(ang)
    return jnp.concatenate([cos, cos], axis=1), jnp.concatenate([-sin, sin], axis=1)


def _mixer(x2, B, S, l, p, lower_bounds, band, cband, farq, cos_t, sin_t):
    T = B * S
    w_wide, w_narrow = _build_w_in(p["w_in"][l])
    tn = NCOL * LANES // 4
    proj = _proj(x2, w_wide.reshape(-1, 4, tn).transpose(1, 0, 2), tm=min(1024, T))
    kc, vc, ks, vs, kw, vw, small = _kvproj(x2, w_narrow, tm=min(512, T))
    row = lambda v: v.reshape(1, -1).astype(F32)

    lb = lower_bounds[l].astype(F32)
    o_a = _hgrn(proj, B, S, row(jnp.log(lb)), row(jnp.log1p(-lb)), row(1.0 - lb),
                row(p["hgrn_norm_g"][l]))

    nb = S // NSA_SLC_BLOCK
    grp = lambda t: t.reshape(B, nb, 4 * NSA_CMP_STRIDE * LANES)
    kcmp = _compress(grp(kc), *_build_cmp_weights(p["nsa_pe_k"][l], p["nsa_w1_k"][l], p["nsa_w2_k"][l]))
    vcmp = _compress(grp(vc), *_build_cmp_weights(p["nsa_pe_v"][l], p["nsa_w1_v"][l], p["nsa_w2_v"][l]))
    o_cmp, selb = _cmpattn(proj, kcmp, vcmp, cband, B, S)
    seq = lambda t: t.reshape(B, S, t.shape[1])
    o_sel = _selattn(proj, selb, seq(ks), seq(vs), band, farq, B, S)
    ngw = _widen_heads(p["nsa_norm_g"][l].astype(F32), 0).reshape(1, -1)
    o_b = _winattn(proj, small, o_cmp, o_sel, seq(kw), seq(vw), band, ngw, B, S)

    lane_vec = lambda v: jnp.zeros((1, LANES), F32).at[0, DT_LANE0:DT_LANE0 + SSM_HEADS].set(v.astype(F32))
    o_c = _ssd(proj, small, B, S, p["ssm_conv_w"][l].astype(F32), row(p["ssm_conv_b"][l]),
               lane_vec(p["ssm_dt_bias"][l]), lane_vec(-jnp.exp(p["ssm_a_log"][l].astype(F32))),
               row(jnp.repeat(p["ssm_d"][l].astype(F32), SSM_HEAD_DIM)), row(p["ssm_norm_g"][l]))

    o_d = _retention(proj, B, S, cos_t, sin_t)

    w_out = p["w_out"][l]
    wa, wb, wc, wd = (w_out[i * GROUP_W:(i + 1) * GROUP_W] for i in range(4))
    return o_a, o_b, o_c, o_d, wa.astype(BF16), _widen_heads(wb, 0).astype(BF16), wc.astype(BF16), wd.astype(BF16)


def kernel(x, ln1_g, ln1_b, ffn1_w1, ffn1_w3, ffn1_w2, ln2_g, ln2_b, w_in, w_out, hgrn_lb_logits, hgrn_norm_g, nsa_pe_k, nsa_w1_k, nsa_w2_k, nsa_pe_v, nsa_w1_v, nsa_w2_v, nsa_norm_g, rel_bias, ssm_conv_w, ssm_conv_b, ssm_dt_bias, ssm_a_log, ssm_d, ssm_norm_g, ln3_g, ln3_b, ffn2_w1, ffn2_w3, ffn2_w2):
    B, S, D = x.shape
    T = B * S
    depth = w_in.shape[0]
    p = dict(w_in=w_in, w_out=w_out, hgrn_norm_g=hgrn_norm_g, nsa_pe_k=nsa_pe_k, nsa_w1_k=nsa_w1_k,
             nsa_w2_k=nsa_w2_k, nsa_pe_v=nsa_pe_v, nsa_w1_v=nsa_w1_v, nsa_w2_v=nsa_w2_v,
             nsa_norm_g=nsa_norm_g, ssm_conv_w=ssm_conv_w, ssm_conv_b=ssm_conv_b,
             ssm_dt_bias=ssm_dt_bias, ssm_a_log=ssm_a_log, ssm_d=ssm_d, ssm_norm_g=ssm_norm_g)
    cum = jnp.cumsum(jax.nn.softmax(hgrn_lb_logits.astype(F32), axis=0), axis=0)
    lower_bounds = cum - cum[:1]
    rel_t = jnp.zeros((NSA_HEADS, LANES), F32).at[:, :REL_BUCKETS].set(rel_bias.astype(F32).T)
    band = _band_table(rel_t, TQ, BAND_OFF)
    cband = _band_table(rel_t, TQ, CMP_BAND_OFF, NSA_CMP_STRIDE, NSA_CMP_BLOCK - 1)
    bpt = TK // NSA_SLC_BLOCK
    farq = jnp.zeros((NSA_HEADS, LANES), BF16).at[:, bpt:bpt + 3].set(
        jnp.stack(_split3(rel_bias.astype(F32)[REL_BUCKETS - 1]), axis=1))
    farq = jnp.repeat(farq, TQ, axis=0)
    cos_t, sin_t = _rotary_tables(S)
    row = lambda v: v.reshape(1, -1).astype(F32)
    tm = min(512, T)
    tf = 512 if ffn1_w1.shape[2] % 512 == 0 else ffn1_w1.shape[2]
    x2 = x.reshape(T, D).astype(F32)
    for l in range(depth):
        x2 = _ffn(x2, _to_bf16_col_blocks(ffn1_w1, l, tf), _to_bf16_col_blocks(ffn1_w3, l, tf),
                  _to_bf16(ffn1_w2, l), row(ln1_g[l]), row(ln1_b[l]), tm)
        o_a, o_b, o_c, o_d, wa, wb, wc, wd = _mixer(x2, B, S, l, p, lower_bounds, band, cband, farq, cos_t, sin_t)
        x2 = _outproj(x2, o_a, o_b, o_c, o_d, wa, wb, wc, wd, row(ln2_g[l]), row(ln2_b[l]), min(512, T))
        x2 = _ffn(x2, _to_bf16_col_blocks(ffn2_w1, l, tf), _to_bf16_col_blocks(ffn2_w3, l, tf),
                  _to_bf16(ffn2_w2, l), row(ln3_g[l]), row(ln3_b[l]), tm)
    return x2.reshape(B, S, D).astype(x.dtype)
```

```python
import functools
import math

import numpy as np
import jax
import jax.numpy as jnp
from jax import lax
from jax.experimental import pallas as pl
from jax.experimental.pallas import tpu as pltpu

F32 = jnp.float32
BF16 = jnp.bfloat16

DEPTH = 2
GROUP_W = 512
ALPHA = (2 * DEPTH) ** 0.25
HG_HEADS = 4
NSA_HEADS = 8
NSA_KV = 2
NSA_GROUP = 4
NSA_HEAD_DIM = 64
NSA_CMP_STRIDE = 16
NSA_CMP_BLOCK = 32
NSA_SLC_BLOCK = 64
NSA_TOP_N = 16
NSA_WINDOW = 512
NSA_CMP_HIDDEN = 256
SSM_HEADS = 8
SSM_HEAD_DIM = 64
SSM_GROUPS = 2
SSM_STATE = 128
SSM_CONV = 4
RET_HEADS = 4
RET_DK = 128
REL_BUCKETS = 32
REL_EXACT = 16
REL_MAX_DIST = 2048
IN_SIZES = ((GROUP_W,) * 4 + (GROUP_W,) + (128,) * 6 + (24,)
            + (GROUP_W, 1024, SSM_HEADS) + (GROUP_W,) * 4)
IN_SPLITS = tuple(int(v) for v in np.cumsum(IN_SIZES)[:-1])

LANES = 128
SUBLANES = 8
VMEM_LIMIT = 56 * 1024 * 1024
CAST_BLOCK_BYTES = 4 * 1024 * 1024

COL = dict(hq=0, hf=4, hi=8, hg=12, sxbc=16, nq=24, sz=28, rq=32, rk=36, rv=40, rg=44)
NCOL = 48
KV_COLS = ("nkc", "nvc", "nks", "nvs", "nkw", "nvw", "small")
GATE_LANE0 = 0
DT_LANE0 = 24

CHUNK = 128
TQ = 128
TK = 1024
BAND_OFF = 2
CMP_BAND_OFF = 1
CMP_STEP = 256
WIN_SPAN = NSA_WINDOW + 2 * TQ
NEG = -1e30


def _params(sem):
    return pltpu.CompilerParams(dimension_semantics=sem, vmem_limit_bytes=VMEM_LIMIT)


def _dot(a, b):
    return jnp.dot(a, b, preferred_element_type=F32)


def _dot_nt(a, b):
    return lax.dot_general(a, b, (((1,), (1,)), ((), ())), preferred_element_type=F32)


def _split3(x):
    hi = x.astype(BF16)
    r1 = x - hi.astype(F32)
    mid = r1.astype(BF16)
    return hi, mid, (r1 - mid.astype(F32)).astype(BF16)


def _exact_left_dot(w, x):
    n = x.shape[1]
    y = _dot(w, jnp.concatenate(_split3(x), axis=1))
    return y[:, 0:n] + y[:, n:2 * n] + y[:, 2 * n:3 * n]


def _exact_right_dot(x, w):
    n = x.shape[0]
    y = _dot(jnp.concatenate(_split3(x), axis=0), w)
    return y[0:n] + y[n:2 * n] + y[2 * n:3 * n]


def _sigmoid(x):
    return 1.0 / (1.0 + jnp.exp(-x))


def _silu(x):
    return x * _sigmoid(x)


def _softplus(x):
    return jnp.maximum(x, 0.0) + jnp.log1p(jnp.exp(-jnp.abs(x)))


def _layer_norm(r, g, b):
    mu = jnp.mean(r, axis=-1, keepdims=True)
    d = r - mu
    var = jnp.mean(d * d, axis=-1, keepdims=True)
    return d * lax.rsqrt(var + 1e-5) * g + b


def _cast_kernel(x_ref, o_ref):
    o_ref[...] = x_ref[...].astype(o_ref.dtype)


def _to_bf16(w, l):
    _, r, c = w.shape
    tr = min(r, max(16, CAST_BLOCK_BYTES // (4 * c) // 16 * 16))
    while r % tr:
        tr -= 16
    return pl.pallas_call(
        _cast_kernel,
        grid=(r // tr,),
        in_specs=[pl.BlockSpec((None, tr, c), lambda i: (l, i, 0))],
        out_specs=pl.BlockSpec((tr, c), lambda i: (i, 0)),
        out_shape=jax.ShapeDtypeStruct((r, c), BF16),
        compiler_params=_params(("parallel",)),
        name="cast_bf16",
    )(w)


def _to_bf16_col_blocks(w, l, tc):
    _, r, c = w.shape
    return pl.pallas_call(
        _cast_kernel,
        grid=(c // tc,),
        in_specs=[pl.BlockSpec((None, r, tc), lambda j: (l, 0, j))],
        out_specs=pl.BlockSpec((None, r, tc), lambda j: (j, 0, 0)),
        out_shape=jax.ShapeDtypeStruct((c // tc, r, tc), BF16),
        compiler_params=_params(("parallel",)),
        name="cast_bf16_blocked",
    )(w)


def _ffn_kernel(x_ref, w1_ref, w3_ref, w2_ref, g_ref, b_ref, o_ref, acc_ref, xb_ref):
    j = pl.program_id(1)

    @pl.when(j == 0)
    def _():
        xb_ref[...] = x_ref[...].astype(BF16)
        acc_ref[...] = jnp.zeros_like(acc_ref)

    xb = xb_ref[...]
    h1 = _dot(xb, w1_ref[...])
    h3 = _dot(xb, w3_ref[...])
    a = (_silu(h1) * h3).astype(BF16)
    acc_ref[...] += _dot(a, w2_ref[...])

    @pl.when(j == pl.num_programs(1) - 1)
    def _():
        r = ALPHA * x_ref[...] + 0.5 * acc_ref[...]
        o_ref[...] = _layer_norm(r, g_ref[...], b_ref[...])


def _ffn(x, w1, w3, w2, g, b, tm):
    T, D = x.shape
    nf, _, tf = w1.shape
    return pl.pallas_call(
        _ffn_kernel,
        grid=(T // tm, nf),
        in_specs=[
            pl.BlockSpec((tm, D), lambda i, j: (i, 0)),
            pl.BlockSpec((None, D, tf), lambda i, j: (j, 0, 0)),
            pl.BlockSpec((None, D, tf), lambda i, j: (j, 0, 0)),
            pl.BlockSpec((tf, D), lambda i, j: (j, 0)),
            pl.BlockSpec((1, D), lambda i, j: (0, 0)),
            pl.BlockSpec((1, D), lambda i, j: (0, 0)),
        ],
        out_specs=pl.BlockSpec((tm, D), lambda i, j: (i, 0)),
        out_shape=jax.ShapeDtypeStruct((T, D), F32),
        scratch_shapes=[pltpu.VMEM((tm, D), F32), pltpu.VMEM((tm, D), BF16)],
        compiler_params=_params(("parallel", "arbitrary")),
        name="ffn_ln",
    )(x, w1, w3, w2, g, b)


def _proj_kernel(x_ref, w_ref, o_ref, xb_ref):
    @pl.when(pl.program_id(1) == 0)
    def _():
        xb_ref[...] = x_ref[...].astype(BF16)

    o_ref[...] = _dot(xb_ref[...], w_ref[...])


def _proj(x, w, tm):
    T, D = x.shape
    nn, _, tn = w.shape
    return pl.pallas_call(
        _proj_kernel,
        grid=(T // tm, nn),
        in_specs=[pl.BlockSpec((tm, D), lambda i, j: (i, 0)),
                  pl.BlockSpec((None, D, tn), lambda i, j: (j, 0, 0))],
        out_specs=pl.BlockSpec((tm, tn), lambda i, j: (i, j)),
        out_shape=jax.ShapeDtypeStruct((T, nn * tn), F32),
        scratch_shapes=[pltpu.VMEM((tm, D), BF16)],
        compiler_params=_params(("parallel", "arbitrary")),
        name="in_proj",
    )(x, w)


def _kvproj_kernel(x_ref, w_ref, kc_ref, vc_ref, ks_ref, vs_ref, kw_ref, vw_ref, sm_ref):
    tm = x_ref.shape[0]
    y = _dot(x_ref[...].astype(BF16), w_ref[...])
    piece = lambda n: y[:, n * LANES:(n + 1) * LANES]
    kc_ref[...] = piece(0)
    vc_ref[...] = piece(1)
    row = pl.program_id(0) * tm + lax.broadcasted_iota(jnp.int32, (tm, LANES), 0)
    lane = lax.broadcasted_iota(jnp.int32, (tm, LANES), 1)
    bpt = TK // NSA_SLC_BLOCK
    onehot = (((row // NSA_SLC_BLOCK) % bpt == lane) | ((lane >= bpt) & (lane < bpt + 3))).astype(BF16)
    ks_ref[...] = jnp.concatenate([piece(2).astype(BF16), onehot], axis=1)
    vs_ref[...] = jnp.concatenate([piece(3).astype(BF16), jnp.ones((tm, LANES), BF16)], axis=1)
    kw_ref[...] = piece(4).astype(BF16)
    vw_ref[...] = piece(5).astype(BF16)
    sm_ref[...] = piece(6)


def _kvproj(x, w, tm):
    T, D = x.shape
    narrow = lambda dt, width=LANES: (pl.BlockSpec((tm, width), lambda i: (i, 0)),
                                      jax.ShapeDtypeStruct((T, width), dt))
    outs = [narrow(F32), narrow(F32), narrow(BF16, 2 * LANES), narrow(BF16, 2 * LANES),
            narrow(BF16), narrow(BF16), narrow(F32)]
    return pl.pallas_call(
        _kvproj_kernel,
        grid=(T // tm,),
        in_specs=[pl.BlockSpec((tm, D), lambda i: (i, 0)), pl.BlockSpec(w.shape, lambda i: (0, 0))],
        out_specs=[o[0] for o in outs],
        out_shape=[o[1] for o in outs],
        compiler_params=_params(("parallel",)),
        name="kv_proj",
    )(x, w)


def _outproj_kernel(x_ref, oa_ref, ob_ref, oc_ref, od_ref, wa_ref, wb_ref, wc_ref, wd_ref,
                    g_ref, b_ref, o_ref):
    mix = (_dot(oa_ref[...], wa_ref[...]) + _dot(ob_ref[...], wb_ref[...])
           + _dot(oc_ref[...], wc_ref[...]) + _dot(od_ref[...], wd_ref[...]))
    o_ref[...] = _layer_norm(ALPHA * x_ref[...] + mix, g_ref[...], b_ref[...])


def _outproj(x, oa, ob, oc, od, wa, wb, wc, wd, g, b, tm):
    T, D = x.shape
    row = lambda a: pl.BlockSpec((tm, a.shape[1]), lambda i: (i, 0))
    full = lambda a: pl.BlockSpec(a.shape, lambda i: (0, 0))
    return pl.pallas_call(
        _outproj_kernel,
        grid=(T // tm,),
        in_specs=[row(x), row(oa), row(ob), row(oc), row(od),
                  full(wa), full(wb), full(wc), full(wd), full(g), full(b)],
        out_specs=row(x),
        out_shape=jax.ShapeDtypeStruct((T, D), F32),
        compiler_params=_params(("parallel",)),
        name="out_proj_ln",
    )(x, oa, ob, oc, od, wa, wb, wc, wd, g, b)


def _hgrn_tables(C):
    i = np.arange(C)[:, None]
    ip = np.arange(C)[None, :]
    seg = [(ip <= i),
           (ip > i)]
    masks = [np.eye(C, dtype=bool)]
    s = C // 2
    while s >= 1:
        blk = i // s
        if s < SUBLANES:
            seg.append(np.where(blk % 2 == 1, (ip > blk * s) & (ip <= i), (ip > i) & (ip <= (blk + 1) * s)))
        masks.append((blk % 2 == 1) & (ip // s == blk - 1))
        s //= 2
    seg = np.concatenate([x.astype(np.float32) for x in seg], axis=0)
    return seg, np.stack([m.astype(np.float32) for m in masks])


def _hgrn_kernel(q_ref, f_ref, i_ref, g_ref, llb_ref, l1m_ref, oml_ref, ng_ref,
                 seg_ref, msk_ref, o_ref, st_ref):
    @pl.when(pl.program_id(0) == 0)
    def _():
        st_ref[...] = jnp.zeros_like(st_ref)

    C = q_ref.shape[1]
    nlev = msk_ref.shape[0] - 1
    for bi in range(q_ref.shape[0]):
        q = _silu(q_ref[bi])
        z = f_ref[bi]
        log_sig = jnp.minimum(z, 0.0) - jnp.log1p(jnp.exp(-jnp.abs(z)))
        cc = l1m_ref[...] + log_sig
        llb = llb_ref[...]
        logf = jnp.maximum(llb, cc) + jnp.log1p(jnp.exp(-jnp.abs(llb - cc)))
        k = oml_ref[...] * _sigmoid(-z)
        v = i_ref[bi]
        seg = _exact_left_dot(seg_ref[...], logf)
        b_all = seg[0:C]
        lev = []
        s = C // 2
        while s >= SUBLANES:
            b3 = b_all.reshape(C // s, s, GROUP_W)
            start = b3[:, 0:1, :]
            nxt = jnp.concatenate([start[1:], start[-1:]], axis=0)
            odd = lax.broadcasted_iota(jnp.int32, b3.shape, 0) % 2 == 1
            lev.append(jnp.where(odd, b3 - start, nxt - b3).reshape(C, GROUP_W))
            s //= 2
        lev += [seg[r * C:(r + 1) * C] for r in range(2, seg.shape[0] // C)]
        outs = []
        for h in range(HG_HEADS):
            sl = slice(h * LANES, (h + 1) * LANES)
            qh, kh, vh = q[:, sl], k[:, sl], v[:, sl]
            a = msk_ref[0] * _dot_nt(qh.astype(BF16), kh.astype(BF16))
            for l in range(nlev):
                dec = jnp.exp(lev[l][:, sl])
                a = a + msk_ref[1 + l] * _dot_nt((qh * dec).astype(BF16), (kh * dec).astype(BF16))
            b = seg[0:C, sl]
            st = st_ref[bi, h]
            o = _dot(a.astype(BF16), vh.astype(BF16))
            o = o + _dot_nt((qh * jnp.exp(b)).astype(BF16), st.astype(BF16))
            kd = (kh * jnp.exp(seg[C:2 * C, sl])).astype(BF16)
            st_ref[bi, h] = st * jnp.exp(b[C - 1:C, :]) + _dot(vh.T.astype(BF16), kd)
            outs.append(o * lax.rsqrt(jnp.mean(o * o, axis=-1, keepdims=True) + 1e-6))
        o = jnp.concatenate(outs, axis=1)
        o_ref[bi] = (o * ng_ref[...] * _silu(g_ref[bi])).astype(o_ref.dtype)


def _hgrn(proj, B, S, llb, l1m, oml, ng):
    C = CHUNK
    nc = S // C
    seg, msk = _hgrn_tables(C)
    seg, msk = jnp.asarray(seg, BF16), jnp.asarray(msk)
    proj3 = proj.reshape(B, S, proj.shape[1])
    col = lambda name: pl.BlockSpec((B, C, GROUP_W), lambda c, n=COL[name] // 4: (0, c, n))
    vec = pl.BlockSpec((1, GROUP_W), lambda c: (0, 0))
    full2 = lambda a: pl.BlockSpec(a.shape, lambda c: (0, 0))
    return pl.pallas_call(
        _hgrn_kernel,
        grid=(nc,),
        in_specs=[col("hq"), col("hf"), col("hi"), col("hg"), vec, vec, vec, vec,
                  full2(seg), pl.BlockSpec(msk.shape, lambda c: (0, 0, 0))],
        out_specs=pl.BlockSpec((B, C, GROUP_W), lambda c: (0, c, 0)),
        out_shape=jax.ShapeDtypeStruct((B, S, GROUP_W), BF16),
        scratch_shapes=[pltpu.VMEM((B, HG_HEADS, LANES, LANES), F32)],
        compiler_params=_params(("arbitrary",)),
        name="hgrn2",
    )(proj3, proj3, proj3, proj3, llb, l1m, oml, ng, seg, msk).reshape(B * S, GROUP_W)


def _ssd_kernel(z_ref, xbc_ref, sm_ref, cw_ref, cb_ref, dtb_ref, aneg_ref, dsk_ref, ng_ref,
                ex_ref, o_ref, tail_ref, st_ref):
    @pl.when(pl.program_id(0) == 0)
    def _():
        tail_ref[...] = jnp.zeros_like(tail_ref)
        st_ref[...] = jnp.zeros_like(st_ref)

    for b in range(xbc_ref.shape[0]):
        _ssd_chunk(z_ref.at[b], xbc_ref.at[b], sm_ref.at[b], cw_ref, cb_ref, dtb_ref, aneg_ref, dsk_ref, ng_ref,
                   ex_ref, o_ref.at[b], tail_ref.at[b], st_ref.at[b])


def _ssd_chunk(z_ref, xbc_ref, sm_ref, cw_ref, cb_ref, dtb_ref, aneg_ref, dsk_ref, ng_ref,
               ex_ref, o_ref, tail_ref, st_ref):
    L = xbc_ref.shape[0]
    x = xbc_ref[...]
    xe = jnp.concatenate([tail_ref[...], x], axis=0)
    cw = cw_ref[...]
    conv = cb_ref[...]
    for kk in range(SSM_CONV):
        conv = conv + cw[kk:kk + 1, :] * xe[5 + kk:5 + kk + L, :]
    tail_ref[...] = x[L - 8:L, :]
    conv = _silu(conv)
    xs = conv[:, 0:GROUP_W]
    bm = conv[:, GROUP_W:GROUP_W + 256]
    cm = conv[:, GROUP_W + 256:GROUP_W + 512]

    dtf = _softplus(sm_ref[...] + dtb_ref[...])
    la = dtf * aneg_ref[...]
    ri = lax.broadcasted_iota(jnp.int32, (L, L), 0)
    ci = lax.broadcasted_iota(jnp.int32, (L, L), 1)
    tri = ri >= ci
    bfull = _exact_left_dot(tri.astype(BF16), la)
    ex = ex_ref[...]
    bexp = _exact_right_dot(bfull, ex)
    dtexp = _exact_right_dot(dtf, ex)
    b_t = bfull.T
    xdt = xs * dtexp
    lane = lax.broadcasted_iota(jnp.int32, (L, LANES), 1)

    scores = []
    for g in range(SSM_GROUPS):
        cg = cm[:, g * SSM_STATE:(g + 1) * SSM_STATE].astype(BF16)
        bg = bm[:, g * SSM_STATE:(g + 1) * SSM_STATE].astype(BF16)
        cb = _dot_nt(cg, bg)
        for hh in range(SSM_HEADS // SSM_GROUPS):
            h = g * (SSM_HEADS // SSM_GROUPS) + hh
            bcol = bfull[:, DT_LANE0 + h:DT_LANE0 + h + 1]
            brow = b_t[DT_LANE0 + h:DT_LANE0 + h + 1, :]
            dec = jnp.exp(jnp.where(tri, bcol - brow, NEG))
            scores.append((cb * dec).astype(BF16))
    y_pairs = []
    for u in range(SSM_HEADS // 2):
        slab = xdt[:, u * LANES:(u + 1) * LANES]
        lo = jnp.where(lane < SSM_HEAD_DIM, slab, 0.0).astype(BF16)
        hi = jnp.where(lane >= SSM_HEAD_DIM, slab, 0.0).astype(BF16)
        y_pairs.append(_dot(scores[2 * u], lo) + _dot(scores[2 * u + 1], hi))
    y_intra = jnp.concatenate(y_pairs, axis=1)

    blast = bexp[L - 1:L, :]
    w = (xdt * jnp.exp(blast - bexp)).astype(BF16)
    y_inter = []
    for g in range(SSM_GROUPS):
        gs = slice(g * 256, (g + 1) * 256)
        cg = cm[:, g * SSM_STATE:(g + 1) * SSM_STATE].astype(BF16)
        st = st_ref[g]
        y_inter.append(_dot(cg, st.astype(BF16)))
        bg_t = bm[:, g * SSM_STATE:(g + 1) * SSM_STATE].T.astype(BF16)
        st_ref[g] = st * jnp.exp(blast[:, gs]) + _dot(bg_t, w[:, gs])
    y = y_intra + jnp.concatenate(y_inter, axis=1) * jnp.exp(bexp) + dsk_ref[...] * xs
    y = y * _silu(z_ref[...])
    halves = []
    for g in range(SSM_GROUPS):
        seg = y[:, g * 256:(g + 1) * 256]
        halves.append(seg * lax.rsqrt(jnp.mean(seg * seg, axis=-1, keepdims=True) + 1e-6))
    o_ref[...] = (jnp.concatenate(halves, axis=1) * ng_ref[...]).astype(o_ref.dtype)


def _ssd(proj, small, B, S, cw, cb, dtb, aneg, dsk, ng):
    L = CHUNK
    nc = S // L
    ex = np.zeros((LANES, GROUP_W), np.float32)
    for h in range(SSM_HEADS):
        ex[DT_LANE0 + h, h * SSM_HEAD_DIM:(h + 1) * SSM_HEAD_DIM] = 1.0
    ex = jnp.asarray(ex, BF16)
    full2 = lambda a: pl.BlockSpec(a.shape, lambda c: (0, 0))
    proj3 = proj.reshape(B, S, proj.shape[1])
    return pl.pallas_call(
        _ssd_kernel,
        grid=(nc,),
        in_specs=[
            pl.BlockSpec((B, L, GROUP_W), lambda c: (0, c, COL["sz"] // 4)),
            pl.BlockSpec((B, L, 1024), lambda c: (0, c, COL["sxbc"] // 8)),
            pl.BlockSpec((B, L, LANES), lambda c: (0, c, 0)),
            full2(cw), full2(cb), full2(dtb), full2(aneg), full2(dsk), full2(ng), full2(ex)],
        out_specs=pl.BlockSpec((B, L, GROUP_W), lambda c: (0, c, 0)),
        out_shape=jax.ShapeDtypeStruct((B, S, GROUP_W), BF16),
        scratch_shapes=[pltpu.VMEM((B, 8, 1024), F32), pltpu.VMEM((B, SSM_GROUPS, SSM_STATE, 256), F32)],
        compiler_params=_params(("arbitrary",)),
        name="ssd",
    )(proj3, proj3, small.reshape(B, S, LANES), cw, cb, dtb, aneg, dsk, ng, ex).reshape(B * S, GROUP_W)


def _ret_kernel(q_ref, k_ref, v_ref, g_ref, cos_ref, sin_ref, dec_ref, qs_ref, ks_ref, sd_ref,
                o_ref, st_ref):
    @pl.when(pl.program_id(0) == 0)
    def _():
        st_ref[...] = jnp.zeros_like(st_ref)

    cos = cos_ref[...]
    sin = sin_ref[...]
    for b in range(q_ref.shape[0]):
        outs = []
        for h in range(RET_HEADS):
            sl = slice(h * LANES, (h + 1) * LANES)
            qh = q_ref[b, :, sl]
            kh = k_ref[b, :, sl]
            qh = qh * cos + pltpu.roll(qh, RET_DK // 2, axis=1) * sin
            kh = (kh * cos + pltpu.roll(kh, RET_DK // 2, axis=1) * sin) * (RET_DK ** -0.5)
            vh = v_ref[b, :, sl].astype(BF16)
            sc = (_dot_nt(qh.astype(BF16), kh.astype(BF16)) * dec_ref[h]).astype(BF16)
            st = st_ref[b, h]
            y = _dot(sc, vh) + _dot((qh * qs_ref[:, sl]).astype(BF16), st.astype(BF16))
            kd_t = (kh * ks_ref[:, sl]).T.astype(BF16)
            st_ref[b, h] = st * sd_ref[h] + _dot(kd_t, vh)
            mu = jnp.mean(y, axis=-1, keepdims=True)
            d = y - mu
            outs.append(d * lax.rsqrt(jnp.mean(d * d, axis=-1, keepdims=True) + 1e-5))
        o_ref[b] = (_silu(g_ref[b]) * jnp.concatenate(outs, axis=1)).astype(o_ref.dtype)


def _retention(proj, B, S, cos_t, sin_t):
    L = CHUNK
    nc = S // L
    lg = jnp.log(1.0 - 2.0 ** (-5.0 - jnp.arange(RET_HEADS, dtype=F32)))
    i = jnp.arange(L, dtype=F32)
    diff = i[:, None] - i[None, :]
    dec = jnp.where(diff >= 0, jnp.exp(lg[:, None, None] * jnp.maximum(diff, 0.0)), 0.0)
    rep = lambda t: jnp.repeat(t, LANES, axis=1)
    qs = rep(jnp.exp((i[:, None] + 1.0) * lg[None, :]))
    ks = rep(jnp.exp((L - 1.0 - i[:, None]) * lg[None, :]))
    sd = jnp.broadcast_to(jnp.exp(L * lg)[:, None, None], (RET_HEADS, LANES, LANES))
    proj3 = proj.reshape(B, S, proj.shape[1])
    col = lambda name: pl.BlockSpec((B, L, GROUP_W), lambda c, n=COL[name] // 4: (0, c, n))
    return pl.pallas_call(
        _ret_kernel,
        grid=(nc,),
        in_specs=[col("rq"), col("rk"), col("rv"), col("rg"),
                  pl.BlockSpec((L, LANES), lambda c: (c, 0)),
                  pl.BlockSpec((L, LANES), lambda c: (c, 0)),
                  pl.BlockSpec((RET_HEADS, L, L), lambda c: (0, 0, 0)),
                  pl.BlockSpec((L, GROUP_W), lambda c: (0, 0)),
                  pl.BlockSpec((L, GROUP_W), lambda c: (0, 0)),
                  pl.BlockSpec((RET_HEADS, LANES, LANES), lambda c: (0, 0, 0))],
        out_specs=pl.BlockSpec((B, L, GROUP_W), lambda c: (0, c, 0)),
        out_shape=jax.ShapeDtypeStruct((B, S, GROUP_W), BF16),
        scratch_shapes=[pltpu.VMEM((B, RET_HEADS, RET_DK, RET_DK), F32)],
        compiler_params=_params(("arbitrary",)),
        name="retention",
    )(proj3, proj3, proj3, proj3, cos_t, sin_t, dec, qs, ks, sd).reshape(B * S, GROUP_W)


def _t5_bucket(dist):
    n = jnp.maximum(dist, 0)
    nf = jnp.maximum(n, 1).astype(F32)
    large = REL_EXACT + (jnp.log(nf / REL_EXACT) / math.log(REL_MAX_DIST / REL_EXACT)
                         * (REL_BUCKETS - REL_EXACT)).astype(jnp.int32)
    return jnp.where(n < REL_EXACT, n, jnp.minimum(large, REL_BUCKETS - 1))


def _head_bias(bucket, rel_ref):
    rows, cols = bucket.shape
    per_head = []
    for h in range(NSA_HEADS):
        tbl = jnp.broadcast_to(rel_ref[h:h + 1, :], (rows, LANES))
        chunks = [jnp.take_along_axis(tbl, bucket[:, c:c + LANES], axis=1)
                  for c in range(0, cols, LANES)]
        per_head.append(chunks[0] if len(chunks) == 1 else jnp.concatenate(chunks, axis=1))
    return jnp.stack(per_head, axis=0)


def _stack_heads(q):
    lane = lax.broadcasted_iota(jnp.int32, (q.shape[0], LANES), 1)
    rows = []
    for h in range(NSA_HEADS):
        slab = q[:, (h // 2) * LANES:(h // 2 + 1) * LANES]
        src_half, dst_half = h % 2, h // NSA_GROUP
        if src_half != dst_half:
            slab = pltpu.roll(slab, NSA_HEAD_DIM, axis=1)
        rows.append(jnp.where(lane // NSA_HEAD_DIM == dst_half, slab, 0.0))
    return jnp.concatenate(rows, axis=0).astype(BF16)


def _cmp_kernel(g_ref, pe_ref, w1a_ref, w1b_ref, w2_ref, o_ref):
    nb = g_ref.shape[0]
    gw = g_ref.shape[1] // 4
    pe = pe_ref[...]
    slabs = [g_ref[:, s * gw:(s + 1) * gw] for s in range(4)]
    nxt0 = pltpu.roll(slabs[0], nb - 1, axis=0)
    for s in range(4):
        a = (slabs[s] + pe[0:1, :]).astype(BF16)
        bn = ((slabs[s + 1] if s < 3 else nxt0) + pe[1:2, :]).astype(BF16)
        hid = _silu(_dot(a, w1a_ref[...]) + _dot(bn, w1b_ref[...]))
        o_ref[:, s * LANES:(s + 1) * LANES] = _dot(hid.astype(BF16), w2_ref[...]).astype(o_ref.dtype)


def _compress(g, pe2, w1a, w1b, w2bd):
    B, nb, gw4 = g.shape
    full2 = lambda a: pl.BlockSpec(a.shape, lambda b: (0, 0))
    return pl.pallas_call(
        _cmp_kernel,
        grid=(B,),
        in_specs=[pl.BlockSpec((None, nb, gw4), lambda b: (b, 0, 0)),
                  full2(pe2), full2(w1a), full2(w1b), full2(w2bd)],
        out_specs=pl.BlockSpec((None, nb, 4 * LANES), lambda b: (b, 0, 0)),
        out_shape=jax.ShapeDtypeStruct((B, nb, 4 * LANES), BF16),
        compiler_params=_params(("parallel",)),
        name="nsa_compress",
    )(g, pe2, w1a, w1b, w2bd).reshape(B, 4 * nb, LANES)


def _cmpattn_kernel(q_ref, kc_ref, vc_ref, band_ref, st_ref, oc_ref, selb_ref):
    tq = q_ref.shape[1]
    ncmp = kc_ref.shape[1]
    nb = ncmp // 4
    R = NSA_HEADS * tq
    q0 = pl.program_id(0) * tq
    nd = band_ref.shape[0]

    def attend(ncv, nbv):
        for b in range(q_ref.shape[0]):
            attend_one(b, ncv, nbv)

    def attend_one(b, ncv, nbv):
        Q = _stack_heads(q_ref[b])
        s3 = _dot_nt(Q, kc_ref[b, 0:ncv, :]).reshape(NSA_HEADS, tq, ncv)
        parts = []
        for ch in range(ncv // LANES):
            d = jnp.clip((q0 - ch * LANES * NSA_CMP_STRIDE) // tq + CMP_BAND_OFF, 0, nd - 1)
            parts.append(s3[:, :, ch * LANES:(ch + 1) * LANES] + band_ref[d])
        s3 = jnp.concatenate(parts, axis=2)
        mx = jnp.max(s3, axis=-1, keepdims=True)
        e = jnp.exp(s3 - mx)
        live = (q0 + lax.broadcasted_iota(jnp.int32, (tq, 1), 0) >= NSA_CMP_BLOCK - 1)[None]
        p = e * jnp.where(live, 1.0 / jnp.sum(e, axis=-1, keepdims=True), 0.0)
        oc_ref[b] = _dot(p.reshape(R, ncv).astype(BF16), vc_ref[b, 0:ncv, :])

        ps = p.reshape(NSA_KV, NSA_GROUP, tq, ncv).sum(axis=1).reshape(NSA_KV * tq, ncv)
        imp = _exact_right_dot(ps, st_ref[0:ncv, 0:nbv])
        j = lax.broadcasted_iota(jnp.int32, (NSA_KV * tq, nbv), 1)
        t = q0 + (lax.broadcasted_iota(jnp.int32, (NSA_KV * tq, nbv), 0) % tq)
        cur = t // NSA_SLC_BLOCK
        forced = (j == 0) | (j == cur) | (j == cur - 1)
        score = jnp.where(j > cur, -1.0, jnp.where(forced, NSA_GROUP + 1.0, imp))
        score = score.T
        jf = lax.broadcasted_iota(jnp.int32, score.shape, 0).astype(F32)
        sel = score == NSA_GROUP + 1.0
        score = jnp.where(sel, -jnp.inf, score)
        for _ in range(min(NSA_TOP_N, nb) - 3):
            best = jnp.max(score, axis=0, keepdims=True)
            first = jnp.min(jnp.where(score == best, jf, float(nbv)), axis=0, keepdims=True)
            hit = jf == first
            sel = sel | hit
            score = jnp.where(hit, -jnp.inf, score)
        selb = jnp.where(sel, 0.0, NEG).T.astype(selb_ref.dtype)
        for kv in range(NSA_KV):
            selb_ref[b, :, kv * nb:kv * nb + nbv] = selb[kv * tq:(kv + 1) * tq]
            if nbv < nb:
                selb_ref[b, :, kv * nb + nbv:(kv + 1) * nb] = jnp.full((tq, nb - nbv), NEG, selb_ref.dtype)

    if ncmp % CMP_STEP:
        attend(ncmp, nb)
    else:
        nvar = ncmp // CMP_STEP
        sizes = [(CMP_STEP * v, min(nb, LANES * ((v + 1) // 2))) for v in range(1, nvar + 1)]
        last_key = (q0 + tq - 1) // NSA_CMP_STRIDE
        lax.switch(jnp.minimum(last_key // CMP_STEP, nvar - 1),
                   [functools.partial(attend, ncv, nbv) for ncv, nbv in sizes])


def _cmpattn(proj, kcmp, vcmp, cband, B, S):
    tq = TQ
    nqt = S // tq
    ncmp = kcmp.shape[1]
    nb = ncmp // 4
    off = np.arange(ncmp)[:, None] - 4 * np.arange(nb)[None, :]
    stencil = np.where((off >= 0) & (off <= 2), 1.0, np.where((off == -1) | (off == 3), 0.5, 0.0))
    stencil = jnp.asarray(stencil, BF16)
    once = pl.Buffered(1)
    R = NSA_HEADS * tq
    o_cmp, selb = pl.pallas_call(
        _cmpattn_kernel,
        grid=(nqt,),
        in_specs=[pl.BlockSpec((B, tq, GROUP_W), lambda i: (0, i, COL["nq"] // 4)),
                  pl.BlockSpec((B, ncmp, LANES), lambda i: (0, 0, 0), pipeline_mode=once),
                  pl.BlockSpec((B, ncmp, LANES), lambda i: (0, 0, 0), pipeline_mode=once),
                  pl.BlockSpec(cband.shape, lambda i: (0, 0, 0, 0), pipeline_mode=once),
                  pl.BlockSpec(stencil.shape, lambda i: (0, 0), pipeline_mode=once)],
        out_specs=[pl.BlockSpec((B, R, LANES), lambda i: (0, i, 0)),
                   pl.BlockSpec((B, tq, 2 * nb), lambda i: (0, i, 0))],
        out_shape=[jax.ShapeDtypeStruct((B, S * NSA_HEADS, LANES), F32),
                   jax.ShapeDtypeStruct((B, S, 2 * nb), BF16)],
        compiler_params=_params(("arbitrary",)),
        name="nsa_cmp_attn_topk",
    )(proj.reshape(B, S, proj.shape[1]), kcmp, vcmp, cband, stencil)
    return o_cmp.reshape(B * S * NSA_HEADS, LANES), selb.reshape(B * S, 2 * nb)


def _band_kernel(rel_ref, o_ref, *, entry_off, key_step, key_end):
    tq = o_ref.shape[1]
    delta = (pl.program_id(0) - entry_off) * tq
    row = lax.broadcasted_iota(jnp.int32, (tq, LANES), 0)
    col = lax.broadcasted_iota(jnp.int32, (tq, LANES), 1)
    dist = delta + row - (col * key_step + key_end)
    bias = _head_bias(_t5_bucket(dist), rel_ref)
    o_ref[...] = jnp.where((dist >= 0)[None], bias, NEG)


def _band_table(rel_t, tq, entry_off, key_step=1, key_end=0):
    nd = -(-(REL_MAX_DIST + key_step * (LANES - 1) + key_end) // tq) + entry_off + 1
    return pl.pallas_call(
        functools.partial(_band_kernel, entry_off=entry_off, key_step=key_step, key_end=key_end),
        grid=(nd,),
        in_specs=[pl.BlockSpec(rel_t.shape, lambda d: (0, 0))],
        out_specs=pl.BlockSpec((None, NSA_HEADS, tq, LANES), lambda d: (d, 0, 0, 0)),
        out_shape=jax.ShapeDtypeStruct((nd, NSA_HEADS, tq, LANES), F32),
        compiler_params=_params(("parallel",)),
        name="nsa_bias_band",
    )(rel_t)


def _selattn_kernel(q_ref, selb_ref, ks_ref, vs_ref, band_ref, sp_ref, farq_ref, os_ref,
                    qaug_ref, msel_ref, s_ref, mt_ref, m_ref, acc_ref):
    tq = q_ref.shape[0]
    R = NSA_HEADS * tq
    nb = selb_ref.shape[1] // 2
    n_kt = msel_ref.shape[0]
    tk = ks_ref.shape[0] // n_kt
    nd = band_ref.shape[0]
    q0 = pl.program_id(1) * tq
    qaug_ref[:, 0:LANES] = _stack_heads(q_ref[...])
    selb2 = jnp.concatenate([selb_ref[:, 0:nb], selb_ref[:, nb:2 * nb]], axis=0)
    spread = _dot(selb2, sp_ref[...])
    for c in range(n_kt):
        msel_ref[c] = spread[:, c * LANES:(c + 1) * LANES].astype(BF16)
    m_ref[...] = jnp.full(m_ref.shape, NEG, F32)
    acc_ref[...] = jnp.zeros_like(acc_ref)
    n_tiles = (q0 + tq - 1) // tk + 1

    far_pairs = (jnp.maximum(q0 - (REL_MAX_DIST - 1), 0) // tk) // 2

    def scores(c, slot, far):
        c = jnp.minimum(c, n_kt - 1)
        k0 = pl.multiple_of(c * tk, tk)
        m2 = msel_ref[c]
        flags = jnp.concatenate([m2[0:tq]] * NSA_GROUP + [m2[tq:2 * tq]] * NSA_GROUP, axis=0)
        qaug_ref[:, LANES:2 * LANES] = flags + farq_ref[...] if far else flags
        s = _dot_nt(qaug_ref[...], ks_ref[pl.ds(k0, tk), :])
        if not far:
            s3 = s.reshape(NSA_HEADS, tq, tk)
            parts = []
            for ch in range(tk // LANES):
                d = jnp.clip((q0 - k0) // tq - ch * (LANES // tq) + BAND_OFF, 0, nd - 1)
                parts.append(s3[:, :, ch * LANES:(ch + 1) * LANES] + band_ref[d])
            s = jnp.concatenate(parts, axis=2).reshape(R, tk)
        s_ref[slot] = s
        mt_ref[slot] = jnp.broadcast_to(jnp.max(s, axis=-1, keepdims=True), (R, LANES))

    def accumulate(c, slot):
        m_old = m_ref[...]
        m_new = jnp.maximum(m_old, mt_ref[slot])
        alpha = jnp.exp(m_old - m_new)
        p = jnp.exp(s_ref[slot] - jnp.tile(m_new, (1, tk // LANES)))
        vt = vs_ref[pl.ds(pl.multiple_of(c * tk, tk), tk), :]
        acc_ref[...] = jnp.tile(alpha, (1, 2)) * acc_ref[...] + _dot(p.astype(BF16), vt)
        m_ref[...] = m_new

    def pair(i, far_odd, far_even):
        c = 2 * i
        scores(c + 1, 1, far_odd)
        accumulate(c, 0)
        scores(c + 2, 0, far_even)
        accumulate(c + 1, 1)

    scores(0, 0, False)
    last_far = jnp.maximum(far_pairs - 1, 0)
    lax.fori_loop(0, last_far, lambda i, carry: pair(i, True, True), None)
    lax.fori_loop(last_far, far_pairs, lambda i, carry: pair(i, True, False), None)
    lax.fori_loop(far_pairs, n_tiles // 2, lambda i, carry: pair(i, False, False), None)

    @pl.when(n_tiles % 2 == 1)
    def _():
        accumulate(n_tiles - 1, 0)

    acc = acc_ref[...]
    os_ref[...] = acc[:, 0:LANES] / acc[:, LANES:2 * LANES]


def _selattn(proj, selb, ks, vs, band, farq, B, S):
    tq = TQ
    tk = min(TK, S)
    n_kt = S // tk
    bpt = tk // NSA_SLC_BLOCK
    nqt = S // tq
    nb = S // NSA_SLC_BLOCK
    R = NSA_HEADS * tq
    sp = np.zeros((nb, n_kt * LANES), np.float32)
    sp[np.arange(nb), (np.arange(nb) // bpt) * LANES + np.arange(nb) % bpt] = 1.0
    once = pl.Buffered(1)
    return pl.pallas_call(
        _selattn_kernel,
        grid=(B, nqt),
        in_specs=[pl.BlockSpec((tq, GROUP_W), lambda b, i: (b * nqt + i, COL["nq"] // 4)),
                  pl.BlockSpec((tq, 2 * nb), lambda b, i: (b * nqt + i, 0)),
                  pl.BlockSpec((None, S, 2 * LANES), lambda b, i: (b, 0, 0), pipeline_mode=once),
                  pl.BlockSpec((None, S, 2 * LANES), lambda b, i: (b, 0, 0), pipeline_mode=once),
                  pl.BlockSpec(band.shape, lambda b, i: (0, 0, 0, 0), pipeline_mode=once),
                  pl.BlockSpec(sp.shape, lambda b, i: (0, 0)),
                  pl.BlockSpec(farq.shape, lambda b, i: (0, 0))],
        out_specs=pl.BlockSpec((R, LANES), lambda b, i: (b * nqt + i, 0)),
        out_shape=jax.ShapeDtypeStruct((B * S * NSA_HEADS, LANES), F32),
        scratch_shapes=[pltpu.VMEM((R, 2 * LANES), BF16), pltpu.VMEM((n_kt, 2 * tq, LANES), BF16),
                        pltpu.VMEM((2, R, tk), F32), pltpu.VMEM((2, R, LANES), F32),
                        pltpu.VMEM((R, LANES), F32), pltpu.VMEM((R, 2 * LANES), F32)],
        compiler_params=_params(("parallel", "arbitrary")),
        name="nsa_sel_attn",
    )(proj, selb, ks, vs, band, jnp.asarray(sp, BF16), farq)


def _winattn_kernel(q_ref, sm_ref, oc_ref, os_ref, kw_ref, vw_ref, band_ref, ng_ref, o_ref):
    tq = q_ref.shape[1]
    R = NSA_HEADS * tq
    S = kw_ref.shape[1]
    span = min(WIN_SPAN, S)
    nd = band_ref.shape[0]
    q0 = pl.program_id(0) * tq
    start = pl.multiple_of(jnp.clip(q0 + tq - span, 0, S - span), tq)
    row = lax.broadcasted_iota(jnp.int32, (tq, span), 0)
    col = lax.broadcasted_iota(jnp.int32, (tq, span), 1)
    in_window = ((q0 + row) - (start + col) < NSA_WINDOW)[None]
    lane = lax.broadcasted_iota(jnp.int32, (tq, LANES), 1)
    for b in range(q_ref.shape[0]):
        Q = _stack_heads(q_ref[b])
        kt = kw_ref[b, pl.ds(start, span), :]
        vt = vw_ref[b, pl.ds(start, span), :]
        s3 = _dot_nt(Q, kt).reshape(NSA_HEADS, tq, span)
        parts = []
        for ch in range(span // LANES):
            d = jnp.clip((q0 - start) // tq - ch * (LANES // tq) + BAND_OFF, 0, nd - 1)
            part = s3[:, :, ch * LANES:(ch + 1) * LANES] + band_ref[d]
            if (span - tq) + tq - 1 - ch * LANES >= NSA_WINDOW:
                part = jnp.where(in_window[:, :, ch * LANES:(ch + 1) * LANES], part, NEG)
            parts.append(part)
        s3 = jnp.concatenate(parts, axis=2)
        mx = jnp.max(s3, axis=-1, keepdims=True)
        e = jnp.exp(s3 - mx)
        inv = (1.0 / jnp.sum(e, axis=-1, keepdims=True)).reshape(R, 1)
        ow = _dot(e.reshape(R, span).astype(BF16), vt) * inv

        gates = _sigmoid(sm_ref[b])
        heads = []
        ssq = jnp.zeros((tq, 1), F32)
        for h in range(NSA_HEADS):
            rs = slice(h * tq, (h + 1) * tq)
            g = [gates[:, GATE_LANE0 + 3 * h + br:GATE_LANE0 + 3 * h + br + 1] for br in range(3)]
            oh = g[0] * oc_ref[b, rs, :] + g[1] * os_ref[b, rs, :] + g[2] * ow[rs, :]
            kv = h // NSA_GROUP
            valid = (lane >= kv * NSA_HEAD_DIM) & (lane < (kv + 1) * NSA_HEAD_DIM)
            oh = jnp.where(valid, oh, 0.0)
            ssq = ssq + jnp.sum(oh * oh, axis=-1, keepdims=True)
            heads.append(oh)
        rinv = lax.rsqrt(ssq / GROUP_W + 1e-6)
        o_ref[b] = (jnp.concatenate(heads, axis=1) * rinv * ng_ref[...]).astype(o_ref.dtype)


def _winattn(proj, small, oc, os_, kw, vw, band, ngw, B, S):
    tq = TQ
    R = NSA_HEADS * tq
    once = pl.Buffered(1)
    per_b = lambda t: t.reshape(B, t.shape[0] // B, t.shape[1])
    return pl.pallas_call(
        _winattn_kernel,
        grid=(S // tq,),
        in_specs=[pl.BlockSpec((B, tq, GROUP_W), lambda i: (0, i, COL["nq"] // 4)),
                  pl.BlockSpec((B, tq, LANES), lambda i: (0, i, 0)),
                  pl.BlockSpec((B, R, LANES), lambda i: (0, i, 0)),
                  pl.BlockSpec((B, R, LANES), lambda i: (0, i, 0)),
                  pl.BlockSpec((B, S, LANES), lambda i: (0, 0, 0), pipeline_mode=once),
                  pl.BlockSpec((B, S, LANES), lambda i: (0, 0, 0), pipeline_mode=once),
                  pl.BlockSpec(band.shape, lambda i: (0, 0, 0, 0), pipeline_mode=once),
                  pl.BlockSpec(ngw.shape, lambda i: (0, 0))],
        out_specs=pl.BlockSpec((B, tq, NSA_HEADS * LANES), lambda i: (0, i, 0)),
        out_shape=jax.ShapeDtypeStruct((B, S, NSA_HEADS * LANES), BF16),
        compiler_params=_params(("arbitrary",)),
        name="nsa_win_attn_merge",
    )(per_b(proj), per_b(small), per_b(oc), per_b(os_), kw, vw, band, ngw).reshape(B * S, NSA_HEADS * LANES)


def _widen_heads(x, axis):
    x = jnp.moveaxis(x, axis, -1)
    lead = x.shape[:-1]
    x = x.reshape(*lead, NSA_KV, NSA_GROUP, 1, NSA_HEAD_DIM)
    sel = jnp.eye(NSA_KV, dtype=x.dtype).reshape(NSA_KV, 1, NSA_KV, 1)
    x = (x * sel).reshape(*lead, NSA_HEADS * LANES)
    return jnp.moveaxis(x, -1, axis)


def _build_w_in(w):
    (hq, hf, hi, hg, nq, nkc, nvc, nks, nvs, nkw, nvw, ngate,
     sz, sxbc, sdt, rq, rk, rv, rg) = jnp.split(w, IN_SPLITS, axis=1)
    D = w.shape[0]
    nq = nq * NSA_HEAD_DIM ** -0.5
    deint = lambda t: t.reshape(D, RET_HEADS, RET_DK // 2, 2).transpose(0, 1, 3, 2).reshape(D, GROUP_W)
    small = jnp.concatenate([ngate, sdt, jnp.zeros((D, LANES - 32), w.dtype)], axis=1)
    wide = [hq, hf, hi, hg, sxbc, nq, sz, deint(rq), deint(rk), rv, rg]
    narrow = [nkc, nvc, nks, nvs, nkw, nvw, small]
    return jnp.concatenate(wide, axis=1).astype(BF16), jnp.concatenate(narrow, axis=1).astype(BF16)


def _build_cmp_weights(pe, w1, w2):
    w1r = w1.reshape(2, NSA_CMP_STRIDE, NSA_HEAD_DIM, NSA_CMP_HIDDEN)
    eye = jnp.eye(NSA_KV, dtype=w1.dtype)
    big = jnp.einsum("ardc,kj->arkdjc", w1r, eye).reshape(
        2, NSA_CMP_STRIDE * NSA_KV * NSA_HEAD_DIM, NSA_KV * NSA_CMP_HIDDEN)
    w2bd = jnp.einsum("cd,kj->kcjd", w2, eye).reshape(NSA_KV * NSA_CMP_HIDDEN, NSA_KV * NSA_HEAD_DIM)
    per = pe.reshape(2, NSA_CMP_STRIDE, 1, NSA_HEAD_DIM)
    pe2 = jnp.broadcast_to(per, (2, NSA_CMP_STRIDE, NSA_KV, NSA_HEAD_DIM)).reshape(2, -1)
    return pe2, big[0].astype(BF16), big[1].astype(BF16), w2bd.astype(BF16)


def _rotary_tables(S):
    half = RET_DK // 2
    theta = 1.0 / (10000.0 ** jnp.linspace(0.0, 1.0, half, dtype=F32))
    ang = jnp.arange(S, dtype=F32)[:, None] * theta[None, :]
    cos, sin = jnp.cos(ang), jnp.sin(ang)
    return jnp.concatenate([cos, cos], axis=1), jnp.concatenate([-sin, sin], axis=1)


def _mixer(x2, B, S, l, p, lower_bounds, band, cband, farq, cos_t, sin_t):
    T = B * S
    w_wide, w_narrow = _build_w_in(p["w_in"][l])
    tn = NCOL * LANES // 4
    proj = _proj(x2, w_wide.reshape(-1, 4, tn).transpose(1, 0, 2), tm=min(1024, T))
    kc, vc, ks, vs, kw, vw, small = _kvproj(x2, w_narrow, tm=min(512, T))
    row = lambda v: v.reshape(1, -1).astype(F32)

    lb = lower_bounds[l].astype(F32)
    o_a = _hgrn(proj, B, S, row(jnp.log(lb)), row(jnp.log1p(-lb)), row(1.0 - lb),
                row(p["hgrn_norm_g"][l]))

    nb = S // NSA_SLC_BLOCK
    grp = lambda t: t.reshape(B, nb, 4 * NSA_CMP_STRIDE * LANES)
    kcmp = _compress(grp(kc), *_build_cmp_weights(p["nsa_pe_k"][l], p["nsa_w1_k"][l], p["nsa_w2_k"][l]))
    vcmp = _compress(grp(vc), *_build_cmp_weights(p["nsa_pe_v"][l], p["nsa_w1_v"][l], p["nsa_w2_v"][l]))
    o_cmp, selb = _cmpattn(proj, kcmp, vcmp, cband, B, S)
    seq = lambda t: t.reshape(B, S, t.shape[1])
    o_sel = _selattn(proj, selb, seq(ks), seq(vs), band, farq, B, S)
    ngw = _widen_heads(p["nsa_norm_g"][l].astype(F32), 0).reshape(1, -1)
    o_b = _winattn(proj, small, o_cmp, o_sel, seq(kw), seq(vw), band, ngw, B, S)

    lane_vec = lambda v: jnp.zeros((1, LANES), F32).at[0, DT_LANE0:DT_LANE0 + SSM_HEADS].set(v.astype(F32))
    o_c = _ssd(proj, small, B, S, p["ssm_conv_w"][l].astype(F32), row(p["ssm_conv_b"][l]),
               lane_vec(p["ssm_dt_bias"][l]), lane_vec(-jnp.exp(p["ssm_a_log"][l].astype(F32))),
               row(jnp.repeat(p["ssm_d"][l].astype(F32), SSM_HEAD_DIM)), row(p["ssm_norm_g"][l]))

    o_d = _retention(proj, B, S, cos_t, sin_t)

    w_out = p["w_out"][l]
    wa, wb, wc, wd = (w_out[i * GROUP_W:(i + 1) * GROUP_W] for i in range(4))
    return o_a, o_b, o_c, o_d, wa.astype(BF16), _widen_heads(wb, 0).astype(BF16), wc.astype(BF16), wd.astype(BF16)


def kernel(x, ln1_g, ln1_b, ffn1_w1, ffn1_w3, ffn1_w2, ln2_g, ln2_b, w_in, w_out, hgrn_lb_logits, hgrn_norm_g, nsa_pe_k, nsa_w1_k, nsa_w2_k, nsa_pe_v, nsa_w1_v, nsa_w2_v, nsa_norm_g, rel_bias, ssm_conv_w, ssm_conv_b, ssm_dt_bias, ssm_a_log, ssm_d, ssm_norm_g, ln3_g, ln3_b, ffn2_w1, ffn2_w3, ffn2_w2):
    B, S, D = x.shape
    T = B * S
    depth = w_in.shape[0]
    p = dict(w_in=w_in, w_out=w_out, hgrn_norm_g=hgrn_norm_g, nsa_pe_k=nsa_pe_k, nsa_w1_k=nsa_w1_k,
             nsa_w2_k=nsa_w2_k, nsa_pe_v=nsa_pe_v, nsa_w1_v=nsa_w1_v, nsa_w2_v=nsa_w2_v,
             nsa_norm_g=nsa_norm_g, ssm_conv_w=ssm_conv_w, ssm_conv_b=ssm_conv_b,
             ssm_dt_bias=ssm_dt_bias, ssm_a_log=ssm_a_log, ssm_d=ssm_d, ssm_norm_g=ssm_norm_g)
    cum = jnp.cumsum(jax.nn.softmax(hgrn_lb_logits.astype(F32), axis=0), axis=0)
    lower_bounds = cum - cum[:1]
    rel_t = jnp.zeros((NSA_HEADS, LANES), F32).at[:, :REL_BUCKETS].set(rel_bias.astype(F32).T)
    band = _band_table(rel_t, TQ, BAND_OFF)
    cband = _band_table(rel_t, TQ, CMP_BAND_OFF, NSA_CMP_STRIDE, NSA_CMP_BLOCK - 1)
    bpt = TK // NSA_SLC_BLOCK
    farq = jnp.zeros((NSA_HEADS, LANES), BF16).at[:, bpt:bpt + 3].set(
        jnp.stack(_split3(rel_bias.astype(F32)[REL_BUCKETS - 1]), axis=1))
    farq = jnp.repeat(farq, TQ, axis=0)
    cos_t, sin_t = _rotary_tables(S)
    row = lambda v: v.reshape(1, -1).astype(F32)
    tm = min(512, T)
    tf = 512 if ffn1_w1.shape[2] % 512 == 0 else ffn1_w1.shape[2]
    x2 = x.reshape(T, D).astype(F32)
    for l in range(depth):
        x2 = _ffn(x2, _to_bf16_col_blocks(ffn1_w1, l, tf), _to_bf16_col_blocks(ffn1_w3, l, tf),
                  _to_bf16(ffn1_w2, l), row(ln1_g[l]), row(ln1_b[l]), tm)
        o_a, o_b, o_c, o_d, wa, wb, wc, wd = _mixer(x2, B, S, l, p, lower_bounds, band, cband, farq, cos_t, sin_t)
        x2 = _outproj(x2, o_a, o_b, o_c, o_d, wa, wb, wc, wd, row(ln2_g[l]), row(ln2_b[l]), min(512, T))
        x2 = _ffn(x2, _to_bf16_col_blocks(ffn2_w1, l, tf), _to_bf16_col_blocks(ffn2_w3, l, tf),
                  _to_bf16(ffn2_w2, l), row(ln3_g[l]), row(ln3_b[l]), tm)
    return x2.reshape(B, S, D).astype(x.dtype)
```

```python
import functools
import math

import numpy as np
import jax
import jax.numpy as jnp
from jax import lax
from jax.experimental import pallas as pl
from jax.experimental.pallas import tpu as pltpu

F32 = jnp.float32
BF16 = jnp.bfloat16

DEPTH = 2
GROUP_W = 512
ALPHA = (2 * DEPTH) ** 0.25
HG_HEADS = 4
NSA_HEADS = 8
NSA_KV = 2
NSA_GROUP = 4
NSA_HEAD_DIM = 64
NSA_CMP_STRIDE = 16
NSA_CMP_BLOCK = 32
NSA_SLC_BLOCK = 64
NSA_TOP_N = 16
NSA_WINDOW = 512
NSA_CMP_HIDDEN = 256
SSM_HEADS = 8
SSM_HEAD_DIM = 64
SSM_GROUPS = 2
SSM_STATE = 128
SSM_CONV = 4
RET_HEADS = 4
RET_DK = 128
REL_BUCKETS = 32
REL_EXACT = 16
REL_MAX_DIST = 2048
IN_SIZES = ((GROUP_W,) * 4 + (GROUP_W,) + (128,) * 6 + (24,)
            + (GROUP_W, 1024, SSM_HEADS) + (GROUP_W,) * 4)
IN_SPLITS = tuple(int(v) for v in np.cumsum(IN_SIZES)[:-1])

LANES = 128
SUBLANES = 8
VMEM_LIMIT = 56 * 1024 * 1024
CAST_BLOCK_BYTES = 4 * 1024 * 1024

COL = dict(hq=0, hf=4, hi=8, hg=12, sxbc=16, nq=24, sz=28, rq=32, rk=36, rv=40, rg=44)
NCOL = 48
KV_COLS = ("nkc", "nvc", "nks", "nvs", "nkw", "nvw", "small")
GATE_LANE0 = 0
DT_LANE0 = 24

CHUNK = 128
RET_CHUNK = 256
TQ = 128
TK = 1024
BAND_OFF = 2
CMP_BAND_OFF = 1
CMP_STEP = 256
WIN_SPAN = NSA_WINDOW + 2 * TQ
NEG = -1e30


def _params(sem):
    return pltpu.CompilerParams(dimension_semantics=sem, vmem_limit_bytes=VMEM_LIMIT)


def _dot(a, b):
    return jnp.dot(a, b, preferred_element_type=F32)


def _dot_nt(a, b):
    return lax.dot_general(a, b, (((1,), (1,)), ((), ())), preferred_element_type=F32)


def _split3(x):
    hi = x.astype(BF16)
    r1 = x - hi.astype(F32)
    mid = r1.astype(BF16)
    return hi, mid, (r1 - mid.astype(F32)).astype(BF16)


def _exact_left_dot(w, x):
    n = x.shape[1]
    y = _dot(w, jnp.concatenate(_split3(x), axis=1))
    return y[:, 0:n] + y[:, n:2 * n] + y[:, 2 * n:3 * n]


def _exact_right_dot(x, w):
    n = x.shape[0]
    y = _dot(jnp.concatenate(_split3(x), axis=0), w)
    return y[0:n] + y[n:2 * n] + y[2 * n:3 * n]


def _sigmoid(x):
    return 1.0 / (1.0 + jnp.exp(-x))


def _silu(x):
    return x * _sigmoid(x)


def _softplus(x):
    return jnp.maximum(x, 0.0) + jnp.log1p(jnp.exp(-jnp.abs(x)))


def _layer_norm(r, g, b):
    mu = jnp.mean(r, axis=-1, keepdims=True)
    d = r - mu
    var = jnp.mean(d * d, axis=-1, keepdims=True)
    return d * lax.rsqrt(var + 1e-5) * g + b


def _cast_kernel(x_ref, o_ref):
    o_ref[...] = x_ref[...].astype(o_ref.dtype)


def _to_bf16(w, l):
    _, r, c = w.shape
    tr = min(r, max(16, CAST_BLOCK_BYTES // (4 * c) // 16 * 16))
    while r % tr:
        tr -= 16
    return pl.pallas_call(
        _cast_kernel,
        grid=(r // tr,),
        in_specs=[pl.BlockSpec((None, tr, c), lambda i: (l, i, 0))],
        out_specs=pl.BlockSpec((tr, c), lambda i: (i, 0)),
        out_shape=jax.ShapeDtypeStruct((r, c), BF16),
        compiler_params=_params(("parallel",)),
        name="cast_bf16",
    )(w)


def _to_bf16_col_blocks(w, l, tc):
    _, r, c = w.shape
    return pl.pallas_call(
        _cast_kernel,
        grid=(c // tc,),
        in_specs=[pl.BlockSpec((None, r, tc), lambda j: (l, 0, j))],
        out_specs=pl.BlockSpec((None, r, tc), lambda j: (j, 0, 0)),
        out_shape=jax.ShapeDtypeStruct((c // tc, r, tc), BF16),
        compiler_params=_params(("parallel",)),
        name="cast_bf16_blocked",
    )(w)


def _ffn_kernel(x_ref, w1_ref, w3_ref, w2_ref, g_ref, b_ref, o_ref, acc_ref, xb_ref):
    j = pl.program_id(1)

    @pl.when(j == 0)
    def _():
        xb_ref[...] = x_ref[...].astype(BF16)
        acc_ref[...] = jnp.zeros_like(acc_ref)

    xb = xb_ref[...]
    h1 = _dot(xb, w1_ref[...])
    h3 = _dot(xb, w3_ref[...])
    a = (_silu(h1) * h3).astype(BF16)
    acc_ref[...] += _dot(a, w2_ref[...])

    @pl.when(j == pl.num_programs(1) - 1)
    def _():
        r = ALPHA * x_ref[...] + 0.5 * acc_ref[...]
        o_ref[...] = _layer_norm(r, g_ref[...], b_ref[...])


def _ffn(x, w1, w3, w2, g, b, tm):
    T, D = x.shape
    nf, _, tf = w1.shape
    return pl.pallas_call(
        _ffn_kernel,
        grid=(T // tm, nf),
        in_specs=[
            pl.BlockSpec((tm, D), lambda i, j: (i, 0)),
            pl.BlockSpec((None, D, tf), lambda i, j: (j, 0, 0)),
            pl.BlockSpec((None, D, tf), lambda i, j: (j, 0, 0)),
            pl.BlockSpec((tf, D), lambda i, j: (j, 0)),
            pl.BlockSpec((1, D), lambda i, j: (0, 0)),
            pl.BlockSpec((1, D), lambda i, j: (0, 0)),
        ],
        out_specs=pl.BlockSpec((tm, D), lambda i, j: (i, 0)),
        out_shape=jax.ShapeDtypeStruct((T, D), F32),
        scratch_shapes=[pltpu.VMEM((tm, D), F32), pltpu.VMEM((tm, D), BF16)],
        compiler_params=_params(("parallel", "arbitrary")),
        name="ffn_ln",
    )(x, w1, w3, w2, g, b)


def _proj_kernel(x_ref, w_ref, o_ref, xb_ref):
    @pl.when(pl.program_id(1) == 0)
    def _():
        xb_ref[...] = x_ref[...].astype(BF16)

    o_ref[...] = _dot(xb_ref[...], w_ref[...])


def _proj(x, w, tm):
    T, D = x.shape
    nn, _, tn = w.shape
    return pl.pallas_call(
        _proj_kernel,
        grid=(T // tm, nn),
        in_specs=[pl.BlockSpec((tm, D), lambda i, j: (i, 0)),
                  pl.BlockSpec((None, D, tn), lambda i, j: (j, 0, 0))],
        out_specs=pl.BlockSpec((tm, tn), lambda i, j: (i, j)),
        out_shape=jax.ShapeDtypeStruct((T, nn * tn), F32),
        scratch_shapes=[pltpu.VMEM((tm, D), BF16)],
        compiler_params=_params(("parallel", "arbitrary")),
        name="in_proj",
    )(x, w)


def _kvproj_kernel(x_ref, w_ref, kc_ref, vc_ref, ks_ref, vs_ref, kw_ref, vw_ref, sm_ref):
    tm = x_ref.shape[0]
    y = _dot(x_ref[...].astype(BF16), w_ref[...])
    piece = lambda n: y[:, n * LANES:(n + 1) * LANES]
    kc_ref[...] = piece(0)
    vc_ref[...] = piece(1)
    row = pl.program_id(0) * tm + lax.broadcasted_iota(jnp.int32, (tm, LANES), 0)
    lane = lax.broadcasted_iota(jnp.int32, (tm, LANES), 1)
    bpt = TK // NSA_SLC_BLOCK
    onehot = (((row // NSA_SLC_BLOCK) % bpt == lane) | ((lane >= bpt) & (lane < bpt + 3))).astype(BF16)
    ks_ref[...] = jnp.concatenate([piece(2).astype(BF16), onehot], axis=1)
    vs_ref[...] = jnp.concatenate([piece(3).astype(BF16), jnp.ones((tm, LANES), BF16)], axis=1)
    kw_ref[...] = piece(4).astype(BF16)
    vw_ref[...] = piece(5).astype(BF16)
    sm_ref[...] = piece(6)


def _kvproj(x, w, tm):
    T, D = x.shape
    narrow = lambda dt, width=LANES: (pl.BlockSpec((tm, width), lambda i: (i, 0)),
                                      jax.ShapeDtypeStruct((T, width), dt))
    outs = [narrow(F32), narrow(F32), narrow(BF16, 2 * LANES), narrow(BF16, 2 * LANES),
            narrow(BF16), narrow(BF16), narrow(F32)]
    return pl.pallas_call(
        _kvproj_kernel,
        grid=(T // tm,),
        in_specs=[pl.BlockSpec((tm, D), lambda i: (i, 0)), pl.BlockSpec(w.shape, lambda i: (0, 0))],
        out_specs=[o[0] for o in outs],
        out_shape=[o[1] for o in outs],
        compiler_params=_params(("parallel",)),
        name="kv_proj",
    )(x, w)


def _outproj_kernel(x_ref, oa_ref, ob_ref, oc_ref, od_ref, wa_ref, wb_ref, wc_ref, wd_ref,
                    g_ref, b_ref, o_ref):
    mix = (_dot(oa_ref[...], wa_ref[...]) + _dot(ob_ref[...], wb_ref[...])
           + _dot(oc_ref[...], wc_ref[...]) + _dot(od_ref[...], wd_ref[...]))
    o_ref[...] = _layer_norm(ALPHA * x_ref[...] + mix, g_ref[...], b_ref[...])


def _outproj(x, oa, ob, oc, od, wa, wb, wc, wd, g, b, tm):
    T, D = x.shape
    row = lambda a: pl.BlockSpec((tm, a.shape[1]), lambda i: (i, 0))
    full = lambda a: pl.BlockSpec(a.shape, lambda i: (0, 0))
    return pl.pallas_call(
        _outproj_kernel,
        grid=(T // tm,),
        in_specs=[row(x), row(oa), row(ob), row(oc), row(od),
                  full(wa), full(wb), full(wc), full(wd), full(g), full(b)],
        out_specs=row(x),
        out_shape=jax.ShapeDtypeStruct((T, D), F32),
        compiler_params=_params(("parallel",)),
        name="out_proj_ln",
    )(x, oa, ob, oc, od, wa, wb, wc, wd, g, b)


def _hgrn_tables(C):
    i = np.arange(C)[:, None]
    ip = np.arange(C)[None, :]
    seg = [(ip <= i),
           (ip > i)]
    masks = [np.eye(C, dtype=bool)]
    s = C // 2
    while s >= 1:
        blk = i // s
        if s < SUBLANES:
            seg.append(np.where(blk % 2 == 1, (ip > blk * s) & (ip <= i), (ip > i) & (ip <= (blk + 1) * s)))
        masks.append((blk % 2 == 1) & (ip // s == blk - 1))
        s //= 2
    seg = np.concatenate([x.astype(np.float32) for x in seg], axis=0)
    return seg, np.stack([m.astype(np.float32) for m in masks])


def _hgrn_kernel(q_ref, f_ref, i_ref, g_ref, llb_ref, l1m_ref, oml_ref, ng_ref,
                 seg_ref, msk_ref, o_ref, st_ref):
    @pl.when(pl.program_id(0) == 0)
    def _():
        st_ref[...] = jnp.zeros_like(st_ref)

    C = q_ref.shape[1]
    nlev = msk_ref.shape[0] - 1
    for bi in range(q_ref.shape[0]):
        q = _silu(q_ref[bi])
        z = f_ref[bi]
        log_sig = jnp.minimum(z, 0.0) - jnp.log1p(jnp.exp(-jnp.abs(z)))
        cc = l1m_ref[...] + log_sig
        llb = llb_ref[...]
        logf = jnp.maximum(llb, cc) + jnp.log1p(jnp.exp(-jnp.abs(llb - cc)))
        k = oml_ref[...] * _sigmoid(-z)
        v = i_ref[bi]
        seg = _exact_left_dot(seg_ref[...], logf)
        b_all = seg[0:C]
        lev = []
        s = C // 2
        while s >= SUBLANES:
            b3 = b_all.reshape(C // s, s, GROUP_W)
            start = b3[:, 0:1, :]
            nxt = jnp.concatenate([start[1:], start[-1:]], axis=0)
            odd = lax.broadcasted_iota(jnp.int32, b3.shape, 0) % 2 == 1
            lev.append(jnp.where(odd, b3 - start, nxt - b3).reshape(C, GROUP_W))
            s //= 2
        lev += [seg[r * C:(r + 1) * C] for r in range(2, seg.shape[0] // C)]
        outs = []
        for h in range(HG_HEADS):
            sl = slice(h * LANES, (h + 1) * LANES)
            qh, kh, vh = q[:, sl], k[:, sl], v[:, sl]
            a = msk_ref[0] * _dot_nt(qh.astype(BF16), kh.astype(BF16))
            for l in range(nlev):
                dec = jnp.exp(lev[l][:, sl])
                a = a + msk_ref[1 + l] * _dot_nt((qh * dec).astype(BF16), (kh * dec).astype(BF16))
            b = seg[0:C, sl]
            st = st_ref[bi, h]
            o = _dot(a.astype(BF16), vh.astype(BF16))
            o = o + _dot_nt((qh * jnp.exp(b)).astype(BF16), st.astype(BF16))
            kd = (kh * jnp.exp(seg[C:2 * C, sl])).astype(BF16)
            st_ref[bi, h] = st * jnp.exp(b[C - 1:C, :]) + _dot(vh.T.astype(BF16), kd)
            outs.append(o * lax.rsqrt(jnp.mean(o * o, axis=-1, keepdims=True) + 1e-6))
        o = jnp.concatenate(outs, axis=1)
        o_ref[bi] = (o * ng_ref[...] * _silu(g_ref[bi])).astype(o_ref.dtype)


def _hgrn(proj, B, S, llb, l1m, oml, ng):
    C = CHUNK
    nc = S // C
    seg, msk = _hgrn_tables(C)
    seg, msk = jnp.asarray(seg, BF16), jnp.asarray(msk)
    proj3 = proj.reshape(B, S, proj.shape[1])
    col = lambda name: pl.BlockSpec((B, C, GROUP_W), lambda c, n=COL[name] // 4: (0, c, n))
    vec = pl.BlockSpec((1, GROUP_W), lambda c: (0, 0))
    full2 = lambda a: pl.BlockSpec(a.shape, lambda c: (0, 0))
    return pl.pallas_call(
        _hgrn_kernel,
        grid=(nc,),
        in_specs=[col("hq"), col("hf"), col("hi"), col("hg"), vec, vec, vec, vec,
                  full2(seg), pl.BlockSpec(msk.shape, lambda c: (0, 0, 0))],
        out_specs=pl.BlockSpec((B, C, GROUP_W), lambda c: (0, c, 0)),
        out_shape=jax.ShapeDtypeStruct((B, S, GROUP_W), BF16),
        scratch_shapes=[pltpu.VMEM((B, HG_HEADS, LANES, LANES), F32)],
        compiler_params=_params(("arbitrary",)),
        name="hgrn2",
    )(proj3, proj3, proj3, proj3, llb, l1m, oml, ng, seg, msk).reshape(B * S, GROUP_W)


def _ssd_kernel(z_ref, xbc_ref, sm_ref, cw_ref, cb_ref, dtb_ref, aneg_ref, dsk_ref, ng_ref,
                ex_ref, o_ref, tail_ref, st_ref):
    @pl.when(pl.program_id(0) == 0)
    def _():
        tail_ref[...] = jnp.zeros_like(tail_ref)
        st_ref[...] = jnp.zeros_like(st_ref)

    for b in range(xbc_ref.shape[0]):
        _ssd_chunk(z_ref.at[b], xbc_ref.at[b], sm_ref.at[b], cw_ref, cb_ref, dtb_ref, aneg_ref, dsk_ref, ng_ref,
                   ex_ref, o_ref.at[b], tail_ref.at[b], st_ref.at[b])


def _ssd_chunk(z_ref, xbc_ref, sm_ref, cw_ref, cb_ref, dtb_ref, aneg_ref, dsk_ref, ng_ref,
               ex_ref, o_ref, tail_ref, st_ref):
    L = xbc_ref.shape[0]
    x = xbc_ref[...]
    xe = jnp.concatenate([tail_ref[...], x], axis=0)
    cw = cw_ref[...]
    conv = cb_ref[...]
    for kk in range(SSM_CONV):
        conv = conv + cw[kk:kk + 1, :] * xe[5 + kk:5 + kk + L, :]
    tail_ref[...] = x[L - 8:L, :]
    conv = _silu(conv)
    xs = conv[:, 0:GROUP_W]
    bm = conv[:, GROUP_W:GROUP_W + 256]
    cm = conv[:, GROUP_W + 256:GROUP_W + 512]

    dtf = _softplus(sm_ref[...] + dtb_ref[...])
    la = dtf * aneg_ref[...]
    ri = lax.broadcasted_iota(jnp.int32, (L, L), 0)
    ci = lax.broadcasted_iota(jnp.int32, (L, L), 1)
    tri = ri >= ci
    bfull = _exact_left_dot(tri.astype(BF16), la)
    ex = ex_ref[...]
    bexp = _exact_right_dot(bfull, ex)
    dtexp = _exact_right_dot(dtf, ex)
    b_t = bfull.T
    xdt = xs * dtexp
    lane = lax.broadcasted_iota(jnp.int32, (L, LANES), 1)

    scores = []
    for g in range(SSM_GROUPS):
        cg = cm[:, g * SSM_STATE:(g + 1) * SSM_STATE].astype(BF16)
        bg = bm[:, g * SSM_STATE:(g + 1) * SSM_STATE].astype(BF16)
        cb = _dot_nt(cg, bg)
        for hh in range(SSM_HEADS // SSM_GROUPS):
            h = g * (SSM_HEADS // SSM_GROUPS) + hh
            bcol = bfull[:, DT_LANE0 + h:DT_LANE0 + h + 1]
            brow = b_t[DT_LANE0 + h:DT_LANE0 + h + 1, :]
            dec = jnp.exp(jnp.where(tri, bcol - brow, NEG))
            scores.append((cb * dec).astype(BF16))
    y_pairs = []
    for u in range(SSM_HEADS // 2):
        slab = xdt[:, u * LANES:(u + 1) * LANES]
        lo = jnp.where(lane < SSM_HEAD_DIM, slab, 0.0).astype(BF16)
        hi = jnp.where(lane >= SSM_HEAD_DIM, slab, 0.0).astype(BF16)
        y_pairs.append(_dot(scores[2 * u], lo) + _dot(scores[2 * u + 1], hi))
    y_intra = jnp.concatenate(y_pairs, axis=1)

    blast = bexp[L - 1:L, :]
    w = (xdt * jnp.exp(blast - bexp)).astype(BF16)
    y_inter = []
    for g in range(SSM_GROUPS):
        gs = slice(g * 256, (g + 1) * 256)
        cg = cm[:, g * SSM_STATE:(g + 1) * SSM_STATE].astype(BF16)
        st = st_ref[g]
        y_inter.append(_dot(cg, st.astype(BF16)))
        bg_t = bm[:, g * SSM_STATE:(g + 1) * SSM_STATE].T.astype(BF16)
        st_ref[g] = st * jnp.exp(blast[:, gs]) + _dot(bg_t, w[:, gs])
    y = y_intra + jnp.concatenate(y_inter, axis=1) * jnp.exp(bexp) + dsk_ref[...] * xs
    y = y * _silu(z_ref[...])
    halves = []
    for g in range(SSM_GROUPS):
        seg = y[:, g * 256:(g + 1) * 256]
        halves.append(seg * lax.rsqrt(jnp.mean(seg * seg, axis=-1, keepdims=True) + 1e-6))
    o_ref[...] = (jnp.concatenate(halves, axis=1) * ng_ref[...]).astype(o_ref.dtype)


def _ssd(proj, small, B, S, cw, cb, dtb, aneg, dsk, ng):
    L = CHUNK
    nc = S // L
    ex = np.zeros((LANES, GROUP_W), np.float32)
    for h in range(SSM_HEADS):
        ex[DT_LANE0 + h, h * SSM_HEAD_DIM:(h + 1) * SSM_HEAD_DIM] = 1.0
    ex = jnp.asarray(ex, BF16)
    full2 = lambda a: pl.BlockSpec(a.shape, lambda c: (0, 0))
    proj3 = proj.reshape(B, S, proj.shape[1])
    return pl.pallas_call(
        _ssd_kernel,
        grid=(nc,),
        in_specs=[
            pl.BlockSpec((B, L, GROUP_W), lambda c: (0, c, COL["sz"] // 4)),
            pl.BlockSpec((B, L, 1024), lambda c: (0, c, COL["sxbc"] // 8)),
            pl.BlockSpec((B, L, LANES), lambda c: (0, c, 0)),
            full2(cw), full2(cb), full2(dtb), full2(aneg), full2(dsk), full2(ng), full2(ex)],
        out_specs=pl.BlockSpec((B, L, GROUP_W), lambda c: (0, c, 0)),
        out_shape=jax.ShapeDtypeStruct((B, S, GROUP_W), BF16),
        scratch_shapes=[pltpu.VMEM((B, 8, 1024), F32), pltpu.VMEM((B, SSM_GROUPS, SSM_STATE, 256), F32)],
        compiler_params=_params(("arbitrary",)),
        name="ssd",
    )(proj3, proj3, small.reshape(B, S, LANES), cw, cb, dtb, aneg, dsk, ng, ex).reshape(B * S, GROUP_W)


def _ret_kernel(q_ref, k_ref, v_ref, g_ref, cos_ref, sin_ref, dec_ref, qs_ref, ks_ref, sd_ref,
                o_ref, st_ref):
    @pl.when(pl.program_id(0) == 0)
    def _():
        st_ref[...] = jnp.zeros_like(st_ref)

    cos = cos_ref[...]
    sin = sin_ref[...]
    for b in range(q_ref.shape[0]):
        outs = []
        for h in range(RET_HEADS):
            sl = slice(h * LANES, (h + 1) * LANES)
            qh = q_ref[b, :, sl]
            kh = k_ref[b, :, sl]
            qh = qh * cos + pltpu.roll(qh, RET_DK // 2, axis=1) * sin
            kh = (kh * cos + pltpu.roll(kh, RET_DK // 2, axis=1) * sin) * (RET_DK ** -0.5)
            vh = v_ref[b, :, sl].astype(BF16)
            sc = (_dot_nt(qh.astype(BF16), kh.astype(BF16)) * dec_ref[h]).astype(BF16)
            st = st_ref[b, h]
            y = _dot(sc, vh) + _dot((qh * qs_ref[:, sl]).astype(BF16), st.astype(BF16))
            kd_t = (kh * ks_ref[:, sl]).T.astype(BF16)
            st_ref[b, h] = st * sd_ref[h] + _dot(kd_t, vh)
            mu = jnp.mean(y, axis=-1, keepdims=True)
            d = y - mu
            outs.append(d * lax.rsqrt(jnp.mean(d * d, axis=-1, keepdims=True) + 1e-5))
        o_ref[b] = (_silu(g_ref[b]) * jnp.concatenate(outs, axis=1)).astype(o_ref.dtype)


def _retention(proj, B, S, cos_t, sin_t):
    L = min(RET_CHUNK, S)
    nc = S // L
    lg = jnp.log(1.0 - 2.0 ** (-5.0 - jnp.arange(RET_HEADS, dtype=F32)))
    i = jnp.arange(L, dtype=F32)
    diff = i[:, None] - i[None, :]
    dec = jnp.where(diff >= 0, jnp.exp(lg[:, None, None] * jnp.maximum(diff, 0.0)), 0.0)
    rep = lambda t: jnp.repeat(t, LANES, axis=1)
    qs = rep(jnp.exp((i[:, None] + 1.0) * lg[None, :]))
    ks = rep(jnp.exp((L - 1.0 - i[:, None]) * lg[None, :]))
    sd = jnp.broadcast_to(jnp.exp(L * lg)[:, None, None], (RET_HEADS, LANES, LANES))
    proj3 = proj.reshape(B, S, proj.shape[1])
    col = lambda name: pl.BlockSpec((B, L, GROUP_W), lambda c, n=COL[name] // 4: (0, c, n))
    return pl.pallas_call(
        _ret_kernel,
        grid=(nc,),
        in_specs=[col("rq"), col("rk"), col("rv"), col("rg"),
                  pl.BlockSpec((L, LANES), lambda c: (c, 0)),
                  pl.BlockSpec((L, LANES), lambda c: (c, 0)),
                  pl.BlockSpec((RET_HEADS, L, L), lambda c: (0, 0, 0)),
                  pl.BlockSpec((L, GROUP_W), lambda c: (0, 0)),
                  pl.BlockSpec((L, GROUP_W), lambda c: (0, 0)),
                  pl.BlockSpec((RET_HEADS, LANES, LANES), lambda c: (0, 0, 0))],
        out_specs=pl.BlockSpec((B, L, GROUP_W), lambda c: (0, c, 0)),
        out_shape=jax.ShapeDtypeStruct((B, S, GROUP_W), BF16),
        scratch_shapes=[pltpu.VMEM((B, RET_HEADS, RET_DK, RET_DK), F32)],
        compiler_params=_params(("arbitrary",)),
        name="retention",
    )(proj3, proj3, proj3, proj3, cos_t, sin_t, dec, qs, ks, sd).reshape(B * S, GROUP_W)


def _t5_bucket(dist):
    n = jnp.maximum(dist, 0)
    nf = jnp.maximum(n, 1).astype(F32)
    large = REL_EXACT + (jnp.log(nf / REL_EXACT) / math.log(REL_MAX_DIST / REL_EXACT)
                         * (REL_BUCKETS - REL_EXACT)).astype(jnp.int32)
    return jnp.where(n < REL_EXACT, n, jnp.minimum(large, REL_BUCKETS - 1))


def _head_bias(bucket, rel_ref):
    rows, cols = bucket.shape
    per_head = []
    for h in range(NSA_HEADS):
        tbl = jnp.broadcast_to(rel_ref[h:h + 1, :], (rows, LANES))
        chunks = [jnp.take_along_axis(tbl, bucket[:, c:c + LANES], axis=1)
                  for c in range(0, cols, LANES)]
        per_head.append(chunks[0] if len(chunks) == 1 else jnp.concatenate(chunks, axis=1))
    return jnp.stack(per_head, axis=0)


def _stack_heads(q):
    lane = lax.broadcasted_iota(jnp.int32, (q.shape[0], LANES), 1)
    rows = []
    for h in range(NSA_HEADS):
        slab = q[:, (h // 2) * LANES:(h // 2 + 1) * LANES]
        src_half, dst_half = h % 2, h // NSA_GROUP
        if src_half != dst_half:
            slab = pltpu.roll(slab, NSA_HEAD_DIM, axis=1)
        rows.append(jnp.where(lane // NSA_HEAD_DIM == dst_half, slab, 0.0))
    return jnp.concatenate(rows, axis=0).astype(BF16)


def _cmp_kernel(g_ref, pe_ref, w1a_ref, w1b_ref, w2_ref, o_ref):
    nb = g_ref.shape[0]
    gw = g_ref.shape[1] // 4
    pe = pe_ref[...]
    slabs = [g_ref[:, s * gw:(s + 1) * gw] for s in range(4)]
    nxt0 = pltpu.roll(slabs[0], nb - 1, axis=0)
    for s in range(4):
        a = (slabs[s] + pe[0:1, :]).astype(BF16)
        bn = ((slabs[s + 1] if s < 3 else nxt0) + pe[1:2, :]).astype(BF16)
        hid = _silu(_dot(a, w1a_ref[...]) + _dot(bn, w1b_ref[...]))
        o_ref[:, s * LANES:(s + 1) * LANES] = _dot(hid.astype(BF16), w2_ref[...]).astype(o_ref.dtype)


def _compress(g, pe2, w1a, w1b, w2bd):
    B, nb, gw4 = g.shape
    full2 = lambda a: pl.BlockSpec(a.shape, lambda b: (0, 0))
    return pl.pallas_call(
        _cmp_kernel,
        grid=(B,),
        in_specs=[pl.BlockSpec((None, nb, gw4), lambda b: (b, 0, 0)),
                  full2(pe2), full2(w1a), full2(w1b), full2(w2bd)],
        out_specs=pl.BlockSpec((None, nb, 4 * LANES), lambda b: (b, 0, 0)),
        out_shape=jax.ShapeDtypeStruct((B, nb, 4 * LANES), BF16),
        compiler_params=_params(("parallel",)),
        name="nsa_compress",
    )(g, pe2, w1a, w1b, w2bd).reshape(B, 4 * nb, LANES)


def _cmpattn_kernel(q_ref, kc_ref, vc_ref, band_ref, st_ref, oc_ref, selb_ref):
    tq = q_ref.shape[1]
    ncmp = kc_ref.shape[1]
    nb = ncmp // 4
    R = NSA_HEADS * tq
    q0 = pl.program_id(0) * tq
    nd = band_ref.shape[0]

    def attend(ncv, nbv):
        for b in range(q_ref.shape[0]):
            attend_one(b, ncv, nbv)

    def attend_one(b, ncv, nbv):
        Q = _stack_heads(q_ref[b])
        s3 = _dot_nt(Q, kc_ref[b, 0:ncv, :]).reshape(NSA_HEADS, tq, ncv)
        parts = []
        for ch in range(ncv // LANES):
            d = jnp.clip((q0 - ch * LANES * NSA_CMP_STRIDE) // tq + CMP_BAND_OFF, 0, nd - 1)
            parts.append(s3[:, :, ch * LANES:(ch + 1) * LANES] + band_ref[d])
        s3 = jnp.concatenate(parts, axis=2)
        mx = jnp.max(s3, axis=-1, keepdims=True)
        e = jnp.exp(s3 - mx)
        live = (q0 + lax.broadcasted_iota(jnp.int32, (tq, 1), 0) >= NSA_CMP_BLOCK - 1)[None]
        p = e * jnp.where(live, 1.0 / jnp.sum(e, axis=-1, keepdims=True), 0.0)
        oc_ref[b] = _dot(p.reshape(R, ncv).astype(BF16), vc_ref[b, 0:ncv, :])

        ps = p.reshape(NSA_KV, NSA_GROUP, tq, ncv).sum(axis=1).reshape(NSA_KV * tq, ncv)
        imp = _exact_right_dot(ps, st_ref[0:ncv, 0:nbv])
        j = lax.broadcasted_iota(jnp.int32, (NSA_KV * tq, nbv), 1)
        t = q0 + (lax.broadcasted_iota(jnp.int32, (NSA_KV * tq, nbv), 0) % tq)
        cur = t // NSA_SLC_BLOCK
        forced = (j == 0) | (j == cur) | (j == cur - 1)
        score = jnp.where(j > cur, -1.0, jnp.where(forced, NSA_GROUP + 1.0, imp))
        score = score.T
        jf = lax.broadcasted_iota(jnp.int32, score.shape, 0).astype(F32)
        sel = score == NSA_GROUP + 1.0
        score = jnp.where(sel, -jnp.inf, score)
        for _ in range(min(NSA_TOP_N, nb) - 3):
            best = jnp.max(score, axis=0, keepdims=True)
            first = jnp.min(jnp.where(score == best, jf, float(nbv)), axis=0, keepdims=True)
            hit = jf == first
            sel = sel | hit
            score = jnp.where(hit, -jnp.inf, score)
        selb = jnp.where(sel, 0.0, NEG).T.astype(selb_ref.dtype)
        for kv in range(NSA_KV):
            selb_ref[b, :, kv * nb:kv * nb + nbv] = selb[kv * tq:(kv + 1) * tq]
            if nbv < nb:
                selb_ref[b, :, kv * nb + nbv:(kv + 1) * nb] = jnp.full((tq, nb - nbv), NEG, selb_ref.dtype)

    if ncmp % CMP_STEP:
        attend(ncmp, nb)
    else:
        nvar = ncmp // CMP_STEP
        sizes = [(CMP_STEP * v, min(nb, LANES * ((v + 1) // 2))) for v in range(1, nvar + 1)]
        last_key = (q0 + tq - 1) // NSA_CMP_STRIDE
        lax.switch(jnp.minimum(last_key // CMP_STEP, nvar - 1),
                   [functools.partial(attend, ncv, nbv) for ncv, nbv in sizes])


def _cmpattn(proj, kcmp, vcmp, cband, B, S):
    tq = TQ
    nqt = S // tq
    ncmp = kcmp.shape[1]
    nb = ncmp // 4
    off = np.arange(ncmp)[:, None] - 4 * np.arange(nb)[None, :]
    stencil = np.where((off >= 0) & (off <= 2), 1.0, np.where((off == -1) | (off == 3), 0.5, 0.0))
    stencil = jnp.asarray(stencil, BF16)
    once = pl.Buffered(1)
    R = NSA_HEADS * tq
    o_cmp, selb = pl.pallas_call(
        _cmpattn_kernel,
        grid=(nqt,),
        in_specs=[pl.BlockSpec((B, tq, GROUP_W), lambda i: (0, i, COL["nq"] // 4)),
                  pl.BlockSpec((B, ncmp, LANES), lambda i: (0, 0, 0), pipeline_mode=once),
                  pl.BlockSpec((B, ncmp, LANES), lambda i: (0, 0, 0), pipeline_mode=once),
                  pl.BlockSpec(cband.shape, lambda i: (0, 0, 0, 0), pipeline_mode=once),
                  pl.BlockSpec(stencil.shape, lambda i: (0, 0), pipeline_mode=once)],
        out_specs=[pl.BlockSpec((B, R, LANES), lambda i: (0, i, 0)),
                   pl.BlockSpec((B, tq, 2 * nb), lambda i: (0, i, 0))],
        out_shape=[jax.ShapeDtypeStruct((B, S * NSA_HEADS, LANES), F32),
                   jax.ShapeDtypeStruct((B, S, 2 * nb), BF16)],
        compiler_params=_params(("arbitrary",)),
        name="nsa_cmp_attn_topk",
    )(proj.reshape(B, S, proj.shape[1]), kcmp, vcmp, cband, stencil)
    return o_cmp.reshape(B * S * NSA_HEADS, LANES), selb.reshape(B * S, 2 * nb)


def _band_kernel(rel_ref, o_ref, *, entry_off, key_step, key_end):
    tq = o_ref.shape[1]
    delta = (pl.program_id(0) - entry_off) * tq
    row = lax.broadcasted_iota(jnp.int32, (tq, LANES), 0)
    col = lax.broadcasted_iota(jnp.int32, (tq, LANES), 1)
    dist = delta + row - (col * key_step + key_end)
    bias = _head_bias(_t5_bucket(dist), rel_ref)
    o_ref[...] = jnp.where((dist >= 0)[None], bias, NEG)


def _band_table(rel_t, tq, entry_off, key_step=1, key_end=0):
    nd = -(-(REL_MAX_DIST + key_step * (LANES - 1) + key_end) // tq) + entry_off + 1
    return pl.pallas_call(
        functools.partial(_band_kernel, entry_off=entry_off, key_step=key_step, key_end=key_end),
        grid=(nd,),
        in_specs=[pl.BlockSpec(rel_t.shape, lambda d: (0, 0))],
        out_specs=pl.BlockSpec((None, NSA_HEADS, tq, LANES), lambda d: (d, 0, 0, 0)),
        out_shape=jax.ShapeDtypeStruct((nd, NSA_HEADS, tq, LANES), F32),
        compiler_params=_params(("parallel",)),
        name="nsa_bias_band",
    )(rel_t)


def _selattn_kernel(q_ref, selb_ref, ks_ref, vs_ref, band_ref, sp_ref, farq_ref, os_ref,
                    qaug_ref, msel_ref, s_ref, mt_ref, m_ref, acc_ref):
    tq = q_ref.shape[0]
    R = NSA_HEADS * tq
    nb = selb_ref.shape[1] // 2
    n_kt = msel_ref.shape[0]
    tk = ks_ref.shape[0] // n_kt
    nd = band_ref.shape[0]
    q0 = pl.program_id(1) * tq
    qaug_ref[:, 0:LANES] = _stack_heads(q_ref[...])
    selb2 = jnp.concatenate([selb_ref[:, 0:nb], selb_ref[:, nb:2 * nb]], axis=0)
    spread = _dot(selb2, sp_ref[...])
    for c in range(n_kt):
        msel_ref[c] = spread[:, c * LANES:(c + 1) * LANES].astype(BF16)
    m_ref[...] = jnp.full(m_ref.shape, NEG, F32)
    acc_ref[...] = jnp.zeros_like(acc_ref)
    n_tiles = (q0 + tq - 1) // tk + 1

    far_pairs = (jnp.maximum(q0 - (REL_MAX_DIST - 1), 0) // tk) // 2

    def scores(c, slot, far):
        c = jnp.minimum(c, n_kt - 1)
        k0 = pl.multiple_of(c * tk, tk)
        m2 = msel_ref[c]
        flags = jnp.concatenate([m2[0:tq]] * NSA_GROUP + [m2[tq:2 * tq]] * NSA_GROUP, axis=0)
        qaug_ref[:, LANES:2 * LANES] = flags + farq_ref[...] if far else flags
        s = _dot_nt(qaug_ref[...], ks_ref[pl.ds(k0, tk), :])
        if not far:
            s3 = s.reshape(NSA_HEADS, tq, tk)
            parts = []
            for ch in range(tk // LANES):
                d = jnp.clip((q0 - k0) // tq - ch * (LANES // tq) + BAND_OFF, 0, nd - 1)
                parts.append(s3[:, :, ch * LANES:(ch + 1) * LANES] + band_ref[d])
            s = jnp.concatenate(parts, axis=2).reshape(R, tk)
        s_ref[slot] = s
        mt_ref[slot] = jnp.broadcast_to(jnp.max(s, axis=-1, keepdims=True), (R, LANES))

    def accumulate(c, slot):
        m_old = m_ref[...]
        m_new = jnp.maximum(m_old, mt_ref[slot])
        alpha = jnp.exp(m_old - m_new)
        p = jnp.exp(s_ref[slot] - jnp.tile(m_new, (1, tk // LANES)))
        vt = vs_ref[pl.ds(pl.multiple_of(c * tk, tk), tk), :]
        acc_ref[...] = jnp.tile(alpha, (1, 2)) * acc_ref[...] + _dot(p.astype(BF16), vt)
        m_ref[...] = m_new

    def pair(i, far_odd, far_even):
        c = 2 * i
        scores(c + 1, 1, far_odd)
        accumulate(c, 0)
        scores(c + 2, 0, far_even)
        accumulate(c + 1, 1)

    scores(0, 0, False)
    last_far = jnp.maximum(far_pairs - 1, 0)
    lax.fori_loop(0, last_far, lambda i, carry: pair(i, True, True), None)
    lax.fori_loop(last_far, far_pairs, lambda i, carry: pair(i, True, False), None)
    lax.fori_loop(far_pairs, n_tiles // 2, lambda i, carry: pair(i, False, False), None)

    @pl.when(n_tiles % 2 == 1)
    def _():
        accumulate(n_tiles - 1, 0)

    acc = acc_ref[...]
    os_ref[...] = acc[:, 0:LANES] / acc[:, LANES:2 * LANES]


def _selattn(proj, selb, ks, vs, band, farq, B, S):
    tq = TQ
    tk = min(TK, S)
    n_kt = S // tk
    bpt = tk // NSA_SLC_BLOCK
    nqt = S // tq
    nb = S // NSA_SLC_BLOCK
    R = NSA_HEADS * tq
    sp = np.zeros((nb, n_kt * LANES), np.float32)
    sp[np.arange(nb), (np.arange(nb) // bpt) * LANES + np.arange(nb) % bpt] = 1.0
    once = pl.Buffered(1)
    return pl.pallas_call(
        _selattn_kernel,
        grid=(B, nqt),
        in_specs=[pl.BlockSpec((tq, GROUP_W), lambda b, i: (b * nqt + i, COL["nq"] // 4)),
                  pl.BlockSpec((tq, 2 * nb), lambda b, i: (b * nqt + i, 0)),
                  pl.BlockSpec((None, S, 2 * LANES), lambda b, i: (b, 0, 0), pipeline_mode=once),
                  pl.BlockSpec((None, S, 2 * LANES), lambda b, i: (b, 0, 0), pipeline_mode=once),
                  pl.BlockSpec(band.shape, lambda b, i: (0, 0, 0, 0), pipeline_mode=once),
                  pl.BlockSpec(sp.shape, lambda b, i: (0, 0)),
                  pl.BlockSpec(farq.shape, lambda b, i: (0, 0))],
        out_specs=pl.BlockSpec((R, LANES), lambda b, i: (b * nqt + i, 0)),
        out_shape=jax.ShapeDtypeStruct((B * S * NSA_HEADS, LANES), F32),
        scratch_shapes=[pltpu.VMEM((R, 2 * LANES), BF16), pltpu.VMEM((n_kt, 2 * tq, LANES), BF16),
                        pltpu.VMEM((2, R, tk), F32), pltpu.VMEM((2, R, LANES), F32),
                        pltpu.VMEM((R, LANES), F32), pltpu.VMEM((R, 2 * LANES), F32)],
        compiler_params=_params(("parallel", "arbitrary")),
        name="nsa_sel_attn",
    )(proj, selb, ks, vs, band, jnp.asarray(sp, BF16), farq)


def _winattn_kernel(q_ref, sm_ref, oc_ref, os_ref, kw_ref, vw_ref, band_ref, ng_ref, o_ref):
    tq = q_ref.shape[1]
    R = NSA_HEADS * tq
    S = kw_ref.shape[1]
    span = min(WIN_SPAN, S)
    nd = band_ref.shape[0]
    q0 = pl.program_id(0) * tq
    start = pl.multiple_of(jnp.clip(q0 + tq - span, 0, S - span), tq)
    row = lax.broadcasted_iota(jnp.int32, (tq, span), 0)
    col = lax.broadcasted_iota(jnp.int32, (tq, span), 1)
    in_window = ((q0 + row) - (start + col) < NSA_WINDOW)[None]
    lane = lax.broadcasted_iota(jnp.int32, (tq, LANES), 1)
    for b in range(q_ref.shape[0]):
        Q = _stack_heads(q_ref[b])
        kt = kw_ref[b, pl.ds(start, span), :]
        vt = vw_ref[b, pl.ds(start, span), :]
        s3 = _dot_nt(Q, kt).reshape(NSA_HEADS, tq, span)
        parts = []
        for ch in range(span // LANES):
            d = jnp.clip((q0 - start) // tq - ch * (LANES // tq) + BAND_OFF, 0, nd - 1)
            part = s3[:, :, ch * LANES:(ch + 1) * LANES] + band_ref[d]
            if (span - tq) + tq - 1 - ch * LANES >= NSA_WINDOW:
                part = jnp.where(in_window[:, :, ch * LANES:(ch + 1) * LANES], part, NEG)
            parts.append(part)
        s3 = jnp.concatenate(parts, axis=2)
        mx = jnp.max(s3, axis=-1, keepdims=True)
        e = jnp.exp(s3 - mx)
        inv = (1.0 / jnp.sum(e, axis=-1, keepdims=True)).reshape(R, 1)
        ow = _dot(e.reshape(R, span).astype(BF16), vt) * inv

        gates = _sigmoid(sm_ref[b])
        heads = []
        ssq = jnp.zeros((tq, 1), F32)
        for h in range(NSA_HEADS):
            rs = slice(h * tq, (h + 1) * tq)
            g = [gates[:, GATE_LANE0 + 3 * h + br:GATE_LANE0 + 3 * h + br + 1] for br in range(3)]
            oh = g[0] * oc_ref[b, rs, :] + g[1] * os_ref[b, rs, :] + g[2] * ow[rs, :]
            kv = h // NSA_GROUP
            valid = (lane >= kv * NSA_HEAD_DIM) & (lane < (kv + 1) * NSA_HEAD_DIM)
            oh = jnp.where(valid, oh, 0.0)
            ssq = ssq + jnp.sum(oh * oh, axis=-1, keepdims=True)
            heads.append(oh)
        rinv = lax.rsqrt(ssq / GROUP_W + 1e-6)
        o_ref[b] = (jnp.concatenate(heads, axis=1) * rinv * ng_ref[...]).astype(o_ref.dtype)


def _winattn(proj, small, oc, os_, kw, vw, band, ngw, B, S):
    tq = TQ
    R = NSA_HEADS * tq
    once = pl.Buffered(1)
    per_b = lambda t: t.reshape(B, t.shape[0] // B, t.shape[1])
    return pl.pallas_call(
        _winattn_kernel,
        grid=(S // tq,),
        in_specs=[pl.BlockSpec((B, tq, GROUP_W), lambda i: (0, i, COL["nq"] // 4)),
                  pl.BlockSpec((B, tq, LANES), lambda i: (0, i, 0)),
                  pl.BlockSpec((B, R, LANES), lambda i: (0, i, 0)),
                  pl.BlockSpec((B, R, LANES), lambda i: (0, i, 0)),
                  pl.BlockSpec((B, S, LANES), lambda i: (0, 0, 0), pipeline_mode=once),
                  pl.BlockSpec((B, S, LANES), lambda i: (0, 0, 0), pipeline_mode=once),
                  pl.BlockSpec(band.shape, lambda i: (0, 0, 0, 0), pipeline_mode=once),
                  pl.BlockSpec(ngw.shape, lambda i: (0, 0))],
        out_specs=pl.BlockSpec((B, tq, NSA_HEADS * LANES), lambda i: (0, i, 0)),
        out_shape=jax.ShapeDtypeStruct((B, S, NSA_HEADS * LANES), BF16),
        compiler_params=_params(("arbitrary",)),
        name="nsa_win_attn_merge",
    )(per_b(proj), per_b(small), per_b(oc), per_b(os_), kw, vw, band, ngw).reshape(B * S, NSA_HEADS * LANES)


def _widen_heads(x, axis):
    x = jnp.moveaxis(x, axis, -1)
    lead = x.shape[:-1]
    x = x.reshape(*lead, NSA_KV, NSA_GROUP, 1, NSA_HEAD_DIM)
    sel = jnp.eye(NSA_KV, dtype=x.dtype).reshape(NSA_KV, 1, NSA_KV, 1)
    x = (x * sel).reshape(*lead, NSA_HEADS * LANES)
    return jnp.moveaxis(x, -1, axis)


def _build_w_in(w):
    (hq, hf, hi, hg, nq, nkc, nvc, nks, nvs, nkw, nvw, ngate,
     sz, sxbc, sdt, rq, rk, rv, rg) = jnp.split(w, IN_SPLITS, axis=1)
    D = w.shape[0]
    nq = nq * NSA_HEAD_DIM ** -0.5
    deint = lambda t: t.reshape(D, RET_HEADS, RET_DK // 2, 2).transpose(0, 1, 3, 2).reshape(D, GROUP_W)
    small = jnp.concatenate([ngate, sdt, jnp.zeros((D, LANES - 32), w.dtype)], axis=1)
    wide = [hq, hf, hi, hg, sxbc, nq, sz, deint(rq), deint(rk), rv, rg]
    narrow = [nkc, nvc, nks, nvs, nkw, nvw, small]
    return jnp.concatenate(wide, axis=1).astype(BF16), jnp.concatenate(narrow, axis=1).astype(BF16)


def _build_cmp_weights(pe, w1, w2):
    w1r = w1.reshape(2, NSA_CMP_STRIDE, NSA_HEAD_DIM, NSA_CMP_HIDDEN)
    eye = jnp.eye(NSA_KV, dtype=w1.dtype)
    big = jnp.einsum("ardc,kj->arkdjc", w1r, eye).reshape(
        2, NSA_CMP_STRIDE * NSA_KV * NSA_HEAD_DIM, NSA_KV * NSA_CMP_HIDDEN)
    w2bd = jnp.einsum("cd,kj->kcjd", w2, eye).reshape(NSA_KV * NSA_CMP_HIDDEN, NSA_KV * NSA_HEAD_DIM)
    per = pe.reshape(2, NSA_CMP_STRIDE, 1, NSA_HEAD_DIM)
    pe2 = jnp.broadcast_to(per, (2, NSA_CMP_STRIDE, NSA_KV, NSA_HEAD_DIM)).reshape(2, -1)
    return pe2, big[0].astype(BF16), big[1].astype(BF16), w2bd.astype(BF16)


def _rotary_tables(S):
    half = RET_DK // 2
    theta = 1.0 / (10000.0 ** jnp.linspace(0.0, 1.0, half, dtype=F32))
    ang = jnp.arange(S, dtype=F32)[:, None] * theta[None, :]
    cos, sin = jnp.cos(ang), jnp.sin(ang)
    return jnp.concatenate([cos, cos], axis=1), jnp.concatenate([-sin, sin], axis=1)


def _mixer(x2, B, S, l, p, lower_bounds, band, cband, farq, cos_t, sin_t):
    T = B * S
    w_wide, w_narrow = _build_w_in(p["w_in"][l])
    tn = NCOL * LANES // 4
    proj = _proj(x2, w_wide.reshape(-1, 4, tn).transpose(1, 0, 2), tm=min(1024, T))
    kc, vc, ks, vs, kw, vw, small = _kvproj(x2, w_narrow, tm=min(512, T))
    row = lambda v: v.reshape(1, -1).astype(F32)

    lb = lower_bounds[l].astype(F32)
    o_a = _hgrn(proj, B, S, row(jnp.log(lb)), row(jnp.log1p(-lb)), row(1.0 - lb),
                row(p["hgrn_norm_g"][l]))

    nb = S // NSA_SLC_BLOCK
    grp = lambda t: t.reshape(B, nb, 4 * NSA_CMP_STRIDE * LANES)
    kcmp = _compress(grp(kc), *_build_cmp_weights(p["nsa_pe_k"][l], p["nsa_w1_k"][l], p["nsa_w2_k"][l]))
    vcmp = _compress(grp(vc), *_build_cmp_weights(p["nsa_pe_v"][l], p["nsa_w1_v"][l], p["nsa_w2_v"][l]))
    o_cmp, selb = _cmpattn(proj, kcmp, vcmp, cband, B, S)
    seq = lambda t: t.reshape(B, S, t.shape[1])
    o_sel = _selattn(proj, selb, seq(ks), seq(vs), band, farq, B, S)
    ngw = _widen_heads(p["nsa_norm_g"][l].astype(F32), 0).reshape(1, -1)
    o_b = _winattn(proj, small, o_cmp, o_sel, seq(kw), seq(vw), band, ngw, B, S)

    lane_vec = lambda v: jnp.zeros((1, LANES), F32).at[0, DT_LANE0:DT_LANE0 + SSM_HEADS].set(v.astype(F32))
    o_c = _ssd(proj, small, B, S, p["ssm_conv_w"][l].astype(F32), row(p["ssm_conv_b"][l]),
               lane_vec(p["ssm_dt_bias"][l]), lane_vec(-jnp.exp(p["ssm_a_log"][l].astype(F32))),
               row(jnp.repeat(p["ssm_d"][l].astype(F32), SSM_HEAD_DIM)), row(p["ssm_norm_g"][l]))

    o_d = _retention(proj, B, S, cos_t, sin_t)

    w_out = p["w_out"][l]
    wa, wb, wc, wd = (w_out[i * GROUP_W:(i + 1) * GROUP_W] for i in range(4))
    return o_a, o_b, o_c, o_d, wa.astype(BF16), _widen_heads(wb, 0).astype(BF16), wc.astype(BF16), wd.astype(BF16)


def kernel(x, ln1_g, ln1_b, ffn1_w1, ffn1_w3, ffn1_w2, ln2_g, ln2_b, w_in, w_out, hgrn_lb_logits, hgrn_norm_g, nsa_pe_k, nsa_w1_k, nsa_w2_k, nsa_pe_v, nsa_w1_v, nsa_w2_v, nsa_norm_g, rel_bias, ssm_conv_w, ssm_conv_b, ssm_dt_bias, ssm_a_log, ssm_d, ssm_norm_g, ln3_g, ln3_b, ffn2_w1, ffn2_w3, ffn2_w2):
    B, S, D = x.shape
    T = B * S
    depth = w_in.shape[0]
    p = dict(w_in=w_in, w_out=w_out, hgrn_norm_g=hgrn_norm_g, nsa_pe_k=nsa_pe_k, nsa_w1_k=nsa_w1_k,
             nsa_w2_k=nsa_w2_k, nsa_pe_v=nsa_pe_v, nsa_w1_v=nsa_w1_v, nsa_w2_v=nsa_w2_v,
             nsa_norm_g=nsa_norm_g, ssm_conv_w=ssm_conv_w, ssm_conv_b=ssm_conv_b,
             ssm_dt_bias=ssm_dt_bias, ssm_a_log=ssm_a_log, ssm_d=ssm_d, ssm_norm_g=ssm_norm_g)
    cum = jnp.cumsum(jax.nn.softmax(hgrn_lb_logits.astype(F32), axis=0), axis=0)
    lower_bounds = cum - cum[:1]
    rel_t = jnp.zeros((NSA_HEADS, LANES), F32).at[:, :REL_BUCKETS].set(rel_bias.astype(F32).T)
    band = _band_table(rel_t, TQ, BAND_OFF)
    cband = _band_table(rel_t, TQ, CMP_BAND_OFF, NSA_CMP_STRIDE, NSA_CMP_BLOCK - 1)
    bpt = TK // NSA_SLC_BLOCK
    farq = jnp.zeros((NSA_HEADS, LANES), BF16).at[:, bpt:bpt + 3].set(
        jnp.stack(_split3(rel_bias.astype(F32)[REL_BUCKETS - 1]), axis=1))
    farq = jnp.repeat(farq, TQ, axis=0)
    cos_t, sin_t = _rotary_tables(S)
    row = lambda v: v.reshape(1, -1).astype(F32)
    tm = min(512, T)
    tf = 512 if ffn1_w1.shape[2] % 512 == 0 else ffn1_w1.shape[2]
    x2 = x.reshape(T, D).astype(F32)
    for l in range(depth):
        x2 = _ffn(x2, _to_bf16_col_blocks(ffn1_w1, l, tf), _to_bf16_col_blocks(ffn1_w3, l, tf),
                  _to_bf16(ffn1_w2, l), row(ln1_g[l]), row(ln1_b[l]), tm)
        o_a, o_b, o_c, o_d, wa, wb, wc, wd = _mixer(x2, B, S, l, p, lower_bounds, band, cband, farq, cos_t, sin_t)
        x2 = _outproj(x2, o_a, o_b, o_c, o_d, wa, wb, wc, wd, row(ln2_g[l]), row(ln2_b[l]), min(512, T))
        x2 = _ffn(x2, _to_bf16_col_blocks(ffn2_w1, l, tf), _to_bf16_col_blocks(ffn2_w3, l, tf),
                  _to_bf16(ffn2_w2, l), row(ln3_g[l]), row(ln3_b[l]), tm)
    return x2.reshape(B, S, D).astype(x.dtype)
```

```python
import functools
import math

import numpy as np
import jax
import jax.numpy as jnp
from jax import lax
from jax.experimental import pallas as pl
from jax.experimental.pallas import tpu as pltpu

F32 = jnp.float32
BF16 = jnp.bfloat16

DEPTH = 2
GROUP_W = 512
ALPHA = (2 * DEPTH) ** 0.25
HG_HEADS = 4
NSA_HEADS = 8
NSA_KV = 2
NSA_GROUP = 4
NSA_HEAD_DIM = 64
NSA_CMP_STRIDE = 16
NSA_CMP_BLOCK = 32
NSA_SLC_BLOCK = 64
NSA_TOP_N = 16
NSA_WINDOW = 512
NSA_CMP_HIDDEN = 256
SSM_HEADS = 8
SSM_HEAD_DIM = 64
SSM_GROUPS = 2
SSM_STATE = 128
SSM_CONV = 4
RET_HEADS = 4
RET_DK = 128
REL_BUCKETS = 32
REL_EXACT = 16
REL_MAX_DIST = 2048
IN_SIZES = ((GROUP_W,) * 4 + (GROUP_W,) + (128,) * 6 + (24,)
            + (GROUP_W, 1024, SSM_HEADS) + (GROUP_W,) * 4)
IN_SPLITS = tuple(int(v) for v in np.cumsum(IN_SIZES)[:-1])

LANES = 128
SUBLANES = 8
VMEM_LIMIT = 56 * 1024 * 1024
CAST_BLOCK_BYTES = 4 * 1024 * 1024

COL = dict(hq=0, hf=4, hi=8, hg=12, sxbc=16, nq=24, sz=28, rq=32, rk=36, rv=40, rg=44)
NCOL = 48
KV_COLS = ("nkc", "nvc", "nks", "nvs", "nkw", "nvw", "small")
GATE_LANE0 = 0
DT_LANE0 = 24

CHUNK = 128
RET_CHUNK = 256
TQ = 128
TK = 1024
BAND_OFF = 2
CMP_BAND_OFF = 1
CMP_STEP = 256
WIN_SPAN = NSA_WINDOW + 2 * TQ
NEG = -1e30


def _params(sem):
    return pltpu.CompilerParams(dimension_semantics=sem, vmem_limit_bytes=VMEM_LIMIT)


def _dot(a, b):
    return jnp.dot(a, b, preferred_element_type=F32)


def _dot_nt(a, b):
    return lax.dot_general(a, b, (((1,), (1,)), ((), ())), preferred_element_type=F32)


def _split3(x):
    hi = x.astype(BF16)
    r1 = x - hi.astype(F32)
    mid = r1.astype(BF16)
    return hi, mid, (r1 - mid.astype(F32)).astype(BF16)


def _exact_left_dot(w, x):
    n = x.shape[1]
    y = _dot(w, jnp.concatenate(_split3(x), axis=1))
    return y[:, 0:n] + y[:, n:2 * n] + y[:, 2 * n:3 * n]


def _exact_right_dot(x, w):
    n = x.shape[0]
    y = _dot(jnp.concatenate(_split3(x), axis=0), w)
    return y[0:n] + y[n:2 * n] + y[2 * n:3 * n]


def _sigmoid(x):
    return 1.0 / (1.0 + jnp.exp(-x))


def _silu(x):
    return x * _sigmoid(x)


def _softplus(x):
    return jnp.maximum(x, 0.0) + jnp.log1p(jnp.exp(-jnp.abs(x)))


def _layer_norm(r, g, b):
    mu = jnp.mean(r, axis=-1, keepdims=True)
    d = r - mu
    var = jnp.mean(d * d, axis=-1, keepdims=True)
    return d * lax.rsqrt(var + 1e-5) * g + b


def _cast_kernel(x_ref, o_ref):
    o_ref[...] = x_ref[...].astype(o_ref.dtype)


def _to_bf16(w, l):
    _, r, c = w.shape
    tr = min(r, max(16, CAST_BLOCK_BYTES // (4 * c) // 16 * 16))
    while r % tr:
        tr -= 16
    return pl.pallas_call(
        _cast_kernel,
        grid=(r // tr,),
        in_specs=[pl.BlockSpec((None, tr, c), lambda i: (l, i, 0))],
        out_specs=pl.BlockSpec((tr, c), lambda i: (i, 0)),
        out_shape=jax.ShapeDtypeStruct((r, c), BF16),
        compiler_params=_params(("parallel",)),
        name="cast_bf16",
    )(w)


def _to_bf16_col_blocks(w, l, tc):
    _, r, c = w.shape
    return pl.pallas_call(
        _cast_kernel,
        grid=(c // tc,),
        in_specs=[pl.BlockSpec((None, r, tc), lambda j: (l, 0, j))],
        out_specs=pl.BlockSpec((None, r, tc), lambda j: (j, 0, 0)),
        out_shape=jax.ShapeDtypeStruct((c // tc, r, tc), BF16),
        compiler_params=_params(("parallel",)),
        name="cast_bf16_blocked",
    )(w)


def _ffn_kernel(x_ref, w1_ref, w3_ref, w2_ref, g_ref, b_ref, o_ref, acc_ref, xb_ref):
    j = pl.program_id(1)

    @pl.when(j == 0)
    def _():
        xb_ref[...] = x_ref[...].astype(BF16)
        acc_ref[...] = jnp.zeros_like(acc_ref)

    xb = xb_ref[...]
    h1 = _dot(xb, w1_ref[...])
    h3 = _dot(xb, w3_ref[...])
    a = (_silu(h1) * h3).astype(BF16)
    acc_ref[...] += _dot(a, w2_ref[...])

    @pl.when(j == pl.num_programs(1) - 1)
    def _():
        r = ALPHA * x_ref[...] + 0.5 * acc_ref[...]
        o_ref[...] = _layer_norm(r, g_ref[...], b_ref[...])


def _ffn(x, w1, w3, w2, g, b, tm):
    T, D = x.shape
    nf, _, tf = w1.shape
    return pl.pallas_call(
        _ffn_kernel,
        grid=(T // tm, nf),
        in_specs=[
            pl.BlockSpec((tm, D), lambda i, j: (i, 0)),
            pl.BlockSpec((None, D, tf), lambda i, j: (j, 0, 0)),
            pl.BlockSpec((None, D, tf), lambda i, j: (j, 0, 0)),
            pl.BlockSpec((tf, D), lambda i, j: (j, 0)),
            pl.BlockSpec((1, D), lambda i, j: (0, 0)),
            pl.BlockSpec((1, D), lambda i, j: (0, 0)),
        ],
        out_specs=pl.BlockSpec((tm, D), lambda i, j: (i, 0)),
        out_shape=jax.ShapeDtypeStruct((T, D), F32),
        scratch_shapes=[pltpu.VMEM((tm, D), F32), pltpu.VMEM((tm, D), BF16)],
        compiler_params=_params(("parallel", "arbitrary")),
        name="ffn_ln",
    )(x, w1, w3, w2, g, b)


def _proj_kernel(x_ref, w_ref, o_ref, xb_ref):
    @pl.when(pl.program_id(1) == 0)
    def _():
        xb_ref[...] = x_ref[...].astype(BF16)

    o_ref[...] = _dot(xb_ref[...], w_ref[...])


def _proj(x, w, tm):
    T, D = x.shape
    nn, _, tn = w.shape
    return pl.pallas_call(
        _proj_kernel,
        grid=(T // tm, nn),
        in_specs=[pl.BlockSpec((tm, D), lambda i, j: (i, 0)),
                  pl.BlockSpec((None, D, tn), lambda i, j: (j, 0, 0))],
        out_specs=pl.BlockSpec((tm, tn), lambda i, j: (i, j)),
        out_shape=jax.ShapeDtypeStruct((T, nn * tn), F32),
        scratch_shapes=[pltpu.VMEM((tm, D), BF16)],
        compiler_params=_params(("parallel", "arbitrary")),
        name="in_proj",
    )(x, w)


def _kvproj_kernel(x_ref, w_ref, kc_ref, vc_ref, ks_ref, vs_ref, kw_ref, vw_ref, sm_ref):
    tm = x_ref.shape[0]
    y = _dot(x_ref[...].astype(BF16), w_ref[...])
    piece = lambda n: y[:, n * LANES:(n + 1) * LANES]
    kc_ref[...] = piece(0)
    vc_ref[...] = piece(1)
    row = pl.program_id(0) * tm + lax.broadcasted_iota(jnp.int32, (tm, LANES), 0)
    lane = lax.broadcasted_iota(jnp.int32, (tm, LANES), 1)
    bpt = TK // NSA_SLC_BLOCK
    onehot = (((row // NSA_SLC_BLOCK) % bpt == lane) | ((lane >= bpt) & (lane < bpt + 3))).astype(BF16)
    ks_ref[...] = jnp.concatenate([piece(2).astype(BF16), onehot], axis=1)
    vs_ref[...] = jnp.concatenate([piece(3).astype(BF16), jnp.ones((tm, LANES), BF16)], axis=1)
    kw_ref[...] = piece(4).astype(BF16)
    vw_ref[...] = piece(5).astype(BF16)
    sm_ref[...] = piece(6)


def _kvproj(x, w, tm):
    T, D = x.shape
    narrow = lambda dt, width=LANES: (pl.BlockSpec((tm, width), lambda i: (i, 0)),
                                      jax.ShapeDtypeStruct((T, width), dt))
    outs = [narrow(F32), narrow(F32), narrow(BF16, 2 * LANES), narrow(BF16, 2 * LANES),
            narrow(BF16), narrow(BF16), narrow(F32)]
    return pl.pallas_call(
        _kvproj_kernel,
        grid=(T // tm,),
        in_specs=[pl.BlockSpec((tm, D), lambda i: (i, 0)), pl.BlockSpec(w.shape, lambda i: (0, 0))],
        out_specs=[o[0] for o in outs],
        out_shape=[o[1] for o in outs],
        compiler_params=_params(("parallel",)),
        name="kv_proj",
    )(x, w)


def _outproj_kernel(x_ref, oa_ref, ob_ref, oc_ref, od_ref, wa_ref, wb_ref, wc_ref, wd_ref,
                    g_ref, b_ref, o_ref):
    mix = (_dot(oa_ref[...], wa_ref[...]) + _dot(ob_ref[...], wb_ref[...])
           + _dot(oc_ref[...], wc_ref[...]) + _dot(od_ref[...], wd_ref[...]))
    o_ref[...] = _layer_norm(ALPHA * x_ref[...] + mix, g_ref[...], b_ref[...])


def _outproj(x, oa, ob, oc, od, wa, wb, wc, wd, g, b, tm):
    T, D = x.shape
    row = lambda a: pl.BlockSpec((tm, a.shape[1]), lambda i: (i, 0))
    full = lambda a: pl.BlockSpec(a.shape, lambda i: (0, 0))
    return pl.pallas_call(
        _outproj_kernel,
        grid=(T // tm,),
        in_specs=[row(x), row(oa), row(ob), row(oc), row(od),
                  full(wa), full(wb), full(wc), full(wd), full(g), full(b)],
        out_specs=row(x),
        out_shape=jax.ShapeDtypeStruct((T, D), F32),
        compiler_params=_params(("parallel",)),
        name="out_proj_ln",
    )(x, oa, ob, oc, od, wa, wb, wc, wd, g, b)


def _hgrn_tables(C):
    i = np.arange(C)[:, None]
    ip = np.arange(C)[None, :]
    seg = [(ip <= i),
           (ip > i)]
    masks = [np.eye(C, dtype=bool)]
    s = C // 2
    while s >= 1:
        blk = i // s
        if s < SUBLANES:
            seg.append(np.where(blk % 2 == 1, (ip > blk * s) & (ip <= i), (ip > i) & (ip <= (blk + 1) * s)))
        masks.append((blk % 2 == 1) & (ip // s == blk - 1))
        s //= 2
    seg = np.concatenate([x.astype(np.float32) for x in seg], axis=0)
    return seg, np.stack([m.astype(np.float32) for m in masks])


def _hgrn_kernel(q_ref, f_ref, i_ref, g_ref, llb_ref, l1m_ref, oml_ref, ng_ref,
                 seg_ref, msk_ref, o_ref, st_ref):
    @pl.when(pl.program_id(0) == 0)
    def _():
        st_ref[...] = jnp.zeros_like(st_ref)

    C = q_ref.shape[1]
    nlev = msk_ref.shape[0] - 1
    for bi in range(q_ref.shape[0]):
        q = _silu(q_ref[bi])
        z = f_ref[bi]
        log_sig = jnp.minimum(z, 0.0) - jnp.log1p(jnp.exp(-jnp.abs(z)))
        cc = l1m_ref[...] + log_sig
        llb = llb_ref[...]
        logf = jnp.maximum(llb, cc) + jnp.log1p(jnp.exp(-jnp.abs(llb - cc)))
        k = oml_ref[...] * _sigmoid(-z)
        v = i_ref[bi]
        seg = _exact_left_dot(seg_ref[...], logf)
        b_all = seg[0:C]
        lev = []
        s = C // 2
        while s >= SUBLANES:
            b3 = b_all.reshape(C // s, s, GROUP_W)
            start = b3[:, 0:1, :]
            nxt = jnp.concatenate([start[1:], start[-1:]], axis=0)
            odd = lax.broadcasted_iota(jnp.int32, b3.shape, 0) % 2 == 1
            lev.append(jnp.where(odd, b3 - start, nxt - b3).reshape(C, GROUP_W))
            s //= 2
        lev += [seg[r * C:(r + 1) * C] for r in range(2, seg.shape[0] // C)]
        outs = []
        for h in range(HG_HEADS):
            sl = slice(h * LANES, (h + 1) * LANES)
            qh, kh, vh = q[:, sl], k[:, sl], v[:, sl]
            a = msk_ref[0] * _dot_nt(qh.astype(BF16), kh.astype(BF16))
            for l in range(nlev):
                dec = jnp.exp(lev[l][:, sl])
                a = a + msk_ref[1 + l] * _dot_nt((qh * dec).astype(BF16), (kh * dec).astype(BF16))
            b = seg[0:C, sl]
            st = st_ref[bi, h]
            o = _dot(a.astype(BF16), vh.astype(BF16))
            o = o + _dot_nt((qh * jnp.exp(b)).astype(BF16), st.astype(BF16))
            kd = (kh * jnp.exp(seg[C:2 * C, sl])).astype(BF16)
            st_ref[bi, h] = st * jnp.exp(b[C - 1:C, :]) + _dot(vh.T.astype(BF16), kd)
            outs.append(o * lax.rsqrt(jnp.mean(o * o, axis=-1, keepdims=True) + 1e-6))
        o = jnp.concatenate(outs, axis=1)
        o_ref[bi] = (o * ng_ref[...] * _silu(g_ref[bi])).astype(o_ref.dtype)


def _hgrn(proj, B, S, llb, l1m, oml, ng):
    C = CHUNK
    nc = S // C
    seg, msk = _hgrn_tables(C)
    seg, msk = jnp.asarray(seg, BF16), jnp.asarray(msk)
    proj3 = proj.reshape(B, S, proj.shape[1])
    col = lambda name: pl.BlockSpec((B, C, GROUP_W), lambda c, n=COL[name] // 4: (0, c, n))
    vec = pl.BlockSpec((1, GROUP_W), lambda c: (0, 0))
    full2 = lambda a: pl.BlockSpec(a.shape, lambda c: (0, 0))
    return pl.pallas_call(
        _hgrn_kernel,
        grid=(nc,),
        in_specs=[col("hq"), col("hf"), col("hi"), col("hg"), vec, vec, vec, vec,
                  full2(seg), pl.BlockSpec(msk.shape, lambda c: (0, 0, 0))],
        out_specs=pl.BlockSpec((B, C, GROUP_W), lambda c: (0, c, 0)),
        out_shape=jax.ShapeDtypeStruct((B, S, GROUP_W), BF16),
        scratch_shapes=[pltpu.VMEM((B, HG_HEADS, LANES, LANES), F32)],
        compiler_params=_params(("arbitrary",)),
        name="hgrn2",
    )(proj3, proj3, proj3, proj3, llb, l1m, oml, ng, seg, msk).reshape(B * S, GROUP_W)


def _ssd_chunk(z_ref, xbc_ref, sm_ref, cw_ref, cb_ref, dtb_ref, aneg_ref, dsk_ref, ng_ref,
               ex_ref, o_ref, tail_ref, st_ref):
    L = xbc_ref.shape[0]
    x = xbc_ref[...]
    xe = jnp.concatenate([tail_ref[...], x], axis=0)
    cw = cw_ref[...]
    conv = cb_ref[...]
    for kk in range(SSM_CONV):
        conv = conv + cw[kk:kk + 1, :] * xe[5 + kk:5 + kk + L, :]
    tail_ref[...] = x[L - 8:L, :]
    conv = _silu(conv)
    xs = conv[:, 0:GROUP_W]
    bm = conv[:, GROUP_W:GROUP_W + 256]
    cm = conv[:, GROUP_W + 256:GROUP_W + 512]

    dtf = _softplus(sm_ref[...] + dtb_ref[...])
    la = dtf * aneg_ref[...]
    ri = lax.broadcasted_iota(jnp.int32, (L, L), 0)
    ci = lax.broadcasted_iota(jnp.int32, (L, L), 1)
    tri = ri >= ci
    bfull = _exact_left_dot(tri.astype(BF16), la)
    ex = ex_ref[...]
    bexp = _exact_right_dot(bfull, ex)
    dtexp = _exact_right_dot(dtf, ex)
    b_t = bfull.T
    xdt = xs * dtexp
    lane = lax.broadcasted_iota(jnp.int32, (L, LANES), 1)

    scores = []
    for g in range(SSM_GROUPS):
        cg = cm[:, g * SSM_STATE:(g + 1) * SSM_STATE].astype(BF16)
        bg = bm[:, g * SSM_STATE:(g + 1) * SSM_STATE].astype(BF16)
        cb = _dot_nt(cg, bg)
        for hh in range(SSM_HEADS // SSM_GROUPS):
            h = g * (SSM_HEADS // SSM_GROUPS) + hh
            bcol = bfull[:, DT_LANE0 + h:DT_LANE0 + h + 1]
            brow = b_t[DT_LANE0 + h:DT_LANE0 + h + 1, :]
            dec = jnp.exp(jnp.where(tri, bcol - brow, NEG))
            scores.append((cb * dec).astype(BF16))
    y_pairs = []
    for u in range(SSM_HEADS // 2):
        slab = xdt[:, u * LANES:(u + 1) * LANES]
        lo = jnp.where(lane < SSM_HEAD_DIM, slab, 0.0).astype(BF16)
        hi = jnp.where(lane >= SSM_HEAD_DIM, slab, 0.0).astype(BF16)
        y_pairs.append(_dot(scores[2 * u], lo) + _dot(scores[2 * u + 1], hi))
    y_intra = jnp.concatenate(y_pairs, axis=1)

    blast = bexp[L - 1:L, :]
    w = (xdt * jnp.exp(blast - bexp)).astype(BF16)
    y_inter = []
    for g in range(SSM_GROUPS):
        gs = slice(g * 256, (g + 1) * 256)
        cg = cm[:, g * SSM_STATE:(g + 1) * SSM_STATE].astype(BF16)
        st = st_ref[g]
        y_inter.append(_dot(cg, st.astype(BF16)))
        bg_t = bm[:, g * SSM_STATE:(g + 1) * SSM_STATE].T.astype(BF16)
        st_ref[g] = st * jnp.exp(blast[:, gs]) + _dot(bg_t, w[:, gs])
    y = y_intra + jnp.concatenate(y_inter, axis=1) * jnp.exp(bexp) + dsk_ref[...] * xs
    y = y * _silu(z_ref[...])
    halves = []
    for g in range(SSM_GROUPS):
        seg = y[:, g * 256:(g + 1) * 256]
        halves.append(seg * lax.rsqrt(jnp.mean(seg * seg, axis=-1, keepdims=True) + 1e-6))
    o_ref[...] = (jnp.concatenate(halves, axis=1) * ng_ref[...]).astype(o_ref.dtype)


def _ssd_ret_kernel(z_ref, xbc_ref, sm_ref, cw_ref, cb_ref, dtb_ref, aneg_ref, dsk_ref, ng_ref, ex_ref,
                    q_ref, k_ref, v_ref, g_ref, cos_ref, sin_ref, dec_ref, qs_ref, ks_ref, sd_ref,
                    o_ssd_ref, o_ret_ref, tail_ref, st_ssd_ref, st_ret_ref):
    @pl.when(pl.program_id(0) == 0)
    def _():
        tail_ref[...] = jnp.zeros_like(tail_ref)
        st_ssd_ref[...] = jnp.zeros_like(st_ssd_ref)
        st_ret_ref[...] = jnp.zeros_like(st_ret_ref)

    for b in range(xbc_ref.shape[0]):
        for r in range(0, xbc_ref.shape[1], CHUNK):
            rows = pl.ds(r, CHUNK)
            _ssd_chunk(z_ref.at[b, rows], xbc_ref.at[b, rows], sm_ref.at[b, rows], cw_ref, cb_ref, dtb_ref,
                       aneg_ref, dsk_ref, ng_ref, ex_ref, o_ssd_ref.at[b, rows], tail_ref.at[b],
                       st_ssd_ref.at[b])
    _ret_step(q_ref, k_ref, v_ref, g_ref, cos_ref, sin_ref, dec_ref, qs_ref, ks_ref, sd_ref,
              o_ret_ref, st_ret_ref)


def _ret_step(q_ref, k_ref, v_ref, g_ref, cos_ref, sin_ref, dec_ref, qs_ref, ks_ref, sd_ref,
              o_ref, st_ref):
    cos = cos_ref[...]
    sin = sin_ref[...]
    for b in range(q_ref.shape[0]):
        outs = []
        for h in range(RET_HEADS):
            sl = slice(h * LANES, (h + 1) * LANES)
            qh = q_ref[b, :, sl]
            kh = k_ref[b, :, sl]
            qh = qh * cos + pltpu.roll(qh, RET_DK // 2, axis=1) * sin
            kh = (kh * cos + pltpu.roll(kh, RET_DK // 2, axis=1) * sin) * (RET_DK ** -0.5)
            vh = v_ref[b, :, sl].astype(BF16)
            sc = (_dot_nt(qh.astype(BF16), kh.astype(BF16)) * dec_ref[h]).astype(BF16)
            st = st_ref[b, h]
            y = _dot(sc, vh) + _dot((qh * qs_ref[:, sl]).astype(BF16), st.astype(BF16))
            kd_t = (kh * ks_ref[:, sl]).T.astype(BF16)
            st_ref[b, h] = st * sd_ref[h] + _dot(kd_t, vh)
            mu = jnp.mean(y, axis=-1, keepdims=True)
            d = y - mu
            outs.append(d * lax.rsqrt(jnp.mean(d * d, axis=-1, keepdims=True) + 1e-5))
        o_ref[b] = (_silu(g_ref[b]) * jnp.concatenate(outs, axis=1)).astype(o_ref.dtype)


def _ssd_retention(proj, small, B, S, cw, cb, dtb, aneg, dsk, ng, cos_t, sin_t):
    L = min(RET_CHUNK, S)
    nc = S // L
    ex = np.zeros((LANES, GROUP_W), np.float32)
    for h in range(SSM_HEADS):
        ex[DT_LANE0 + h, h * SSM_HEAD_DIM:(h + 1) * SSM_HEAD_DIM] = 1.0
    ex = jnp.asarray(ex, BF16)
    lg = jnp.log(1.0 - 2.0 ** (-5.0 - jnp.arange(RET_HEADS, dtype=F32)))
    i = jnp.arange(L, dtype=F32)
    diff = i[:, None] - i[None, :]
    dec = jnp.where(diff >= 0, jnp.exp(lg[:, None, None] * jnp.maximum(diff, 0.0)), 0.0)
    rep = lambda t: jnp.repeat(t, LANES, axis=1)
    qs = rep(jnp.exp((i[:, None] + 1.0) * lg[None, :]))
    ks = rep(jnp.exp((L - 1.0 - i[:, None]) * lg[None, :]))
    sd = jnp.broadcast_to(jnp.exp(L * lg)[:, None, None], (RET_HEADS, LANES, LANES))
    proj3 = proj.reshape(B, S, proj.shape[1])
    full2 = lambda a: pl.BlockSpec(a.shape, lambda c: (0, 0))
    col = lambda name: pl.BlockSpec((B, L, GROUP_W), lambda c, n=COL[name] // 4: (0, c, n))
    out = pl.BlockSpec((B, L, GROUP_W), lambda c: (0, c, 0))
    o_c, o_d = pl.pallas_call(
        _ssd_ret_kernel,
        grid=(nc,),
        in_specs=[col("sz"),
                  pl.BlockSpec((B, L, 1024), lambda c: (0, c, COL["sxbc"] // 8)),
                  pl.BlockSpec((B, L, LANES), lambda c: (0, c, 0)),
                  full2(cw), full2(cb), full2(dtb), full2(aneg), full2(dsk), full2(ng), full2(ex),
                  col("rq"), col("rk"), col("rv"), col("rg"),
                  pl.BlockSpec((L, LANES), lambda c: (c, 0)),
                  pl.BlockSpec((L, LANES), lambda c: (c, 0)),
                  pl.BlockSpec((RET_HEADS, L, L), lambda c: (0, 0, 0)),
                  pl.BlockSpec((L, GROUP_W), lambda c: (0, 0)),
                  pl.BlockSpec((L, GROUP_W), lambda c: (0, 0)),
                  pl.BlockSpec((RET_HEADS, LANES, LANES), lambda c: (0, 0, 0))],
        out_specs=[out, out],
        out_shape=[jax.ShapeDtypeStruct((B, S, GROUP_W), BF16)] * 2,
        scratch_shapes=[pltpu.VMEM((B, 8, 1024), F32), pltpu.VMEM((B, SSM_GROUPS, SSM_STATE, 256), F32),
                        pltpu.VMEM((B, RET_HEADS, RET_DK, RET_DK), F32)],
        compiler_params=_params(("arbitrary",)),
        name="ssd_retention",
    )(proj3, proj3, small.reshape(B, S, LANES), cw, cb, dtb, aneg, dsk, ng, ex,
      proj3, proj3, proj3, proj3, cos_t, sin_t, dec, qs, ks, sd)
    return o_c.reshape(B * S, GROUP_W), o_d.reshape(B * S, GROUP_W)


def _t5_bucket(dist):
    n = jnp.maximum(dist, 0)
    nf = jnp.maximum(n, 1).astype(F32)
    large = REL_EXACT + (jnp.log(nf / REL_EXACT) / math.log(REL_MAX_DIST / REL_EXACT)
                         * (REL_BUCKETS - REL_EXACT)).astype(jnp.int32)
    return jnp.where(n < REL_EXACT, n, jnp.minimum(large, REL_BUCKETS - 1))


def _head_bias(bucket, rel_ref):
    rows, cols = bucket.shape
    per_head = []
    for h in range(NSA_HEADS):
        tbl = jnp.broadcast_to(rel_ref[h:h + 1, :], (rows, LANES))
        chunks = [jnp.take_along_axis(tbl, bucket[:, c:c + LANES], axis=1)
                  for c in range(0, cols, LANES)]
        per_head.append(chunks[0] if len(chunks) == 1 else jnp.concatenate(chunks, axis=1))
    return jnp.stack(per_head, axis=0)


def _stack_heads(q):
    lane = lax.broadcasted_iota(jnp.int32, (q.shape[0], LANES), 1)
    rows = []
    for h in range(NSA_HEADS):
        slab = q[:, (h // 2) * LANES:(h // 2 + 1) * LANES]
        src_half, dst_half = h % 2, h // NSA_GROUP
        if src_half != dst_half:
            slab = pltpu.roll(slab, NSA_HEAD_DIM, axis=1)
        rows.append(jnp.where(lane // NSA_HEAD_DIM == dst_half, slab, 0.0))
    return jnp.concatenate(rows, axis=0).astype(BF16)


def _cmp_kernel(g_ref, pe_ref, w1a_ref, w1b_ref, w2_ref, o_ref):
    nb = g_ref.shape[0]
    gw = g_ref.shape[1] // 4
    pe = pe_ref[...]
    slabs = [g_ref[:, s * gw:(s + 1) * gw] for s in range(4)]
    nxt0 = pltpu.roll(slabs[0], nb - 1, axis=0)
    for s in range(4):
        a = (slabs[s] + pe[0:1, :]).astype(BF16)
        bn = ((slabs[s + 1] if s < 3 else nxt0) + pe[1:2, :]).astype(BF16)
        hid = _silu(_dot(a, w1a_ref[...]) + _dot(bn, w1b_ref[...]))
        o_ref[:, s * LANES:(s + 1) * LANES] = _dot(hid.astype(BF16), w2_ref[...]).astype(o_ref.dtype)


def _compress(g, pe2, w1a, w1b, w2bd):
    B, nb, gw4 = g.shape
    full2 = lambda a: pl.BlockSpec(a.shape, lambda b: (0, 0))
    return pl.pallas_call(
        _cmp_kernel,
        grid=(B,),
        in_specs=[pl.BlockSpec((None, nb, gw4), lambda b: (b, 0, 0)),
                  full2(pe2), full2(w1a), full2(w1b), full2(w2bd)],
        out_specs=pl.BlockSpec((None, nb, 4 * LANES), lambda b: (b, 0, 0)),
        out_shape=jax.ShapeDtypeStruct((B, nb, 4 * LANES), BF16),
        compiler_params=_params(("parallel",)),
        name="nsa_compress",
    )(g, pe2, w1a, w1b, w2bd).reshape(B, 4 * nb, LANES)


def _cmpattn_kernel(q_ref, kc_ref, vc_ref, band_ref, st_ref, oc_ref, selb_ref):
    tq = q_ref.shape[1]
    ncmp = kc_ref.shape[1]
    nb = ncmp // 4
    R = NSA_HEADS * tq
    q0 = pl.program_id(0) * tq
    nd = band_ref.shape[0]

    def attend(ncv, nbv):
        for b in range(q_ref.shape[0]):
            attend_one(b, ncv, nbv)

    def attend_one(b, ncv, nbv):
        Q = _stack_heads(q_ref[b])
        s3 = _dot_nt(Q, kc_ref[b, 0:ncv, :]).reshape(NSA_HEADS, tq, ncv)
        parts = []
        for ch in range(ncv // LANES):
            d = jnp.clip((q0 - ch * LANES * NSA_CMP_STRIDE) // tq + CMP_BAND_OFF, 0, nd - 1)
            parts.append(s3[:, :, ch * LANES:(ch + 1) * LANES] + band_ref[d])
        s3 = jnp.concatenate(parts, axis=2)
        mx = jnp.max(s3, axis=-1, keepdims=True)
        e = jnp.exp(s3 - mx)
        live = (q0 + lax.broadcasted_iota(jnp.int32, (tq, 1), 0) >= NSA_CMP_BLOCK - 1)[None]
        p = e * jnp.where(live, 1.0 / jnp.sum(e, axis=-1, keepdims=True), 0.0)
        oc_ref[b] = _dot(p.reshape(R, ncv).astype(BF16), vc_ref[b, 0:ncv, :])

        ps = p.reshape(NSA_KV, NSA_GROUP, tq, ncv).sum(axis=1).reshape(NSA_KV * tq, ncv)
        imp = _exact_right_dot(ps, st_ref[0:ncv, 0:nbv])
        j = lax.broadcasted_iota(jnp.int32, (NSA_KV * tq, nbv), 1)
        t = q0 + (lax.broadcasted_iota(jnp.int32, (NSA_KV * tq, nbv), 0) % tq)
        cur = t // NSA_SLC_BLOCK
        forced = (j == 0) | (j == cur) | (j == cur - 1)
        score = jnp.where(j > cur, -1.0, jnp.where(forced, NSA_GROUP + 1.0, imp))
        score = score.T
        jf = lax.broadcasted_iota(jnp.int32, score.shape, 0).astype(F32)
        sel = score == NSA_GROUP + 1.0
        score = jnp.where(sel, -jnp.inf, score)
        for _ in range(min(NSA_TOP_N, nb) - 3):
            best = jnp.max(score, axis=0, keepdims=True)
            first = jnp.min(jnp.where(score == best, jf, float(nbv)), axis=0, keepdims=True)
            hit = jf == first
            sel = sel | hit
            score = jnp.where(hit, -jnp.inf, score)
        selb = jnp.where(sel, 0.0, NEG).T.astype(selb_ref.dtype)
        for kv in range(NSA_KV):
            selb_ref[b, :, kv * nb:kv * nb + nbv] = selb[kv * tq:(kv + 1) * tq]
            if nbv < nb:
                selb_ref[b, :, kv * nb + nbv:(kv + 1) * nb] = jnp.full((tq, nb - nbv), NEG, selb_ref.dtype)

    if ncmp % CMP_STEP:
        attend(ncmp, nb)
    else:
        nvar = ncmp // CMP_STEP
        sizes = [(CMP_STEP * v, min(nb, LANES * ((v + 1) // 2))) for v in range(1, nvar + 1)]
        last_key = (q0 + tq - 1) // NSA_CMP_STRIDE
        lax.switch(jnp.minimum(last_key // CMP_STEP, nvar - 1),
                   [functools.partial(attend, ncv, nbv) for ncv, nbv in sizes])


def _cmpattn(proj, kcmp, vcmp, cband, B, S):
    tq = TQ
    nqt = S // tq
    ncmp = kcmp.shape[1]
    nb = ncmp // 4
    off = np.arange(ncmp)[:, None] - 4 * np.arange(nb)[None, :]
    stencil = np.where((off >= 0) & (off <= 2), 1.0, np.where((off == -1) | (off == 3), 0.5, 0.0))
    stencil = jnp.asarray(stencil, BF16)
    once = pl.Buffered(1)
    R = NSA_HEADS * tq
    o_cmp, selb = pl.pallas_call(
        _cmpattn_kernel,
        grid=(nqt,),
        in_specs=[pl.BlockSpec((B, tq, GROUP_W), lambda i: (0, i, COL["nq"] // 4)),
                  pl.BlockSpec((B, ncmp, LANES), lambda i: (0, 0, 0), pipeline_mode=once),
                  pl.BlockSpec((B, ncmp, LANES), lambda i: (0, 0, 0), pipeline_mode=once),
                  pl.BlockSpec(cband.shape, lambda i: (0, 0, 0, 0), pipeline_mode=once),
                  pl.BlockSpec(stencil.shape, lambda i: (0, 0), pipeline_mode=once)],
        out_specs=[pl.BlockSpec((B, R, LANES), lambda i: (0, i, 0)),
                   pl.BlockSpec((B, tq, 2 * nb), lambda i: (0, i, 0))],
        out_shape=[jax.ShapeDtypeStruct((B, S * NSA_HEADS, LANES), F32),
                   jax.ShapeDtypeStruct((B, S, 2 * nb), BF16)],
        compiler_params=_params(("arbitrary",)),
        name="nsa_cmp_attn_topk",
    )(proj.reshape(B, S, proj.shape[1]), kcmp, vcmp, cband, stencil)
    return o_cmp.reshape(B * S * NSA_HEADS, LANES), selb.reshape(B * S, 2 * nb)


def _band_kernel(rel_ref, o_ref, *, entry_off, key_step, key_end):
    tq = o_ref.shape[1]
    delta = (pl.program_id(0) - entry_off) * tq
    row = lax.broadcasted_iota(jnp.int32, (tq, LANES), 0)
    col = lax.broadcasted_iota(jnp.int32, (tq, LANES), 1)
    dist = delta + row - (col * key_step + key_end)
    bias = _head_bias(_t5_bucket(dist), rel_ref)
    o_ref[...] = jnp.where((dist >= 0)[None], bias, NEG)


def _band_table(rel_t, tq, entry_off, key_step=1, key_end=0):
    nd = -(-(REL_MAX_DIST + key_step * (LANES - 1) + key_end) // tq) + entry_off + 1
    return pl.pallas_call(
        functools.partial(_band_kernel, entry_off=entry_off, key_step=key_step, key_end=key_end),
        grid=(nd,),
        in_specs=[pl.BlockSpec(rel_t.shape, lambda d: (0, 0))],
        out_specs=pl.BlockSpec((None, NSA_HEADS, tq, LANES), lambda d: (d, 0, 0, 0)),
        out_shape=jax.ShapeDtypeStruct((nd, NSA_HEADS, tq, LANES), F32),
        compiler_params=_params(("parallel",)),
        name="nsa_bias_band",
    )(rel_t)


def _selattn_kernel(q_ref, selb_ref, ks_ref, vs_ref, band_ref, sp_ref, farq_ref, os_ref,
                    qaug_ref, msel_ref, s_ref, mt_ref, m_ref, acc_ref):
    tq = q_ref.shape[0]
    R = NSA_HEADS * tq
    nb = selb_ref.shape[1] // 2
    n_kt = msel_ref.shape[0]
    tk = ks_ref.shape[0] // n_kt
    nd = band_ref.shape[0]
    q0 = pl.program_id(1) * tq
    qaug_ref[:, 0:LANES] = _stack_heads(q_ref[...])
    selb2 = jnp.concatenate([selb_ref[:, 0:nb], selb_ref[:, nb:2 * nb]], axis=0)
    spread = _dot(selb2, sp_ref[...])
    for c in range(n_kt):
        msel_ref[c] = spread[:, c * LANES:(c + 1) * LANES].astype(BF16)
    m_ref[...] = jnp.full(m_ref.shape, NEG, F32)
    acc_ref[...] = jnp.zeros_like(acc_ref)
    n_tiles = (q0 + tq - 1) // tk + 1

    far_pairs = (jnp.maximum(q0 - (REL_MAX_DIST - 1), 0) // tk) // 2

    def scores(c, slot, far):
        c = jnp.minimum(c, n_kt - 1)
        k0 = pl.multiple_of(c * tk, tk)
        m2 = msel_ref[c]
        flags = jnp.concatenate([m2[0:tq]] * NSA_GROUP + [m2[tq:2 * tq]] * NSA_GROUP, axis=0)
        qaug_ref[:, LANES:2 * LANES] = flags + farq_ref[...] if far else flags
        s = _dot_nt(qaug_ref[...], ks_ref[pl.ds(k0, tk), :])
        if not far:
            s3 = s.reshape(NSA_HEADS, tq, tk)
            parts = []
            for ch in range(tk // LANES):
                d = jnp.clip((q0 - k0) // tq - ch * (LANES // tq) + BAND_OFF, 0, nd - 1)
                parts.append(s3[:, :, ch * LANES:(ch + 1) * LANES] + band_ref[d])
            s = jnp.concatenate(parts, axis=2).reshape(R, tk)
        s_ref[slot] = s
        mt_ref[slot] = jnp.broadcast_to(jnp.max(s, axis=-1, keepdims=True), (R, LANES))

    def accumulate(c, slot):
        m_old = m_ref[...]
        m_new = jnp.maximum(m_old, mt_ref[slot])
        alpha = jnp.exp(m_old - m_new)
        p = jnp.exp(s_ref[slot] - jnp.tile(m_new, (1, tk // LANES)))
        vt = vs_ref[pl.ds(pl.multiple_of(c * tk, tk), tk), :]
        acc_ref[...] = jnp.tile(alpha, (1, 2)) * acc_ref[...] + _dot(p.astype(BF16), vt)
        m_ref[...] = m_new

    def pair(i, far_odd, far_even):
        c = 2 * i
        scores(c + 1, 1, far_odd)
        accumulate(c, 0)
        scores(c + 2, 0, far_even)
        accumulate(c + 1, 1)

    scores(0, 0, False)
    last_far = jnp.maximum(far_pairs - 1, 0)
    lax.fori_loop(0, last_far, lambda i, carry: pair(i, True, True), None)
    lax.fori_loop(last_far, far_pairs, lambda i, carry: pair(i, True, False), None)
    lax.fori_loop(far_pairs, n_tiles // 2, lambda i, carry: pair(i, False, False), None)

    @pl.when(n_tiles % 2 == 1)
    def _():
        accumulate(n_tiles - 1, 0)

    acc = acc_ref[...]
    os_ref[...] = acc[:, 0:LANES] / acc[:, LANES:2 * LANES]


def _selattn(proj, selb, ks, vs, band, farq, B, S):
    tq = TQ
    tk = min(TK, S)
    n_kt = S // tk
    bpt = tk // NSA_SLC_BLOCK
    nqt = S // tq
    nb = S // NSA_SLC_BLOCK
    R = NSA_HEADS * tq
    sp = np.zeros((nb, n_kt * LANES), np.float32)
    sp[np.arange(nb), (np.arange(nb) // bpt) * LANES + np.arange(nb) % bpt] = 1.0
    once = pl.Buffered(1)
    return pl.pallas_call(
        _selattn_kernel,
        grid=(B, nqt),
        in_specs=[pl.BlockSpec((tq, GROUP_W), lambda b, i: (b * nqt + i, COL["nq"] // 4)),
                  pl.BlockSpec((tq, 2 * nb), lambda b, i: (b * nqt + i, 0)),
                  pl.BlockSpec((None, S, 2 * LANES), lambda b, i: (b, 0, 0), pipeline_mode=once),
                  pl.BlockSpec((None, S, 2 * LANES), lambda b, i: (b, 0, 0), pipeline_mode=once),
                  pl.BlockSpec(band.shape, lambda b, i: (0, 0, 0, 0), pipeline_mode=once),
                  pl.BlockSpec(sp.shape, lambda b, i: (0, 0)),
                  pl.BlockSpec(farq.shape, lambda b, i: (0, 0))],
        out_specs=pl.BlockSpec((R, LANES), lambda b, i: (b * nqt + i, 0)),
        out_shape=jax.ShapeDtypeStruct((B * S * NSA_HEADS, LANES), F32),
        scratch_shapes=[pltpu.VMEM((R, 2 * LANES), BF16), pltpu.VMEM((n_kt, 2 * tq, LANES), BF16),
                        pltpu.VMEM((2, R, tk), F32), pltpu.VMEM((2, R, LANES), F32),
                        pltpu.VMEM((R, LANES), F32), pltpu.VMEM((R, 2 * LANES), F32)],
        compiler_params=_params(("parallel", "arbitrary")),
        name="nsa_sel_attn",
    )(proj, selb, ks, vs, band, jnp.asarray(sp, BF16), farq)


def _winattn_kernel(q_ref, sm_ref, oc_ref, os_ref, kw_ref, vw_ref, band_ref, ng_ref, o_ref):
    tq = q_ref.shape[1]
    R = NSA_HEADS * tq
    S = kw_ref.shape[1]
    span = min(WIN_SPAN, S)
    nd = band_ref.shape[0]
    q0 = pl.program_id(0) * tq
    start = pl.multiple_of(jnp.clip(q0 + tq - span, 0, S - span), tq)
    row = lax.broadcasted_iota(jnp.int32, (tq, span), 0)
    col = lax.broadcasted_iota(jnp.int32, (tq, span), 1)
    in_window = ((q0 + row) - (start + col) < NSA_WINDOW)[None]
    lane = lax.broadcasted_iota(jnp.int32, (tq, LANES), 1)
    for b in range(q_ref.shape[0]):
        Q = _stack_heads(q_ref[b])
        kt = kw_ref[b, pl.ds(start, span), :]
        vt = vw_ref[b, pl.ds(start, span), :]
        s3 = _dot_nt(Q, kt).reshape(NSA_HEADS, tq, span)
        parts = []
        for ch in range(span // LANES):
            d = jnp.clip((q0 - start) // tq - ch * (LANES // tq) + BAND_OFF, 0, nd - 1)
            part = s3[:, :, ch * LANES:(ch + 1) * LANES] + band_ref[d]
            if (span - tq) + tq - 1 - ch * LANES >= NSA_WINDOW:
                part = jnp.where(in_window[:, :, ch * LANES:(ch + 1) * LANES], part, NEG)
            parts.append(part)
        s3 = jnp.concatenate(parts, axis=2)
        mx = jnp.max(s3, axis=-1, keepdims=True)
        e = jnp.exp(s3 - mx)
        inv = (1.0 / jnp.sum(e, axis=-1, keepdims=True)).reshape(R, 1)
        ow = _dot(e.reshape(R, span).astype(BF16), vt) * inv

        gates = _sigmoid(sm_ref[b])
        heads = []
        ssq = jnp.zeros((tq, 1), F32)
        for h in range(NSA_HEADS):
            rs = slice(h * tq, (h + 1) * tq)
            g = [gates[:, GATE_LANE0 + 3 * h + br:GATE_LANE0 + 3 * h + br + 1] for br in range(3)]
            oh = g[0] * oc_ref[b, rs, :] + g[1] * os_ref[b, rs, :] + g[2] * ow[rs, :]
            kv = h // NSA_GROUP
            valid = (lane >= kv * NSA_HEAD_DIM) & (lane < (kv + 1) * NSA_HEAD_DIM)
            oh = jnp.where(valid, oh, 0.0)
            ssq = ssq + jnp.sum(oh * oh, axis=-1, keepdims=True)
            heads.append(oh)
        rinv = lax.rsqrt(ssq / GROUP_W + 1e-6)
        o_ref[b] = (jnp.concatenate(heads, axis=1) * rinv * ng_ref[...]).astype(o_ref.dtype)


def _winattn(proj, small, oc, os_, kw, vw, band, ngw, B, S):
    tq = TQ
    R = NSA_HEADS * tq
    once = pl.Buffered(1)
    per_b = lambda t: t.reshape(B, t.shape[0] // B, t.shape[1])
    return pl.pallas_call(
        _winattn_kernel,
        grid=(S // tq,),
        in_specs=[pl.BlockSpec((B, tq, GROUP_W), lambda i: (0, i, COL["nq"] // 4)),
                  pl.BlockSpec((B, tq, LANES), lambda i: (0, i, 0)),
                  pl.BlockSpec((B, R, LANES), lambda i: (0, i, 0)),
                  pl.BlockSpec((B, R, LANES), lambda i: (0, i, 0)),
                  pl.BlockSpec((B, S, LANES), lambda i: (0, 0, 0), pipeline_mode=once),
                  pl.BlockSpec((B, S, LANES), lambda i: (0, 0, 0), pipeline_mode=once),
                  pl.BlockSpec(band.shape, lambda i: (0, 0, 0, 0), pipeline_mode=once),
                  pl.BlockSpec(ngw.shape, lambda i: (0, 0))],
        out_specs=pl.BlockSpec((B, tq, NSA_HEADS * LANES), lambda i: (0, i, 0)),
        out_shape=jax.ShapeDtypeStruct((B, S, NSA_HEADS * LANES), BF16),
        compiler_params=_params(("arbitrary",)),
        name="nsa_win_attn_merge",
    )(per_b(proj), per_b(small), per_b(oc), per_b(os_), kw, vw, band, ngw).reshape(B * S, NSA_HEADS * LANES)


def _widen_heads(x, axis):
    x = jnp.moveaxis(x, axis, -1)
    lead = x.shape[:-1]
    x = x.reshape(*lead, NSA_KV, NSA_GROUP, 1, NSA_HEAD_DIM)
    sel = jnp.eye(NSA_KV, dtype=x.dtype).reshape(NSA_KV, 1, NSA_KV, 1)
    x = (x * sel).reshape(*lead, NSA_HEADS * LANES)
    return jnp.moveaxis(x, -1, axis)


def _build_w_in(w):
    (hq, hf, hi, hg, nq, nkc, nvc, nks, nvs, nkw, nvw, ngate,
     sz, sxbc, sdt, rq, rk, rv, rg) = jnp.split(w, IN_SPLITS, axis=1)
    D = w.shape[0]
    nq = nq * NSA_HEAD_DIM ** -0.5
    deint = lambda t: t.reshape(D, RET_HEADS, RET_DK // 2, 2).transpose(0, 1, 3, 2).reshape(D, GROUP_W)
    small = jnp.concatenate([ngate, sdt, jnp.zeros((D, LANES - 32), w.dtype)], axis=1)
    wide = [hq, hf, hi, hg, sxbc, nq, sz, deint(rq), deint(rk), rv, rg]
    narrow = [nkc, nvc, nks, nvs, nkw, nvw, small]
    return jnp.concatenate(wide, axis=1).astype(BF16), jnp.concatenate(narrow, axis=1).astype(BF16)


def _build_cmp_weights(pe, w1, w2):
    w1r = w1.reshape(2, NSA_CMP_STRIDE, NSA_HEAD_DIM, NSA_CMP_HIDDEN)
    eye = jnp.eye(NSA_KV, dtype=w1.dtype)
    big = jnp.einsum("ardc,kj->arkdjc", w1r, eye).reshape(
        2, NSA_CMP_STRIDE * NSA_KV * NSA_HEAD_DIM, NSA_KV * NSA_CMP_HIDDEN)
    w2bd = jnp.einsum("cd,kj->kcjd", w2, eye).reshape(NSA_KV * NSA_CMP_HIDDEN, NSA_KV * NSA_HEAD_DIM)
    per = pe.reshape(2, NSA_CMP_STRIDE, 1, NSA_HEAD_DIM)
    pe2 = jnp.broadcast_to(per, (2, NSA_CMP_STRIDE, NSA_KV, NSA_HEAD_DIM)).reshape(2, -1)
    return pe2, big[0].astype(BF16), big[1].astype(BF16), w2bd.astype(BF16)


def _rotary_tables(S):
    half = RET_DK // 2
    theta = 1.0 / (10000.0 ** jnp.linspace(0.0, 1.0, half, dtype=F32))
    ang = jnp.arange(S, dtype=F32)[:, None] * theta[None, :]
    cos, sin = jnp.cos(ang), jnp.sin(ang)
    return jnp.concatenate([cos, cos], axis=1), jnp.concatenate([-sin, sin], axis=1)


def _mixer(x2, B, S, l, p, lower_bounds, band, cband, farq, cos_t, sin_t):
    T = B * S
    w_wide, w_narrow = _build_w_in(p["w_in"][l])
    tn = NCOL * LANES // 4
    proj = _proj(x2, w_wide.reshape(-1, 4, tn).transpose(1, 0, 2), tm=min(1024, T))
    kc, vc, ks, vs, kw, vw, small = _kvproj(x2, w_narrow, tm=min(512, T))
    row = lambda v: v.reshape(1, -1).astype(F32)

    lb = lower_bounds[l].astype(F32)
    o_a = _hgrn(proj, B, S, row(jnp.log(lb)), row(jnp.log1p(-lb)), row(1.0 - lb),
                row(p["hgrn_norm_g"][l]))

    nb = S // NSA_SLC_BLOCK
    grp = lambda t: t.reshape(B, nb, 4 * NSA_CMP_STRIDE * LANES)
    kcmp = _compress(grp(kc), *_build_cmp_weights(p["nsa_pe_k"][l], p["nsa_w1_k"][l], p["nsa_w2_k"][l]))
    vcmp = _compress(grp(vc), *_build_cmp_weights(p["nsa_pe_v"][l], p["nsa_w1_v"][l], p["nsa_w2_v"][l]))
    o_cmp, selb = _cmpattn(proj, kcmp, vcmp, cband, B, S)
    seq = lambda t: t.reshape(B, S, t.shape[1])
    o_sel = _selattn(proj, selb, seq(ks), seq(vs), band, farq, B, S)
    ngw = _widen_heads(p["nsa_norm_g"][l].astype(F32), 0).reshape(1, -1)
    o_b = _winattn(proj, small, o_cmp, o_sel, seq(kw), seq(vw), band, ngw, B, S)

    lane_vec = lambda v: jnp.zeros((1, LANES), F32).at[0, DT_LANE0:DT_LANE0 + SSM_HEADS].set(v.astype(F32))
    o_c, o_d = _ssd_retention(
        proj, small, B, S, p["ssm_conv_w"][l].astype(F32), row(p["ssm_conv_b"][l]),
        lane_vec(p["ssm_dt_bias"][l]), lane_vec(-jnp.exp(p["ssm_a_log"][l].astype(F32))),
        row(jnp.repeat(p["ssm_d"][l].astype(F32), SSM_HEAD_DIM)), row(p["ssm_norm_g"][l]), cos_t, sin_t)

    w_out = p["w_out"][l]
    wa, wb, wc, wd = (w_out[i * GROUP_W:(i + 1) * GROUP_W] for i in range(4))
    return o_a, o_b, o_c, o_d, wa.astype(BF16), _widen_heads(wb, 0).astype(BF16), wc.astype(BF16), wd.astype(BF16)


def kernel(x, ln1_g, ln1_b, ffn1_w1, ffn1_w3, ffn1_w2, ln2_g, ln2_b, w_in, w_out, hgrn_lb_logits, hgrn_norm_g, nsa_pe_k, nsa_w1_k, nsa_w2_k, nsa_pe_v, nsa_w1_v, nsa_w2_v, nsa_norm_g, rel_bias, ssm_conv_w, ssm_conv_b, ssm_dt_bias, ssm_a_log, ssm_d, ssm_norm_g, ln3_g, ln3_b, ffn2_w1, ffn2_w3, ffn2_w2):
    B, S, D = x.shape
    T = B * S
    depth = w_in.shape[0]
    p = dict(w_in=w_in, w_out=w_out, hgrn_norm_g=hgrn_norm_g, nsa_pe_k=nsa_pe_k, nsa_w1_k=nsa_w1_k,
             nsa_w2_k=nsa_w2_k, nsa_pe_v=nsa_pe_v, nsa_w1_v=nsa_w1_v, nsa_w2_v=nsa_w2_v,
             nsa_norm_g=nsa_norm_g, ssm_conv_w=ssm_conv_w, ssm_conv_b=ssm_conv_b,
             ssm_dt_bias=ssm_dt_bias, ssm_a_log=ssm_a_log, ssm_d=ssm_d, ssm_norm_g=ssm_norm_g)
    cum = jnp.cumsum(jax.nn.softmax(hgrn_lb_logits.astype(F32), axis=0), axis=0)
    lower_bounds = cum - cum[:1]
    rel_t = jnp.zeros((NSA_HEADS, LANES), F32).at[:, :REL_BUCKETS].set(rel_bias.astype(F32).T)
    band = _band_table(rel_t, TQ, BAND_OFF)
    cband = _band_table(rel_t, TQ, CMP_BAND_OFF, NSA_CMP_STRIDE, NSA_CMP_BLOCK - 1)
    bpt = TK // NSA_SLC_BLOCK
    farq = jnp.zeros((NSA_HEADS, LANES), BF16).at[:, bpt:bpt + 3].set(
        jnp.stack(_split3(rel_bias.astype(F32)[REL_BUCKETS - 1]), axis=1))
    farq = jnp.repeat(farq, TQ, axis=0)
    cos_t, sin_t = _rotary_tables(S)
    row = lambda v: v.reshape(1, -1).astype(F32)
    tm = min(512, T)
    tf = 512 if ffn1_w1.shape[2] % 512 == 0 else ffn1_w1.shape[2]
    x2 = x.reshape(T, D).astype(F32)
    for l in range(depth):
        x2 = _ffn(x2, _to_bf16_col_blocks(ffn1_w1, l, tf), _to_bf16_col_blocks(ffn1_w3, l, tf),
                  _to_bf16(ffn1_w2, l), row(ln1_g[l]), row(ln1_b[l]), tm)
        o_a, o_b, o_c, o_d, wa, wb, wc, wd = _mixer(x2, B, S, l, p, lower_bounds, band, cband, farq, cos_t, sin_t)
        x2 = _outproj(x2, o_a, o_b, o_c, o_d, wa, wb, wc, wd, row(ln2_g[l]), row(ln2_b[l]), min(512, T))
        x2 = _ffn(x2, _to_bf16_col_blocks(ffn2_w1, l, tf), _to_bf16_col_blocks(ffn2_w3, l, tf),
                  _to_bf16(ffn2_w2, l), row(ln3_g[l]), row(ln3_b[l]), tm)
    return x2.reshape(B, S, D).astype(x.dtype)
```

```python
import functools
import math

import numpy as np
import jax
import jax.numpy as jnp
from jax import lax
from jax.experimental import pallas as pl
from jax.experimental.pallas import tpu as pltpu

F32 = jnp.float32
BF16 = jnp.bfloat16

DEPTH = 2
GROUP_W = 512
ALPHA = (2 * DEPTH) ** 0.25
HG_HEADS = 4
NSA_HEADS = 8
NSA_KV = 2
NSA_GROUP = 4
NSA_HEAD_DIM = 64
NSA_CMP_STRIDE = 16
NSA_CMP_BLOCK = 32
NSA_SLC_BLOCK = 64
NSA_TOP_N = 16
NSA_WINDOW = 512
NSA_CMP_HIDDEN = 256
SSM_HEADS = 8
SSM_HEAD_DIM = 64
SSM_GROUPS = 2
SSM_STATE = 128
SSM_CONV = 4
RET_HEADS = 4
RET_DK = 128
REL_BUCKETS = 32
REL_EXACT = 16
REL_MAX_DIST = 2048
IN_SIZES = ((GROUP_W,) * 4 + (GROUP_W,) + (128,) * 6 + (24,)
            + (GROUP_W, 1024, SSM_HEADS) + (GROUP_W,) * 4)
IN_SPLITS = tuple(int(v) for v in np.cumsum(IN_SIZES)[:-1])

LANES = 128
SUBLANES = 8
VMEM_LIMIT = 56 * 1024 * 1024
CAST_BLOCK_BYTES = 4 * 1024 * 1024

COL = dict(hq=0, hf=4, hi=8, hg=12, sxbc=16, nq=24, sz=28, rq=32, rk=36, rv=40, rg=44)
NCOL = 48
KV_COLS = ("nkc", "nvc", "nks", "nvs", "nkw", "nvw", "small")
GATE_LANE0 = 0
DT_LANE0 = 24

CHUNK = 128
RET_CHUNK = 256
TQ = 128
TK = 1024
BAND_OFF = 2
CMP_BAND_OFF = 1
CMP_STEP = 256
WIN_SPAN = NSA_WINDOW + 2 * TQ
NEG = -1e30


def _params(sem):
    return pltpu.CompilerParams(dimension_semantics=sem, vmem_limit_bytes=VMEM_LIMIT)


def _dot(a, b):
    return jnp.dot(a, b, preferred_element_type=F32)


def _dot_nt(a, b):
    return lax.dot_general(a, b, (((1,), (1,)), ((), ())), preferred_element_type=F32)


def _split3(x):
    hi = x.astype(BF16)
    r1 = x - hi.astype(F32)
    mid = r1.astype(BF16)
    return hi, mid, (r1 - mid.astype(F32)).astype(BF16)


def _exact_left_dot(w, x):
    n = x.shape[1]
    y = _dot(w, jnp.concatenate(_split3(x), axis=1))
    return y[:, 0:n] + y[:, n:2 * n] + y[:, 2 * n:3 * n]


def _exact_right_dot(x, w):
    n = x.shape[0]
    y = _dot(jnp.concatenate(_split3(x), axis=0), w)
    return y[0:n] + y[n:2 * n] + y[2 * n:3 * n]


def _sigmoid(x):
    return 1.0 / (1.0 + jnp.exp(-x))


def _silu(x):
    return x * _sigmoid(x)


def _softplus(x):
    return jnp.maximum(x, 0.0) + jnp.log1p(jnp.exp(-jnp.abs(x)))


def _layer_norm(r, g, b):
    mu = jnp.mean(r, axis=-1, keepdims=True)
    d = r - mu
    var = jnp.mean(d * d, axis=-1, keepdims=True)
    return d * lax.rsqrt(var + 1e-5) * g + b


def _cast_kernel(x_ref, o_ref):
    o_ref[...] = x_ref[...].astype(o_ref.dtype)


def _to_bf16(w, l):
    _, r, c = w.shape
    tr = min(r, max(16, CAST_BLOCK_BYTES // (4 * c) // 16 * 16))
    while r % tr:
        tr -= 16
    return pl.pallas_call(
        _cast_kernel,
        grid=(r // tr,),
        in_specs=[pl.BlockSpec((None, tr, c), lambda i: (l, i, 0))],
        out_specs=pl.BlockSpec((tr, c), lambda i: (i, 0)),
        out_shape=jax.ShapeDtypeStruct((r, c), BF16),
        compiler_params=_params(("parallel",)),
        name="cast_bf16",
    )(w)


def _to_bf16_col_blocks(w, l, tc):
    _, r, c = w.shape
    return pl.pallas_call(
        _cast_kernel,
        grid=(c // tc,),
        in_specs=[pl.BlockSpec((None, r, tc), lambda j: (l, 0, j))],
        out_specs=pl.BlockSpec((None, r, tc), lambda j: (j, 0, 0)),
        out_shape=jax.ShapeDtypeStruct((c // tc, r, tc), BF16),
        compiler_params=_params(("parallel",)),
        name="cast_bf16_blocked",
    )(w)


def _ffn_kernel(x_ref, w1_ref, w3_ref, w2_ref, g_ref, b_ref, o_ref, acc_ref, xb_ref):
    j = pl.program_id(1)

    @pl.when(j == 0)
    def _():
        xb_ref[...] = x_ref[...].astype(BF16)
        acc_ref[...] = jnp.zeros_like(acc_ref)

    xb = xb_ref[...]
    h1 = _dot(xb, w1_ref[...])
    h3 = _dot(xb, w3_ref[...])
    a = (_silu(h1) * h3).astype(BF16)
    acc_ref[...] += _dot(a, w2_ref[...])

    @pl.when(j == pl.num_programs(1) - 1)
    def _():
        r = ALPHA * x_ref[...] + 0.5 * acc_ref[...]
        o_ref[...] = _layer_norm(r, g_ref[...], b_ref[...])


def _ffn(x, w1, w3, w2, g, b, tm):
    T, D = x.shape
    nf, _, tf = w1.shape
    return pl.pallas_call(
        _ffn_kernel,
        grid=(T // tm, nf),
        in_specs=[
            pl.BlockSpec((tm, D), lambda i, j: (i, 0)),
            pl.BlockSpec((None, D, tf), lambda i, j: (j, 0, 0)),
            pl.BlockSpec((None, D, tf), lambda i, j: (j, 0, 0)),
            pl.BlockSpec((tf, D), lambda i, j: (j, 0)),
            pl.BlockSpec((1, D), lambda i, j: (0, 0)),
            pl.BlockSpec((1, D), lambda i, j: (0, 0)),
        ],
        out_specs=pl.BlockSpec((tm, D), lambda i, j: (i, 0)),
        out_shape=jax.ShapeDtypeStruct((T, D), F32),
        scratch_shapes=[pltpu.VMEM((tm, D), F32), pltpu.VMEM((tm, D), BF16)],
        compiler_params=_params(("parallel", "arbitrary")),
        name="ffn_ln",
    )(x, w1, w3, w2, g, b)


def _proj_kernel(x_ref, w_ref, o_ref, xb_ref):
    @pl.when(pl.program_id(1) == 0)
    def _():
        xb_ref[...] = x_ref[...].astype(BF16)

    o_ref[...] = _dot(xb_ref[...], w_ref[...])


def _proj(x, w, tm):
    T, D = x.shape
    nn, _, tn = w.shape
    return pl.pallas_call(
        _proj_kernel,
        grid=(T // tm, nn),
        in_specs=[pl.BlockSpec((tm, D), lambda i, j: (i, 0)),
                  pl.BlockSpec((None, D, tn), lambda i, j: (j, 0, 0))],
        out_specs=pl.BlockSpec((tm, tn), lambda i, j: (i, j)),
        out_shape=jax.ShapeDtypeStruct((T, nn * tn), F32),
        scratch_shapes=[pltpu.VMEM((tm, D), BF16)],
        compiler_params=_params(("parallel", "arbitrary")),
        name="in_proj",
    )(x, w)


def _kvproj_kernel(x_ref, w_ref, kc_ref, vc_ref, ks_ref, vs_ref, kw_ref, vw_ref, sm_ref):
    tm = x_ref.shape[0]
    y = _dot(x_ref[...].astype(BF16), w_ref[...])
    piece = lambda n: y[:, n * LANES:(n + 1) * LANES]
    kc_ref[...] = piece(0)
    vc_ref[...] = piece(1)
    row = pl.program_id(0) * tm + lax.broadcasted_iota(jnp.int32, (tm, LANES), 0)
    lane = lax.broadcasted_iota(jnp.int32, (tm, LANES), 1)
    bpt = TK // NSA_SLC_BLOCK
    onehot = (((row // NSA_SLC_BLOCK) % bpt == lane) | ((lane >= bpt) & (lane < bpt + 3))).astype(BF16)
    ks_ref[...] = jnp.concatenate([piece(2).astype(BF16), onehot], axis=1)
    vs_ref[...] = jnp.concatenate([piece(3).astype(BF16), jnp.ones((tm, LANES), BF16)], axis=1)
    kw_ref[...] = piece(4).astype(BF16)
    vw_ref[...] = piece(5).astype(BF16)
    sm_ref[...] = piece(6)


def _kvproj(x, w, tm):
    T, D = x.shape
    narrow = lambda dt, width=LANES: (pl.BlockSpec((tm, width), lambda i: (i, 0)),
                                      jax.ShapeDtypeStruct((T, width), dt))
    outs = [narrow(F32), narrow(F32), narrow(BF16, 2 * LANES), narrow(BF16, 2 * LANES),
            narrow(BF16), narrow(BF16), narrow(F32)]
    return pl.pallas_call(
        _kvproj_kernel,
        grid=(T // tm,),
        in_specs=[pl.BlockSpec((tm, D), lambda i: (i, 0)), pl.BlockSpec(w.shape, lambda i: (0, 0))],
        out_specs=[o[0] for o in outs],
        out_shape=[o[1] for o in outs],
        compiler_params=_params(("parallel",)),
        name="kv_proj",
    )(x, w)


def _outproj_kernel(x_ref, oa_ref, ob_ref, oc_ref, od_ref, wa_ref, wb_ref, wc_ref, wd_ref,
                    g_ref, b_ref, o_ref):
    mix = (_dot(oa_ref[...], wa_ref[...]) + _dot(ob_ref[...], wb_ref[...])
           + _dot(oc_ref[...], wc_ref[...]) + _dot(od_ref[...], wd_ref[...]))
    o_ref[...] = _layer_norm(ALPHA * x_ref[...] + mix, g_ref[...], b_ref[...])


def _outproj(x, oa, ob, oc, od, wa, wb, wc, wd, g, b, tm):
    T, D = x.shape
    row = lambda a: pl.BlockSpec((tm, a.shape[1]), lambda i: (i, 0))
    full = lambda a: pl.BlockSpec(a.shape, lambda i: (0, 0))
    return pl.pallas_call(
        _outproj_kernel,
        grid=(T // tm,),
        in_specs=[row(x), row(oa), row(ob), row(oc), row(od),
                  full(wa), full(wb), full(wc), full(wd), full(g), full(b)],
        out_specs=row(x),
        out_shape=jax.ShapeDtypeStruct((T, D), F32),
        compiler_params=_params(("parallel",)),
        name="out_proj_ln",
    )(x, oa, ob, oc, od, wa, wb, wc, wd, g, b)


def _hgrn_tables(C):
    i = np.arange(C)[:, None]
    ip = np.arange(C)[None, :]
    seg = [(ip <= i),
           (ip > i)]
    masks = [np.eye(C, dtype=bool)]
    s = C // 2
    while s >= 1:
        blk = i // s
        if s < SUBLANES:
            seg.append(np.where(blk % 2 == 1, (ip > blk * s) & (ip <= i), (ip > i) & (ip <= (blk + 1) * s)))
        masks.append((blk % 2 == 1) & (ip // s == blk - 1))
        s //= 2
    seg = np.concatenate([x.astype(np.float32) for x in seg], axis=0)
    return seg, np.stack([m.astype(np.float32) for m in masks])


def _hgrn_chunks(q_ref, f_ref, i_ref, g_ref, llb_ref, l1m_ref, oml_ref, ng_ref,
                 seg_ref, msk_ref, o_ref, st_ref):
    for r in range(0, q_ref.shape[1], CHUNK):
        _hgrn_rows(q_ref, f_ref, i_ref, g_ref, llb_ref, l1m_ref, oml_ref, ng_ref, seg_ref, msk_ref,
                   o_ref, st_ref, pl.ds(r, CHUNK))


def _hgrn_rows(q_ref, f_ref, i_ref, g_ref, llb_ref, l1m_ref, oml_ref, ng_ref,
               seg_ref, msk_ref, o_ref, st_ref, rows):
    C = CHUNK
    nlev = msk_ref.shape[0] - 1
    for bi in range(q_ref.shape[0]):
        q = _silu(q_ref[bi, rows])
        z = f_ref[bi, rows]
        log_sig = jnp.minimum(z, 0.0) - jnp.log1p(jnp.exp(-jnp.abs(z)))
        cc = l1m_ref[...] + log_sig
        llb = llb_ref[...]
        logf = jnp.maximum(llb, cc) + jnp.log1p(jnp.exp(-jnp.abs(llb - cc)))
        k = oml_ref[...] * _sigmoid(-z)
        v = i_ref[bi, rows]
        seg = _exact_left_dot(seg_ref[...], logf)
        b_all = seg[0:C]
        lev = []
        s = C // 2
        while s >= SUBLANES:
            b3 = b_all.reshape(C // s, s, GROUP_W)
            start = b3[:, 0:1, :]
            nxt = jnp.concatenate([start[1:], start[-1:]], axis=0)
            odd = lax.broadcasted_iota(jnp.int32, b3.shape, 0) % 2 == 1
            lev.append(jnp.where(odd, b3 - start, nxt - b3).reshape(C, GROUP_W))
            s //= 2
        lev += [seg[r * C:(r + 1) * C] for r in range(2, seg.shape[0] // C)]
        outs = []
        for h in range(HG_HEADS):
            sl = slice(h * LANES, (h + 1) * LANES)
            qh, kh, vh = q[:, sl], k[:, sl], v[:, sl]
            a = msk_ref[0] * _dot_nt(qh.astype(BF16), kh.astype(BF16))
            for l in range(nlev):
                dec = jnp.exp(lev[l][:, sl])
                a = a + msk_ref[1 + l] * _dot_nt((qh * dec).astype(BF16), (kh * dec).astype(BF16))
            b = seg[0:C, sl]
            st = st_ref[bi, h]
            o = _dot(a.astype(BF16), vh.astype(BF16))
            o = o + _dot_nt((qh * jnp.exp(b)).astype(BF16), st.astype(BF16))
            kd = (kh * jnp.exp(seg[C:2 * C, sl])).astype(BF16)
            st_ref[bi, h] = st * jnp.exp(b[C - 1:C, :]) + _dot(vh.T.astype(BF16), kd)
            outs.append(o * lax.rsqrt(jnp.mean(o * o, axis=-1, keepdims=True) + 1e-6))
        o = jnp.concatenate(outs, axis=1)
        o_ref[bi, rows] = (o * ng_ref[...] * _silu(g_ref[bi, rows])).astype(o_ref.dtype)


def _ssd_chunk(z_ref, xbc_ref, sm_ref, cw_ref, cb_ref, dtb_ref, aneg_ref, dsk_ref, ng_ref,
               ex_ref, o_ref, tail_ref, st_ref):
    L = xbc_ref.shape[0]
    x = xbc_ref[...]
    xe = jnp.concatenate([tail_ref[...], x], axis=0)
    cw = cw_ref[...]
    conv = cb_ref[...]
    for kk in range(SSM_CONV):
        conv = conv + cw[kk:kk + 1, :] * xe[5 + kk:5 + kk + L, :]
    tail_ref[...] = x[L - 8:L, :]
    conv = _silu(conv)
    xs = conv[:, 0:GROUP_W]
    bm = conv[:, GROUP_W:GROUP_W + 256]
    cm = conv[:, GROUP_W + 256:GROUP_W + 512]

    dtf = _softplus(sm_ref[...] + dtb_ref[...])
    la = dtf * aneg_ref[...]
    ri = lax.broadcasted_iota(jnp.int32, (L, L), 0)
    ci = lax.broadcasted_iota(jnp.int32, (L, L), 1)
    tri = ri >= ci
    bfull = _exact_left_dot(tri.astype(BF16), la)
    ex = ex_ref[...]
    bexp = _exact_right_dot(bfull, ex)
    dtexp = _exact_right_dot(dtf, ex)
    b_t = bfull.T
    xdt = xs * dtexp
    lane = lax.broadcasted_iota(jnp.int32, (L, LANES), 1)

    scores = []
    for g in range(SSM_GROUPS):
        cg = cm[:, g * SSM_STATE:(g + 1) * SSM_STATE].astype(BF16)
        bg = bm[:, g * SSM_STATE:(g + 1) * SSM_STATE].astype(BF16)
        cb = _dot_nt(cg, bg)
        for hh in range(SSM_HEADS // SSM_GROUPS):
            h = g * (SSM_HEADS // SSM_GROUPS) + hh
            bcol = bfull[:, DT_LANE0 + h:DT_LANE0 + h + 1]
            brow = b_t[DT_LANE0 + h:DT_LANE0 + h + 1, :]
            dec = jnp.exp(jnp.where(tri, bcol - brow, NEG))
            scores.append((cb * dec).astype(BF16))
    y_pairs = []
    for u in range(SSM_HEADS // 2):
        slab = xdt[:, u * LANES:(u + 1) * LANES]
        lo = jnp.where(lane < SSM_HEAD_DIM, slab, 0.0).astype(BF16)
        hi = jnp.where(lane >= SSM_HEAD_DIM, slab, 0.0).astype(BF16)
        y_pairs.append(_dot(scores[2 * u], lo) + _dot(scores[2 * u + 1], hi))
    y_intra = jnp.concatenate(y_pairs, axis=1)

    blast = bexp[L - 1:L, :]
    w = (xdt * jnp.exp(blast - bexp)).astype(BF16)
    y_inter = []
    for g in range(SSM_GROUPS):
        gs = slice(g * 256, (g + 1) * 256)
        cg = cm[:, g * SSM_STATE:(g + 1) * SSM_STATE].astype(BF16)
        st = st_ref[g]
        y_inter.append(_dot(cg, st.astype(BF16)))
        bg_t = bm[:, g * SSM_STATE:(g + 1) * SSM_STATE].T.astype(BF16)
        st_ref[g] = st * jnp.exp(blast[:, gs]) + _dot(bg_t, w[:, gs])
    y = y_intra + jnp.concatenate(y_inter, axis=1) * jnp.exp(bexp) + dsk_ref[...] * xs
    y = y * _silu(z_ref[...])
    halves = []
    for g in range(SSM_GROUPS):
        seg = y[:, g * 256:(g + 1) * 256]
        halves.append(seg * lax.rsqrt(jnp.mean(seg * seg, axis=-1, keepdims=True) + 1e-6))
    o_ref[...] = (jnp.concatenate(halves, axis=1) * ng_ref[...]).astype(o_ref.dtype)


def _scans_kernel(hq_ref, hf_ref, hi_ref, hg_ref, llb_ref, l1m_ref, oml_ref, hng_ref, seg_ref, msk_ref,
                  z_ref, xbc_ref, sm_ref, cw_ref, cb_ref, dtb_ref, aneg_ref, dsk_ref, ng_ref, ex_ref,
                  q_ref, k_ref, v_ref, g_ref, cos_ref, sin_ref, dec_ref, qs_ref, ks_ref, sd_ref,
                  o_hg_ref, o_ssd_ref, o_ret_ref, st_hg_ref, tail_ref, st_ssd_ref, st_ret_ref):
    @pl.when(pl.program_id(0) == 0)
    def _():
        st_hg_ref[...] = jnp.zeros_like(st_hg_ref)
        tail_ref[...] = jnp.zeros_like(tail_ref)
        st_ssd_ref[...] = jnp.zeros_like(st_ssd_ref)
        st_ret_ref[...] = jnp.zeros_like(st_ret_ref)

    _hgrn_chunks(hq_ref, hf_ref, hi_ref, hg_ref, llb_ref, l1m_ref, oml_ref, hng_ref, seg_ref, msk_ref,
                 o_hg_ref, st_hg_ref)

    for b in range(xbc_ref.shape[0]):
        for r in range(0, xbc_ref.shape[1], CHUNK):
            rows = pl.ds(r, CHUNK)
            _ssd_chunk(z_ref.at[b, rows], xbc_ref.at[b, rows], sm_ref.at[b, rows], cw_ref, cb_ref, dtb_ref,
                       aneg_ref, dsk_ref, ng_ref, ex_ref, o_ssd_ref.at[b, rows], tail_ref.at[b],
                       st_ssd_ref.at[b])
    _ret_step(q_ref, k_ref, v_ref, g_ref, cos_ref, sin_ref, dec_ref, qs_ref, ks_ref, sd_ref,
              o_ret_ref, st_ret_ref)


def _ret_step(q_ref, k_ref, v_ref, g_ref, cos_ref, sin_ref, dec_ref, qs_ref, ks_ref, sd_ref,
              o_ref, st_ref):
    cos = cos_ref[...]
    sin = sin_ref[...]
    for b in range(q_ref.shape[0]):
        outs = []
        for h in range(RET_HEADS):
            sl = slice(h * LANES, (h + 1) * LANES)
            qh = q_ref[b, :, sl]
            kh = k_ref[b, :, sl]
            qh = qh * cos + pltpu.roll(qh, RET_DK // 2, axis=1) * sin
            kh = (kh * cos + pltpu.roll(kh, RET_DK // 2, axis=1) * sin) * (RET_DK ** -0.5)
            vh = v_ref[b, :, sl].astype(BF16)
            sc = (_dot_nt(qh.astype(BF16), kh.astype(BF16)) * dec_ref[h]).astype(BF16)
            st = st_ref[b, h]
            y = _dot(sc, vh) + _dot((qh * qs_ref[:, sl]).astype(BF16), st.astype(BF16))
            kd_t = (kh * ks_ref[:, sl]).T.astype(BF16)
            st_ref[b, h] = st * sd_ref[h] + _dot(kd_t, vh)
            mu = jnp.mean(y, axis=-1, keepdims=True)
            d = y - mu
            outs.append(d * lax.rsqrt(jnp.mean(d * d, axis=-1, keepdims=True) + 1e-5))
        o_ref[b] = (_silu(g_ref[b]) * jnp.concatenate(outs, axis=1)).astype(o_ref.dtype)


def _scans(proj, small, B, S, llb, l1m, oml, hng, cw, cb, dtb, aneg, dsk, ng, cos_t, sin_t):
    L = min(RET_CHUNK, S)
    nc = S // L
    seg, msk = _hgrn_tables(CHUNK)
    seg, msk = jnp.asarray(seg, BF16), jnp.asarray(msk)
    vec = pl.BlockSpec((1, GROUP_W), lambda c: (0, 0))
    ex = np.zeros((LANES, GROUP_W), np.float32)
    for h in range(SSM_HEADS):
        ex[DT_LANE0 + h, h * SSM_HEAD_DIM:(h + 1) * SSM_HEAD_DIM] = 1.0
    ex = jnp.asarray(ex, BF16)
    lg = jnp.log(1.0 - 2.0 ** (-5.0 - jnp.arange(RET_HEADS, dtype=F32)))
    i = jnp.arange(L, dtype=F32)
    diff = i[:, None] - i[None, :]
    dec = jnp.where(diff >= 0, jnp.exp(lg[:, None, None] * jnp.maximum(diff, 0.0)), 0.0)
    rep = lambda t: jnp.repeat(t, LANES, axis=1)
    qs = rep(jnp.exp((i[:, None] + 1.0) * lg[None, :]))
    ks = rep(jnp.exp((L - 1.0 - i[:, None]) * lg[None, :]))
    sd = jnp.broadcast_to(jnp.exp(L * lg)[:, None, None], (RET_HEADS, LANES, LANES))
    proj3 = proj.reshape(B, S, proj.shape[1])
    full2 = lambda a: pl.BlockSpec(a.shape, lambda c: (0, 0))
    col = lambda name: pl.BlockSpec((B, L, GROUP_W), lambda c, n=COL[name] // 4: (0, c, n))
    out = pl.BlockSpec((B, L, GROUP_W), lambda c: (0, c, 0))
    o_a, o_c, o_d = pl.pallas_call(
        _scans_kernel,
        grid=(nc,),
        in_specs=[col("hq"), col("hf"), col("hi"), col("hg"), vec, vec, vec, vec,
                  full2(seg), pl.BlockSpec(msk.shape, lambda c: (0, 0, 0)),
                  col("sz"),
                  pl.BlockSpec((B, L, 1024), lambda c: (0, c, COL["sxbc"] // 8)),
                  pl.BlockSpec((B, L, LANES), lambda c: (0, c, 0)),
                  full2(cw), full2(cb), full2(dtb), full2(aneg), full2(dsk), full2(ng), full2(ex),
                  col("rq"), col("rk"), col("rv"), col("rg"),
                  pl.BlockSpec((L, LANES), lambda c: (c, 0)),
                  pl.BlockSpec((L, LANES), lambda c: (c, 0)),
                  pl.BlockSpec((RET_HEADS, L, L), lambda c: (0, 0, 0)),
                  pl.BlockSpec((L, GROUP_W), lambda c: (0, 0)),
                  pl.BlockSpec((L, GROUP_W), lambda c: (0, 0)),
                  pl.BlockSpec((RET_HEADS, LANES, LANES), lambda c: (0, 0, 0))],
        out_specs=[out, out, out],
        out_shape=[jax.ShapeDtypeStruct((B, S, GROUP_W), BF16)] * 3,
        scratch_shapes=[pltpu.VMEM((B, HG_HEADS, LANES, LANES), F32),
                        pltpu.VMEM((B, 8, 1024), F32), pltpu.VMEM((B, SSM_GROUPS, SSM_STATE, 256), F32),
                        pltpu.VMEM((B, RET_HEADS, RET_DK, RET_DK), F32)],
        compiler_params=_params(("arbitrary",)),
        name="recurrent_mixers",
    )(proj3, proj3, proj3, proj3, llb, l1m, oml, hng, seg, msk,
      proj3, proj3, small.reshape(B, S, LANES), cw, cb, dtb, aneg, dsk, ng, ex,
      proj3, proj3, proj3, proj3, cos_t, sin_t, dec, qs, ks, sd)
    return tuple(o.reshape(B * S, GROUP_W) for o in (o_a, o_c, o_d))


def _t5_bucket(dist):
    n = jnp.maximum(dist, 0)
    nf = jnp.maximum(n, 1).astype(F32)
    large = REL_EXACT + (jnp.log(nf / REL_EXACT) / math.log(REL_MAX_DIST / REL_EXACT)
                         * (REL_BUCKETS - REL_EXACT)).astype(jnp.int32)
    return jnp.where(n < REL_EXACT, n, jnp.minimum(large, REL_BUCKETS - 1))


def _head_bias(bucket, rel_ref):
    rows, cols = bucket.shape
    per_head = []
    for h in range(NSA_HEADS):
        tbl = jnp.broadcast_to(rel_ref[h:h + 1, :], (rows, LANES))
        chunks = [jnp.take_along_axis(tbl, bucket[:, c:c + LANES], axis=1)
                  for c in range(0, cols, LANES)]
        per_head.append(chunks[0] if len(chunks) == 1 else jnp.concatenate(chunks, axis=1))
    return jnp.stack(per_head, axis=0)


def _stack_heads(q):
    lane = lax.broadcasted_iota(jnp.int32, (q.shape[0], LANES), 1)
    rows = []
    for h in range(NSA_HEADS):
        slab = q[:, (h // 2) * LANES:(h // 2 + 1) * LANES]
        src_half, dst_half = h % 2, h // NSA_GROUP
        if src_half != dst_half:
            slab = pltpu.roll(slab, NSA_HEAD_DIM, axis=1)
        rows.append(jnp.where(lane // NSA_HEAD_DIM == dst_half, slab, 0.0))
    return jnp.concatenate(rows, axis=0).astype(BF16)


def _cmp_kernel(g_ref, pe_ref, w1a_ref, w1b_ref, w2_ref, o_ref):
    nb = g_ref.shape[0]
    gw = g_ref.shape[1] // 4
    pe = pe_ref[...]
    slabs = [g_ref[:, s * gw:(s + 1) * gw] for s in range(4)]
    nxt0 = pltpu.roll(slabs[0], nb - 1, axis=0)
    for s in range(4):
        a = (slabs[s] + pe[0:1, :]).astype(BF16)
        bn = ((slabs[s + 1] if s < 3 else nxt0) + pe[1:2, :]).astype(BF16)
        hid = _silu(_dot(a, w1a_ref[...]) + _dot(bn, w1b_ref[...]))
        o_ref[:, s * LANES:(s + 1) * LANES] = _dot(hid.astype(BF16), w2_ref[...]).astype(o_ref.dtype)


def _compress(g, pe2, w1a, w1b, w2bd):
    B, nb, gw4 = g.shape
    full2 = lambda a: pl.BlockSpec(a.shape, lambda b: (0, 0))
    return pl.pallas_call(
        _cmp_kernel,
        grid=(B,),
        in_specs=[pl.BlockSpec((None, nb, gw4), lambda b: (b, 0, 0)),
                  full2(pe2), full2(w1a), full2(w1b), full2(w2bd)],
        out_specs=pl.BlockSpec((None, nb, 4 * LANES), lambda b: (b, 0, 0)),
        out_shape=jax.ShapeDtypeStruct((B, nb, 4 * LANES), BF16),
        compiler_params=_params(("parallel",)),
        name="nsa_compress",
    )(g, pe2, w1a, w1b, w2bd).reshape(B, 4 * nb, LANES)


def _cmpattn_kernel(q_ref, kc_ref, vc_ref, band_ref, st_ref, oc_ref, selb_ref):
    tq = q_ref.shape[1]
    ncmp = kc_ref.shape[1]
    nb = ncmp // 4
    R = NSA_HEADS * tq
    q0 = pl.program_id(0) * tq
    nd = band_ref.shape[0]

    def attend(ncv, nbv):
        for b in range(q_ref.shape[0]):
            attend_one(b, ncv, nbv)

    def attend_one(b, ncv, nbv):
        Q = _stack_heads(q_ref[b])
        s3 = _dot_nt(Q, kc_ref[b, 0:ncv, :]).reshape(NSA_HEADS, tq, ncv)
        parts = []
        for ch in range(ncv // LANES):
            d = jnp.clip((q0 - ch * LANES * NSA_CMP_STRIDE) // tq + CMP_BAND_OFF, 0, nd - 1)
            parts.append(s3[:, :, ch * LANES:(ch + 1) * LANES] + band_ref[d])
        s3 = jnp.concatenate(parts, axis=2)
        mx = jnp.max(s3, axis=-1, keepdims=True)
        e = jnp.exp(s3 - mx)
        live = (q0 + lax.broadcasted_iota(jnp.int32, (tq, 1), 0) >= NSA_CMP_BLOCK - 1)[None]
        p = e * jnp.where(live, 1.0 / jnp.sum(e, axis=-1, keepdims=True), 0.0)
        oc_ref[b] = _dot(p.reshape(R, ncv).astype(BF16), vc_ref[b, 0:ncv, :])

        ps = p.reshape(NSA_KV, NSA_GROUP, tq, ncv).sum(axis=1).reshape(NSA_KV * tq, ncv)
        imp = _exact_right_dot(ps, st_ref[0:ncv, 0:nbv])
        j = lax.broadcasted_iota(jnp.int32, (NSA_KV * tq, nbv), 1)
        t = q0 + (lax.broadcasted_iota(jnp.int32, (NSA_KV * tq, nbv), 0) % tq)
        cur = t // NSA_SLC_BLOCK
        forced = (j == 0) | (j == cur) | (j == cur - 1)
        score = jnp.where(j > cur, -1.0, jnp.where(forced, NSA_GROUP + 1.0, imp))
        score = score.T
        jf = lax.broadcasted_iota(jnp.int32, score.shape, 0).astype(F32)
        sel = score == NSA_GROUP + 1.0
        score = jnp.where(sel, -jnp.inf, score)
        for _ in range(min(NSA_TOP_N, nb) - 3):
            best = jnp.max(score, axis=0, keepdims=True)
            first = jnp.min(jnp.where(score == best, jf, float(nbv)), axis=0, keepdims=True)
            hit = jf == first
            sel = sel | hit
            score = jnp.where(hit, -jnp.inf, score)
        selb = jnp.where(sel, 0.0, NEG).T.astype(selb_ref.dtype)
        for kv in range(NSA_KV):
            selb_ref[b, :, kv * nb:kv * nb + nbv] = selb[kv * tq:(kv + 1) * tq]
            if nbv < nb:
                selb_ref[b, :, kv * nb + nbv:(kv + 1) * nb] = jnp.full((tq, nb - nbv), NEG, selb_ref.dtype)

    if ncmp % CMP_STEP:
        attend(ncmp, nb)
    else:
        nvar = ncmp // CMP_STEP
        sizes = [(CMP_STEP * v, min(nb, LANES * ((v + 1) // 2))) for v in range(1, nvar + 1)]
        last_key = (q0 + tq - 1) // NSA_CMP_STRIDE
        lax.switch(jnp.minimum(last_key // CMP_STEP, nvar - 1),
                   [functools.partial(attend, ncv, nbv) for ncv, nbv in sizes])


def _cmpattn(proj, kcmp, vcmp, cband, B, S):
    tq = TQ
    nqt = S // tq
    ncmp = kcmp.shape[1]
    nb = ncmp // 4
    off = np.arange(ncmp)[:, None] - 4 * np.arange(nb)[None, :]
    stencil = np.where((off >= 0) & (off <= 2), 1.0, np.where((off == -1) | (off == 3), 0.5, 0.0))
    stencil = jnp.asarray(stencil, BF16)
    once = pl.Buffered(1)
    R = NSA_HEADS * tq
    o_cmp, selb = pl.pallas_call(
        _cmpattn_kernel,
        grid=(nqt,),
        in_specs=[pl.BlockSpec((B, tq, GROUP_W), lambda i: (0, i, COL["nq"] // 4)),
                  pl.BlockSpec((B, ncmp, LANES), lambda i: (0, 0, 0), pipeline_mode=once),
                  pl.BlockSpec((B, ncmp, LANES), lambda i: (0, 0, 0), pipeline_mode=once),
                  pl.BlockSpec(cband.shape, lambda i: (0, 0, 0, 0), pipeline_mode=once),
                  pl.BlockSpec(stencil.shape, lambda i: (0, 0), pipeline_mode=once)],
        out_specs=[pl.BlockSpec((B, R, LANES), lambda i: (0, i, 0)),
                   pl.BlockSpec((B, tq, 2 * nb), lambda i: (0, i, 0))],
        out_shape=[jax.ShapeDtypeStruct((B, S * NSA_HEADS, LANES), F32),
                   jax.ShapeDtypeStruct((B, S, 2 * nb), BF16)],
        compiler_params=_params(("arbitrary",)),
        name="nsa_cmp_attn_topk",
    )(proj.reshape(B, S, proj.shape[1]), kcmp, vcmp, cband, stencil)
    return o_cmp.reshape(B * S * NSA_HEADS, LANES), selb.reshape(B * S, 2 * nb)


def _band_kernel(rel_ref, o_ref, *, entry_off, key_step, key_end):
    tq = o_ref.shape[1]
    delta = (pl.program_id(0) - entry_off) * tq
    row = lax.broadcasted_iota(jnp.int32, (tq, LANES), 0)
    col = lax.broadcasted_iota(jnp.int32, (tq, LANES), 1)
    dist = delta + row - (col * key_step + key_end)
    bias = _head_bias(_t5_bucket(dist), rel_ref)
    o_ref[...] = jnp.where((dist >= 0)[None], bias, NEG)


def _band_table(rel_t, tq, entry_off, key_step=1, key_end=0):
    nd = -(-(REL_MAX_DIST + key_step * (LANES - 1) + key_end) // tq) + entry_off + 1
    return pl.pallas_call(
        functools.partial(_band_kernel, entry_off=entry_off, key_step=key_step, key_end=key_end),
        grid=(nd,),
        in_specs=[pl.BlockSpec(rel_t.shape, lambda d: (0, 0))],
        out_specs=pl.BlockSpec((None, NSA_HEADS, tq, LANES), lambda d: (d, 0, 0, 0)),
        out_shape=jax.ShapeDtypeStruct((nd, NSA_HEADS, tq, LANES), F32),
        compiler_params=_params(("parallel",)),
        name="nsa_bias_band",
    )(rel_t)


def _selattn_kernel(q_ref, selb_ref, ks_ref, vs_ref, band_ref, sp_ref, farq_ref, os_ref,
                    qaug_ref, msel_ref, s_ref, mt_ref, m_ref, acc_ref):
    tq = q_ref.shape[0]
    R = NSA_HEADS * tq
    nb = selb_ref.shape[1] // 2
    n_kt = msel_ref.shape[0]
    tk = ks_ref.shape[0] // n_kt
    nd = band_ref.shape[0]
    q0 = pl.program_id(1) * tq
    qaug_ref[:, 0:LANES] = _stack_heads(q_ref[...])
    selb2 = jnp.concatenate([selb_ref[:, 0:nb], selb_ref[:, nb:2 * nb]], axis=0)
    spread = _dot(selb2, sp_ref[...])
    for c in range(n_kt):
        msel_ref[c] = spread[:, c * LANES:(c + 1) * LANES].astype(BF16)
    m_ref[...] = jnp.full(m_ref.shape, NEG, F32)
    acc_ref[...] = jnp.zeros_like(acc_ref)
    n_tiles = (q0 + tq - 1) // tk + 1

    far_pairs = (jnp.maximum(q0 - (REL_MAX_DIST - 1), 0) // tk) // 2

    def scores(c, slot, far):
        c = jnp.minimum(c, n_kt - 1)
        k0 = pl.multiple_of(c * tk, tk)
        m2 = msel_ref[c]
        flags = jnp.concatenate([m2[0:tq]] * NSA_GROUP + [m2[tq:2 * tq]] * NSA_GROUP, axis=0)
        qaug_ref[:, LANES:2 * LANES] = flags + farq_ref[...] if far else flags
        s = _dot_nt(qaug_ref[...], ks_ref[pl.ds(k0, tk), :])
        if not far:
            s3 = s.reshape(NSA_HEADS, tq, tk)
            parts = []
            for ch in range(tk // LANES):
                d = jnp.clip((q0 - k0) // tq - ch * (LANES // tq) + BAND_OFF, 0, nd - 1)
                parts.append(s3[:, :, ch * LANES:(ch + 1) * LANES] + band_ref[d])
            s = jnp.concatenate(parts, axis=2).reshape(R, tk)
        s_ref[slot] = s
        mt_ref[slot] = jnp.broadcast_to(jnp.max(s, axis=-1, keepdims=True), (R, LANES))

    def accumulate(c, slot):
        m_old = m_ref[...]
        m_new = jnp.maximum(m_old, mt_ref[slot])
        alpha = jnp.exp(m_old - m_new)
        p = jnp.exp(s_ref[slot] - jnp.tile(m_new, (1, tk // LANES)))
        vt = vs_ref[pl.ds(pl.multiple_of(c * tk, tk), tk), :]
        acc_ref[...] = jnp.tile(alpha, (1, 2)) * acc_ref[...] + _dot(p.astype(BF16), vt)
        m_ref[...] = m_new

    def pair(i, far_odd, far_even):
        c = 2 * i
        scores(c + 1, 1, far_odd)
        accumulate(c, 0)
        scores(c + 2, 0, far_even)
        accumulate(c + 1, 1)

    scores(0, 0, False)
    last_far = jnp.maximum(far_pairs - 1, 0)
    lax.fori_loop(0, last_far, lambda i, carry: pair(i, True, True), None)
    lax.fori_loop(last_far, far_pairs, lambda i, carry: pair(i, True, False), None)
    lax.fori_loop(far_pairs, n_tiles // 2, lambda i, carry: pair(i, False, False), None)

    @pl.when(n_tiles % 2 == 1)
    def _():
        accumulate(n_tiles - 1, 0)

    acc = acc_ref[...]
    os_ref[...] = acc[:, 0:LANES] / acc[:, LANES:2 * LANES]


def _selattn(proj, selb, ks, vs, band, farq, B, S):
    tq = TQ
    tk = min(TK, S)
    n_kt = S // tk
    bpt = tk // NSA_SLC_BLOCK
    nqt = S // tq
    nb = S // NSA_SLC_BLOCK
    R = NSA_HEADS * tq
    sp = np.zeros((nb, n_kt * LANES), np.float32)
    sp[np.arange(nb), (np.arange(nb) // bpt) * LANES + np.arange(nb) % bpt] = 1.0
    once = pl.Buffered(1)
    return pl.pallas_call(
        _selattn_kernel,
        grid=(B, nqt),
        in_specs=[pl.BlockSpec((tq, GROUP_W), lambda b, i: (b * nqt + i, COL["nq"] // 4)),
                  pl.BlockSpec((tq, 2 * nb), lambda b, i: (b * nqt + i, 0)),
                  pl.BlockSpec((None, S, 2 * LANES), lambda b, i: (b, 0, 0), pipeline_mode=once),
                  pl.BlockSpec((None, S, 2 * LANES), lambda b, i: (b, 0, 0), pipeline_mode=once),
                  pl.BlockSpec(band.shape, lambda b, i: (0, 0, 0, 0), pipeline_mode=once),
                  pl.BlockSpec(sp.shape, lambda b, i: (0, 0)),
                  pl.BlockSpec(farq.shape, lambda b, i: (0, 0))],
        out_specs=pl.BlockSpec((R, LANES), lambda b, i: (b * nqt + i, 0)),
        out_shape=jax.ShapeDtypeStruct((B * S * NSA_HEADS, LANES), F32),
        scratch_shapes=[pltpu.VMEM((R, 2 * LANES), BF16), pltpu.VMEM((n_kt, 2 * tq, LANES), BF16),
                        pltpu.VMEM((2, R, tk), F32), pltpu.VMEM((2, R, LANES), F32),
                        pltpu.VMEM((R, LANES), F32), pltpu.VMEM((R, 2 * LANES), F32)],
        compiler_params=_params(("parallel", "arbitrary")),
        name="nsa_sel_attn",
    )(proj, selb, ks, vs, band, jnp.asarray(sp, BF16), farq)


def _winattn_kernel(q_ref, sm_ref, oc_ref, os_ref, kw_ref, vw_ref, band_ref, ng_ref, o_ref):
    tq = q_ref.shape[1]
    R = NSA_HEADS * tq
    S = kw_ref.shape[1]
    span = min(WIN_SPAN, S)
    nd = band_ref.shape[0]
    q0 = pl.program_id(0) * tq
    start = pl.multiple_of(jnp.clip(q0 + tq - span, 0, S - span), tq)
    row = lax.broadcasted_iota(jnp.int32, (tq, span), 0)
    col = lax.broadcasted_iota(jnp.int32, (tq, span), 1)
    in_window = ((q0 + row) - (start + col) < NSA_WINDOW)[None]
    lane = lax.broadcasted_iota(jnp.int32, (tq, LANES), 1)
    for b in range(q_ref.shape[0]):
        Q = _stack_heads(q_ref[b])
        kt = kw_ref[b, pl.ds(start, span), :]
        vt = vw_ref[b, pl.ds(start, span), :]
        s3 = _dot_nt(Q, kt).reshape(NSA_HEADS, tq, span)
        parts = []
        for ch in range(span // LANES):
            d = jnp.clip((q0 - start) // tq - ch * (LANES // tq) + BAND_OFF, 0, nd - 1)
            part = s3[:, :, ch * LANES:(ch + 1) * LANES] + band_ref[d]
            if (span - tq) + tq - 1 - ch * LANES >= NSA_WINDOW:
                part = jnp.where(in_window[:, :, ch * LANES:(ch + 1) * LANES], part, NEG)
            parts.append(part)
        s3 = jnp.concatenate(parts, axis=2)
        mx = jnp.max(s3, axis=-1, keepdims=True)
        e = jnp.exp(s3 - mx)
        inv = (1.0 / jnp.sum(e, axis=-1, keepdims=True)).reshape(R, 1)
        ow = _dot(e.reshape(R, span).astype(BF16), vt) * inv

        gates = _sigmoid(sm_ref[b])
        heads = []
        ssq = jnp.zeros((tq, 1), F32)
        for h in range(NSA_HEADS):
            rs = slice(h * tq, (h + 1) * tq)
            g = [gates[:, GATE_LANE0 + 3 * h + br:GATE_LANE0 + 3 * h + br + 1] for br in range(3)]
            oh = g[0] * oc_ref[b, rs, :] + g[1] * os_ref[b, rs, :] + g[2] * ow[rs, :]
            kv = h // NSA_GROUP
            valid = (lane >= kv * NSA_HEAD_DIM) & (lane < (kv + 1) * NSA_HEAD_DIM)
            oh = jnp.where(valid, oh, 0.0)
            ssq = ssq + jnp.sum(oh * oh, axis=-1, keepdims=True)
            heads.append(oh)
        rinv = lax.rsqrt(ssq / GROUP_W + 1e-6)
        o_ref[b] = (jnp.concatenate(heads, axis=1) * rinv * ng_ref[...]).astype(o_ref.dtype)


def _winattn(proj, small, oc, os_, kw, vw, band, ngw, B, S):
    tq = TQ
    R = NSA_HEADS * tq
    once = pl.Buffered(1)
    per_b = lambda t: t.reshape(B, t.shape[0] // B, t.shape[1])
    return pl.pallas_call(
        _winattn_kernel,
        grid=(S // tq,),
        in_specs=[pl.BlockSpec((B, tq, GROUP_W), lambda i: (0, i, COL["nq"] // 4)),
                  pl.BlockSpec((B, tq, LANES), lambda i: (0, i, 0)),
                  pl.BlockSpec((B, R, LANES), lambda i: (0, i, 0)),
                  pl.BlockSpec((B, R, LANES), lambda i: (0, i, 0)),
                  pl.BlockSpec((B, S, LANES), lambda i: (0, 0, 0), pipeline_mode=once),
                  pl.BlockSpec((B, S, LANES), lambda i: (0, 0, 0), pipeline_mode=once),
                  pl.BlockSpec(band.shape, lambda i: (0, 0, 0, 0), pipeline_mode=once),
                  pl.BlockSpec(ngw.shape, lambda i: (0, 0))],
        out_specs=pl.BlockSpec((B, tq, NSA_HEADS * LANES), lambda i: (0, i, 0)),
        out_shape=jax.ShapeDtypeStruct((B, S, NSA_HEADS * LANES), BF16),
        compiler_params=_params(("arbitrary",)),
        name="nsa_win_attn_merge",
    )(per_b(proj), per_b(small), per_b(oc), per_b(os_), kw, vw, band, ngw).reshape(B * S, NSA_HEADS * LANES)


def _widen_heads(x, axis):
    x = jnp.moveaxis(x, axis, -1)
    lead = x.shape[:-1]
    x = x.reshape(*lead, NSA_KV, NSA_GROUP, 1, NSA_HEAD_DIM)
    sel = jnp.eye(NSA_KV, dtype=x.dtype).reshape(NSA_KV, 1, NSA_KV, 1)
    x = (x * sel).reshape(*lead, NSA_HEADS * LANES)
    return jnp.moveaxis(x, -1, axis)


def _build_w_in(w):
    (hq, hf, hi, hg, nq, nkc, nvc, nks, nvs, nkw, nvw, ngate,
     sz, sxbc, sdt, rq, rk, rv, rg) = jnp.split(w, IN_SPLITS, axis=1)
    D = w.shape[0]
    nq = nq * NSA_HEAD_DIM ** -0.5
    deint = lambda t: t.reshape(D, RET_HEADS, RET_DK // 2, 2).transpose(0, 1, 3, 2).reshape(D, GROUP_W)
    small = jnp.concatenate([ngate, sdt, jnp.zeros((D, LANES - 32), w.dtype)], axis=1)
    wide = [hq, hf, hi, hg, sxbc, nq, sz, deint(rq), deint(rk), rv, rg]
    narrow = [nkc, nvc, nks, nvs, nkw, nvw, small]
    return jnp.concatenate(wide, axis=1).astype(BF16), jnp.concatenate(narrow, axis=1).astype(BF16)


def _build_cmp_weights(pe, w1, w2):
    w1r = w1.reshape(2, NSA_CMP_STRIDE, NSA_HEAD_DIM, NSA_CMP_HIDDEN)
    eye = jnp.eye(NSA_KV, dtype=w1.dtype)
    big = jnp.einsum("ardc,kj->arkdjc", w1r, eye).reshape(
        2, NSA_CMP_STRIDE * NSA_KV * NSA_HEAD_DIM, NSA_KV * NSA_CMP_HIDDEN)
    w2bd = jnp.einsum("cd,kj->kcjd", w2, eye).reshape(NSA_KV * NSA_CMP_HIDDEN, NSA_KV * NSA_HEAD_DIM)
    per = pe.reshape(2, NSA_CMP_STRIDE, 1, NSA_HEAD_DIM)
    pe2 = jnp.broadcast_to(per, (2, NSA_CMP_STRIDE, NSA_KV, NSA_HEAD_DIM)).reshape(2, -1)
    return pe2, big[0].astype(BF16), big[1].astype(BF16), w2bd.astype(BF16)


def _rotary_tables(S):
    half = RET_DK // 2
    theta = 1.0 / (10000.0 ** jnp.linspace(0.0, 1.0, half, dtype=F32))
    ang = jnp.arange(S, dtype=F32)[:, None] * theta[None, :]
    cos, sin = jnp.cos(ang), jnp.sin(ang)
    return jnp.concatenate([cos, cos], axis=1), jnp.concatenate([-sin, sin], axis=1)


def _mixer(x2, B, S, l, p, lower_bounds, band, cband, farq, cos_t, sin_t):
    T = B * S
    w_wide, w_narrow = _build_w_in(p["w_in"][l])
    tn = NCOL * LANES // 4
    proj = _proj(x2, w_wide.reshape(-1, 4, tn).transpose(1, 0, 2), tm=min(1024, T))
    kc, vc, ks, vs, kw, vw, small = _kvproj(x2, w_narrow, tm=min(512, T))
    row = lambda v: v.reshape(1, -1).astype(F32)

    nb = S // NSA_SLC_BLOCK
    grp = lambda t: t.reshape(B, nb, 4 * NSA_CMP_STRIDE * LANES)
    kcmp = _compress(grp(kc), *_build_cmp_weights(p["nsa_pe_k"][l], p["nsa_w1_k"][l], p["nsa_w2_k"][l]))
    vcmp = _compress(grp(vc), *_build_cmp_weights(p["nsa_pe_v"][l], p["nsa_w1_v"][l], p["nsa_w2_v"][l]))
    o_cmp, selb = _cmpattn(proj, kcmp, vcmp, cband, B, S)
    seq = lambda t: t.reshape(B, S, t.shape[1])
    o_sel = _selattn(proj, selb, seq(ks), seq(vs), band, farq, B, S)
    ngw = _widen_heads(p["nsa_norm_g"][l].astype(F32), 0).reshape(1, -1)
    o_b = _winattn(proj, small, o_cmp, o_sel, seq(kw), seq(vw), band, ngw, B, S)

    lb = lower_bounds[l].astype(F32)
    lane_vec = lambda v: jnp.zeros((1, LANES), F32).at[0, DT_LANE0:DT_LANE0 + SSM_HEADS].set(v.astype(F32))
    o_a, o_c, o_d = _scans(
        proj, small, B, S, row(jnp.log(lb)), row(jnp.log1p(-lb)), row(1.0 - lb), row(p["hgrn_norm_g"][l]),
        p["ssm_conv_w"][l].astype(F32), row(p["ssm_conv_b"][l]),
        lane_vec(p["ssm_dt_bias"][l]), lane_vec(-jnp.exp(p["ssm_a_log"][l].astype(F32))),
        row(jnp.repeat(p["ssm_d"][l].astype(F32), SSM_HEAD_DIM)), row(p["ssm_norm_g"][l]), cos_t, sin_t)

    w_out = p["w_out"][l]
    wa, wb, wc, wd = (w_out[i * GROUP_W:(i + 1) * GROUP_W] for i in range(4))
    return o_a, o_b, o_c, o_d, wa.astype(BF16), _widen_heads(wb, 0).astype(BF16), wc.astype(BF16), wd.astype(BF16)


def kernel(x, ln1_g, ln1_b, ffn1_w1, ffn1_w3, ffn1_w2, ln2_g, ln2_b, w_in, w_out, hgrn_lb_logits, hgrn_norm_g, nsa_pe_k, nsa_w1_k, nsa_w2_k, nsa_pe_v, nsa_w1_v, nsa_w2_v, nsa_norm_g, rel_bias, ssm_conv_w, ssm_conv_b, ssm_dt_bias, ssm_a_log, ssm_d, ssm_norm_g, ln3_g, ln3_b, ffn2_w1, ffn2_w3, ffn2_w2):
    B, S, D = x.shape
    T = B * S
    depth = w_in.shape[0]
    p = dict(w_in=w_in, w_out=w_out, hgrn_norm_g=hgrn_norm_g, nsa_pe_k=nsa_pe_k, nsa_w1_k=nsa_w1_k,
             nsa_w2_k=nsa_w2_k, nsa_pe_v=nsa_pe_v, nsa_w1_v=nsa_w1_v, nsa_w2_v=nsa_w2_v,
             nsa_norm_g=nsa_norm_g, ssm_conv_w=ssm_conv_w, ssm_conv_b=ssm_conv_b,
             ssm_dt_bias=ssm_dt_bias, ssm_a_log=ssm_a_log, ssm_d=ssm_d, ssm_norm_g=ssm_norm_g)
    cum = jnp.cumsum(jax.nn.softmax(hgrn_lb_logits.astype(F32), axis=0), axis=0)
    lower_bounds = cum - cum[:1]
    rel_t = jnp.zeros((NSA_HEADS, LANES), F32).at[:, :REL_BUCKETS].set(rel_bias.astype(F32).T)
    band = _band_table(rel_t, TQ, BAND_OFF)
    cband = _band_table(rel_t, TQ, CMP_BAND_OFF, NSA_CMP_STRIDE, NSA_CMP_BLOCK - 1)
    bpt = TK // NSA_SLC_BLOCK
    farq = jnp.zeros((NSA_HEADS, LANES), BF16).at[:, bpt:bpt + 3].set(
        jnp.stack(_split3(rel_bias.astype(F32)[REL_BUCKETS - 1]), axis=1))
    farq = jnp.repeat(farq, TQ, axis=0)
    cos_t, sin_t = _rotary_tables(S)
    row = lambda v: v.reshape(1, -1).astype(F32)
    tm = min(512, T)
    tf = 512 if ffn1_w1.shape[2] % 512 == 0 else ffn1_w1.shape[2]
    x2 = x.reshape(T, D).astype(F32)
    for l in range(depth):
        x2 = _ffn(x2, _to_bf16_col_blocks(ffn1_w1, l, tf), _to_bf16_col_blocks(ffn1_w3, l, tf),
                  _to_bf16(ffn1_w2, l), row(ln1_g[l]), row(ln1_b[l]), tm)
        o_a, o_b, o_c, o_d, wa, wb, wc, wd = _mixer(x2, B, S, l, p, lower_bounds, band, cband, farq, cos_t, sin_t)
        x2 = _outproj(x2, o_a, o_b, o_c, o_d, wa, wb, wc, wd, row(ln2_g[l]), row(ln2_b[l]), min(512, T))
        x2 = _ffn(x2, _to_bf16_col_blocks(ffn2_w1, l, tf), _to_bf16_col_blocks(ffn2_w3, l, tf),
                  _to_bf16(ffn2_w2, l), row(ln3_g[l]), row(ln3_b[l]), tm)
    return x2.reshape(B, S, D).astype(x.dtype)
```

```python
import functools
import math

import numpy as np
import jax
import jax.numpy as jnp
from jax import lax
from jax.experimental import pallas as pl
from jax.experimental.pallas import tpu as pltpu

F32 = jnp.float32
BF16 = jnp.bfloat16

DEPTH = 2
GROUP_W = 512
ALPHA = (2 * DEPTH) ** 0.25
HG_HEADS = 4
NSA_HEADS = 8
NSA_KV = 2
NSA_GROUP = 4
NSA_HEAD_DIM = 64
NSA_CMP_STRIDE = 16
NSA_CMP_BLOCK = 32
NSA_SLC_BLOCK = 64
NSA_TOP_N = 16
NSA_WINDOW = 512
NSA_CMP_HIDDEN = 256
SSM_HEADS = 8
SSM_HEAD_DIM = 64
SSM_GROUPS = 2
SSM_STATE = 128
SSM_CONV = 4
RET_HEADS = 4
RET_DK = 128
REL_BUCKETS = 32
REL_EXACT = 16
REL_MAX_DIST = 2048
IN_SIZES = ((GROUP_W,) * 4 + (GROUP_W,) + (128,) * 6 + (24,)
            + (GROUP_W, 1024, SSM_HEADS) + (GROUP_W,) * 4)
IN_SPLITS = tuple(int(v) for v in np.cumsum(IN_SIZES)[:-1])

LANES = 128
SUBLANES = 8
VMEM_LIMIT = 56 * 1024 * 1024
CAST_BLOCK_BYTES = 4 * 1024 * 1024

COL = dict(hq=0, hf=4, hi=8, hg=12, sxbc=16, nq=24, sz=28, rq=32, rk=36, rv=40, rg=44)
NCOL = 48
KV_COLS = ("nkc", "nvc", "nks", "nvs", "nkw", "nvw", "small")
GATE_LANE0 = 0
DT_LANE0 = 24

CHUNK = 128
RET_CHUNK = 256
TQ = 128
TK = 1024
BAND_OFF = 2
CMP_BAND_OFF = 1
CMP_STEP = 256
WIN_SPAN = NSA_WINDOW + 2 * TQ
NEG = -1e30


def _params(sem):
    return pltpu.CompilerParams(dimension_semantics=sem, vmem_limit_bytes=VMEM_LIMIT)


def _dot(a, b):
    return jnp.dot(a, b, preferred_element_type=F32)


def _dot_nt(a, b):
    return lax.dot_general(a, b, (((1,), (1,)), ((), ())), preferred_element_type=F32)


def _split3(x):
    hi = x.astype(BF16)
    r1 = x - hi.astype(F32)
    mid = r1.astype(BF16)
    return hi, mid, (r1 - mid.astype(F32)).astype(BF16)


def _exact_left_dot(w, x):
    n = x.shape[1]
    y = _dot(w, jnp.concatenate(_split3(x), axis=1))
    return y[:, 0:n] + y[:, n:2 * n] + y[:, 2 * n:3 * n]


def _exact_right_dot(x, w):
    n = x.shape[0]
    y = _dot(jnp.concatenate(_split3(x), axis=0), w)
    return y[0:n] + y[n:2 * n] + y[2 * n:3 * n]


def _sigmoid(x):
    return 1.0 / (1.0 + jnp.exp(-x))


def _silu(x):
    return x * _sigmoid(x)


def _softplus(x):
    return jnp.maximum(x, 0.0) + jnp.log1p(jnp.exp(-jnp.abs(x)))


def _layer_norm(r, g, b):
    mu = jnp.mean(r, axis=-1, keepdims=True)
    d = r - mu
    var = jnp.mean(d * d, axis=-1, keepdims=True)
    return d * lax.rsqrt(var + 1e-5) * g + b


def _cast_kernel(x_ref, o_ref):
    o_ref[...] = x_ref[...].astype(o_ref.dtype)


def _to_bf16(w, l):
    _, r, c = w.shape
    tr = min(r, max(16, CAST_BLOCK_BYTES // (4 * c) // 16 * 16))
    while r % tr:
        tr -= 16
    return pl.pallas_call(
        _cast_kernel,
        grid=(r // tr,),
        in_specs=[pl.BlockSpec((None, tr, c), lambda i: (l, i, 0))],
        out_specs=pl.BlockSpec((tr, c), lambda i: (i, 0)),
        out_shape=jax.ShapeDtypeStruct((r, c), BF16),
        compiler_params=_params(("parallel",)),
        name="cast_bf16",
    )(w)


def _to_bf16_col_blocks(w, l, tc):
    _, r, c = w.shape
    return pl.pallas_call(
        _cast_kernel,
        grid=(c // tc,),
        in_specs=[pl.BlockSpec((None, r, tc), lambda j: (l, 0, j))],
        out_specs=pl.BlockSpec((None, r, tc), lambda j: (j, 0, 0)),
        out_shape=jax.ShapeDtypeStruct((c // tc, r, tc), BF16),
        compiler_params=_params(("parallel",)),
        name="cast_bf16_blocked",
    )(w)


def _ffn_kernel(x_ref, w1_ref, w3_ref, w2_ref, g_ref, b_ref, o_ref, acc_ref, xb_ref):
    j = pl.program_id(1)

    @pl.when(j == 0)
    def _():
        xb_ref[...] = x_ref[...].astype(BF16)
        acc_ref[...] = jnp.zeros_like(acc_ref)

    xb = xb_ref[...]
    h1 = _dot(xb, w1_ref[...])
    h3 = _dot(xb, w3_ref[...])
    a = (_silu(h1) * h3).astype(BF16)
    acc_ref[...] += _dot(a, w2_ref[...])

    @pl.when(j == pl.num_programs(1) - 1)
    def _():
        r = ALPHA * x_ref[...] + 0.5 * acc_ref[...]
        o_ref[...] = _layer_norm(r, g_ref[...], b_ref[...])


def _ffn(x, w1, w3, w2, g, b, tm):
    T, D = x.shape
    nf, _, tf = w1.shape
    return pl.pallas_call(
        _ffn_kernel,
        grid=(T // tm, nf),
        in_specs=[
            pl.BlockSpec((tm, D), lambda i, j: (i, 0)),
            pl.BlockSpec((None, D, tf), lambda i, j: (j, 0, 0)),
            pl.BlockSpec((None, D, tf), lambda i, j: (j, 0, 0)),
            pl.BlockSpec((tf, D), lambda i, j: (j, 0)),
            pl.BlockSpec((1, D), lambda i, j: (0, 0)),
            pl.BlockSpec((1, D), lambda i, j: (0, 0)),
        ],
        out_specs=pl.BlockSpec((tm, D), lambda i, j: (i, 0)),
        out_shape=jax.ShapeDtypeStruct((T, D), F32),
        scratch_shapes=[pltpu.VMEM((tm, D), F32), pltpu.VMEM((tm, D), BF16)],
        compiler_params=_params(("parallel", "arbitrary")),
        name="ffn_ln",
    )(x, w1, w3, w2, g, b)


def _proj_kernel(x_ref, w_ref, o_ref, xb_ref):
    @pl.when(pl.program_id(1) == 0)
    def _():
        xb_ref[...] = x_ref[...].astype(BF16)

    o_ref[...] = _dot(xb_ref[...], w_ref[...])


def _proj(x, w, tm):
    T, D = x.shape
    nn, _, tn = w.shape
    return pl.pallas_call(
        _proj_kernel,
        grid=(T // tm, nn),
        in_specs=[pl.BlockSpec((tm, D), lambda i, j: (i, 0)),
                  pl.BlockSpec((None, D, tn), lambda i, j: (j, 0, 0))],
        out_specs=pl.BlockSpec((tm, tn), lambda i, j: (i, j)),
        out_shape=jax.ShapeDtypeStruct((T, nn * tn), F32),
        scratch_shapes=[pltpu.VMEM((tm, D), BF16)],
        compiler_params=_params(("parallel", "arbitrary")),
        name="in_proj",
    )(x, w)


def _kvproj_kernel(x_ref, w_ref, kc_ref, vc_ref, ks_ref, vs_ref, kw_ref, vw_ref, sm_ref):
    tm = x_ref.shape[0]
    y = _dot(x_ref[...].astype(BF16), w_ref[...])
    piece = lambda n: y[:, n * LANES:(n + 1) * LANES]
    kc_ref[...] = piece(0)
    vc_ref[...] = piece(1)
    row = pl.program_id(0) * tm + lax.broadcasted_iota(jnp.int32, (tm, LANES), 0)
    lane = lax.broadcasted_iota(jnp.int32, (tm, LANES), 1)
    bpt = TK // NSA_SLC_BLOCK
    onehot = (((row // NSA_SLC_BLOCK) % bpt == lane) | ((lane >= bpt) & (lane < bpt + 3))).astype(BF16)
    ks_ref[...] = jnp.concatenate([piece(2).astype(BF16), onehot], axis=1)
    vs_ref[...] = jnp.concatenate([piece(3).astype(BF16), jnp.ones((tm, LANES), BF16)], axis=1)
    kw_ref[...] = piece(4).astype(BF16)
    vw_ref[...] = piece(5).astype(BF16)
    sm_ref[...] = piece(6)


def _kvproj(x, w, tm):
    T, D = x.shape
    narrow = lambda dt, width=LANES: (pl.BlockSpec((tm, width), lambda i: (i, 0)),
                                      jax.ShapeDtypeStruct((T, width), dt))
    outs = [narrow(F32), narrow(F32), narrow(BF16, 2 * LANES), narrow(BF16, 2 * LANES),
            narrow(BF16), narrow(BF16), narrow(F32)]
    return pl.pallas_call(
        _kvproj_kernel,
        grid=(T // tm,),
        in_specs=[pl.BlockSpec((tm, D), lambda i: (i, 0)), pl.BlockSpec(w.shape, lambda i: (0, 0))],
        out_specs=[o[0] for o in outs],
        out_shape=[o[1] for o in outs],
        compiler_params=_params(("parallel",)),
        name="kv_proj",
    )(x, w)


def _outproj_kernel(x_ref, oa_ref, ob_ref, oc_ref, od_ref, wa_ref, wb_ref, wc_ref, wd_ref,
                    g_ref, b_ref, o_ref):
    mix = (_dot(oa_ref[...], wa_ref[...]) + _dot(ob_ref[...], wb_ref[...])
           + _dot(oc_ref[...], wc_ref[...]) + _dot(od_ref[...], wd_ref[...]))
    o_ref[...] = _layer_norm(ALPHA * x_ref[...] + mix, g_ref[...], b_ref[...])


def _outproj(x, oa, ob, oc, od, wa, wb, wc, wd, g, b, tm):
    T, D = x.shape
    row = lambda a: pl.BlockSpec((tm, a.shape[1]), lambda i: (i, 0))
    full = lambda a: pl.BlockSpec(a.shape, lambda i: (0, 0))
    return pl.pallas_call(
        _outproj_kernel,
        grid=(T // tm,),
        in_specs=[row(x), row(oa), row(ob), row(oc), row(od),
                  full(wa), full(wb), full(wc), full(wd), full(g), full(b)],
        out_specs=row(x),
        out_shape=jax.ShapeDtypeStruct((T, D), F32),
        compiler_params=_params(("parallel",)),
        name="out_proj_ln",
    )(x, oa, ob, oc, od, wa, wb, wc, wd, g, b)


def _hgrn_tables(C):
    i = np.arange(C)[:, None]
    ip = np.arange(C)[None, :]
    seg = [(ip <= i),
           (ip > i)]
    masks = [np.eye(C, dtype=bool)]
    s = C // 2
    while s >= 1:
        blk = i // s
        if s < SUBLANES:
            seg.append(np.where(blk % 2 == 1, (ip > blk * s) & (ip <= i), (ip > i) & (ip <= (blk + 1) * s)))
        masks.append((blk % 2 == 1) & (ip // s == blk - 1))
        s //= 2
    seg = np.concatenate([x.astype(np.float32) for x in seg], axis=0)
    return seg, np.stack([m.astype(np.float32) for m in masks])


def _hgrn_chunks(q_ref, f_ref, i_ref, g_ref, llb_ref, l1m_ref, oml_ref, ng_ref,
                 seg_ref, msk_ref, o_ref, st_ref):
    for r in range(0, q_ref.shape[1], CHUNK):
        _hgrn_rows(q_ref, f_ref, i_ref, g_ref, llb_ref, l1m_ref, oml_ref, ng_ref, seg_ref, msk_ref,
                   o_ref, st_ref, pl.ds(r, CHUNK))


def _hgrn_rows(q_ref, f_ref, i_ref, g_ref, llb_ref, l1m_ref, oml_ref, ng_ref,
               seg_ref, msk_ref, o_ref, st_ref, rows):
    C = CHUNK
    nlev = msk_ref.shape[0] - 1
    for bi in range(q_ref.shape[0]):
        q = _silu(q_ref[bi, rows])
        z = f_ref[bi, rows]
        log_sig = jnp.minimum(z, 0.0) - jnp.log1p(jnp.exp(-jnp.abs(z)))
        cc = l1m_ref[...] + log_sig
        llb = llb_ref[...]
        logf = jnp.maximum(llb, cc) + jnp.log1p(jnp.exp(-jnp.abs(llb - cc)))
        k = oml_ref[...] * _sigmoid(-z)
        v = i_ref[bi, rows]
        seg = _exact_left_dot(seg_ref[...], logf)
        b_all = seg[0:C]
        lev = []
        s = C // 2
        while s >= SUBLANES:
            b3 = b_all.reshape(C // s, s, GROUP_W)
            start = b3[:, 0:1, :]
            nxt = jnp.concatenate([start[1:], start[-1:]], axis=0)
            odd = lax.broadcasted_iota(jnp.int32, b3.shape, 0) % 2 == 1
            lev.append(jnp.where(odd, b3 - start, nxt - b3).reshape(C, GROUP_W))
            s //= 2
        lev += [seg[r * C:(r + 1) * C] for r in range(2, seg.shape[0] // C)]
        outs = []
        for h in range(HG_HEADS):
            sl = slice(h * LANES, (h + 1) * LANES)
            qh, kh, vh = q[:, sl], k[:, sl], v[:, sl]
            a = msk_ref[0] * _dot_nt(qh.astype(BF16), kh.astype(BF16))
            for l in range(nlev):
                dec = jnp.exp(lev[l][:, sl])
                a = a + msk_ref[1 + l] * _dot_nt((qh * dec).astype(BF16), (kh * dec).astype(BF16))
            b = seg[0:C, sl]
            st = st_ref[bi, h]
            o = _dot(a.astype(BF16), vh.astype(BF16))
            o = o + _dot_nt((qh * jnp.exp(b)).astype(BF16), st.astype(BF16))
            kd = (kh * jnp.exp(seg[C:2 * C, sl])).astype(BF16)
            st_ref[bi, h] = st * jnp.exp(b[C - 1:C, :]) + _dot(vh.T.astype(BF16), kd)
            outs.append(o * lax.rsqrt(jnp.mean(o * o, axis=-1, keepdims=True) + 1e-6))
        o = jnp.concatenate(outs, axis=1)
        o_ref[bi, rows] = (o * ng_ref[...] * _silu(g_ref[bi, rows])).astype(o_ref.dtype)


def _ssd_chunk(z_ref, xbc_ref, sm_ref, cw_ref, cb_ref, dtb_ref, aneg_ref, dsk_ref, ng_ref,
               ex_ref, o_ref, tail_ref, st_ref):
    L = xbc_ref.shape[0]
    x = xbc_ref[...]
    xe = jnp.concatenate([tail_ref[...], x], axis=0)
    cw = cw_ref[...]
    conv = cb_ref[...]
    for kk in range(SSM_CONV):
        conv = conv + cw[kk:kk + 1, :] * xe[5 + kk:5 + kk + L, :]
    tail_ref[...] = x[L - 8:L, :]
    conv = _silu(conv)
    xs = conv[:, 0:GROUP_W]
    bm = conv[:, GROUP_W:GROUP_W + 256]
    cm = conv[:, GROUP_W + 256:GROUP_W + 512]

    dtf = _softplus(sm_ref[...] + dtb_ref[...])
    la = dtf * aneg_ref[...]
    ri = lax.broadcasted_iota(jnp.int32, (L, L), 0)
    ci = lax.broadcasted_iota(jnp.int32, (L, L), 1)
    tri = ri >= ci
    bfull = _exact_left_dot(tri.astype(BF16), la)
    ex = ex_ref[...]
    bexp = _exact_right_dot(bfull, ex)
    dtexp = _exact_right_dot(dtf, ex)
    b_t = bfull.T
    xdt = xs * dtexp
    lane = lax.broadcasted_iota(jnp.int32, (L, LANES), 1)

    scores = []
    for g in range(SSM_GROUPS):
        cg = cm[:, g * SSM_STATE:(g + 1) * SSM_STATE].astype(BF16)
        bg = bm[:, g * SSM_STATE:(g + 1) * SSM_STATE].astype(BF16)
        cb = _dot_nt(cg, bg)
        for hh in range(SSM_HEADS // SSM_GROUPS):
            h = g * (SSM_HEADS // SSM_GROUPS) + hh
            bcol = bfull[:, DT_LANE0 + h:DT_LANE0 + h + 1]
            brow = b_t[DT_LANE0 + h:DT_LANE0 + h + 1, :]
            dec = jnp.exp(jnp.where(tri, bcol - brow, NEG))
            scores.append((cb * dec).astype(BF16))
    y_pairs = []
    for u in range(SSM_HEADS // 2):
        slab = xdt[:, u * LANES:(u + 1) * LANES]
        lo = jnp.where(lane < SSM_HEAD_DIM, slab, 0.0).astype(BF16)
        hi = jnp.where(lane >= SSM_HEAD_DIM, slab, 0.0).astype(BF16)
        y_pairs.append(_dot(scores[2 * u], lo) + _dot(scores[2 * u + 1], hi))
    y_intra = jnp.concatenate(y_pairs, axis=1)

    blast = bexp[L - 1:L, :]
    w = (xdt * jnp.exp(blast - bexp)).astype(BF16)
    y_inter = []
    for g in range(SSM_GROUPS):
        gs = slice(g * 256, (g + 1) * 256)
        cg = cm[:, g * SSM_STATE:(g + 1) * SSM_STATE].astype(BF16)
        st = st_ref[g]
        y_inter.append(_dot(cg, st.astype(BF16)))
        bg_t = bm[:, g * SSM_STATE:(g + 1) * SSM_STATE].T.astype(BF16)
        st_ref[g] = st * jnp.exp(blast[:, gs]) + _dot(bg_t, w[:, gs])
    y = y_intra + jnp.concatenate(y_inter, axis=1) * jnp.exp(bexp) + dsk_ref[...] * xs
    y = y * _silu(z_ref[...])
    halves = []
    for g in range(SSM_GROUPS):
        seg = y[:, g * 256:(g + 1) * 256]
        halves.append(seg * lax.rsqrt(jnp.mean(seg * seg, axis=-1, keepdims=True) + 1e-6))
    o_ref[...] = (jnp.concatenate(halves, axis=1) * ng_ref[...]).astype(o_ref.dtype)


def _scans_kernel(hq_ref, hf_ref, hi_ref, hg_ref, llb_ref, l1m_ref, oml_ref, hng_ref, seg_ref, msk_ref,
                  z_ref, xbc_ref, sm_ref, cw_ref, cb_ref, dtb_ref, aneg_ref, dsk_ref, ng_ref, ex_ref,
                  q_ref, k_ref, v_ref, g_ref, cos_ref, sin_ref, dec_ref, qs_ref, ks_ref, sd_ref,
                  o_hg_ref, o_ssd_ref, o_ret_ref, st_hg_ref, tail_ref, st_ssd_ref, st_ret_ref):
    @pl.when(pl.program_id(0) == 0)
    def _():
        st_hg_ref[...] = jnp.zeros_like(st_hg_ref)
        tail_ref[...] = jnp.zeros_like(tail_ref)
        st_ssd_ref[...] = jnp.zeros_like(st_ssd_ref)
        st_ret_ref[...] = jnp.zeros_like(st_ret_ref)

    _hgrn_chunks(hq_ref, hf_ref, hi_ref, hg_ref, llb_ref, l1m_ref, oml_ref, hng_ref, seg_ref, msk_ref,
                 o_hg_ref, st_hg_ref)

    for b in range(xbc_ref.shape[0]):
        for r in range(0, xbc_ref.shape[1], CHUNK):
            rows = pl.ds(r, CHUNK)
            _ssd_chunk(z_ref.at[b, rows], xbc_ref.at[b, rows], sm_ref.at[b, rows], cw_ref, cb_ref, dtb_ref,
                       aneg_ref, dsk_ref, ng_ref, ex_ref, o_ssd_ref.at[b, rows], tail_ref.at[b],
                       st_ssd_ref.at[b])
    _ret_step(q_ref, k_ref, v_ref, g_ref, cos_ref, sin_ref, dec_ref, qs_ref, ks_ref, sd_ref,
              o_ret_ref, st_ret_ref)


def _ret_step(q_ref, k_ref, v_ref, g_ref, cos_ref, sin_ref, dec_ref, qs_ref, ks_ref, sd_ref,
              o_ref, st_ref):
    cos = cos_ref[...]
    sin = sin_ref[...]
    for b in range(q_ref.shape[0]):
        outs = []
        for h in range(RET_HEADS):
            sl = slice(h * LANES, (h + 1) * LANES)
            qh = q_ref[b, :, sl]
            kh = k_ref[b, :, sl]
            qh = qh * cos + pltpu.roll(qh, RET_DK // 2, axis=1) * sin
            kh = (kh * cos + pltpu.roll(kh, RET_DK // 2, axis=1) * sin) * (RET_DK ** -0.5)
            vh = v_ref[b, :, sl].astype(BF16)
            sc = (_dot_nt(qh.astype(BF16), kh.astype(BF16)) * dec_ref[h]).astype(BF16)
            st = st_ref[b, h]
            y = _dot(sc, vh) + _dot((qh * qs_ref[:, sl]).astype(BF16), st.astype(BF16))
            kd_t = (kh * ks_ref[:, sl]).T.astype(BF16)
            st_ref[b, h] = st * sd_ref[h] + _dot(kd_t, vh)
            mu = jnp.mean(y, axis=-1, keepdims=True)
            d = y - mu
            outs.append(d * lax.rsqrt(jnp.mean(d * d, axis=-1, keepdims=True) + 1e-5))
        o_ref[b] = (_silu(g_ref[b]) * jnp.concatenate(outs, axis=1)).astype(o_ref.dtype)


def _scans(proj, small, B, S, llb, l1m, oml, hng, cw, cb, dtb, aneg, dsk, ng, cos_t, sin_t):
    L = min(RET_CHUNK, S)
    nc = S // L
    seg, msk = _hgrn_tables(CHUNK)
    seg, msk = jnp.asarray(seg, BF16), jnp.asarray(msk)
    vec = pl.BlockSpec((1, GROUP_W), lambda c: (0, 0))
    ex = np.zeros((LANES, GROUP_W), np.float32)
    for h in range(SSM_HEADS):
        ex[DT_LANE0 + h, h * SSM_HEAD_DIM:(h + 1) * SSM_HEAD_DIM] = 1.0
    ex = jnp.asarray(ex, BF16)
    lg = jnp.log(1.0 - 2.0 ** (-5.0 - jnp.arange(RET_HEADS, dtype=F32)))
    i = jnp.arange(L, dtype=F32)
    diff = i[:, None] - i[None, :]
    dec = jnp.where(diff >= 0, jnp.exp(lg[:, None, None] * jnp.maximum(diff, 0.0)), 0.0)
    rep = lambda t: jnp.repeat(t, LANES, axis=1)
    qs = rep(jnp.exp((i[:, None] + 1.0) * lg[None, :]))
    ks = rep(jnp.exp((L - 1.0 - i[:, None]) * lg[None, :]))
    sd = jnp.broadcast_to(jnp.exp(L * lg)[:, None, None], (RET_HEADS, LANES, LANES))
    proj3 = proj.reshape(B, S, proj.shape[1])
    full2 = lambda a: pl.BlockSpec(a.shape, lambda c: (0, 0))
    col = lambda name: pl.BlockSpec((B, L, GROUP_W), lambda c, n=COL[name] // 4: (0, c, n))
    out = pl.BlockSpec((B, L, GROUP_W), lambda c: (0, c, 0))
    o_a, o_c, o_d = pl.pallas_call(
        _scans_kernel,
        grid=(nc,),
        in_specs=[col("hq"), col("hf"), col("hi"), col("hg"), vec, vec, vec, vec,
                  full2(seg), pl.BlockSpec(msk.shape, lambda c: (0, 0, 0)),
                  col("sz"),
                  pl.BlockSpec((B, L, 1024), lambda c: (0, c, COL["sxbc"] // 8)),
                  pl.BlockSpec((B, L, LANES), lambda c: (0, c, 0)),
                  full2(cw), full2(cb), full2(dtb), full2(aneg), full2(dsk), full2(ng), full2(ex),
                  col("rq"), col("rk"), col("rv"), col("rg"),
                  pl.BlockSpec((L, LANES), lambda c: (c, 0)),
                  pl.BlockSpec((L, LANES), lambda c: (c, 0)),
                  pl.BlockSpec((RET_HEADS, L, L), lambda c: (0, 0, 0)),
                  pl.BlockSpec((L, GROUP_W), lambda c: (0, 0)),
                  pl.BlockSpec((L, GROUP_W), lambda c: (0, 0)),
                  pl.BlockSpec((RET_HEADS, LANES, LANES), lambda c: (0, 0, 0))],
        out_specs=[out, out, out],
        out_shape=[jax.ShapeDtypeStruct((B, S, GROUP_W), BF16)] * 3,
        scratch_shapes=[pltpu.VMEM((B, HG_HEADS, LANES, LANES), F32),
                        pltpu.VMEM((B, 8, 1024), F32), pltpu.VMEM((B, SSM_GROUPS, SSM_STATE, 256), F32),
                        pltpu.VMEM((B, RET_HEADS, RET_DK, RET_DK), F32)],
        compiler_params=_params(("arbitrary",)),
        name="recurrent_mixers",
    )(proj3, proj3, proj3, proj3, llb, l1m, oml, hng, seg, msk,
      proj3, proj3, small.reshape(B, S, LANES), cw, cb, dtb, aneg, dsk, ng, ex,
      proj3, proj3, proj3, proj3, cos_t, sin_t, dec, qs, ks, sd)
    return tuple(o.reshape(B * S, GROUP_W) for o in (o_a, o_c, o_d))


def _t5_bucket(dist):
    n = jnp.maximum(dist, 0)
    nf = jnp.maximum(n, 1).astype(F32)
    large = REL_EXACT + (jnp.log(nf / REL_EXACT) / math.log(REL_MAX_DIST / REL_EXACT)
                         * (REL_BUCKETS - REL_EXACT)).astype(jnp.int32)
    return jnp.where(n < REL_EXACT, n, jnp.minimum(large, REL_BUCKETS - 1))


def _head_bias(bucket, rel_ref):
    rows, cols = bucket.shape
    per_head = []
    for h in range(NSA_HEADS):
        tbl = jnp.broadcast_to(rel_ref[h:h + 1, :], (rows, LANES))
        chunks = [jnp.take_along_axis(tbl, bucket[:, c:c + LANES], axis=1)
                  for c in range(0, cols, LANES)]
        per_head.append(chunks[0] if len(chunks) == 1 else jnp.concatenate(chunks, axis=1))
    return jnp.stack(per_head, axis=0)


def _stack_heads(q):
    lane = lax.broadcasted_iota(jnp.int32, (q.shape[0], LANES), 1)
    rows = []
    for h in range(NSA_HEADS):
        slab = q[:, (h // 2) * LANES:(h // 2 + 1) * LANES]
        src_half, dst_half = h % 2, h // NSA_GROUP
        if src_half != dst_half:
            slab = pltpu.roll(slab, NSA_HEAD_DIM, axis=1)
        rows.append(jnp.where(lane // NSA_HEAD_DIM == dst_half, slab, 0.0))
    return jnp.concatenate(rows, axis=0).astype(BF16)


def _cmp_kernel(g_ref, pe_ref, w1a_ref, w1b_ref, w2_ref, o_ref):
    nb = g_ref.shape[0]
    gw = g_ref.shape[1] // 4
    pe = pe_ref[...]
    slabs = [g_ref[:, s * gw:(s + 1) * gw] for s in range(4)]
    nxt0 = pltpu.roll(slabs[0], nb - 1, axis=0)
    for s in range(4):
        a = (slabs[s] + pe[0:1, :]).astype(BF16)
        bn = ((slabs[s + 1] if s < 3 else nxt0) + pe[1:2, :]).astype(BF16)
        hid = _silu(_dot(a, w1a_ref[...]) + _dot(bn, w1b_ref[...]))
        o_ref[:, s * LANES:(s + 1) * LANES] = _dot(hid.astype(BF16), w2_ref[...]).astype(o_ref.dtype)


def _cmp_kv_kernel(gk_ref, gv_ref, pek_ref, w1ak_ref, w1bk_ref, w2k_ref,
                   pev_ref, w1av_ref, w1bv_ref, w2v_ref, ok_ref, ov_ref):
    _cmp_kernel(gk_ref, pek_ref, w1ak_ref, w1bk_ref, w2k_ref, ok_ref)
    _cmp_kernel(gv_ref, pev_ref, w1av_ref, w1bv_ref, w2v_ref, ov_ref)


def _compress(gk, gv, wk, wv):
    B, nb, gw4 = gk.shape
    full2 = lambda a: pl.BlockSpec(a.shape, lambda b: (0, 0), pipeline_mode=pl.Buffered(1))
    g_spec = pl.BlockSpec((None, nb, gw4), lambda b: (b, 0, 0))
    o_spec = pl.BlockSpec((None, nb, 4 * LANES), lambda b: (b, 0, 0))
    o_shape = jax.ShapeDtypeStruct((B, nb, 4 * LANES), BF16)
    kcmp, vcmp = pl.pallas_call(
        _cmp_kv_kernel,
        grid=(B,),
        in_specs=[g_spec, g_spec] + [full2(a) for a in wk] + [full2(a) for a in wv],
        out_specs=[o_spec, o_spec],
        out_shape=[o_shape, o_shape],
        compiler_params=_params(("parallel",)),
        name="nsa_compress",
    )(gk, gv, *wk, *wv)
    return kcmp.reshape(B, 4 * nb, LANES), vcmp.reshape(B, 4 * nb, LANES)


def _cmpattn_kernel(q_ref, kc_ref, vc_ref, band_ref, st_ref, oc_ref, selb_ref):
    tq = q_ref.shape[1]
    ncmp = kc_ref.shape[1]
    nb = ncmp // 4
    R = NSA_HEADS * tq
    q0 = pl.program_id(0) * tq
    nd = band_ref.shape[0]

    def attend(ncv, nbv):
        for b in range(q_ref.shape[0]):
            attend_one(b, ncv, nbv)

    def attend_one(b, ncv, nbv):
        Q = _stack_heads(q_ref[b])
        s3 = _dot_nt(Q, kc_ref[b, 0:ncv, :]).reshape(NSA_HEADS, tq, ncv)
        parts = []
        for ch in range(ncv // LANES):
            d = jnp.clip((q0 - ch * LANES * NSA_CMP_STRIDE) // tq + CMP_BAND_OFF, 0, nd - 1)
            parts.append(s3[:, :, ch * LANES:(ch + 1) * LANES] + band_ref[d])
        s3 = jnp.concatenate(parts, axis=2)
        mx = jnp.max(s3, axis=-1, keepdims=True)
        e = jnp.exp(s3 - mx)
        live = (q0 + lax.broadcasted_iota(jnp.int32, (tq, 1), 0) >= NSA_CMP_BLOCK - 1)[None]
        p = e * jnp.where(live, 1.0 / jnp.sum(e, axis=-1, keepdims=True), 0.0)
        oc_ref[b] = _dot(p.reshape(R, ncv).astype(BF16), vc_ref[b, 0:ncv, :])

        ps = p.reshape(NSA_KV, NSA_GROUP, tq, ncv).sum(axis=1).reshape(NSA_KV * tq, ncv)
        imp = _exact_right_dot(ps, st_ref[0:ncv, 0:nbv])
        j = lax.broadcasted_iota(jnp.int32, (NSA_KV * tq, nbv), 1)
        t = q0 + (lax.broadcasted_iota(jnp.int32, (NSA_KV * tq, nbv), 0) % tq)
        cur = t // NSA_SLC_BLOCK
        forced = (j == 0) | (j == cur) | (j == cur - 1)
        score = jnp.where(j > cur, -1.0, jnp.where(forced, NSA_GROUP + 1.0, imp))
        score = score.T
        jf = lax.broadcasted_iota(jnp.int32, score.shape, 0).astype(F32)
        sel = score == NSA_GROUP + 1.0
        score = jnp.where(sel, -jnp.inf, score)
        for _ in range(min(NSA_TOP_N, nb) - 3):
            best = jnp.max(score, axis=0, keepdims=True)
            first = jnp.min(jnp.where(score == best, jf, float(nbv)), axis=0, keepdims=True)
            hit = jf == first
            sel = sel | hit
            score = jnp.where(hit, -jnp.inf, score)
        selb = jnp.where(sel, 0.0, NEG).T.astype(selb_ref.dtype)
        for kv in range(NSA_KV):
            selb_ref[b, :, kv * nb:kv * nb + nbv] = selb[kv * tq:(kv + 1) * tq]
            if nbv < nb:
                selb_ref[b, :, kv * nb + nbv:(kv + 1) * nb] = jnp.full((tq, nb - nbv), NEG, selb_ref.dtype)

    if ncmp % CMP_STEP:
        attend(ncmp, nb)
    else:
        nvar = ncmp // CMP_STEP
        sizes = [(CMP_STEP * v, min(nb, LANES * ((v + 1) // 2))) for v in range(1, nvar + 1)]
        last_key = (q0 + tq - 1) // NSA_CMP_STRIDE
        lax.switch(jnp.minimum(last_key // CMP_STEP, nvar - 1),
                   [functools.partial(attend, ncv, nbv) for ncv, nbv in sizes])


def _cmpattn(proj, kcmp, vcmp, cband, B, S):
    tq = TQ
    nqt = S // tq
    ncmp = kcmp.shape[1]
    nb = ncmp // 4
    off = np.arange(ncmp)[:, None] - 4 * np.arange(nb)[None, :]
    stencil = np.where((off >= 0) & (off <= 2), 1.0, np.where((off == -1) | (off == 3), 0.5, 0.0))
    stencil = jnp.asarray(stencil, BF16)
    once = pl.Buffered(1)
    R = NSA_HEADS * tq
    o_cmp, selb = pl.pallas_call(
        _cmpattn_kernel,
        grid=(nqt,),
        in_specs=[pl.BlockSpec((B, tq, GROUP_W), lambda i: (0, i, COL["nq"] // 4)),
                  pl.BlockSpec((B, ncmp, LANES), lambda i: (0, 0, 0), pipeline_mode=once),
                  pl.BlockSpec((B, ncmp, LANES), lambda i: (0, 0, 0), pipeline_mode=once),
                  pl.BlockSpec(cband.shape, lambda i: (0, 0, 0, 0), pipeline_mode=once),
                  pl.BlockSpec(stencil.shape, lambda i: (0, 0), pipeline_mode=once)],
        out_specs=[pl.BlockSpec((B, R, LANES), lambda i: (0, i, 0)),
                   pl.BlockSpec((B, tq, 2 * nb), lambda i: (0, i, 0))],
        out_shape=[jax.ShapeDtypeStruct((B, S * NSA_HEADS, LANES), F32),
                   jax.ShapeDtypeStruct((B, S, 2 * nb), BF16)],
        compiler_params=_params(("arbitrary",)),
        name="nsa_cmp_attn_topk",
    )(proj.reshape(B, S, proj.shape[1]), kcmp, vcmp, cband, stencil)
    return o_cmp.reshape(B * S * NSA_HEADS, LANES), selb.reshape(B * S, 2 * nb)


def _band_kernel(rel_ref, o_ref, *, entry_off, key_step, key_end):
    tq = o_ref.shape[1]
    delta = (pl.program_id(0) - entry_off) * tq
    row = lax.broadcasted_iota(jnp.int32, (tq, LANES), 0)
    col = lax.broadcasted_iota(jnp.int32, (tq, LANES), 1)
    dist = delta + row - (col * key_step + key_end)
    bias = _head_bias(_t5_bucket(dist), rel_ref)
    o_ref[...] = jnp.where((dist >= 0)[None], bias, NEG)


def _band_table(rel_t, tq, entry_off, key_step=1, key_end=0):
    nd = -(-(REL_MAX_DIST + key_step * (LANES - 1) + key_end) // tq) + entry_off + 1
    return pl.pallas_call(
        functools.partial(_band_kernel, entry_off=entry_off, key_step=key_step, key_end=key_end),
        grid=(nd,),
        in_specs=[pl.BlockSpec(rel_t.shape, lambda d: (0, 0))],
        out_specs=pl.BlockSpec((None, NSA_HEADS, tq, LANES), lambda d: (d, 0, 0, 0)),
        out_shape=jax.ShapeDtypeStruct((nd, NSA_HEADS, tq, LANES), F32),
        compiler_params=_params(("parallel",)),
        name="nsa_bias_band",
    )(rel_t)


def _selattn_kernel(q_ref, selb_ref, ks_ref, vs_ref, band_ref, sp_ref, farq_ref, os_ref,
                    qaug_ref, msel_ref, s_ref, mt_ref, m_ref, acc_ref):
    tq = q_ref.shape[0]
    R = NSA_HEADS * tq
    nb = selb_ref.shape[1] // 2
    n_kt = msel_ref.shape[0]
    tk = ks_ref.shape[0] // n_kt
    nd = band_ref.shape[0]
    q0 = pl.program_id(1) * tq
    qaug_ref[:, 0:LANES] = _stack_heads(q_ref[...])
    selb2 = jnp.concatenate([selb_ref[:, 0:nb], selb_ref[:, nb:2 * nb]], axis=0)
    spread = _dot(selb2, sp_ref[...])
    for c in range(n_kt):
        msel_ref[c] = spread[:, c * LANES:(c + 1) * LANES].astype(BF16)
    m_ref[...] = jnp.full(m_ref.shape, NEG, F32)
    acc_ref[...] = jnp.zeros_like(acc_ref)
    n_tiles = (q0 + tq - 1) // tk + 1

    far_pairs = (jnp.maximum(q0 - (REL_MAX_DIST - 1), 0) // tk) // 2

    def scores(c, slot, far):
        c = jnp.minimum(c, n_kt - 1)
        k0 = pl.multiple_of(c * tk, tk)
        m2 = msel_ref[c]
        flags = jnp.concatenate([m2[0:tq]] * NSA_GROUP + [m2[tq:2 * tq]] * NSA_GROUP, axis=0)
        qaug_ref[:, LANES:2 * LANES] = flags + farq_ref[...] if far else flags
        s = _dot_nt(qaug_ref[...], ks_ref[pl.ds(k0, tk), :])
        if not far:
            s3 = s.reshape(NSA_HEADS, tq, tk)
            parts = []
            for ch in range(tk // LANES):
                d = jnp.clip((q0 - k0) // tq - ch * (LANES // tq) + BAND_OFF, 0, nd - 1)
                parts.append(s3[:, :, ch * LANES:(ch + 1) * LANES] + band_ref[d])
            s = jnp.concatenate(parts, axis=2).reshape(R, tk)
        s_ref[slot] = s
        mt_ref[slot] = jnp.broadcast_to(jnp.max(s, axis=-1, keepdims=True), (R, LANES))

    def accumulate(c, slot):
        m_old = m_ref[...]
        m_new = jnp.maximum(m_old, mt_ref[slot])
        alpha = jnp.exp(m_old - m_new)
        p = jnp.exp(s_ref[slot] - jnp.tile(m_new, (1, tk // LANES)))
        vt = vs_ref[pl.ds(pl.multiple_of(c * tk, tk), tk), :]
        acc_ref[...] = jnp.tile(alpha, (1, 2)) * acc_ref[...] + _dot(p.astype(BF16), vt)
        m_ref[...] = m_new

    def pair(i, far_odd, far_even):
        c = 2 * i
        scores(c + 1, 1, far_odd)
        accumulate(c, 0)
        scores(c + 2, 0, far_even)
        accumulate(c + 1, 1)

    scores(0, 0, False)
    last_far = jnp.maximum(far_pairs - 1, 0)
    lax.fori_loop(0, last_far, lambda i, carry: pair(i, True, True), None)
    lax.fori_loop(last_far, far_pairs, lambda i, carry: pair(i, True, False), None)
    lax.fori_loop(far_pairs, n_tiles // 2, lambda i, carry: pair(i, False, False), None)

    @pl.when(n_tiles % 2 == 1)
    def _():
        accumulate(n_tiles - 1, 0)

    acc = acc_ref[...]
    os_ref[...] = acc[:, 0:LANES] / acc[:, LANES:2 * LANES]


def _selattn(proj, selb, ks, vs, band, farq, B, S):
    tq = TQ
    tk = min(TK, S)
    n_kt = S // tk
    bpt = tk // NSA_SLC_BLOCK
    nqt = S // tq
    nb = S // NSA_SLC_BLOCK
    R = NSA_HEADS * tq
    sp = np.zeros((nb, n_kt * LANES), np.float32)
    sp[np.arange(nb), (np.arange(nb) // bpt) * LANES + np.arange(nb) % bpt] = 1.0
    once = pl.Buffered(1)
    return pl.pallas_call(
        _selattn_kernel,
        grid=(B, nqt),
        in_specs=[pl.BlockSpec((tq, GROUP_W), lambda b, i: (b * nqt + i, COL["nq"] // 4)),
                  pl.BlockSpec((tq, 2 * nb), lambda b, i: (b * nqt + i, 0)),
                  pl.BlockSpec((None, S, 2 * LANES), lambda b, i: (b, 0, 0), pipeline_mode=once),
                  pl.BlockSpec((None, S, 2 * LANES), lambda b, i: (b, 0, 0), pipeline_mode=once),
                  pl.BlockSpec(band.shape, lambda b, i: (0, 0, 0, 0), pipeline_mode=once),
                  pl.BlockSpec(sp.shape, lambda b, i: (0, 0)),
                  pl.BlockSpec(farq.shape, lambda b, i: (0, 0))],
        out_specs=pl.BlockSpec((R, LANES), lambda b, i: (b * nqt + i, 0)),
        out_shape=jax.ShapeDtypeStruct((B * S * NSA_HEADS, LANES), F32),
        scratch_shapes=[pltpu.VMEM((R, 2 * LANES), BF16), pltpu.VMEM((n_kt, 2 * tq, LANES), BF16),
                        pltpu.VMEM((2, R, tk), F32), pltpu.VMEM((2, R, LANES), F32),
                        pltpu.VMEM((R, LANES), F32), pltpu.VMEM((R, 2 * LANES), F32)],
        compiler_params=_params(("parallel", "arbitrary")),
        name="nsa_sel_attn",
    )(proj, selb, ks, vs, band, jnp.asarray(sp, BF16), farq)


def _winattn_kernel(q_ref, sm_ref, oc_ref, os_ref, kw_ref, vw_ref, band_ref, ng_ref, o_ref):
    tq = q_ref.shape[1]
    R = NSA_HEADS * tq
    S = kw_ref.shape[1]
    span = min(WIN_SPAN, S)
    nd = band_ref.shape[0]
    q0 = pl.program_id(0) * tq
    start = pl.multiple_of(jnp.clip(q0 + tq - span, 0, S - span), tq)
    row = lax.broadcasted_iota(jnp.int32, (tq, span), 0)
    col = lax.broadcasted_iota(jnp.int32, (tq, span), 1)
    in_window = ((q0 + row) - (start + col) < NSA_WINDOW)[None]
    lane = lax.broadcasted_iota(jnp.int32, (tq, LANES), 1)
    for b in range(q_ref.shape[0]):
        Q = _stack_heads(q_ref[b])
        kt = kw_ref[b, pl.ds(start, span), :]
        vt = vw_ref[b, pl.ds(start, span), :]
        s3 = _dot_nt(Q, kt).reshape(NSA_HEADS, tq, span)
        parts = []
        for ch in range(span // LANES):
            d = jnp.clip((q0 - start) // tq - ch * (LANES // tq) + BAND_OFF, 0, nd - 1)
            part = s3[:, :, ch * LANES:(ch + 1) * LANES] + band_ref[d]
            if (span - tq) + tq - 1 - ch * LANES >= NSA_WINDOW:
                part = jnp.where(in_window[:, :, ch * LANES:(ch + 1) * LANES], part, NEG)
            parts.append(part)
        s3 = jnp.concatenate(parts, axis=2)
        mx = jnp.max(s3, axis=-1, keepdims=True)
        e = jnp.exp(s3 - mx)
        inv = (1.0 / jnp.sum(e, axis=-1, keepdims=True)).reshape(R, 1)
        ow = _dot(e.reshape(R, span).astype(BF16), vt) * inv

        gates = _sigmoid(sm_ref[b])
        heads = []
        ssq = jnp.zeros((tq, 1), F32)
        for h in range(NSA_HEADS):
            rs = slice(h * tq, (h + 1) * tq)
            g = [gates[:, GATE_LANE0 + 3 * h + br:GATE_LANE0 + 3 * h + br + 1] for br in range(3)]
            oh = g[0] * oc_ref[b, rs, :] + g[1] * os_ref[b, rs, :] + g[2] * ow[rs, :]
            kv = h // NSA_GROUP
            valid = (lane >= kv * NSA_HEAD_DIM) & (lane < (kv + 1) * NSA_HEAD_DIM)
            oh = jnp.where(valid, oh, 0.0)
            ssq = ssq + jnp.sum(oh * oh, axis=-1, keepdims=True)
            heads.append(oh)
        rinv = lax.rsqrt(ssq / GROUP_W + 1e-6)
        o_ref[b] = (jnp.concatenate(heads, axis=1) * rinv * ng_ref[...]).astype(o_ref.dtype)


def _winattn(proj, small, oc, os_, kw, vw, band, ngw, B, S):
    tq = TQ
    R = NSA_HEADS * tq
    once = pl.Buffered(1)
    per_b = lambda t: t.reshape(B, t.shape[0] // B, t.shape[1])
    return pl.pallas_call(
        _winattn_kernel,
        grid=(S // tq,),
        in_specs=[pl.BlockSpec((B, tq, GROUP_W), lambda i: (0, i, COL["nq"] // 4)),
                  pl.BlockSpec((B, tq, LANES), lambda i: (0, i, 0)),
                  pl.BlockSpec((B, R, LANES), lambda i: (0, i, 0)),
                  pl.BlockSpec((B, R, LANES), lambda i: (0, i, 0)),
                  pl.BlockSpec((B, S, LANES), lambda i: (0, 0, 0), pipeline_mode=once),
                  pl.BlockSpec((B, S, LANES), lambda i: (0, 0, 0), pipeline_mode=once),
                  pl.BlockSpec(band.shape, lambda i: (0, 0, 0, 0), pipeline_mode=once),
                  pl.BlockSpec(ngw.shape, lambda i: (0, 0))],
        out_specs=pl.BlockSpec((B, tq, NSA_HEADS * LANES), lambda i: (0, i, 0)),
        out_shape=jax.ShapeDtypeStruct((B, S, NSA_HEADS * LANES), BF16),
        compiler_params=_params(("arbitrary",)),
        name="nsa_win_attn_merge",
    )(per_b(proj), per_b(small), per_b(oc), per_b(os_), kw, vw, band, ngw).reshape(B * S, NSA_HEADS * LANES)


def _widen_heads(x, axis):
    x = jnp.moveaxis(x, axis, -1)
    lead = x.shape[:-1]
    x = x.reshape(*lead, NSA_KV, NSA_GROUP, 1, NSA_HEAD_DIM)
    sel = jnp.eye(NSA_KV, dtype=x.dtype).reshape(NSA_KV, 1, NSA_KV, 1)
    x = (x * sel).reshape(*lead, NSA_HEADS * LANES)
    return jnp.moveaxis(x, -1, axis)


def _build_w_in(w):
    (hq, hf, hi, hg, nq, nkc, nvc, nks, nvs, nkw, nvw, ngate,
     sz, sxbc, sdt, rq, rk, rv, rg) = jnp.split(w, IN_SPLITS, axis=1)
    D = w.shape[0]
    nq = nq * NSA_HEAD_DIM ** -0.5
    deint = lambda t: t.reshape(D, RET_HEADS, RET_DK // 2, 2).transpose(0, 1, 3, 2).reshape(D, GROUP_W)
    small = jnp.concatenate([ngate, sdt, jnp.zeros((D, LANES - 32), w.dtype)], axis=1)
    wide = [hq, hf, hi, hg, sxbc, nq, sz, deint(rq), deint(rk), rv, rg]
    narrow = [nkc, nvc, nks, nvs, nkw, nvw, small]
    return jnp.concatenate(wide, axis=1).astype(BF16), jnp.concatenate(narrow, axis=1).astype(BF16)


def _build_cmp_weights(pe, w1, w2):
    w1r = w1.reshape(2, NSA_CMP_STRIDE, NSA_HEAD_DIM, NSA_CMP_HIDDEN)
    eye = jnp.eye(NSA_KV, dtype=w1.dtype)
    big = jnp.einsum("ardc,kj->arkdjc", w1r, eye).reshape(
        2, NSA_CMP_STRIDE * NSA_KV * NSA_HEAD_DIM, NSA_KV * NSA_CMP_HIDDEN)
    w2bd = jnp.einsum("cd,kj->kcjd", w2, eye).reshape(NSA_KV * NSA_CMP_HIDDEN, NSA_KV * NSA_HEAD_DIM)
    per = pe.reshape(2, NSA_CMP_STRIDE, 1, NSA_HEAD_DIM)
    pe2 = jnp.broadcast_to(per, (2, NSA_CMP_STRIDE, NSA_KV, NSA_HEAD_DIM)).reshape(2, -1)
    return pe2, big[0].astype(BF16), big[1].astype(BF16), w2bd.astype(BF16)


def _rotary_tables(S):
    half = RET_DK // 2
    theta = 1.0 / (10000.0 ** jnp.linspace(0.0, 1.0, half, dtype=F32))
    ang = jnp.arange(S, dtype=F32)[:, None] * theta[None, :]
    cos, sin = jnp.cos(ang), jnp.sin(ang)
    return jnp.concatenate([cos, cos], axis=1), jnp.concatenate([-sin, sin], axis=1)


def _mixer(x2, B, S, l, p, lower_bounds, band, cband, farq, cos_t, sin_t):
    T = B * S
    w_wide, w_narrow = _build_w_in(p["w_in"][l])
    tn = NCOL * LANES // 4
    proj = _proj(x2, w_wide.reshape(-1, 4, tn).transpose(1, 0, 2), tm=min(1024, T))
    kc, vc, ks, vs, kw, vw, small = _kvproj(x2, w_narrow, tm=min(512, T))
    row = lambda v: v.reshape(1, -1).astype(F32)

    nb = S // NSA_SLC_BLOCK
    grp = lambda t: t.reshape(B, nb, 4 * NSA_CMP_STRIDE * LANES)
    kcmp, vcmp = _compress(
        grp(kc), grp(vc),
        _build_cmp_weights(p["nsa_pe_k"][l], p["nsa_w1_k"][l], p["nsa_w2_k"][l]),
        _build_cmp_weights(p["nsa_pe_v"][l], p["nsa_w1_v"][l], p["nsa_w2_v"][l]))
    o_cmp, selb = _cmpattn(proj, kcmp, vcmp, cband, B, S)
    seq = lambda t: t.reshape(B, S, t.shape[1])
    o_sel = _selattn(proj, selb, seq(ks), seq(vs), band, farq, B, S)
    ngw = _widen_heads(p["nsa_norm_g"][l].astype(F32), 0).reshape(1, -1)
    o_b = _winattn(proj, small, o_cmp, o_sel, seq(kw), seq(vw), band, ngw, B, S)

    lb = lower_bounds[l].astype(F32)
    lane_vec = lambda v: jnp.zeros((1, LANES), F32).at[0, DT_LANE0:DT_LANE0 + SSM_HEADS].set(v.astype(F32))
    o_a, o_c, o_d = _scans(
        proj, small, B, S, row(jnp.log(lb)), row(jnp.log1p(-lb)), row(1.0 - lb), row(p["hgrn_norm_g"][l]),
        p["ssm_conv_w"][l].astype(F32), row(p["ssm_conv_b"][l]),
        lane_vec(p["ssm_dt_bias"][l]), lane_vec(-jnp.exp(p["ssm_a_log"][l].astype(F32))),
        row(jnp.repeat(p["ssm_d"][l].astype(F32), SSM_HEAD_DIM)), row(p["ssm_norm_g"][l]), cos_t, sin_t)

    w_out = p["w_out"][l]
    wa, wb, wc, wd = (w_out[i * GROUP_W:(i + 1) * GROUP_W] for i in range(4))
    return o_a, o_b, o_c, o_d, wa.astype(BF16), _widen_heads(wb, 0).astype(BF16), wc.astype(BF16), wd.astype(BF16)


def kernel(x, ln1_g, ln1_b, ffn1_w1, ffn1_w3, ffn1_w2, ln2_g, ln2_b, w_in, w_out, hgrn_lb_logits, hgrn_norm_g, nsa_pe_k, nsa_w1_k, nsa_w2_k, nsa_pe_v, nsa_w1_v, nsa_w2_v, nsa_norm_g, rel_bias, ssm_conv_w, ssm_conv_b, ssm_dt_bias, ssm_a_log, ssm_d, ssm_norm_g, ln3_g, ln3_b, ffn2_w1, ffn2_w3, ffn2_w2):
    B, S, D = x.shape
    T = B * S
    depth = w_in.shape[0]
    p = dict(w_in=w_in, w_out=w_out, hgrn_norm_g=hgrn_norm_g, nsa_pe_k=nsa_pe_k, nsa_w1_k=nsa_w1_k,
             nsa_w2_k=nsa_w2_k, nsa_pe_v=nsa_pe_v, nsa_w1_v=nsa_w1_v, nsa_w2_v=nsa_w2_v,
             nsa_norm_g=nsa_norm_g, ssm_conv_w=ssm_conv_w, ssm_conv_b=ssm_conv_b,
             ssm_dt_bias=ssm_dt_bias, ssm_a_log=ssm_a_log, ssm_d=ssm_d, ssm_norm_g=ssm_norm_g)
    cum = jnp.cumsum(jax.nn.softmax(hgrn_lb_logits.astype(F32), axis=0), axis=0)
    lower_bounds = cum - cum[:1]
    rel_t = jnp.zeros((NSA_HEADS, LANES), F32).at[:, :REL_BUCKETS].set(rel_bias.astype(F32).T)
    band = _band_table(rel_t, TQ, BAND_OFF)
    cband = _band_table(rel_t, TQ, CMP_BAND_OFF, NSA_CMP_STRIDE, NSA_CMP_BLOCK - 1)
    bpt = TK // NSA_SLC_BLOCK
    farq = jnp.zeros((NSA_HEADS, LANES), BF16).at[:, bpt:bpt + 3].set(
        jnp.stack(_split3(rel_bias.astype(F32)[REL_BUCKETS - 1]), axis=1))
    farq = jnp.repeat(farq, TQ, axis=0)
    cos_t, sin_t = _rotary_tables(S)
    row = lambda v: v.reshape(1, -1).astype(F32)
    tm = min(512, T)
    tf = 512 if ffn1_w1.shape[2] % 512 == 0 else ffn1_w1.shape[2]
    x2 = x.reshape(T, D).astype(F32)
    for l in range(depth):
        x2 = _ffn(x2, _to_bf16_col_blocks(ffn1_w1, l, tf), _to_bf16_col_blocks(ffn1_w3, l, tf),
                  _to_bf16(ffn1_w2, l), row(ln1_g[l]), row(ln1_b[l]), tm)
        o_a, o_b, o_c, o_d, wa, wb, wc, wd = _mixer(x2, B, S, l, p, lower_bounds, band, cband, farq, cos_t, sin_t)
        x2 = _outproj(x2, o_a, o_b, o_c, o_d, wa, wb, wc, wd, row(ln2_g[l]), row(ln2_b[l]), min(512, T))
        x2 = _ffn(x2, _to_bf16_col_blocks(ffn2_w1, l, tf), _to_bf16_col_blocks(ffn2_w3, l, tf),
                  _to_bf16(ffn2_w2, l), row(ln3_g[l]), row(ln3_b[l]), tm)
    return x2.reshape(B, S, D).astype(x.dtype)
```
